```python
import math
import jax, jax.numpy as jnp
from jax import lax
import numpy as np

D_MODEL = 2048
BATCH = 8
SEQ = 8192
DEPTH = 2

N_A = DEPTH // 2
N_B = DEPTH - N_A

POOL_WINDOWS = (2, 4, 8, 16)
N_POOL_GROUPS = len(POOL_WINDOWS)
POOL_GROUP = D_MODEL // N_POOL_GROUPS

N_HEADS = 16
QK_NOPE_DIM = 128
QK_ROPE_DIM = 64
QK_DIM = QK_NOPE_DIM + QK_ROPE_DIM
V_DIM = 128
Q_LORA_RANK = 512
KV_LORA_RANK = 512
ROPE_BASE = 10000.0
ATTN_SCALE = 1.0 / math.sqrt(QK_DIM)
Q_BLOCK = 128

FFN_MULT = 256
D_FF = ((8 * D_MODEL + 3 * FFN_MULT - 1) // (3 * FFN_MULT)) * FFN_MULT

NORM_EPS = 1e-6
POS_OFFSET_MAX = 4096

kernel_name = "yoco_pool_mla_hybrid"


def rms_norm(x, g):
    xf = x.astype(jnp.float32)
    y = xf * lax.rsqrt(jnp.mean(xf * xf, axis=-1, keepdims=True) + NORM_EPS)
    return (y * g.astype(jnp.float32)).astype(x.dtype)


def swiglu(h, w_gate, w_up, w_down):
    return (jax.nn.silu(h @ w_gate) * (h @ w_up)) @ w_down


def rope_tables(positions):
    half = QK_ROPE_DIM // 2
    inv_freq = ROPE_BASE ** (-jnp.arange(half, dtype=jnp.float32) / half)
    ang = positions.astype(jnp.float32)[..., None] * inv_freq
    return jnp.cos(ang), jnp.sin(ang)


def apply_rope(x, cos, sin):
    half = QK_ROPE_DIM // 2
    xf = x.astype(jnp.float32)
    x1, x2 = xf[..., :half], xf[..., half:]
    out = jnp.concatenate([x1 * cos - x2 * sin, x2 * cos + x1 * sin], axis=-1)
    return out.astype(x.dtype)


def multiscale_pool_mixer(h, pool_w, pool_scale):
    S = h.shape[1]
    hf = h.astype(jnp.float32)
    cs = jnp.cumsum(hf, axis=1)
    t = jnp.arange(S)
    outs = []
    for g, w in enumerate(POOL_WINDOWS):
        sl = slice(g * POOL_GROUP, (g + 1) * POOL_GROUP)
        c = cs[..., sl]
        c_prev = jnp.pad(c, ((0, 0), (w, 0), (0, 0)))[:, :S]
        count = jnp.minimum(t + 1, w).astype(jnp.float32)[None, :, None]
        diff = (c - c_prev) / count - hf[..., sl]
        outs.append(jnp.einsum('bsc,cd->bsd', diff.astype(h.dtype), pool_w[g]))
    return jnp.concatenate(outs, axis=-1) * pool_scale


def shared_mla_kv(h, w_kv_a, kv_latent_norm, w_kv_b, cos, sin):
    B, S, _ = h.shape
    kv = h @ w_kv_a
    c_kv = rms_norm(kv[..., :KV_LORA_RANK], kv_latent_norm)
    k_pe = apply_rope(kv[..., KV_LORA_RANK:], cos, sin)
    kv_up = (c_kv @ w_kv_b).reshape(B, S, N_HEADS, QK_NOPE_DIM + V_DIM)
    return kv_up[..., :QK_NOPE_DIM], k_pe, kv_up[..., QK_NOPE_DIM:]


def causal_mla_attention(q_nope, q_pe, k_nope, k_pe, v):
    S = q_nope.shape[1]
    outs = []
    for i in range(S // Q_BLOCK):
        s0, e = i * Q_BLOCK, (i + 1) * Q_BLOCK
        sc = (jnp.einsum('bqhd,bkhd->bhqk', q_nope[:, s0:e], k_nope[:, :e])
              + jnp.einsum('bqhr,bkr->bhqk', q_pe[:, s0:e], k_pe[:, :e]))
        sc = sc.astype(jnp.float32) * ATTN_SCALE
        mask = jnp.arange(e)[None, :] <= (s0 + jnp.arange(Q_BLOCK))[:, None]
        sc = jnp.where(mask, sc, -jnp.inf)
        p = jax.nn.softmax(sc, axis=-1).astype(v.dtype)
        outs.append(jnp.einsum('bhqk,bkhd->bqhd', p, v[:, :e]))
    return jnp.concatenate(outs, axis=1)


def mla_layer_mixer(h, w_q_a, q_latent_norm, w_q_b, w_o, k_nope, k_pe, v, cos, sin):
    B, S, _ = h.shape
    q = (rms_norm(h @ w_q_a, q_latent_norm) @ w_q_b).reshape(B, S, N_HEADS, QK_DIM)
    q_nope = q[..., :QK_NOPE_DIM]
    q_pe = apply_rope(q[..., QK_NOPE_DIM:], cos[:, :, None, :], sin[:, :, None, :])
    o = causal_mla_attention(q_nope, q_pe, k_nope, k_pe, v)
    return o.reshape(B, S, N_HEADS * V_DIM) @ w_o


def _fwd_setup_inputs(seed: int = 0) -> dict:
    key = jax.random.key(seed)
    ks = jax.random.split(key, 24)
    f32 = jnp.float32

    def dense(k, shape, fan_in):
        return jax.random.normal(k, shape, f32) * (fan_in ** -0.5)

    def gain(k, shape):
        return 1.0 + 0.05 * jax.random.normal(k, shape, f32)

    x = jax.random.normal(ks[0], (BATCH, SEQ, D_MODEL), f32)
    positions = (jax.random.randint(ks[1], (BATCH, 1), 0, POS_OFFSET_MAX, dtype=jnp.int32)
                 + jnp.arange(SEQ, dtype=jnp.int32)[None, :])
    return {
        "x": x,
        "positions": positions,
        "pool_norm": gain(ks[2], (N_A, D_MODEL)),
        "pool_w": dense(ks[3], (N_A, N_POOL_GROUPS, POOL_GROUP, POOL_GROUP), POOL_GROUP),
        "pool_scale": gain(ks[4], (N_A, D_MODEL)),
        "kv_in_norm": gain(ks[5], (D_MODEL,)),
        "w_kv_a": dense(ks[6], (D_MODEL, KV_LORA_RANK + QK_ROPE_DIM), D_MODEL),
        "kv_latent_norm": gain(ks[7], (KV_LORA_RANK,)),
        "w_kv_b": dense(ks[8], (KV_LORA_RANK, N_HEADS * (QK_NOPE_DIM + V_DIM)), KV_LORA_RANK),
        "attn_norm": gain(ks[9], (N_B, D_MODEL)),
        "w_q_a": dense(ks[10], (N_B, D_MODEL, Q_LORA_RANK), D_MODEL),
        "q_latent_norm": gain(ks[11], (N_B, Q_LORA_RANK)),
        "w_q_b": dense(ks[12], (N_B, Q_LORA_RANK, N_HEADS * QK_DIM), Q_LORA_RANK),
        "w_o": dense(ks[13], (N_B, N_HEADS * V_DIM, D_MODEL), N_HEADS * V_DIM),
        "ffn_norm": gain(ks[14], (DEPTH, D_MODEL)),
        "w_gate": dense(ks[15], (DEPTH, D_MODEL, D_FF), D_MODEL),
        "w_up": dense(ks[16], (DEPTH, D_MODEL, D_FF), D_MODEL),
        "w_down": dense(ks[17], (DEPTH, D_FF, D_MODEL), D_FF),
        "final_norm": gain(ks[18], (D_MODEL,)),
    }


def _fwd_reference(x, positions, pool_norm, pool_w, pool_scale, kv_in_norm, w_kv_a,
              kv_latent_norm, w_kv_b, attn_norm, w_q_a, q_latent_norm, w_q_b, w_o,
              ffn_norm, w_gate, w_up, w_down, final_norm):
    cos, sin = rope_tables(positions)
    k_nope = k_pe = v = None
    for l in range(DEPTH):
        if l == N_A:
            k_nope, k_pe, v = shared_mla_kv(rms_norm(x, kv_in_norm), w_kv_a,
                                            kv_latent_norm, w_kv_b, cos, sin)
        if l < N_A:
            x = x + multiscale_pool_mixer(rms_norm(x, pool_norm[l]), pool_w[l], pool_scale[l])
        else:
            b = l - N_A
            x = x + mla_layer_mixer(rms_norm(x, attn_norm[b]), w_q_a[b], q_latent_norm[b],
                                    w_q_b[b], w_o[b], k_nope, k_pe, v, cos, sin)
        x = x + swiglu(rms_norm(x, ffn_norm[l]), w_gate[l], w_up[l], w_down[l])
    return rms_norm(x, final_norm)


import jax as _jax
import jax.numpy as _jnp

TWIN_FORMAT = 'train_step'
FWD_PARAMS = ['x', 'positions', 'pool_norm', 'pool_w', 'pool_scale', 'kv_in_norm', 'w_kv_a', 'kv_latent_norm', 'w_kv_b', 'attn_norm', 'w_q_a', 'q_latent_norm', 'w_q_b', 'w_o', 'ffn_norm', 'w_gate', 'w_up', 'w_down', 'final_norm']
TWIN_WEIGHTS = ['pool_norm', 'pool_w', 'pool_scale', 'kv_in_norm', 'w_kv_a', 'kv_latent_norm', 'w_kv_b', 'attn_norm', 'w_q_a', 'q_latent_norm', 'w_q_b', 'w_o', 'ffn_norm', 'w_gate', 'w_up', 'w_down', 'final_norm']
TWIN_DIFF_INPUT = 'x'
TWIN_INPUTS = ['x', 'positions', 'pool_norm', 'pool_w', 'pool_scale', 'kv_in_norm', 'w_kv_a', 'kv_latent_norm', 'w_kv_b', 'attn_norm', 'w_q_a', 'q_latent_norm', 'w_q_b', 'w_o', 'ffn_norm', 'w_gate', 'w_up', 'w_down', 'final_norm', 'loss_target', 'm_pool_norm', 'm_pool_w', 'm_pool_scale', 'm_kv_in_norm', 'm_w_kv_a', 'm_kv_latent_norm', 'm_w_kv_b', 'm_attn_norm', 'm_w_q_a', 'm_q_latent_norm', 'm_w_q_b', 'm_w_o', 'm_ffn_norm', 'm_w_gate', 'm_w_up', 'm_w_down', 'm_final_norm', 'v_pool_norm', 'v_pool_w', 'v_pool_scale', 'v_kv_in_norm', 'v_w_kv_a', 'v_kv_latent_norm', 'v_w_kv_b', 'v_attn_norm', 'v_w_q_a', 'v_q_latent_norm', 'v_w_q_b', 'v_w_o', 'v_ffn_norm', 'v_w_gate', 'v_w_up', 'v_w_down', 'v_final_norm']
TWIN_OUTPUTS = ['loss', 'grad_x', 'grad_pool_norm', 'grad_pool_w', 'grad_pool_scale', 'grad_kv_in_norm', 'grad_w_kv_a', 'grad_kv_latent_norm', 'grad_w_kv_b', 'grad_attn_norm', 'grad_w_q_a', 'grad_q_latent_norm', 'grad_w_q_b', 'grad_w_o', 'grad_ffn_norm', 'grad_w_gate', 'grad_w_up', 'grad_w_down', 'grad_final_norm', 'delta_pool_norm', 'delta_pool_w', 'delta_pool_scale', 'delta_kv_in_norm', 'delta_w_kv_a', 'delta_kv_latent_norm', 'delta_w_kv_b', 'delta_attn_norm', 'delta_w_q_a', 'delta_q_latent_norm', 'delta_w_q_b', 'delta_w_o', 'delta_ffn_norm', 'delta_w_gate', 'delta_w_up', 'delta_w_down', 'delta_final_norm', 'new_m_pool_norm', 'new_m_pool_w', 'new_m_pool_scale', 'new_m_kv_in_norm', 'new_m_w_kv_a', 'new_m_kv_latent_norm', 'new_m_w_kv_b', 'new_m_attn_norm', 'new_m_w_q_a', 'new_m_q_latent_norm', 'new_m_w_q_b', 'new_m_w_o', 'new_m_ffn_norm', 'new_m_w_gate', 'new_m_w_up', 'new_m_w_down', 'new_m_final_norm', 'new_v_pool_norm', 'new_v_pool_w', 'new_v_pool_scale', 'new_v_kv_in_norm', 'new_v_w_kv_a', 'new_v_kv_latent_norm', 'new_v_w_kv_b', 'new_v_attn_norm', 'new_v_w_q_a', 'new_v_q_latent_norm', 'new_v_w_q_b', 'new_v_w_o', 'new_v_ffn_norm', 'new_v_w_gate', 'new_v_w_up', 'new_v_w_down', 'new_v_final_norm']
TWIN_LEAF_KINDS = {'loss': 'loss', 'grad_x': 'grad_x', 'grad_pool_norm': 'grad_w', 'grad_pool_w': 'grad_w', 'grad_pool_scale': 'grad_w', 'grad_kv_in_norm': 'grad_w', 'grad_w_kv_a': 'grad_w', 'grad_kv_latent_norm': 'grad_w', 'grad_w_kv_b': 'grad_w', 'grad_attn_norm': 'grad_w', 'grad_w_q_a': 'grad_w', 'grad_q_latent_norm': 'grad_w', 'grad_w_q_b': 'grad_w', 'grad_w_o': 'grad_w', 'grad_ffn_norm': 'grad_w', 'grad_w_gate': 'grad_w', 'grad_w_up': 'grad_w', 'grad_w_down': 'grad_w', 'grad_final_norm': 'grad_w', 'delta_pool_norm': 'delta_w', 'delta_pool_w': 'delta_w', 'delta_pool_scale': 'delta_w', 'delta_kv_in_norm': 'delta_w', 'delta_w_kv_a': 'delta_w', 'delta_kv_latent_norm': 'delta_w', 'delta_w_kv_b': 'delta_w', 'delta_attn_norm': 'delta_w', 'delta_w_q_a': 'delta_w', 'delta_q_latent_norm': 'delta_w', 'delta_w_q_b': 'delta_w', 'delta_w_o': 'delta_w', 'delta_ffn_norm': 'delta_w', 'delta_w_gate': 'delta_w', 'delta_w_up': 'delta_w', 'delta_w_down': 'delta_w', 'delta_final_norm': 'delta_w', 'new_m_pool_norm': 'new_m', 'new_m_pool_w': 'new_m', 'new_m_pool_scale': 'new_m', 'new_m_kv_in_norm': 'new_m', 'new_m_w_kv_a': 'new_m', 'new_m_kv_latent_norm': 'new_m', 'new_m_w_kv_b': 'new_m', 'new_m_attn_norm': 'new_m', 'new_m_w_q_a': 'new_m', 'new_m_q_latent_norm': 'new_m', 'new_m_w_q_b': 'new_m', 'new_m_w_o': 'new_m', 'new_m_ffn_norm': 'new_m', 'new_m_w_gate': 'new_m', 'new_m_w_up': 'new_m', 'new_m_w_down': 'new_m', 'new_m_final_norm': 'new_m', 'new_v_pool_norm': 'new_v', 'new_v_pool_w': 'new_v', 'new_v_pool_scale': 'new_v', 'new_v_kv_in_norm': 'new_v', 'new_v_w_kv_a': 'new_v', 'new_v_kv_latent_norm': 'new_v', 'new_v_w_kv_b': 'new_v', 'new_v_attn_norm': 'new_v', 'new_v_w_q_a': 'new_v', 'new_v_q_latent_norm': 'new_v', 'new_v_w_q_b': 'new_v', 'new_v_w_o': 'new_v', 'new_v_ffn_norm': 'new_v', 'new_v_w_gate': 'new_v', 'new_v_w_up': 'new_v', 'new_v_w_down': 'new_v', 'new_v_final_norm': 'new_v'}


def _forward(args):
    return _fwd_reference(*[args[k] for k in FWD_PARAMS])


def _output_shape():
    def fwd():
        inp = _fwd_setup_inputs(0)
        return _fwd_reference(*[inp[k] for k in FWD_PARAMS])
    out = _jax.eval_shape(fwd)
    return out.shape, out.dtype

N_MICROBATCH = 1
ADAM_LR = 0.001
ADAM_B1 = 0.9
ADAM_B2 = 0.999
ADAM_EPS = 1e-08
ADAM_WD = 0.01
ADAM_STEP = 10
PER_EXAMPLE_BATCH_AXIS = {'x': 0, 'positions': 0, 'loss_target': 0}
SHARED_INPUTS = []
_WEIGHT_DTYPES = {'pool_norm': _jnp.float32, 'pool_w': _jnp.float32, 'pool_scale': _jnp.float32, 'kv_in_norm': _jnp.float32, 'w_kv_a': _jnp.float32, 'kv_latent_norm': _jnp.float32, 'w_kv_b': _jnp.float32, 'attn_norm': _jnp.float32, 'w_q_a': _jnp.float32, 'q_latent_norm': _jnp.float32, 'w_q_b': _jnp.float32, 'w_o': _jnp.float32, 'ffn_norm': _jnp.float32, 'w_gate': _jnp.float32, 'w_up': _jnp.float32, 'w_down': _jnp.float32, 'final_norm': _jnp.float32}
MOMENT_SCALE = {'pool_norm': 1.011431e-01, 'pool_w': 1.025383e-01, 'pool_scale': 9.281912e-01, 'kv_in_norm': 2.493184e-02, 'w_kv_a': 4.616163e-02, 'kv_latent_norm': 4.827492e-02, 'w_kv_b': 1.635217e-02, 'attn_norm': 1.607437e-02, 'w_q_a': 3.250445e-02, 'q_latent_norm': 3.294623e-02, 'w_q_b': 1.329485e-02, 'w_o': 1.874372e-02, 'ffn_norm': 7.451522e-02, 'w_gate': 3.241078e-02, 'w_up': 3.137532e-02, 'w_down': 5.210261e-02, 'final_norm': 3.215385e+01}


def _to_microbatches(a, axis):
    t = _jnp.moveaxis(a, axis, 0)
    t = t.reshape((N_MICROBATCH, t.shape[0] // N_MICROBATCH) + t.shape[1:])
    return _jnp.moveaxis(t, 1, axis + 1)


def setup_inputs(seed: int = 0) -> dict:
    inp = _fwd_setup_inputs(seed)
    key = _jax.random.fold_in(_jax.random.key(seed), 7919)
    shape, _ = _output_shape()
    out = dict(inp)
    out["loss_target"] = _jax.random.normal(_jax.random.fold_in(key, 0), shape, _jnp.float32)
    for i, name in enumerate(TWIN_WEIGHTS):
        w = inp[name].astype(_jnp.float32)
        if MOMENT_SCALE is None:
            s = _jnp.sqrt(_jnp.mean(_jnp.square(w)) + 1e-30)
        else:
            s = MOMENT_SCALE[name]
        km, kv = _jax.random.split(_jax.random.fold_in(key, i + 1))
        out[name] = w
        out["m_" + name] = s * _jax.random.normal(km, w.shape, _jnp.float32)
        out["v_" + name] = (s * s) * _jax.random.uniform(kv, w.shape, _jnp.float32, 0.5, 1.5)
    if N_MICROBATCH > 1:
        for name, axis in PER_EXAMPLE_BATCH_AXIS.items():
            out[name] = _to_microbatches(out[name], axis)
    return {'x': out['x'], 'positions': out['positions'], 'pool_norm': out['pool_norm'], 'pool_w': out['pool_w'], 'pool_scale': out['pool_scale'], 'kv_in_norm': out['kv_in_norm'], 'w_kv_a': out['w_kv_a'], 'kv_latent_norm': out['kv_latent_norm'], 'w_kv_b': out['w_kv_b'], 'attn_norm': out['attn_norm'], 'w_q_a': out['w_q_a'], 'q_latent_norm': out['q_latent_norm'], 'w_q_b': out['w_q_b'], 'w_o': out['w_o'], 'ffn_norm': out['ffn_norm'], 'w_gate': out['w_gate'], 'w_up': out['w_up'], 'w_down': out['w_down'], 'final_norm': out['final_norm'], 'loss_target': out['loss_target'], 'm_pool_norm': out['m_pool_norm'], 'm_pool_w': out['m_pool_w'], 'm_pool_scale': out['m_pool_scale'], 'm_kv_in_norm': out['m_kv_in_norm'], 'm_w_kv_a': out['m_w_kv_a'], 'm_kv_latent_norm': out['m_kv_latent_norm'], 'm_w_kv_b': out['m_w_kv_b'], 'm_attn_norm': out['m_attn_norm'], 'm_w_q_a': out['m_w_q_a'], 'm_q_latent_norm': out['m_q_latent_norm'], 'm_w_q_b': out['m_w_q_b'], 'm_w_o': out['m_w_o'], 'm_ffn_norm': out['m_ffn_norm'], 'm_w_gate': out['m_w_gate'], 'm_w_up': out['m_w_up'], 'm_w_down': out['m_w_down'], 'm_final_norm': out['m_final_norm'], 'v_pool_norm': out['v_pool_norm'], 'v_pool_w': out['v_pool_w'], 'v_pool_scale': out['v_pool_scale'], 'v_kv_in_norm': out['v_kv_in_norm'], 'v_w_kv_a': out['v_w_kv_a'], 'v_kv_latent_norm': out['v_kv_latent_norm'], 'v_w_kv_b': out['v_w_kv_b'], 'v_attn_norm': out['v_attn_norm'], 'v_w_q_a': out['v_w_q_a'], 'v_q_latent_norm': out['v_q_latent_norm'], 'v_w_q_b': out['v_w_q_b'], 'v_w_o': out['v_w_o'], 'v_ffn_norm': out['v_ffn_norm'], 'v_w_gate': out['v_w_gate'], 'v_w_up': out['v_w_up'], 'v_w_down': out['v_w_down'], 'v_final_norm': out['v_final_norm']}


def _loss(weights, diff, rest, loss_target):
    with _jax.named_scope("forward"):
        args = {**rest, TWIN_DIFF_INPUT: diff, **{k: w.astype(_WEIGHT_DTYPES[k]) for k, w in weights.items()}}
        y = _forward(args)
    with _jax.named_scope("loss_head"):
        err = _jnp.square(y.astype(_jnp.float32) - loss_target)
        return 0.5 * _jnp.sum(_jnp.mean(err, axis=-1)) if err.ndim else 0.5 * err


def _adamw(w, g, m, v):
    m = ADAM_B1 * m + (1.0 - ADAM_B1) * g
    v = ADAM_B2 * v + (1.0 - ADAM_B2) * _jnp.square(g)
    m_hat = m / (1.0 - ADAM_B1 ** ADAM_STEP)
    v_hat = v / (1.0 - ADAM_B2 ** ADAM_STEP)
    delta = -ADAM_LR * (m_hat / (_jnp.sqrt(v_hat) + ADAM_EPS) + ADAM_WD * w)
    return delta, m, v


def reference(x, positions, pool_norm, pool_w, pool_scale, kv_in_norm, w_kv_a, kv_latent_norm, w_kv_b, attn_norm, w_q_a, q_latent_norm, w_q_b, w_o, ffn_norm, w_gate, w_up, w_down, final_norm, loss_target, m_pool_norm, m_pool_w, m_pool_scale, m_kv_in_norm, m_w_kv_a, m_kv_latent_norm, m_w_kv_b, m_attn_norm, m_w_q_a, m_q_latent_norm, m_w_q_b, m_w_o, m_ffn_norm, m_w_gate, m_w_up, m_w_down, m_final_norm, v_pool_norm, v_pool_w, v_pool_scale, v_kv_in_norm, v_w_kv_a, v_kv_latent_norm, v_w_kv_b, v_attn_norm, v_w_q_a, v_q_latent_norm, v_w_q_b, v_w_o, v_ffn_norm, v_w_gate, v_w_up, v_w_down, v_final_norm):
    given = dict(x=x, positions=positions, pool_norm=pool_norm, pool_w=pool_w, pool_scale=pool_scale, kv_in_norm=kv_in_norm, w_kv_a=w_kv_a, kv_latent_norm=kv_latent_norm, w_kv_b=w_kv_b, attn_norm=attn_norm, w_q_a=w_q_a, q_latent_norm=q_latent_norm, w_q_b=w_q_b, w_o=w_o, ffn_norm=ffn_norm, w_gate=w_gate, w_up=w_up, w_down=w_down, final_norm=final_norm, loss_target=loss_target, m_pool_norm=m_pool_norm, m_pool_w=m_pool_w, m_pool_scale=m_pool_scale, m_kv_in_norm=m_kv_in_norm, m_w_kv_a=m_w_kv_a, m_kv_latent_norm=m_kv_latent_norm, m_w_kv_b=m_w_kv_b, m_attn_norm=m_attn_norm, m_w_q_a=m_w_q_a, m_q_latent_norm=m_q_latent_norm, m_w_q_b=m_w_q_b, m_w_o=m_w_o, m_ffn_norm=m_ffn_norm, m_w_gate=m_w_gate, m_w_up=m_w_up, m_w_down=m_w_down, m_final_norm=m_final_norm, v_pool_norm=v_pool_norm, v_pool_w=v_pool_w, v_pool_scale=v_pool_scale, v_kv_in_norm=v_kv_in_norm, v_w_kv_a=v_w_kv_a, v_kv_latent_norm=v_kv_latent_norm, v_w_kv_b=v_w_kv_b, v_attn_norm=v_attn_norm, v_w_q_a=v_w_q_a, v_q_latent_norm=v_q_latent_norm, v_w_q_b=v_w_q_b, v_w_o=v_w_o, v_ffn_norm=v_ffn_norm, v_w_gate=v_w_gate, v_w_up=v_w_up, v_w_down=v_w_down, v_final_norm=v_final_norm)
    weights = {n: given[n] for n in TWIN_WEIGHTS}
    shared = {n: given[n] for n in SHARED_INPUTS}
    per_example = {n: given[n] for n in ['x', 'positions']}
    grad_fn = _jax.value_and_grad(_loss, argnums=(0, 1))

    def one_microbatch(ex, loss_target):
        ex = dict(ex)
        diff = ex.pop(TWIN_DIFF_INPUT)
        return grad_fn(weights, diff, {**shared, **ex}, loss_target)

    if N_MICROBATCH == 1:
        loss, (grad_w, grad_x) = one_microbatch(per_example, given["loss_target"])
    else:
        def body(carry, xs):
            loss_sum, grad_sum = carry
            l_k, (gw_k, gx_k) = one_microbatch(xs[0], xs[1])
            with _jax.named_scope("update"):
                return (loss_sum + l_k, _jax.tree.map(_jnp.add, grad_sum, gw_k)), gx_k

        init = (_jnp.zeros((), _jnp.float32), _jax.tree.map(_jnp.zeros_like, weights))
        (loss, grad_w), grad_x = _jax.lax.scan(body, init, (per_example, given["loss_target"]))
    with _jax.named_scope("update"):
        delta_w, new_m, new_v = {}, {}, {}
        for n in TWIN_WEIGHTS:
            delta_w[n], new_m[n], new_v[n] = _adamw(weights[n], grad_w[n], given["m_" + n], given["v_" + n])
    return (loss, grad_x, *[grad_w[n] for n in TWIN_WEIGHTS], *[delta_w[n] for n in TWIN_WEIGHTS],
            *[new_m[n] for n in TWIN_WEIGHTS], *[new_v[n] for n in TWIN_WEIGHTS])
```

```python
import functools
import math

import jax
import jax.numpy as jnp
from jax import lax
from jax.experimental import pallas as pl
from jax.experimental.pallas import tpu as pltpu

F32 = jnp.float32
BF16 = jnp.bfloat16

N_DEV = 8
POOL_WINDOWS = (2, 4, 8, 16)
POOL_HALO = 16
QK_NOPE_DIM = 128
QK_ROPE_DIM = 64
HEAD_PAD = 256
V_DIM = 128
ROPE_BASE = 10000.0
ATTN_SCALE = 1.0 / math.sqrt(QK_NOPE_DIM + QK_ROPE_DIM)
NORM_EPS = 1e-6
ADAM_LR = 0.001
ADAM_B1 = 0.9
ADAM_B2 = 0.999
ADAM_EPS = 1e-08
ADAM_WD = 0.01
ADAM_STEP = 10

ROW_TILE = 512
ATT_BLOCK = 512
MM_TM, MM_TN, MM_TK = 512, 512, 512
PACK_LANES = 1024
VMEM_LIMIT = 48 * 1024 * 1024

_pcall = pl.pallas_call


def _params(sem):
    return pltpu.CompilerParams(dimension_semantics=sem, vmem_limit_bytes=VMEM_LIMIT)


def _tile(dim, default):
    if dim % default == 0:
        return default
    assert dim <= 2 * default, (dim, default)
    return dim


def _row_tile(rows, width):
    ts = _tile(rows, ROW_TILE)
    while ts * width * 4 > (2 << 20) and ts % 16 == 0:
        ts //= 2
    return ts


def _rms_scale(x):
    return lax.rsqrt(jnp.mean(x * x, axis=-1, keepdims=True) + NORM_EPS)


def _rms_bwd(x, g, dy):
    r = _rms_scale(x)
    xn = x * r
    u = dy * g
    dx = r * (u - xn * jnp.mean(u * xn, axis=-1, keepdims=True))
    return dx, dy * xn


def _swap_halves(x):
    lane = lax.broadcasted_iota(jnp.int32, x.shape, 1)
    return jnp.where(lane < QK_ROPE_DIM // 2, pltpu.roll(x, 128 - QK_ROPE_DIM // 2, 1), pltpu.roll(x, QK_ROPE_DIM // 2, 1))


def _rope(x, cos_t, sin_t):
    return x * cos_t + _swap_halves(x) * sin_t


def _rmsnorm(x, g, name):
    S, D = x.shape
    ts = _row_tile(S, D)

    def body(x_ref, g_ref, o_ref):
        xf = x_ref[...]
        o_ref[...] = (xf * _rms_scale(xf) * g_ref[...]).astype(o_ref.dtype)

    return _pcall(
        body, name=name, grid=(S // ts,),
        in_specs=[pl.BlockSpec((ts, D), lambda i: (i, 0)), pl.BlockSpec((1, D), lambda i: (0, 0))],
        out_specs=pl.BlockSpec((ts, D), lambda i: (i, 0)),
        out_shape=jax.ShapeDtypeStruct((S, D), BF16),
        compiler_params=_params(("parallel",)),
    )(x, g)


def _rmsnorm_bwd(x, g, dy, res, name, out_dtype=F32):
    S, D = x.shape
    ts = _row_tile(S, D)
    has_res = res is not None

    def body(*refs):
        if has_res:
            x_ref, g_ref, dy_ref, res_ref, dx_ref, dg_ref = refs
        else:
            x_ref, g_ref, dy_ref, dx_ref, dg_ref = refs
        dx, dgt = _rms_bwd(x_ref[...], g_ref[...], dy_ref[...].astype(F32))
        if has_res:
            dx = res_ref[...] + dx
        dx_ref[...] = dx.astype(dx_ref.dtype)

        @pl.when(pl.program_id(0) == 0)
        def _():
            dg_ref[...] = jnp.zeros_like(dg_ref)

        dg_ref[...] += jnp.sum(dgt, axis=0, keepdims=True)

    row = pl.BlockSpec((ts, D), lambda i: (i, 0))
    vec = pl.BlockSpec((1, D), lambda i: (0, 0))
    return _pcall(
        body, name=name, grid=(S // ts,),
        in_specs=[row, vec, row] + ([row] if has_res else []),
        out_specs=(row, vec),
        out_shape=(jax.ShapeDtypeStruct((S, D), out_dtype), jax.ShapeDtypeStruct((1, D), F32)),
        compiler_params=_params(("arbitrary",)),
    )(*([x, g, dy] + ([res] if has_res else [])))


def _pool_pre(x, g):
    S, D = x.shape
    ts = _row_tile(S, D)
    gw = D // len(POOL_WINDOWS)
    hb = ts // POOL_HALO

    def body(x_ref, halo_ref, g_ref, o_ref):
        i = pl.program_id(0)
        xx = jnp.concatenate([halo_ref[...], x_ref[...]], axis=0)
        h = xx * _rms_scale(xx) * g_ref[...]
        t = i * ts - POOL_HALO + lax.broadcasted_iota(jnp.int32, (ts + POOL_HALO, 1), 0)
        h = jnp.where(t >= 0, h, 0.0)
        tf = t[POOL_HALO:].astype(F32)
        for gi, w in enumerate(POOL_WINDOWS):
            hg = h[:, gi * gw:(gi + 1) * gw]
            acc, span = hg, 1
            while span < w:
                acc = acc + pltpu.roll(acc, span, 0)
                span *= 2
            count = jnp.minimum(tf + 1.0, float(w))
            diff = acc[POOL_HALO:] / count - hg[POOL_HALO:]
            o_ref[:, gi * gw:(gi + 1) * gw] = diff.astype(o_ref.dtype)

    return _pcall(
        body, name="pool_pre", grid=(S // ts,),
        in_specs=[pl.BlockSpec((ts, D), lambda i: (i, 0)),
                  pl.BlockSpec((POOL_HALO, D), lambda i: (jnp.maximum(i * hb - 1, 0), 0)),
                  pl.BlockSpec((1, D), lambda i: (0, 0))],
        out_specs=pl.BlockSpec((ts, D), lambda i: (i, 0)),
        out_shape=jax.ShapeDtypeStruct((S, D), BF16),
        compiler_params=_params(("parallel",)),
    )(x, x, g)


def _pool_pre_bwd(x, g, dd, res):
    S, D = x.shape
    ts = _row_tile(S, D)
    gw = D // len(POOL_WINDOWS)
    hb = ts // POOL_HALO
    last_halo = S // POOL_HALO - 1

    def body(x_ref, g_ref, dd_ref, halo_ref, res_ref, dx_ref, dg_ref):
        i = pl.program_id(0)
        ee = jnp.concatenate([dd_ref[...], halo_ref[...]], axis=0)
        t = i * ts + lax.broadcasted_iota(jnp.int32, (ts + POOL_HALO, 1), 0)
        ee = jnp.where(t < S, ee, 0.0)
        tf = t.astype(F32)
        n = ts + POOL_HALO
        for gi, w in enumerate(POOL_WINDOWS):
            eg = ee[:, gi * gw:(gi + 1) * gw]
            acc, span = eg / jnp.minimum(tf + 1.0, float(w)), 1
            while span < w:
                acc = acc + pltpu.roll(acc, n - span, 0)
                span *= 2
            dh = acc[:ts] - eg[:ts]
            dx_ref[:, gi * gw:(gi + 1) * gw] = dh
        dx, dgt = _rms_bwd(x_ref[...], g_ref[...], dx_ref[...])
        dx_ref[...] = res_ref[...] + dx

        @pl.when(i == 0)
        def _():
            dg_ref[...] = jnp.zeros_like(dg_ref)

        dg_ref[...] += jnp.sum(dgt, axis=0, keepdims=True)

    row = pl.BlockSpec((ts, D), lambda i: (i, 0))
    vec = pl.BlockSpec((1, D), lambda i: (0, 0))
    return _pcall(
        body, name="pool_pre_bwd", grid=(S // ts,),
        in_specs=[row, vec, row,
                  pl.BlockSpec((POOL_HALO, D), lambda i: (jnp.minimum((i + 1) * hb, last_halo), 0)), row],
        out_specs=(row, vec),
        out_shape=(jax.ShapeDtypeStruct((S, D), F32), jax.ShapeDtypeStruct((1, D), F32)),
        compiler_params=_params(("arbitrary",)),
    )(x, g, dd, dd, res)


def _pool_mix(diff, w, scale, x):
    S, D = x.shape
    G, gw, _ = w.shape
    ts = _tile(S, ROW_TILE)

    def body(d_ref, w_ref, s_ref, x_ref, o_ref):
        mo = jnp.dot(d_ref[...], w_ref[0], preferred_element_type=F32)
        o_ref[...] = x_ref[...] + mo * s_ref[...]

    blk = pl.BlockSpec((ts, gw), lambda i, g: (i, g))
    return _pcall(
        body, name="pool_mix", grid=(S // ts, G),
        in_specs=[blk, pl.BlockSpec((1, gw, gw), lambda i, g: (g, 0, 0)), pl.BlockSpec((1, gw), lambda i, g: (0, g)), blk],
        out_specs=blk,
        out_shape=jax.ShapeDtypeStruct((S, D), F32),
        compiler_params=_params(("parallel", "parallel")),
    )(diff, w, scale, x)


def _pool_mix_bwd(diff, w, scale, dy):
    S, D = dy.shape
    G, gw, _ = w.shape
    ts = _tile(S, ROW_TILE)

    def body(d_ref, w_ref, s_ref, dy_ref, dd_ref, dw_ref, ds_ref):
        i = pl.program_id(1)
        d = d_ref[...]
        dy = dy_ref[...]
        mo = jnp.dot(d, w_ref[0], preferred_element_type=F32)
        dmo = (dy * s_ref[...]).astype(BF16)
        dd_ref[...] = lax.dot_general(dmo, w_ref[0], (((1,), (1,)), ((), ())), preferred_element_type=F32)

        @pl.when(i == 0)
        def _():
            dw_ref[...] = jnp.zeros_like(dw_ref)
            ds_ref[...] = jnp.zeros_like(ds_ref)

        dw_ref[0] += lax.dot_general(d, dmo, (((0,), (0,)), ((), ())), preferred_element_type=F32)
        ds_ref[...] += jnp.sum(dy * mo, axis=0, keepdims=True)

    blk = pl.BlockSpec((ts, gw), lambda g, i: (i, g))
    wspec = pl.BlockSpec((1, gw, gw), lambda g, i: (g, 0, 0))
    vec = pl.BlockSpec((1, gw), lambda g, i: (0, g))
    return _pcall(
        body, name="pool_mix_bwd", grid=(G, S // ts),
        in_specs=[blk, wspec, vec, blk],
        out_specs=(blk, wspec, vec),
        out_shape=(jax.ShapeDtypeStruct((S, D), F32), jax.ShapeDtypeStruct((G, gw, gw), F32),
                   jax.ShapeDtypeStruct((1, D), F32)),
        compiler_params=_params(("parallel", "arbitrary")),
    )(diff, w, scale, dy)


def _loss_head(x, g, target):
    S, D = x.shape
    ts = _row_tile(S, D)

    def body(x_ref, g_ref, t_ref, loss_ref, dx_ref, dg_ref):
        xf = x_ref[...]
        gv = g_ref[...]
        err = xf * _rms_scale(xf) * gv - t_ref[...]
        dx, dgt = _rms_bwd(xf, gv, err * (1.0 / D))
        dx_ref[...] = dx

        @pl.when(pl.program_id(0) == 0)
        def _():
            loss_ref[...] = jnp.zeros_like(loss_ref)
            dg_ref[...] = jnp.zeros_like(dg_ref)

        part = 0.5 * jnp.sum(jnp.sum(err * err, axis=-1, keepdims=True) * (1.0 / D), axis=0, keepdims=True)
        loss_ref[...] += jnp.broadcast_to(part, loss_ref.shape)
        dg_ref[...] += jnp.sum(dgt, axis=0, keepdims=True)

    row = pl.BlockSpec((ts, D), lambda i: (i, 0))
    vec = pl.BlockSpec((1, D), lambda i: (0, 0))
    return _pcall(
        body, name="loss_head", grid=(S // ts,),
        in_specs=[row, vec, row],
        out_specs=(pl.BlockSpec((1, 128), lambda i: (0, 0)), row, vec),
        out_shape=(jax.ShapeDtypeStruct((1, 128), F32), jax.ShapeDtypeStruct((S, D), F32),
                   jax.ShapeDtypeStruct((1, D), F32)),
        compiler_params=_params(("arbitrary",)),
    )(x, g, target)


_DIMS = {"nn": (((1,), (0,)), ((), ())), "nt": (((1,), (1,)), ((), ())), "tn": (((0,), (0,)), ((), ()))}


def _mm(a, b, mode, name, out_dtype=F32, res=None, a2=None, b2=None):
    if mode == "tn":
        (K, M), (_, N) = a.shape, b.shape
    elif mode == "nt":
        (M, K), (N, _) = a.shape, b.shape
    else:
        (M, K), (_, N) = a.shape, b.shape
    tm, tn, tk = _tile(M, MM_TM), _tile(N, MM_TN), _tile(K, MM_TK)
    nk = K // tk
    dual, has_res = a2 is not None, res is not None
    dn = _DIMS[mode]

    def body(*refs):
        refs = list(refs)
        a_ref, b_ref = refs[:2]
        pos = 2
        if dual:
            a2_ref, b2_ref = refs[2:4]
            pos = 4
        if has_res:
            res_ref = refs[pos]
            pos += 1
        o_ref, acc_ref = refs[pos], refs[pos + 1]
        k = pl.program_id(2)

        @pl.when(k == 0)
        def _():
            acc_ref[...] = jnp.zeros_like(acc_ref)

        part = lax.dot_general(a_ref[...].astype(BF16), b_ref[...].astype(BF16), dn, preferred_element_type=F32)
        if dual:
            part += lax.dot_general(a2_ref[...].astype(BF16), b2_ref[...].astype(BF16), dn, preferred_element_type=F32)
        acc_ref[...] += part

        @pl.when(k == nk - 1)
        def _():
            out = acc_ref[...]
            if has_res:
                out = res_ref[...] + out
            o_ref[...] = out.astype(o_ref.dtype)

    a_spec = (pl.BlockSpec((tk, tm), lambda i, j, k: (k, i)) if mode == "tn"
              else pl.BlockSpec((tm, tk), lambda i, j, k: (i, k)))
    b_spec = (pl.BlockSpec((tn, tk), lambda i, j, k: (j, k)) if mode == "nt"
              else pl.BlockSpec((tk, tn), lambda i, j, k: (k, j)))
    o_spec = pl.BlockSpec((tm, tn), lambda i, j, k: (i, j))
    operands, in_specs = [a, b], [a_spec, b_spec]
    if dual:
        operands += [a2, b2]
        in_specs += [a_spec, b_spec]
    if has_res:
        operands.append(res)
        in_specs.append(o_spec)
    return _pcall(
        body, name=name, grid=(M // tm, N // tn, nk),
        in_specs=in_specs, out_specs=o_spec,
        out_shape=jax.ShapeDtypeStruct((M, N), out_dtype),
        scratch_shapes=[pltpu.VMEM((tm, tn), F32)],
        compiler_params=_params(("parallel", "parallel", "arbitrary")),
    )(*operands)


def _sigmoid(x):
    return 1.0 / (1.0 + jnp.exp(-x))


def _ffn_up(h, wg, wu, name):
    M, K = h.shape
    N = wg.shape[1]
    tm, tn, tk = _tile(M, MM_TM), _tile(N, MM_TN), _tile(K, MM_TK)
    nk = K // tk

    def body(h_ref, wg_ref, wu_ref, g_ref, u_ref, a_ref, accg_ref, accu_ref):
        k = pl.program_id(2)

        @pl.when(k == 0)
        def _():
            accg_ref[...] = jnp.zeros_like(accg_ref)
            accu_ref[...] = jnp.zeros_like(accu_ref)

        hv = h_ref[...]
        accg_ref[...] += jnp.dot(hv, wg_ref[...], preferred_element_type=F32)
        accu_ref[...] += jnp.dot(hv, wu_ref[...], preferred_element_type=F32)

        @pl.when(k == nk - 1)
        def _():
            g, u = accg_ref[...], accu_ref[...]
            g_ref[...] = g
            u_ref[...] = u
            a_ref[...] = (g * _sigmoid(g) * u).astype(a_ref.dtype)

    w_spec = pl.BlockSpec((tk, tn), lambda i, j, k: (k, j))
    o_spec = pl.BlockSpec((tm, tn), lambda i, j, k: (i, j))
    return _pcall(
        body, name=name, grid=(M // tm, N // tn, nk),
        in_specs=[pl.BlockSpec((tm, tk), lambda i, j, k: (i, k)), w_spec, w_spec],
        out_specs=(o_spec, o_spec, o_spec),
        out_shape=(jax.ShapeDtypeStruct((M, N), F32), jax.ShapeDtypeStruct((M, N), F32),
                   jax.ShapeDtypeStruct((M, N), BF16)),
        scratch_shapes=[pltpu.VMEM((tm, tn), F32), pltpu.VMEM((tm, tn), F32)],
        compiler_params=_params(("parallel", "parallel", "arbitrary")),
    )(h, wg, wu)


def _ffn_dact(dy, wd, g, u, name):
    M, K = dy.shape
    N = wd.shape[0]
    tm, tn, tk = _tile(M, MM_TM), _tile(N, MM_TN), _tile(K, MM_TK)
    nk = K // tk

    def body(dy_ref, wd_ref, g_ref, u_ref, dg_ref, du_ref, acc_ref):
        k = pl.program_id(2)

        @pl.when(k == 0)
        def _():
            acc_ref[...] = jnp.zeros_like(acc_ref)

        acc_ref[...] += lax.dot_general(dy_ref[...].astype(BF16), wd_ref[...], _DIMS["nt"],
                                        preferred_element_type=F32)

        @pl.when(k == nk - 1)
        def _():
            da, g, u = acc_ref[...], g_ref[...], u_ref[...]
            sig = _sigmoid(g)
            dg_ref[...] = (da * u * (sig * (1.0 + g * (1.0 - sig)))).astype(dg_ref.dtype)
            du_ref[...] = (da * (g * sig)).astype(du_ref.dtype)

    o_spec = pl.BlockSpec((tm, tn), lambda i, j, k: (i, j))
    return _pcall(
        body, name=name, grid=(M // tm, N // tn, nk),
        in_specs=[pl.BlockSpec((tm, tk), lambda i, j, k: (i, k)), pl.BlockSpec((tn, tk), lambda i, j, k: (j, k)),
                  o_spec, o_spec],
        out_specs=(o_spec, o_spec),
        out_shape=(jax.ShapeDtypeStruct((M, N), BF16), jax.ShapeDtypeStruct((M, N), BF16)),
        scratch_shapes=[pltpu.VMEM((tm, tn), F32)],
        compiler_params=_params(("parallel", "parallel", "arbitrary")),
    )(dy, wd, g, u)


def _q_rope(q, cos_t, sin_t):
    S, W = q.shape
    ts = _tile(S, ROW_TILE)

    def body(q_ref, c_ref, s_ref, o_ref):
        o_ref[:, :QK_NOPE_DIM] = q_ref[:, :QK_NOPE_DIM].astype(o_ref.dtype)
        o_ref[:, QK_NOPE_DIM:] = _rope(q_ref[:, QK_NOPE_DIM:], c_ref[...], s_ref[...]).astype(o_ref.dtype)

    blk = pl.BlockSpec((ts, HEAD_PAD), lambda i, h: (i, h))
    tab = pl.BlockSpec((ts, 128), lambda i, h: (i, 0))
    return _pcall(
        body, name="q_rope", grid=(S // ts, W // HEAD_PAD),
        in_specs=[blk, tab, tab], out_specs=blk,
        out_shape=jax.ShapeDtypeStruct((S, W), BF16),
        compiler_params=_params(("parallel", "parallel")),
    )(q, cos_t, sin_t)


def _kv_mid(kv, g, cos_t, sin_t):
    S, W = kv.shape
    R = W - 128
    ts = _tile(S, ROW_TILE)

    def body(kv_ref, g_ref, c_ref, s_ref, c_out, k_out):
        cf = kv_ref[:, :R]
        c_out[...] = (cf * _rms_scale(cf) * g_ref[...]).astype(c_out.dtype)
        k_out[...] = _rope(kv_ref[:, R:], c_ref[...], s_ref[...]).astype(k_out.dtype)

    tab = pl.BlockSpec((ts, 128), lambda i: (i, 0))
    return _pcall(
        body, name="kv_mid", grid=(S // ts,),
        in_specs=[pl.BlockSpec((ts, W), lambda i: (i, 0)), pl.BlockSpec((1, R), lambda i: (0, 0)), tab, tab],
        out_specs=(pl.BlockSpec((ts, R), lambda i: (i, 0)), tab),
        out_shape=(jax.ShapeDtypeStruct((S, R), BF16), jax.ShapeDtypeStruct((S, 128), BF16)),
        compiler_params=_params(("parallel",)),
    )(kv, g, cos_t, sin_t)


def _kv_mid_bwd(kv, g, dc, dkpe, cos_t, sin_t):
    S, W = kv.shape
    R = W - 128
    H = dkpe.shape[0]
    ts = _row_tile(S, H * 128)

    def body(kv_ref, g_ref, dc_ref, dk_ref, c_ref, s_ref, dkv_ref, dg_ref):
        dx, dgt = _rms_bwd(kv_ref[:, :R], g_ref[...], dc_ref[...])
        dkv_ref[:, :R] = dx.astype(dkv_ref.dtype)
        dk = dk_ref[0]
        for h in range(1, H):
            dk = dk + dk_ref[h]
        dkv_ref[:, R:] = _rope(dk, c_ref[...], -s_ref[...]).astype(dkv_ref.dtype)

        @pl.when(pl.program_id(0) == 0)
        def _():
            dg_ref[...] = jnp.zeros_like(dg_ref)

        dg_ref[...] += jnp.sum(dgt, axis=0, keepdims=True)

    tab = pl.BlockSpec((ts, 128), lambda i: (i, 0))
    vec = pl.BlockSpec((1, R), lambda i: (0, 0))
    return _pcall(
        body, name="kv_mid_bwd", grid=(S // ts,),
        in_specs=[pl.BlockSpec((ts, W), lambda i: (i, 0)), vec, pl.BlockSpec((ts, R), lambda i: (i, 0)),
                  pl.BlockSpec((H, ts, 128), lambda i: (0, i, 0)), tab, tab],
        out_specs=(pl.BlockSpec((ts, W), lambda i: (i, 0)), vec),
        out_shape=(jax.ShapeDtypeStruct((S, W), BF16), jax.ShapeDtypeStruct((1, R), F32)),
        compiler_params=_params(("arbitrary",)),
    )(kv, g, dc, dkpe, cos_t, sin_t)


def _causal_mask(qi, ki, blk, transposed=False):
    r = lax.broadcasted_iota(jnp.int32, (blk, blk), 0)
    c = lax.broadcasted_iota(jnp.int32, (blk, blk), 1)
    if transposed:
        return ki * blk + r <= qi * blk + c
    return ki * blk + c <= qi * blk + r


def _attn_fwd(q, kvu, kpe, H):
    S = q.shape[0]
    blk = _tile(S, ATT_BLOCK)
    nb = S // blk

    def body(q_ref, kn_ref, kp_ref, v_ref, o_ref, lse_ref, m_ref, l_ref, acc_ref):
        qi, ki = pl.program_id(1), pl.program_id(2)

        @pl.when(ki == 0)
        def _():
            m_ref[...] = jnp.full_like(m_ref, -jnp.inf)
            l_ref[...] = jnp.zeros_like(l_ref)
            acc_ref[...] = jnp.zeros_like(acc_ref)

        @pl.when(ki <= qi)
        def _():
            k = jnp.concatenate([kn_ref[...], kp_ref[...]], axis=1)
            s = lax.dot_general(q_ref[...], k, _DIMS["nt"], preferred_element_type=F32) * ATTN_SCALE
            s = jnp.where(_causal_mask(qi, ki, blk), s, -jnp.inf)
            m_old = m_ref[...]
            m_new = jnp.maximum(m_old, jnp.max(s, axis=-1, keepdims=True))
            alpha = jnp.exp(m_old - m_new)
            p = jnp.exp(s - m_new)
            l_ref[...] = alpha * l_ref[...] + jnp.sum(p, axis=-1, keepdims=True)
            acc_ref[...] = alpha * acc_ref[...] + jnp.dot(p.astype(BF16), v_ref[...], preferred_element_type=F32)
            m_ref[...] = m_new

        @pl.when(ki == qi)
        def _():
            o_ref[...] = (acc_ref[...] / l_ref[...]).astype(o_ref.dtype)
            lse_ref[0] = m_ref[...] + jnp.log(l_ref[...])

    kb = lambda qi, ki: jnp.minimum(ki, qi)
    return _pcall(
        body, name="attn_fwd", grid=(H, nb, nb),
        in_specs=[pl.BlockSpec((blk, HEAD_PAD), lambda h, qi, ki: (qi, h)),
                  pl.BlockSpec((blk, 128), lambda h, qi, ki: (kb(qi, ki), 2 * h)),
                  pl.BlockSpec((blk, 128), lambda h, qi, ki: (kb(qi, ki), 0)),
                  pl.BlockSpec((blk, 128), lambda h, qi, ki: (kb(qi, ki), 2 * h + 1))],
        out_specs=(pl.BlockSpec((blk, V_DIM), lambda h, qi, ki: (qi, h)),
                   pl.BlockSpec((1, blk, 1), lambda h, qi, ki: (h, qi, 0))),
        out_shape=(jax.ShapeDtypeStruct((S, H * V_DIM), BF16), jax.ShapeDtypeStruct((H, S, 1), F32)),
        scratch_shapes=[pltpu.VMEM((blk, 1), F32), pltpu.VMEM((blk, 1), F32), pltpu.VMEM((blk, V_DIM), F32)],
        compiler_params=_params(("parallel", "parallel", "arbitrary")),
    )(q, kvu, kpe, kvu)


def _attn_delta(do, o, H):
    S = do.shape[0]
    ts = _tile(S, ROW_TILE)

    def body(do_ref, o_ref, d_ref):
        d_ref[0] = jnp.sum(do_ref[...].astype(F32) * o_ref[...].astype(F32), axis=-1, keepdims=True)

    blk = pl.BlockSpec((ts, V_DIM), lambda i, h: (i, h))
    return _pcall(
        body, name="attn_delta", grid=(S // ts, H),
        in_specs=[blk, blk], out_specs=pl.BlockSpec((1, ts, 1), lambda i, h: (h, i, 0)),
        out_shape=jax.ShapeDtypeStruct((H, S, 1), F32),
        compiler_params=_params(("parallel", "parallel")),
    )(do, o)


def _attn_dq(q, kvu, kpe, do, lse, delta, cos_t, sin_t, H):
    S = q.shape[0]
    blk = _tile(S, ATT_BLOCK)
    nb = S // blk

    def body(q_ref, kn_ref, kp_ref, v_ref, do_ref, lse_ref, dl_ref, c_ref, s_ref, dq_ref, acc_ref):
        qi, ki = pl.program_id(1), pl.program_id(2)

        @pl.when(ki == 0)
        def _():
            acc_ref[...] = jnp.zeros_like(acc_ref)

        @pl.when(ki <= qi)
        def _():
            k = jnp.concatenate([kn_ref[...], kp_ref[...]], axis=1)
            s = lax.dot_general(q_ref[...], k, _DIMS["nt"], preferred_element_type=F32) * ATTN_SCALE
            p = jnp.where(_causal_mask(qi, ki, blk), jnp.exp(s - lse_ref[0]), 0.0)
            dp = lax.dot_general(do_ref[...], v_ref[...], _DIMS["nt"], preferred_element_type=F32)
            ds = (p * (dp - dl_ref[0]) * ATTN_SCALE).astype(BF16)
            acc_ref[...] += jnp.dot(ds, k, preferred_element_type=F32)

        @pl.when(ki == qi)
        def _():
            dq_ref[:, :QK_NOPE_DIM] = acc_ref[:, :QK_NOPE_DIM].astype(dq_ref.dtype)
            dq_ref[:, QK_NOPE_DIM:] = _rope(acc_ref[:, QK_NOPE_DIM:], c_ref[...], -s_ref[...]).astype(dq_ref.dtype)

    kb = lambda qi, ki: jnp.minimum(ki, qi)
    col = pl.BlockSpec((1, blk, 1), lambda h, qi, ki: (h, qi, 0))
    tab = pl.BlockSpec((blk, 128), lambda h, qi, ki: (qi, 0))
    qspec = pl.BlockSpec((blk, HEAD_PAD), lambda h, qi, ki: (qi, h))
    return _pcall(
        body, name="attn_dq", grid=(H, nb, nb),
        in_specs=[qspec,
                  pl.BlockSpec((blk, 128), lambda h, qi, ki: (kb(qi, ki), 2 * h)),
                  pl.BlockSpec((blk, 128), lambda h, qi, ki: (kb(qi, ki), 0)),
                  pl.BlockSpec((blk, 128), lambda h, qi, ki: (kb(qi, ki), 2 * h + 1)),
                  pl.BlockSpec((blk, V_DIM), lambda h, qi, ki: (qi, h)), col, col, tab, tab],
        out_specs=qspec,
        out_shape=jax.ShapeDtypeStruct((S, H * HEAD_PAD), BF16),
        scratch_shapes=[pltpu.VMEM((blk, HEAD_PAD), F32)],
        compiler_params=_params(("parallel", "parallel", "arbitrary")),
    )(q, kvu, kpe, kvu, do, lse, delta, cos_t, sin_t)


def _attn_dkv(q, kvu, kpe, do, lse_row, delta_row, H):
    S = q.shape[0]
    blk = _tile(S, ATT_BLOCK)
    nb = S // blk

    def body(q_ref, kn_ref, kp_ref, v_ref, do_ref, lse_ref, dl_ref, dkv_ref, dkp_ref, dk_acc, dv_acc):
        ki, qi = pl.program_id(1), pl.program_id(2)

        @pl.when(qi == 0)
        def _():
            dk_acc[...] = jnp.zeros_like(dk_acc)
            dv_acc[...] = jnp.zeros_like(dv_acc)

        @pl.when(qi >= ki)
        def _():
            k = jnp.concatenate([kn_ref[...], kp_ref[...]], axis=1)
            qv, dov = q_ref[...], do_ref[...]
            st = lax.dot_general(k, qv, _DIMS["nt"], preferred_element_type=F32) * ATTN_SCALE
            pt = jnp.where(_causal_mask(qi, ki, blk, transposed=True), jnp.exp(st - lse_ref[0]), 0.0)
            dv_acc[...] += jnp.dot(pt.astype(BF16), dov, preferred_element_type=F32)
            dpt = lax.dot_general(v_ref[...], dov, _DIMS["nt"], preferred_element_type=F32)
            dst = (pt * (dpt - dl_ref[0]) * ATTN_SCALE).astype(BF16)
            dk_acc[...] += jnp.dot(dst, qv, preferred_element_type=F32)

        @pl.when(qi == nb - 1)
        def _():
            dkv_ref[:, :QK_NOPE_DIM] = dk_acc[:, :QK_NOPE_DIM].astype(dkv_ref.dtype)
            dkv_ref[:, QK_NOPE_DIM:] = dv_acc[...].astype(dkv_ref.dtype)
            dkp_ref[0] = dk_acc[:, QK_NOPE_DIM:]

    qb = lambda ki, qi: jnp.maximum(qi, ki)
    row = pl.BlockSpec((1, 1, blk), lambda h, ki, qi: (h, 0, qb(ki, qi)))
    return _pcall(
        body, name="attn_dkv", grid=(H, nb, nb),
        in_specs=[pl.BlockSpec((blk, HEAD_PAD), lambda h, ki, qi: (qb(ki, qi), h)),
                  pl.BlockSpec((blk, 128), lambda h, ki, qi: (ki, 2 * h)),
                  pl.BlockSpec((blk, 128), lambda h, ki, qi: (ki, 0)),
                  pl.BlockSpec((blk, 128), lambda h, ki, qi: (ki, 2 * h + 1)),
                  pl.BlockSpec((blk, V_DIM), lambda h, ki, qi: (qb(ki, qi), h)), row, row],
        out_specs=(pl.BlockSpec((blk, HEAD_PAD), lambda h, ki, qi: (ki, h)),
                   pl.BlockSpec((1, blk, 128), lambda h, ki, qi: (h, ki, 0))),
        out_shape=(jax.ShapeDtypeStruct((S, H * HEAD_PAD), BF16), jax.ShapeDtypeStruct((H, S, 128), F32)),
        scratch_shapes=[pltpu.VMEM((blk, HEAD_PAD), F32), pltpu.VMEM((blk, V_DIM), F32)],
        compiler_params=_params(("parallel", "parallel", "arbitrary")),
    )(q, kvu, kpe, kvu, do, lse_row, delta_row)


def _mesh_pos():
    return lax.axis_index("x"), lax.axis_index("y"), lax.axis_index("c")


def _flip(pos, k):
    x, y, c = pos
    return (1 - x if k & 4 else x, 1 - y if k & 2 else y, 1 - c if k & 1 else c)


def _rank(pos):
    return 4 * pos[0] + 2 * pos[1] + pos[2]


def _exchange(src_of, dst_ref, local_src, send_sems, recv_sems, local_sem):
    me = _mesh_pos()
    local = pltpu.make_async_copy(local_src, dst_ref.at[_rank(me)], local_sem)
    local.start()
    sends = []
    for k in range(1, N_DEV):
        peer = _flip(me, k)
        cp = pltpu.make_async_remote_copy(
            src_ref=src_of(peer), dst_ref=dst_ref.at[_rank(me)],
            send_sem=send_sems.at[k - 1], recv_sem=recv_sems.at[k - 1],
            device_id=peer, device_id_type=pl.DeviceIdType.MESH)
        cp.start()
        sends.append(cp)
    for k in range(1, N_DEV):
        peer = _flip(me, k)
        pltpu.make_async_remote_copy(
            src_ref=src_of(peer), dst_ref=dst_ref.at[_rank(peer)],
            send_sem=send_sems.at[k - 1], recv_sem=recv_sems.at[k - 1],
            device_id=peer, device_id_type=pl.DeviceIdType.MESH).wait_recv()
    for cp in sends:
        cp.wait_send()
    local.wait()


_SEMS = [pltpu.SemaphoreType.DMA((N_DEV - 1,)), pltpu.SemaphoreType.DMA((N_DEV - 1,)), pltpu.SemaphoreType.DMA(())]
_ANY = pl.BlockSpec(memory_space=pl.ANY)


def _all_gather(shard, name):
    def body(x_ref, out_ref, send_sems, recv_sems, local_sem):
        _exchange(lambda peer: x_ref, out_ref, x_ref, send_sems, recv_sems, local_sem)

    return _pcall(
        body, name=name, in_specs=[_ANY], out_specs=_ANY,
        out_shape=jax.ShapeDtypeStruct((N_DEV,) + shard.shape, shard.dtype),
        scratch_shapes=_SEMS,
    )(shard)


def _all_to_all(parts, name):
    def body(x_ref, out_ref, send_sems, recv_sems, local_sem):
        me = _mesh_pos()
        _exchange(lambda peer: x_ref.at[_rank(peer)], out_ref, x_ref.at[_rank(me)], send_sems, recv_sems, local_sem)

    return _pcall(
        body, name=name, in_specs=[_ANY], out_specs=_ANY,
        out_shape=jax.ShapeDtypeStruct(parts.shape, parts.dtype),
        scratch_shapes=_SEMS,
    )(parts)


def _all_reduce_small(v):
    R, C = v.shape

    def body(x_ref, out_ref, buf_ref, send_sems, recv_sems, local_sem):
        _exchange(lambda peer: x_ref, buf_ref, x_ref, send_sems, recv_sems, local_sem)
        total = buf_ref[0]
        for j in range(1, N_DEV):
            total = total + buf_ref[j]
        out_ref[...] = total

    vm = pl.BlockSpec(memory_space=pltpu.VMEM)
    return _pcall(
        body, name="all_reduce_small", in_specs=[vm], out_specs=vm,
        out_shape=jax.ShapeDtypeStruct((R, C), F32),
        scratch_shapes=[pltpu.VMEM((N_DEV, R, C), F32)] + _SEMS,
    )(v)


def _sum_slots(parts):
    _, R, C = parts.shape
    tr = _tile(R, 64)

    def body(p_ref, o_ref):
        total = p_ref[0]
        for j in range(1, N_DEV):
            total = total + p_ref[j]
        o_ref[...] = total

    return _pcall(
        body, name="sum_slots", grid=(R // tr,),
        in_specs=[pl.BlockSpec((N_DEV, tr, C), lambda i: (0, i, 0))],
        out_specs=pl.BlockSpec((tr, C), lambda i: (i, 0)),
        out_shape=jax.ShapeDtypeStruct((R, C), F32),
        compiler_params=_params(("parallel",)),
    )(parts)


def _adamw(w, g, m, v, name):
    shape = w.shape
    C = shape[-1]
    R = w.size // C
    tr = R
    while tr * C * 4 > (1 << 20) and tr % 16 == 0:
        tr //= 2

    def body(w_ref, g_ref, m_ref, v_ref, d_ref, nm_ref, nv_ref):
        gv = g_ref[...]
        nm = ADAM_B1 * m_ref[...] + (1.0 - ADAM_B1) * gv
        nv = ADAM_B2 * v_ref[...] + (1.0 - ADAM_B2) * (gv * gv)
        m_hat = nm / (1.0 - ADAM_B1 ** ADAM_STEP)
        v_hat = nv / (1.0 - ADAM_B2 ** ADAM_STEP)
        d_ref[...] = -ADAM_LR * (m_hat / (jnp.sqrt(v_hat) + ADAM_EPS) + ADAM_WD * w_ref[...])
        nm_ref[...] = nm
        nv_ref[...] = nv

    blk = pl.BlockSpec((tr, C), lambda i: (i, 0))
    sds = jax.ShapeDtypeStruct((R, C), F32)
    outs = _pcall(
        body, name=name, grid=(R // tr,),
        in_specs=[blk] * 4, out_specs=(blk,) * 3, out_shape=(sds,) * 3,
        compiler_params=_params(("parallel",)),
    )(*(t.reshape(R, C) for t in (w, g, m, v)))
    return tuple(o.reshape(shape) for o in outs)


def _unshard(pieces, axis):
    t = jnp.moveaxis(pieces, 0, axis)
    shp = list(t.shape)
    shp[axis:axis + 2] = [shp[axis] * shp[axis + 1]]
    return t.reshape(shp)


def _to_shards(full, axis):
    shp = list(full.shape)
    shp[axis:axis + 1] = [N_DEV, shp[axis] // N_DEV]
    return jnp.moveaxis(full.reshape(shp), axis, 0)


def _pack_rows(flat_parts, lead):
    cat = jnp.concatenate(flat_parts, axis=1)
    n = cat.shape[1]
    unit = 64 * PACK_LANES
    pad = (-n) % unit
    if pad:
        cat = jnp.pad(cat, ((0, 0), (0, pad)))
    return cat.reshape(lead, (n + pad) // PACK_LANES, PACK_LANES)


_SHARDED = (("pool_w", 2), ("w_kv_a", 0), ("w_kv_b", 1), ("w_q_a", 1), ("w_q_b", 2), ("w_o", 1),
            ("w_gate", 2), ("w_up", 2), ("w_down", 1), ("pool_norm", 1), ("pool_scale", 1))
_REPLICATED = ("kv_in_norm", "kv_latent_norm", "attn_norm", "q_latent_norm", "ffn_norm", "final_norm")
_WEIGHTS = ("pool_norm", "pool_w", "pool_scale", "kv_in_norm", "w_kv_a", "kv_latent_norm", "w_kv_b", "attn_norm",
            "w_q_a", "q_latent_norm", "w_q_b", "w_o", "ffn_norm", "w_gate", "w_up", "w_down", "final_norm")


def _ffn_fwd(x, norm_g, wg, wu, wd, tag):
    h = _rmsnorm(x, norm_g, f"ffn_norm{tag}")
    g, u, a = _ffn_up(h, wg, wu, f"ffn_up{tag}")
    y = _mm(a, wd, "nn", f"ffn_down{tag}", res=x)
    return y, (h, g, u, a)


def _ffn_bwd(x, norm_g, wg, wu, wd, saved, dy, tag):
    h, g, u, a = saved
    dg, du = _ffn_dact(dy, wd, g, u, f"ffn_dact{tag}")
    dwd = _mm(a, dy, "tn", f"ffn_dwd{tag}")
    dh = _mm(dg, wg, "nt", f"ffn_dh{tag}", a2=du, b2=wu)
    dwg = _mm(h, dg, "tn", f"ffn_dwg{tag}")
    dwu = _mm(h, du, "tn", f"ffn_dwu{tag}")
    dx, dnorm = _rmsnorm_bwd(x, norm_g, dh, dy, f"ffn_norm_bwd{tag}")
    return dx, dnorm, dwg, dwu, dwd


def kernel(x, positions, pool_norm, pool_w, pool_scale, kv_in_norm, w_kv_a, kv_latent_norm, w_kv_b, attn_norm, w_q_a, q_latent_norm, w_q_b, w_o, ffn_norm, w_gate, w_up, w_down, final_norm, loss_target, m_pool_norm, m_pool_w, m_pool_scale, m_kv_in_norm, m_w_kv_a, m_kv_latent_norm, m_w_kv_b, m_attn_norm, m_w_q_a, m_q_latent_norm, m_w_q_b, m_w_o, m_ffn_norm, m_w_gate, m_w_up, m_w_down, m_final_norm, v_pool_norm, v_pool_w, v_pool_scale, v_kv_in_norm, v_w_kv_a, v_kv_latent_norm, v_w_kv_b, v_attn_norm, v_w_q_a, v_q_latent_norm, v_w_q_b, v_w_o, v_ffn_norm, v_w_gate, v_w_up, v_w_down, v_final_norm):
    env = dict(locals())
    weights = {n: env[n] for n in _WEIGHTS}
    m_in = {n: env["m_" + n] for n in _WEIGHTS}
    v_in = {n: env["v_" + n] for n in _WEIGHTS}

    S, D = x.shape[1], x.shape[2]
    xs = x.reshape(S, D)
    target = loss_target.reshape(S, D)

    mats = [(n, ax) for n, ax in _SHARDED if weights[n].size > 4096]
    vecs = [(n, ax) for n, ax in _SHARDED if weights[n].size <= 4096]
    packed = _pack_rows([weights[n].astype(BF16).reshape(1, -1) for n, _ in mats], 1)[0]
    gathered = _all_gather(packed, "gather_weights").reshape(N_DEV, -1)
    packed_v = _pack_rows([weights[n].reshape(1, -1) for n, _ in vecs], 1)[0]
    gathered_v = _all_gather(packed_v, "gather_gains").reshape(N_DEV, -1)
    full = {}
    for group, buf in ((mats, gathered), (vecs, gathered_v)):
        off = 0
        for n, ax in group:
            shp = weights[n].shape
            full[n] = _unshard(buf[:, off:off + weights[n].size].reshape((N_DEV,) + shp), ax)
            off += weights[n].size

    g_pool_norm = full["pool_norm"].reshape(1, D)
    g_pool_scale = full["pool_scale"].reshape(1, D)
    wp = full["pool_w"][0]
    R_kv = w_kv_b.shape[0]
    H = full["w_kv_b"].shape[1] // (QK_NOPE_DIM + V_DIM)
    wkva = jnp.pad(full["w_kv_a"], ((0, 0), (0, 128 - QK_ROPE_DIM)))
    wkvb = full["w_kv_b"]
    wqa = full["w_q_a"][0]
    R_q = wqa.shape[1]
    qk = QK_NOPE_DIM + QK_ROPE_DIM
    wqb = jnp.pad(full["w_q_b"][0].reshape(R_q, H, qk), ((0, 0), (0, 0), (0, HEAD_PAD - qk))).reshape(R_q, H * HEAD_PAD)
    wo = full["w_o"][0]
    wg, wu, wd = full["w_gate"], full["w_up"], full["w_down"]

    g_kv_in = kv_in_norm.reshape(1, D)
    g_kv_lat = kv_latent_norm.reshape(1, R_kv)
    g_attn = attn_norm.reshape(1, D)
    g_q_lat = q_latent_norm.reshape(1, R_q)
    g_final = final_norm.reshape(1, D)

    half = QK_ROPE_DIM // 2
    inv_freq = ROPE_BASE ** (-jnp.arange(half, dtype=F32) / half)
    ang = positions.reshape(S).astype(F32)[:, None] * inv_freq
    cos, sin = jnp.cos(ang), jnp.sin(ang)
    zeros = jnp.zeros((S, 128 - QK_ROPE_DIM), F32)
    cos_t = jnp.concatenate([cos, cos, zeros], axis=1)
    sin_t = jnp.concatenate([-sin, sin, zeros], axis=1)

    diff = _pool_pre(xs, g_pool_norm)
    x1 = _pool_mix(diff, wp, g_pool_scale, xs)
    x2, ffn0 = _ffn_fwd(x1, ffn_norm[0:1], wg[0], wu[0], wd[0], "0")

    hk = _rmsnorm(x2, g_kv_in, "kv_in_norm")
    kv = _mm(hk, wkva, "nn", "kv_a")
    ckv, kpe = _kv_mid(kv, g_kv_lat, cos_t, sin_t)
    kvu = _mm(ckv, wkvb, "nn", "kv_b", out_dtype=BF16)

    ha = _rmsnorm(x2, g_attn, "attn_norm")
    ql = _mm(ha, wqa, "nn", "q_a")
    qn = _rmsnorm(ql, g_q_lat, "q_latent_norm")
    qr = _q_rope(_mm(qn, wqb, "nn", "q_b"), cos_t, sin_t)
    o, lse = _attn_fwd(qr, kvu, kpe, H)
    x3 = _mm(o, wo, "nn", "attn_out", res=x2)
    x4, ffn1 = _ffn_fwd(x3, ffn_norm[1:2], wg[1], wu[1], wd[1], "1")

    loss_part, dx4, d_final = _loss_head(x4, g_final, target)

    dx3, d_ffn1, dwg1, dwu1, dwd1 = _ffn_bwd(x3, ffn_norm[1:2], wg[1], wu[1], wd[1], ffn1, dx4, "1")

    do = _mm(dx3, wo, "nt", "attn_out_dx", out_dtype=BF16)
    dwo = _mm(o, dx3, "tn", "attn_out_dw")
    delta = _attn_delta(do, o, H)
    dq = _attn_dq(qr, kvu, kpe, do, lse, delta, cos_t, sin_t, H)
    dkvu, dkpe = _attn_dkv(qr, kvu, kpe, do, lse.reshape(H, 1, S), delta.reshape(H, 1, S), H)

    dqn = _mm(dq, wqb, "nt", "q_b_dx")
    dwqb = _mm(qn, dq, "tn", "q_b_dw")
    dql, d_q_lat = _rmsnorm_bwd(ql, g_q_lat, dqn, None, "q_latent_norm_bwd", out_dtype=BF16)
    dha = _mm(dql, wqa, "nt", "q_a_dx")
    dwqa = _mm(ha, dql, "tn", "q_a_dw")
    dx2, d_attn = _rmsnorm_bwd(x2, g_attn, dha, dx3, "attn_norm_bwd")

    dckv = _mm(dkvu, wkvb, "nt", "kv_b_dx")
    dwkvb = _mm(ckv, dkvu, "tn", "kv_b_dw")
    dkv, d_kv_lat = _kv_mid_bwd(kv, g_kv_lat, dckv, dkpe, cos_t, sin_t)
    dhk = _mm(dkv, wkva, "nt", "kv_a_dx")
    dwkva = _mm(hk, dkv, "tn", "kv_a_dw")
    dx2, d_kv_in = _rmsnorm_bwd(x2, g_kv_in, dhk, dx2, "kv_in_norm_bwd")

    dx1, d_ffn0, dwg0, dwu0, dwd0 = _ffn_bwd(x1, ffn_norm[0:1], wg[0], wu[0], wd[0], ffn0, dx2, "0")

    ddiff, dwp, d_pool_scale = _pool_mix_bwd(diff, wp, g_pool_scale, dx1)
    grad_x, d_pool_norm = _pool_pre_bwd(xs, g_pool_norm, ddiff, dx1)

    partial = {
        "pool_w": dwp[None],
        "w_kv_a": dwkva[:, :R_kv + QK_ROPE_DIM],
        "w_kv_b": dwkvb,
        "w_q_a": dwqa[None],
        "w_q_b": dwqb.reshape(R_q, H, HEAD_PAD)[:, :, :qk].reshape(1, R_q, H * qk),
        "w_o": dwo[None],
        "w_gate": jnp.stack([dwg0, dwg1]),
        "w_up": jnp.stack([dwu0, dwu1]),
        "w_down": jnp.stack([dwd0, dwd1]),
        "pool_norm": d_pool_norm,
        "pool_scale": d_pool_scale,
    }
    send = _pack_rows([_to_shards(partial[n], ax).reshape(N_DEV, -1) for n, ax in _SHARDED], N_DEV)
    summed = _sum_slots(_all_to_all(send, "exchange_grads")).reshape(-1)
    grads = {}
    off = 0
    for n, _ in _SHARDED:
        grads[n] = summed[off:off + weights[n].size].reshape(weights[n].shape)
        off += weights[n].size

    small = {"kv_in_norm": d_kv_in, "kv_latent_norm": d_kv_lat, "attn_norm": d_attn, "q_latent_norm": d_q_lat,
             "ffn_norm": jnp.concatenate([d_ffn0, d_ffn1], axis=0), "final_norm": d_final}
    small_packed = jnp.concatenate([small[n].reshape(-1) for n in _REPLICATED] + [loss_part.reshape(-1)])
    n_small = small_packed.shape[0]
    pad = (-n_small) % (8 * 128)
    reduced = _all_reduce_small(jnp.pad(small_packed, (0, pad)).reshape(-1, 128)).reshape(-1)
    off = 0
    for n in _REPLICATED:
        grads[n] = reduced[off:off + weights[n].size].reshape(weights[n].shape)
        off += weights[n].size
    loss = reduced[off]

    delta_w, new_m, new_v = {}, {}, {}
    for n in _WEIGHTS:
        delta_w[n], new_m[n], new_v[n] = _adamw(weights[n], grads[n], m_in[n], v_in[n], "adamw_" + n)

    return (loss, grad_x.reshape(x.shape), *[grads[n] for n in _WEIGHTS], *[delta_w[n] for n in _WEIGHTS],
            *[new_m[n] for n in _WEIGHTS], *[new_v[n] for n in _WEIGHTS])
```

```python
import math

import jax
import jax.numpy as jnp
import numpy as np
from jax import lax
from jax.experimental import pallas as pl
from jax.experimental.pallas import tpu as pltpu

F32 = jnp.float32
BF16 = jnp.bfloat16

N_DEV = 8
POOL_WINDOWS = (2, 4, 8, 16)
POOL_HALO = 16
QK_NOPE_DIM = 128
QK_ROPE_DIM = 64
QK_DIM = QK_NOPE_DIM + QK_ROPE_DIM
HEAD_PAD = 256
V_DIM = 128
ROPE_BASE = 10000.0
ATTN_SCALE = 1.0 / math.sqrt(QK_DIM)
EXP2_SCALE = ATTN_SCALE * math.log2(math.e)
NORM_EPS = 1e-6
ADAM_LR = 0.001
ADAM_B1 = 0.9
ADAM_B2 = 0.999
ADAM_EPS = 1e-08
ADAM_WD = 0.01
ADAM_STEP = 10

ROW_TILE = 512
ATT_BLOCK = 512
MM_TN, MM_TK = 512, 512
VMEM_LIMIT = 48 * 1024 * 1024

_pcall = pl.pallas_call


def _params(sem):
    return pltpu.CompilerParams(dimension_semantics=sem, vmem_limit_bytes=VMEM_LIMIT)


def _tile(dim, default):
    if dim % default == 0:
        return default
    assert dim <= 2 * default, (dim, default)
    return dim


def _row_tile(rows, width):
    ts = _tile(rows, ROW_TILE)
    while ts * width * 4 > (2 << 20) and ts % 16 == 0:
        ts //= 2
    return ts


def _rms_scale(x):
    return lax.rsqrt(jnp.mean(x * x, axis=-1, keepdims=True) + NORM_EPS)


def _rms_bwd(x, g, dy):
    r = _rms_scale(x)
    xn = x * r
    u = dy * g
    dx = r * (u - xn * jnp.mean(u * xn, axis=-1, keepdims=True))
    return dx, dy * xn


def _swap_halves(x):
    lane = lax.broadcasted_iota(jnp.int32, x.shape, 1)
    return jnp.where(lane < QK_ROPE_DIM // 2, pltpu.roll(x, 128 - QK_ROPE_DIM // 2, 1), pltpu.roll(x, QK_ROPE_DIM // 2, 1))


def _rope(x, cos_t, sin_t):
    return x * cos_t + _swap_halves(x) * sin_t


def _rmsnorm(x, g, name):
    S, D = x.shape
    ts = _row_tile(S, D)

    def body(x_ref, g_ref, o_ref):
        xf = x_ref[...]
        o_ref[...] = (xf * _rms_scale(xf) * g_ref[...]).astype(o_ref.dtype)

    return _pcall(
        body, name=name, grid=(S // ts,),
        in_specs=[pl.BlockSpec((ts, D), lambda i: (i, 0)), pl.BlockSpec((1, D), lambda i: (0, 0))],
        out_specs=pl.BlockSpec((ts, D), lambda i: (i, 0)),
        out_shape=jax.ShapeDtypeStruct((S, D), BF16),
        compiler_params=_params(("parallel",)),
    )(x, g)


def _rmsnorm_bwd(x, g, dy, res, name, out_dtype=F32, with_bf16=False):
    S, D = x.shape
    ts = _row_tile(S, D)
    has_res = res is not None

    def body(*refs):
        refs = list(refs)
        x_ref, g_ref, dy_ref = refs[:3]
        res_ref = refs[3] if has_res else None
        outs = refs[3 + has_res:]
        dx_ref, dg_ref = outs[0], outs[-1]
        dx, dgt = _rms_bwd(x_ref[...], g_ref[...], dy_ref[...].astype(F32))
        if has_res:
            dx = res_ref[...] + dx
        dx_ref[...] = dx.astype(dx_ref.dtype)
        if with_bf16:
            outs[1][...] = dx.astype(BF16)

        @pl.when(pl.program_id(0) == 0)
        def _():
            dg_ref[...] = jnp.zeros_like(dg_ref)

        dg_ref[...] += jnp.sum(dgt, axis=0, keepdims=True)

    row = pl.BlockSpec((ts, D), lambda i: (i, 0))
    vec = pl.BlockSpec((1, D), lambda i: (0, 0))
    out_specs = [row] + ([row] if with_bf16 else []) + [vec]
    out_shape = ([jax.ShapeDtypeStruct((S, D), out_dtype)] + ([jax.ShapeDtypeStruct((S, D), BF16)] if with_bf16 else [])
                 + [jax.ShapeDtypeStruct((1, D), F32)])
    return _pcall(
        body, name=name, grid=(S // ts,),
        in_specs=[row, vec, row] + ([row] if has_res else []),
        out_specs=tuple(out_specs), out_shape=tuple(out_shape),
        compiler_params=_params(("arbitrary",)),
    )(*([x, g, dy] + ([res] if has_res else [])))


def _pool_pre(x, g):
    S, D = x.shape
    ts = _row_tile(S, D)
    gw = D // len(POOL_WINDOWS)
    hb = ts // POOL_HALO

    def body(x_ref, halo_ref, g_ref, o_ref):
        i = pl.program_id(0)
        xx = jnp.concatenate([halo_ref[...], x_ref[...]], axis=0)
        h = xx * _rms_scale(xx) * g_ref[...]
        t = i * ts - POOL_HALO + lax.broadcasted_iota(jnp.int32, (ts + POOL_HALO, 1), 0)
        h = jnp.where(t >= 0, h, 0.0)
        tf = t[POOL_HALO:].astype(F32)
        for gi, w in enumerate(POOL_WINDOWS):
            hg = h[:, gi * gw:(gi + 1) * gw]
            acc, span = hg, 1
            while span < w:
                acc = acc + pltpu.roll(acc, span, 0)
                span *= 2
            count = jnp.minimum(tf + 1.0, float(w))
            diff = acc[POOL_HALO:] / count - hg[POOL_HALO:]
            o_ref[:, gi * gw:(gi + 1) * gw] = diff.astype(o_ref.dtype)

    return _pcall(
        body, name="pool_pre", grid=(S // ts,),
        in_specs=[pl.BlockSpec((ts, D), lambda i: (i, 0)),
                  pl.BlockSpec((POOL_HALO, D), lambda i: (jnp.maximum(i * hb - 1, 0), 0)),
                  pl.BlockSpec((1, D), lambda i: (0, 0))],
        out_specs=pl.BlockSpec((ts, D), lambda i: (i, 0)),
        out_shape=jax.ShapeDtypeStruct((S, D), BF16),
        compiler_params=_params(("parallel",)),
    )(x, x, g)


def _pool_pre_bwd(x, g, dd, res):
    S, D = x.shape
    ts = _row_tile(S, D)
    gw = D // len(POOL_WINDOWS)
    hb = ts // POOL_HALO
    last_halo = S // POOL_HALO - 1

    def body(x_ref, g_ref, dd_ref, halo_ref, res_ref, dx_ref, dg_ref):
        i = pl.program_id(0)
        ee = jnp.concatenate([dd_ref[...], halo_ref[...]], axis=0)
        t = i * ts + lax.broadcasted_iota(jnp.int32, (ts + POOL_HALO, 1), 0)
        ee = jnp.where(t < S, ee, 0.0)
        tf = t.astype(F32)
        n = ts + POOL_HALO
        for gi, w in enumerate(POOL_WINDOWS):
            eg = ee[:, gi * gw:(gi + 1) * gw]
            acc, span = eg / jnp.minimum(tf + 1.0, float(w)), 1
            while span < w:
                acc = acc + pltpu.roll(acc, n - span, 0)
                span *= 2
            dx_ref[:, gi * gw:(gi + 1) * gw] = acc[:ts] - eg[:ts]
        dx, dgt = _rms_bwd(x_ref[...], g_ref[...], dx_ref[...])
        dx_ref[...] = res_ref[...] + dx

        @pl.when(i == 0)
        def _():
            dg_ref[...] = jnp.zeros_like(dg_ref)

        dg_ref[...] += jnp.sum(dgt, axis=0, keepdims=True)

    row = pl.BlockSpec((ts, D), lambda i: (i, 0))
    vec = pl.BlockSpec((1, D), lambda i: (0, 0))
    return _pcall(
        body, name="pool_pre_bwd", grid=(S // ts,),
        in_specs=[row, vec, row,
                  pl.BlockSpec((POOL_HALO, D), lambda i: (jnp.minimum((i + 1) * hb, last_halo), 0)), row],
        out_specs=(row, vec),
        out_shape=(jax.ShapeDtypeStruct((S, D), F32), jax.ShapeDtypeStruct((1, D), F32)),
        compiler_params=_params(("arbitrary",)),
    )(x, g, dd, dd, res)


def _pool_mix(diff, w, scale, x):
    S, D = x.shape
    G, gw, _ = w.shape
    ts = _tile(S, ROW_TILE)

    def body(d_ref, w_ref, s_ref, x_ref, o_ref):
        mo = jnp.dot(d_ref[...], w_ref[0], preferred_element_type=F32)
        o_ref[...] = x_ref[...] + mo * s_ref[...]

    blk = pl.BlockSpec((ts, gw), lambda i, g: (i, g))
    return _pcall(
        body, name="pool_mix", grid=(S // ts, G),
        in_specs=[blk, pl.BlockSpec((1, gw, gw), lambda i, g: (g, 0, 0)), pl.BlockSpec((1, gw), lambda i, g: (0, g)), blk],
        out_specs=blk,
        out_shape=jax.ShapeDtypeStruct((S, D), F32),
        compiler_params=_params(("parallel", "parallel")),
    )(diff, w, scale, x)


def _pool_mix_bwd(diff, w, scale, dy):
    S, D = dy.shape
    G, gw, _ = w.shape
    ts = _tile(S, ROW_TILE)

    def body(d_ref, w_ref, s_ref, dy_ref, dd_ref, dw_ref, ds_ref):
        i = pl.program_id(1)
        d = d_ref[...]
        dy = dy_ref[...]
        mo = jnp.dot(d, w_ref[0], preferred_element_type=F32)
        dmo = (dy * s_ref[...]).astype(BF16)
        dd_ref[...] = lax.dot_general(dmo, w_ref[0], (((1,), (1,)), ((), ())), preferred_element_type=F32)

        @pl.when(i == 0)
        def _():
            dw_ref[...] = jnp.zeros_like(dw_ref)
            ds_ref[...] = jnp.zeros_like(ds_ref)

        dw_ref[0] += lax.dot_general(d, dmo, (((0,), (0,)), ((), ())), preferred_element_type=F32)
        ds_ref[...] += jnp.sum(dy * mo, axis=0, keepdims=True)

    blk = pl.BlockSpec((ts, gw), lambda g, i: (i, g))
    wspec = pl.BlockSpec((1, gw, gw), lambda g, i: (g, 0, 0))
    vec = pl.BlockSpec((1, gw), lambda g, i: (0, g))
    return _pcall(
        body, name="pool_mix_bwd", grid=(G, S // ts),
        in_specs=[blk, wspec, vec, blk],
        out_specs=(blk, wspec, vec),
        out_shape=(jax.ShapeDtypeStruct((S, D), F32), jax.ShapeDtypeStruct((G, gw, gw), F32),
                   jax.ShapeDtypeStruct((1, D), F32)),
        compiler_params=_params(("parallel", "arbitrary")),
    )(diff, w, scale, dy)


def _loss_head(x, g, target):
    S, D = x.shape
    ts = _row_tile(S, D)

    def body(x_ref, g_ref, t_ref, loss_ref, dx_ref, dxb_ref, dg_ref):
        xf = x_ref[...]
        gv = g_ref[...]
        err = xf * _rms_scale(xf) * gv - t_ref[...]
        dx, dgt = _rms_bwd(xf, gv, err * (1.0 / D))
        dx_ref[...] = dx
        dxb_ref[...] = dx.astype(BF16)

        @pl.when(pl.program_id(0) == 0)
        def _():
            loss_ref[...] = jnp.zeros_like(loss_ref)
            dg_ref[...] = jnp.zeros_like(dg_ref)

        part = 0.5 * jnp.sum(jnp.sum(err * err, axis=-1, keepdims=True) * (1.0 / D), axis=0, keepdims=True)
        loss_ref[...] += jnp.broadcast_to(part, loss_ref.shape)
        dg_ref[...] += jnp.sum(dgt, axis=0, keepdims=True)

    row = pl.BlockSpec((ts, D), lambda i: (i, 0))
    vec = pl.BlockSpec((1, D), lambda i: (0, 0))
    return _pcall(
        body, name="loss_head", grid=(S // ts,),
        in_specs=[row, vec, row],
        out_specs=(pl.BlockSpec((1, 128), lambda i: (0, 0)), row, row, vec),
        out_shape=(jax.ShapeDtypeStruct((1, 128), F32), jax.ShapeDtypeStruct((S, D), F32),
                   jax.ShapeDtypeStruct((S, D), BF16), jax.ShapeDtypeStruct((1, D), F32)),
        compiler_params=_params(("arbitrary",)),
    )(x, g, target)


_DIMS = {"nn": (((1,), (0,)), ((), ())), "nt": (((1,), (1,)), ((), ())), "tn": (((0,), (0,)), ((), ()))}


def _mm(name, mode, a, b, tiles, grid, a_map, b_map, o_map, out_shape, out_dtype=F32, res=None, a2=None, b2=None):
    tm, tn, tk = tiles
    nk = grid[-1]
    dual, has_res = a2 is not None, res is not None
    dn = _DIMS[mode]

    def body(*refs):
        refs = list(refs)
        n_in = 2 + 2 * dual + has_res
        ins, o_ref = refs[:n_in], refs[n_in]

        def product():
            part = lax.dot_general(ins[0][...].astype(BF16), ins[1][...].astype(BF16), dn, preferred_element_type=F32)
            if dual:
                part += lax.dot_general(ins[2][...].astype(BF16), ins[3][...].astype(BF16), dn,
                                        preferred_element_type=F32)
            return part

        def finish(out):
            if has_res:
                out = ins[-1][...] + out
            o_ref[...] = out.astype(o_ref.dtype)

        if nk == 1:
            finish(product())
            return
        acc_ref = refs[n_in + 1]
        k = pl.program_id(2)

        @pl.when(k == 0)
        def _():
            acc_ref[...] = jnp.zeros_like(acc_ref)

        acc_ref[...] += product()

        @pl.when(k == nk - 1)
        def _():
            finish(acc_ref[...])

    a_spec = pl.BlockSpec((tk, tm) if mode == "tn" else (tm, tk), a_map)
    b_spec = pl.BlockSpec((tn, tk) if mode == "nt" else (tk, tn), b_map)
    o_spec = pl.BlockSpec((tm, tn), o_map)
    operands, in_specs = [a, b], [a_spec, b_spec]
    if dual:
        operands += [a2, b2]
        in_specs += [a_spec, b_spec]
    if has_res:
        operands.append(res)
        in_specs.append(o_spec)
    return _pcall(
        body, name=name, grid=grid,
        in_specs=in_specs, out_specs=o_spec,
        out_shape=jax.ShapeDtypeStruct(out_shape, out_dtype),
        scratch_shapes=[pltpu.VMEM((tm, tn), F32)] if nk > 1 else [],
        compiler_params=_params(("parallel", "parallel", "arbitrary")),
    )(*operands)


def _mm_plain(name, mode, a, b, out_dtype=F32, res=None):
    if mode == "tn":
        (K, M), N = a.shape, b.shape[1]
    elif mode == "nt":
        (M, K), N = a.shape, b.shape[0]
    else:
        (M, K), N = a.shape, b.shape[1]
    tm, tn, tk = _tile(M, ROW_TILE), _tile(N, MM_TN), _tile(K, MM_TK)
    a_map = (lambda i, j, k: (k, i)) if mode == "tn" else (lambda i, j, k: (i, k))
    b_map = (lambda i, j, k: (j, k)) if mode == "nt" else (lambda i, j, k: (k, j))
    return _mm(name, mode, a, b, (tm, tn, tk), (M // tm, N // tn, K // tk), a_map, b_map, lambda i, j, k: (i, j),
               (M, N), out_dtype, res)


def _slots_out(name, a, w_slots, out_dtype):
    S, K = a.shape
    n = w_slots.shape[1]
    tm = _tile(S, ROW_TILE)
    nmb = S // tm
    return _mm(name, "nn", a, w_slots, (tm, n, K), (nmb, N_DEV, 1),
               lambda i, j, k: (i, 0), lambda i, j, k: (j, 0), lambda i, j, k: (j * nmb + i, 0),
               (N_DEV * S, n), out_dtype)


def _slots_in_nt(name, a_slots, w_slots, S, a2=None, w2=None):
    n = a_slots.shape[1]
    K = w_slots.shape[0] // N_DEV
    tm = _tile(S, ROW_TILE)
    nmb = S // tm
    return _mm(name, "nt", a_slots, w_slots, (tm, K, n), (nmb, 1, N_DEV),
               lambda i, q, j: (j * nmb + i, 0), lambda i, q, j: (j, 0), lambda i, q, j: (i, 0),
               (S, K), F32, a2=a2, b2=w2)


def _slots_dw(name, a, d_slots):
    S, K = a.shape
    n = d_slots.shape[1]
    tk = _tile(S, ROW_TILE)
    nkb = S // tk
    return _mm(name, "tn", a, d_slots, (K, n, tk), (N_DEV, 1, nkb),
               lambda j, q, k: (k, 0), lambda j, q, k: (j * nkb + k, 0), lambda j, q, k: (j, 0),
               (N_DEV * K, n), F32)


def _sigmoid(x):
    return 1.0 / (1.0 + jnp.exp(-x))


def _ffn_up(h, wg, wu, layer, n_layers, name):
    S, D = h.shape
    fs = wg.shape[1]
    tm = _tile(S, ROW_TILE)
    nmb = S // tm

    def body(h_ref, wg_ref, wu_ref, g_ref, u_ref, a_ref):
        hv = h_ref[...]
        g = jnp.dot(hv, wg_ref[...], preferred_element_type=F32)
        u = jnp.dot(hv, wu_ref[...], preferred_element_type=F32)
        g_ref[...] = g
        u_ref[...] = u
        a_ref[...] = (g * _sigmoid(g) * u).astype(a_ref.dtype)

    w_spec = pl.BlockSpec((D, fs), lambda i, j: (j * n_layers + layer, 0))
    o_spec = pl.BlockSpec((tm, fs), lambda i, j: (j * nmb + i, 0))
    sds = jax.ShapeDtypeStruct((N_DEV * S, fs), F32)
    return _pcall(
        body, name=name, grid=(nmb, N_DEV),
        in_specs=[pl.BlockSpec((tm, D), lambda i, j: (i, 0)), w_spec, w_spec],
        out_specs=(o_spec, o_spec, o_spec),
        out_shape=(sds, sds, jax.ShapeDtypeStruct((N_DEV * S, fs), BF16)),
        compiler_params=_params(("parallel", "arbitrary")),
    )(h, wg, wu)


def _ffn_dact(dy, wd, g, u, layer, n_layers, name):
    S, D = dy.shape
    fs = g.shape[1]
    tm = _tile(S, ROW_TILE)
    nmb = S // tm

    def body(dy_ref, wd_ref, g_ref, u_ref, dg_ref, du_ref):
        da = lax.dot_general(dy_ref[...], wd_ref[...], _DIMS["nt"], preferred_element_type=F32)
        g, u = g_ref[...], u_ref[...]
        sig = _sigmoid(g)
        dg_ref[...] = (da * u * (sig * (1.0 + g * (1.0 - sig)))).astype(dg_ref.dtype)
        du_ref[...] = (da * (g * sig)).astype(du_ref.dtype)

    o_spec = pl.BlockSpec((tm, fs), lambda i, j: (j * nmb + i, 0))
    sds = jax.ShapeDtypeStruct((N_DEV * S, fs), BF16)
    return _pcall(
        body, name=name, grid=(nmb, N_DEV),
        in_specs=[pl.BlockSpec((tm, D), lambda i, j: (i, 0)),
                  pl.BlockSpec((fs, D), lambda i, j: (j * n_layers + layer, 0)), o_spec, o_spec],
        out_specs=(o_spec, o_spec), out_shape=(sds, sds),
        compiler_params=_params(("parallel", "arbitrary")),
    )(dy, wd, g, u)


def _q_rope(q, cos_t, sin_t, S):
    rows, W = q.shape
    ts = _tile(S, ROW_TILE)
    nsb = S // ts

    def body(q_ref, c_ref, s_ref, o_ref):
        o_ref[:, :QK_NOPE_DIM] = q_ref[:, :QK_NOPE_DIM].astype(o_ref.dtype)
        o_ref[:, QK_NOPE_DIM:] = _rope(q_ref[:, QK_NOPE_DIM:], c_ref[...], s_ref[...]).astype(o_ref.dtype)

    blk = pl.BlockSpec((ts, HEAD_PAD), lambda i, h: (i, h))
    tab = pl.BlockSpec((ts, 128), lambda i, h: (i % nsb, 0))
    return _pcall(
        body, name="q_rope", grid=(rows // ts, W // HEAD_PAD),
        in_specs=[blk, tab, tab], out_specs=blk,
        out_shape=jax.ShapeDtypeStruct((rows, W), BF16),
        compiler_params=_params(("parallel", "parallel")),
    )(q, cos_t, sin_t)


def _kv_mid(kv, g, cos_t, sin_t):
    S, W = kv.shape
    R = W - 128
    ts = _tile(S, ROW_TILE)

    def body(kv_ref, g_ref, c_ref, s_ref, c_out, k_out):
        cf = kv_ref[:, :R]
        c_out[...] = (cf * _rms_scale(cf) * g_ref[...]).astype(c_out.dtype)
        k_out[...] = _rope(kv_ref[:, R:], c_ref[...], s_ref[...]).astype(k_out.dtype)

    tab = pl.BlockSpec((ts, 128), lambda i: (i, 0))
    return _pcall(
        body, name="kv_mid", grid=(S // ts,),
        in_specs=[pl.BlockSpec((ts, W), lambda i: (i, 0)), pl.BlockSpec((1, R), lambda i: (0, 0)), tab, tab],
        out_specs=(pl.BlockSpec((ts, R), lambda i: (i, 0)), tab),
        out_shape=(jax.ShapeDtypeStruct((S, R), BF16), jax.ShapeDtypeStruct((S, 128), BF16)),
        compiler_params=_params(("parallel",)),
    )(kv, g, cos_t, sin_t)


def _kv_mid_bwd(kv, g, dc, dkpe, cos_t, sin_t):
    S, W = kv.shape
    R = W - 128
    H = dkpe.shape[0]
    ts = _row_tile(S, H * 128)

    def body(kv_ref, g_ref, dc_ref, dk_ref, c_ref, s_ref, dkv_ref, dg_ref):
        dx, dgt = _rms_bwd(kv_ref[:, :R], g_ref[...], dc_ref[...])
        dkv_ref[:, :R] = dx.astype(dkv_ref.dtype)
        dk = dk_ref[0]
        for h in range(1, H):
            dk = dk + dk_ref[h]
        dkv_ref[:, R:] = _rope(dk, c_ref[...], -s_ref[...]).astype(dkv_ref.dtype)

        @pl.when(pl.program_id(0) == 0)
        def _():
            dg_ref[...] = jnp.zeros_like(dg_ref)

        dg_ref[...] += jnp.sum(dgt, axis=0, keepdims=True)

    tab = pl.BlockSpec((ts, 128), lambda i: (i, 0))
    vec = pl.BlockSpec((1, R), lambda i: (0, 0))
    return _pcall(
        body, name="kv_mid_bwd", grid=(S // ts,),
        in_specs=[pl.BlockSpec((ts, W), lambda i: (i, 0)), vec, pl.BlockSpec((ts, R), lambda i: (i, 0)),
                  pl.BlockSpec((H, ts, 128), lambda i: (0, i, 0)), tab, tab],
        out_specs=(pl.BlockSpec((ts, W), lambda i: (i, 0)), vec),
        out_shape=(jax.ShapeDtypeStruct((S, W), BF16), jax.ShapeDtypeStruct((1, R), F32)),
        compiler_params=_params(("arbitrary",)),
    )(kv, g, dc, dkpe, cos_t, sin_t)


def _block_pairs(nb, by_key):
    pairs = ([(qi, ki) for ki in range(nb) for qi in range(ki, nb)] if by_key
             else [(qi, ki) for qi in range(nb) for ki in range(qi + 1)])
    return jnp.asarray(np.array([p[0] for p in pairs], np.int32)), jnp.asarray(np.array([p[1] for p in pairs], np.int32))


def _causal_mask(blk, transposed=False):
    r = lax.broadcasted_iota(jnp.int32, (blk, blk), 0)
    c = lax.broadcasted_iota(jnp.int32, (blk, blk), 1)
    return (r <= c) if transposed else (c <= r)


def _attn_specs(S, blk, hpd, by_key):
    nb = S // blk
    if by_key:
        qrow = lambda h, t, qt, kt: (h // hpd) * nb + qt[t]
        krow = lambda h, t, qt, kt: (h // hpd) * nb + kt[t]
    else:
        qrow = lambda h, t, qt, kt: (h // hpd) * nb + qt[t]
        krow = lambda h, t, qt, kt: (h // hpd) * nb + kt[t]
    q_spec = pl.BlockSpec((blk, HEAD_PAD), lambda h, t, qt, kt: (qrow(h, t, qt, kt), h % hpd))
    kn_spec = pl.BlockSpec((blk, 128), lambda h, t, qt, kt: (krow(h, t, qt, kt), 2 * (h % hpd)))
    kp_spec = pl.BlockSpec((blk, 128), lambda h, t, qt, kt: (kt[t], 0))
    v_spec = pl.BlockSpec((blk, 128), lambda h, t, qt, kt: (krow(h, t, qt, kt), 2 * (h % hpd) + 1))
    do_spec = pl.BlockSpec((blk, V_DIM), lambda h, t, qt, kt: (qt[t], h))
    return q_spec, kn_spec, kp_spec, v_spec, do_spec


def _attn_fwd(q, kvu, kpe, H):
    S = kpe.shape[0]
    hpd = H // N_DEV
    blk = _tile(S, ATT_BLOCK)
    nb = S // blk
    q_tab, k_tab = _block_pairs(nb, by_key=False)

    def body(qt, kt, q_ref, kn_ref, kp_ref, v_ref, o_ref, lse_ref, m_ref, acc_ref):
        t = pl.program_id(1)
        qi, ki = qt[t], kt[t]

        @pl.when(ki == 0)
        def _():
            m_ref[...] = jnp.full_like(m_ref, -jnp.inf)
            acc_ref[...] = jnp.zeros_like(acc_ref)

        def step(masked):
            k = jnp.concatenate([kn_ref[...], kp_ref[...]], axis=1)
            s = lax.dot_general(q_ref[...], k, _DIMS["nt"], preferred_element_type=F32)
            if masked:
                s = jnp.where(_causal_mask(blk), s, -jnp.inf)
            m_old = m_ref[...]
            m_new = jnp.maximum(m_old, jnp.max(s, axis=-1, keepdims=True))
            p = jnp.exp2((s - m_new) * EXP2_SCALE).astype(BF16)
            v_ext = jnp.concatenate([v_ref[...], jnp.ones((blk, 128), BF16)], axis=1)
            acc_ref[...] = (jnp.exp2((m_old - m_new) * EXP2_SCALE) * acc_ref[...]
                            + jnp.dot(p, v_ext, preferred_element_type=F32))
            m_ref[...] = m_new

        @pl.when(ki < qi)
        def _():
            step(False)

        @pl.when(ki == qi)
        def _():
            step(True)
            l = acc_ref[:, V_DIM:]
            o_ref[...] = (acc_ref[:, :V_DIM] / l).astype(o_ref.dtype)
            lse_ref[0] = m_ref[...] * EXP2_SCALE + jnp.log2(l[:, :1])

    q_spec, kn_spec, kp_spec, v_spec, do_spec = _attn_specs(S, blk, hpd, by_key=False)
    grid_spec = pltpu.PrefetchScalarGridSpec(
        num_scalar_prefetch=2, grid=(H, q_tab.shape[0]),
        in_specs=[q_spec, kn_spec, kp_spec, v_spec],
        out_specs=(do_spec, pl.BlockSpec((1, blk, 1), lambda h, t, qt, kt: (h, qt[t], 0))),
        scratch_shapes=[pltpu.VMEM((blk, 1), F32), pltpu.VMEM((blk, 2 * V_DIM), F32)])
    return _pcall(
        body, name="attn_fwd", grid_spec=grid_spec,
        out_shape=(jax.ShapeDtypeStruct((S, H * V_DIM), BF16), jax.ShapeDtypeStruct((H, S, 1), F32)),
        compiler_params=_params(("parallel", "arbitrary")),
    )(q_tab, k_tab, q, kvu, kpe, kvu)


def _attn_delta(do, o, H):
    S = do.shape[0]
    ts = _tile(S, ROW_TILE)

    def body(do_ref, o_ref, d_ref):
        d_ref[0] = jnp.sum(do_ref[...].astype(F32) * o_ref[...].astype(F32), axis=-1, keepdims=True)

    blk = pl.BlockSpec((ts, V_DIM), lambda i, h: (i, h))
    return _pcall(
        body, name="attn_delta", grid=(S // ts, H),
        in_specs=[blk, blk], out_specs=pl.BlockSpec((1, ts, 1), lambda i, h: (h, i, 0)),
        out_shape=jax.ShapeDtypeStruct((H, S, 1), F32),
        compiler_params=_params(("parallel", "parallel")),
    )(do, o)


def _attn_dq(q, kvu, kpe, do, lse, delta, cos_t, sin_t, H):
    S = kpe.shape[0]
    hpd = H // N_DEV
    blk = _tile(S, ATT_BLOCK)
    nb = S // blk
    q_tab, k_tab = _block_pairs(nb, by_key=False)

    def body(qt, kt, q_ref, kn_ref, kp_ref, v_ref, do_ref, lse_ref, dl_ref, c_ref, s_ref, dq_ref, acc_ref):
        t = pl.program_id(1)
        qi, ki = qt[t], kt[t]

        @pl.when(ki == 0)
        def _():
            acc_ref[...] = jnp.zeros_like(acc_ref)

        def step(masked):
            k = jnp.concatenate([kn_ref[...], kp_ref[...]], axis=1)
            s = lax.dot_general(q_ref[...], k, _DIMS["nt"], preferred_element_type=F32)
            p = jnp.exp2(s * EXP2_SCALE - lse_ref[0])
            if masked:
                p = jnp.where(_causal_mask(blk), p, 0.0)
            dp = lax.dot_general(do_ref[...], v_ref[...], _DIMS["nt"], preferred_element_type=F32)
            ds = (p * (dp - dl_ref[0])).astype(BF16)
            acc_ref[...] += jnp.dot(ds, k, preferred_element_type=F32)

        @pl.when(ki < qi)
        def _():
            step(False)

        @pl.when(ki == qi)
        def _():
            step(True)
            dq_ref[:, :QK_NOPE_DIM] = (acc_ref[:, :QK_NOPE_DIM] * ATTN_SCALE).astype(dq_ref.dtype)
            dq_ref[:, QK_NOPE_DIM:] = _rope(acc_ref[:, QK_NOPE_DIM:] * ATTN_SCALE, c_ref[...],
                                            -s_ref[...]).astype(dq_ref.dtype)

    q_spec, kn_spec, kp_spec, v_spec, do_spec = _attn_specs(S, blk, hpd, by_key=False)
    col = pl.BlockSpec((1, blk, 1), lambda h, t, qt, kt: (h, qt[t], 0))
    tab = pl.BlockSpec((blk, 128), lambda h, t, qt, kt: (qt[t], 0))
    grid_spec = pltpu.PrefetchScalarGridSpec(
        num_scalar_prefetch=2, grid=(H, q_tab.shape[0]),
        in_specs=[q_spec, kn_spec, kp_spec, v_spec, do_spec, col, col, tab, tab],
        out_specs=q_spec,
        scratch_shapes=[pltpu.VMEM((blk, HEAD_PAD), F32)])
    return _pcall(
        body, name="attn_dq", grid_spec=grid_spec,
        out_shape=jax.ShapeDtypeStruct(q.shape, BF16),
        compiler_params=_params(("parallel", "arbitrary")),
    )(q_tab, k_tab, q, kvu, kpe, kvu, do, lse, delta, cos_t, sin_t)


def _attn_dkv(q, kvu, kpe, do, lse_row, delta_row, H):
    S = kpe.shape[0]
    hpd = H // N_DEV
    blk = _tile(S, ATT_BLOCK)
    nb = S // blk
    q_tab, k_tab = _block_pairs(nb, by_key=True)

    def body(qt, kt, q_ref, kn_ref, kp_ref, v_ref, do_ref, lse_ref, dl_ref, dkv_ref, dkp_ref, dk_acc, dv_acc):
        t = pl.program_id(1)
        qi, ki = qt[t], kt[t]

        def step(masked):
            k = jnp.concatenate([kn_ref[...], kp_ref[...]], axis=1)
            qv, dov = q_ref[...], do_ref[...]
            st = lax.dot_general(k, qv, _DIMS["nt"], preferred_element_type=F32)
            pt = jnp.exp2(st * EXP2_SCALE - lse_ref[0])
            if masked:
                pt = jnp.where(_causal_mask(blk, transposed=True), pt, 0.0)
            dv_new = jnp.dot(pt.astype(BF16), dov, preferred_element_type=F32)
            dpt = lax.dot_general(v_ref[...], dov, _DIMS["nt"], preferred_element_type=F32)
            dst = (pt * (dpt - dl_ref[0])).astype(BF16)
            dk_new = jnp.dot(dst, qv, preferred_element_type=F32)
            if masked:
                dv_acc[...] = dv_new
                dk_acc[...] = dk_new
            else:
                dv_acc[...] += dv_new
                dk_acc[...] += dk_new

        @pl.when(qi == ki)
        def _():
            step(True)

        @pl.when(qi > ki)
        def _():
            step(False)

        @pl.when(qi == nb - 1)
        def _():
            dkv_ref[:, :QK_NOPE_DIM] = (dk_acc[:, :QK_NOPE_DIM] * ATTN_SCALE).astype(dkv_ref.dtype)
            dkv_ref[:, QK_NOPE_DIM:] = dv_acc[...].astype(dkv_ref.dtype)
            dkp_ref[0] = dk_acc[:, QK_NOPE_DIM:] * ATTN_SCALE

    q_spec, kn_spec, kp_spec, v_spec, do_spec = _attn_specs(S, blk, hpd, by_key=True)
    row = pl.BlockSpec((1, 1, blk), lambda h, t, qt, kt: (h, 0, qt[t]))
    grid_spec = pltpu.PrefetchScalarGridSpec(
        num_scalar_prefetch=2, grid=(H, q_tab.shape[0]),
        in_specs=[q_spec, kn_spec, kp_spec, v_spec, do_spec, row, row],
        out_specs=(pl.BlockSpec((blk, HEAD_PAD), lambda h, t, qt, kt: ((h // hpd) * nb + kt[t], h % hpd)),
                   pl.BlockSpec((1, blk, 128), lambda h, t, qt, kt: (h, kt[t], 0))),
        scratch_shapes=[pltpu.VMEM((blk, HEAD_PAD), F32), pltpu.VMEM((blk, V_DIM), F32)])
    return _pcall(
        body, name="attn_dkv", grid_spec=grid_spec,
        out_shape=(jax.ShapeDtypeStruct(kvu.shape, BF16), jax.ShapeDtypeStruct((H, S, 128), F32)),
        compiler_params=_params(("parallel", "arbitrary")),
    )(q_tab, k_tab, q, kvu, kpe, kvu, do, lse_row, delta_row)


def _mesh_pos():
    return lax.axis_index("x"), lax.axis_index("y"), lax.axis_index("c")


def _flip(pos, k):
    x, y, c = pos
    return (1 - x if k & 4 else x, 1 - y if k & 2 else y, 1 - c if k & 1 else c)


def _rank(pos):
    return 4 * pos[0] + 2 * pos[1] + pos[2]


def _exchange(n, src_of, dst_refs, local_src_of, send_sems, recv_sems, local_sems):
    me = _mesh_pos()
    started = []
    for a in range(n):
        local = pltpu.make_async_copy(local_src_of(a), dst_refs[a].at[_rank(me)], local_sems.at[a])
        local.start()
        started.append(local)
    sends = []
    for k in range(1, N_DEV):
        peer = _flip(me, k)
        for a in range(n):
            idx = a * (N_DEV - 1) + k - 1
            cp = pltpu.make_async_remote_copy(
                src_ref=src_of(a, peer), dst_ref=dst_refs[a].at[_rank(me)],
                send_sem=send_sems.at[idx], recv_sem=recv_sems.at[idx],
                device_id=peer, device_id_type=pl.DeviceIdType.MESH)
            cp.start()
            sends.append(cp)
    for k in range(1, N_DEV):
        peer = _flip(me, k)
        for a in range(n):
            idx = a * (N_DEV - 1) + k - 1
            pltpu.make_async_remote_copy(
                src_ref=src_of(a, peer), dst_ref=dst_refs[a].at[_rank(peer)],
                send_sem=send_sems.at[idx], recv_sem=recv_sems.at[idx],
                device_id=peer, device_id_type=pl.DeviceIdType.MESH).wait_recv()
    for cp in sends:
        cp.wait_send()
    for local in started:
        local.wait()


def _sems(n):
    return [pltpu.SemaphoreType.DMA((n * (N_DEV - 1),)), pltpu.SemaphoreType.DMA((n * (N_DEV - 1),)),
            pltpu.SemaphoreType.DMA((n,))]


_ANY = pl.BlockSpec(memory_space=pl.ANY)


def _all_gather(shards, name):
    n = len(shards)

    def body(*refs):
        x_refs, out_refs = refs[:n], refs[n:2 * n]
        _exchange(n, lambda a, peer: x_refs[a], out_refs, lambda a: x_refs[a], *refs[2 * n:])

    return _pcall(
        body, name=name, in_specs=[_ANY] * n, out_specs=tuple([_ANY] * n),
        out_shape=tuple(jax.ShapeDtypeStruct((N_DEV,) + s.shape, s.dtype) for s in shards),
        scratch_shapes=_sems(n),
    )(*shards)


def _all_to_all(parts, name):
    n = len(parts)

    def body(*refs):
        x_refs, out_refs = refs[:n], refs[n:2 * n]
        me = _mesh_pos()
        _exchange(n, lambda a, peer: x_refs[a].at[_rank(peer)], out_refs, lambda a: x_refs[a].at[_rank(me)],
                  *refs[2 * n:])

    return _pcall(
        body, name=name, in_specs=[_ANY] * n, out_specs=tuple([_ANY] * n),
        out_shape=tuple(jax.ShapeDtypeStruct(p.shape, p.dtype) for p in parts),
        scratch_shapes=_sems(n),
    )(*parts)


def _all_reduce_small(v):
    R, C = v.shape

    def body(x_ref, out_ref, buf_ref, send_sems, recv_sems, local_sems):
        _exchange(1, lambda a, peer: x_ref, [buf_ref], lambda a: x_ref, send_sems, recv_sems, local_sems)
        total = buf_ref[0]
        for j in range(1, N_DEV):
            total = total + buf_ref[j]
        out_ref[...] = total

    vm = pl.BlockSpec(memory_space=pltpu.VMEM)
    return _pcall(
        body, name="all_reduce_small", in_specs=[vm], out_specs=vm,
        out_shape=jax.ShapeDtypeStruct((R, C), F32),
        scratch_shapes=[pltpu.VMEM((N_DEV, R, C), F32)] + _sems(1),
    )(v)


def _sum_slots(parts, name):
    shape = parts.shape[1:]
    C = shape[-1]
    R = parts.size // (N_DEV * C)
    tr = R
    while N_DEV * tr * C * 4 > (4 << 20) and tr % 16 == 0:
        tr //= 2

    def body(p_ref, o_ref):
        total = p_ref[0]
        for j in range(1, N_DEV):
            total = total + p_ref[j]
        o_ref[...] = total

    return _pcall(
        body, name=name, grid=(R // tr,),
        in_specs=[pl.BlockSpec((N_DEV, tr, C), lambda i: (0, i, 0))],
        out_specs=pl.BlockSpec((tr, C), lambda i: (i, 0)),
        out_shape=jax.ShapeDtypeStruct((R, C), F32),
        compiler_params=_params(("parallel",)),
    )(parts.reshape(N_DEV, R, C)).reshape(shape)


def _adamw(w, g, m, v, name):
    shape = w.shape
    C = shape[-1]
    R = w.size // C
    tr = R
    while tr * C * 4 > (1 << 20) and tr % 16 == 0:
        tr //= 2

    def body(w_ref, g_ref, m_ref, v_ref, d_ref, nm_ref, nv_ref):
        gv = g_ref[...]
        nm = ADAM_B1 * m_ref[...] + (1.0 - ADAM_B1) * gv
        nv = ADAM_B2 * v_ref[...] + (1.0 - ADAM_B2) * (gv * gv)
        m_hat = nm / (1.0 - ADAM_B1 ** ADAM_STEP)
        v_hat = nv / (1.0 - ADAM_B2 ** ADAM_STEP)
        d_ref[...] = -ADAM_LR * (m_hat / (jnp.sqrt(v_hat) + ADAM_EPS) + ADAM_WD * w_ref[...])
        nm_ref[...] = nm
        nv_ref[...] = nv

    blk = pl.BlockSpec((tr, C), lambda i: (i, 0))
    sds = jax.ShapeDtypeStruct((R, C), F32)
    outs = _pcall(
        body, name=name, grid=(R // tr,),
        in_specs=[blk] * 4, out_specs=(blk,) * 3, out_shape=(sds,) * 3,
        compiler_params=_params(("parallel",)),
    )(*(t.reshape(R, C) for t in (w, g, m, v)))
    return tuple(o.reshape(shape) for o in outs)


_REPLICATED = ("kv_in_norm", "kv_latent_norm", "attn_norm", "q_latent_norm", "ffn_norm", "final_norm")
_WEIGHTS = ("pool_norm", "pool_w", "pool_scale", "kv_in_norm", "w_kv_a", "kv_latent_norm", "w_kv_b", "attn_norm",
            "w_q_a", "q_latent_norm", "w_q_b", "w_o", "ffn_norm", "w_gate", "w_up", "w_down", "final_norm")


def _ffn_fwd(x, norm_g, wg, wu, wd, layer, n_layers):
    S, D = x.shape
    fs = wg.shape[1]
    tm = _tile(S, ROW_TILE)
    nmb = S // tm
    tag = str(layer)
    h = _rmsnorm(x, norm_g, "ffn_norm" + tag)
    g, u, a = _ffn_up(h, wg, wu, layer, n_layers, "ffn_up" + tag)
    y = _mm("ffn_down" + tag, "nn", a, wd, (tm, D, fs), (nmb, 1, N_DEV),
            lambda i, q, j: (j * nmb + i, 0), lambda i, q, j: (j * n_layers + layer, 0), lambda i, q, j: (i, 0),
            (S, D), F32, res=x)
    return y, (h, g, u, a)


def _ffn_bwd(x, norm_g, wg, wu, wd, saved, dy, dyb, layer, n_layers, with_bf16):
    S, D = x.shape
    h, g, u, a = saved
    fs = g.shape[1]
    tm = _tile(S, ROW_TILE)
    nmb = S // tm
    tag = str(layer)
    dg, du = _ffn_dact(dyb, wd, g, u, layer, n_layers, "ffn_dact" + tag)
    dwd = _mm("ffn_dwd" + tag, "tn", a, dyb, (fs, D, tm), (N_DEV, 1, nmb),
              lambda j, q, k: (j * nmb + k, 0), lambda j, q, k: (k, 0), lambda j, q, k: (j, 0), (N_DEV * fs, D))
    dwg = _slots_dw("ffn_dwg" + tag, h, dg)
    dwu = _slots_dw("ffn_dwu" + tag, h, du)
    dh = _mm("ffn_dh" + tag, "nt", dg, wg, (tm, D, fs), (nmb, 1, N_DEV),
             lambda i, q, j: (j * nmb + i, 0), lambda i, q, j: (j * n_layers + layer, 0), lambda i, q, j: (i, 0),
             (S, D), F32, a2=du, b2=wu)
    outs = _rmsnorm_bwd(x, norm_g, dh, dy, "ffn_norm_bwd" + tag, with_bf16=with_bf16)
    return outs, dwg, dwu, dwd


def kernel(x, positions, pool_norm, pool_w, pool_scale, kv_in_norm, w_kv_a, kv_latent_norm, w_kv_b, attn_norm, w_q_a, q_latent_norm, w_q_b, w_o, ffn_norm, w_gate, w_up, w_down, final_norm, loss_target, m_pool_norm, m_pool_w, m_pool_scale, m_kv_in_norm, m_w_kv_a, m_kv_latent_norm, m_w_kv_b, m_attn_norm, m_w_q_a, m_q_latent_norm, m_w_q_b, m_w_o, m_ffn_norm, m_w_gate, m_w_up, m_w_down, m_final_norm, v_pool_norm, v_pool_w, v_pool_scale, v_kv_in_norm, v_w_kv_a, v_kv_latent_norm, v_w_kv_b, v_attn_norm, v_w_q_a, v_q_latent_norm, v_w_q_b, v_w_o, v_ffn_norm, v_w_gate, v_w_up, v_w_down, v_final_norm):
    env = dict(locals())
    weights = {n: env[n] for n in _WEIGHTS}
    m_in = {n: env["m_" + n] for n in _WEIGHTS}
    v_in = {n: env["v_" + n] for n in _WEIGHTS}

    S, D = x.shape[1], x.shape[2]
    xs = x.reshape(S, D)
    target = loss_target.reshape(S, D)
    G, gws, gw = pool_w.shape[1:]
    Ds = w_kv_a.shape[0]
    R_kv, kvw = w_kv_b.shape
    hpd = kvw // (QK_NOPE_DIM + V_DIM)
    H = hpd * N_DEV
    R_q = w_q_a.shape[2]
    n_layers, _, fs = w_gate.shape

    wqb_pad = jnp.pad(w_q_b.reshape(R_q, hpd, QK_DIM), ((0, 0), (0, 0), (0, HEAD_PAD - QK_DIM))).reshape(R_q, hpd * HEAD_PAD)
    shards = [pool_w[0], jnp.pad(w_kv_a, ((0, 0), (0, 128 - QK_ROPE_DIM))), w_kv_b, w_q_a[0], wqb_pad, w_o[0],
              w_gate, w_up, w_down]
    gains = jnp.concatenate([pool_norm, pool_scale], axis=0)
    (g_pool_w, g_wkva, g_wkvb, g_wqa, g_wqb, g_wo, g_wg, g_wu, g_wd, g_gains) = _all_gather(
        [s.astype(BF16) for s in shards] + [gains], "gather_weights")

    wp = jnp.swapaxes(g_pool_w, 0, 1).reshape(G, gw, gw)
    wkva = g_wkva.reshape(D, R_kv + 128)
    wkvb = g_wkvb.reshape(N_DEV * R_kv, kvw)
    wqa = g_wqa.reshape(D, R_q)
    wqb = g_wqb.reshape(N_DEV * R_q, hpd * HEAD_PAD)
    wo = g_wo.reshape(H * V_DIM, D)
    wg = g_wg.reshape(N_DEV * n_layers * D, fs)
    wu = g_wu.reshape(N_DEV * n_layers * D, fs)
    wd = g_wd.reshape(N_DEV * n_layers * fs, D)
    gains_full = jnp.swapaxes(g_gains, 0, 1).reshape(2, D)
    g_pool_norm, g_pool_scale = gains_full[0:1], gains_full[1:2]

    g_kv_in = kv_in_norm.reshape(1, D)
    g_kv_lat = kv_latent_norm.reshape(1, R_kv)
    g_attn = attn_norm.reshape(1, D)
    g_q_lat = q_latent_norm.reshape(1, R_q)
    g_final = final_norm.reshape(1, D)

    half = QK_ROPE_DIM // 2
    inv_freq = ROPE_BASE ** (-jnp.arange(half, dtype=F32) / half)
    ang = positions.reshape(S).astype(F32)[:, None] * inv_freq
    cos, sin = jnp.cos(ang), jnp.sin(ang)
    zeros = jnp.zeros((S, 128 - QK_ROPE_DIM), F32)
    cos_t = jnp.concatenate([cos, cos, zeros], axis=1)
    sin_t = jnp.concatenate([-sin, sin, zeros], axis=1)

    diff = _pool_pre(xs, g_pool_norm)
    x1 = _pool_mix(diff, wp, g_pool_scale, xs)
    x2, ffn0 = _ffn_fwd(x1, ffn_norm[0:1], wg, wu, wd, 0, n_layers)

    hk = _rmsnorm(x2, g_kv_in, "kv_in_norm")
    kv = _mm_plain("kv_a", "nn", hk, wkva)
    ckv, kpe = _kv_mid(kv, g_kv_lat, cos_t, sin_t)
    kvu = _slots_out("kv_b", ckv, wkvb, BF16)

    ha = _rmsnorm(x2, g_attn, "attn_norm")
    ql = _mm_plain("q_a", "nn", ha, wqa)
    qn = _rmsnorm(ql, g_q_lat, "q_latent_norm")
    qr = _q_rope(_slots_out("q_b", qn, wqb, F32), cos_t, sin_t, S)
    o, lse = _attn_fwd(qr, kvu, kpe, H)
    x3 = _mm_plain("attn_out", "nn", o, wo, res=x2)
    x4, ffn1 = _ffn_fwd(x3, ffn_norm[1:2], wg, wu, wd, 1, n_layers)

    loss_part, dx4, dx4b, d_final = _loss_head(x4, g_final, target)

    (dx3, dx3b, d_ffn1), dwg1, dwu1, dwd1 = _ffn_bwd(x3, ffn_norm[1:2], wg, wu, wd, ffn1, dx4, dx4b, 1, n_layers, True)

    do = _mm_plain("attn_out_dx", "nt", dx3b, wo, out_dtype=BF16)
    dwo = _mm_plain("attn_out_dw", "tn", o, dx3b)
    delta = _attn_delta(do, o, H)
    dq = _attn_dq(qr, kvu, kpe, do, lse, delta, cos_t, sin_t, H)
    dkvu, dkpe = _attn_dkv(qr, kvu, kpe, do, lse.reshape(H, 1, S), delta.reshape(H, 1, S), H)

    dqn = _slots_in_nt("q_b_dx", dq, wqb, S)
    dwqb = _slots_dw("q_b_dw", qn, dq)
    dql, d_q_lat = _rmsnorm_bwd(ql, g_q_lat, dqn, None, "q_latent_norm_bwd", out_dtype=BF16)
    dha = _mm_plain("q_a_dx", "nt", dql, wqa)
    dwqa = _mm_plain("q_a_dw", "tn", ha, dql)
    dx2, d_attn = _rmsnorm_bwd(x2, g_attn, dha, dx3, "attn_norm_bwd")

    dckv = _slots_in_nt("kv_b_dx", dkvu, wkvb, S)
    dwkvb = _slots_dw("kv_b_dw", ckv, dkvu)
    dkv, d_kv_lat = _kv_mid_bwd(kv, g_kv_lat, dckv, dkpe, cos_t, sin_t)
    dhk = _mm_plain("kv_a_dx", "nt", dkv, wkva)
    dwkva = _mm_plain("kv_a_dw", "tn", hk, dkv)
    dx2, dx2b, d_kv_in = _rmsnorm_bwd(x2, g_kv_in, dhk, dx2, "kv_in_norm_bwd", with_bf16=True)

    (dx1, d_ffn0), dwg0, dwu0, dwd0 = _ffn_bwd(x1, ffn_norm[0:1], wg, wu, wd, ffn0, dx2, dx2b, 0, n_layers, False)

    ddiff, dwp, d_pool_scale = _pool_mix_bwd(diff, wp, g_pool_scale, dx1)
    grad_x, d_pool_norm = _pool_pre_bwd(xs, g_pool_norm, ddiff, dx1)

    d_gains = jnp.swapaxes(jnp.concatenate([d_pool_norm, d_pool_scale], axis=0).reshape(2, N_DEV, D // N_DEV), 0, 1)
    parts = [jnp.swapaxes(dwp.reshape(G, N_DEV, gws, gw), 0, 1),
             dwkva.reshape(N_DEV, Ds, R_kv + 128), dwkvb.reshape(N_DEV, R_kv, kvw),
             dwqa.reshape(N_DEV, Ds, R_q), dwqb.reshape(N_DEV, R_q, hpd * HEAD_PAD),
             dwo.reshape(N_DEV, H * V_DIM // N_DEV, D),
             dwg0.reshape(N_DEV, D, fs), dwg1.reshape(N_DEV, D, fs), dwu0.reshape(N_DEV, D, fs),
             dwu1.reshape(N_DEV, D, fs), dwd0.reshape(N_DEV, fs, D), dwd1.reshape(N_DEV, fs, D), d_gains]
    names = ["pool_w", "w_kv_a", "w_kv_b", "w_q_a", "w_q_b", "w_o", "w_gate0", "w_gate1", "w_up0", "w_up1",
             "w_down0", "w_down1", "gains"]
    received = _all_to_all(parts, "exchange_grads")
    sums = {n: _sum_slots(r, "sum_" + n) for n, r in zip(names, received)}
    grads = {
        "pool_w": sums["pool_w"][None],
        "w_kv_a": sums["w_kv_a"][:, :R_kv + QK_ROPE_DIM],
        "w_kv_b": sums["w_kv_b"],
        "w_q_a": sums["w_q_a"][None],
        "w_q_b": sums["w_q_b"].reshape(R_q, hpd, HEAD_PAD)[:, :, :QK_DIM].reshape(1, R_q, hpd * QK_DIM),
        "w_o": sums["w_o"][None],
        "w_gate": jnp.stack([sums["w_gate0"], sums["w_gate1"]]),
        "w_up": jnp.stack([sums["w_up0"], sums["w_up1"]]),
        "w_down": jnp.stack([sums["w_down0"], sums["w_down1"]]),
        "pool_norm": sums["gains"][0:1],
        "pool_scale": sums["gains"][1:2],
    }

    small = {"kv_in_norm": d_kv_in, "kv_latent_norm": d_kv_lat, "attn_norm": d_attn, "q_latent_norm": d_q_lat,
             "ffn_norm": jnp.concatenate([d_ffn0, d_ffn1], axis=0), "final_norm": d_final}
    small_packed = jnp.concatenate([small[n].reshape(-1) for n in _REPLICATED] + [loss_part.reshape(-1)])
    pad = (-small_packed.shape[0]) % (8 * 128)
    reduced = _all_reduce_small(jnp.pad(small_packed, (0, pad)).reshape(-1, 128)).reshape(-1)
    off = 0
    for n in _REPLICATED:
        grads[n] = reduced[off:off + weights[n].size].reshape(weights[n].shape)
        off += weights[n].size
    loss = reduced[off]

    delta_w, new_m, new_v = {}, {}, {}
    for n in _WEIGHTS:
        delta_w[n], new_m[n], new_v[n] = _adamw(weights[n], grads[n], m_in[n], v_in[n], "adamw_" + n)

    return (loss, grad_x.reshape(x.shape), *[grads[n] for n in _WEIGHTS], *[delta_w[n] for n in _WEIGHTS],
            *[new_m[n] for n in _WEIGHTS], *[new_v[n] for n in _WEIGHTS])
```

```python
import math

import jax
import jax.numpy as jnp
import numpy as np
from jax import lax
from jax.experimental import pallas as pl
from jax.experimental.pallas import tpu as pltpu

F32 = jnp.float32
BF16 = jnp.bfloat16

N_DEV = 8
POOL_WINDOWS = (2, 4, 8, 16)
POOL_HALO = 16
QK_NOPE_DIM = 128
QK_ROPE_DIM = 64
QK_DIM = QK_NOPE_DIM + QK_ROPE_DIM
HEAD_PAD = 256
V_DIM = 128
ROPE_BASE = 10000.0
ATTN_SCALE = 1.0 / math.sqrt(QK_DIM)
EXP2_SCALE = ATTN_SCALE * math.log2(math.e)
NORM_EPS = 1e-6
ADAM_LR = 0.001
ADAM_B1 = 0.9
ADAM_B2 = 0.999
ADAM_EPS = 1e-08
ADAM_WD = 0.01
ADAM_STEP = 10

ROW_TILE = 512
ATT_BLOCK = 512
MM_TN, MM_TK = 512, 512
VMEM_LIMIT = 48 * 1024 * 1024

_pcall = pl.pallas_call


def _params(sem):
    return pltpu.CompilerParams(dimension_semantics=sem, vmem_limit_bytes=VMEM_LIMIT)


def _tile(dim, default):
    if dim % default == 0:
        return default
    assert dim <= 2 * default, (dim, default)
    return dim


def _row_tile(rows, width):
    ts = _tile(rows, ROW_TILE)
    while ts * width * 4 > (2 << 20) and ts % 16 == 0:
        ts //= 2
    return ts


def _rms_scale(x):
    return lax.rsqrt(jnp.mean(x * x, axis=-1, keepdims=True) + NORM_EPS)


def _rms_bwd(x, g, dy):
    r = _rms_scale(x)
    xn = x * r
    u = dy * g
    dx = r * (u - xn * jnp.mean(u * xn, axis=-1, keepdims=True))
    return dx, dy * xn


def _swap_halves(x):
    lane = lax.broadcasted_iota(jnp.int32, x.shape, 1)
    return jnp.where(lane < QK_ROPE_DIM // 2, pltpu.roll(x, 128 - QK_ROPE_DIM // 2, 1), pltpu.roll(x, QK_ROPE_DIM // 2, 1))


def _rope(x, cos_t, sin_t):
    return x * cos_t + _swap_halves(x) * sin_t


def _rmsnorm(x, g, name):
    S, D = x.shape
    ts = _row_tile(S, D)

    def body(x_ref, g_ref, o_ref):
        xf = x_ref[...]
        o_ref[...] = (xf * _rms_scale(xf) * g_ref[...]).astype(o_ref.dtype)

    return _pcall(
        body, name=name, grid=(S // ts,),
        in_specs=[pl.BlockSpec((ts, D), lambda i: (i, 0)), pl.BlockSpec((1, D), lambda i: (0, 0))],
        out_specs=pl.BlockSpec((ts, D), lambda i: (i, 0)),
        out_shape=jax.ShapeDtypeStruct((S, D), BF16),
        compiler_params=_params(("parallel",)),
    )(x, g)


def _rmsnorm_bwd(x, g, dy, res, name, out_dtype=F32, with_bf16=False):
    S, D = x.shape
    ts = _row_tile(S, D)
    has_res = res is not None

    def body(*refs):
        refs = list(refs)
        x_ref, g_ref, dy_ref = refs[:3]
        res_ref = refs[3] if has_res else None
        outs = refs[3 + has_res:]
        dx_ref, dg_ref = outs[0], outs[-1]
        dx, dgt = _rms_bwd(x_ref[...], g_ref[...], dy_ref[...].astype(F32))
        if has_res:
            dx = res_ref[...] + dx
        dx_ref[...] = dx.astype(dx_ref.dtype)
        if with_bf16:
            outs[1][...] = dx.astype(BF16)

        @pl.when(pl.program_id(0) == 0)
        def _():
            dg_ref[...] = jnp.zeros_like(dg_ref)

        dg_ref[...] += jnp.sum(dgt, axis=0, keepdims=True)

    row = pl.BlockSpec((ts, D), lambda i: (i, 0))
    vec = pl.BlockSpec((1, D), lambda i: (0, 0))
    out_specs = [row] + ([row] if with_bf16 else []) + [vec]
    out_shape = ([jax.ShapeDtypeStruct((S, D), out_dtype)] + ([jax.ShapeDtypeStruct((S, D), BF16)] if with_bf16 else [])
                 + [jax.ShapeDtypeStruct((1, D), F32)])
    return _pcall(
        body, name=name, grid=(S // ts,),
        in_specs=[row, vec, row] + ([row] if has_res else []),
        out_specs=tuple(out_specs), out_shape=tuple(out_shape),
        compiler_params=_params(("arbitrary",)),
    )(*([x, g, dy] + ([res] if has_res else [])))


def _pool_pre(x, g):
    S, D = x.shape
    ts = _row_tile(S, D)
    gw = D // len(POOL_WINDOWS)
    hb = ts // POOL_HALO

    def body(x_ref, halo_ref, g_ref, o_ref):
        i = pl.program_id(0)
        xx = jnp.concatenate([halo_ref[...], x_ref[...]], axis=0)
        h = xx * _rms_scale(xx) * g_ref[...]
        t = i * ts - POOL_HALO + lax.broadcasted_iota(jnp.int32, (ts + POOL_HALO, 1), 0)
        h = jnp.where(t >= 0, h, 0.0)
        tf = t[POOL_HALO:].astype(F32)
        for gi, w in enumerate(POOL_WINDOWS):
            hg = h[:, gi * gw:(gi + 1) * gw]
            acc, span = hg, 1
            while span < w:
                acc = acc + pltpu.roll(acc, span, 0)
                span *= 2
            count = jnp.minimum(tf + 1.0, float(w))
            diff = acc[POOL_HALO:] / count - hg[POOL_HALO:]
            o_ref[:, gi * gw:(gi + 1) * gw] = diff.astype(o_ref.dtype)

    return _pcall(
        body, name="pool_pre", grid=(S // ts,),
        in_specs=[pl.BlockSpec((ts, D), lambda i: (i, 0)),
                  pl.BlockSpec((POOL_HALO, D), lambda i: (jnp.maximum(i * hb - 1, 0), 0)),
                  pl.BlockSpec((1, D), lambda i: (0, 0))],
        out_specs=pl.BlockSpec((ts, D), lambda i: (i, 0)),
        out_shape=jax.ShapeDtypeStruct((S, D), BF16),
        compiler_params=_params(("parallel",)),
    )(x, x, g)


def _pool_pre_bwd(x, g, dd, res):
    S, D = x.shape
    ts = _row_tile(S, D)
    gw = D // len(POOL_WINDOWS)
    hb = ts // POOL_HALO
    last_halo = S // POOL_HALO - 1

    def body(x_ref, g_ref, dd_ref, halo_ref, res_ref, dx_ref, dg_ref):
        i = pl.program_id(0)
        ee = jnp.concatenate([dd_ref[...], halo_ref[...]], axis=0)
        t = i * ts + lax.broadcasted_iota(jnp.int32, (ts + POOL_HALO, 1), 0)
        ee = jnp.where(t < S, ee, 0.0)
        tf = t.astype(F32)
        n = ts + POOL_HALO
        for gi, w in enumerate(POOL_WINDOWS):
            eg = ee[:, gi * gw:(gi + 1) * gw]
            acc, span = eg / jnp.minimum(tf + 1.0, float(w)), 1
            while span < w:
                acc = acc + pltpu.roll(acc, n - span, 0)
                span *= 2
            dx_ref[:, gi * gw:(gi + 1) * gw] = acc[:ts] - eg[:ts]
        dx, dgt = _rms_bwd(x_ref[...], g_ref[...], dx_ref[...])
        dx_ref[...] = res_ref[...] + dx

        @pl.when(i == 0)
        def _():
            dg_ref[...] = jnp.zeros_like(dg_ref)

        dg_ref[...] += jnp.sum(dgt, axis=0, keepdims=True)

    row = pl.BlockSpec((ts, D), lambda i: (i, 0))
    vec = pl.BlockSpec((1, D), lambda i: (0, 0))
    return _pcall(
        body, name="pool_pre_bwd", grid=(S // ts,),
        in_specs=[row, vec, row,
                  pl.BlockSpec((POOL_HALO, D), lambda i: (jnp.minimum((i + 1) * hb, last_halo), 0)), row],
        out_specs=(row, vec),
        out_shape=(jax.ShapeDtypeStruct((S, D), F32), jax.ShapeDtypeStruct((1, D), F32)),
        compiler_params=_params(("arbitrary",)),
    )(x, g, dd, dd, res)


def _pool_mix(diff, w, scale, x):
    S, D = x.shape
    G, gw, _ = w.shape
    ts = _tile(S, ROW_TILE)

    def body(d_ref, w_ref, s_ref, x_ref, o_ref):
        mo = jnp.dot(d_ref[...], w_ref[0], preferred_element_type=F32)
        o_ref[...] = x_ref[...] + mo * s_ref[...]

    blk = pl.BlockSpec((ts, gw), lambda i, g: (i, g))
    return _pcall(
        body, name="pool_mix", grid=(S // ts, G),
        in_specs=[blk, pl.BlockSpec((1, gw, gw), lambda i, g: (g, 0, 0)), pl.BlockSpec((1, gw), lambda i, g: (0, g)), blk],
        out_specs=blk,
        out_shape=jax.ShapeDtypeStruct((S, D), F32),
        compiler_params=_params(("parallel", "parallel")),
    )(diff, w, scale, x)


def _pool_mix_bwd(diff, w, scale, dy):
    S, D = dy.shape
    G, gw, _ = w.shape
    ts = _tile(S, ROW_TILE)

    def body(d_ref, w_ref, s_ref, dy_ref, dd_ref, dw_ref, ds_ref):
        i = pl.program_id(1)
        d = d_ref[...]
        dy = dy_ref[...]
        mo = jnp.dot(d, w_ref[0], preferred_element_type=F32)
        dmo = (dy * s_ref[...]).astype(BF16)
        dd_ref[...] = lax.dot_general(dmo, w_ref[0], (((1,), (1,)), ((), ())), preferred_element_type=F32)

        @pl.when(i == 0)
        def _():
            dw_ref[...] = jnp.zeros_like(dw_ref)
            ds_ref[...] = jnp.zeros_like(ds_ref)

        dw_ref[0] += lax.dot_general(d, dmo, (((0,), (0,)), ((), ())), preferred_element_type=F32)
        ds_ref[...] += jnp.sum(dy * mo, axis=0, keepdims=True)

    blk = pl.BlockSpec((ts, gw), lambda g, i: (i, g))
    wspec = pl.BlockSpec((1, gw, gw), lambda g, i: (g, 0, 0))
    vec = pl.BlockSpec((1, gw), lambda g, i: (0, g))
    return _pcall(
        body, name="pool_mix_bwd", grid=(G, S // ts),
        in_specs=[blk, wspec, vec, blk],
        out_specs=(blk, wspec, vec),
        out_shape=(jax.ShapeDtypeStruct((S, D), F32), jax.ShapeDtypeStruct((G, gw, gw), F32),
                   jax.ShapeDtypeStruct((1, D), F32)),
        compiler_params=_params(("parallel", "arbitrary")),
    )(diff, w, scale, dy)


def _loss_head(x, g, target):
    S, D = x.shape
    ts = _row_tile(S, D)

    def body(x_ref, g_ref, t_ref, loss_ref, dx_ref, dxb_ref, dg_ref):
        xf = x_ref[...]
        gv = g_ref[...]
        err = xf * _rms_scale(xf) * gv - t_ref[...]
        dx, dgt = _rms_bwd(xf, gv, err * (1.0 / D))
        dx_ref[...] = dx
        dxb_ref[...] = dx.astype(BF16)

        @pl.when(pl.program_id(0) == 0)
        def _():
            loss_ref[...] = jnp.zeros_like(loss_ref)
            dg_ref[...] = jnp.zeros_like(dg_ref)

        part = 0.5 * jnp.sum(jnp.sum(err * err, axis=-1, keepdims=True) * (1.0 / D), axis=0, keepdims=True)
        loss_ref[...] += jnp.broadcast_to(part, loss_ref.shape)
        dg_ref[...] += jnp.sum(dgt, axis=0, keepdims=True)

    row = pl.BlockSpec((ts, D), lambda i: (i, 0))
    vec = pl.BlockSpec((1, D), lambda i: (0, 0))
    return _pcall(
        body, name="loss_head", grid=(S // ts,),
        in_specs=[row, vec, row],
        out_specs=(pl.BlockSpec((1, 128), lambda i: (0, 0)), row, row, vec),
        out_shape=(jax.ShapeDtypeStruct((1, 128), F32), jax.ShapeDtypeStruct((S, D), F32),
                   jax.ShapeDtypeStruct((S, D), BF16), jax.ShapeDtypeStruct((1, D), F32)),
        compiler_params=_params(("arbitrary",)),
    )(x, g, target)


_DIMS = {"nn": (((1,), (0,)), ((), ())), "nt": (((1,), (1,)), ((), ())), "tn": (((0,), (0,)), ((), ()))}


def _mm(name, mode, a, b, tiles, grid, a_map, b_map, o_map, out_shape, out_dtype=F32, res=None, a2=None, b2=None,
        ride=None):
    tm, tn, tk = tiles
    nk = grid[-1]
    dual, has_res = a2 is not None, res is not None
    dn = _DIMS[mode]

    def body(ins, outs, scratch):
        o_ref = outs[0]

        def product():
            part = lax.dot_general(ins[0][...].astype(BF16), ins[1][...].astype(BF16), dn, preferred_element_type=F32)
            if dual:
                part += lax.dot_general(ins[2][...].astype(BF16), ins[3][...].astype(BF16), dn,
                                        preferred_element_type=F32)
            return part

        def finish(out):
            if has_res:
                out = ins[-1][...] + out
            o_ref[...] = out.astype(o_ref.dtype)

        if nk == 1:
            finish(product())
            return
        acc_ref = scratch[0]
        k = pl.program_id(2)

        @pl.when(k == 0)
        def _():
            acc_ref[...] = jnp.zeros_like(acc_ref)

        acc_ref[...] += product()

        @pl.when(k == nk - 1)
        def _():
            finish(acc_ref[...])

    a_spec = pl.BlockSpec((tk, tm) if mode == "tn" else (tm, tk), a_map)
    b_spec = pl.BlockSpec((tn, tk) if mode == "nt" else (tk, tn), b_map)
    o_spec = pl.BlockSpec((tm, tn), o_map)
    operands, in_specs = [a, b], [a_spec, b_spec]
    if dual:
        operands += [a2, b2]
        in_specs += [a_spec, b_spec]
    if has_res:
        operands.append(res)
        in_specs.append(o_spec)
    outs = _pallas(body, name, grid, operands, in_specs, [o_spec], [jax.ShapeDtypeStruct(out_shape, out_dtype)],
                   scratch=[pltpu.VMEM((tm, tn), F32)] if nk > 1 else [],
                   sem=("parallel", "parallel", "arbitrary"), ride=ride)
    return (outs[0], outs[1:]) if ride else outs[0]


def _mm_plain(name, mode, a, b, out_dtype=F32, res=None):
    if mode == "tn":
        (K, M), N = a.shape, b.shape[1]
    elif mode == "nt":
        (M, K), N = a.shape, b.shape[0]
    else:
        (M, K), N = a.shape, b.shape[1]
    tm, tn, tk = _tile(M, ROW_TILE), _tile(N, MM_TN), _tile(K, MM_TK)
    a_map = (lambda i, j, k: (k, i)) if mode == "tn" else (lambda i, j, k: (i, k))
    b_map = (lambda i, j, k: (j, k)) if mode == "nt" else (lambda i, j, k: (k, j))
    return _mm(name, mode, a, b, (tm, tn, tk), (M // tm, N // tn, K // tk), a_map, b_map, lambda i, j, k: (i, j),
               (M, N), out_dtype, res)


def _slots_out(name, a, w_slots, out_dtype):
    S, K = a.shape
    n = w_slots.shape[1]
    tm = _tile(S, ROW_TILE)
    nmb = S // tm
    return _mm(name, "nn", a, w_slots, (tm, n, K), (nmb, N_DEV, 1),
               lambda i, j, k: (i, 0), lambda i, j, k: (j, 0), lambda i, j, k: (j * nmb + i, 0),
               (N_DEV * S, n), out_dtype)


def _slots_in_nt(name, a_slots, w_slots, S, a2=None, w2=None, ride=None):
    n = a_slots.shape[1]
    K = w_slots.shape[0] // N_DEV
    tm = _tile(S, ROW_TILE)
    nmb = S // tm
    return _mm(name, "nt", a_slots, w_slots, (tm, K, n), (nmb, 1, N_DEV),
               lambda i, q, j: (j * nmb + i, 0), lambda i, q, j: (j, 0), lambda i, q, j: (i, 0),
               (S, K), F32, a2=a2, b2=w2, ride=ride)


def _slots_dw(name, a, d_slots, out_dtype=F32, ride=None):
    S, K = a.shape
    n = d_slots.shape[1]
    tk = _tile(S, ROW_TILE)
    nkb = S // tk
    return _mm(name, "tn", a, d_slots, (K, n, tk), (N_DEV, 1, nkb),
               lambda j, q, k: (k, 0), lambda j, q, k: (j * nkb + k, 0), lambda j, q, k: (j, 0),
               (N_DEV * K, n), out_dtype, ride=ride)


def _sigmoid(x):
    return 1.0 / (1.0 + jnp.exp(-x))


def _ffn_up(h, wg, wu, name, ride=None):
    S, D = h.shape
    fs = wg.shape[1]
    tm = _tile(S, ROW_TILE)
    nmb = S // tm

    def body(ins, outs, scratch):
        h_ref, wg_ref, wu_ref = ins
        g_ref, u_ref, a_ref = outs
        hv = h_ref[...]
        g = jnp.dot(hv, wg_ref[...], preferred_element_type=F32)
        u = jnp.dot(hv, wu_ref[...], preferred_element_type=F32)
        g_ref[...] = g
        u_ref[...] = u
        a_ref[...] = (g * _sigmoid(g) * u).astype(a_ref.dtype)

    w_spec = pl.BlockSpec((D, fs), lambda i, j: (j, 0))
    o_spec = pl.BlockSpec((tm, fs), lambda i, j: (j * nmb + i, 0))
    sds = jax.ShapeDtypeStruct((N_DEV * S, fs), F32)
    return _pallas(body, name, (nmb, N_DEV), [h, wg, wu],
                   [pl.BlockSpec((tm, D), lambda i, j: (i, 0)), w_spec, w_spec], [o_spec, o_spec, o_spec],
                   [sds, sds, jax.ShapeDtypeStruct((N_DEV * S, fs), BF16)], sem=("parallel", "arbitrary"), ride=ride)


def _ffn_dact(dy, wd, g, u, name, ride=None):
    S, D = dy.shape
    fs = g.shape[1]
    tm = _tile(S, ROW_TILE)
    nmb = S // tm

    def body(ins, outs, scratch):
        dy_ref, wd_ref, g_ref, u_ref = ins
        dg_ref, du_ref = outs
        da = lax.dot_general(dy_ref[...], wd_ref[...], _DIMS["nt"], preferred_element_type=F32)
        g, u = g_ref[...], u_ref[...]
        sig = _sigmoid(g)
        dg_ref[...] = (da * u * (sig * (1.0 + g * (1.0 - sig)))).astype(dg_ref.dtype)
        du_ref[...] = (da * (g * sig)).astype(du_ref.dtype)

    o_spec = pl.BlockSpec((tm, fs), lambda i, j: (j * nmb + i, 0))
    sds = jax.ShapeDtypeStruct((N_DEV * S, fs), BF16)
    return _pallas(body, name, (nmb, N_DEV), [dy, wd, g, u],
                   [pl.BlockSpec((tm, D), lambda i, j: (i, 0)), pl.BlockSpec((fs, D), lambda i, j: (j, 0)),
                    o_spec, o_spec], [o_spec, o_spec], [sds, sds], sem=("parallel", "arbitrary"), ride=ride)


def _q_rope(q, cos_t, sin_t, S):
    rows, W = q.shape
    ts = _tile(S, ROW_TILE)
    nsb = S // ts

    def body(q_ref, c_ref, s_ref, o_ref):
        o_ref[:, :QK_NOPE_DIM] = q_ref[:, :QK_NOPE_DIM].astype(o_ref.dtype)
        o_ref[:, QK_NOPE_DIM:] = _rope(q_ref[:, QK_NOPE_DIM:], c_ref[...], s_ref[...]).astype(o_ref.dtype)

    blk = pl.BlockSpec((ts, HEAD_PAD), lambda i, h: (i, h))
    tab = pl.BlockSpec((ts, 128), lambda i, h: (i % nsb, 0))
    return _pcall(
        body, name="q_rope", grid=(rows // ts, W // HEAD_PAD),
        in_specs=[blk, tab, tab], out_specs=blk,
        out_shape=jax.ShapeDtypeStruct((rows, W), BF16),
        compiler_params=_params(("parallel", "parallel")),
    )(q, cos_t, sin_t)


def _kv_mid(kv, g, cos_t, sin_t):
    S, W = kv.shape
    R = W - 128
    ts = _tile(S, ROW_TILE)

    def body(kv_ref, g_ref, c_ref, s_ref, c_out, k_out):
        cf = kv_ref[:, :R]
        c_out[...] = (cf * _rms_scale(cf) * g_ref[...]).astype(c_out.dtype)
        k_out[...] = _rope(kv_ref[:, R:], c_ref[...], s_ref[...]).astype(k_out.dtype)

    tab = pl.BlockSpec((ts, 128), lambda i: (i, 0))
    return _pcall(
        body, name="kv_mid", grid=(S // ts,),
        in_specs=[pl.BlockSpec((ts, W), lambda i: (i, 0)), pl.BlockSpec((1, R), lambda i: (0, 0)), tab, tab],
        out_specs=(pl.BlockSpec((ts, R), lambda i: (i, 0)), tab),
        out_shape=(jax.ShapeDtypeStruct((S, R), BF16), jax.ShapeDtypeStruct((S, 128), BF16)),
        compiler_params=_params(("parallel",)),
    )(kv, g, cos_t, sin_t)


def _kv_mid_bwd(kv, g, dc, dkpe, cos_t, sin_t):
    S, W = kv.shape
    R = W - 128
    H = dkpe.shape[0]
    ts = _row_tile(S, H * 128)

    def body(kv_ref, g_ref, dc_ref, dk_ref, c_ref, s_ref, dkv_ref, dg_ref):
        dx, dgt = _rms_bwd(kv_ref[:, :R], g_ref[...], dc_ref[...])
        dkv_ref[:, :R] = dx.astype(dkv_ref.dtype)
        dk = dk_ref[0]
        for h in range(1, H):
            dk = dk + dk_ref[h]
        dkv_ref[:, R:] = _rope(dk, c_ref[...], -s_ref[...]).astype(dkv_ref.dtype)

        @pl.when(pl.program_id(0) == 0)
        def _():
            dg_ref[...] = jnp.zeros_like(dg_ref)

        dg_ref[...] += jnp.sum(dgt, axis=0, keepdims=True)

    tab = pl.BlockSpec((ts, 128), lambda i: (i, 0))
    vec = pl.BlockSpec((1, R), lambda i: (0, 0))
    return _pcall(
        body, name="kv_mid_bwd", grid=(S // ts,),
        in_specs=[pl.BlockSpec((ts, W), lambda i: (i, 0)), vec, pl.BlockSpec((ts, R), lambda i: (i, 0)),
                  pl.BlockSpec((H, ts, 128), lambda i: (0, i, 0)), tab, tab],
        out_specs=(pl.BlockSpec((ts, W), lambda i: (i, 0)), vec),
        out_shape=(jax.ShapeDtypeStruct((S, W), BF16), jax.ShapeDtypeStruct((1, R), F32)),
        compiler_params=_params(("arbitrary",)),
    )(kv, g, dc, dkpe, cos_t, sin_t)


def _block_pairs(nb, by_key):
    pairs = ([(qi, ki) for ki in range(nb) for qi in range(ki, nb)] if by_key
             else [(qi, ki) for qi in range(nb) for ki in range(qi + 1)])
    return jnp.asarray(np.array([p[0] for p in pairs], np.int32)), jnp.asarray(np.array([p[1] for p in pairs], np.int32))


def _causal_mask(blk, transposed=False):
    r = lax.broadcasted_iota(jnp.int32, (blk, blk), 0)
    c = lax.broadcasted_iota(jnp.int32, (blk, blk), 1)
    return (r <= c) if transposed else (c <= r)


def _attn_specs(S, blk, hpd, by_key):
    nb = S // blk
    if by_key:
        qrow = lambda h, t, qt, kt: (h // hpd) * nb + qt[t]
        krow = lambda h, t, qt, kt: (h // hpd) * nb + kt[t]
    else:
        qrow = lambda h, t, qt, kt: (h // hpd) * nb + qt[t]
        krow = lambda h, t, qt, kt: (h // hpd) * nb + kt[t]
    q_spec = pl.BlockSpec((blk, HEAD_PAD), lambda h, t, qt, kt: (qrow(h, t, qt, kt), h % hpd))
    kn_spec = pl.BlockSpec((blk, 128), lambda h, t, qt, kt: (krow(h, t, qt, kt), 2 * (h % hpd)))
    kp_spec = pl.BlockSpec((blk, 128), lambda h, t, qt, kt: (kt[t], 0))
    v_spec = pl.BlockSpec((blk, 128), lambda h, t, qt, kt: (krow(h, t, qt, kt), 2 * (h % hpd) + 1))
    do_spec = pl.BlockSpec((blk, V_DIM), lambda h, t, qt, kt: (qt[t], h))
    return q_spec, kn_spec, kp_spec, v_spec, do_spec


def _attn_fwd(q, kvu, kpe, H, ride=None):
    S = kpe.shape[0]
    hpd = H // N_DEV
    blk = _tile(S, ATT_BLOCK)
    nb = S // blk
    q_tab, k_tab = _block_pairs(nb, by_key=False)

    def body(ins, outs, scratch):
        qt, kt, q_ref, kn_ref, kp_ref, v_ref = ins
        o_ref, lse_ref = outs
        m_ref, acc_ref = scratch
        t = pl.program_id(1)
        qi, ki = qt[t], kt[t]

        @pl.when(ki == 0)
        def _():
            m_ref[...] = jnp.full_like(m_ref, -jnp.inf)
            acc_ref[...] = jnp.zeros_like(acc_ref)

        def step(masked):
            k = jnp.concatenate([kn_ref[...], kp_ref[...]], axis=1)
            s = lax.dot_general(q_ref[...], k, _DIMS["nt"], preferred_element_type=F32)
            if masked:
                s = jnp.where(_causal_mask(blk), s, -jnp.inf)
            m_old = m_ref[...]
            m_new = jnp.maximum(m_old, jnp.max(s, axis=-1, keepdims=True))
            p = jnp.exp2((s - m_new) * EXP2_SCALE).astype(BF16)
            v_ext = jnp.concatenate([v_ref[...], jnp.ones((blk, 128), BF16)], axis=1)
            acc_ref[...] = (jnp.exp2((m_old - m_new) * EXP2_SCALE) * acc_ref[...]
                            + jnp.dot(p, v_ext, preferred_element_type=F32))
            m_ref[...] = m_new

        @pl.when(ki < qi)
        def _():
            step(False)

        @pl.when(ki == qi)
        def _():
            step(True)
            l = acc_ref[:, V_DIM:]
            o_ref[...] = (acc_ref[:, :V_DIM] / l).astype(o_ref.dtype)
            lse_ref[0] = m_ref[...] * EXP2_SCALE + jnp.log2(l[:, :1])

    q_spec, kn_spec, kp_spec, v_spec, do_spec = _attn_specs(S, blk, hpd, by_key=False)
    return _pallas(body, "attn_fwd", (H, q_tab.shape[0]), [q, kvu, kpe, kvu], [q_spec, kn_spec, kp_spec, v_spec],
                   [do_spec, pl.BlockSpec((1, blk, 1), lambda h, t, qt, kt: (h, qt[t], 0))],
                   [jax.ShapeDtypeStruct((S, H * V_DIM), BF16), jax.ShapeDtypeStruct((H, S, 1), F32)],
                   scratch=[pltpu.VMEM((blk, 1), F32), pltpu.VMEM((blk, 2 * V_DIM), F32)],
                   sem=("parallel", "arbitrary"), ride=ride, prefetch=[q_tab, k_tab])


def _attn_delta(do, o, H):
    S = do.shape[0]
    ts = _tile(S, ROW_TILE)

    def body(do_ref, o_ref, d_ref):
        d_ref[0] = jnp.sum(do_ref[...].astype(F32) * o_ref[...].astype(F32), axis=-1, keepdims=True)

    blk = pl.BlockSpec((ts, V_DIM), lambda i, h: (i, h))
    return _pcall(
        body, name="attn_delta", grid=(S // ts, H),
        in_specs=[blk, blk], out_specs=pl.BlockSpec((1, ts, 1), lambda i, h: (h, i, 0)),
        out_shape=jax.ShapeDtypeStruct((H, S, 1), F32),
        compiler_params=_params(("parallel", "parallel")),
    )(do, o)


def _attn_dq(q, kvu, kpe, do, lse, delta, cos_t, sin_t, H):
    S = kpe.shape[0]
    hpd = H // N_DEV
    blk = _tile(S, ATT_BLOCK)
    nb = S // blk
    q_tab, k_tab = _block_pairs(nb, by_key=False)

    def body(qt, kt, q_ref, kn_ref, kp_ref, v_ref, do_ref, lse_ref, dl_ref, c_ref, s_ref, dq_ref, acc_ref):
        t = pl.program_id(1)
        qi, ki = qt[t], kt[t]

        @pl.when(ki == 0)
        def _():
            acc_ref[...] = jnp.zeros_like(acc_ref)

        def step(masked):
            k = jnp.concatenate([kn_ref[...], kp_ref[...]], axis=1)
            s = lax.dot_general(q_ref[...], k, _DIMS["nt"], preferred_element_type=F32)
            p = jnp.exp2(s * EXP2_SCALE - lse_ref[0])
            if masked:
                p = jnp.where(_causal_mask(blk), p, 0.0)
            dp = lax.dot_general(do_ref[...], v_ref[...], _DIMS["nt"], preferred_element_type=F32)
            ds = (p * (dp - dl_ref[0])).astype(BF16)
            acc_ref[...] += jnp.dot(ds, k, preferred_element_type=F32)

        @pl.when(ki < qi)
        def _():
            step(False)

        @pl.when(ki == qi)
        def _():
            step(True)
            dq_ref[:, :QK_NOPE_DIM] = (acc_ref[:, :QK_NOPE_DIM] * ATTN_SCALE).astype(dq_ref.dtype)
            dq_ref[:, QK_NOPE_DIM:] = _rope(acc_ref[:, QK_NOPE_DIM:] * ATTN_SCALE, c_ref[...],
                                            -s_ref[...]).astype(dq_ref.dtype)

    q_spec, kn_spec, kp_spec, v_spec, do_spec = _attn_specs(S, blk, hpd, by_key=False)
    col = pl.BlockSpec((1, blk, 1), lambda h, t, qt, kt: (h, qt[t], 0))
    tab = pl.BlockSpec((blk, 128), lambda h, t, qt, kt: (qt[t], 0))
    grid_spec = pltpu.PrefetchScalarGridSpec(
        num_scalar_prefetch=2, grid=(H, q_tab.shape[0]),
        in_specs=[q_spec, kn_spec, kp_spec, v_spec, do_spec, col, col, tab, tab],
        out_specs=q_spec,
        scratch_shapes=[pltpu.VMEM((blk, HEAD_PAD), F32)])
    return _pcall(
        body, name="attn_dq", grid_spec=grid_spec,
        out_shape=jax.ShapeDtypeStruct(q.shape, BF16),
        compiler_params=_params(("parallel", "arbitrary")),
    )(q_tab, k_tab, q, kvu, kpe, kvu, do, lse, delta, cos_t, sin_t)


def _attn_dkv(q, kvu, kpe, do, lse_row, delta_row, H):
    S = kpe.shape[0]
    hpd = H // N_DEV
    blk = _tile(S, ATT_BLOCK)
    nb = S // blk
    q_tab, k_tab = _block_pairs(nb, by_key=True)

    def body(qt, kt, q_ref, kn_ref, kp_ref, v_ref, do_ref, lse_ref, dl_ref, dkv_ref, dkp_ref, dk_acc, dv_acc):
        t = pl.program_id(1)
        qi, ki = qt[t], kt[t]

        def step(masked):
            k = jnp.concatenate([kn_ref[...], kp_ref[...]], axis=1)
            qv, dov = q_ref[...], do_ref[...]
            st = lax.dot_general(k, qv, _DIMS["nt"], preferred_element_type=F32)
            pt = jnp.exp2(st * EXP2_SCALE - lse_ref[0])
            if masked:
                pt = jnp.where(_causal_mask(blk, transposed=True), pt, 0.0)
            dv_new = jnp.dot(pt.astype(BF16), dov, preferred_element_type=F32)
            dpt = lax.dot_general(v_ref[...], dov, _DIMS["nt"], preferred_element_type=F32)
            dst = (pt * (dpt - dl_ref[0])).astype(BF16)
            dk_new = jnp.dot(dst, qv, preferred_element_type=F32)
            if masked:
                dv_acc[...] = dv_new
                dk_acc[...] = dk_new
            else:
                dv_acc[...] += dv_new
                dk_acc[...] += dk_new

        @pl.when(qi == ki)
        def _():
            step(True)

        @pl.when(qi > ki)
        def _():
            step(False)

        @pl.when(qi == nb - 1)
        def _():
            dkv_ref[:, :QK_NOPE_DIM] = (dk_acc[:, :QK_NOPE_DIM] * ATTN_SCALE).astype(dkv_ref.dtype)
            dkv_ref[:, QK_NOPE_DIM:] = dv_acc[...].astype(dkv_ref.dtype)
            dkp_ref[0] = dk_acc[:, QK_NOPE_DIM:] * ATTN_SCALE

    q_spec, kn_spec, kp_spec, v_spec, do_spec = _attn_specs(S, blk, hpd, by_key=True)
    row = pl.BlockSpec((1, 1, blk), lambda h, t, qt, kt: (h, 0, qt[t]))
    grid_spec = pltpu.PrefetchScalarGridSpec(
        num_scalar_prefetch=2, grid=(H, q_tab.shape[0]),
        in_specs=[q_spec, kn_spec, kp_spec, v_spec, do_spec, row, row],
        out_specs=(pl.BlockSpec((blk, HEAD_PAD), lambda h, t, qt, kt: ((h // hpd) * nb + kt[t], h % hpd)),
                   pl.BlockSpec((1, blk, 128), lambda h, t, qt, kt: (h, kt[t], 0))),
        scratch_shapes=[pltpu.VMEM((blk, HEAD_PAD), F32), pltpu.VMEM((blk, V_DIM), F32)])
    return _pcall(
        body, name="attn_dkv", grid_spec=grid_spec,
        out_shape=(jax.ShapeDtypeStruct(kvu.shape, BF16), jax.ShapeDtypeStruct((H, S, 128), F32)),
        compiler_params=_params(("parallel", "arbitrary")),
    )(q_tab, k_tab, q, kvu, kpe, kvu, do, lse_row, delta_row)


def _mesh_pos():
    return lax.axis_index("x"), lax.axis_index("y"), lax.axis_index("c")


def _flip(pos, k):
    x, y, c = pos
    return (1 - x if k & 4 else x, 1 - y if k & 2 else y, 1 - c if k & 1 else c)


def _rank(pos):
    return 4 * pos[0] + 2 * pos[1] + pos[2]


def _copies(n, src_of, dst_refs, local_src_of, send_sems, recv_sems, local_sems):
    me = _mesh_pos()
    local = [pltpu.make_async_copy(local_src_of(a), dst_refs[a].at[_rank(me)], local_sems.at[a]) for a in range(n)]
    sends, arrivals = [], []
    for k in range(1, N_DEV):
        peer = _flip(me, k)
        for a in range(n):
            idx = a * (N_DEV - 1) + k - 1
            for into, group in ((me, sends), (peer, arrivals)):
                group.append(pltpu.make_async_remote_copy(
                    src_ref=src_of(a, peer), dst_ref=dst_refs[a].at[_rank(into)],
                    send_sem=send_sems.at[idx], recv_sem=recv_sems.at[idx],
                    device_id=peer, device_id_type=pl.DeviceIdType.MESH))
    return local, sends, arrivals


def _exchange_start(*args):
    local, sends, _ = _copies(*args)
    for cp in local + sends:
        cp.start()


def _exchange_wait(*args):
    local, sends, arrivals = _copies(*args)
    for cp in arrivals:
        cp.wait_recv()
    for cp in sends:
        cp.wait_send()
    for cp in local:
        cp.wait()


def _exchange(*args):
    _exchange_start(*args)
    _exchange_wait(*args)


def _sems(n):
    return [pltpu.SemaphoreType.DMA((n * (N_DEV - 1),)), pltpu.SemaphoreType.DMA((n * (N_DEV - 1),)),
            pltpu.SemaphoreType.DMA((n,))]


_ANY = pl.BlockSpec(memory_space=pl.ANY)


class _Ride:
    def __init__(self, kind, arrays):
        self.kind, self.arrays, self.n = kind, list(arrays), len(arrays)

    def out_shape(self):
        lead = (N_DEV,) if self.kind == "gather" else ()
        return [jax.ShapeDtypeStruct(lead + a.shape, a.dtype) for a in self.arrays]

    def _args(self, x_refs, out_refs, sems):
        if self.kind == "gather":
            return (self.n, lambda a, peer: x_refs[a], out_refs, lambda a: x_refs[a]) + tuple(sems)
        me = _mesh_pos()
        return (self.n, lambda a, peer: x_refs[a].at[_rank(peer)], out_refs, lambda a: x_refs[a].at[_rank(me)]) + tuple(sems)

    def start(self, x_refs, out_refs, sems):
        _exchange_start(*self._args(x_refs, out_refs, sems))

    def wait(self, x_refs, out_refs, sems):
        _exchange_wait(*self._args(x_refs, out_refs, sems))


def _pallas(body, name, grid, operands, in_specs, out_specs, out_shape, scratch=(), sem=None, ride=None, prefetch=()):
    n_pf, n_in, n_out, n_scr = len(prefetch), len(in_specs), len(out_specs), len(scratch)
    nr = ride.n if ride else 0

    def wrapped(*refs):
        refs = list(refs)
        pf, refs = refs[:n_pf], refs[n_pf:]
        ins, r_in = refs[:n_in], refs[n_in:n_in + nr]
        outs, r_out = refs[n_in + nr:n_in + nr + n_out], refs[n_in + nr + n_out:n_in + 2 * nr + n_out]
        rest = refs[n_in + 2 * nr + n_out:]
        scr, sems = rest[:n_scr], rest[n_scr:]
        if ride:
            first, last = None, None
            for d, size in enumerate(grid):
                i = pl.program_id(d)
                first = (i == 0) if first is None else jnp.logical_and(first, i == 0)
                last = (i == size - 1) if last is None else jnp.logical_and(last, i == size - 1)

            @pl.when(first)
            def _():
                ride.start(r_in, r_out, sems)

        body(pf + ins, outs, scr)
        if ride:
            @pl.when(last)
            def _():
                ride.wait(r_in, r_out, sems)

    grid_spec = pltpu.PrefetchScalarGridSpec(
        num_scalar_prefetch=n_pf, grid=grid,
        in_specs=list(in_specs) + [_ANY] * nr, out_specs=tuple(list(out_specs) + [_ANY] * nr),
        scratch_shapes=list(scratch) + (_sems(nr) if ride else []))
    sem = tuple(sem) if sem is not None and not ride else ("arbitrary",) * len(grid)
    return list(_pcall(
        wrapped, name=name, grid_spec=grid_spec,
        out_shape=tuple(list(out_shape) + (ride.out_shape() if ride else [])),
        compiler_params=_params(sem),
    )(*(list(prefetch) + list(operands) + (ride.arrays if ride else []))))


def _all_gather(shards, name):
    n = len(shards)

    def body(*refs):
        x_refs, out_refs = refs[:n], refs[n:2 * n]
        _exchange(n, lambda a, peer: x_refs[a], out_refs, lambda a: x_refs[a], *refs[2 * n:])

    return _pcall(
        body, name=name, in_specs=[_ANY] * n, out_specs=tuple([_ANY] * n),
        out_shape=tuple(jax.ShapeDtypeStruct((N_DEV,) + s.shape, s.dtype) for s in shards),
        scratch_shapes=_sems(n),
    )(*shards)


def _all_to_all(parts, name):
    n = len(parts)

    def body(*refs):
        x_refs, out_refs = refs[:n], refs[n:2 * n]
        me = _mesh_pos()
        _exchange(n, lambda a, peer: x_refs[a].at[_rank(peer)], out_refs, lambda a: x_refs[a].at[_rank(me)],
                  *refs[2 * n:])

    return _pcall(
        body, name=name, in_specs=[_ANY] * n, out_specs=tuple([_ANY] * n),
        out_shape=tuple(jax.ShapeDtypeStruct(p.shape, p.dtype) for p in parts),
        scratch_shapes=_sems(n),
    )(*parts)


def _all_reduce_small(v):
    R, C = v.shape

    def body(x_ref, out_ref, buf_ref, send_sems, recv_sems, local_sems):
        _exchange(1, lambda a, peer: x_ref, [buf_ref], lambda a: x_ref, send_sems, recv_sems, local_sems)
        total = buf_ref[0]
        for j in range(1, N_DEV):
            total = total + buf_ref[j]
        out_ref[...] = total

    vm = pl.BlockSpec(memory_space=pltpu.VMEM)
    return _pcall(
        body, name="all_reduce_small", in_specs=[vm], out_specs=vm,
        out_shape=jax.ShapeDtypeStruct((R, C), F32),
        scratch_shapes=[pltpu.VMEM((N_DEV, R, C), F32)] + _sems(1),
    )(v)


def _sum_slots(parts, name):
    shape = parts.shape[1:]
    C = shape[-1]
    R = parts.size // (N_DEV * C)
    tr = R
    while N_DEV * tr * C * 4 > (4 << 20) and tr % 32 == 0:
        tr //= 2

    def body(p_ref, o_ref):
        total = p_ref[0].astype(F32)
        for j in range(1, N_DEV):
            total = total + p_ref[j].astype(F32)
        o_ref[...] = total

    return _pcall(
        body, name=name, grid=(R // tr,),
        in_specs=[pl.BlockSpec((N_DEV, tr, C), lambda i: (0, i, 0))],
        out_specs=pl.BlockSpec((tr, C), lambda i: (i, 0)),
        out_shape=jax.ShapeDtypeStruct((R, C), F32),
        compiler_params=_params(("parallel",)),
    )(parts.reshape(N_DEV, R, C)).reshape(shape)


def _adamw(w, g, m, v, name):
    shape = w.shape
    C = shape[-1]
    R = w.size // C
    tr = R
    while tr * C * 4 > (1 << 20) and tr % 16 == 0:
        tr //= 2

    def body(w_ref, g_ref, m_ref, v_ref, d_ref, nm_ref, nv_ref):
        gv = g_ref[...]
        nm = ADAM_B1 * m_ref[...] + (1.0 - ADAM_B1) * gv
        nv = ADAM_B2 * v_ref[...] + (1.0 - ADAM_B2) * (gv * gv)
        m_hat = nm / (1.0 - ADAM_B1 ** ADAM_STEP)
        v_hat = nv / (1.0 - ADAM_B2 ** ADAM_STEP)
        d_ref[...] = -ADAM_LR * (m_hat / (jnp.sqrt(v_hat) + ADAM_EPS) + ADAM_WD * w_ref[...])
        nm_ref[...] = nm
        nv_ref[...] = nv

    blk = pl.BlockSpec((tr, C), lambda i: (i, 0))
    sds = jax.ShapeDtypeStruct((R, C), F32)
    outs = _pcall(
        body, name=name, grid=(R // tr,),
        in_specs=[blk] * 4, out_specs=(blk,) * 3, out_shape=(sds,) * 3,
        compiler_params=_params(("parallel",)),
    )(*(t.reshape(R, C) for t in (w, g, m, v)))
    return tuple(o.reshape(shape) for o in outs)


_REPLICATED = ("kv_in_norm", "kv_latent_norm", "attn_norm", "q_latent_norm", "ffn_norm", "final_norm")
_WEIGHTS = ("pool_norm", "pool_w", "pool_scale", "kv_in_norm", "w_kv_a", "kv_latent_norm", "w_kv_b", "attn_norm",
            "w_q_a", "q_latent_norm", "w_q_b", "w_o", "ffn_norm", "w_gate", "w_up", "w_down", "final_norm")


def _ffn_fwd(x, norm_g, wg, wu, wd, tag, gather=None):
    S, D = x.shape
    fs = wg.shape[1]
    tm = _tile(S, ROW_TILE)
    nmb = S // tm
    h = _rmsnorm(x, norm_g, "ffn_norm" + tag)
    g, u, a, *gathered = _ffn_up(h, wg, wu, "ffn_up" + tag, ride=gather)
    if gather is not None:
        wd = gathered[0].reshape(N_DEV * fs, D)
    y = _mm("ffn_down" + tag, "nn", a, wd, (tm, D, fs), (nmb, 1, N_DEV),
            lambda i, q, j: (j * nmb + i, 0), lambda i, q, j: (j, 0), lambda i, q, j: (i, 0), (S, D), F32, res=x)
    return y, (h, g, u, a), wd, gathered


def _ffn_bwd(x, norm_g, wg, wu, wd, saved, dy, dyb, tag, with_bf16, scatter=None):
    S, D = x.shape
    h, g, u, a = saved
    fs = g.shape[1]
    tm = _tile(S, ROW_TILE)
    nmb = S // tm
    dg, du, *got_rider = _ffn_dact(dyb, wd, g, u, "ffn_dact" + tag, ride=scatter)
    dwd = _mm("ffn_dwd" + tag, "tn", a, dyb, (fs, D, tm), (N_DEV, 1, nmb),
              lambda j, q, k: (j * nmb + k, 0), lambda j, q, k: (k, 0), lambda j, q, k: (j, 0), (N_DEV * fs, D), BF16)
    dwg, (got_dwd,) = _slots_dw("ffn_dwg" + tag, h, dg, BF16, ride=_Ride("scatter", [dwd.reshape(N_DEV, fs, D)]))
    dwu, (got_dwg,) = _slots_dw("ffn_dwu" + tag, h, du, BF16, ride=_Ride("scatter", [dwg.reshape(N_DEV, D, fs)]))
    dh, (got_dwu,) = _mm("ffn_dh" + tag, "nt", dg, wg, (tm, D, fs), (nmb, 1, N_DEV),
                         lambda i, q, j: (j * nmb + i, 0), lambda i, q, j: (j, 0), lambda i, q, j: (i, 0),
                         (S, D), F32, a2=du, b2=wu, ride=_Ride("scatter", [dwu.reshape(N_DEV, D, fs)]))
    outs = _rmsnorm_bwd(x, norm_g, dh, dy, "ffn_norm_bwd" + tag, with_bf16=with_bf16)
    return outs, got_rider, (got_dwg, got_dwu, got_dwd)


def kernel(x, positions, pool_norm, pool_w, pool_scale, kv_in_norm, w_kv_a, kv_latent_norm, w_kv_b, attn_norm, w_q_a, q_latent_norm, w_q_b, w_o, ffn_norm, w_gate, w_up, w_down, final_norm, loss_target, m_pool_norm, m_pool_w, m_pool_scale, m_kv_in_norm, m_w_kv_a, m_kv_latent_norm, m_w_kv_b, m_attn_norm, m_w_q_a, m_q_latent_norm, m_w_q_b, m_w_o, m_ffn_norm, m_w_gate, m_w_up, m_w_down, m_final_norm, v_pool_norm, v_pool_w, v_pool_scale, v_kv_in_norm, v_w_kv_a, v_kv_latent_norm, v_w_kv_b, v_attn_norm, v_w_q_a, v_q_latent_norm, v_w_q_b, v_w_o, v_ffn_norm, v_w_gate, v_w_up, v_w_down, v_final_norm):
    env = dict(locals())
    weights = {n: env[n] for n in _WEIGHTS}
    m_in = {n: env["m_" + n] for n in _WEIGHTS}
    v_in = {n: env["v_" + n] for n in _WEIGHTS}

    S, D = x.shape[1], x.shape[2]
    xs = x.reshape(S, D)
    target = loss_target.reshape(S, D)
    G, gws, gw = pool_w.shape[1:]
    Ds = w_kv_a.shape[0]
    R_kv, kvw = w_kv_b.shape
    hpd = kvw // (QK_NOPE_DIM + V_DIM)
    H = hpd * N_DEV
    R_q = w_q_a.shape[2]
    fs = w_gate.shape[2]
    bf = lambda t: t.astype(BF16)

    gains = jnp.concatenate([pool_norm, pool_scale], axis=0)
    g_pool_w, g_gains, g_wg0, g_wu0 = _all_gather([bf(pool_w[0]), gains, bf(w_gate[0]), bf(w_up[0])], "gather_first")
    wqb_pad = jnp.pad(w_q_b.reshape(R_q, hpd, QK_DIM), ((0, 0), (0, 0), (0, HEAD_PAD - QK_DIM))).reshape(R_q, hpd * HEAD_PAD)
    gather_mla = _Ride("gather", [bf(w_down[0]), bf(jnp.pad(w_kv_a, ((0, 0), (0, 128 - QK_ROPE_DIM)))), bf(w_kv_b),
                                  bf(w_q_a[0]), bf(wqb_pad), bf(w_o[0])])
    gather_ffn1 = _Ride("gather", [bf(w_gate[1]), bf(w_up[1]), bf(w_down[1])])

    wp = jnp.swapaxes(g_pool_w, 0, 1).reshape(G, gw, gw)
    gains_full = jnp.swapaxes(g_gains, 0, 1).reshape(2, D)
    g_pool_norm, g_pool_scale = gains_full[0:1], gains_full[1:2]
    wg0, wu0 = g_wg0.reshape(N_DEV * D, fs), g_wu0.reshape(N_DEV * D, fs)

    g_kv_in = kv_in_norm.reshape(1, D)
    g_kv_lat = kv_latent_norm.reshape(1, R_kv)
    g_attn = attn_norm.reshape(1, D)
    g_q_lat = q_latent_norm.reshape(1, R_q)
    g_final = final_norm.reshape(1, D)

    half = QK_ROPE_DIM // 2
    inv_freq = ROPE_BASE ** (-jnp.arange(half, dtype=F32) / half)
    ang = positions.reshape(S).astype(F32)[:, None] * inv_freq
    cos, sin = jnp.cos(ang), jnp.sin(ang)
    zeros = jnp.zeros((S, 128 - QK_ROPE_DIM), F32)
    cos_t = jnp.concatenate([cos, cos, zeros], axis=1)
    sin_t = jnp.concatenate([-sin, sin, zeros], axis=1)

    diff = _pool_pre(xs, g_pool_norm)
    x1 = _pool_mix(diff, wp, g_pool_scale, xs)
    x2, ffn0, wd0, (_, g_wkva, g_wkvb, g_wqa, g_wqb, g_wo) = _ffn_fwd(x1, ffn_norm[0:1], wg0, wu0, None, "0", gather_mla)
    wkva = g_wkva.reshape(D, R_kv + 128)
    wkvb = g_wkvb.reshape(N_DEV * R_kv, kvw)
    wqa = g_wqa.reshape(D, R_q)
    wqb = g_wqb.reshape(N_DEV * R_q, hpd * HEAD_PAD)
    wo = g_wo.reshape(H * V_DIM, D)

    hk = _rmsnorm(x2, g_kv_in, "kv_in_norm")
    kv = _mm_plain("kv_a", "nn", hk, wkva)
    ckv, kpe = _kv_mid(kv, g_kv_lat, cos_t, sin_t)
    kvu = _slots_out("kv_b", ckv, wkvb, BF16)

    ha = _rmsnorm(x2, g_attn, "attn_norm")
    ql = _mm_plain("q_a", "nn", ha, wqa)
    qn = _rmsnorm(ql, g_q_lat, "q_latent_norm")
    qr = _q_rope(_slots_out("q_b", qn, wqb, F32), cos_t, sin_t, S)
    o, lse, g_wg1, g_wu1, g_wd1 = _attn_fwd(qr, kvu, kpe, H, ride=gather_ffn1)
    wg1, wu1, wd1 = g_wg1.reshape(N_DEV * D, fs), g_wu1.reshape(N_DEV * D, fs), g_wd1.reshape(N_DEV * fs, D)
    x3 = _mm_plain("attn_out", "nn", o, wo, res=x2)
    x4, ffn1, _, _ = _ffn_fwd(x3, ffn_norm[1:2], wg1, wu1, wd1, "1")

    loss_part, dx4, dx4b, d_final = _loss_head(x4, g_final, target)

    (dx3, dx3b, d_ffn1), _, got_ffn1 = _ffn_bwd(x3, ffn_norm[1:2], wg1, wu1, wd1, ffn1, dx4, dx4b, "1", True)

    do = _mm_plain("attn_out_dx", "nt", dx3b, wo, out_dtype=BF16)
    dwo = _mm_plain("attn_out_dw", "tn", o, dx3b, out_dtype=BF16)
    delta = _attn_delta(do, o, H)
    dq = _attn_dq(qr, kvu, kpe, do, lse, delta, cos_t, sin_t, H)
    dkvu, dkpe = _attn_dkv(qr, kvu, kpe, do, lse.reshape(H, 1, S), delta.reshape(H, 1, S), H)

    dqn = _slots_in_nt("q_b_dx", dq, wqb, S)
    dwqb = _slots_dw("q_b_dw", qn, dq, BF16)
    dql, d_q_lat = _rmsnorm_bwd(ql, g_q_lat, dqn, None, "q_latent_norm_bwd", out_dtype=BF16)
    dha = _mm_plain("q_a_dx", "nt", dql, wqa)
    dwqa = _mm_plain("q_a_dw", "tn", ha, dql, out_dtype=BF16)
    dx2, d_attn = _rmsnorm_bwd(x2, g_attn, dha, dx3, "attn_norm_bwd")

    dckv = _slots_in_nt("kv_b_dx", dkvu, wkvb, S)
    dwkvb = _slots_dw("kv_b_dw", ckv, dkvu, BF16)
    dkv, d_kv_lat = _kv_mid_bwd(kv, g_kv_lat, dckv, dkpe, cos_t, sin_t)
    dhk = _mm_plain("kv_a_dx", "nt", dkv, wkva)
    dwkva = _mm_plain("kv_a_dw", "tn", hk, dkv, out_dtype=BF16)
    dx2, dx2b, d_kv_in = _rmsnorm_bwd(x2, g_kv_in, dhk, dx2, "kv_in_norm_bwd", with_bf16=True)

    scatter_mla = _Ride("scatter", [dwkva.reshape(N_DEV, Ds, R_kv + 128), dwkvb.reshape(N_DEV, R_kv, kvw),
                                    dwqa.reshape(N_DEV, Ds, R_q), dwqb.reshape(N_DEV, R_q, hpd * HEAD_PAD),
                                    dwo.reshape(N_DEV, H * V_DIM // N_DEV, D)])
    (dx1, d_ffn0), got_mla, got_ffn0 = _ffn_bwd(x1, ffn_norm[0:1], wg0, wu0, wd0, ffn0, dx2, dx2b, "0", False,
                                                  scatter=scatter_mla)

    ddiff, dwp, d_pool_scale = _pool_mix_bwd(diff, wp, g_pool_scale, dx1)
    grad_x, d_pool_norm = _pool_pre_bwd(xs, g_pool_norm, ddiff, dx1)

    d_gains = jnp.swapaxes(jnp.concatenate([d_pool_norm, d_pool_scale], axis=0).reshape(2, N_DEV, D // N_DEV), 0, 1)
    got_pool_w, got_gains = _all_to_all([jnp.swapaxes(dwp.reshape(G, N_DEV, gws, gw), 0, 1), d_gains], "exchange_pool")

    received = dict(zip(("w_kv_a", "w_kv_b", "w_q_a", "w_q_b", "w_o"), got_mla))
    received.update(zip(("w_gate0", "w_up0", "w_down0"), got_ffn0))
    received.update(zip(("w_gate1", "w_up1", "w_down1"), got_ffn1))
    received.update(pool_w=got_pool_w, gains=got_gains)
    sums = {n: _sum_slots(r, "sum_" + n) for n, r in received.items()}
    grads = {
        "pool_w": sums["pool_w"][None],
        "w_kv_a": sums["w_kv_a"][:, :R_kv + QK_ROPE_DIM],
        "w_kv_b": sums["w_kv_b"],
        "w_q_a": sums["w_q_a"][None],
        "w_q_b": sums["w_q_b"].reshape(R_q, hpd, HEAD_PAD)[:, :, :QK_DIM].reshape(1, R_q, hpd * QK_DIM),
        "w_o": sums["w_o"][None],
        "w_gate": jnp.stack([sums["w_gate0"], sums["w_gate1"]]),
        "w_up": jnp.stack([sums["w_up0"], sums["w_up1"]]),
        "w_down": jnp.stack([sums["w_down0"], sums["w_down1"]]),
        "pool_norm": sums["gains"][0:1],
        "pool_scale": sums["gains"][1:2],
    }

    small = {"kv_in_norm": d_kv_in, "kv_latent_norm": d_kv_lat, "attn_norm": d_attn, "q_latent_norm": d_q_lat,
             "ffn_norm": jnp.concatenate([d_ffn0, d_ffn1], axis=0), "final_norm": d_final}
    small_packed = jnp.concatenate([small[n].reshape(-1) for n in _REPLICATED] + [loss_part.reshape(-1)])
    pad = (-small_packed.shape[0]) % (8 * 128)
    reduced = _all_reduce_small(jnp.pad(small_packed, (0, pad)).reshape(-1, 128)).reshape(-1)
    off = 0
    for n in _REPLICATED:
        grads[n] = reduced[off:off + weights[n].size].reshape(weights[n].shape)
        off += weights[n].size
    loss = reduced[off]

    delta_w, new_m, new_v = {}, {}, {}
    for n in _WEIGHTS:
        delta_w[n], new_m[n], new_v[n] = _adamw(weights[n], grads[n], m_in[n], v_in[n], "adamw_" + n)

    return (loss, grad_x.reshape(x.shape), *[grads[n] for n in _WEIGHTS], *[delta_w[n] for n in _WEIGHTS],
            *[new_m[n] for n in _WEIGHTS], *[new_v[n] for n in _WEIGHTS])
```

```python
import math

import jax
import jax.numpy as jnp
import numpy as np
from jax import lax
from jax.experimental import pallas as pl
from jax.experimental.pallas import tpu as pltpu

F32 = jnp.float32
BF16 = jnp.bfloat16

N_DEV = 8
POOL_WINDOWS = (2, 4, 8, 16)
POOL_HALO = 16
QK_NOPE_DIM = 128
QK_ROPE_DIM = 64
QK_DIM = QK_NOPE_DIM + QK_ROPE_DIM
HEAD_PAD = 256
V_DIM = 128
ROPE_BASE = 10000.0
ATTN_SCALE = 1.0 / math.sqrt(QK_DIM)
EXP2_SCALE = ATTN_SCALE * math.log2(math.e)
NORM_EPS = 1e-6
ADAM_LR = 0.001
ADAM_B1 = 0.9
ADAM_B2 = 0.999
ADAM_EPS = 1e-08
ADAM_WD = 0.01
ADAM_STEP = 10

ROW_TILE = 512
ATT_BLOCK = 512
MM_TN, MM_TK = 512, 512
VMEM_LIMIT = 48 * 1024 * 1024

_pcall = pl.pallas_call


def _params(sem):
    return pltpu.CompilerParams(dimension_semantics=sem, vmem_limit_bytes=VMEM_LIMIT)


def _tile(dim, default):
    if dim % default == 0:
        return default
    assert dim <= 2 * default, (dim, default)
    return dim


def _row_tile(rows, width):
    ts = _tile(rows, ROW_TILE)
    while ts * width * 4 > (2 << 20) and ts % 16 == 0:
        ts //= 2
    return ts


def _rms_scale(x):
    return lax.rsqrt(jnp.mean(x * x, axis=-1, keepdims=True) + NORM_EPS)


def _rms_bwd(x, g, dy):
    r = _rms_scale(x)
    xn = x * r
    u = dy * g
    dx = r * (u - xn * jnp.mean(u * xn, axis=-1, keepdims=True))
    return dx, dy * xn


def _swap_halves(x):
    lane = lax.broadcasted_iota(jnp.int32, x.shape, 1)
    return jnp.where(lane < QK_ROPE_DIM // 2, pltpu.roll(x, 128 - QK_ROPE_DIM // 2, 1), pltpu.roll(x, QK_ROPE_DIM // 2, 1))


def _rope(x, cos_t, sin_t):
    return x * cos_t + _swap_halves(x) * sin_t


def _rmsnorm(x, g, name):
    S, D = x.shape
    ts = _row_tile(S, D)

    def body(x_ref, g_ref, o_ref):
        xf = x_ref[...]
        o_ref[...] = (xf * _rms_scale(xf) * g_ref[...]).astype(o_ref.dtype)

    return _pcall(
        body, name=name, grid=(S // ts,),
        in_specs=[pl.BlockSpec((ts, D), lambda i: (i, 0)), pl.BlockSpec((1, D), lambda i: (0, 0))],
        out_specs=pl.BlockSpec((ts, D), lambda i: (i, 0)),
        out_shape=jax.ShapeDtypeStruct((S, D), BF16),
        compiler_params=_params(("parallel",)),
    )(x, g)


def _rmsnorm_bwd(x, g, dy, res, name, out_dtype=F32, with_bf16=False):
    S, D = x.shape
    ts = _row_tile(S, D)
    has_res = res is not None

    def body(*refs):
        refs = list(refs)
        x_ref, g_ref, dy_ref = refs[:3]
        res_ref = refs[3] if has_res else None
        outs = refs[3 + has_res:]
        dx_ref, dg_ref = outs[0], outs[-1]
        dx, dgt = _rms_bwd(x_ref[...], g_ref[...], dy_ref[...].astype(F32))
        if has_res:
            dx = res_ref[...] + dx
        dx_ref[...] = dx.astype(dx_ref.dtype)
        if with_bf16:
            outs[1][...] = dx.astype(BF16)

        @pl.when(pl.program_id(0) == 0)
        def _():
            dg_ref[...] = jnp.zeros_like(dg_ref)

        dg_ref[...] += jnp.sum(dgt, axis=0, keepdims=True)

    row = pl.BlockSpec((ts, D), lambda i: (i, 0))
    vec = pl.BlockSpec((1, D), lambda i: (0, 0))
    out_specs = [row] + ([row] if with_bf16 else []) + [vec]
    out_shape = ([jax.ShapeDtypeStruct((S, D), out_dtype)] + ([jax.ShapeDtypeStruct((S, D), BF16)] if with_bf16 else [])
                 + [jax.ShapeDtypeStruct((1, D), F32)])
    return _pcall(
        body, name=name, grid=(S // ts,),
        in_specs=[row, vec, row] + ([row] if has_res else []),
        out_specs=tuple(out_specs), out_shape=tuple(out_shape),
        compiler_params=_params(("arbitrary",)),
    )(*([x, g, dy] + ([res] if has_res else [])))


def _pool_pre(x, g):
    S, D = x.shape
    ts = _row_tile(S, D)
    gw = D // len(POOL_WINDOWS)
    hb = ts // POOL_HALO

    def body(x_ref, halo_ref, g_ref, o_ref):
        i = pl.program_id(0)
        xx = jnp.concatenate([halo_ref[...], x_ref[...]], axis=0)
        h = xx * _rms_scale(xx) * g_ref[...]
        t = i * ts - POOL_HALO + lax.broadcasted_iota(jnp.int32, (ts + POOL_HALO, 1), 0)
        h = jnp.where(t >= 0, h, 0.0)
        tf = t[POOL_HALO:].astype(F32)
        for gi, w in enumerate(POOL_WINDOWS):
            hg = h[:, gi * gw:(gi + 1) * gw]
            acc, span = hg, 1
            while span < w:
                acc = acc + pltpu.roll(acc, span, 0)
                span *= 2
            count = jnp.minimum(tf + 1.0, float(w))
            diff = acc[POOL_HALO:] / count - hg[POOL_HALO:]
            o_ref[:, gi * gw:(gi + 1) * gw] = diff.astype(o_ref.dtype)

    return _pcall(
        body, name="pool_pre", grid=(S // ts,),
        in_specs=[pl.BlockSpec((ts, D), lambda i: (i, 0)),
                  pl.BlockSpec((POOL_HALO, D), lambda i: (jnp.maximum(i * hb - 1, 0), 0)),
                  pl.BlockSpec((1, D), lambda i: (0, 0))],
        out_specs=pl.BlockSpec((ts, D), lambda i: (i, 0)),
        out_shape=jax.ShapeDtypeStruct((S, D), BF16),
        compiler_params=_params(("parallel",)),
    )(x, x, g)


def _pool_pre_bwd(x, g, dd, res):
    S, D = x.shape
    ts = _row_tile(S, D)
    gw = D // len(POOL_WINDOWS)
    hb = ts // POOL_HALO
    last_halo = S // POOL_HALO - 1

    def body(x_ref, g_ref, dd_ref, halo_ref, res_ref, dx_ref, dg_ref):
        i = pl.program_id(0)
        ee = jnp.concatenate([dd_ref[...], halo_ref[...]], axis=0)
        t = i * ts + lax.broadcasted_iota(jnp.int32, (ts + POOL_HALO, 1), 0)
        ee = jnp.where(t < S, ee, 0.0)
        tf = t.astype(F32)
        n = ts + POOL_HALO
        for gi, w in enumerate(POOL_WINDOWS):
            eg = ee[:, gi * gw:(gi + 1) * gw]
            acc, span = eg / jnp.minimum(tf + 1.0, float(w)), 1
            while span < w:
                acc = acc + pltpu.roll(acc, n - span, 0)
                span *= 2
            dx_ref[:, gi * gw:(gi + 1) * gw] = acc[:ts] - eg[:ts]
        dx, dgt = _rms_bwd(x_ref[...], g_ref[...], dx_ref[...])
        dx_ref[...] = res_ref[...] + dx

        @pl.when(i == 0)
        def _():
            dg_ref[...] = jnp.zeros_like(dg_ref)

        dg_ref[...] += jnp.sum(dgt, axis=0, keepdims=True)

    row = pl.BlockSpec((ts, D), lambda i: (i, 0))
    vec = pl.BlockSpec((1, D), lambda i: (0, 0))
    return _pcall(
        body, name="pool_pre_bwd", grid=(S // ts,),
        in_specs=[row, vec, row,
                  pl.BlockSpec((POOL_HALO, D), lambda i: (jnp.minimum((i + 1) * hb, last_halo), 0)), row],
        out_specs=(row, vec),
        out_shape=(jax.ShapeDtypeStruct((S, D), F32), jax.ShapeDtypeStruct((1, D), F32)),
        compiler_params=_params(("arbitrary",)),
    )(x, g, dd, dd, res)


def _pool_mix(diff, w, scale, x):
    S, D = x.shape
    G, gw, _ = w.shape
    ts = _tile(S, ROW_TILE)

    def body(d_ref, w_ref, s_ref, x_ref, o_ref):
        mo = jnp.dot(d_ref[...], w_ref[0], preferred_element_type=F32)
        o_ref[...] = x_ref[...] + mo * s_ref[...]

    blk = pl.BlockSpec((ts, gw), lambda i, g: (i, g))
    return _pcall(
        body, name="pool_mix", grid=(S // ts, G),
        in_specs=[blk, pl.BlockSpec((1, gw, gw), lambda i, g: (g, 0, 0)), pl.BlockSpec((1, gw), lambda i, g: (0, g)), blk],
        out_specs=blk,
        out_shape=jax.ShapeDtypeStruct((S, D), F32),
        compiler_params=_params(("parallel", "parallel")),
    )(diff, w, scale, x)


def _pool_mix_bwd(diff, w, scale, dy):
    S, D = dy.shape
    G, gw, _ = w.shape
    ts = _tile(S, ROW_TILE)

    def body(d_ref, w_ref, s_ref, dy_ref, dd_ref, dw_ref, ds_ref):
        i = pl.program_id(1)
        d = d_ref[...]
        dy = dy_ref[...]
        mo = jnp.dot(d, w_ref[0], preferred_element_type=F32)
        dmo = (dy * s_ref[...]).astype(BF16)
        dd_ref[...] = lax.dot_general(dmo, w_ref[0], (((1,), (1,)), ((), ())), preferred_element_type=F32)

        @pl.when(i == 0)
        def _():
            dw_ref[...] = jnp.zeros_like(dw_ref)
            ds_ref[...] = jnp.zeros_like(ds_ref)

        dw_ref[0] += lax.dot_general(d, dmo, (((0,), (0,)), ((), ())), preferred_element_type=F32)
        ds_ref[...] += jnp.sum(dy * mo, axis=0, keepdims=True)

    blk = pl.BlockSpec((ts, gw), lambda g, i: (i, g))
    wspec = pl.BlockSpec((1, gw, gw), lambda g, i: (g, 0, 0))
    vec = pl.BlockSpec((1, gw), lambda g, i: (0, g))
    return _pcall(
        body, name="pool_mix_bwd", grid=(G, S // ts),
        in_specs=[blk, wspec, vec, blk],
        out_specs=(blk, wspec, vec),
        out_shape=(jax.ShapeDtypeStruct((S, D), F32), jax.ShapeDtypeStruct((G, gw, gw), F32),
                   jax.ShapeDtypeStruct((1, D), F32)),
        compiler_params=_params(("parallel", "arbitrary")),
    )(diff, w, scale, dy)


def _loss_head(x, g, target):
    S, D = x.shape
    ts = _row_tile(S, D)

    def body(x_ref, g_ref, t_ref, loss_ref, dx_ref, dxb_ref, dg_ref):
        xf = x_ref[...]
        gv = g_ref[...]
        err = xf * _rms_scale(xf) * gv - t_ref[...]
        dx, dgt = _rms_bwd(xf, gv, err * (1.0 / D))
        dx_ref[...] = dx
        dxb_ref[...] = dx.astype(BF16)

        @pl.when(pl.program_id(0) == 0)
        def _():
            loss_ref[...] = jnp.zeros_like(loss_ref)
            dg_ref[...] = jnp.zeros_like(dg_ref)

        part = 0.5 * jnp.sum(jnp.sum(err * err, axis=-1, keepdims=True) * (1.0 / D), axis=0, keepdims=True)
        loss_ref[...] += jnp.broadcast_to(part, loss_ref.shape)
        dg_ref[...] += jnp.sum(dgt, axis=0, keepdims=True)

    row = pl.BlockSpec((ts, D), lambda i: (i, 0))
    vec = pl.BlockSpec((1, D), lambda i: (0, 0))
    return _pcall(
        body, name="loss_head", grid=(S // ts,),
        in_specs=[row, vec, row],
        out_specs=(pl.BlockSpec((1, 128), lambda i: (0, 0)), row, row, vec),
        out_shape=(jax.ShapeDtypeStruct((1, 128), F32), jax.ShapeDtypeStruct((S, D), F32),
                   jax.ShapeDtypeStruct((S, D), BF16), jax.ShapeDtypeStruct((1, D), F32)),
        compiler_params=_params(("arbitrary",)),
    )(x, g, target)


_DIMS = {"nn": (((1,), (0,)), ((), ())), "nt": (((1,), (1,)), ((), ())), "tn": (((0,), (0,)), ((), ()))}


def _mm(name, mode, a, b, tiles, grid, a_map, b_map, o_map, out_shape, out_dtype=F32, res=None, a2=None, b2=None,
        ride=None):
    tm, tn, tk = tiles
    nk = grid[-1]
    dual, has_res = a2 is not None, res is not None
    dn = _DIMS[mode]

    def body(ins, outs, scratch):
        o_ref = outs[0]

        def product():
            part = lax.dot_general(ins[0][...].astype(BF16), ins[1][...].astype(BF16), dn, preferred_element_type=F32)
            if dual:
                part += lax.dot_general(ins[2][...].astype(BF16), ins[3][...].astype(BF16), dn,
                                        preferred_element_type=F32)
            return part

        def finish(out):
            if has_res:
                out = ins[-1][...] + out
            o_ref[...] = out.astype(o_ref.dtype)

        if nk == 1:
            finish(product())
            return
        acc_ref = scratch[0]
        k = pl.program_id(2)

        @pl.when(k == 0)
        def _():
            acc_ref[...] = jnp.zeros_like(acc_ref)

        acc_ref[...] += product()

        @pl.when(k == nk - 1)
        def _():
            finish(acc_ref[...])

    a_spec = pl.BlockSpec((tk, tm) if mode == "tn" else (tm, tk), a_map)
    b_spec = pl.BlockSpec((tn, tk) if mode == "nt" else (tk, tn), b_map)
    o_spec = pl.BlockSpec((tm, tn), o_map)
    operands, in_specs = [a, b], [a_spec, b_spec]
    if dual:
        operands += [a2, b2]
        in_specs += [a_spec, b_spec]
    if has_res:
        operands.append(res)
        in_specs.append(o_spec)
    outs = _pallas(body, name, grid, operands, in_specs, [o_spec], [jax.ShapeDtypeStruct(out_shape, out_dtype)],
                   scratch=[pltpu.VMEM((tm, tn), F32)] if nk > 1 else [],
                   sem=("parallel", "parallel", "arbitrary"), ride=ride)
    return (outs[0], outs[1:]) if ride else outs[0]


def _mm_plain(name, mode, a, b, out_dtype=F32, res=None):
    if mode == "tn":
        (K, M), N = a.shape, b.shape[1]
    elif mode == "nt":
        (M, K), N = a.shape, b.shape[0]
    else:
        (M, K), N = a.shape, b.shape[1]
    tm, tn, tk = _tile(M, ROW_TILE), _tile(N, MM_TN), _tile(K, MM_TK)
    a_map = (lambda i, j, k: (k, i)) if mode == "tn" else (lambda i, j, k: (i, k))
    b_map = (lambda i, j, k: (j, k)) if mode == "nt" else (lambda i, j, k: (k, j))
    return _mm(name, mode, a, b, (tm, tn, tk), (M // tm, N // tn, K // tk), a_map, b_map, lambda i, j, k: (i, j),
               (M, N), out_dtype, res)


def _slots_out(name, a, w_slots, out_dtype):
    S, K = a.shape
    n = w_slots.shape[1]
    tm = _tile(S, ROW_TILE)
    nmb = S // tm
    return _mm(name, "nn", a, w_slots, (tm, n, K), (nmb, N_DEV, 1),
               lambda i, j, k: (i, 0), lambda i, j, k: (j, 0), lambda i, j, k: (j * nmb + i, 0),
               (N_DEV * S, n), out_dtype)


def _slots_in_nt(name, a_slots, w_slots, S, a2=None, w2=None, ride=None):
    n = a_slots.shape[1]
    K = w_slots.shape[0] // N_DEV
    tm = _tile(S, ROW_TILE)
    nmb = S // tm
    return _mm(name, "nt", a_slots, w_slots, (tm, K, n), (nmb, 1, N_DEV),
               lambda i, q, j: (j * nmb + i, 0), lambda i, q, j: (j, 0), lambda i, q, j: (i, 0),
               (S, K), F32, a2=a2, b2=w2, ride=ride)


def _slots_dw(name, a, d_slots, out_dtype=F32, ride=None):
    S, K = a.shape
    n = d_slots.shape[1]
    tk = _tile(S, ROW_TILE)
    nkb = S // tk
    return _mm(name, "tn", a, d_slots, (K, n, tk), (N_DEV, 1, nkb),
               lambda j, q, k: (k, 0), lambda j, q, k: (j * nkb + k, 0), lambda j, q, k: (j, 0),
               (N_DEV * K, n), out_dtype, ride=ride)


def _sigmoid(x):
    return 1.0 / (1.0 + jnp.exp(-x))


def _ffn_up(h, wg, wu, name, ride=None):
    S, D = h.shape
    fs = wg.shape[1]
    tm = _tile(S, ROW_TILE)
    nmb = S // tm

    def body(ins, outs, scratch):
        h_ref, wg_ref, wu_ref = ins
        g_ref, u_ref, a_ref = outs
        hv = h_ref[...]
        g = jnp.dot(hv, wg_ref[...], preferred_element_type=F32)
        u = jnp.dot(hv, wu_ref[...], preferred_element_type=F32)
        g_ref[...] = g
        u_ref[...] = u
        a_ref[...] = (g * _sigmoid(g) * u).astype(a_ref.dtype)

    w_spec = pl.BlockSpec((D, fs), lambda i, j: (j, 0))
    o_spec = pl.BlockSpec((tm, fs), lambda i, j: (j * nmb + i, 0))
    sds = jax.ShapeDtypeStruct((N_DEV * S, fs), F32)
    return _pallas(body, name, (nmb, N_DEV), [h, wg, wu],
                   [pl.BlockSpec((tm, D), lambda i, j: (i, 0)), w_spec, w_spec], [o_spec, o_spec, o_spec],
                   [sds, sds, jax.ShapeDtypeStruct((N_DEV * S, fs), BF16)], sem=("parallel", "arbitrary"), ride=ride)


def _ffn_dact(dy, wd, g, u, name, ride=None):
    S, D = dy.shape
    fs = g.shape[1]
    tm = _tile(S, ROW_TILE)
    nmb = S // tm

    def body(ins, outs, scratch):
        dy_ref, wd_ref, g_ref, u_ref = ins
        dg_ref, du_ref = outs
        da = lax.dot_general(dy_ref[...], wd_ref[...], _DIMS["nt"], preferred_element_type=F32)
        g, u = g_ref[...], u_ref[...]
        sig = _sigmoid(g)
        dg_ref[...] = (da * u * (sig * (1.0 + g * (1.0 - sig)))).astype(dg_ref.dtype)
        du_ref[...] = (da * (g * sig)).astype(du_ref.dtype)

    o_spec = pl.BlockSpec((tm, fs), lambda i, j: (j * nmb + i, 0))
    sds = jax.ShapeDtypeStruct((N_DEV * S, fs), BF16)
    return _pallas(body, name, (nmb, N_DEV), [dy, wd, g, u],
                   [pl.BlockSpec((tm, D), lambda i, j: (i, 0)), pl.BlockSpec((fs, D), lambda i, j: (j, 0)),
                    o_spec, o_spec], [o_spec, o_spec], [sds, sds], sem=("parallel", "arbitrary"), ride=ride)


def _q_rope(q, cos_t, sin_t, S):
    rows, W = q.shape
    ts = _tile(S, ROW_TILE)
    nsb = S // ts

    def body(q_ref, c_ref, s_ref, o_ref):
        o_ref[:, :QK_NOPE_DIM] = q_ref[:, :QK_NOPE_DIM].astype(o_ref.dtype)
        o_ref[:, QK_NOPE_DIM:] = _rope(q_ref[:, QK_NOPE_DIM:], c_ref[...], s_ref[...]).astype(o_ref.dtype)

    blk = pl.BlockSpec((ts, HEAD_PAD), lambda i, h: (i, h))
    tab = pl.BlockSpec((ts, 128), lambda i, h: (i % nsb, 0))
    return _pcall(
        body, name="q_rope", grid=(rows // ts, W // HEAD_PAD),
        in_specs=[blk, tab, tab], out_specs=blk,
        out_shape=jax.ShapeDtypeStruct((rows, W), BF16),
        compiler_params=_params(("parallel", "parallel")),
    )(q, cos_t, sin_t)


def _kv_mid(kv, g, cos_t, sin_t):
    S, W = kv.shape
    R = W - 128
    ts = _tile(S, ROW_TILE)

    def body(kv_ref, g_ref, c_ref, s_ref, c_out, k_out):
        cf = kv_ref[:, :R]
        c_out[...] = (cf * _rms_scale(cf) * g_ref[...]).astype(c_out.dtype)
        k_out[...] = _rope(kv_ref[:, R:], c_ref[...], s_ref[...]).astype(k_out.dtype)

    tab = pl.BlockSpec((ts, 128), lambda i: (i, 0))
    return _pcall(
        body, name="kv_mid", grid=(S // ts,),
        in_specs=[pl.BlockSpec((ts, W), lambda i: (i, 0)), pl.BlockSpec((1, R), lambda i: (0, 0)), tab, tab],
        out_specs=(pl.BlockSpec((ts, R), lambda i: (i, 0)), tab),
        out_shape=(jax.ShapeDtypeStruct((S, R), BF16), jax.ShapeDtypeStruct((S, 128), BF16)),
        compiler_params=_params(("parallel",)),
    )(kv, g, cos_t, sin_t)


def _kv_mid_bwd(kv, g, dc, dkpe, cos_t, sin_t):
    S, W = kv.shape
    R = W - 128
    H = dkpe.shape[0]
    ts = _row_tile(S, H * 128)

    def body(kv_ref, g_ref, dc_ref, dk_ref, c_ref, s_ref, dkv_ref, dg_ref):
        dx, dgt = _rms_bwd(kv_ref[:, :R], g_ref[...], dc_ref[...])
        dkv_ref[:, :R] = dx.astype(dkv_ref.dtype)
        dk = dk_ref[0]
        for h in range(1, H):
            dk = dk + dk_ref[h]
        dkv_ref[:, R:] = _rope(dk, c_ref[...], -s_ref[...]).astype(dkv_ref.dtype)

        @pl.when(pl.program_id(0) == 0)
        def _():
            dg_ref[...] = jnp.zeros_like(dg_ref)

        dg_ref[...] += jnp.sum(dgt, axis=0, keepdims=True)

    tab = pl.BlockSpec((ts, 128), lambda i: (i, 0))
    vec = pl.BlockSpec((1, R), lambda i: (0, 0))
    return _pcall(
        body, name="kv_mid_bwd", grid=(S // ts,),
        in_specs=[pl.BlockSpec((ts, W), lambda i: (i, 0)), vec, pl.BlockSpec((ts, R), lambda i: (i, 0)),
                  pl.BlockSpec((H, ts, 128), lambda i: (0, i, 0)), tab, tab],
        out_specs=(pl.BlockSpec((ts, W), lambda i: (i, 0)), vec),
        out_shape=(jax.ShapeDtypeStruct((S, W), BF16), jax.ShapeDtypeStruct((1, R), F32)),
        compiler_params=_params(("arbitrary",)),
    )(kv, g, dc, dkpe, cos_t, sin_t)


def _block_pairs(nb, by_key):
    pairs = ([(qi, ki) for ki in range(nb) for qi in range(ki, nb)] if by_key
             else [(qi, ki) for qi in range(nb) for ki in range(qi + 1)])
    return jnp.asarray(np.array([p[0] for p in pairs], np.int32)), jnp.asarray(np.array([p[1] for p in pairs], np.int32))


def _causal_mask(blk, transposed=False):
    r = lax.broadcasted_iota(jnp.int32, (blk, blk), 0)
    c = lax.broadcasted_iota(jnp.int32, (blk, blk), 1)
    return (r <= c) if transposed else (c <= r)


def _attn_specs(S, blk, hpd):
    nb = S // blk
    w = hpd * HEAD_PAD
    return dict(
        q=pl.BlockSpec((blk, w), lambda j, t, qt, kt: (j * nb + qt[t], 0)),
        kv=pl.BlockSpec((blk, w), lambda j, t, qt, kt: (j * nb + kt[t], 0)),
        kp=pl.BlockSpec((blk, 128), lambda j, t, qt, kt: (kt[t], 0)),
        do=pl.BlockSpec((blk, hpd * V_DIM), lambda j, t, qt, kt: (qt[t], j)),
        col=pl.BlockSpec((hpd, blk, 1), lambda j, t, qt, kt: (j, qt[t], 0)),
        row=pl.BlockSpec((hpd, 1, blk), lambda j, t, qt, kt: (j, 0, qt[t])),
        tab=pl.BlockSpec((blk, 128), lambda j, t, qt, kt: (qt[t], 0)))


def _head_k(kv_ref, kp, hh):
    return jnp.concatenate([kv_ref[:, hh * HEAD_PAD:hh * HEAD_PAD + QK_NOPE_DIM], kp], axis=1)


def _head_v(kv_ref, hh):
    return kv_ref[:, hh * HEAD_PAD + QK_NOPE_DIM:(hh + 1) * HEAD_PAD]


def _attn_fwd(q, kvu, kpe, H, ride=None):
    S = kpe.shape[0]
    hpd = H // N_DEV
    blk = _tile(S, ATT_BLOCK)
    q_tab, k_tab = _block_pairs(S // blk, by_key=False)

    def body(ins, outs, scratch):
        qt, kt, q_ref, kv_ref, kp_ref = ins
        o_ref, lse_ref = outs
        m_ref, acc_ref = scratch
        t = pl.program_id(1)
        qi, ki = qt[t], kt[t]

        @pl.when(ki == 0)
        def _():
            m_ref[...] = jnp.full_like(m_ref, -jnp.inf)
            acc_ref[...] = jnp.zeros_like(acc_ref)

        def step(masked):
            kp = kp_ref[...]
            ones = jnp.ones((blk, 128), BF16)
            for hh in range(hpd):
                s = lax.dot_general(q_ref[:, hh * HEAD_PAD:(hh + 1) * HEAD_PAD], _head_k(kv_ref, kp, hh), _DIMS["nt"],
                                    preferred_element_type=F32)
                if masked:
                    s = jnp.where(_causal_mask(blk), s, -jnp.inf)
                m_old = m_ref[hh]
                m_new = jnp.maximum(m_old, jnp.max(s, axis=-1, keepdims=True))
                p = jnp.exp2((s - m_new) * EXP2_SCALE).astype(BF16)
                v_ext = jnp.concatenate([_head_v(kv_ref, hh), ones], axis=1)
                acc_ref[hh] = (jnp.exp2((m_old - m_new) * EXP2_SCALE) * acc_ref[hh]
                               + jnp.dot(p, v_ext, preferred_element_type=F32))
                m_ref[hh] = m_new

        @pl.when(ki < qi)
        def _():
            step(False)

        @pl.when(ki == qi)
        def _():
            step(True)
            for hh in range(hpd):
                l = acc_ref[hh, :, V_DIM:]
                o_ref[:, hh * V_DIM:(hh + 1) * V_DIM] = (acc_ref[hh, :, :V_DIM] / l).astype(o_ref.dtype)
                lse_ref[hh] = m_ref[hh] * EXP2_SCALE + jnp.log2(l[:, :1])

    sp = _attn_specs(S, blk, hpd)
    return _pallas(body, "attn_fwd", (N_DEV, q_tab.shape[0]), [q, kvu, kpe], [sp["q"], sp["kv"], sp["kp"]],
                   [sp["do"], sp["col"]],
                   [jax.ShapeDtypeStruct((S, H * V_DIM), BF16), jax.ShapeDtypeStruct((H, S, 1), F32)],
                   scratch=[pltpu.VMEM((hpd, blk, 1), F32), pltpu.VMEM((hpd, blk, 2 * V_DIM), F32)],
                   sem=("parallel", "arbitrary"), ride=ride, prefetch=[q_tab, k_tab])


def _attn_delta(do, o, H):
    S = do.shape[0]
    ts = _tile(S, ROW_TILE)

    def body(do_ref, o_ref, d_ref):
        d_ref[0] = jnp.sum(do_ref[...].astype(F32) * o_ref[...].astype(F32), axis=-1, keepdims=True)

    blk = pl.BlockSpec((ts, V_DIM), lambda i, h: (i, h))
    return _pcall(
        body, name="attn_delta", grid=(S // ts, H),
        in_specs=[blk, blk], out_specs=pl.BlockSpec((1, ts, 1), lambda i, h: (h, i, 0)),
        out_shape=jax.ShapeDtypeStruct((H, S, 1), F32),
        compiler_params=_params(("parallel", "parallel")),
    )(do, o)


def _attn_dq(q, kvu, kpe, do, lse, delta, cos_t, sin_t, H):
    S = kpe.shape[0]
    hpd = H // N_DEV
    blk = _tile(S, ATT_BLOCK)
    q_tab, k_tab = _block_pairs(S // blk, by_key=False)

    def body(qt, kt, q_ref, kv_ref, kp_ref, do_ref, lse_ref, dl_ref, c_ref, s_ref, dq_ref, acc_ref):
        t = pl.program_id(1)
        qi, ki = qt[t], kt[t]

        @pl.when(ki == 0)
        def _():
            acc_ref[...] = jnp.zeros_like(acc_ref)

        def step(masked):
            kp = kp_ref[...]
            for hh in range(hpd):
                k = _head_k(kv_ref, kp, hh)
                s = lax.dot_general(q_ref[:, hh * HEAD_PAD:(hh + 1) * HEAD_PAD], k, _DIMS["nt"],
                                    preferred_element_type=F32)
                p = jnp.exp2(s * EXP2_SCALE - lse_ref[hh])
                if masked:
                    p = jnp.where(_causal_mask(blk), p, 0.0)
                dp = lax.dot_general(do_ref[:, hh * V_DIM:(hh + 1) * V_DIM], _head_v(kv_ref, hh), _DIMS["nt"],
                                     preferred_element_type=F32)
                ds = (p * (dp - dl_ref[hh])).astype(BF16)
                acc_ref[hh] += jnp.dot(ds, k, preferred_element_type=F32)

        @pl.when(ki < qi)
        def _():
            step(False)

        @pl.when(ki == qi)
        def _():
            step(True)
            for hh in range(hpd):
                c0 = hh * HEAD_PAD
                dq_ref[:, c0:c0 + QK_NOPE_DIM] = (acc_ref[hh, :, :QK_NOPE_DIM] * ATTN_SCALE).astype(dq_ref.dtype)
                dq_ref[:, c0 + QK_NOPE_DIM:c0 + HEAD_PAD] = _rope(
                    acc_ref[hh, :, QK_NOPE_DIM:] * ATTN_SCALE, c_ref[...], -s_ref[...]).astype(dq_ref.dtype)

    sp = _attn_specs(S, blk, hpd)
    grid_spec = pltpu.PrefetchScalarGridSpec(
        num_scalar_prefetch=2, grid=(N_DEV, q_tab.shape[0]),
        in_specs=[sp["q"], sp["kv"], sp["kp"], sp["do"], sp["col"], sp["col"], sp["tab"], sp["tab"]],
        out_specs=sp["q"],
        scratch_shapes=[pltpu.VMEM((hpd, blk, HEAD_PAD), F32)])
    return _pcall(
        body, name="attn_dq", grid_spec=grid_spec,
        out_shape=jax.ShapeDtypeStruct(q.shape, BF16),
        compiler_params=_params(("parallel", "arbitrary")),
    )(q_tab, k_tab, q, kvu, kpe, do, lse, delta, cos_t, sin_t)


def _attn_dkv(q, kvu, kpe, do, lse_row, delta_row, H):
    S = kpe.shape[0]
    hpd = H // N_DEV
    blk = _tile(S, ATT_BLOCK)
    nb = S // blk
    q_tab, k_tab = _block_pairs(nb, by_key=True)

    def body(qt, kt, q_ref, kv_ref, kp_ref, do_ref, lse_ref, dl_ref, dkv_ref, dkp_ref, dk_acc, dv_acc):
        t = pl.program_id(1)
        qi, ki = qt[t], kt[t]

        def step(masked):
            kp = kp_ref[...]
            for hh in range(hpd):
                qv = q_ref[:, hh * HEAD_PAD:(hh + 1) * HEAD_PAD]
                dov = do_ref[:, hh * V_DIM:(hh + 1) * V_DIM]
                st = lax.dot_general(_head_k(kv_ref, kp, hh), qv, _DIMS["nt"],
                                     preferred_element_type=F32)
                pt = jnp.exp2(st * EXP2_SCALE - lse_ref[hh])
                if masked:
                    pt = jnp.where(_causal_mask(blk, transposed=True), pt, 0.0)
                dv_new = jnp.dot(pt.astype(BF16), dov, preferred_element_type=F32)
                dpt = lax.dot_general(_head_v(kv_ref, hh), dov, _DIMS["nt"], preferred_element_type=F32)
                dst = (pt * (dpt - dl_ref[hh])).astype(BF16)
                dk_new = jnp.dot(dst, qv, preferred_element_type=F32)
                if masked:
                    dv_acc[hh] = dv_new
                    dk_acc[hh] = dk_new
                else:
                    dv_acc[hh] += dv_new
                    dk_acc[hh] += dk_new

        @pl.when(qi == ki)
        def _():
            step(True)

        @pl.when(qi > ki)
        def _():
            step(False)

        @pl.when(qi == nb - 1)
        def _():
            for hh in range(hpd):
                c0 = hh * HEAD_PAD
                dkv_ref[:, c0:c0 + QK_NOPE_DIM] = (dk_acc[hh, :, :QK_NOPE_DIM] * ATTN_SCALE).astype(dkv_ref.dtype)
                dkv_ref[:, c0 + QK_NOPE_DIM:c0 + HEAD_PAD] = dv_acc[hh].astype(dkv_ref.dtype)
                dkp_ref[hh] = dk_acc[hh, :, QK_NOPE_DIM:] * ATTN_SCALE

    sp = _attn_specs(S, blk, hpd)
    grid_spec = pltpu.PrefetchScalarGridSpec(
        num_scalar_prefetch=2, grid=(N_DEV, q_tab.shape[0]),
        in_specs=[sp["q"], sp["kv"], sp["kp"], sp["do"], sp["row"], sp["row"]],
        out_specs=(sp["kv"], pl.BlockSpec((hpd, blk, 128), lambda j, t, qt, kt: (j, kt[t], 0))),
        scratch_shapes=[pltpu.VMEM((hpd, blk, HEAD_PAD), F32), pltpu.VMEM((hpd, blk, V_DIM), F32)])
    return _pcall(
        body, name="attn_dkv", grid_spec=grid_spec,
        out_shape=(jax.ShapeDtypeStruct(kvu.shape, BF16), jax.ShapeDtypeStruct((H, S, 128), F32)),
        compiler_params=_params(("parallel", "arbitrary")),
    )(q_tab, k_tab, q, kvu, kpe, do, lse_row, delta_row)


def _mesh_pos():
    return lax.axis_index("x"), lax.axis_index("y"), lax.axis_index("c")


def _flip(pos, k):
    x, y, c = pos
    return (1 - x if k & 4 else x, 1 - y if k & 2 else y, 1 - c if k & 1 else c)


def _rank(pos):
    return 4 * pos[0] + 2 * pos[1] + pos[2]


def _copies(n, src_of, dst_refs, local_src_of, send_sems, recv_sems, local_sems):
    me = _mesh_pos()
    local = [pltpu.make_async_copy(local_src_of(a), dst_refs[a].at[_rank(me)], local_sems.at[a]) for a in range(n)]
    sends, arrivals = [], []
    for k in range(1, N_DEV):
        peer = _flip(me, k)
        for a in range(n):
            idx = a * (N_DEV - 1) + k - 1
            for into, group in ((me, sends), (peer, arrivals)):
                group.append(pltpu.make_async_remote_copy(
                    src_ref=src_of(a, peer), dst_ref=dst_refs[a].at[_rank(into)],
                    send_sem=send_sems.at[idx], recv_sem=recv_sems.at[idx],
                    device_id=peer, device_id_type=pl.DeviceIdType.MESH))
    return local, sends, arrivals


def _exchange_start(*args):
    local, sends, _ = _copies(*args)
    for cp in local + sends:
        cp.start()


def _exchange_wait(*args):
    local, sends, arrivals = _copies(*args)
    for cp in arrivals:
        cp.wait_recv()
    for cp in sends:
        cp.wait_send()
    for cp in local:
        cp.wait()


def _exchange(*args):
    _exchange_start(*args)
    _exchange_wait(*args)


def _sems(n):
    return [pltpu.SemaphoreType.DMA((n * (N_DEV - 1),)), pltpu.SemaphoreType.DMA((n * (N_DEV - 1),)),
            pltpu.SemaphoreType.DMA((n,))]


_ANY = pl.BlockSpec(memory_space=pl.ANY)


class _Ride:
    def __init__(self, kind, arrays):
        self.kind, self.arrays, self.n = kind, list(arrays), len(arrays)

    def out_shape(self):
        lead = (N_DEV,) if self.kind == "gather" else ()
        return [jax.ShapeDtypeStruct(lead + a.shape, a.dtype) for a in self.arrays]

    def _args(self, x_refs, out_refs, sems):
        if self.kind == "gather":
            return (self.n, lambda a, peer: x_refs[a], out_refs, lambda a: x_refs[a]) + tuple(sems)
        me = _mesh_pos()
        return (self.n, lambda a, peer: x_refs[a].at[_rank(peer)], out_refs, lambda a: x_refs[a].at[_rank(me)]) + tuple(sems)

    def start(self, x_refs, out_refs, sems):
        _exchange_start(*self._args(x_refs, out_refs, sems))

    def wait(self, x_refs, out_refs, sems):
        _exchange_wait(*self._args(x_refs, out_refs, sems))


def _pallas(body, name, grid, operands, in_specs, out_specs, out_shape, scratch=(), sem=None, ride=None, prefetch=()):
    n_pf, n_in, n_out, n_scr = len(prefetch), len(in_specs), len(out_specs), len(scratch)
    nr = ride.n if ride else 0

    def wrapped(*refs):
        refs = list(refs)
        pf, refs = refs[:n_pf], refs[n_pf:]
        ins, r_in = refs[:n_in], refs[n_in:n_in + nr]
        outs, r_out = refs[n_in + nr:n_in + nr + n_out], refs[n_in + nr + n_out:n_in + 2 * nr + n_out]
        rest = refs[n_in + 2 * nr + n_out:]
        scr, sems = rest[:n_scr], rest[n_scr:]
        if ride:
            first, last = None, None
            for d, size in enumerate(grid):
                i = pl.program_id(d)
                first = (i == 0) if first is None else jnp.logical_and(first, i == 0)
                last = (i == size - 1) if last is None else jnp.logical_and(last, i == size - 1)

            @pl.when(first)
            def _():
                ride.start(r_in, r_out, sems)

        body(pf + ins, outs, scr)
        if ride:
            @pl.when(last)
            def _():
                ride.wait(r_in, r_out, sems)

    grid_spec = pltpu.PrefetchScalarGridSpec(
        num_scalar_prefetch=n_pf, grid=grid,
        in_specs=list(in_specs) + [_ANY] * nr, out_specs=tuple(list(out_specs) + [_ANY] * nr),
        scratch_shapes=list(scratch) + (_sems(nr) if ride else []))
    sem = tuple(sem) if sem is not None and not ride else ("arbitrary",) * len(grid)
    return list(_pcall(
        wrapped, name=name, grid_spec=grid_spec,
        out_shape=tuple(list(out_shape) + (ride.out_shape() if ride else [])),
        compiler_params=_params(sem),
    )(*(list(prefetch) + list(operands) + (ride.arrays if ride else []))))


def _all_gather(shards, name):
    n = len(shards)

    def body(*refs):
        x_refs, out_refs = refs[:n], refs[n:2 * n]
        _exchange(n, lambda a, peer: x_refs[a], out_refs, lambda a: x_refs[a], *refs[2 * n:])

    return _pcall(
        body, name=name, in_specs=[_ANY] * n, out_specs=tuple([_ANY] * n),
        out_shape=tuple(jax.ShapeDtypeStruct((N_DEV,) + s.shape, s.dtype) for s in shards),
        scratch_shapes=_sems(n),
    )(*shards)


def _all_to_all(parts, name):
    n = len(parts)

    def body(*refs):
        x_refs, out_refs = refs[:n], refs[n:2 * n]
        me = _mesh_pos()
        _exchange(n, lambda a, peer: x_refs[a].at[_rank(peer)], out_refs, lambda a: x_refs[a].at[_rank(me)],
                  *refs[2 * n:])

    return _pcall(
        body, name=name, in_specs=[_ANY] * n, out_specs=tuple([_ANY] * n),
        out_shape=tuple(jax.ShapeDtypeStruct(p.shape, p.dtype) for p in parts),
        scratch_shapes=_sems(n),
    )(*parts)


def _all_reduce_small(v):
    R, C = v.shape

    def body(x_ref, out_ref, buf_ref, send_sems, recv_sems, local_sems):
        _exchange(1, lambda a, peer: x_ref, [buf_ref], lambda a: x_ref, send_sems, recv_sems, local_sems)
        total = buf_ref[0]
        for j in range(1, N_DEV):
            total = total + buf_ref[j]
        out_ref[...] = total

    vm = pl.BlockSpec(memory_space=pltpu.VMEM)
    return _pcall(
        body, name="all_reduce_small", in_specs=[vm], out_specs=vm,
        out_shape=jax.ShapeDtypeStruct((R, C), F32),
        scratch_shapes=[pltpu.VMEM((N_DEV, R, C), F32)] + _sems(1),
    )(v)


def _sum_slots(parts, name):
    shape = parts.shape[1:]
    C = shape[-1]
    R = parts.size // (N_DEV * C)
    tr = R
    while N_DEV * tr * C * 4 > (4 << 20) and tr % 32 == 0:
        tr //= 2

    def body(p_ref, o_ref):
        total = p_ref[0].astype(F32)
        for j in range(1, N_DEV):
            total = total + p_ref[j].astype(F32)
        o_ref[...] = total

    return _pcall(
        body, name=name, grid=(R // tr,),
        in_specs=[pl.BlockSpec((N_DEV, tr, C), lambda i: (0, i, 0))],
        out_specs=pl.BlockSpec((tr, C), lambda i: (i, 0)),
        out_shape=jax.ShapeDtypeStruct((R, C), F32),
        compiler_params=_params(("parallel",)),
    )(parts.reshape(N_DEV, R, C)).reshape(shape)


def _adamw(w, g, m, v, name):
    shape = w.shape
    C = shape[-1]
    R = w.size // C
    tr = R
    while tr * C * 4 > (1 << 20) and tr % 16 == 0:
        tr //= 2

    def body(w_ref, g_ref, m_ref, v_ref, d_ref, nm_ref, nv_ref):
        gv = g_ref[...]
        nm = ADAM_B1 * m_ref[...] + (1.0 - ADAM_B1) * gv
        nv = ADAM_B2 * v_ref[...] + (1.0 - ADAM_B2) * (gv * gv)
        m_hat = nm / (1.0 - ADAM_B1 ** ADAM_STEP)
        v_hat = nv / (1.0 - ADAM_B2 ** ADAM_STEP)
        d_ref[...] = -ADAM_LR * (m_hat / (jnp.sqrt(v_hat) + ADAM_EPS) + ADAM_WD * w_ref[...])
        nm_ref[...] = nm
        nv_ref[...] = nv

    blk = pl.BlockSpec((tr, C), lambda i: (i, 0))
    sds = jax.ShapeDtypeStruct((R, C), F32)
    outs = _pcall(
        body, name=name, grid=(R // tr,),
        in_specs=[blk] * 4, out_specs=(blk,) * 3, out_shape=(sds,) * 3,
        compiler_params=_params(("parallel",)),
    )(*(t.reshape(R, C) for t in (w, g, m, v)))
    return tuple(o.reshape(shape) for o in outs)


_REPLICATED = ("kv_in_norm", "kv_latent_norm", "attn_norm", "q_latent_norm", "ffn_norm", "final_norm")
_WEIGHTS = ("pool_norm", "pool_w", "pool_scale", "kv_in_norm", "w_kv_a", "kv_latent_norm", "w_kv_b", "attn_norm",
            "w_q_a", "q_latent_norm", "w_q_b", "w_o", "ffn_norm", "w_gate", "w_up", "w_down", "final_norm")


def _ffn_fwd(x, norm_g, wg, wu, wd, tag, gather=None):
    S, D = x.shape
    fs = wg.shape[1]
    tm = _tile(S, ROW_TILE)
    nmb = S // tm
    h = _rmsnorm(x, norm_g, "ffn_norm" + tag)
    g, u, a, *gathered = _ffn_up(h, wg, wu, "ffn_up" + tag, ride=gather)
    if gather is not None:
        wd = gathered[0].reshape(N_DEV * fs, D)
    y = _mm("ffn_down" + tag, "nn", a, wd, (tm, D, fs), (nmb, 1, N_DEV),
            lambda i, q, j: (j * nmb + i, 0), lambda i, q, j: (j, 0), lambda i, q, j: (i, 0), (S, D), F32, res=x)
    return y, (h, g, u, a), wd, gathered


def _ffn_bwd(x, norm_g, wg, wu, wd, saved, dy, dyb, tag, with_bf16, scatter=None):
    S, D = x.shape
    h, g, u, a = saved
    fs = g.shape[1]
    tm = _tile(S, ROW_TILE)
    nmb = S // tm
    dg, du, *got_rider = _ffn_dact(dyb, wd, g, u, "ffn_dact" + tag, ride=scatter)
    dwd = _mm("ffn_dwd" + tag, "tn", a, dyb, (fs, D, tm), (N_DEV, 1, nmb),
              lambda j, q, k: (j * nmb + k, 0), lambda j, q, k: (k, 0), lambda j, q, k: (j, 0), (N_DEV * fs, D), BF16)
    dwg, (got_dwd,) = _slots_dw("ffn_dwg" + tag, h, dg, BF16, ride=_Ride("scatter", [dwd.reshape(N_DEV, fs, D)]))
    dwu, (got_dwg,) = _slots_dw("ffn_dwu" + tag, h, du, BF16, ride=_Ride("scatter", [dwg.reshape(N_DEV, D, fs)]))
    dh, (got_dwu,) = _mm("ffn_dh" + tag, "nt", dg, wg, (tm, D, fs), (nmb, 1, N_DEV),
                         lambda i, q, j: (j * nmb + i, 0), lambda i, q, j: (j, 0), lambda i, q, j: (i, 0),
                         (S, D), F32, a2=du, b2=wu, ride=_Ride("scatter", [dwu.reshape(N_DEV, D, fs)]))
    outs = _rmsnorm_bwd(x, norm_g, dh, dy, "ffn_norm_bwd" + tag, with_bf16=with_bf16)
    return outs, got_rider, (got_dwg, got_dwu, got_dwd)


def kernel(x, positions, pool_norm, pool_w, pool_scale, kv_in_norm, w_kv_a, kv_latent_norm, w_kv_b, attn_norm, w_q_a, q_latent_norm, w_q_b, w_o, ffn_norm, w_gate, w_up, w_down, final_norm, loss_target, m_pool_norm, m_pool_w, m_pool_scale, m_kv_in_norm, m_w_kv_a, m_kv_latent_norm, m_w_kv_b, m_attn_norm, m_w_q_a, m_q_latent_norm, m_w_q_b, m_w_o, m_ffn_norm, m_w_gate, m_w_up, m_w_down, m_final_norm, v_pool_norm, v_pool_w, v_pool_scale, v_kv_in_norm, v_w_kv_a, v_kv_latent_norm, v_w_kv_b, v_attn_norm, v_w_q_a, v_q_latent_norm, v_w_q_b, v_w_o, v_ffn_norm, v_w_gate, v_w_up, v_w_down, v_final_norm):
    env = dict(locals())
    weights = {n: env[n] for n in _WEIGHTS}
    m_in = {n: env["m_" + n] for n in _WEIGHTS}
    v_in = {n: env["v_" + n] for n in _WEIGHTS}

    S, D = x.shape[1], x.shape[2]
    xs = x.reshape(S, D)
    target = loss_target.reshape(S, D)
    G, gws, gw = pool_w.shape[1:]
    Ds = w_kv_a.shape[0]
    R_kv, kvw = w_kv_b.shape
    hpd = kvw // (QK_NOPE_DIM + V_DIM)
    H = hpd * N_DEV
    R_q = w_q_a.shape[2]
    fs = w_gate.shape[2]
    bf = lambda t: t.astype(BF16)

    gains = jnp.concatenate([pool_norm, pool_scale], axis=0)
    g_pool_w, g_gains, g_wg0, g_wu0 = _all_gather([bf(pool_w[0]), gains, bf(w_gate[0]), bf(w_up[0])], "gather_first")
    wqb_pad = jnp.pad(w_q_b.reshape(R_q, hpd, QK_DIM), ((0, 0), (0, 0), (0, HEAD_PAD - QK_DIM))).reshape(R_q, hpd * HEAD_PAD)
    gather_mla = _Ride("gather", [bf(w_down[0]), bf(jnp.pad(w_kv_a, ((0, 0), (0, 128 - QK_ROPE_DIM)))), bf(w_kv_b),
                                  bf(w_q_a[0]), bf(wqb_pad), bf(w_o[0])])
    gather_ffn1 = _Ride("gather", [bf(w_gate[1]), bf(w_up[1]), bf(w_down[1])])

    wp = jnp.swapaxes(g_pool_w, 0, 1).reshape(G, gw, gw)
    gains_full = jnp.swapaxes(g_gains, 0, 1).reshape(2, D)
    g_pool_norm, g_pool_scale = gains_full[0:1], gains_full[1:2]
    wg0, wu0 = g_wg0.reshape(N_DEV * D, fs), g_wu0.reshape(N_DEV * D, fs)

    g_kv_in = kv_in_norm.reshape(1, D)
    g_kv_lat = kv_latent_norm.reshape(1, R_kv)
    g_attn = attn_norm.reshape(1, D)
    g_q_lat = q_latent_norm.reshape(1, R_q)
    g_final = final_norm.reshape(1, D)

    half = QK_ROPE_DIM // 2
    inv_freq = ROPE_BASE ** (-jnp.arange(half, dtype=F32) / half)
    ang = positions.reshape(S).astype(F32)[:, None] * inv_freq
    cos, sin = jnp.cos(ang), jnp.sin(ang)
    zeros = jnp.zeros((S, 128 - QK_ROPE_DIM), F32)
    cos_t = jnp.concatenate([cos, cos, zeros], axis=1)
    sin_t = jnp.concatenate([-sin, sin, zeros], axis=1)

    diff = _pool_pre(xs, g_pool_norm)
    x1 = _pool_mix(diff, wp, g_pool_scale, xs)
    x2, ffn0, wd0, (_, g_wkva, g_wkvb, g_wqa, g_wqb, g_wo) = _ffn_fwd(x1, ffn_norm[0:1], wg0, wu0, None, "0", gather_mla)
    wkva = g_wkva.reshape(D, R_kv + 128)
    wkvb = g_wkvb.reshape(N_DEV * R_kv, kvw)
    wqa = g_wqa.reshape(D, R_q)
    wqb = g_wqb.reshape(N_DEV * R_q, hpd * HEAD_PAD)
    wo = g_wo.reshape(H * V_DIM, D)

    hk = _rmsnorm(x2, g_kv_in, "kv_in_norm")
    kv = _mm_plain("kv_a", "nn", hk, wkva)
    ckv, kpe = _kv_mid(kv, g_kv_lat, cos_t, sin_t)
    kvu = _slots_out("kv_b", ckv, wkvb, BF16)

    ha = _rmsnorm(x2, g_attn, "attn_norm")
    ql = _mm_plain("q_a", "nn", ha, wqa)
    qn = _rmsnorm(ql, g_q_lat, "q_latent_norm")
    qr = _q_rope(_slots_out("q_b", qn, wqb, F32), cos_t, sin_t, S)
    o, lse, g_wg1, g_wu1, g_wd1 = _attn_fwd(qr, kvu, kpe, H, ride=gather_ffn1)
    wg1, wu1, wd1 = g_wg1.reshape(N_DEV * D, fs), g_wu1.reshape(N_DEV * D, fs), g_wd1.reshape(N_DEV * fs, D)
    x3 = _mm_plain("attn_out", "nn", o, wo, res=x2)
    x4, ffn1, _, _ = _ffn_fwd(x3, ffn_norm[1:2], wg1, wu1, wd1, "1")

    loss_part, dx4, dx4b, d_final = _loss_head(x4, g_final, target)

    (dx3, dx3b, d_ffn1), _, got_ffn1 = _ffn_bwd(x3, ffn_norm[1:2], wg1, wu1, wd1, ffn1, dx4, dx4b, "1", True)

    do = _mm_plain("attn_out_dx", "nt", dx3b, wo, out_dtype=BF16)
    dwo = _mm_plain("attn_out_dw", "tn", o, dx3b, out_dtype=BF16)
    delta = _attn_delta(do, o, H)
    dq = _attn_dq(qr, kvu, kpe, do, lse, delta, cos_t, sin_t, H)
    dkvu, dkpe = _attn_dkv(qr, kvu, kpe, do, lse.reshape(H, 1, S), delta.reshape(H, 1, S), H)

    dqn = _slots_in_nt("q_b_dx", dq, wqb, S)
    dwqb = _slots_dw("q_b_dw", qn, dq, BF16)
    dql, d_q_lat = _rmsnorm_bwd(ql, g_q_lat, dqn, None, "q_latent_norm_bwd", out_dtype=BF16)
    dha = _mm_plain("q_a_dx", "nt", dql, wqa)
    dwqa = _mm_plain("q_a_dw", "tn", ha, dql, out_dtype=BF16)
    dx2, d_attn = _rmsnorm_bwd(x2, g_attn, dha, dx3, "attn_norm_bwd")

    dckv = _slots_in_nt("kv_b_dx", dkvu, wkvb, S)
    dwkvb = _slots_dw("kv_b_dw", ckv, dkvu, BF16)
    dkv, d_kv_lat = _kv_mid_bwd(kv, g_kv_lat, dckv, dkpe, cos_t, sin_t)
    dhk = _mm_plain("kv_a_dx", "nt", dkv, wkva)
    dwkva = _mm_plain("kv_a_dw", "tn", hk, dkv, out_dtype=BF16)
    dx2, dx2b, d_kv_in = _rmsnorm_bwd(x2, g_kv_in, dhk, dx2, "kv_in_norm_bwd", with_bf16=True)

    scatter_mla = _Ride("scatter", [dwkva.reshape(N_DEV, Ds, R_kv + 128), dwkvb.reshape(N_DEV, R_kv, kvw),
                                    dwqa.reshape(N_DEV, Ds, R_q), dwqb.reshape(N_DEV, R_q, hpd * HEAD_PAD),
                                    dwo.reshape(N_DEV, H * V_DIM // N_DEV, D)])
    (dx1, d_ffn0), got_mla, got_ffn0 = _ffn_bwd(x1, ffn_norm[0:1], wg0, wu0, wd0, ffn0, dx2, dx2b, "0", False,
                                                  scatter=scatter_mla)

    ddiff, dwp, d_pool_scale = _pool_mix_bwd(diff, wp, g_pool_scale, dx1)
    grad_x, d_pool_norm = _pool_pre_bwd(xs, g_pool_norm, ddiff, dx1)

    d_gains = jnp.swapaxes(jnp.concatenate([d_pool_norm, d_pool_scale], axis=0).reshape(2, N_DEV, D // N_DEV), 0, 1)
    got_pool_w, got_gains = _all_to_all([jnp.swapaxes(dwp.reshape(G, N_DEV, gws, gw), 0, 1), d_gains], "exchange_pool")

    received = dict(zip(("w_kv_a", "w_kv_b", "w_q_a", "w_q_b", "w_o"), got_mla))
    received.update(zip(("w_gate0", "w_up0", "w_down0"), got_ffn0))
    received.update(zip(("w_gate1", "w_up1", "w_down1"), got_ffn1))
    received.update(pool_w=got_pool_w, gains=got_gains)
    sums = {n: _sum_slots(r, "sum_" + n) for n, r in received.items()}
    grads = {
        "pool_w": sums["pool_w"][None],
        "w_kv_a": sums["w_kv_a"][:, :R_kv + QK_ROPE_DIM],
        "w_kv_b": sums["w_kv_b"],
        "w_q_a": sums["w_q_a"][None],
        "w_q_b": sums["w_q_b"].reshape(R_q, hpd, HEAD_PAD)[:, :, :QK_DIM].reshape(1, R_q, hpd * QK_DIM),
        "w_o": sums["w_o"][None],
        "w_gate": jnp.stack([sums["w_gate0"], sums["w_gate1"]]),
        "w_up": jnp.stack([sums["w_up0"], sums["w_up1"]]),
        "w_down": jnp.stack([sums["w_down0"], sums["w_down1"]]),
        "pool_norm": sums["gains"][0:1],
        "pool_scale": sums["gains"][1:2],
    }

    small = {"kv_in_norm": d_kv_in, "kv_latent_norm": d_kv_lat, "attn_norm": d_attn, "q_latent_norm": d_q_lat,
             "ffn_norm": jnp.concatenate([d_ffn0, d_ffn1], axis=0), "final_norm": d_final}
    small_packed = jnp.concatenate([small[n].reshape(-1) for n in _REPLICATED] + [loss_part.reshape(-1)])
    pad = (-small_packed.shape[0]) % (8 * 128)
    reduced = _all_reduce_small(jnp.pad(small_packed, (0, pad)).reshape(-1, 128)).reshape(-1)
    off = 0
    for n in _REPLICATED:
        grads[n] = reduced[off:off + weights[n].size].reshape(weights[n].shape)
        off += weights[n].size
    loss = reduced[off]

    delta_w, new_m, new_v = {}, {}, {}
    for n in _WEIGHTS:
        delta_w[n], new_m[n], new_v[n] = _adamw(weights[n], grads[n], m_in[n], v_in[n], "adamw_" + n)

    return (loss, grad_x.reshape(x.shape), *[grads[n] for n in _WEIGHTS], *[delta_w[n] for n in _WEIGHTS],
            *[new_m[n] for n in _WEIGHTS], *[new_v[n] for n in _WEIGHTS])
```

```python
import math

import jax
import jax.numpy as jnp
import numpy as np
from jax import lax
from jax.experimental import pallas as pl
from jax.experimental.pallas import tpu as pltpu

F32 = jnp.float32
BF16 = jnp.bfloat16

N_DEV = 8
POOL_WINDOWS = (2, 4, 8, 16)
POOL_HALO = 16
QK_NOPE_DIM = 128
QK_ROPE_DIM = 64
QK_DIM = QK_NOPE_DIM + QK_ROPE_DIM
HEAD_PAD = 256
V_DIM = 128
ROPE_BASE = 10000.0
ATTN_SCALE = 1.0 / math.sqrt(QK_DIM)
EXP2_SCALE = ATTN_SCALE * math.log2(math.e)
NORM_EPS = 1e-6
ADAM_LR = 0.001
ADAM_B1 = 0.9
ADAM_B2 = 0.999
ADAM_EPS = 1e-08
ADAM_WD = 0.01
ADAM_STEP = 10

ROW_TILE = 512
ATT_BLOCK = 512
MM_TN, MM_TK = 512, 512
VMEM_LIMIT = 48 * 1024 * 1024

_pcall = pl.pallas_call


def _params(sem):
    return pltpu.CompilerParams(dimension_semantics=sem, vmem_limit_bytes=VMEM_LIMIT)


def _tile(dim, default):
    if dim % default == 0:
        return default
    assert dim <= 2 * default, (dim, default)
    return dim


def _row_tile(rows, width):
    ts = _tile(rows, ROW_TILE)
    while ts * width * 4 > (2 << 20) and ts % 16 == 0:
        ts //= 2
    return ts


def _rms_scale(x):
    return lax.rsqrt(jnp.mean(x * x, axis=-1, keepdims=True) + NORM_EPS)


def _rms_bwd(x, g, dy):
    r = _rms_scale(x)
    xn = x * r
    u = dy * g
    dx = r * (u - xn * jnp.mean(u * xn, axis=-1, keepdims=True))
    return dx, dy * xn


def _swap_halves(x):
    lane = lax.broadcasted_iota(jnp.int32, x.shape, 1)
    return jnp.where(lane < QK_ROPE_DIM // 2, pltpu.roll(x, 128 - QK_ROPE_DIM // 2, 1), pltpu.roll(x, QK_ROPE_DIM // 2, 1))


def _rope(x, cos_t, sin_t):
    return x * cos_t + _swap_halves(x) * sin_t


def _rmsnorm(x, g, name):
    S, D = x.shape
    ts = _row_tile(S, D)

    def body(x_ref, g_ref, o_ref):
        xf = x_ref[...]
        o_ref[...] = (xf * _rms_scale(xf) * g_ref[...]).astype(o_ref.dtype)

    return _pcall(
        body, name=name, grid=(S // ts,),
        in_specs=[pl.BlockSpec((ts, D), lambda i: (i, 0)), pl.BlockSpec((1, D), lambda i: (0, 0))],
        out_specs=pl.BlockSpec((ts, D), lambda i: (i, 0)),
        out_shape=jax.ShapeDtypeStruct((S, D), BF16),
        compiler_params=_params(("parallel",)),
    )(x, g)


def _rmsnorm_bwd(x, g, dy, res, name, out_dtype=F32, with_bf16=False):
    S, D = x.shape
    ts = _row_tile(S, D)
    has_res = res is not None

    def body(*refs):
        refs = list(refs)
        x_ref, g_ref, dy_ref = refs[:3]
        res_ref = refs[3] if has_res else None
        outs = refs[3 + has_res:]
        dx_ref, dg_ref = outs[0], outs[-1]
        dx, dgt = _rms_bwd(x_ref[...], g_ref[...], dy_ref[...].astype(F32))
        if has_res:
            dx = res_ref[...] + dx
        dx_ref[...] = dx.astype(dx_ref.dtype)
        if with_bf16:
            outs[1][...] = dx.astype(BF16)

        @pl.when(pl.program_id(0) == 0)
        def _():
            dg_ref[...] = jnp.zeros_like(dg_ref)

        dg_ref[...] += jnp.sum(dgt, axis=0, keepdims=True)

    row = pl.BlockSpec((ts, D), lambda i: (i, 0))
    vec = pl.BlockSpec((1, D), lambda i: (0, 0))
    out_specs = [row] + ([row] if with_bf16 else []) + [vec]
    out_shape = ([jax.ShapeDtypeStruct((S, D), out_dtype)] + ([jax.ShapeDtypeStruct((S, D), BF16)] if with_bf16 else [])
                 + [jax.ShapeDtypeStruct((1, D), F32)])
    return _pcall(
        body, name=name, grid=(S // ts,),
        in_specs=[row, vec, row] + ([row] if has_res else []),
        out_specs=tuple(out_specs), out_shape=tuple(out_shape),
        compiler_params=_params(("arbitrary",)),
    )(*([x, g, dy] + ([res] if has_res else [])))


def _pool_pre(x, g):
    S, D = x.shape
    ts = _row_tile(S, D)
    gw = D // len(POOL_WINDOWS)
    hb = ts // POOL_HALO

    def body(x_ref, halo_ref, g_ref, o_ref):
        i = pl.program_id(0)
        xx = jnp.concatenate([halo_ref[...], x_ref[...]], axis=0)
        h = xx * _rms_scale(xx) * g_ref[...]
        t = i * ts - POOL_HALO + lax.broadcasted_iota(jnp.int32, (ts + POOL_HALO, 1), 0)
        h = jnp.where(t >= 0, h, 0.0)
        tf = t[POOL_HALO:].astype(F32)
        for gi, w in enumerate(POOL_WINDOWS):
            hg = h[:, gi * gw:(gi + 1) * gw]
            acc, span = hg, 1
            while span < w:
                acc = acc + pltpu.roll(acc, span, 0)
                span *= 2
            count = jnp.minimum(tf + 1.0, float(w))
            diff = acc[POOL_HALO:] / count - hg[POOL_HALO:]
            o_ref[:, gi * gw:(gi + 1) * gw] = diff.astype(o_ref.dtype)

    return _pcall(
        body, name="pool_pre", grid=(S // ts,),
        in_specs=[pl.BlockSpec((ts, D), lambda i: (i, 0)),
                  pl.BlockSpec((POOL_HALO, D), lambda i: (jnp.maximum(i * hb - 1, 0), 0)),
                  pl.BlockSpec((1, D), lambda i: (0, 0))],
        out_specs=pl.BlockSpec((ts, D), lambda i: (i, 0)),
        out_shape=jax.ShapeDtypeStruct((S, D), BF16),
        compiler_params=_params(("parallel",)),
    )(x, x, g)


def _pool_pre_bwd(x, g, dd, res):
    S, D = x.shape
    ts = _row_tile(S, D)
    gw = D // len(POOL_WINDOWS)
    hb = ts // POOL_HALO
    last_halo = S // POOL_HALO - 1

    def body(x_ref, g_ref, dd_ref, halo_ref, res_ref, dx_ref, dg_ref):
        i = pl.program_id(0)
        ee = jnp.concatenate([dd_ref[...], halo_ref[...]], axis=0)
        t = i * ts + lax.broadcasted_iota(jnp.int32, (ts + POOL_HALO, 1), 0)
        ee = jnp.where(t < S, ee, 0.0)
        tf = t.astype(F32)
        n = ts + POOL_HALO
        for gi, w in enumerate(POOL_WINDOWS):
            eg = ee[:, gi * gw:(gi + 1) * gw]
            acc, span = eg / jnp.minimum(tf + 1.0, float(w)), 1
            while span < w:
                acc = acc + pltpu.roll(acc, n - span, 0)
                span *= 2
            dx_ref[:, gi * gw:(gi + 1) * gw] = acc[:ts] - eg[:ts]
        dx, dgt = _rms_bwd(x_ref[...], g_ref[...], dx_ref[...])
        dx_ref[...] = res_ref[...] + dx

        @pl.when(i == 0)
        def _():
            dg_ref[...] = jnp.zeros_like(dg_ref)

        dg_ref[...] += jnp.sum(dgt, axis=0, keepdims=True)

    row = pl.BlockSpec((ts, D), lambda i: (i, 0))
    vec = pl.BlockSpec((1, D), lambda i: (0, 0))
    return _pcall(
        body, name="pool_pre_bwd", grid=(S // ts,),
        in_specs=[row, vec, row,
                  pl.BlockSpec((POOL_HALO, D), lambda i: (jnp.minimum((i + 1) * hb, last_halo), 0)), row],
        out_specs=(row, vec),
        out_shape=(jax.ShapeDtypeStruct((S, D), F32), jax.ShapeDtypeStruct((1, D), F32)),
        compiler_params=_params(("arbitrary",)),
    )(x, g, dd, dd, res)


def _pool_mix(diff, w, scale, x):
    S, D = x.shape
    G, gw, _ = w.shape
    ts = _tile(S, ROW_TILE)

    def body(d_ref, w_ref, s_ref, x_ref, o_ref):
        mo = jnp.dot(d_ref[...], w_ref[0], preferred_element_type=F32)
        o_ref[...] = x_ref[...] + mo * s_ref[...]

    blk = pl.BlockSpec((ts, gw), lambda i, g: (i, g))
    return _pcall(
        body, name="pool_mix", grid=(S // ts, G),
        in_specs=[blk, pl.BlockSpec((1, gw, gw), lambda i, g: (g, 0, 0)), pl.BlockSpec((1, gw), lambda i, g: (0, g)), blk],
        out_specs=blk,
        out_shape=jax.ShapeDtypeStruct((S, D), F32),
        compiler_params=_params(("parallel", "parallel")),
    )(diff, w, scale, x)


def _pool_mix_bwd(diff, w, scale, dy):
    S, D = dy.shape
    G, gw, _ = w.shape
    ts = _tile(S, ROW_TILE)

    def body(d_ref, w_ref, s_ref, dy_ref, dd_ref, dw_ref, ds_ref):
        i = pl.program_id(1)
        d = d_ref[...]
        dy = dy_ref[...]
        mo = jnp.dot(d, w_ref[0], preferred_element_type=F32)
        dmo = (dy * s_ref[...]).astype(BF16)
        dd_ref[...] = lax.dot_general(dmo, w_ref[0], (((1,), (1,)), ((), ())), preferred_element_type=F32)

        @pl.when(i == 0)
        def _():
            dw_ref[...] = jnp.zeros_like(dw_ref)
            ds_ref[...] = jnp.zeros_like(ds_ref)

        dw_ref[0] += lax.dot_general(d, dmo, (((0,), (0,)), ((), ())), preferred_element_type=F32)
        ds_ref[...] += jnp.sum(dy * mo, axis=0, keepdims=True)

    blk = pl.BlockSpec((ts, gw), lambda g, i: (i, g))
    wspec = pl.BlockSpec((1, gw, gw), lambda g, i: (g, 0, 0))
    vec = pl.BlockSpec((1, gw), lambda g, i: (0, g))
    return _pcall(
        body, name="pool_mix_bwd", grid=(G, S // ts),
        in_specs=[blk, wspec, vec, blk],
        out_specs=(blk, wspec, vec),
        out_shape=(jax.ShapeDtypeStruct((S, D), F32), jax.ShapeDtypeStruct((G, gw, gw), F32),
                   jax.ShapeDtypeStruct((1, D), F32)),
        compiler_params=_params(("parallel", "arbitrary")),
    )(diff, w, scale, dy)


def _loss_head(x, g, target):
    S, D = x.shape
    ts = _row_tile(S, D)

    def body(x_ref, g_ref, t_ref, loss_ref, dx_ref, dxb_ref, dg_ref):
        xf = x_ref[...]
        gv = g_ref[...]
        err = xf * _rms_scale(xf) * gv - t_ref[...]
        dx, dgt = _rms_bwd(xf, gv, err * (1.0 / D))
        dx_ref[...] = dx
        dxb_ref[...] = dx.astype(BF16)

        @pl.when(pl.program_id(0) == 0)
        def _():
            loss_ref[...] = jnp.zeros_like(loss_ref)
            dg_ref[...] = jnp.zeros_like(dg_ref)

        part = 0.5 * jnp.sum(jnp.sum(err * err, axis=-1, keepdims=True) * (1.0 / D), axis=0, keepdims=True)
        loss_ref[...] += jnp.broadcast_to(part, loss_ref.shape)
        dg_ref[...] += jnp.sum(dgt, axis=0, keepdims=True)

    row = pl.BlockSpec((ts, D), lambda i: (i, 0))
    vec = pl.BlockSpec((1, D), lambda i: (0, 0))
    return _pcall(
        body, name="loss_head", grid=(S // ts,),
        in_specs=[row, vec, row],
        out_specs=(pl.BlockSpec((1, 128), lambda i: (0, 0)), row, row, vec),
        out_shape=(jax.ShapeDtypeStruct((1, 128), F32), jax.ShapeDtypeStruct((S, D), F32),
                   jax.ShapeDtypeStruct((S, D), BF16), jax.ShapeDtypeStruct((1, D), F32)),
        compiler_params=_params(("arbitrary",)),
    )(x, g, target)


_DIMS = {"nn": (((1,), (0,)), ((), ())), "nt": (((1,), (1,)), ((), ())), "tn": (((0,), (0,)), ((), ()))}


def _mm(name, mode, a, b, tiles, grid, a_map, b_map, o_map, out_shape, out_dtype=F32, res=None, a2=None, b2=None,
        ride=None, epilogue=None):
    tm, tn, tk = tiles
    nk = grid[-1]
    dual, has_res = a2 is not None, res is not None
    n_main = 2 + 2 * dual
    dn = _DIMS[mode]

    def body(ins, outs, scratch):
        o_ref = outs[0]

        def product():
            part = lax.dot_general(ins[0][...].astype(BF16), ins[1][...].astype(BF16), dn, preferred_element_type=F32)
            if dual:
                part += lax.dot_general(ins[2][...].astype(BF16), ins[3][...].astype(BF16), dn,
                                        preferred_element_type=F32)
            return part

        def finish(out):
            if has_res:
                out = ins[n_main][...] + out
            if epilogue is not None:
                out = epilogue[0](out, *[r[...] for r in ins[n_main + has_res:]])
            o_ref[...] = out.astype(o_ref.dtype)

        if nk == 1:
            finish(product())
            return
        acc_ref = scratch[0]
        k = pl.program_id(2)

        @pl.when(k == 0)
        def _():
            acc_ref[...] = jnp.zeros_like(acc_ref)

        acc_ref[...] += product()

        @pl.when(k == nk - 1)
        def _():
            finish(acc_ref[...])

    a_spec = pl.BlockSpec((tk, tm) if mode == "tn" else (tm, tk), a_map)
    b_spec = pl.BlockSpec((tn, tk) if mode == "nt" else (tk, tn), b_map)
    o_spec = pl.BlockSpec((tm, tn), o_map)
    operands, in_specs = [a, b], [a_spec, b_spec]
    if dual:
        operands += [a2, b2]
        in_specs += [a_spec, b_spec]
    if has_res:
        operands.append(res)
        in_specs.append(o_spec)
    if epilogue is not None:
        operands += list(epilogue[1])
        in_specs += list(epilogue[2])
    outs = _pallas(body, name, grid, operands, in_specs, [o_spec], [jax.ShapeDtypeStruct(out_shape, out_dtype)],
                   scratch=[pltpu.VMEM((tm, tn), F32)] if nk > 1 else [],
                   sem=("parallel", "parallel", "arbitrary"), ride=ride)
    return (outs[0], outs[1:]) if ride else outs[0]


def _mm_plain(name, mode, a, b, out_dtype=F32, res=None):
    if mode == "tn":
        (K, M), N = a.shape, b.shape[1]
    elif mode == "nt":
        (M, K), N = a.shape, b.shape[0]
    else:
        (M, K), N = a.shape, b.shape[1]
    if mode == "tn":
        tm, tn, tk = _tile(M, 2 * MM_TN), _tile(N, 2 * MM_TN), _tile(K, ROW_TILE)
    else:
        tm = _tile(M, ROW_TILE)
        tk = K if K <= 4 * MM_TK else _tile(K, MM_TK)
        tn = N if N * tk * 2 <= (8 << 20) else _tile(N, MM_TN)
    a_map = (lambda i, j, k: (k, i)) if mode == "tn" else (lambda i, j, k: (i, k))
    b_map = (lambda i, j, k: (j, k)) if mode == "nt" else (lambda i, j, k: (k, j))
    return _mm(name, mode, a, b, (tm, tn, tk), (M // tm, N // tn, K // tk), a_map, b_map, lambda i, j, k: (i, j),
               (M, N), out_dtype, res)


def _slots_out(name, a, w_slots, out_dtype, epilogue=None):
    S, K = a.shape
    n = w_slots.shape[1]
    tm = _tile(S, ROW_TILE)
    nmb = S // tm
    return _mm(name, "nn", a, w_slots, (tm, n, K), (nmb, N_DEV, 1),
               lambda i, j, k: (i, 0), lambda i, j, k: (j, 0), lambda i, j, k: (j * nmb + i, 0),
               (N_DEV * S, n), out_dtype, epilogue=epilogue)


def _slots_in_nt(name, a_slots, w_slots, S, a2=None, w2=None, ride=None):
    n = a_slots.shape[1]
    K = w_slots.shape[0] // N_DEV
    tm = _tile(S, ROW_TILE)
    nmb = S // tm
    return _mm(name, "nt", a_slots, w_slots, (tm, K, n), (nmb, 1, N_DEV),
               lambda i, q, j: (j * nmb + i, 0), lambda i, q, j: (j, 0), lambda i, q, j: (i, 0),
               (S, K), F32, a2=a2, b2=w2, ride=ride)


def _slots_dw(name, a, d_slots, out_dtype=F32, ride=None):
    S, K = a.shape
    n = d_slots.shape[1]
    tk = _tile(S, ROW_TILE)
    nkb = S // tk
    return _mm(name, "tn", a, d_slots, (K, n, tk), (N_DEV, 1, nkb),
               lambda j, q, k: (k, 0), lambda j, q, k: (j * nkb + k, 0), lambda j, q, k: (j, 0),
               (N_DEV * K, n), out_dtype, ride=ride)


def _sigmoid(x):
    return 1.0 / (1.0 + jnp.exp(-x))


def _ffn_up(h, wg, wu, name, ride=None):
    S, D = h.shape
    fs = wg.shape[1]
    tm = _tile(S, ROW_TILE)
    nmb = S // tm

    def body(ins, outs, scratch):
        h_ref, wg_ref, wu_ref = ins
        g_ref, u_ref, a_ref = outs
        hv = h_ref[...]
        g = jnp.dot(hv, wg_ref[...], preferred_element_type=F32)
        u = jnp.dot(hv, wu_ref[...], preferred_element_type=F32)
        g_ref[...] = g
        u_ref[...] = u
        a_ref[...] = (g * _sigmoid(g) * u).astype(a_ref.dtype)

    w_spec = pl.BlockSpec((D, fs), lambda i, j: (j, 0))
    o_spec = pl.BlockSpec((tm, fs), lambda i, j: (j * nmb + i, 0))
    sds = jax.ShapeDtypeStruct((N_DEV * S, fs), F32)
    return _pallas(body, name, (nmb, N_DEV), [h, wg, wu],
                   [pl.BlockSpec((tm, D), lambda i, j: (i, 0)), w_spec, w_spec], [o_spec, o_spec, o_spec],
                   [sds, sds, jax.ShapeDtypeStruct((N_DEV * S, fs), BF16)], sem=("parallel", "arbitrary"), ride=ride)


def _ffn_dact(dy, wd, g, u, name, ride=None):
    S, D = dy.shape
    fs = g.shape[1]
    tm = _tile(S, ROW_TILE)
    nmb = S // tm

    def body(ins, outs, scratch):
        dy_ref, wd_ref, g_ref, u_ref = ins
        dg_ref, du_ref = outs
        da = lax.dot_general(dy_ref[...], wd_ref[...], _DIMS["nt"], preferred_element_type=F32)
        g, u = g_ref[...], u_ref[...]
        sig = _sigmoid(g)
        dg_ref[...] = (da * u * (sig * (1.0 + g * (1.0 - sig)))).astype(dg_ref.dtype)
        du_ref[...] = (da * (g * sig)).astype(du_ref.dtype)

    o_spec = pl.BlockSpec((tm, fs), lambda i, j: (j * nmb + i, 0))
    sds = jax.ShapeDtypeStruct((N_DEV * S, fs), BF16)
    return _pallas(body, name, (nmb, N_DEV), [dy, wd, g, u],
                   [pl.BlockSpec((tm, D), lambda i, j: (i, 0)), pl.BlockSpec((fs, D), lambda i, j: (j, 0)),
                    o_spec, o_spec], [o_spec, o_spec], [sds, sds], sem=("parallel", "arbitrary"), ride=ride)


def _rope_heads(q, cos_t, sin_t):
    parts = []
    for c0 in range(0, q.shape[1], HEAD_PAD):
        parts += [q[:, c0:c0 + QK_NOPE_DIM], _rope(q[:, c0 + QK_NOPE_DIM:c0 + HEAD_PAD], cos_t, sin_t)]
    return jnp.concatenate(parts, axis=1)


def _kv_mid(kv, g, cos_t, sin_t):
    S, W = kv.shape
    R = W - 128
    ts = _tile(S, ROW_TILE)

    def body(kv_ref, g_ref, c_ref, s_ref, c_out, k_out):
        cf = kv_ref[:, :R]
        c_out[...] = (cf * _rms_scale(cf) * g_ref[...]).astype(c_out.dtype)
        k_out[...] = _rope(kv_ref[:, R:], c_ref[...], s_ref[...]).astype(k_out.dtype)

    tab = pl.BlockSpec((ts, 128), lambda i: (i, 0))
    return _pcall(
        body, name="kv_mid", grid=(S // ts,),
        in_specs=[pl.BlockSpec((ts, W), lambda i: (i, 0)), pl.BlockSpec((1, R), lambda i: (0, 0)), tab, tab],
        out_specs=(pl.BlockSpec((ts, R), lambda i: (i, 0)), tab),
        out_shape=(jax.ShapeDtypeStruct((S, R), BF16), jax.ShapeDtypeStruct((S, 128), BF16)),
        compiler_params=_params(("parallel",)),
    )(kv, g, cos_t, sin_t)


def _kv_mid_bwd(kv, g, dc, dkpe, cos_t, sin_t):
    S, W = kv.shape
    R = W - 128
    H = dkpe.shape[0]
    ts = _row_tile(S, H * 128)

    def body(kv_ref, g_ref, dc_ref, dk_ref, c_ref, s_ref, dkv_ref, dg_ref):
        dx, dgt = _rms_bwd(kv_ref[:, :R], g_ref[...], dc_ref[...])
        dkv_ref[:, :R] = dx.astype(dkv_ref.dtype)
        dk = dk_ref[0]
        for h in range(1, H):
            dk = dk + dk_ref[h]
        dkv_ref[:, R:] = _rope(dk, c_ref[...], -s_ref[...]).astype(dkv_ref.dtype)

        @pl.when(pl.program_id(0) == 0)
        def _():
            dg_ref[...] = jnp.zeros_like(dg_ref)

        dg_ref[...] += jnp.sum(dgt, axis=0, keepdims=True)

    tab = pl.BlockSpec((ts, 128), lambda i: (i, 0))
    vec = pl.BlockSpec((1, R), lambda i: (0, 0))
    return _pcall(
        body, name="kv_mid_bwd", grid=(S // ts,),
        in_specs=[pl.BlockSpec((ts, W), lambda i: (i, 0)), vec, pl.BlockSpec((ts, R), lambda i: (i, 0)),
                  pl.BlockSpec((H, ts, 128), lambda i: (0, i, 0)), tab, tab],
        out_specs=(pl.BlockSpec((ts, W), lambda i: (i, 0)), vec),
        out_shape=(jax.ShapeDtypeStruct((S, W), BF16), jax.ShapeDtypeStruct((1, R), F32)),
        compiler_params=_params(("arbitrary",)),
    )(kv, g, dc, dkpe, cos_t, sin_t)


def _block_pairs(nb, by_key):
    pairs = ([(qi, ki) for ki in range(nb) for qi in range(ki, nb)] if by_key
             else [(qi, ki) for qi in range(nb) for ki in range(qi + 1)])
    return jnp.asarray(np.array([p[0] for p in pairs], np.int32)), jnp.asarray(np.array([p[1] for p in pairs], np.int32))


def _causal_mask(blk, transposed=False):
    r = lax.broadcasted_iota(jnp.int32, (blk, blk), 0)
    c = lax.broadcasted_iota(jnp.int32, (blk, blk), 1)
    return (r <= c) if transposed else (c <= r)


def _attn_specs(S, blk, hpd):
    nb = S // blk
    w = hpd * HEAD_PAD
    return dict(
        q=pl.BlockSpec((blk, w), lambda j, t, qt, kt: (j * nb + qt[t], 0)),
        kv=pl.BlockSpec((blk, w), lambda j, t, qt, kt: (j * nb + kt[t], 0)),
        kp=pl.BlockSpec((blk, 128), lambda j, t, qt, kt: (kt[t], 0)),
        do=pl.BlockSpec((blk, hpd * V_DIM), lambda j, t, qt, kt: (qt[t], j)),
        col=pl.BlockSpec((hpd, blk, 1), lambda j, t, qt, kt: (j, qt[t], 0)),
        row=pl.BlockSpec((hpd, 1, blk), lambda j, t, qt, kt: (j, 0, qt[t])),
        tab=pl.BlockSpec((blk, 128), lambda j, t, qt, kt: (qt[t], 0)))


def _head_k(kv_ref, kp, hh):
    return jnp.concatenate([kv_ref[:, hh * HEAD_PAD:hh * HEAD_PAD + QK_NOPE_DIM], kp], axis=1)


def _head_v(kv_ref, hh):
    return kv_ref[:, hh * HEAD_PAD + QK_NOPE_DIM:(hh + 1) * HEAD_PAD]


def _attn_fwd(q, kvu, kpe, H, ride=None):
    S = kpe.shape[0]
    hpd = H // N_DEV
    blk = _tile(S, ATT_BLOCK)
    q_tab, k_tab = _block_pairs(S // blk, by_key=False)

    def body(ins, outs, scratch):
        qt, kt, q_ref, kv_ref, kp_ref = ins
        o_ref, lse_ref = outs
        m_ref, acc_ref = scratch
        t = pl.program_id(1)
        qi, ki = qt[t], kt[t]

        @pl.when(ki == 0)
        def _():
            m_ref[...] = jnp.full_like(m_ref, -jnp.inf)
            acc_ref[...] = jnp.zeros_like(acc_ref)

        def step(masked):
            kp = kp_ref[...]
            ones = jnp.ones((blk, 128), BF16)
            for hh in range(hpd):
                s = lax.dot_general(q_ref[:, hh * HEAD_PAD:(hh + 1) * HEAD_PAD], _head_k(kv_ref, kp, hh), _DIMS["nt"],
                                    preferred_element_type=F32)
                if masked:
                    s = jnp.where(_causal_mask(blk), s, -jnp.inf)
                m_old = m_ref[hh]
                m_new = jnp.maximum(m_old, jnp.max(s, axis=-1, keepdims=True))
                p = jnp.exp2((s - m_new) * EXP2_SCALE).astype(BF16)
                v_ext = jnp.concatenate([_head_v(kv_ref, hh), ones], axis=1)
                acc_ref[hh] = (jnp.exp2((m_old - m_new) * EXP2_SCALE) * acc_ref[hh]
                               + jnp.dot(p, v_ext, preferred_element_type=F32))
                m_ref[hh] = m_new

        @pl.when(ki < qi)
        def _():
            step(False)

        @pl.when(ki == qi)
        def _():
            step(True)
            for hh in range(hpd):
                l = acc_ref[hh, :, V_DIM:]
                o_ref[:, hh * V_DIM:(hh + 1) * V_DIM] = (acc_ref[hh, :, :V_DIM] / l).astype(o_ref.dtype)
                lse_ref[hh] = m_ref[hh] * EXP2_SCALE + jnp.log2(l[:, :1])

    sp = _attn_specs(S, blk, hpd)
    return _pallas(body, "attn_fwd", (N_DEV, q_tab.shape[0]), [q, kvu, kpe], [sp["q"], sp["kv"], sp["kp"]],
                   [sp["do"], sp["col"]],
                   [jax.ShapeDtypeStruct((S, H * V_DIM), BF16), jax.ShapeDtypeStruct((H, S, 1), F32)],
                   scratch=[pltpu.VMEM((hpd, blk, 1), F32), pltpu.VMEM((hpd, blk, 2 * V_DIM), F32)],
                   sem=("parallel", "arbitrary"), ride=ride, prefetch=[q_tab, k_tab])


def _attn_dq(q, kvu, kpe, do, o, lse, cos_t, sin_t, H):
    S = kpe.shape[0]
    hpd = H // N_DEV
    blk = _tile(S, ATT_BLOCK)
    q_tab, k_tab = _block_pairs(S // blk, by_key=False)

    def body(qt, kt, q_ref, kv_ref, kp_ref, do_ref, o_ref, lse_ref, c_ref, s_ref, dq_ref, dl_ref, acc_ref):
        t = pl.program_id(1)
        qi, ki = qt[t], kt[t]

        @pl.when(ki == 0)
        def _():
            acc_ref[...] = jnp.zeros_like(acc_ref)
            for hh in range(hpd):
                cols = slice(hh * V_DIM, (hh + 1) * V_DIM)
                dl_ref[hh] = jnp.sum(do_ref[:, cols].astype(F32) * o_ref[:, cols].astype(F32), axis=-1, keepdims=True)

        def step(masked):
            kp = kp_ref[...]
            for hh in range(hpd):
                k = _head_k(kv_ref, kp, hh)
                s = lax.dot_general(q_ref[:, hh * HEAD_PAD:(hh + 1) * HEAD_PAD], k, _DIMS["nt"],
                                    preferred_element_type=F32)
                p = jnp.exp2(s * EXP2_SCALE - lse_ref[hh])
                if masked:
                    p = jnp.where(_causal_mask(blk), p, 0.0)
                dp = lax.dot_general(do_ref[:, hh * V_DIM:(hh + 1) * V_DIM], _head_v(kv_ref, hh), _DIMS["nt"],
                                     preferred_element_type=F32)
                ds = (p * (dp - dl_ref[hh])).astype(BF16)
                acc_ref[hh] += jnp.dot(ds, k, preferred_element_type=F32)

        @pl.when(ki < qi)
        def _():
            step(False)

        @pl.when(ki == qi)
        def _():
            step(True)
            for hh in range(hpd):
                c0 = hh * HEAD_PAD
                dq_ref[:, c0:c0 + QK_NOPE_DIM] = (acc_ref[hh, :, :QK_NOPE_DIM] * ATTN_SCALE).astype(dq_ref.dtype)
                dq_ref[:, c0 + QK_NOPE_DIM:c0 + HEAD_PAD] = _rope(
                    acc_ref[hh, :, QK_NOPE_DIM:] * ATTN_SCALE, c_ref[...], -s_ref[...]).astype(dq_ref.dtype)

    sp = _attn_specs(S, blk, hpd)
    grid_spec = pltpu.PrefetchScalarGridSpec(
        num_scalar_prefetch=2, grid=(N_DEV, q_tab.shape[0]),
        in_specs=[sp["q"], sp["kv"], sp["kp"], sp["do"], sp["do"], sp["col"], sp["tab"], sp["tab"]],
        out_specs=(sp["q"], sp["col"]),
        scratch_shapes=[pltpu.VMEM((hpd, blk, HEAD_PAD), F32)])
    return _pcall(
        body, name="attn_dq", grid_spec=grid_spec,
        out_shape=(jax.ShapeDtypeStruct(q.shape, BF16), jax.ShapeDtypeStruct((H, S, 1), F32)),
        compiler_params=_params(("parallel", "arbitrary")),
    )(q_tab, k_tab, q, kvu, kpe, do, o, lse, cos_t, sin_t)


def _attn_dkv(q, kvu, kpe, do, lse_row, delta_row, H):
    S = kpe.shape[0]
    hpd = H // N_DEV
    blk = _tile(S, ATT_BLOCK)
    nb = S // blk
    q_tab, k_tab = _block_pairs(nb, by_key=True)

    def body(qt, kt, q_ref, kv_ref, kp_ref, do_ref, lse_ref, dl_ref, dkv_ref, dkp_ref, dk_acc, dv_acc):
        t = pl.program_id(1)
        qi, ki = qt[t], kt[t]

        def step(masked):
            kp = kp_ref[...]
            for hh in range(hpd):
                qv = q_ref[:, hh * HEAD_PAD:(hh + 1) * HEAD_PAD]
                dov = do_ref[:, hh * V_DIM:(hh + 1) * V_DIM]
                st = lax.dot_general(_head_k(kv_ref, kp, hh), qv, _DIMS["nt"],
                                     preferred_element_type=F32)
                pt = jnp.exp2(st * EXP2_SCALE - lse_ref[hh])
                if masked:
                    pt = jnp.where(_causal_mask(blk, transposed=True), pt, 0.0)
                dv_new = jnp.dot(pt.astype(BF16), dov, preferred_element_type=F32)
                dpt = lax.dot_general(_head_v(kv_ref, hh), dov, _DIMS["nt"], preferred_element_type=F32)
                dst = (pt * (dpt - dl_ref[hh])).astype(BF16)
                dk_new = jnp.dot(dst, qv, preferred_element_type=F32)
                if masked:
                    dv_acc[hh] = dv_new
                    dk_acc[hh] = dk_new
                else:
                    dv_acc[hh] += dv_new
                    dk_acc[hh] += dk_new

        @pl.when(qi == ki)
        def _():
            step(True)

        @pl.when(qi > ki)
        def _():
            step(False)

        @pl.when(qi == nb - 1)
        def _():
            for hh in range(hpd):
                c0 = hh * HEAD_PAD
                dkv_ref[:, c0:c0 + QK_NOPE_DIM] = (dk_acc[hh, :, :QK_NOPE_DIM] * ATTN_SCALE).astype(dkv_ref.dtype)
                dkv_ref[:, c0 + QK_NOPE_DIM:c0 + HEAD_PAD] = dv_acc[hh].astype(dkv_ref.dtype)
                dkp_ref[hh] = dk_acc[hh, :, QK_NOPE_DIM:] * ATTN_SCALE

    sp = _attn_specs(S, blk, hpd)
    grid_spec = pltpu.PrefetchScalarGridSpec(
        num_scalar_prefetch=2, grid=(N_DEV, q_tab.shape[0]),
        in_specs=[sp["q"], sp["kv"], sp["kp"], sp["do"], sp["row"], sp["row"]],
        out_specs=(sp["kv"], pl.BlockSpec((hpd, blk, 128), lambda j, t, qt, kt: (j, kt[t], 0))),
        scratch_shapes=[pltpu.VMEM((hpd, blk, HEAD_PAD), F32), pltpu.VMEM((hpd, blk, V_DIM), F32)])
    return _pcall(
        body, name="attn_dkv", grid_spec=grid_spec,
        out_shape=(jax.ShapeDtypeStruct(kvu.shape, BF16), jax.ShapeDtypeStruct((H, S, 128), F32)),
        compiler_params=_params(("parallel", "arbitrary")),
    )(q_tab, k_tab, q, kvu, kpe, do, lse_row, delta_row)


def _mesh_pos():
    return lax.axis_index("x"), lax.axis_index("y"), lax.axis_index("c")


def _flip(pos, k):
    x, y, c = pos
    return (1 - x if k & 4 else x, 1 - y if k & 2 else y, 1 - c if k & 1 else c)


def _rank(pos):
    return 4 * pos[0] + 2 * pos[1] + pos[2]


def _copies(n, src_of, dst_refs, local_src_of, send_sems, recv_sems, local_sems):
    me = _mesh_pos()
    local = [pltpu.make_async_copy(local_src_of(a), dst_refs[a].at[_rank(me)], local_sems.at[a]) for a in range(n)]
    sends, arrivals = [], []
    for k in range(1, N_DEV):
        peer = _flip(me, k)
        for a in range(n):
            idx = a * (N_DEV - 1) + k - 1
            for into, group in ((me, sends), (peer, arrivals)):
                group.append(pltpu.make_async_remote_copy(
                    src_ref=src_of(a, peer), dst_ref=dst_refs[a].at[_rank(into)],
                    send_sem=send_sems.at[idx], recv_sem=recv_sems.at[idx],
                    device_id=peer, device_id_type=pl.DeviceIdType.MESH))
    return local, sends, arrivals


def _exchange_start(*args):
    local, sends, _ = _copies(*args)
    for cp in local + sends:
        cp.start()


def _exchange_wait(*args):
    local, sends, arrivals = _copies(*args)
    for cp in arrivals:
        cp.wait_recv()
    for cp in sends:
        cp.wait_send()
    for cp in local:
        cp.wait()


def _exchange(*args):
    _exchange_start(*args)
    _exchange_wait(*args)


def _sems(n):
    return [pltpu.SemaphoreType.DMA((n * (N_DEV - 1),)), pltpu.SemaphoreType.DMA((n * (N_DEV - 1),)),
            pltpu.SemaphoreType.DMA((n,))]


_ANY = pl.BlockSpec(memory_space=pl.ANY)


class _Ride:
    def __init__(self, kind, arrays):
        self.kind, self.arrays, self.n = kind, list(arrays), len(arrays)

    def out_shape(self):
        lead = (N_DEV,) if self.kind == "gather" else ()
        return [jax.ShapeDtypeStruct(lead + a.shape, a.dtype) for a in self.arrays]

    def _args(self, x_refs, out_refs, sems):
        if self.kind == "gather":
            return (self.n, lambda a, peer: x_refs[a], out_refs, lambda a: x_refs[a]) + tuple(sems)
        me = _mesh_pos()
        return (self.n, lambda a, peer: x_refs[a].at[_rank(peer)], out_refs, lambda a: x_refs[a].at[_rank(me)]) + tuple(sems)

    def start(self, x_refs, out_refs, sems):
        _exchange_start(*self._args(x_refs, out_refs, sems))

    def wait(self, x_refs, out_refs, sems):
        _exchange_wait(*self._args(x_refs, out_refs, sems))


def _pallas(body, name, grid, operands, in_specs, out_specs, out_shape, scratch=(), sem=None, ride=None, prefetch=()):
    n_pf, n_in, n_out, n_scr = len(prefetch), len(in_specs), len(out_specs), len(scratch)
    nr = ride.n if ride else 0

    def wrapped(*refs):
        refs = list(refs)
        pf, refs = refs[:n_pf], refs[n_pf:]
        ins, r_in = refs[:n_in], refs[n_in:n_in + nr]
        outs, r_out = refs[n_in + nr:n_in + nr + n_out], refs[n_in + nr + n_out:n_in + 2 * nr + n_out]
        rest = refs[n_in + 2 * nr + n_out:]
        scr, sems = rest[:n_scr], rest[n_scr:]
        if ride:
            first, last = None, None
            for d, size in enumerate(grid):
                i = pl.program_id(d)
                first = (i == 0) if first is None else jnp.logical_and(first, i == 0)
                last = (i == size - 1) if last is None else jnp.logical_and(last, i == size - 1)

            @pl.when(first)
            def _():
                ride.start(r_in, r_out, sems)

        body(pf + ins, outs, scr)
        if ride:
            @pl.when(last)
            def _():
                ride.wait(r_in, r_out, sems)

    grid_spec = pltpu.PrefetchScalarGridSpec(
        num_scalar_prefetch=n_pf, grid=grid,
        in_specs=list(in_specs) + [_ANY] * nr, out_specs=tuple(list(out_specs) + [_ANY] * nr),
        scratch_shapes=list(scratch) + (_sems(nr) if ride else []))
    sem = tuple(sem) if sem is not None and not ride else ("arbitrary",) * len(grid)
    return list(_pcall(
        wrapped, name=name, grid_spec=grid_spec,
        out_shape=tuple(list(out_shape) + (ride.out_shape() if ride else [])),
        compiler_params=_params(sem),
    )(*(list(prefetch) + list(operands) + (ride.arrays if ride else []))))


def _all_gather(shards, name):
    n = len(shards)

    def body(*refs):
        x_refs, out_refs = refs[:n], refs[n:2 * n]
        _exchange(n, lambda a, peer: x_refs[a], out_refs, lambda a: x_refs[a], *refs[2 * n:])

    return _pcall(
        body, name=name, in_specs=[_ANY] * n, out_specs=tuple([_ANY] * n),
        out_shape=tuple(jax.ShapeDtypeStruct((N_DEV,) + s.shape, s.dtype) for s in shards),
        scratch_shapes=_sems(n),
    )(*shards)


def _all_to_all(parts, name):
    n = len(parts)

    def body(*refs):
        x_refs, out_refs = refs[:n], refs[n:2 * n]
        me = _mesh_pos()
        _exchange(n, lambda a, peer: x_refs[a].at[_rank(peer)], out_refs, lambda a: x_refs[a].at[_rank(me)],
                  *refs[2 * n:])

    return _pcall(
        body, name=name, in_specs=[_ANY] * n, out_specs=tuple([_ANY] * n),
        out_shape=tuple(jax.ShapeDtypeStruct(p.shape, p.dtype) for p in parts),
        scratch_shapes=_sems(n),
    )(*parts)


def _all_reduce_small(v):
    R, C = v.shape

    def body(x_ref, out_ref, buf_ref, send_sems, recv_sems, local_sems):
        _exchange(1, lambda a, peer: x_ref, [buf_ref], lambda a: x_ref, send_sems, recv_sems, local_sems)
        total = buf_ref[0]
        for j in range(1, N_DEV):
            total = total + buf_ref[j]
        out_ref[...] = total

    vm = pl.BlockSpec(memory_space=pltpu.VMEM)
    return _pcall(
        body, name="all_reduce_small", in_specs=[vm], out_specs=vm,
        out_shape=jax.ShapeDtypeStruct((R, C), F32),
        scratch_shapes=[pltpu.VMEM((N_DEV, R, C), F32)] + _sems(1),
    )(v)


def _sum_slots(parts, name):
    shape = parts.shape[1:]
    C = shape[-1]
    R = parts.size // (N_DEV * C)
    tr = R
    while N_DEV * tr * C * 4 > (4 << 20) and tr % 32 == 0:
        tr //= 2

    def body(p_ref, o_ref):
        total = p_ref[0].astype(F32)
        for j in range(1, N_DEV):
            total = total + p_ref[j].astype(F32)
        o_ref[...] = total

    return _pcall(
        body, name=name, grid=(R // tr,),
        in_specs=[pl.BlockSpec((N_DEV, tr, C), lambda i: (0, i, 0))],
        out_specs=pl.BlockSpec((tr, C), lambda i: (i, 0)),
        out_shape=jax.ShapeDtypeStruct((R, C), F32),
        compiler_params=_params(("parallel",)),
    )(parts.reshape(N_DEV, R, C)).reshape(shape)


def _adamw(w, g, m, v, name):
    shape = w.shape
    C = shape[-1]
    R = w.size // C
    tr = R
    while tr * C * 4 > (1 << 20) and tr % 16 == 0:
        tr //= 2

    def body(w_ref, g_ref, m_ref, v_ref, d_ref, nm_ref, nv_ref):
        gv = g_ref[...]
        nm = ADAM_B1 * m_ref[...] + (1.0 - ADAM_B1) * gv
        nv = ADAM_B2 * v_ref[...] + (1.0 - ADAM_B2) * (gv * gv)
        m_hat = nm / (1.0 - ADAM_B1 ** ADAM_STEP)
        v_hat = nv / (1.0 - ADAM_B2 ** ADAM_STEP)
        d_ref[...] = -ADAM_LR * (m_hat / (jnp.sqrt(v_hat) + ADAM_EPS) + ADAM_WD * w_ref[...])
        nm_ref[...] = nm
        nv_ref[...] = nv

    blk = pl.BlockSpec((tr, C), lambda i: (i, 0))
    sds = jax.ShapeDtypeStruct((R, C), F32)
    outs = _pcall(
        body, name=name, grid=(R // tr,),
        in_specs=[blk] * 4, out_specs=(blk,) * 3, out_shape=(sds,) * 3,
        compiler_params=_params(("parallel",)),
    )(*(t.reshape(R, C) for t in (w, g, m, v)))
    return tuple(o.reshape(shape) for o in outs)


_REPLICATED = ("kv_in_norm", "kv_latent_norm", "attn_norm", "q_latent_norm", "ffn_norm", "final_norm")
_WEIGHTS = ("pool_norm", "pool_w", "pool_scale", "kv_in_norm", "w_kv_a", "kv_latent_norm", "w_kv_b", "attn_norm",
            "w_q_a", "q_latent_norm", "w_q_b", "w_o", "ffn_norm", "w_gate", "w_up", "w_down", "final_norm")


def _ffn_fwd(x, norm_g, wg, wu, wd, tag, gather=None):
    S, D = x.shape
    fs = wg.shape[1]
    tm = _tile(S, ROW_TILE)
    nmb = S // tm
    h = _rmsnorm(x, norm_g, "ffn_norm" + tag)
    g, u, a, *gathered = _ffn_up(h, wg, wu, "ffn_up" + tag, ride=gather)
    if gather is not None:
        wd = gathered[0].reshape(N_DEV * fs, D)
    y = _mm("ffn_down" + tag, "nn", a, wd, (tm, D, fs), (nmb, 1, N_DEV),
            lambda i, q, j: (j * nmb + i, 0), lambda i, q, j: (j, 0), lambda i, q, j: (i, 0), (S, D), F32, res=x)
    return y, (h, g, u, a), wd, gathered


def _ffn_bwd(x, norm_g, wg, wu, wd, saved, dy, dyb, tag, with_bf16, scatter=None):
    S, D = x.shape
    h, g, u, a = saved
    fs = g.shape[1]
    tm = _tile(S, ROW_TILE)
    nmb = S // tm
    dg, du, *got_rider = _ffn_dact(dyb, wd, g, u, "ffn_dact" + tag, ride=scatter)
    dwd = _mm("ffn_dwd" + tag, "tn", a, dyb, (fs, D, tm), (N_DEV, 1, nmb),
              lambda j, q, k: (j * nmb + k, 0), lambda j, q, k: (k, 0), lambda j, q, k: (j, 0), (N_DEV * fs, D), BF16)
    dwg, (got_dwd,) = _slots_dw("ffn_dwg" + tag, h, dg, BF16, ride=_Ride("scatter", [dwd.reshape(N_DEV, fs, D)]))
    dwu, (got_dwg,) = _slots_dw("ffn_dwu" + tag, h, du, BF16, ride=_Ride("scatter", [dwg.reshape(N_DEV, D, fs)]))
    dh, (got_dwu,) = _mm("ffn_dh" + tag, "nt", dg, wg, (tm, D, fs), (nmb, 1, N_DEV),
                         lambda i, q, j: (j * nmb + i, 0), lambda i, q, j: (j, 0), lambda i, q, j: (i, 0),
                         (S, D), F32, a2=du, b2=wu, ride=_Ride("scatter", [dwu.reshape(N_DEV, D, fs)]))
    outs = _rmsnorm_bwd(x, norm_g, dh, dy, "ffn_norm_bwd" + tag, with_bf16=with_bf16)
    return outs, got_rider, (got_dwg, got_dwu, got_dwd)


def kernel(x, positions, pool_norm, pool_w, pool_scale, kv_in_norm, w_kv_a, kv_latent_norm, w_kv_b, attn_norm, w_q_a, q_latent_norm, w_q_b, w_o, ffn_norm, w_gate, w_up, w_down, final_norm, loss_target, m_pool_norm, m_pool_w, m_pool_scale, m_kv_in_norm, m_w_kv_a, m_kv_latent_norm, m_w_kv_b, m_attn_norm, m_w_q_a, m_q_latent_norm, m_w_q_b, m_w_o, m_ffn_norm, m_w_gate, m_w_up, m_w_down, m_final_norm, v_pool_norm, v_pool_w, v_pool_scale, v_kv_in_norm, v_w_kv_a, v_kv_latent_norm, v_w_kv_b, v_attn_norm, v_w_q_a, v_q_latent_norm, v_w_q_b, v_w_o, v_ffn_norm, v_w_gate, v_w_up, v_w_down, v_final_norm):
    env = dict(locals())
    weights = {n: env[n] for n in _WEIGHTS}
    m_in = {n: env["m_" + n] for n in _WEIGHTS}
    v_in = {n: env["v_" + n] for n in _WEIGHTS}

    S, D = x.shape[1], x.shape[2]
    xs = x.reshape(S, D)
    target = loss_target.reshape(S, D)
    G, gws, gw = pool_w.shape[1:]
    Ds = w_kv_a.shape[0]
    R_kv, kvw = w_kv_b.shape
    hpd = kvw // (QK_NOPE_DIM + V_DIM)
    H = hpd * N_DEV
    R_q = w_q_a.shape[2]
    fs = w_gate.shape[2]
    bf = lambda t: t.astype(BF16)

    gains = jnp.concatenate([pool_norm, pool_scale], axis=0)
    g_pool_w, g_gains, g_wg0, g_wu0 = _all_gather([bf(pool_w[0]), gains, bf(w_gate[0]), bf(w_up[0])], "gather_first")
    wqb_pad = jnp.pad(w_q_b.reshape(R_q, hpd, QK_DIM), ((0, 0), (0, 0), (0, HEAD_PAD - QK_DIM))).reshape(R_q, hpd * HEAD_PAD)
    gather_mla = _Ride("gather", [bf(w_down[0]), bf(jnp.pad(w_kv_a, ((0, 0), (0, 128 - QK_ROPE_DIM)))), bf(w_kv_b),
                                  bf(w_q_a[0]), bf(wqb_pad), bf(w_o[0])])
    gather_ffn1 = _Ride("gather", [bf(w_gate[1]), bf(w_up[1]), bf(w_down[1])])

    wp = jnp.swapaxes(g_pool_w, 0, 1).reshape(G, gw, gw)
    gains_full = jnp.swapaxes(g_gains, 0, 1).reshape(2, D)
    g_pool_norm, g_pool_scale = gains_full[0:1], gains_full[1:2]
    wg0, wu0 = g_wg0.reshape(N_DEV * D, fs), g_wu0.reshape(N_DEV * D, fs)

    g_kv_in = kv_in_norm.reshape(1, D)
    g_kv_lat = kv_latent_norm.reshape(1, R_kv)
    g_attn = attn_norm.reshape(1, D)
    g_q_lat = q_latent_norm.reshape(1, R_q)
    g_final = final_norm.reshape(1, D)

    half = QK_ROPE_DIM // 2
    inv_freq = ROPE_BASE ** (-jnp.arange(half, dtype=F32) / half)
    ang = positions.reshape(S).astype(F32)[:, None] * inv_freq
    cos, sin = jnp.cos(ang), jnp.sin(ang)
    zeros = jnp.zeros((S, 128 - QK_ROPE_DIM), F32)
    cos_t = jnp.concatenate([cos, cos, zeros], axis=1)
    sin_t = jnp.concatenate([-sin, sin, zeros], axis=1)

    diff = _pool_pre(xs, g_pool_norm)
    x1 = _pool_mix(diff, wp, g_pool_scale, xs)
    x2, ffn0, wd0, (_, g_wkva, g_wkvb, g_wqa, g_wqb, g_wo) = _ffn_fwd(x1, ffn_norm[0:1], wg0, wu0, None, "0", gather_mla)
    wkva = g_wkva.reshape(D, R_kv + 128)
    wkvb = g_wkvb.reshape(N_DEV * R_kv, kvw)
    wqa = g_wqa.reshape(D, R_q)
    wqb = g_wqb.reshape(N_DEV * R_q, hpd * HEAD_PAD)
    wo = g_wo.reshape(H * V_DIM, D)

    hk = _rmsnorm(x2, g_kv_in, "kv_in_norm")
    kv = _mm_plain("kv_a", "nn", hk, wkva)
    ckv, kpe = _kv_mid(kv, g_kv_lat, cos_t, sin_t)
    kvu = _slots_out("kv_b", ckv, wkvb, BF16)

    ha = _rmsnorm(x2, g_attn, "attn_norm")
    ql = _mm_plain("q_a", "nn", ha, wqa)
    qn = _rmsnorm(ql, g_q_lat, "q_latent_norm")
    tab = pl.BlockSpec((_tile(S, ROW_TILE), 128), lambda i, j, k: (i, 0))
    qr = _slots_out("q_b", qn, wqb, BF16, epilogue=(_rope_heads, [cos_t, sin_t], [tab, tab]))
    o, lse, g_wg1, g_wu1, g_wd1 = _attn_fwd(qr, kvu, kpe, H, ride=gather_ffn1)
    wg1, wu1, wd1 = g_wg1.reshape(N_DEV * D, fs), g_wu1.reshape(N_DEV * D, fs), g_wd1.reshape(N_DEV * fs, D)
    x3 = _mm_plain("attn_out", "nn", o, wo, res=x2)
    x4, ffn1, _, _ = _ffn_fwd(x3, ffn_norm[1:2], wg1, wu1, wd1, "1")

    loss_part, dx4, dx4b, d_final = _loss_head(x4, g_final, target)

    (dx3, dx3b, d_ffn1), _, got_ffn1 = _ffn_bwd(x3, ffn_norm[1:2], wg1, wu1, wd1, ffn1, dx4, dx4b, "1", True)

    do = _mm_plain("attn_out_dx", "nt", dx3b, wo, out_dtype=BF16)
    dwo = _mm_plain("attn_out_dw", "tn", o, dx3b, out_dtype=BF16)
    dq, delta = _attn_dq(qr, kvu, kpe, do, o, lse, cos_t, sin_t, H)
    dkvu, dkpe = _attn_dkv(qr, kvu, kpe, do, lse.reshape(H, 1, S), delta.reshape(H, 1, S), H)

    dqn = _slots_in_nt("q_b_dx", dq, wqb, S)
    dwqb = _slots_dw("q_b_dw", qn, dq, BF16)
    dql, d_q_lat = _rmsnorm_bwd(ql, g_q_lat, dqn, None, "q_latent_norm_bwd", out_dtype=BF16)
    dha = _mm_plain("q_a_dx", "nt", dql, wqa)
    dwqa = _mm_plain("q_a_dw", "tn", ha, dql, out_dtype=BF16)
    dx2, d_attn = _rmsnorm_bwd(x2, g_attn, dha, dx3, "attn_norm_bwd")

    dckv = _slots_in_nt("kv_b_dx", dkvu, wkvb, S)
    dwkvb = _slots_dw("kv_b_dw", ckv, dkvu, BF16)
    dkv, d_kv_lat = _kv_mid_bwd(kv, g_kv_lat, dckv, dkpe, cos_t, sin_t)
    dhk = _mm_plain("kv_a_dx", "nt", dkv, wkva)
    dwkva = _mm_plain("kv_a_dw", "tn", hk, dkv, out_dtype=BF16)
    dx2, dx2b, d_kv_in = _rmsnorm_bwd(x2, g_kv_in, dhk, dx2, "kv_in_norm_bwd", with_bf16=True)

    scatter_mla = _Ride("scatter", [dwkva.reshape(N_DEV, Ds, R_kv + 128), dwkvb.reshape(N_DEV, R_kv, kvw),
                                    dwqa.reshape(N_DEV, Ds, R_q), dwqb.reshape(N_DEV, R_q, hpd * HEAD_PAD),
                                    dwo.reshape(N_DEV, H * V_DIM // N_DEV, D)])
    (dx1, d_ffn0), got_mla, got_ffn0 = _ffn_bwd(x1, ffn_norm[0:1], wg0, wu0, wd0, ffn0, dx2, dx2b, "0", False,
                                                  scatter=scatter_mla)

    ddiff, dwp, d_pool_scale = _pool_mix_bwd(diff, wp, g_pool_scale, dx1)
    grad_x, d_pool_norm = _pool_pre_bwd(xs, g_pool_norm, ddiff, dx1)

    d_gains = jnp.swapaxes(jnp.concatenate([d_pool_norm, d_pool_scale], axis=0).reshape(2, N_DEV, D // N_DEV), 0, 1)
    got_pool_w, got_gains = _all_to_all([jnp.swapaxes(dwp.reshape(G, N_DEV, gws, gw), 0, 1), d_gains], "exchange_pool")

    received = dict(zip(("w_kv_a", "w_kv_b", "w_q_a", "w_q_b", "w_o"), got_mla))
    received.update(zip(("w_gate0", "w_up0", "w_down0"), got_ffn0))
    received.update(zip(("w_gate1", "w_up1", "w_down1"), got_ffn1))
    received.update(pool_w=got_pool_w, gains=got_gains)
    sums = {n: _sum_slots(r, "sum_" + n) for n, r in received.items()}
    grads = {
        "pool_w": sums["pool_w"][None],
        "w_kv_a": sums["w_kv_a"][:, :R_kv + QK_ROPE_DIM],
        "w_kv_b": sums["w_kv_b"],
        "w_q_a": sums["w_q_a"][None],
        "w_q_b": sums["w_q_b"].reshape(R_q, hpd, HEAD_PAD)[:, :, :QK_DIM].reshape(1, R_q, hpd * QK_DIM),
        "w_o": sums["w_o"][None],
        "w_gate": jnp.stack([sums["w_gate0"], sums["w_gate1"]]),
        "w_up": jnp.stack([sums["w_up0"], sums["w_up1"]]),
        "w_down": jnp.stack([sums["w_down0"], sums["w_down1"]]),
        "pool_norm": sums["gains"][0:1],
        "pool_scale": sums["gains"][1:2],
    }

    small = {"kv_in_norm": d_kv_in, "kv_latent_norm": d_kv_lat, "attn_norm": d_attn, "q_latent_norm": d_q_lat,
             "ffn_norm": jnp.concatenate([d_ffn0, d_ffn1], axis=0), "final_norm": d_final}
    small_packed = jnp.concatenate([small[n].reshape(-1) for n in _REPLICATED] + [loss_part.reshape(-1)])
    pad = (-small_packed.shape[0]) % (8 * 128)
    reduced = _all_reduce_small(jnp.pad(small_packed, (0, pad)).reshape(-1, 128)).reshape(-1)
    off = 0
    for n in _REPLICATED:
        grads[n] = reduced[off:off + weights[n].size].reshape(weights[n].shape)
        off += weights[n].size
    loss = reduced[off]

    delta_w, new_m, new_v = {}, {}, {}
    for n in _WEIGHTS:
        delta_w[n], new_m[n], new_v[n] = _adamw(weights[n], grads[n], m_in[n], v_in[n], "adamw_" + n)

    return (loss, grad_x.reshape(x.shape), *[grads[n] for n in _WEIGHTS], *[delta_w[n] for n in _WEIGHTS],
            *[new_m[n] for n in _WEIGHTS], *[new_v[n] for n in _WEIGHTS])
```

```python
import math

import jax
import jax.numpy as jnp
import numpy as np
from jax import lax
from jax.experimental import pallas as pl
from jax.experimental.pallas import tpu as pltpu

F32 = jnp.float32
BF16 = jnp.bfloat16

N_DEV = 8
POOL_WINDOWS = (2, 4, 8, 16)
POOL_HALO = 16
QK_NOPE_DIM = 128
QK_ROPE_DIM = 64
QK_DIM = QK_NOPE_DIM + QK_ROPE_DIM
HEAD_PAD = 256
V_DIM = 128
ROPE_BASE = 10000.0
ATTN_SCALE = 1.0 / math.sqrt(QK_DIM)
EXP2_SCALE = ATTN_SCALE * math.log2(math.e)
NORM_EPS = 1e-6
ADAM_LR = 0.001
ADAM_B1 = 0.9
ADAM_B2 = 0.999
ADAM_EPS = 1e-08
ADAM_WD = 0.01
ADAM_STEP = 10

ROW_TILE = 512
ATT_BLOCK = 512
ATT_ROW_CHUNK = 256
MM_TN, MM_TK = 512, 512
VMEM_LIMIT = 48 * 1024 * 1024

_pcall = pl.pallas_call


def _params(sem):
    return pltpu.CompilerParams(dimension_semantics=sem, vmem_limit_bytes=VMEM_LIMIT)


def _tile(dim, default):
    if dim % default == 0:
        return default
    assert dim <= 2 * default, (dim, default)
    return dim


def _row_tile(rows, width):
    ts = _tile(rows, ROW_TILE)
    while ts * width * 4 > (2 << 20) and ts % 16 == 0:
        ts //= 2
    return ts


def _rms_scale(x):
    return lax.rsqrt(jnp.mean(x * x, axis=-1, keepdims=True) + NORM_EPS)


def _rms_bwd(x, g, dy):
    r = _rms_scale(x)
    xn = x * r
    u = dy * g
    dx = r * (u - xn * jnp.mean(u * xn, axis=-1, keepdims=True))
    return dx, dy * xn


def _swap_halves(x):
    lane = lax.broadcasted_iota(jnp.int32, x.shape, 1)
    return jnp.where(lane < QK_ROPE_DIM // 2, pltpu.roll(x, 128 - QK_ROPE_DIM // 2, 1), pltpu.roll(x, QK_ROPE_DIM // 2, 1))


def _rope(x, cos_t, sin_t):
    return x * cos_t + _swap_halves(x) * sin_t


def _rmsnorm(x, g, name):
    S, D = x.shape
    ts = _row_tile(S, D)

    def body(x_ref, g_ref, o_ref):
        xf = x_ref[...]
        o_ref[...] = (xf * _rms_scale(xf) * g_ref[...]).astype(o_ref.dtype)

    return _pcall(
        body, name=name, grid=(S // ts,),
        in_specs=[pl.BlockSpec((ts, D), lambda i: (i, 0)), pl.BlockSpec((1, D), lambda i: (0, 0))],
        out_specs=pl.BlockSpec((ts, D), lambda i: (i, 0)),
        out_shape=jax.ShapeDtypeStruct((S, D), BF16),
        compiler_params=_params(("parallel",)),
    )(x, g)


def _rmsnorm_bwd(x, g, dy, res, name, out_dtype=F32, with_bf16=False):
    S, D = x.shape
    ts = _row_tile(S, D)
    has_res = res is not None

    def body(*refs):
        refs = list(refs)
        x_ref, g_ref, dy_ref = refs[:3]
        res_ref = refs[3] if has_res else None
        outs = refs[3 + has_res:]
        dx_ref, dg_ref = outs[0], outs[-1]
        dx, dgt = _rms_bwd(x_ref[...], g_ref[...], dy_ref[...].astype(F32))
        if has_res:
            dx = res_ref[...] + dx
        dx_ref[...] = dx.astype(dx_ref.dtype)
        if with_bf16:
            outs[1][...] = dx.astype(BF16)

        @pl.when(pl.program_id(0) == 0)
        def _():
            dg_ref[...] = jnp.zeros_like(dg_ref)

        dg_ref[...] += jnp.sum(dgt, axis=0, keepdims=True)

    row = pl.BlockSpec((ts, D), lambda i: (i, 0))
    vec = pl.BlockSpec((1, D), lambda i: (0, 0))
    out_specs = [row] + ([row] if with_bf16 else []) + [vec]
    out_shape = ([jax.ShapeDtypeStruct((S, D), out_dtype)] + ([jax.ShapeDtypeStruct((S, D), BF16)] if with_bf16 else [])
                 + [jax.ShapeDtypeStruct((1, D), F32)])
    return _pcall(
        body, name=name, grid=(S // ts,),
        in_specs=[row, vec, row] + ([row] if has_res else []),
        out_specs=tuple(out_specs), out_shape=tuple(out_shape),
        compiler_params=_params(("arbitrary",)),
    )(*([x, g, dy] + ([res] if has_res else [])))


def _pool_pre(x, g):
    S, D = x.shape
    ts = _row_tile(S, D)
    gw = D // len(POOL_WINDOWS)
    hb = ts // POOL_HALO

    def body(x_ref, halo_ref, g_ref, o_ref):
        i = pl.program_id(0)
        xx = jnp.concatenate([halo_ref[...], x_ref[...]], axis=0)
        h = xx * _rms_scale(xx) * g_ref[...]
        t = i * ts - POOL_HALO + lax.broadcasted_iota(jnp.int32, (ts + POOL_HALO, 1), 0)
        h = jnp.where(t >= 0, h, 0.0)
        tf = t[POOL_HALO:].astype(F32)
        for gi, w in enumerate(POOL_WINDOWS):
            hg = h[:, gi * gw:(gi + 1) * gw]
            acc, span = hg, 1
            while span < w:
                acc = acc + pltpu.roll(acc, span, 0)
                span *= 2
            count = jnp.minimum(tf + 1.0, float(w))
            diff = acc[POOL_HALO:] / count - hg[POOL_HALO:]
            o_ref[:, gi * gw:(gi + 1) * gw] = diff.astype(o_ref.dtype)

    return _pcall(
        body, name="pool_pre", grid=(S // ts,),
        in_specs=[pl.BlockSpec((ts, D), lambda i: (i, 0)),
                  pl.BlockSpec((POOL_HALO, D), lambda i: (jnp.maximum(i * hb - 1, 0), 0)),
                  pl.BlockSpec((1, D), lambda i: (0, 0))],
        out_specs=pl.BlockSpec((ts, D), lambda i: (i, 0)),
        out_shape=jax.ShapeDtypeStruct((S, D), BF16),
        compiler_params=_params(("parallel",)),
    )(x, x, g)


def _pool_pre_bwd(x, g, dd, res):
    S, D = x.shape
    ts = _row_tile(S, D)
    gw = D // len(POOL_WINDOWS)
    hb = ts // POOL_HALO
    last_halo = S // POOL_HALO - 1

    def body(x_ref, g_ref, dd_ref, halo_ref, res_ref, dx_ref, dg_ref):
        i = pl.program_id(0)
        ee = jnp.concatenate([dd_ref[...], halo_ref[...]], axis=0)
        t = i * ts + lax.broadcasted_iota(jnp.int32, (ts + POOL_HALO, 1), 0)
        ee = jnp.where(t < S, ee, 0.0)
        tf = t.astype(F32)
        n = ts + POOL_HALO
        for gi, w in enumerate(POOL_WINDOWS):
            eg = ee[:, gi * gw:(gi + 1) * gw]
            acc, span = eg / jnp.minimum(tf + 1.0, float(w)), 1
            while span < w:
                acc = acc + pltpu.roll(acc, n - span, 0)
                span *= 2
            dx_ref[:, gi * gw:(gi + 1) * gw] = acc[:ts] - eg[:ts]
        dx, dgt = _rms_bwd(x_ref[...], g_ref[...], dx_ref[...])
        dx_ref[...] = res_ref[...] + dx

        @pl.when(i == 0)
        def _():
            dg_ref[...] = jnp.zeros_like(dg_ref)

        dg_ref[...] += jnp.sum(dgt, axis=0, keepdims=True)

    row = pl.BlockSpec((ts, D), lambda i: (i, 0))
    vec = pl.BlockSpec((1, D), lambda i: (0, 0))
    return _pcall(
        body, name="pool_pre_bwd", grid=(S // ts,),
        in_specs=[row, vec, row,
                  pl.BlockSpec((POOL_HALO, D), lambda i: (jnp.minimum((i + 1) * hb, last_halo), 0)), row],
        out_specs=(row, vec),
        out_shape=(jax.ShapeDtypeStruct((S, D), F32), jax.ShapeDtypeStruct((1, D), F32)),
        compiler_params=_params(("arbitrary",)),
    )(x, g, dd, dd, res)


def _pool_mix(diff, w, scale, x):
    S, D = x.shape
    G, gw, _ = w.shape
    ts = _tile(S, ROW_TILE)

    def body(d_ref, w_ref, s_ref, x_ref, o_ref):
        mo = jnp.dot(d_ref[...], w_ref[0], preferred_element_type=F32)
        o_ref[...] = x_ref[...] + mo * s_ref[...]

    blk = pl.BlockSpec((ts, gw), lambda i, g: (i, g))
    return _pcall(
        body, name="pool_mix", grid=(S // ts, G),
        in_specs=[blk, pl.BlockSpec((1, gw, gw), lambda i, g: (g, 0, 0)), pl.BlockSpec((1, gw), lambda i, g: (0, g)), blk],
        out_specs=blk,
        out_shape=jax.ShapeDtypeStruct((S, D), F32),
        compiler_params=_params(("parallel", "parallel")),
    )(diff, w, scale, x)


def _pool_mix_bwd(diff, w, scale, dy):
    S, D = dy.shape
    G, gw, _ = w.shape
    ts = _tile(S, ROW_TILE)

    def body(d_ref, w_ref, s_ref, dy_ref, dd_ref, dw_ref, ds_ref):
        i = pl.program_id(1)
        d = d_ref[...]
        dy = dy_ref[...]
        mo = jnp.dot(d, w_ref[0], preferred_element_type=F32)
        dmo = (dy * s_ref[...]).astype(BF16)
        dd_ref[...] = lax.dot_general(dmo, w_ref[0], (((1,), (1,)), ((), ())), preferred_element_type=F32)

        @pl.when(i == 0)
        def _():
            dw_ref[...] = jnp.zeros_like(dw_ref)
            ds_ref[...] = jnp.zeros_like(ds_ref)

        dw_ref[0] += lax.dot_general(d, dmo, (((0,), (0,)), ((), ())), preferred_element_type=F32)
        ds_ref[...] += jnp.sum(dy * mo, axis=0, keepdims=True)

    blk = pl.BlockSpec((ts, gw), lambda g, i: (i, g))
    wspec = pl.BlockSpec((1, gw, gw), lambda g, i: (g, 0, 0))
    vec = pl.BlockSpec((1, gw), lambda g, i: (0, g))
    return _pcall(
        body, name="pool_mix_bwd", grid=(G, S // ts),
        in_specs=[blk, wspec, vec, blk],
        out_specs=(blk, wspec, vec),
        out_shape=(jax.ShapeDtypeStruct((S, D), F32), jax.ShapeDtypeStruct((G, gw, gw), F32),
                   jax.ShapeDtypeStruct((1, D), F32)),
        compiler_params=_params(("parallel", "arbitrary")),
    )(diff, w, scale, dy)


def _loss_head(x, g, target):
    S, D = x.shape
    ts = _row_tile(S, D)

    def body(x_ref, g_ref, t_ref, loss_ref, dx_ref, dxb_ref, dg_ref):
        xf = x_ref[...]
        gv = g_ref[...]
        err = xf * _rms_scale(xf) * gv - t_ref[...]
        dx, dgt = _rms_bwd(xf, gv, err * (1.0 / D))
        dx_ref[...] = dx
        dxb_ref[...] = dx.astype(BF16)

        @pl.when(pl.program_id(0) == 0)
        def _():
            loss_ref[...] = jnp.zeros_like(loss_ref)
            dg_ref[...] = jnp.zeros_like(dg_ref)

        part = 0.5 * jnp.sum(jnp.sum(err * err, axis=-1, keepdims=True) * (1.0 / D), axis=0, keepdims=True)
        loss_ref[...] += jnp.broadcast_to(part, loss_ref.shape)
        dg_ref[...] += jnp.sum(dgt, axis=0, keepdims=True)

    row = pl.BlockSpec((ts, D), lambda i: (i, 0))
    vec = pl.BlockSpec((1, D), lambda i: (0, 0))
    return _pcall(
        body, name="loss_head", grid=(S // ts,),
        in_specs=[row, vec, row],
        out_specs=(pl.BlockSpec((1, 128), lambda i: (0, 0)), row, row, vec),
        out_shape=(jax.ShapeDtypeStruct((1, 128), F32), jax.ShapeDtypeStruct((S, D), F32),
                   jax.ShapeDtypeStruct((S, D), BF16), jax.ShapeDtypeStruct((1, D), F32)),
        compiler_params=_params(("arbitrary",)),
    )(x, g, target)


_DIMS = {"nn": (((1,), (0,)), ((), ())), "nt": (((1,), (1,)), ((), ())), "tn": (((0,), (0,)), ((), ()))}


def _mm(name, mode, a, b, tiles, grid, a_map, b_map, o_map, out_shape, out_dtype=F32, res=None, a2=None, b2=None,
        ride=None, epilogue=None):
    tm, tn, tk = tiles
    nk = grid[-1]
    dual, has_res = a2 is not None, res is not None
    n_main = 2 + 2 * dual
    dn = _DIMS[mode]

    def body(ins, outs, scratch):
        o_ref = outs[0]

        def product():
            part = lax.dot_general(ins[0][...].astype(BF16), ins[1][...].astype(BF16), dn, preferred_element_type=F32)
            if dual:
                part += lax.dot_general(ins[2][...].astype(BF16), ins[3][...].astype(BF16), dn,
                                        preferred_element_type=F32)
            return part

        def finish(out):
            if has_res:
                out = ins[n_main][...] + out
            if epilogue is not None:
                out = epilogue[0](out, *[r[...] for r in ins[n_main + has_res:]])
            o_ref[...] = out.astype(o_ref.dtype)

        if nk == 1:
            finish(product())
            return
        acc_ref = scratch[0]
        k = pl.program_id(2)

        @pl.when(k == 0)
        def _():
            acc_ref[...] = jnp.zeros_like(acc_ref)

        acc_ref[...] += product()

        @pl.when(k == nk - 1)
        def _():
            finish(acc_ref[...])

    a_spec = pl.BlockSpec((tk, tm) if mode == "tn" else (tm, tk), a_map)
    b_spec = pl.BlockSpec((tn, tk) if mode == "nt" else (tk, tn), b_map)
    o_spec = pl.BlockSpec((tm, tn), o_map)
    operands, in_specs = [a, b], [a_spec, b_spec]
    if dual:
        operands += [a2, b2]
        in_specs += [a_spec, b_spec]
    if has_res:
        operands.append(res)
        in_specs.append(o_spec)
    if epilogue is not None:
        operands += list(epilogue[1])
        in_specs += list(epilogue[2])
    outs = _pallas(body, name, grid, operands, in_specs, [o_spec], [jax.ShapeDtypeStruct(out_shape, out_dtype)],
                   scratch=[pltpu.VMEM((tm, tn), F32)] if nk > 1 else [],
                   sem=("parallel", "parallel", "arbitrary"), ride=ride)
    return (outs[0], outs[1:]) if ride else outs[0]


def _mm_plain(name, mode, a, b, out_dtype=F32, res=None):
    if mode == "tn":
        (K, M), N = a.shape, b.shape[1]
    elif mode == "nt":
        (M, K), N = a.shape, b.shape[0]
    else:
        (M, K), N = a.shape, b.shape[1]
    if mode == "tn":
        tm, tn, tk = _tile(M, 2 * MM_TN), _tile(N, 2 * MM_TN), _tile(K, ROW_TILE)
    else:
        tm = _tile(M, ROW_TILE)
        tk = K if K <= 4 * MM_TK else _tile(K, MM_TK)
        tn = N if N * tk * 2 <= (8 << 20) else _tile(N, MM_TN)
    a_map = (lambda i, j, k: (k, i)) if mode == "tn" else (lambda i, j, k: (i, k))
    b_map = (lambda i, j, k: (j, k)) if mode == "nt" else (lambda i, j, k: (k, j))
    return _mm(name, mode, a, b, (tm, tn, tk), (M // tm, N // tn, K // tk), a_map, b_map, lambda i, j, k: (i, j),
               (M, N), out_dtype, res)


def _slots_out(name, a, w_slots, out_dtype, epilogue=None):
    S, K = a.shape
    n = w_slots.shape[1]
    tm = _tile(S, ROW_TILE)
    nmb = S // tm
    return _mm(name, "nn", a, w_slots, (tm, n, K), (nmb, N_DEV, 1),
               lambda i, j, k: (i, 0), lambda i, j, k: (j, 0), lambda i, j, k: (j * nmb + i, 0),
               (N_DEV * S, n), out_dtype, epilogue=epilogue)


def _slots_in_nt(name, a_slots, w_slots, S, a2=None, w2=None, ride=None):
    n = a_slots.shape[1]
    K = w_slots.shape[0] // N_DEV
    tm = _tile(S, ROW_TILE)
    nmb = S // tm
    return _mm(name, "nt", a_slots, w_slots, (tm, K, n), (nmb, 1, N_DEV),
               lambda i, q, j: (j * nmb + i, 0), lambda i, q, j: (j, 0), lambda i, q, j: (i, 0),
               (S, K), F32, a2=a2, b2=w2, ride=ride)


def _slots_dw(name, a, d_slots, out_dtype=F32, ride=None):
    S, K = a.shape
    n = d_slots.shape[1]
    tk = _tile(S, ROW_TILE)
    nkb = S // tk
    return _mm(name, "tn", a, d_slots, (K, n, tk), (N_DEV, 1, nkb),
               lambda j, q, k: (k, 0), lambda j, q, k: (j * nkb + k, 0), lambda j, q, k: (j, 0),
               (N_DEV * K, n), out_dtype, ride=ride)


def _sigmoid(x):
    return 1.0 / (1.0 + jnp.exp(-x))


def _ffn_up(h, wg, wu, name, ride=None):
    S, D = h.shape
    fs = wg.shape[1]
    tm = _tile(S, ROW_TILE)
    nmb = S // tm

    def body(ins, outs, scratch):
        h_ref, wg_ref, wu_ref = ins
        g_ref, u_ref, a_ref = outs
        hv = h_ref[...]
        g = jnp.dot(hv, wg_ref[...], preferred_element_type=F32)
        u = jnp.dot(hv, wu_ref[...], preferred_element_type=F32)
        g_ref[...] = g.astype(g_ref.dtype)
        u_ref[...] = u.astype(u_ref.dtype)
        a_ref[...] = (g * _sigmoid(g) * u).astype(a_ref.dtype)

    w_spec = pl.BlockSpec((D, fs), lambda i, j: (j, 0))
    o_spec = pl.BlockSpec((tm, fs), lambda i, j: (j * nmb + i, 0))
    sds = jax.ShapeDtypeStruct((N_DEV * S, fs), BF16)
    return _pallas(body, name, (nmb, N_DEV), [h, wg, wu],
                   [pl.BlockSpec((tm, D), lambda i, j: (i, 0)), w_spec, w_spec], [o_spec, o_spec, o_spec],
                   [sds, sds, sds], sem=("parallel", "arbitrary"), ride=ride)


def _ffn_dact(dy, wd, g, u, name, ride=None):
    S, D = dy.shape
    fs = g.shape[1]
    tm = _tile(S, ROW_TILE)
    nmb = S // tm

    def body(ins, outs, scratch):
        dy_ref, wd_ref, g_ref, u_ref = ins
        dg_ref, du_ref = outs
        da = lax.dot_general(dy_ref[...], wd_ref[...], _DIMS["nt"], preferred_element_type=F32)
        g, u = g_ref[...].astype(F32), u_ref[...].astype(F32)
        sig = _sigmoid(g)
        dg_ref[...] = (da * u * (sig * (1.0 + g * (1.0 - sig)))).astype(dg_ref.dtype)
        du_ref[...] = (da * (g * sig)).astype(du_ref.dtype)

    o_spec = pl.BlockSpec((tm, fs), lambda i, j: (j * nmb + i, 0))
    sds = jax.ShapeDtypeStruct((N_DEV * S, fs), BF16)
    return _pallas(body, name, (nmb, N_DEV), [dy, wd, g, u],
                   [pl.BlockSpec((tm, D), lambda i, j: (i, 0)), pl.BlockSpec((fs, D), lambda i, j: (j, 0)),
                    o_spec, o_spec], [o_spec, o_spec], [sds, sds], sem=("parallel", "arbitrary"), ride=ride)


def _rope_heads(q, cos_t, sin_t):
    parts = []
    for c0 in range(0, q.shape[1], HEAD_PAD):
        parts += [q[:, c0:c0 + QK_NOPE_DIM], _rope(q[:, c0 + QK_NOPE_DIM:c0 + HEAD_PAD], cos_t, sin_t)]
    return jnp.concatenate(parts, axis=1)


def _kv_mid(kv, g, cos_t, sin_t):
    S, W = kv.shape
    R = W - 128
    ts = _tile(S, ROW_TILE)

    def body(kv_ref, g_ref, c_ref, s_ref, c_out, k_out):
        cf = kv_ref[:, :R]
        c_out[...] = (cf * _rms_scale(cf) * g_ref[...]).astype(c_out.dtype)
        k_out[...] = _rope(kv_ref[:, R:], c_ref[...], s_ref[...]).astype(k_out.dtype)

    tab = pl.BlockSpec((ts, 128), lambda i: (i, 0))
    return _pcall(
        body, name="kv_mid", grid=(S // ts,),
        in_specs=[pl.BlockSpec((ts, W), lambda i: (i, 0)), pl.BlockSpec((1, R), lambda i: (0, 0)), tab, tab],
        out_specs=(pl.BlockSpec((ts, R), lambda i: (i, 0)), tab),
        out_shape=(jax.ShapeDtypeStruct((S, R), BF16), jax.ShapeDtypeStruct((S, 128), BF16)),
        compiler_params=_params(("parallel",)),
    )(kv, g, cos_t, sin_t)


def _kv_mid_bwd(kv, g, dc, dkpe, cos_t, sin_t):
    S, W = kv.shape
    R = W - 128
    H = dkpe.shape[0]
    ts = _row_tile(S, H * 128)

    def body(kv_ref, g_ref, dc_ref, dk_ref, c_ref, s_ref, dkv_ref, dg_ref):
        dx, dgt = _rms_bwd(kv_ref[:, :R], g_ref[...], dc_ref[...])
        dkv_ref[:, :R] = dx.astype(dkv_ref.dtype)
        dk = dk_ref[0]
        for h in range(1, H):
            dk = dk + dk_ref[h]
        dkv_ref[:, R:] = _rope(dk, c_ref[...], -s_ref[...]).astype(dkv_ref.dtype)

        @pl.when(pl.program_id(0) == 0)
        def _():
            dg_ref[...] = jnp.zeros_like(dg_ref)

        dg_ref[...] += jnp.sum(dgt, axis=0, keepdims=True)

    tab = pl.BlockSpec((ts, 128), lambda i: (i, 0))
    vec = pl.BlockSpec((1, R), lambda i: (0, 0))
    return _pcall(
        body, name="kv_mid_bwd", grid=(S // ts,),
        in_specs=[pl.BlockSpec((ts, W), lambda i: (i, 0)), vec, pl.BlockSpec((ts, R), lambda i: (i, 0)),
                  pl.BlockSpec((H, ts, 128), lambda i: (0, i, 0)), tab, tab],
        out_specs=(pl.BlockSpec((ts, W), lambda i: (i, 0)), vec),
        out_shape=(jax.ShapeDtypeStruct((S, W), BF16), jax.ShapeDtypeStruct((1, R), F32)),
        compiler_params=_params(("arbitrary",)),
    )(kv, g, dc, dkpe, cos_t, sin_t)


def _block_pairs(nb, by_key):
    pairs = ([(qi, ki) for ki in range(nb) for qi in range(ki, nb)] if by_key
             else [(qi, ki) for qi in range(nb) for ki in range(qi + 1)])
    return jnp.asarray(np.array([p[0] for p in pairs], np.int32)), jnp.asarray(np.array([p[1] for p in pairs], np.int32))


def _causal_mask(blk, transposed=False, rows=None, row0=0):
    shape = (blk if rows is None else rows, blk)
    r = lax.broadcasted_iota(jnp.int32, shape, 0) + row0
    c = lax.broadcasted_iota(jnp.int32, shape, 1)
    return (r <= c) if transposed else (c <= r)


def _attn_specs(S, blk, hpd):
    nb = S // blk
    w = hpd * HEAD_PAD
    return dict(
        q=pl.BlockSpec((blk, w), lambda j, t, qt, kt: (j * nb + qt[t], 0)),
        kv=pl.BlockSpec((blk, w), lambda j, t, qt, kt: (j * nb + kt[t], 0)),
        kp=pl.BlockSpec((blk, 128), lambda j, t, qt, kt: (kt[t], 0)),
        do=pl.BlockSpec((blk, hpd * V_DIM), lambda j, t, qt, kt: (qt[t], j)),
        col=pl.BlockSpec((hpd, blk, 1), lambda j, t, qt, kt: (j, qt[t], 0)),
        row=pl.BlockSpec((hpd, 1, blk), lambda j, t, qt, kt: (j, 0, qt[t])),
        tab=pl.BlockSpec((blk, 128), lambda j, t, qt, kt: (qt[t], 0)))


def _head_k(kv_ref, kp, hh):
    return jnp.concatenate([kv_ref[:, hh * HEAD_PAD:hh * HEAD_PAD + QK_NOPE_DIM], kp], axis=1)


def _head_v(kv_ref, hh):
    return kv_ref[:, hh * HEAD_PAD + QK_NOPE_DIM:(hh + 1) * HEAD_PAD]


def _attn_fwd(q, kvu, kpe, H, ride=None):
    S = kpe.shape[0]
    hpd = H // N_DEV
    blk = _tile(S, ATT_BLOCK)
    q_tab, k_tab = _block_pairs(S // blk, by_key=False)
    rc = min(blk, ATT_ROW_CHUNK)

    def body(ins, outs, scratch):
        qt, kt, q_ref, kv_ref, kp_ref = ins
        o_ref, lse_ref = outs
        m_ref, acc_ref = scratch
        t = pl.program_id(1)
        qi, ki = qt[t], kt[t]

        @pl.when(ki == 0)
        def _():
            m_ref[...] = jnp.full_like(m_ref, -jnp.inf)
            acc_ref[...] = jnp.zeros_like(acc_ref)

        def step(masked):
            kp = kp_ref[...]
            ones = jnp.ones((blk, 128), BF16)
            for hh in range(hpd):
                k = _head_k(kv_ref, kp, hh)
                v_ext = jnp.concatenate([_head_v(kv_ref, hh), ones], axis=1)
                for r0 in range(0, blk, rc):
                    rows = slice(r0, r0 + rc)
                    s = lax.dot_general(q_ref[rows, hh * HEAD_PAD:(hh + 1) * HEAD_PAD], k, _DIMS["nt"],
                                        preferred_element_type=F32)
                    if masked:
                        s = jnp.where(_causal_mask(blk, rows=rc, row0=r0), s, -jnp.inf)
                    m_old = m_ref[hh, rows]
                    m_new = jnp.maximum(m_old, jnp.max(s, axis=-1, keepdims=True))
                    p = jnp.exp2((s - m_new) * EXP2_SCALE).astype(BF16)
                    acc_ref[hh, rows] = (jnp.exp2((m_old - m_new) * EXP2_SCALE) * acc_ref[hh, rows]
                                         + jnp.dot(p, v_ext, preferred_element_type=F32))
                    m_ref[hh, rows] = m_new

        @pl.when(ki < qi)
        def _():
            step(False)

        @pl.when(ki == qi)
        def _():
            step(True)
            for hh in range(hpd):
                l = acc_ref[hh, :, V_DIM:]
                o_ref[:, hh * V_DIM:(hh + 1) * V_DIM] = (acc_ref[hh, :, :V_DIM] / l).astype(o_ref.dtype)
                lse_ref[hh] = m_ref[hh] * EXP2_SCALE + jnp.log2(l[:, :1])

    sp = _attn_specs(S, blk, hpd)
    return _pallas(body, "attn_fwd", (N_DEV, q_tab.shape[0]), [q, kvu, kpe], [sp["q"], sp["kv"], sp["kp"]],
                   [sp["do"], sp["col"]],
                   [jax.ShapeDtypeStruct((S, H * V_DIM), BF16), jax.ShapeDtypeStruct((H, S, 1), F32)],
                   scratch=[pltpu.VMEM((hpd, blk, 1), F32), pltpu.VMEM((hpd, blk, 2 * V_DIM), F32)],
                   sem=("parallel", "arbitrary"), ride=ride, prefetch=[q_tab, k_tab])


def _attn_bwd(q, kvu, kpe, do, o, lse_row, cos_t, sin_t, H):
    S = kpe.shape[0]
    hpd = H // N_DEV
    blk = _tile(S, ATT_BLOCK)
    nb = S // blk
    q_tab, k_tab = _block_pairs(nb, by_key=True)

    def body(qt, kt, q_ref, kv_ref, kp_ref, do_ref, o_ref, lse_ref, c_ref, s_ref, dq_ref, dkv_ref, dkp_ref,
             dq_acc, dk_acc, dv_acc):
        t = pl.program_id(1)
        qi, ki = qt[t], kt[t]
        q_rows = pl.ds(pl.multiple_of(qi * blk, blk), blk)

        def step(diag):
            kp = kp_ref[...]
            ones = jnp.ones((8, V_DIM), BF16)
            for hh in range(hpd):
                k = _head_k(kv_ref, kp, hh)
                qv = q_ref[:, hh * HEAD_PAD:(hh + 1) * HEAD_PAD]
                cols = slice(hh * V_DIM, (hh + 1) * V_DIM)
                dov = do_ref[:, cols]
                doo = (dov.astype(F32) * o_ref[:, cols].astype(F32)).astype(BF16)
                delta = lax.dot_general(ones, doo, _DIMS["nt"], preferred_element_type=F32)[:1]
                st = lax.dot_general(k, qv, _DIMS["nt"], preferred_element_type=F32)
                pt = jnp.exp2(st * EXP2_SCALE - lse_ref[hh])
                if diag:
                    pt = jnp.where(_causal_mask(blk, transposed=True), pt, 0.0)
                dv_new = jnp.dot(pt.astype(BF16), dov, preferred_element_type=F32)
                dpt = lax.dot_general(_head_v(kv_ref, hh), dov, _DIMS["nt"], preferred_element_type=F32)
                dst = (pt * (dpt - delta)).astype(BF16)
                dk_new = jnp.dot(dst, qv, preferred_element_type=F32)
                dq_new = lax.dot_general(dst, k, _DIMS["tn"], preferred_element_type=F32)
                if diag:
                    dv_acc[hh] = dv_new
                    dk_acc[hh] = dk_new
                else:
                    dv_acc[hh] += dv_new
                    dk_acc[hh] += dk_new

                @pl.when(ki == 0)
                def _():
                    dq_acc[hh, q_rows] = dq_new

                @pl.when(ki > 0)
                def _():
                    dq_acc[hh, q_rows] += dq_new

        @pl.when(qi == ki)
        def _():
            step(True)
            for hh in range(hpd):
                c0 = hh * HEAD_PAD
                dq = dq_acc[hh, q_rows] * ATTN_SCALE
                dq_ref[:, c0:c0 + QK_NOPE_DIM] = dq[:, :QK_NOPE_DIM].astype(dq_ref.dtype)
                dq_ref[:, c0 + QK_NOPE_DIM:c0 + HEAD_PAD] = _rope(dq[:, QK_NOPE_DIM:], c_ref[...],
                                                                  -s_ref[...]).astype(dq_ref.dtype)

        @pl.when(qi > ki)
        def _():
            step(False)

        @pl.when(qi == nb - 1)
        def _():
            for hh in range(hpd):
                c0 = hh * HEAD_PAD
                dkv_ref[:, c0:c0 + QK_NOPE_DIM] = (dk_acc[hh, :, :QK_NOPE_DIM] * ATTN_SCALE).astype(dkv_ref.dtype)
                dkv_ref[:, c0 + QK_NOPE_DIM:c0 + HEAD_PAD] = dv_acc[hh].astype(dkv_ref.dtype)
                dkp_ref[hh] = dk_acc[hh, :, QK_NOPE_DIM:] * ATTN_SCALE

    sp = _attn_specs(S, blk, hpd)
    key_tab = pl.BlockSpec((blk, 128), lambda j, t, qt, kt: (kt[t], 0))
    grid_spec = pltpu.PrefetchScalarGridSpec(
        num_scalar_prefetch=2, grid=(N_DEV, q_tab.shape[0]),
        in_specs=[sp["q"], sp["kv"], sp["kp"], sp["do"], sp["do"], sp["row"], key_tab, key_tab],
        out_specs=(sp["kv"], sp["kv"], pl.BlockSpec((hpd, blk, 128), lambda j, t, qt, kt: (j, kt[t], 0))),
        scratch_shapes=[pltpu.VMEM((hpd, S, HEAD_PAD), F32), pltpu.VMEM((hpd, blk, HEAD_PAD), F32),
                        pltpu.VMEM((hpd, blk, V_DIM), F32)])
    return _pcall(
        body, name="attn_bwd", grid_spec=grid_spec,
        out_shape=(jax.ShapeDtypeStruct(q.shape, BF16), jax.ShapeDtypeStruct(kvu.shape, BF16),
                   jax.ShapeDtypeStruct((H, S, 128), F32)),
        compiler_params=_params(("parallel", "arbitrary")),
    )(q_tab, k_tab, q, kvu, kpe, do, o, lse_row, cos_t, sin_t)


def _mesh_pos():
    return lax.axis_index("x"), lax.axis_index("y"), lax.axis_index("c")


def _flip(pos, k):
    x, y, c = pos
    return (1 - x if k & 4 else x, 1 - y if k & 2 else y, 1 - c if k & 1 else c)


def _rank(pos):
    return 4 * pos[0] + 2 * pos[1] + pos[2]


def _copies(n, src_of, dst_refs, local_src_of, send_sems, recv_sems, local_sems):
    me = _mesh_pos()
    local = [pltpu.make_async_copy(local_src_of(a), dst_refs[a].at[_rank(me)], local_sems.at[a]) for a in range(n)]
    sends, arrivals = [], []
    for k in range(1, N_DEV):
        peer = _flip(me, k)
        for a in range(n):
            idx = a * (N_DEV - 1) + k - 1
            for into, group in ((me, sends), (peer, arrivals)):
                group.append(pltpu.make_async_remote_copy(
                    src_ref=src_of(a, peer), dst_ref=dst_refs[a].at[_rank(into)],
                    send_sem=send_sems.at[idx], recv_sem=recv_sems.at[idx],
                    device_id=peer, device_id_type=pl.DeviceIdType.MESH))
    return local, sends, arrivals


def _exchange_start(*args):
    local, sends, _ = _copies(*args)
    for cp in local + sends:
        cp.start()


def _exchange_wait(*args):
    local, sends, arrivals = _copies(*args)
    for cp in arrivals:
        cp.wait_recv()
    for cp in sends:
        cp.wait_send()
    for cp in local:
        cp.wait()


def _exchange(*args):
    _exchange_start(*args)
    _exchange_wait(*args)


def _sems(n):
    return [pltpu.SemaphoreType.DMA((n * (N_DEV - 1),)), pltpu.SemaphoreType.DMA((n * (N_DEV - 1),)),
            pltpu.SemaphoreType.DMA((n,))]


_ANY = pl.BlockSpec(memory_space=pl.ANY)


class _Ride:
    def __init__(self, kind, arrays):
        self.kind, self.arrays, self.n = kind, list(arrays), len(arrays)

    def out_shape(self):
        lead = (N_DEV,) if self.kind == "gather" else ()
        return [jax.ShapeDtypeStruct(lead + a.shape, a.dtype) for a in self.arrays]

    def _args(self, x_refs, out_refs, sems):
        if self.kind == "gather":
            return (self.n, lambda a, peer: x_refs[a], out_refs, lambda a: x_refs[a]) + tuple(sems)
        me = _mesh_pos()
        return (self.n, lambda a, peer: x_refs[a].at[_rank(peer)], out_refs, lambda a: x_refs[a].at[_rank(me)]) + tuple(sems)

    def start(self, x_refs, out_refs, sems):
        _exchange_start(*self._args(x_refs, out_refs, sems))

    def wait(self, x_refs, out_refs, sems):
        _exchange_wait(*self._args(x_refs, out_refs, sems))


def _pallas(body, name, grid, operands, in_specs, out_specs, out_shape, scratch=(), sem=None, ride=None, prefetch=()):
    n_pf, n_in, n_out, n_scr = len(prefetch), len(in_specs), len(out_specs), len(scratch)
    nr = ride.n if ride else 0

    def wrapped(*refs):
        refs = list(refs)
        pf, refs = refs[:n_pf], refs[n_pf:]
        ins, r_in = refs[:n_in], refs[n_in:n_in + nr]
        outs, r_out = refs[n_in + nr:n_in + nr + n_out], refs[n_in + nr + n_out:n_in + 2 * nr + n_out]
        rest = refs[n_in + 2 * nr + n_out:]
        scr, sems = rest[:n_scr], rest[n_scr:]
        if ride:
            first, last = None, None
            for d, size in enumerate(grid):
                i = pl.program_id(d)
                first = (i == 0) if first is None else jnp.logical_and(first, i == 0)
                last = (i == size - 1) if last is None else jnp.logical_and(last, i == size - 1)

            @pl.when(first)
            def _():
                ride.start(r_in, r_out, sems)

        body(pf + ins, outs, scr)
        if ride:
            @pl.when(last)
            def _():
                ride.wait(r_in, r_out, sems)

    grid_spec = pltpu.PrefetchScalarGridSpec(
        num_scalar_prefetch=n_pf, grid=grid,
        in_specs=list(in_specs) + [_ANY] * nr, out_specs=tuple(list(out_specs) + [_ANY] * nr),
        scratch_shapes=list(scratch) + (_sems(nr) if ride else []))
    sem = tuple(sem) if sem is not None and not ride else ("arbitrary",) * len(grid)
    return list(_pcall(
        wrapped, name=name, grid_spec=grid_spec,
        out_shape=tuple(list(out_shape) + (ride.out_shape() if ride else [])),
        compiler_params=_params(sem),
    )(*(list(prefetch) + list(operands) + (ride.arrays if ride else []))))


def _all_gather(shards, name):
    n = len(shards)

    def body(*refs):
        x_refs, out_refs = refs[:n], refs[n:2 * n]
        _exchange(n, lambda a, peer: x_refs[a], out_refs, lambda a: x_refs[a], *refs[2 * n:])

    return _pcall(
        body, name=name, in_specs=[_ANY] * n, out_specs=tuple([_ANY] * n),
        out_shape=tuple(jax.ShapeDtypeStruct((N_DEV,) + s.shape, s.dtype) for s in shards),
        scratch_shapes=_sems(n),
    )(*shards)


def _all_to_all(parts, name):
    n = len(parts)

    def body(*refs):
        x_refs, out_refs = refs[:n], refs[n:2 * n]
        me = _mesh_pos()
        _exchange(n, lambda a, peer: x_refs[a].at[_rank(peer)], out_refs, lambda a: x_refs[a].at[_rank(me)],
                  *refs[2 * n:])

    return _pcall(
        body, name=name, in_specs=[_ANY] * n, out_specs=tuple([_ANY] * n),
        out_shape=tuple(jax.ShapeDtypeStruct(p.shape, p.dtype) for p in parts),
        scratch_shapes=_sems(n),
    )(*parts)


def _all_reduce_small(v):
    R, C = v.shape

    def body(x_ref, out_ref, buf_ref, send_sems, recv_sems, local_sems):
        _exchange(1, lambda a, peer: x_ref, [buf_ref], lambda a: x_ref, send_sems, recv_sems, local_sems)
        total = buf_ref[0]
        for j in range(1, N_DEV):
            total = total + buf_ref[j]
        out_ref[...] = total

    vm = pl.BlockSpec(memory_space=pltpu.VMEM)
    return _pcall(
        body, name="all_reduce_small", in_specs=[vm], out_specs=vm,
        out_shape=jax.ShapeDtypeStruct((R, C), F32),
        scratch_shapes=[pltpu.VMEM((N_DEV, R, C), F32)] + _sems(1),
    )(v)


def _sum_slots(parts, name):
    shape = parts.shape[1:]
    C = shape[-1]
    R = parts.size // (N_DEV * C)
    tr = R
    while N_DEV * tr * C * 4 > (4 << 20) and tr % 32 == 0:
        tr //= 2

    def body(p_ref, o_ref):
        total = p_ref[0].astype(F32)
        for j in range(1, N_DEV):
            total = total + p_ref[j].astype(F32)
        o_ref[...] = total

    return _pcall(
        body, name=name, grid=(R // tr,),
        in_specs=[pl.BlockSpec((N_DEV, tr, C), lambda i: (0, i, 0))],
        out_specs=pl.BlockSpec((tr, C), lambda i: (i, 0)),
        out_shape=jax.ShapeDtypeStruct((R, C), F32),
        compiler_params=_params(("parallel",)),
    )(parts.reshape(N_DEV, R, C)).reshape(shape)


def _adamw(w, g, m, v, name):
    shape = w.shape
    C = shape[-1]
    R = w.size // C
    tr = R
    while tr * C * 4 > (1 << 20) and tr % 16 == 0:
        tr //= 2

    def body(w_ref, g_ref, m_ref, v_ref, d_ref, nm_ref, nv_ref):
        gv = g_ref[...]
        nm = ADAM_B1 * m_ref[...] + (1.0 - ADAM_B1) * gv
        nv = ADAM_B2 * v_ref[...] + (1.0 - ADAM_B2) * (gv * gv)
        m_hat = nm / (1.0 - ADAM_B1 ** ADAM_STEP)
        v_hat = nv / (1.0 - ADAM_B2 ** ADAM_STEP)
        d_ref[...] = -ADAM_LR * (m_hat / (jnp.sqrt(v_hat) + ADAM_EPS) + ADAM_WD * w_ref[...])
        nm_ref[...] = nm
        nv_ref[...] = nv

    blk = pl.BlockSpec((tr, C), lambda i: (i, 0))
    sds = jax.ShapeDtypeStruct((R, C), F32)
    outs = _pcall(
        body, name=name, grid=(R // tr,),
        in_specs=[blk] * 4, out_specs=(blk,) * 3, out_shape=(sds,) * 3,
        compiler_params=_params(("parallel",)),
    )(*(t.reshape(R, C) for t in (w, g, m, v)))
    return tuple(o.reshape(shape) for o in outs)


_REPLICATED = ("kv_in_norm", "kv_latent_norm", "attn_norm", "q_latent_norm", "ffn_norm", "final_norm")
_WEIGHTS = ("pool_norm", "pool_w", "pool_scale", "kv_in_norm", "w_kv_a", "kv_latent_norm", "w_kv_b", "attn_norm",
            "w_q_a", "q_latent_norm", "w_q_b", "w_o", "ffn_norm", "w_gate", "w_up", "w_down", "final_norm")


def _ffn_fwd(x, norm_g, wg, wu, wd, tag, gather=None):
    S, D = x.shape
    fs = wg.shape[1]
    tm = _tile(S, ROW_TILE)
    nmb = S // tm
    h = _rmsnorm(x, norm_g, "ffn_norm" + tag)
    g, u, a, *gathered = _ffn_up(h, wg, wu, "ffn_up" + tag, ride=gather)
    if gather is not None:
        wd = gathered[0].reshape(N_DEV * fs, D)
    y = _mm("ffn_down" + tag, "nn", a, wd, (tm, D, fs), (nmb, 1, N_DEV),
            lambda i, q, j: (j * nmb + i, 0), lambda i, q, j: (j, 0), lambda i, q, j: (i, 0), (S, D), F32, res=x)
    return y, (h, g, u, a), wd, gathered


def _ffn_bwd(x, norm_g, wg, wu, wd, saved, dy, dyb, tag, with_bf16, scatter=None):
    S, D = x.shape
    h, g, u, a = saved
    fs = g.shape[1]
    tm = _tile(S, ROW_TILE)
    nmb = S // tm
    dg, du, *got_rider = _ffn_dact(dyb, wd, g, u, "ffn_dact" + tag, ride=scatter)
    dwd = _mm("ffn_dwd" + tag, "tn", a, dyb, (fs, D, tm), (N_DEV, 1, nmb),
              lambda j, q, k: (j * nmb + k, 0), lambda j, q, k: (k, 0), lambda j, q, k: (j, 0), (N_DEV * fs, D), BF16)
    dwg, (got_dwd,) = _slots_dw("ffn_dwg" + tag, h, dg, BF16, ride=_Ride("scatter", [dwd.reshape(N_DEV, fs, D)]))
    dwu, (got_dwg,) = _slots_dw("ffn_dwu" + tag, h, du, BF16, ride=_Ride("scatter", [dwg.reshape(N_DEV, D, fs)]))
    dh, (got_dwu,) = _mm("ffn_dh" + tag, "nt", dg, wg, (tm, D, fs), (nmb, 1, N_DEV),
                         lambda i, q, j: (j * nmb + i, 0), lambda i, q, j: (j, 0), lambda i, q, j: (i, 0),
                         (S, D), F32, a2=du, b2=wu, ride=_Ride("scatter", [dwu.reshape(N_DEV, D, fs)]))
    outs = _rmsnorm_bwd(x, norm_g, dh, dy, "ffn_norm_bwd" + tag, with_bf16=with_bf16)
    return outs, got_rider, (got_dwg, got_dwu, got_dwd)


def kernel(x, positions, pool_norm, pool_w, pool_scale, kv_in_norm, w_kv_a, kv_latent_norm, w_kv_b, attn_norm, w_q_a, q_latent_norm, w_q_b, w_o, ffn_norm, w_gate, w_up, w_down, final_norm, loss_target, m_pool_norm, m_pool_w, m_pool_scale, m_kv_in_norm, m_w_kv_a, m_kv_latent_norm, m_w_kv_b, m_attn_norm, m_w_q_a, m_q_latent_norm, m_w_q_b, m_w_o, m_ffn_norm, m_w_gate, m_w_up, m_w_down, m_final_norm, v_pool_norm, v_pool_w, v_pool_scale, v_kv_in_norm, v_w_kv_a, v_kv_latent_norm, v_w_kv_b, v_attn_norm, v_w_q_a, v_q_latent_norm, v_w_q_b, v_w_o, v_ffn_norm, v_w_gate, v_w_up, v_w_down, v_final_norm):
    env = dict(locals())
    weights = {n: env[n] for n in _WEIGHTS}
    m_in = {n: env["m_" + n] for n in _WEIGHTS}
    v_in = {n: env["v_" + n] for n in _WEIGHTS}

    S, D = x.shape[1], x.shape[2]
    xs = x.reshape(S, D)
    target = loss_target.reshape(S, D)
    G, gws, gw = pool_w.shape[1:]
    Ds = w_kv_a.shape[0]
    R_kv, kvw = w_kv_b.shape
    hpd = kvw // (QK_NOPE_DIM + V_DIM)
    H = hpd * N_DEV
    R_q = w_q_a.shape[2]
    fs = w_gate.shape[2]
    bf = lambda t: t.astype(BF16)

    gains = jnp.concatenate([pool_norm, pool_scale], axis=0)
    g_pool_w, g_gains, g_wg0, g_wu0 = _all_gather([bf(pool_w[0]), gains, bf(w_gate[0]), bf(w_up[0])], "gather_first")
    wqb_pad = jnp.pad(w_q_b.reshape(R_q, hpd, QK_DIM), ((0, 0), (0, 0), (0, HEAD_PAD - QK_DIM))).reshape(R_q, hpd * HEAD_PAD)
    gather_mla = _Ride("gather", [bf(w_down[0]), bf(jnp.pad(w_kv_a, ((0, 0), (0, 128 - QK_ROPE_DIM)))), bf(w_kv_b),
                                  bf(w_q_a[0]), bf(wqb_pad), bf(w_o[0])])
    gather_ffn1 = _Ride("gather", [bf(w_gate[1]), bf(w_up[1]), bf(w_down[1])])

    wp = jnp.swapaxes(g_pool_w, 0, 1).reshape(G, gw, gw)
    gains_full = jnp.swapaxes(g_gains, 0, 1).reshape(2, D)
    g_pool_norm, g_pool_scale = gains_full[0:1], gains_full[1:2]
    wg0, wu0 = g_wg0.reshape(N_DEV * D, fs), g_wu0.reshape(N_DEV * D, fs)

    g_kv_in = kv_in_norm.reshape(1, D)
    g_kv_lat = kv_latent_norm.reshape(1, R_kv)
    g_attn = attn_norm.reshape(1, D)
    g_q_lat = q_latent_norm.reshape(1, R_q)
    g_final = final_norm.reshape(1, D)

    half = QK_ROPE_DIM // 2
    inv_freq = ROPE_BASE ** (-jnp.arange(half, dtype=F32) / half)
    ang = positions.reshape(S).astype(F32)[:, None] * inv_freq
    cos, sin = jnp.cos(ang), jnp.sin(ang)
    zeros = jnp.zeros((S, 128 - QK_ROPE_DIM), F32)
    cos_t = jnp.concatenate([cos, cos, zeros], axis=1)
    sin_t = jnp.concatenate([-sin, sin, zeros], axis=1)

    diff = _pool_pre(xs, g_pool_norm)
    x1 = _pool_mix(diff, wp, g_pool_scale, xs)
    x2, ffn0, wd0, (_, g_wkva, g_wkvb, g_wqa, g_wqb, g_wo) = _ffn_fwd(x1, ffn_norm[0:1], wg0, wu0, None, "0", gather_mla)
    wkva = g_wkva.reshape(D, R_kv + 128)
    wkvb = g_wkvb.reshape(N_DEV * R_kv, kvw)
    wqa = g_wqa.reshape(D, R_q)
    wqb = g_wqb.reshape(N_DEV * R_q, hpd * HEAD_PAD)
    wo = g_wo.reshape(H * V_DIM, D)

    hk = _rmsnorm(x2, g_kv_in, "kv_in_norm")
    kv = _mm_plain("kv_a", "nn", hk, wkva)
    ckv, kpe = _kv_mid(kv, g_kv_lat, cos_t, sin_t)
    kvu = _slots_out("kv_b", ckv, wkvb, BF16)

    ha = _rmsnorm(x2, g_attn, "attn_norm")
    ql = _mm_plain("q_a", "nn", ha, wqa)
    qn = _rmsnorm(ql, g_q_lat, "q_latent_norm")
    tab = pl.BlockSpec((_tile(S, ROW_TILE), 128), lambda i, j, k: (i, 0))
    qr = _slots_out("q_b", qn, wqb, BF16, epilogue=(_rope_heads, [cos_t, sin_t], [tab, tab]))
    o, lse, g_wg1, g_wu1, g_wd1 = _attn_fwd(qr, kvu, kpe, H, ride=gather_ffn1)
    wg1, wu1, wd1 = g_wg1.reshape(N_DEV * D, fs), g_wu1.reshape(N_DEV * D, fs), g_wd1.reshape(N_DEV * fs, D)
    x3 = _mm_plain("attn_out", "nn", o, wo, res=x2)
    x4, ffn1, _, _ = _ffn_fwd(x3, ffn_norm[1:2], wg1, wu1, wd1, "1")

    loss_part, dx4, dx4b, d_final = _loss_head(x4, g_final, target)

    (dx3, dx3b, d_ffn1), _, got_ffn1 = _ffn_bwd(x3, ffn_norm[1:2], wg1, wu1, wd1, ffn1, dx4, dx4b, "1", True)

    do = _mm_plain("attn_out_dx", "nt", dx3b, wo, out_dtype=BF16)
    dwo = _mm_plain("attn_out_dw", "tn", o, dx3b, out_dtype=BF16)
    dq, dkvu, dkpe = _attn_bwd(qr, kvu, kpe, do, o, lse.reshape(H, 1, S), cos_t, sin_t, H)

    dqn = _slots_in_nt("q_b_dx", dq, wqb, S)
    dwqb = _slots_dw("q_b_dw", qn, dq, BF16)
    dql, d_q_lat = _rmsnorm_bwd(ql, g_q_lat, dqn, None, "q_latent_norm_bwd", out_dtype=BF16)
    dha = _mm_plain("q_a_dx", "nt", dql, wqa)
    dwqa = _mm_plain("q_a_dw", "tn", ha, dql, out_dtype=BF16)
    dx2, d_attn = _rmsnorm_bwd(x2, g_attn, dha, dx3, "attn_norm_bwd")

    dckv = _slots_in_nt("kv_b_dx", dkvu, wkvb, S)
    dwkvb = _slots_dw("kv_b_dw", ckv, dkvu, BF16)
    dkv, d_kv_lat = _kv_mid_bwd(kv, g_kv_lat, dckv, dkpe, cos_t, sin_t)
    dhk = _mm_plain("kv_a_dx", "nt", dkv, wkva)
    dwkva = _mm_plain("kv_a_dw", "tn", hk, dkv, out_dtype=BF16)
    dx2, dx2b, d_kv_in = _rmsnorm_bwd(x2, g_kv_in, dhk, dx2, "kv_in_norm_bwd", with_bf16=True)

    scatter_mla = _Ride("scatter", [dwkva.reshape(N_DEV, Ds, R_kv + 128), dwkvb.reshape(N_DEV, R_kv, kvw),
                                    dwqa.reshape(N_DEV, Ds, R_q), dwqb.reshape(N_DEV, R_q, hpd * HEAD_PAD),
                                    dwo.reshape(N_DEV, H * V_DIM // N_DEV, D)])
    (dx1, d_ffn0), got_mla, got_ffn0 = _ffn_bwd(x1, ffn_norm[0:1], wg0, wu0, wd0, ffn0, dx2, dx2b, "0", False,
                                                  scatter=scatter_mla)

    ddiff, dwp, d_pool_scale = _pool_mix_bwd(diff, wp, g_pool_scale, dx1)
    grad_x, d_pool_norm = _pool_pre_bwd(xs, g_pool_norm, ddiff, dx1)

    d_gains = jnp.swapaxes(jnp.concatenate([d_pool_norm, d_pool_scale], axis=0).reshape(2, N_DEV, D // N_DEV), 0, 1)
    got_pool_w, got_gains = _all_to_all([jnp.swapaxes(dwp.reshape(G, N_DEV, gws, gw), 0, 1), d_gains], "exchange_pool")

    received = dict(zip(("w_kv_a", "w_kv_b", "w_q_a", "w_q_b", "w_o"), got_mla))
    received.update(zip(("w_gate0", "w_up0", "w_down0"), got_ffn0))
    received.update(zip(("w_gate1", "w_up1", "w_down1"), got_ffn1))
    received.update(pool_w=got_pool_w, gains=got_gains)
    sums = {n: _sum_slots(r, "sum_" + n) for n, r in received.items()}
    grads = {
        "pool_w": sums["pool_w"][None],
        "w_kv_a": sums["w_kv_a"][:, :R_kv + QK_ROPE_DIM],
        "w_kv_b": sums["w_kv_b"],
        "w_q_a": sums["w_q_a"][None],
        "w_q_b": sums["w_q_b"].reshape(R_q, hpd, HEAD_PAD)[:, :, :QK_DIM].reshape(1, R_q, hpd * QK_DIM),
        "w_o": sums["w_o"][None],
        "w_gate": jnp.stack([sums["w_gate0"], sums["w_gate1"]]),
        "w_up": jnp.stack([sums["w_up0"], sums["w_up1"]]),
        "w_down": jnp.stack([sums["w_down0"], sums["w_down1"]]),
        "pool_norm": sums["gains"][0:1],
        "pool_scale": sums["gains"][1:2],
    }

    small = {"kv_in_norm": d_kv_in, "kv_latent_norm": d_kv_lat, "attn_norm": d_attn, "q_latent_norm": d_q_lat,
             "ffn_norm": jnp.concatenate([d_ffn0, d_ffn1], axis=0), "final_norm": d_final}
    small_packed = jnp.concatenate([small[n].reshape(-1) for n in _REPLICATED] + [loss_part.reshape(-1)])
    pad = (-small_packed.shape[0]) % (8 * 128)
    reduced = _all_reduce_small(jnp.pad(small_packed, (0, pad)).reshape(-1, 128)).reshape(-1)
    off = 0
    for n in _REPLICATED:
        grads[n] = reduced[off:off + weights[n].size].reshape(weights[n].shape)
        off += weights[n].size
    loss = reduced[off]

    delta_w, new_m, new_v = {}, {}, {}
    for n in _WEIGHTS:
        delta_w[n], new_m[n], new_v[n] = _adamw(weights[n], grads[n], m_in[n], v_in[n], "adamw_" + n)

    return (loss, grad_x.reshape(x.shape), *[grads[n] for n in _WEIGHTS], *[delta_w[n] for n in _WEIGHTS],
            *[new_m[n] for n in _WEIGHTS], *[new_v[n] for n in _WEIGHTS])
```

```python
import math

import jax
import jax.numpy as jnp
import numpy as np
from jax import lax
from jax.experimental import pallas as pl
from jax.experimental.pallas import tpu as pltpu

F32 = jnp.float32
BF16 = jnp.bfloat16

N_DEV = 8
POOL_WINDOWS = (2, 4, 8, 16)
POOL_HALO = 16
QK_NOPE_DIM = 128
QK_ROPE_DIM = 64
QK_DIM = QK_NOPE_DIM + QK_ROPE_DIM
HEAD_PAD = 256
V_DIM = 128
ROPE_BASE = 10000.0
ATTN_SCALE = 1.0 / math.sqrt(QK_DIM)
EXP2_SCALE = ATTN_SCALE * math.log2(math.e)
NORM_EPS = 1e-6
ADAM_LR = 0.001
ADAM_B1 = 0.9
ADAM_B2 = 0.999
ADAM_EPS = 1e-08
ADAM_WD = 0.01
ADAM_STEP = 10

ROW_TILE = 512
ATT_BLOCK = 1024
ATT_ROW_CHUNK = 256
MM_TN, MM_TK = 512, 512
VMEM_LIMIT = 48 * 1024 * 1024

_pcall = pl.pallas_call


def _params(sem):
    return pltpu.CompilerParams(dimension_semantics=sem, vmem_limit_bytes=VMEM_LIMIT)


def _tile(dim, default):
    if dim % default == 0:
        return default
    assert dim <= 2 * default, (dim, default)
    return dim


def _row_tile(rows, width):
    ts = _tile(rows, ROW_TILE)
    while ts * width * 4 > (2 << 20) and ts % 16 == 0:
        ts //= 2
    return ts


def _rms_scale(x):
    return lax.rsqrt(jnp.mean(x * x, axis=-1, keepdims=True) + NORM_EPS)


def _rms_bwd(x, g, dy):
    r = _rms_scale(x)
    xn = x * r
    u = dy * g
    dx = r * (u - xn * jnp.mean(u * xn, axis=-1, keepdims=True))
    return dx, dy * xn


def _swap_halves(x):
    lane = lax.broadcasted_iota(jnp.int32, x.shape, 1)
    return jnp.where(lane < QK_ROPE_DIM // 2, pltpu.roll(x, 128 - QK_ROPE_DIM // 2, 1), pltpu.roll(x, QK_ROPE_DIM // 2, 1))


def _rope(x, cos_t, sin_t):
    return x * cos_t + _swap_halves(x) * sin_t


def _rmsnorm(x, g, name):
    S, D = x.shape
    ts = _row_tile(S, D)

    def body(x_ref, g_ref, o_ref):
        xf = x_ref[...]
        o_ref[...] = (xf * _rms_scale(xf) * g_ref[...]).astype(o_ref.dtype)

    return _pcall(
        body, name=name, grid=(S // ts,),
        in_specs=[pl.BlockSpec((ts, D), lambda i: (i, 0)), pl.BlockSpec((1, D), lambda i: (0, 0))],
        out_specs=pl.BlockSpec((ts, D), lambda i: (i, 0)),
        out_shape=jax.ShapeDtypeStruct((S, D), BF16),
        compiler_params=_params(("parallel",)),
    )(x, g)


def _rmsnorm_bwd(x, g, dy, res, name, out_dtype=F32, with_bf16=False):
    S, D = x.shape
    ts = _row_tile(S, D)
    has_res = res is not None

    def body(*refs):
        refs = list(refs)
        x_ref, g_ref, dy_ref = refs[:3]
        res_ref = refs[3] if has_res else None
        outs = refs[3 + has_res:]
        dx_ref, dg_ref = outs[0], outs[-1]
        dx, dgt = _rms_bwd(x_ref[...], g_ref[...], dy_ref[...].astype(F32))
        if has_res:
            dx = res_ref[...] + dx
        dx_ref[...] = dx.astype(dx_ref.dtype)
        if with_bf16:
            outs[1][...] = dx.astype(BF16)

        @pl.when(pl.program_id(0) == 0)
        def _():
            dg_ref[...] = jnp.zeros_like(dg_ref)

        dg_ref[...] += jnp.sum(dgt, axis=0, keepdims=True)

    row = pl.BlockSpec((ts, D), lambda i: (i, 0))
    vec = pl.BlockSpec((1, D), lambda i: (0, 0))
    out_specs = [row] + ([row] if with_bf16 else []) + [vec]
    out_shape = ([jax.ShapeDtypeStruct((S, D), out_dtype)] + ([jax.ShapeDtypeStruct((S, D), BF16)] if with_bf16 else [])
                 + [jax.ShapeDtypeStruct((1, D), F32)])
    return _pcall(
        body, name=name, grid=(S // ts,),
        in_specs=[row, vec, row] + ([row] if has_res else []),
        out_specs=tuple(out_specs), out_shape=tuple(out_shape),
        compiler_params=_params(("arbitrary",)),
    )(*([x, g, dy] + ([res] if has_res else [])))


def _pool_pre(x, g):
    S, D = x.shape
    ts = _row_tile(S, D)
    gw = D // len(POOL_WINDOWS)
    hb = ts // POOL_HALO

    def body(x_ref, halo_ref, g_ref, o_ref):
        i = pl.program_id(0)
        xx = jnp.concatenate([halo_ref[...], x_ref[...]], axis=0)
        h = xx * _rms_scale(xx) * g_ref[...]
        t = i * ts - POOL_HALO + lax.broadcasted_iota(jnp.int32, (ts + POOL_HALO, 1), 0)
        h = jnp.where(t >= 0, h, 0.0)
        tf = t[POOL_HALO:].astype(F32)
        for gi, w in enumerate(POOL_WINDOWS):
            hg = h[:, gi * gw:(gi + 1) * gw]
            acc, span = hg, 1
            while span < w:
                acc = acc + pltpu.roll(acc, span, 0)
                span *= 2
            count = jnp.minimum(tf + 1.0, float(w))
            diff = acc[POOL_HALO:] / count - hg[POOL_HALO:]
            o_ref[:, gi * gw:(gi + 1) * gw] = diff.astype(o_ref.dtype)

    return _pcall(
        body, name="pool_pre", grid=(S // ts,),
        in_specs=[pl.BlockSpec((ts, D), lambda i: (i, 0)),
                  pl.BlockSpec((POOL_HALO, D), lambda i: (jnp.maximum(i * hb - 1, 0), 0)),
                  pl.BlockSpec((1, D), lambda i: (0, 0))],
        out_specs=pl.BlockSpec((ts, D), lambda i: (i, 0)),
        out_shape=jax.ShapeDtypeStruct((S, D), BF16),
        compiler_params=_params(("parallel",)),
    )(x, x, g)


def _pool_pre_bwd(x, g, dd, res):
    S, D = x.shape
    ts = _row_tile(S, D)
    gw = D // len(POOL_WINDOWS)
    hb = ts // POOL_HALO
    last_halo = S // POOL_HALO - 1

    def body(x_ref, g_ref, dd_ref, halo_ref, res_ref, dx_ref, dg_ref):
        i = pl.program_id(0)
        ee = jnp.concatenate([dd_ref[...], halo_ref[...]], axis=0)
        t = i * ts + lax.broadcasted_iota(jnp.int32, (ts + POOL_HALO, 1), 0)
        ee = jnp.where(t < S, ee, 0.0)
        tf = t.astype(F32)
        n = ts + POOL_HALO
        for gi, w in enumerate(POOL_WINDOWS):
            eg = ee[:, gi * gw:(gi + 1) * gw]
            acc, span = eg / jnp.minimum(tf + 1.0, float(w)), 1
            while span < w:
                acc = acc + pltpu.roll(acc, n - span, 0)
                span *= 2
            dx_ref[:, gi * gw:(gi + 1) * gw] = acc[:ts] - eg[:ts]
        dx, dgt = _rms_bwd(x_ref[...], g_ref[...], dx_ref[...])
        dx_ref[...] = res_ref[...] + dx

        @pl.when(i == 0)
        def _():
            dg_ref[...] = jnp.zeros_like(dg_ref)

        dg_ref[...] += jnp.sum(dgt, axis=0, keepdims=True)

    row = pl.BlockSpec((ts, D), lambda i: (i, 0))
    vec = pl.BlockSpec((1, D), lambda i: (0, 0))
    return _pcall(
        body, name="pool_pre_bwd", grid=(S // ts,),
        in_specs=[row, vec, row,
                  pl.BlockSpec((POOL_HALO, D), lambda i: (jnp.minimum((i + 1) * hb, last_halo), 0)), row],
        out_specs=(row, vec),
        out_shape=(jax.ShapeDtypeStruct((S, D), F32), jax.ShapeDtypeStruct((1, D), F32)),
        compiler_params=_params(("arbitrary",)),
    )(x, g, dd, dd, res)


def _pool_mix(diff, w, scale, x):
    S, D = x.shape
    G, gw, _ = w.shape
    ts = _tile(S, ROW_TILE)

    def body(d_ref, w_ref, s_ref, x_ref, o_ref):
        mo = jnp.dot(d_ref[...], w_ref[0], preferred_element_type=F32)
        o_ref[...] = x_ref[...] + mo * s_ref[...]

    blk = pl.BlockSpec((ts, gw), lambda i, g: (i, g))
    return _pcall(
        body, name="pool_mix", grid=(S // ts, G),
        in_specs=[blk, pl.BlockSpec((1, gw, gw), lambda i, g: (g, 0, 0)), pl.BlockSpec((1, gw), lambda i, g: (0, g)), blk],
        out_specs=blk,
        out_shape=jax.ShapeDtypeStruct((S, D), F32),
        compiler_params=_params(("parallel", "parallel")),
    )(diff, w, scale, x)


def _pool_mix_bwd(diff, w, scale, dy):
    S, D = dy.shape
    G, gw, _ = w.shape
    ts = _tile(S, ROW_TILE)

    def body(d_ref, w_ref, s_ref, dy_ref, dd_ref, dw_ref, ds_ref):
        i = pl.program_id(1)
        d = d_ref[...]
        dy = dy_ref[...]
        mo = jnp.dot(d, w_ref[0], preferred_element_type=F32)
        dmo = (dy * s_ref[...]).astype(BF16)
        dd_ref[...] = lax.dot_general(dmo, w_ref[0], (((1,), (1,)), ((), ())), preferred_element_type=F32)

        @pl.when(i == 0)
        def _():
            dw_ref[...] = jnp.zeros_like(dw_ref)
            ds_ref[...] = jnp.zeros_like(ds_ref)

        dw_ref[0] += lax.dot_general(d, dmo, (((0,), (0,)), ((), ())), preferred_element_type=F32)
        ds_ref[...] += jnp.sum(dy * mo, axis=0, keepdims=True)

    blk = pl.BlockSpec((ts, gw), lambda g, i: (i, g))
    wspec = pl.BlockSpec((1, gw, gw), lambda g, i: (g, 0, 0))
    vec = pl.BlockSpec((1, gw), lambda g, i: (0, g))
    return _pcall(
        body, name="pool_mix_bwd", grid=(G, S // ts),
        in_specs=[blk, wspec, vec, blk],
        out_specs=(blk, wspec, vec),
        out_shape=(jax.ShapeDtypeStruct((S, D), F32), jax.ShapeDtypeStruct((G, gw, gw), F32),
                   jax.ShapeDtypeStruct((1, D), F32)),
        compiler_params=_params(("parallel", "arbitrary")),
    )(diff, w, scale, dy)


def _loss_head(x, g, target):
    S, D = x.shape
    ts = _row_tile(S, D)

    def body(x_ref, g_ref, t_ref, loss_ref, dx_ref, dxb_ref, dg_ref):
        xf = x_ref[...]
        gv = g_ref[...]
        err = xf * _rms_scale(xf) * gv - t_ref[...]
        dx, dgt = _rms_bwd(xf, gv, err * (1.0 / D))
        dx_ref[...] = dx
        dxb_ref[...] = dx.astype(BF16)

        @pl.when(pl.program_id(0) == 0)
        def _():
            loss_ref[...] = jnp.zeros_like(loss_ref)
            dg_ref[...] = jnp.zeros_like(dg_ref)

        part = 0.5 * jnp.sum(jnp.sum(err * err, axis=-1, keepdims=True) * (1.0 / D), axis=0, keepdims=True)
        loss_ref[...] += jnp.broadcast_to(part, loss_ref.shape)
        dg_ref[...] += jnp.sum(dgt, axis=0, keepdims=True)

    row = pl.BlockSpec((ts, D), lambda i: (i, 0))
    vec = pl.BlockSpec((1, D), lambda i: (0, 0))
    return _pcall(
        body, name="loss_head", grid=(S // ts,),
        in_specs=[row, vec, row],
        out_specs=(pl.BlockSpec((1, 128), lambda i: (0, 0)), row, row, vec),
        out_shape=(jax.ShapeDtypeStruct((1, 128), F32), jax.ShapeDtypeStruct((S, D), F32),
                   jax.ShapeDtypeStruct((S, D), BF16), jax.ShapeDtypeStruct((1, D), F32)),
        compiler_params=_params(("arbitrary",)),
    )(x, g, target)


_DIMS = {"nn": (((1,), (0,)), ((), ())), "nt": (((1,), (1,)), ((), ())), "tn": (((0,), (0,)), ((), ()))}


def _mm(name, mode, a, b, tiles, grid, a_map, b_map, o_map, out_shape, out_dtype=F32, res=None, a2=None, b2=None,
        ride=None, epilogue=None):
    tm, tn, tk = tiles
    nk = grid[-1]
    dual, has_res = a2 is not None, res is not None
    n_main = 2 + 2 * dual
    dn = _DIMS[mode]

    def body(ins, outs, scratch):
        o_ref = outs[0]

        def product():
            part = lax.dot_general(ins[0][...].astype(BF16), ins[1][...].astype(BF16), dn, preferred_element_type=F32)
            if dual:
                part += lax.dot_general(ins[2][...].astype(BF16), ins[3][...].astype(BF16), dn,
                                        preferred_element_type=F32)
            return part

        def finish(out):
            if has_res:
                out = ins[n_main][...] + out
            if epilogue is not None:
                out = epilogue[0](out, *[r[...] for r in ins[n_main + has_res:]])
            o_ref[...] = out.astype(o_ref.dtype)

        if nk == 1:
            finish(product())
            return
        acc_ref = scratch[0]
        k = pl.program_id(2)

        @pl.when(k == 0)
        def _():
            acc_ref[...] = jnp.zeros_like(acc_ref)

        acc_ref[...] += product()

        @pl.when(k == nk - 1)
        def _():
            finish(acc_ref[...])

    a_spec = pl.BlockSpec((tk, tm) if mode == "tn" else (tm, tk), a_map)
    b_spec = pl.BlockSpec((tn, tk) if mode == "nt" else (tk, tn), b_map)
    o_spec = pl.BlockSpec((tm, tn), o_map)
    operands, in_specs = [a, b], [a_spec, b_spec]
    if dual:
        operands += [a2, b2]
        in_specs += [a_spec, b_spec]
    if has_res:
        operands.append(res)
        in_specs.append(o_spec)
    if epilogue is not None:
        operands += list(epilogue[1])
        in_specs += list(epilogue[2])
    outs = _pallas(body, name, grid, operands, in_specs, [o_spec], [jax.ShapeDtypeStruct(out_shape, out_dtype)],
                   scratch=[pltpu.VMEM((tm, tn), F32)] if nk > 1 else [],
                   sem=("parallel", "parallel", "arbitrary"), ride=ride)
    return (outs[0], outs[1:]) if ride else outs[0]


def _mm_plain(name, mode, a, b, out_dtype=F32, res=None):
    if mode == "tn":
        (K, M), N = a.shape, b.shape[1]
    elif mode == "nt":
        (M, K), N = a.shape, b.shape[0]
    else:
        (M, K), N = a.shape, b.shape[1]
    if mode == "tn":
        tm, tn, tk = _tile(M, 2 * MM_TN), _tile(N, 2 * MM_TN), _tile(K, ROW_TILE)
    else:
        tm = _tile(M, ROW_TILE)
        tk = K if K <= 4 * MM_TK else _tile(K, MM_TK)
        tn = N if N * tk * 2 <= (8 << 20) else _tile(N, MM_TN)
    a_map = (lambda i, j, k: (k, i)) if mode == "tn" else (lambda i, j, k: (i, k))
    b_map = (lambda i, j, k: (j, k)) if mode == "nt" else (lambda i, j, k: (k, j))
    return _mm(name, mode, a, b, (tm, tn, tk), (M // tm, N // tn, K // tk), a_map, b_map, lambda i, j, k: (i, j),
               (M, N), out_dtype, res)


def _slots_out(name, a, w_slots, out_dtype, epilogue=None):
    S, K = a.shape
    n = w_slots.shape[1]
    tm = _tile(S, ROW_TILE)
    nmb = S // tm
    return _mm(name, "nn", a, w_slots, (tm, n, K), (nmb, N_DEV, 1),
               lambda i, j, k: (i, 0), lambda i, j, k: (j, 0), lambda i, j, k: (j * nmb + i, 0),
               (N_DEV * S, n), out_dtype, epilogue=epilogue)


def _slots_in_nt(name, a_slots, w_slots, S, a2=None, w2=None, ride=None):
    n = a_slots.shape[1]
    K = w_slots.shape[0] // N_DEV
    tm = _tile(S, ROW_TILE)
    nmb = S // tm
    return _mm(name, "nt", a_slots, w_slots, (tm, K, n), (nmb, 1, N_DEV),
               lambda i, q, j: (j * nmb + i, 0), lambda i, q, j: (j, 0), lambda i, q, j: (i, 0),
               (S, K), F32, a2=a2, b2=w2, ride=ride)


def _slots_dw(name, a, d_slots, out_dtype=F32, ride=None):
    S, K = a.shape
    n = d_slots.shape[1]
    tk = _tile(S, ROW_TILE)
    nkb = S // tk
    return _mm(name, "tn", a, d_slots, (K, n, tk), (N_DEV, 1, nkb),
               lambda j, q, k: (k, 0), lambda j, q, k: (j * nkb + k, 0), lambda j, q, k: (j, 0),
               (N_DEV * K, n), out_dtype, ride=ride)


def _small_slots_out(name, a, w_slots, out_dtype, epilogue=None):
    S, K = a.shape
    n = w_slots.shape[1]
    tm = _tile(S, ROW_TILE)
    extra = epilogue[1] if epilogue else []

    def body(*refs):
        a_ref, w_ref, o_ref = refs[0], refs[1], refs[-1]
        av = a_ref[...].astype(BF16)
        tables = [r[...] for r in refs[2:-1]]
        for j in range(N_DEV):
            out = jnp.dot(av, w_ref[j], preferred_element_type=F32)
            if epilogue:
                out = epilogue[0](out, *tables)
            o_ref[j] = out.astype(o_ref.dtype)

    return _pcall(
        body, name=name, grid=(S // tm,),
        in_specs=[pl.BlockSpec((tm, K), lambda i: (i, 0)), pl.BlockSpec((N_DEV, K, n), lambda i: (0, 0, 0))]
        + (list(epilogue[2]) if epilogue else []),
        out_specs=pl.BlockSpec((N_DEV, tm, n), lambda i: (0, i, 0)),
        out_shape=jax.ShapeDtypeStruct((N_DEV, S, n), out_dtype),
        compiler_params=_params(("parallel",)),
    )(a, w_slots.reshape(N_DEV, K, n), *extra).reshape(N_DEV * S, n)


def _small_slots_in_nt(name, a_slots, w_slots, S):
    n = a_slots.shape[1]
    K = w_slots.shape[0] // N_DEV
    tm = _tile(S, ROW_TILE)

    def body(a_ref, w_ref, o_ref):
        total = lax.dot_general(a_ref[0], w_ref[0], _DIMS["nt"], preferred_element_type=F32)
        for j in range(1, N_DEV):
            total += lax.dot_general(a_ref[j], w_ref[j], _DIMS["nt"], preferred_element_type=F32)
        o_ref[...] = total

    return _pcall(
        body, name=name, grid=(S // tm,),
        in_specs=[pl.BlockSpec((N_DEV, tm, n), lambda i: (0, i, 0)), pl.BlockSpec((N_DEV, K, n), lambda i: (0, 0, 0))],
        out_specs=pl.BlockSpec((tm, K), lambda i: (i, 0)),
        out_shape=jax.ShapeDtypeStruct((S, K), F32),
        compiler_params=_params(("parallel",)),
    )(a_slots.reshape(N_DEV, S, n), w_slots.reshape(N_DEV, K, n))


def _small_slots_dw(name, a, d_slots, out_dtype):
    S, K = a.shape
    n = d_slots.shape[1]
    tk = _tile(S, ROW_TILE)
    nkb = S // tk

    def body(a_ref, d_ref, o_ref, acc_ref):
        k = pl.program_id(0)
        at = a_ref[...].astype(F32).T.astype(BF16)

        def add(first):
            for j in range(N_DEV):
                part = jnp.dot(at, d_ref[j], preferred_element_type=F32)
                if first:
                    acc_ref[j] = part
                else:
                    acc_ref[j] += part

        @pl.when(k == 0)
        def _():
            add(True)

        @pl.when(k > 0)
        def _():
            add(False)

        @pl.when(k == nkb - 1)
        def _():
            o_ref[...] = acc_ref[...].astype(o_ref.dtype)

    return _pcall(
        body, name=name, grid=(nkb,),
        in_specs=[pl.BlockSpec((tk, K), lambda k: (k, 0)), pl.BlockSpec((N_DEV, tk, n), lambda k: (0, k, 0))],
        out_specs=pl.BlockSpec((N_DEV, K, n), lambda k: (0, 0, 0)),
        out_shape=jax.ShapeDtypeStruct((N_DEV, K, n), out_dtype),
        scratch_shapes=[pltpu.VMEM((N_DEV, K, n), F32)],
        compiler_params=_params(("arbitrary",)),
    )(a, d_slots.reshape(N_DEV, S, n)).reshape(N_DEV * K, n)


def _sigmoid(x):
    return 1.0 / (1.0 + jnp.exp(-x))


def _ffn_up(h, wg, wu, name, ride=None):
    S, D = h.shape
    fs = wg.shape[1]
    tm = _tile(S, ROW_TILE)
    nmb = S // tm

    def body(ins, outs, scratch):
        h_ref, wg_ref, wu_ref = ins
        g_ref, u_ref, a_ref = outs
        hv = h_ref[...]
        g = jnp.dot(hv, wg_ref[...], preferred_element_type=F32)
        u = jnp.dot(hv, wu_ref[...], preferred_element_type=F32)
        g_ref[...] = g.astype(g_ref.dtype)
        u_ref[...] = u.astype(u_ref.dtype)
        a_ref[...] = (g * _sigmoid(g) * u).astype(a_ref.dtype)

    w_spec = pl.BlockSpec((D, fs), lambda i, j: (j, 0))
    o_spec = pl.BlockSpec((tm, fs), lambda i, j: (j * nmb + i, 0))
    sds = jax.ShapeDtypeStruct((N_DEV * S, fs), BF16)
    return _pallas(body, name, (nmb, N_DEV), [h, wg, wu],
                   [pl.BlockSpec((tm, D), lambda i, j: (i, 0)), w_spec, w_spec], [o_spec, o_spec, o_spec],
                   [sds, sds, sds], sem=("parallel", "arbitrary"), ride=ride)


def _ffn_dact(dy, wd, g, u, name, ride=None):
    S, D = dy.shape
    fs = g.shape[1]
    tm = _tile(S, ROW_TILE)
    nmb = S // tm

    def body(ins, outs, scratch):
        dy_ref, wd_ref, g_ref, u_ref = ins
        dg_ref, du_ref = outs
        da = lax.dot_general(dy_ref[...], wd_ref[...], _DIMS["nt"], preferred_element_type=F32)
        g, u = g_ref[...].astype(F32), u_ref[...].astype(F32)
        sig = _sigmoid(g)
        dg_ref[...] = (da * u * (sig * (1.0 + g * (1.0 - sig)))).astype(dg_ref.dtype)
        du_ref[...] = (da * (g * sig)).astype(du_ref.dtype)

    o_spec = pl.BlockSpec((tm, fs), lambda i, j: (j * nmb + i, 0))
    sds = jax.ShapeDtypeStruct((N_DEV * S, fs), BF16)
    return _pallas(body, name, (nmb, N_DEV), [dy, wd, g, u],
                   [pl.BlockSpec((tm, D), lambda i, j: (i, 0)), pl.BlockSpec((fs, D), lambda i, j: (j, 0)),
                    o_spec, o_spec], [o_spec, o_spec], [sds, sds], sem=("parallel", "arbitrary"), ride=ride)


def _rope_heads(q, cos_t, sin_t):
    parts = []
    for c0 in range(0, q.shape[1], HEAD_PAD):
        parts += [q[:, c0:c0 + QK_NOPE_DIM], _rope(q[:, c0 + QK_NOPE_DIM:c0 + HEAD_PAD], cos_t, sin_t)]
    return jnp.concatenate(parts, axis=1)


def _kv_mid(kv, g, cos_t, sin_t):
    S, W = kv.shape
    R = W - 128
    ts = _tile(S, ROW_TILE)

    def body(kv_ref, g_ref, c_ref, s_ref, c_out, k_out):
        cf = kv_ref[:, :R]
        c_out[...] = (cf * _rms_scale(cf) * g_ref[...]).astype(c_out.dtype)
        k_out[...] = _rope(kv_ref[:, R:], c_ref[...], s_ref[...]).astype(k_out.dtype)

    tab = pl.BlockSpec((ts, 128), lambda i: (i, 0))
    return _pcall(
        body, name="kv_mid", grid=(S // ts,),
        in_specs=[pl.BlockSpec((ts, W), lambda i: (i, 0)), pl.BlockSpec((1, R), lambda i: (0, 0)), tab, tab],
        out_specs=(pl.BlockSpec((ts, R), lambda i: (i, 0)), tab),
        out_shape=(jax.ShapeDtypeStruct((S, R), BF16), jax.ShapeDtypeStruct((S, 128), BF16)),
        compiler_params=_params(("parallel",)),
    )(kv, g, cos_t, sin_t)


def _kv_mid_bwd(kv, g, dc, dkpe, cos_t, sin_t):
    S, W = kv.shape
    R = W - 128
    H = dkpe.shape[0]
    ts = _row_tile(S, H * 128)

    def body(kv_ref, g_ref, dc_ref, dk_ref, c_ref, s_ref, dkv_ref, dg_ref):
        dx, dgt = _rms_bwd(kv_ref[:, :R], g_ref[...], dc_ref[...])
        dkv_ref[:, :R] = dx.astype(dkv_ref.dtype)
        dk = dk_ref[0]
        for h in range(1, H):
            dk = dk + dk_ref[h]
        dkv_ref[:, R:] = _rope(dk, c_ref[...], -s_ref[...]).astype(dkv_ref.dtype)

        @pl.when(pl.program_id(0) == 0)
        def _():
            dg_ref[...] = jnp.zeros_like(dg_ref)

        dg_ref[...] += jnp.sum(dgt, axis=0, keepdims=True)

    tab = pl.BlockSpec((ts, 128), lambda i: (i, 0))
    vec = pl.BlockSpec((1, R), lambda i: (0, 0))
    return _pcall(
        body, name="kv_mid_bwd", grid=(S // ts,),
        in_specs=[pl.BlockSpec((ts, W), lambda i: (i, 0)), vec, pl.BlockSpec((ts, R), lambda i: (i, 0)),
                  pl.BlockSpec((H, ts, 128), lambda i: (0, i, 0)), tab, tab],
        out_specs=(pl.BlockSpec((ts, W), lambda i: (i, 0)), vec),
        out_shape=(jax.ShapeDtypeStruct((S, W), BF16), jax.ShapeDtypeStruct((1, R), F32)),
        compiler_params=_params(("arbitrary",)),
    )(kv, g, dc, dkpe, cos_t, sin_t)


def _block_pairs(nb, by_key):
    pairs = ([(qi, ki) for ki in range(nb) for qi in range(ki, nb)] if by_key
             else [(qi, ki) for qi in range(nb) for ki in range(qi + 1)])
    return jnp.asarray(np.array([p[0] for p in pairs], np.int32)), jnp.asarray(np.array([p[1] for p in pairs], np.int32))


def _causal_mask(blk, transposed=False, rows=None, row0=0):
    shape = (blk if rows is None else rows, blk)
    r = lax.broadcasted_iota(jnp.int32, shape, 0) + row0
    c = lax.broadcasted_iota(jnp.int32, shape, 1)
    return (r <= c) if transposed else (c <= r)


def _attn_specs(S, blk, hpd):
    nb = S // blk
    w = hpd * HEAD_PAD
    return dict(
        q=pl.BlockSpec((blk, w), lambda j, t, qt, kt: (j * nb + qt[t], 0)),
        kv=pl.BlockSpec((blk, w), lambda j, t, qt, kt: (j * nb + kt[t], 0)),
        kp=pl.BlockSpec((blk, 128), lambda j, t, qt, kt: (kt[t], 0)),
        do=pl.BlockSpec((blk, hpd * V_DIM), lambda j, t, qt, kt: (qt[t], j)),
        col=pl.BlockSpec((hpd, blk, 1), lambda j, t, qt, kt: (j, qt[t], 0)),
        row=pl.BlockSpec((hpd, 1, blk), lambda j, t, qt, kt: (j, 0, qt[t])),
        tab=pl.BlockSpec((blk, 128), lambda j, t, qt, kt: (qt[t], 0)))


def _head_k(kv_ref, kp, hh):
    return jnp.concatenate([kv_ref[:, hh * HEAD_PAD:hh * HEAD_PAD + QK_NOPE_DIM], kp], axis=1)


def _head_v(kv_ref, hh):
    return kv_ref[:, hh * HEAD_PAD + QK_NOPE_DIM:(hh + 1) * HEAD_PAD]


def _attn_fwd(q, kvu, kpe, H, ride=None):
    S = kpe.shape[0]
    hpd = H // N_DEV
    blk = _tile(S, ATT_BLOCK)
    q_tab, k_tab = _block_pairs(S // blk, by_key=False)
    rc = min(blk, ATT_ROW_CHUNK)

    def body(ins, outs, scratch):
        qt, kt, q_ref, kv_ref, kp_ref = ins
        o_ref, lse_ref = outs
        m_ref, acc_ref = scratch
        t = pl.program_id(1)
        qi, ki = qt[t], kt[t]

        @pl.when(ki == 0)
        def _():
            m_ref[...] = jnp.full_like(m_ref, -jnp.inf)
            acc_ref[...] = jnp.zeros_like(acc_ref)

        def step(masked):
            kp = kp_ref[...]
            ones = jnp.ones((blk, 128), BF16)
            for hh in range(hpd):
                k = _head_k(kv_ref, kp, hh)
                v_ext = jnp.concatenate([_head_v(kv_ref, hh), ones], axis=1)
                for r0 in range(0, blk, rc):
                    rows = slice(r0, r0 + rc)
                    s = lax.dot_general(q_ref[rows, hh * HEAD_PAD:(hh + 1) * HEAD_PAD], k, _DIMS["nt"],
                                        preferred_element_type=F32)
                    if masked:
                        s = jnp.where(_causal_mask(blk, rows=rc, row0=r0), s, -jnp.inf)
                    m_old = m_ref[hh, rows]
                    m_new = jnp.maximum(m_old, jnp.max(s, axis=-1, keepdims=True))
                    p = jnp.exp2((s - m_new) * EXP2_SCALE).astype(BF16)
                    acc_ref[hh, rows] = (jnp.exp2((m_old - m_new) * EXP2_SCALE) * acc_ref[hh, rows]
                                         + jnp.dot(p, v_ext, preferred_element_type=F32))
                    m_ref[hh, rows] = m_new

        @pl.when(ki < qi)
        def _():
            step(False)

        @pl.when(ki == qi)
        def _():
            step(True)
            for hh in range(hpd):
                l = acc_ref[hh, :, V_DIM:]
                o_ref[:, hh * V_DIM:(hh + 1) * V_DIM] = (acc_ref[hh, :, :V_DIM] / l).astype(o_ref.dtype)
                lse_ref[hh] = m_ref[hh] * EXP2_SCALE + jnp.log2(l[:, :1])

    sp = _attn_specs(S, blk, hpd)
    return _pallas(body, "attn_fwd", (N_DEV, q_tab.shape[0]), [q, kvu, kpe], [sp["q"], sp["kv"], sp["kp"]],
                   [sp["do"], sp["col"]],
                   [jax.ShapeDtypeStruct((S, H * V_DIM), BF16), jax.ShapeDtypeStruct((H, S, 1), F32)],
                   scratch=[pltpu.VMEM((hpd, blk, 1), F32), pltpu.VMEM((hpd, blk, 2 * V_DIM), F32)],
                   sem=("parallel", "arbitrary"), ride=ride, prefetch=[q_tab, k_tab])


def _attn_bwd(q, kvu, kpe, do, o, lse_row, cos_t, sin_t, H):
    S = kpe.shape[0]
    hpd = H // N_DEV
    blk = _tile(S, ATT_BLOCK)
    nb = S // blk
    q_tab, k_tab = _block_pairs(nb, by_key=True)

    def body(qt, kt, q_ref, kv_ref, kp_ref, do_ref, o_ref, lse_ref, c_ref, s_ref, dq_ref, dkv_ref, dkp_ref,
             dq_acc, dk_acc, dv_acc):
        t = pl.program_id(1)
        qi, ki = qt[t], kt[t]
        q_rows = pl.ds(pl.multiple_of(qi * blk, blk), blk)

        def step(diag):
            kp = kp_ref[...]
            ones = jnp.ones((8, V_DIM), BF16)
            for hh in range(hpd):
                k = _head_k(kv_ref, kp, hh)
                qv = q_ref[:, hh * HEAD_PAD:(hh + 1) * HEAD_PAD]
                cols = slice(hh * V_DIM, (hh + 1) * V_DIM)
                dov = do_ref[:, cols]
                doo = (dov.astype(F32) * o_ref[:, cols].astype(F32)).astype(BF16)
                delta = lax.dot_general(ones, doo, _DIMS["nt"], preferred_element_type=F32)[:1]
                st = lax.dot_general(k, qv, _DIMS["nt"], preferred_element_type=F32)
                pt = jnp.exp2(st * EXP2_SCALE - lse_ref[hh])
                if diag:
                    pt = jnp.where(_causal_mask(blk, transposed=True), pt, 0.0)
                dv_new = jnp.dot(pt.astype(BF16), dov, preferred_element_type=F32)
                dpt = lax.dot_general(_head_v(kv_ref, hh), dov, _DIMS["nt"], preferred_element_type=F32)
                dst = (pt * (dpt - delta)).astype(BF16)
                dk_new = jnp.dot(dst, qv, preferred_element_type=F32)
                dq_new = lax.dot_general(dst, k, _DIMS["tn"], preferred_element_type=F32)
                if diag:
                    dv_acc[hh] = dv_new
                    dk_acc[hh] = dk_new
                else:
                    dv_acc[hh] += dv_new
                    dk_acc[hh] += dk_new

                @pl.when(ki == 0)
                def _():
                    dq_acc[hh, q_rows] = dq_new

                @pl.when(ki > 0)
                def _():
                    dq_acc[hh, q_rows] += dq_new

        @pl.when(qi == ki)
        def _():
            step(True)
            for hh in range(hpd):
                c0 = hh * HEAD_PAD
                dq = dq_acc[hh, q_rows] * ATTN_SCALE
                dq_ref[:, c0:c0 + QK_NOPE_DIM] = dq[:, :QK_NOPE_DIM].astype(dq_ref.dtype)
                dq_ref[:, c0 + QK_NOPE_DIM:c0 + HEAD_PAD] = _rope(dq[:, QK_NOPE_DIM:], c_ref[...],
                                                                  -s_ref[...]).astype(dq_ref.dtype)

        @pl.when(qi > ki)
        def _():
            step(False)

        @pl.when(qi == nb - 1)
        def _():
            for hh in range(hpd):
                c0 = hh * HEAD_PAD
                dkv_ref[:, c0:c0 + QK_NOPE_DIM] = (dk_acc[hh, :, :QK_NOPE_DIM] * ATTN_SCALE).astype(dkv_ref.dtype)
                dkv_ref[:, c0 + QK_NOPE_DIM:c0 + HEAD_PAD] = dv_acc[hh].astype(dkv_ref.dtype)
                dkp_ref[hh] = dk_acc[hh, :, QK_NOPE_DIM:] * ATTN_SCALE

    sp = _attn_specs(S, blk, hpd)
    key_tab = pl.BlockSpec((blk, 128), lambda j, t, qt, kt: (kt[t], 0))
    grid_spec = pltpu.PrefetchScalarGridSpec(
        num_scalar_prefetch=2, grid=(N_DEV, q_tab.shape[0]),
        in_specs=[sp["q"], sp["kv"], sp["kp"], sp["do"], sp["do"], sp["row"], key_tab, key_tab],
        out_specs=(sp["kv"], sp["kv"], pl.BlockSpec((hpd, blk, 128), lambda j, t, qt, kt: (j, kt[t], 0))),
        scratch_shapes=[pltpu.VMEM((hpd, S, HEAD_PAD), F32), pltpu.VMEM((hpd, blk, HEAD_PAD), F32),
                        pltpu.VMEM((hpd, blk, V_DIM), F32)])
    return _pcall(
        body, name="attn_bwd", grid_spec=grid_spec,
        out_shape=(jax.ShapeDtypeStruct(q.shape, BF16), jax.ShapeDtypeStruct(kvu.shape, BF16),
                   jax.ShapeDtypeStruct((H, S, 128), F32)),
        compiler_params=_params(("parallel", "arbitrary")),
    )(q_tab, k_tab, q, kvu, kpe, do, o, lse_row, cos_t, sin_t)


def _mesh_pos():
    return lax.axis_index("x"), lax.axis_index("y"), lax.axis_index("c")


def _flip(pos, k):
    x, y, c = pos
    return (1 - x if k & 4 else x, 1 - y if k & 2 else y, 1 - c if k & 1 else c)


def _rank(pos):
    return 4 * pos[0] + 2 * pos[1] + pos[2]


def _copies(n, src_of, dst_refs, local_src_of, send_sems, recv_sems, local_sems):
    me = _mesh_pos()
    local = [pltpu.make_async_copy(local_src_of(a), dst_refs[a].at[_rank(me)], local_sems.at[a]) for a in range(n)]
    sends, arrivals = [], []
    for k in range(1, N_DEV):
        peer = _flip(me, k)
        for a in range(n):
            idx = a * (N_DEV - 1) + k - 1
            for into, group in ((me, sends), (peer, arrivals)):
                group.append(pltpu.make_async_remote_copy(
                    src_ref=src_of(a, peer), dst_ref=dst_refs[a].at[_rank(into)],
                    send_sem=send_sems.at[idx], recv_sem=recv_sems.at[idx],
                    device_id=peer, device_id_type=pl.DeviceIdType.MESH))
    return local, sends, arrivals


def _exchange_start(*args):
    local, sends, _ = _copies(*args)
    for cp in local + sends:
        cp.start()


def _exchange_wait(*args):
    local, sends, arrivals = _copies(*args)
    for cp in arrivals:
        cp.wait_recv()
    for cp in sends:
        cp.wait_send()
    for cp in local:
        cp.wait()


def _exchange(*args):
    _exchange_start(*args)
    _exchange_wait(*args)


def _sems(n):
    return [pltpu.SemaphoreType.DMA((n * (N_DEV - 1),)), pltpu.SemaphoreType.DMA((n * (N_DEV - 1),)),
            pltpu.SemaphoreType.DMA((n,))]


_ANY = pl.BlockSpec(memory_space=pl.ANY)


class _Ride:
    def __init__(self, kind, arrays):
        self.kind, self.arrays, self.n = kind, list(arrays), len(arrays)

    def out_shape(self):
        lead = (N_DEV,) if self.kind == "gather" else ()
        return [jax.ShapeDtypeStruct(lead + a.shape, a.dtype) for a in self.arrays]

    def _args(self, x_refs, out_refs, sems):
        if self.kind == "gather":
            return (self.n, lambda a, peer: x_refs[a], out_refs, lambda a: x_refs[a]) + tuple(sems)
        me = _mesh_pos()
        return (self.n, lambda a, peer: x_refs[a].at[_rank(peer)], out_refs, lambda a: x_refs[a].at[_rank(me)]) + tuple(sems)

    def start(self, x_refs, out_refs, sems):
        _exchange_start(*self._args(x_refs, out_refs, sems))

    def wait(self, x_refs, out_refs, sems):
        _exchange_wait(*self._args(x_refs, out_refs, sems))


def _pallas(body, name, grid, operands, in_specs, out_specs, out_shape, scratch=(), sem=None, ride=None, prefetch=()):
    n_pf, n_in, n_out, n_scr = len(prefetch), len(in_specs), len(out_specs), len(scratch)
    nr = ride.n if ride else 0

    def wrapped(*refs):
        refs = list(refs)
        pf, refs = refs[:n_pf], refs[n_pf:]
        ins, r_in = refs[:n_in], refs[n_in:n_in + nr]
        outs, r_out = refs[n_in + nr:n_in + nr + n_out], refs[n_in + nr + n_out:n_in + 2 * nr + n_out]
        rest = refs[n_in + 2 * nr + n_out:]
        scr, sems = rest[:n_scr], rest[n_scr:]
        if ride:
            first, last = None, None
            for d, size in enumerate(grid):
                i = pl.program_id(d)
                first = (i == 0) if first is None else jnp.logical_and(first, i == 0)
                last = (i == size - 1) if last is None else jnp.logical_and(last, i == size - 1)

            @pl.when(first)
            def _():
                ride.start(r_in, r_out, sems)

        body(pf + ins, outs, scr)
        if ride:
            @pl.when(last)
            def _():
                ride.wait(r_in, r_out, sems)

    grid_spec = pltpu.PrefetchScalarGridSpec(
        num_scalar_prefetch=n_pf, grid=grid,
        in_specs=list(in_specs) + [_ANY] * nr, out_specs=tuple(list(out_specs) + [_ANY] * nr),
        scratch_shapes=list(scratch) + (_sems(nr) if ride else []))
    sem = tuple(sem) if sem is not None and not ride else ("arbitrary",) * len(grid)
    return list(_pcall(
        wrapped, name=name, grid_spec=grid_spec,
        out_shape=tuple(list(out_shape) + (ride.out_shape() if ride else [])),
        compiler_params=_params(sem),
    )(*(list(prefetch) + list(operands) + (ride.arrays if ride else []))))


def _all_gather(shards, name):
    n = len(shards)

    def body(*refs):
        x_refs, out_refs = refs[:n], refs[n:2 * n]
        _exchange(n, lambda a, peer: x_refs[a], out_refs, lambda a: x_refs[a], *refs[2 * n:])

    return _pcall(
        body, name=name, in_specs=[_ANY] * n, out_specs=tuple([_ANY] * n),
        out_shape=tuple(jax.ShapeDtypeStruct((N_DEV,) + s.shape, s.dtype) for s in shards),
        scratch_shapes=_sems(n),
    )(*shards)


def _all_to_all(parts, name):
    n = len(parts)

    def body(*refs):
        x_refs, out_refs = refs[:n], refs[n:2 * n]
        me = _mesh_pos()
        _exchange(n, lambda a, peer: x_refs[a].at[_rank(peer)], out_refs, lambda a: x_refs[a].at[_rank(me)],
                  *refs[2 * n:])

    return _pcall(
        body, name=name, in_specs=[_ANY] * n, out_specs=tuple([_ANY] * n),
        out_shape=tuple(jax.ShapeDtypeStruct(p.shape, p.dtype) for p in parts),
        scratch_shapes=_sems(n),
    )(*parts)


def _all_reduce_small(v):
    R, C = v.shape

    def body(x_ref, out_ref, buf_ref, send_sems, recv_sems, local_sems):
        _exchange(1, lambda a, peer: x_ref, [buf_ref], lambda a: x_ref, send_sems, recv_sems, local_sems)
        total = buf_ref[0]
        for j in range(1, N_DEV):
            total = total + buf_ref[j]
        out_ref[...] = total

    vm = pl.BlockSpec(memory_space=pltpu.VMEM)
    return _pcall(
        body, name="all_reduce_small", in_specs=[vm], out_specs=vm,
        out_shape=jax.ShapeDtypeStruct((R, C), F32),
        scratch_shapes=[pltpu.VMEM((N_DEV, R, C), F32)] + _sems(1),
    )(v)


def _sum_slots(parts, name):
    shape = parts.shape[1:]
    C = shape[-1]
    R = parts.size // (N_DEV * C)
    tr = R
    while N_DEV * tr * C * 4 > (4 << 20) and tr % 32 == 0:
        tr //= 2

    def body(p_ref, o_ref):
        total = p_ref[0].astype(F32)
        for j in range(1, N_DEV):
            total = total + p_ref[j].astype(F32)
        o_ref[...] = total

    return _pcall(
        body, name=name, grid=(R // tr,),
        in_specs=[pl.BlockSpec((N_DEV, tr, C), lambda i: (0, i, 0))],
        out_specs=pl.BlockSpec((tr, C), lambda i: (i, 0)),
        out_shape=jax.ShapeDtypeStruct((R, C), F32),
        compiler_params=_params(("parallel",)),
    )(parts.reshape(N_DEV, R, C)).reshape(shape)


def _adamw(w, g, m, v, name):
    shape = w.shape
    C = shape[-1]
    R = w.size // C
    tr = R
    while tr * C * 4 > (1 << 20) and tr % 16 == 0:
        tr //= 2

    def body(w_ref, g_ref, m_ref, v_ref, d_ref, nm_ref, nv_ref):
        gv = g_ref[...]
        nm = ADAM_B1 * m_ref[...] + (1.0 - ADAM_B1) * gv
        nv = ADAM_B2 * v_ref[...] + (1.0 - ADAM_B2) * (gv * gv)
        m_hat = nm / (1.0 - ADAM_B1 ** ADAM_STEP)
        v_hat = nv / (1.0 - ADAM_B2 ** ADAM_STEP)
        d_ref[...] = -ADAM_LR * (m_hat / (jnp.sqrt(v_hat) + ADAM_EPS) + ADAM_WD * w_ref[...])
        nm_ref[...] = nm
        nv_ref[...] = nv

    blk = pl.BlockSpec((tr, C), lambda i: (i, 0))
    sds = jax.ShapeDtypeStruct((R, C), F32)
    outs = _pcall(
        body, name=name, grid=(R // tr,),
        in_specs=[blk] * 4, out_specs=(blk,) * 3, out_shape=(sds,) * 3,
        compiler_params=_params(("parallel",)),
    )(*(t.reshape(R, C) for t in (w, g, m, v)))
    return tuple(o.reshape(shape) for o in outs)


_REPLICATED = ("kv_in_norm", "kv_latent_norm", "attn_norm", "q_latent_norm", "ffn_norm", "final_norm")
_WEIGHTS = ("pool_norm", "pool_w", "pool_scale", "kv_in_norm", "w_kv_a", "kv_latent_norm", "w_kv_b", "attn_norm",
            "w_q_a", "q_latent_norm", "w_q_b", "w_o", "ffn_norm", "w_gate", "w_up", "w_down", "final_norm")


def _ffn_fwd(x, norm_g, wg, wu, wd, tag, gather=None):
    S, D = x.shape
    fs = wg.shape[1]
    tm = _tile(S, ROW_TILE)
    nmb = S // tm
    h = _rmsnorm(x, norm_g, "ffn_norm" + tag)
    g, u, a, *gathered = _ffn_up(h, wg, wu, "ffn_up" + tag, ride=gather)
    if gather is not None:
        wd = gathered[0].reshape(N_DEV * fs, D)
    y = _mm("ffn_down" + tag, "nn", a, wd, (tm, D, fs), (nmb, 1, N_DEV),
            lambda i, q, j: (j * nmb + i, 0), lambda i, q, j: (j, 0), lambda i, q, j: (i, 0), (S, D), F32, res=x)
    return y, (h, g, u, a), wd, gathered


def _ffn_bwd(x, norm_g, wg, wu, wd, saved, dy, dyb, tag, with_bf16, scatter=None):
    S, D = x.shape
    h, g, u, a = saved
    fs = g.shape[1]
    tm = _tile(S, ROW_TILE)
    nmb = S // tm
    dg, du, *got_rider = _ffn_dact(dyb, wd, g, u, "ffn_dact" + tag, ride=scatter)
    dwd = _mm("ffn_dwd" + tag, "tn", a, dyb, (fs, D, tm), (N_DEV, 1, nmb),
              lambda j, q, k: (j * nmb + k, 0), lambda j, q, k: (k, 0), lambda j, q, k: (j, 0), (N_DEV * fs, D), BF16)
    dwg, (got_dwd,) = _slots_dw("ffn_dwg" + tag, h, dg, BF16, ride=_Ride("scatter", [dwd.reshape(N_DEV, fs, D)]))
    dwu, (got_dwg,) = _slots_dw("ffn_dwu" + tag, h, du, BF16, ride=_Ride("scatter", [dwg.reshape(N_DEV, D, fs)]))
    dh, (got_dwu,) = _mm("ffn_dh" + tag, "nt", dg, wg, (tm, D, fs), (nmb, 1, N_DEV),
                         lambda i, q, j: (j * nmb + i, 0), lambda i, q, j: (j, 0), lambda i, q, j: (i, 0),
                         (S, D), F32, a2=du, b2=wu, ride=_Ride("scatter", [dwu.reshape(N_DEV, D, fs)]))
    outs = _rmsnorm_bwd(x, norm_g, dh, dy, "ffn_norm_bwd" + tag, with_bf16=with_bf16)
    return outs, got_rider, (got_dwg, got_dwu, got_dwd)


def kernel(x, positions, pool_norm, pool_w, pool_scale, kv_in_norm, w_kv_a, kv_latent_norm, w_kv_b, attn_norm, w_q_a, q_latent_norm, w_q_b, w_o, ffn_norm, w_gate, w_up, w_down, final_norm, loss_target, m_pool_norm, m_pool_w, m_pool_scale, m_kv_in_norm, m_w_kv_a, m_kv_latent_norm, m_w_kv_b, m_attn_norm, m_w_q_a, m_q_latent_norm, m_w_q_b, m_w_o, m_ffn_norm, m_w_gate, m_w_up, m_w_down, m_final_norm, v_pool_norm, v_pool_w, v_pool_scale, v_kv_in_norm, v_w_kv_a, v_kv_latent_norm, v_w_kv_b, v_attn_norm, v_w_q_a, v_q_latent_norm, v_w_q_b, v_w_o, v_ffn_norm, v_w_gate, v_w_up, v_w_down, v_final_norm):
    env = dict(locals())
    weights = {n: env[n] for n in _WEIGHTS}
    m_in = {n: env["m_" + n] for n in _WEIGHTS}
    v_in = {n: env["v_" + n] for n in _WEIGHTS}

    S, D = x.shape[1], x.shape[2]
    xs = x.reshape(S, D)
    target = loss_target.reshape(S, D)
    G, gws, gw = pool_w.shape[1:]
    Ds = w_kv_a.shape[0]
    R_kv, kvw = w_kv_b.shape
    hpd = kvw // (QK_NOPE_DIM + V_DIM)
    H = hpd * N_DEV
    R_q = w_q_a.shape[2]
    fs = w_gate.shape[2]
    bf = lambda t: t.astype(BF16)

    gains = jnp.concatenate([pool_norm, pool_scale], axis=0)
    g_pool_w, g_gains, g_wg0, g_wu0 = _all_gather([bf(pool_w[0]), gains, bf(w_gate[0]), bf(w_up[0])], "gather_first")
    wqb_pad = jnp.pad(w_q_b.reshape(R_q, hpd, QK_DIM), ((0, 0), (0, 0), (0, HEAD_PAD - QK_DIM))).reshape(R_q, hpd * HEAD_PAD)
    gather_mla = _Ride("gather", [bf(w_down[0]), bf(jnp.pad(w_kv_a, ((0, 0), (0, 128 - QK_ROPE_DIM)))), bf(w_kv_b),
                                  bf(w_q_a[0]), bf(wqb_pad), bf(w_o[0])])
    gather_ffn1 = _Ride("gather", [bf(w_gate[1]), bf(w_up[1]), bf(w_down[1])])

    wp = jnp.swapaxes(g_pool_w, 0, 1).reshape(G, gw, gw)
    gains_full = jnp.swapaxes(g_gains, 0, 1).reshape(2, D)
    g_pool_norm, g_pool_scale = gains_full[0:1], gains_full[1:2]
    wg0, wu0 = g_wg0.reshape(N_DEV * D, fs), g_wu0.reshape(N_DEV * D, fs)

    g_kv_in = kv_in_norm.reshape(1, D)
    g_kv_lat = kv_latent_norm.reshape(1, R_kv)
    g_attn = attn_norm.reshape(1, D)
    g_q_lat = q_latent_norm.reshape(1, R_q)
    g_final = final_norm.reshape(1, D)

    half = QK_ROPE_DIM // 2
    inv_freq = ROPE_BASE ** (-jnp.arange(half, dtype=F32) / half)
    ang = positions.reshape(S).astype(F32)[:, None] * inv_freq
    cos, sin = jnp.cos(ang), jnp.sin(ang)
    zeros = jnp.zeros((S, 128 - QK_ROPE_DIM), F32)
    cos_t = jnp.concatenate([cos, cos, zeros], axis=1)
    sin_t = jnp.concatenate([-sin, sin, zeros], axis=1)

    diff = _pool_pre(xs, g_pool_norm)
    x1 = _pool_mix(diff, wp, g_pool_scale, xs)
    x2, ffn0, wd0, (_, g_wkva, g_wkvb, g_wqa, g_wqb, g_wo) = _ffn_fwd(x1, ffn_norm[0:1], wg0, wu0, None, "0", gather_mla)
    wkva = g_wkva.reshape(D, R_kv + 128)
    wkvb = g_wkvb.reshape(N_DEV * R_kv, kvw)
    wqa = g_wqa.reshape(D, R_q)
    wqb = g_wqb.reshape(N_DEV * R_q, hpd * HEAD_PAD)
    wo = g_wo.reshape(H * V_DIM, D)

    hk = _rmsnorm(x2, g_kv_in, "kv_in_norm")
    kv = _mm_plain("kv_a", "nn", hk, wkva)
    ckv, kpe = _kv_mid(kv, g_kv_lat, cos_t, sin_t)
    kvu = _small_slots_out("kv_b", ckv, wkvb, BF16)

    ha = _rmsnorm(x2, g_attn, "attn_norm")
    ql = _mm_plain("q_a", "nn", ha, wqa)
    qn = _rmsnorm(ql, g_q_lat, "q_latent_norm")
    tab = pl.BlockSpec((_tile(S, ROW_TILE), 128), lambda i: (i, 0))
    qr = _small_slots_out("q_b", qn, wqb, BF16, epilogue=(_rope_heads, [cos_t, sin_t], [tab, tab]))
    o, lse, g_wg1, g_wu1, g_wd1 = _attn_fwd(qr, kvu, kpe, H, ride=gather_ffn1)
    wg1, wu1, wd1 = g_wg1.reshape(N_DEV * D, fs), g_wu1.reshape(N_DEV * D, fs), g_wd1.reshape(N_DEV * fs, D)
    x3 = _mm_plain("attn_out", "nn", o, wo, res=x2)
    x4, ffn1, _, _ = _ffn_fwd(x3, ffn_norm[1:2], wg1, wu1, wd1, "1")

    loss_part, dx4, dx4b, d_final = _loss_head(x4, g_final, target)

    (dx3, dx3b, d_ffn1), _, got_ffn1 = _ffn_bwd(x3, ffn_norm[1:2], wg1, wu1, wd1, ffn1, dx4, dx4b, "1", True)

    do = _mm_plain("attn_out_dx", "nt", dx3b, wo, out_dtype=BF16)
    dwo = _mm_plain("attn_out_dw", "tn", o, dx3b, out_dtype=BF16)
    dq, dkvu, dkpe = _attn_bwd(qr, kvu, kpe, do, o, lse.reshape(H, 1, S), cos_t, sin_t, H)

    dqn = _small_slots_in_nt("q_b_dx", dq, wqb, S)
    dwqb = _small_slots_dw("q_b_dw", qn, dq, BF16)
    dql, d_q_lat = _rmsnorm_bwd(ql, g_q_lat, dqn, None, "q_latent_norm_bwd", out_dtype=BF16)
    dha = _mm_plain("q_a_dx", "nt", dql, wqa)
    dwqa = _mm_plain("q_a_dw", "tn", ha, dql, out_dtype=BF16)
    dx2, d_attn = _rmsnorm_bwd(x2, g_attn, dha, dx3, "attn_norm_bwd")

    dckv = _small_slots_in_nt("kv_b_dx", dkvu, wkvb, S)
    dwkvb = _small_slots_dw("kv_b_dw", ckv, dkvu, BF16)
    dkv, d_kv_lat = _kv_mid_bwd(kv, g_kv_lat, dckv, dkpe, cos_t, sin_t)
    dhk = _mm_plain("kv_a_dx", "nt", dkv, wkva)
    dwkva = _mm_plain("kv_a_dw", "tn", hk, dkv, out_dtype=BF16)
    dx2, dx2b, d_kv_in = _rmsnorm_bwd(x2, g_kv_in, dhk, dx2, "kv_in_norm_bwd", with_bf16=True)

    scatter_mla = _Ride("scatter", [dwkva.reshape(N_DEV, Ds, R_kv + 128), dwkvb.reshape(N_DEV, R_kv, kvw),
                                    dwqa.reshape(N_DEV, Ds, R_q), dwqb.reshape(N_DEV, R_q, hpd * HEAD_PAD),
                                    dwo.reshape(N_DEV, H * V_DIM // N_DEV, D)])
    (dx1, d_ffn0), got_mla, got_ffn0 = _ffn_bwd(x1, ffn_norm[0:1], wg0, wu0, wd0, ffn0, dx2, dx2b, "0", False,
                                                  scatter=scatter_mla)

    ddiff, dwp, d_pool_scale = _pool_mix_bwd(diff, wp, g_pool_scale, dx1)
    grad_x, d_pool_norm = _pool_pre_bwd(xs, g_pool_norm, ddiff, dx1)

    d_gains = jnp.swapaxes(jnp.concatenate([d_pool_norm, d_pool_scale], axis=0).reshape(2, N_DEV, D // N_DEV), 0, 1)
    got_pool_w, got_gains = _all_to_all([jnp.swapaxes(dwp.reshape(G, N_DEV, gws, gw), 0, 1), d_gains], "exchange_pool")

    received = dict(zip(("w_kv_a", "w_kv_b", "w_q_a", "w_q_b", "w_o"), got_mla))
    received.update(zip(("w_gate0", "w_up0", "w_down0"), got_ffn0))
    received.update(zip(("w_gate1", "w_up1", "w_down1"), got_ffn1))
    received.update(pool_w=got_pool_w, gains=got_gains)
    sums = {n: _sum_slots(r, "sum_" + n) for n, r in received.items()}
    grads = {
        "pool_w": sums["pool_w"][None],
        "w_kv_a": sums["w_kv_a"][:, :R_kv + QK_ROPE_DIM],
        "w_kv_b": sums["w_kv_b"],
        "w_q_a": sums["w_q_a"][None],
        "w_q_b": sums["w_q_b"].reshape(R_q, hpd, HEAD_PAD)[:, :, :QK_DIM].reshape(1, R_q, hpd * QK_DIM),
        "w_o": sums["w_o"][None],
        "w_gate": jnp.stack([sums["w_gate0"], sums["w_gate1"]]),
        "w_up": jnp.stack([sums["w_up0"], sums["w_up1"]]),
        "w_down": jnp.stack([sums["w_down0"], sums["w_down1"]]),
        "pool_norm": sums["gains"][0:1],
        "pool_scale": sums["gains"][1:2],
    }

    small = {"kv_in_norm": d_kv_in, "kv_latent_norm": d_kv_lat, "attn_norm": d_attn, "q_latent_norm": d_q_lat,
             "ffn_norm": jnp.concatenate([d_ffn0, d_ffn1], axis=0), "final_norm": d_final}
    small_packed = jnp.concatenate([small[n].reshape(-1) for n in _REPLICATED] + [loss_part.reshape(-1)])
    pad = (-small_packed.shape[0]) % (8 * 128)
    reduced = _all_reduce_small(jnp.pad(small_packed, (0, pad)).reshape(-1, 128)).reshape(-1)
    off = 0
    for n in _REPLICATED:
        grads[n] = reduced[off:off + weights[n].size].reshape(weights[n].shape)
        off += weights[n].size
    loss = reduced[off]

    delta_w, new_m, new_v = {}, {}, {}
    for n in _WEIGHTS:
        delta_w[n], new_m[n], new_v[n] = _adamw(weights[n], grads[n], m_in[n], v_in[n], "adamw_" + n)

    return (loss, grad_x.reshape(x.shape), *[grads[n] for n in _WEIGHTS], *[delta_w[n] for n in _WEIGHTS],
            *[new_m[n] for n in _WEIGHTS], *[new_v[n] for n in _WEIGHTS])
```

```python
import math

import jax
import jax.numpy as jnp
import numpy as np
from jax import lax
from jax.experimental import pallas as pl
from jax.experimental.pallas import tpu as pltpu

F32 = jnp.float32
BF16 = jnp.bfloat16

N_DEV = 8
POOL_WINDOWS = (2, 4, 8, 16)
POOL_HALO = 16
QK_NOPE_DIM = 128
QK_ROPE_DIM = 64
QK_DIM = QK_NOPE_DIM + QK_ROPE_DIM
HEAD_PAD = 256
V_DIM = 128
ROPE_BASE = 10000.0
ATTN_SCALE = 1.0 / math.sqrt(QK_DIM)
EXP2_SCALE = ATTN_SCALE * math.log2(math.e)
NORM_EPS = 1e-6
ADAM_LR = 0.001
ADAM_B1 = 0.9
ADAM_B2 = 0.999
ADAM_EPS = 1e-08
ADAM_WD = 0.01
ADAM_STEP = 10

ROW_TILE = 512
ATT_BLOCK = 1024
ATT_ROW_CHUNK = 256
FFN_SLOTS = 2
MM_TN, MM_TK = 512, 512
VMEM_LIMIT = 48 * 1024 * 1024

_pcall = pl.pallas_call


def _params(sem):
    return pltpu.CompilerParams(dimension_semantics=sem, vmem_limit_bytes=VMEM_LIMIT)


def _tile(dim, default):
    if dim % default == 0:
        return default
    assert dim <= 2 * default, (dim, default)
    return dim


def _row_tile(rows, width):
    ts = _tile(rows, ROW_TILE)
    while ts * width * 4 > (2 << 20) and ts % 16 == 0:
        ts //= 2
    return ts


def _rms_scale(x):
    return lax.rsqrt(jnp.mean(x * x, axis=-1, keepdims=True) + NORM_EPS)


def _rms_bwd(x, g, dy):
    r = _rms_scale(x)
    xn = x * r
    u = dy * g
    dx = r * (u - xn * jnp.mean(u * xn, axis=-1, keepdims=True))
    return dx, dy * xn


def _swap_halves(x):
    lane = lax.broadcasted_iota(jnp.int32, x.shape, 1)
    return jnp.where(lane < QK_ROPE_DIM // 2, pltpu.roll(x, 128 - QK_ROPE_DIM // 2, 1), pltpu.roll(x, QK_ROPE_DIM // 2, 1))


def _rope(x, cos_t, sin_t):
    return x * cos_t + _swap_halves(x) * sin_t


def _rmsnorm(x, g, name, transposed_too=False):
    S, D = x.shape
    ts = _row_tile(S, D)

    def body(x_ref, g_ref, o_ref, *t_ref):
        xf = x_ref[...]
        y = xf * _rms_scale(xf) * g_ref[...]
        o_ref[...] = y.astype(o_ref.dtype)
        if transposed_too:
            t_ref[0][...] = y.T.astype(o_ref.dtype)

    row = pl.BlockSpec((ts, D), lambda i: (i, 0))
    outs = _pcall(
        body, name=name, grid=(S // ts,),
        in_specs=[row, pl.BlockSpec((1, D), lambda i: (0, 0))],
        out_specs=(row, pl.BlockSpec((D, ts), lambda i: (0, i))) if transposed_too else row,
        out_shape=((jax.ShapeDtypeStruct((S, D), BF16), jax.ShapeDtypeStruct((D, S), BF16)) if transposed_too
                   else jax.ShapeDtypeStruct((S, D), BF16)),
        compiler_params=_params(("parallel",)),
    )(x, g)
    return outs


def _rmsnorm_bwd(x, g, dy, res, name, out_dtype=F32, with_bf16=False):
    S, D = x.shape
    ts = _row_tile(S, D)
    has_res = res is not None

    def body(*refs):
        refs = list(refs)
        x_ref, g_ref, dy_ref = refs[:3]
        res_ref = refs[3] if has_res else None
        outs = refs[3 + has_res:]
        dx_ref, dg_ref = outs[0], outs[-1]
        dx, dgt = _rms_bwd(x_ref[...], g_ref[...], dy_ref[...].astype(F32))
        if has_res:
            dx = res_ref[...] + dx
        dx_ref[...] = dx.astype(dx_ref.dtype)
        if with_bf16:
            outs[1][...] = dx.astype(BF16)

        @pl.when(pl.program_id(0) == 0)
        def _():
            dg_ref[...] = jnp.zeros_like(dg_ref)

        dg_ref[...] += jnp.sum(dgt, axis=0, keepdims=True)

    row = pl.BlockSpec((ts, D), lambda i: (i, 0))
    vec = pl.BlockSpec((1, D), lambda i: (0, 0))
    out_specs = [row] + ([row] if with_bf16 else []) + [vec]
    out_shape = ([jax.ShapeDtypeStruct((S, D), out_dtype)] + ([jax.ShapeDtypeStruct((S, D), BF16)] if with_bf16 else [])
                 + [jax.ShapeDtypeStruct((1, D), F32)])
    return _pcall(
        body, name=name, grid=(S // ts,),
        in_specs=[row, vec, row] + ([row] if has_res else []),
        out_specs=tuple(out_specs), out_shape=tuple(out_shape),
        compiler_params=_params(("arbitrary",)),
    )(*([x, g, dy] + ([res] if has_res else [])))


def _pool_pre(x, g):
    S, D = x.shape
    ts = _row_tile(S, D)
    gw = D // len(POOL_WINDOWS)
    hb = ts // POOL_HALO

    def body(x_ref, halo_ref, g_ref, o_ref):
        i = pl.program_id(0)
        xx = jnp.concatenate([halo_ref[...], x_ref[...]], axis=0)
        h = xx * _rms_scale(xx) * g_ref[...]
        t = i * ts - POOL_HALO + lax.broadcasted_iota(jnp.int32, (ts + POOL_HALO, 1), 0)
        h = jnp.where(t >= 0, h, 0.0)
        tf = t[POOL_HALO:].astype(F32)
        for gi, w in enumerate(POOL_WINDOWS):
            hg = h[:, gi * gw:(gi + 1) * gw]
            acc, span = hg, 1
            while span < w:
                acc = acc + pltpu.roll(acc, span, 0)
                span *= 2
            count = jnp.minimum(tf + 1.0, float(w))
            diff = acc[POOL_HALO:] / count - hg[POOL_HALO:]
            o_ref[:, gi * gw:(gi + 1) * gw] = diff.astype(o_ref.dtype)

    return _pcall(
        body, name="pool_pre", grid=(S // ts,),
        in_specs=[pl.BlockSpec((ts, D), lambda i: (i, 0)),
                  pl.BlockSpec((POOL_HALO, D), lambda i: (jnp.maximum(i * hb - 1, 0), 0)),
                  pl.BlockSpec((1, D), lambda i: (0, 0))],
        out_specs=pl.BlockSpec((ts, D), lambda i: (i, 0)),
        out_shape=jax.ShapeDtypeStruct((S, D), BF16),
        compiler_params=_params(("parallel",)),
    )(x, x, g)


def _pool_pre_bwd(x, g, dd, res):
    S, D = x.shape
    ts = _row_tile(S, D)
    gw = D // len(POOL_WINDOWS)
    hb = ts // POOL_HALO
    last_halo = S // POOL_HALO - 1

    def body(x_ref, g_ref, dd_ref, halo_ref, res_ref, dx_ref, dg_ref):
        i = pl.program_id(0)
        ee = jnp.concatenate([dd_ref[...], halo_ref[...]], axis=0)
        t = i * ts + lax.broadcasted_iota(jnp.int32, (ts + POOL_HALO, 1), 0)
        ee = jnp.where(t < S, ee, 0.0)
        tf = t.astype(F32)
        n = ts + POOL_HALO
        for gi, w in enumerate(POOL_WINDOWS):
            eg = ee[:, gi * gw:(gi + 1) * gw]
            acc, span = eg / jnp.minimum(tf + 1.0, float(w)), 1
            while span < w:
                acc = acc + pltpu.roll(acc, n - span, 0)
                span *= 2
            dx_ref[:, gi * gw:(gi + 1) * gw] = acc[:ts] - eg[:ts]
        dx, dgt = _rms_bwd(x_ref[...], g_ref[...], dx_ref[...])
        dx_ref[...] = res_ref[...] + dx

        @pl.when(i == 0)
        def _():
            dg_ref[...] = jnp.zeros_like(dg_ref)

        dg_ref[...] += jnp.sum(dgt, axis=0, keepdims=True)

    row = pl.BlockSpec((ts, D), lambda i: (i, 0))
    vec = pl.BlockSpec((1, D), lambda i: (0, 0))
    return _pcall(
        body, name="pool_pre_bwd", grid=(S // ts,),
        in_specs=[row, vec, row,
                  pl.BlockSpec((POOL_HALO, D), lambda i: (jnp.minimum((i + 1) * hb, last_halo), 0)), row],
        out_specs=(row, vec),
        out_shape=(jax.ShapeDtypeStruct((S, D), F32), jax.ShapeDtypeStruct((1, D), F32)),
        compiler_params=_params(("arbitrary",)),
    )(x, g, dd, dd, res)


def _pool_mix(diff, w, scale, x):
    S, D = x.shape
    G, gw, _ = w.shape
    ts = _tile(S, ROW_TILE)

    def body(d_ref, w_ref, s_ref, x_ref, o_ref):
        mo = jnp.dot(d_ref[...], w_ref[0], preferred_element_type=F32)
        o_ref[...] = x_ref[...] + mo * s_ref[...]

    blk = pl.BlockSpec((ts, gw), lambda i, g: (i, g))
    return _pcall(
        body, name="pool_mix", grid=(S // ts, G),
        in_specs=[blk, pl.BlockSpec((1, gw, gw), lambda i, g: (g, 0, 0)), pl.BlockSpec((1, gw), lambda i, g: (0, g)), blk],
        out_specs=blk,
        out_shape=jax.ShapeDtypeStruct((S, D), F32),
        compiler_params=_params(("parallel", "parallel")),
    )(diff, w, scale, x)


def _pool_mix_bwd(diff, w, scale, dy):
    S, D = dy.shape
    G, gw, _ = w.shape
    ts = _tile(S, ROW_TILE)

    def body(d_ref, w_ref, s_ref, dy_ref, dd_ref, dw_ref, ds_ref):
        i = pl.program_id(1)
        d = d_ref[...]
        dy = dy_ref[...]
        mo = jnp.dot(d, w_ref[0], preferred_element_type=F32)
        dmo = (dy * s_ref[...]).astype(BF16)
        dd_ref[...] = lax.dot_general(dmo, w_ref[0], (((1,), (1,)), ((), ())), preferred_element_type=F32)

        @pl.when(i == 0)
        def _():
            dw_ref[...] = jnp.zeros_like(dw_ref)
            ds_ref[...] = jnp.zeros_like(ds_ref)

        dw_ref[0] += lax.dot_general(d, dmo, (((0,), (0,)), ((), ())), preferred_element_type=F32)
        ds_ref[...] += jnp.sum(dy * mo, axis=0, keepdims=True)

    blk = pl.BlockSpec((ts, gw), lambda g, i: (i, g))
    wspec = pl.BlockSpec((1, gw, gw), lambda g, i: (g, 0, 0))
    vec = pl.BlockSpec((1, gw), lambda g, i: (0, g))
    return _pcall(
        body, name="pool_mix_bwd", grid=(G, S // ts),
        in_specs=[blk, wspec, vec, blk],
        out_specs=(blk, wspec, vec),
        out_shape=(jax.ShapeDtypeStruct((S, D), F32), jax.ShapeDtypeStruct((G, gw, gw), F32),
                   jax.ShapeDtypeStruct((1, D), F32)),
        compiler_params=_params(("parallel", "arbitrary")),
    )(diff, w, scale, dy)


def _loss_head(x, g, target):
    S, D = x.shape
    ts = _row_tile(S, D)

    def body(x_ref, g_ref, t_ref, loss_ref, dx_ref, dxb_ref, dg_ref):
        xf = x_ref[...]
        gv = g_ref[...]
        err = xf * _rms_scale(xf) * gv - t_ref[...]
        dx, dgt = _rms_bwd(xf, gv, err * (1.0 / D))
        dx_ref[...] = dx
        dxb_ref[...] = dx.astype(BF16)

        @pl.when(pl.program_id(0) == 0)
        def _():
            loss_ref[...] = jnp.zeros_like(loss_ref)
            dg_ref[...] = jnp.zeros_like(dg_ref)

        part = 0.5 * jnp.sum(jnp.sum(err * err, axis=-1, keepdims=True) * (1.0 / D), axis=0, keepdims=True)
        loss_ref[...] += jnp.broadcast_to(part, loss_ref.shape)
        dg_ref[...] += jnp.sum(dgt, axis=0, keepdims=True)

    row = pl.BlockSpec((ts, D), lambda i: (i, 0))
    vec = pl.BlockSpec((1, D), lambda i: (0, 0))
    return _pcall(
        body, name="loss_head", grid=(S // ts,),
        in_specs=[row, vec, row],
        out_specs=(pl.BlockSpec((1, 128), lambda i: (0, 0)), row, row, vec),
        out_shape=(jax.ShapeDtypeStruct((1, 128), F32), jax.ShapeDtypeStruct((S, D), F32),
                   jax.ShapeDtypeStruct((S, D), BF16), jax.ShapeDtypeStruct((1, D), F32)),
        compiler_params=_params(("arbitrary",)),
    )(x, g, target)


_DIMS = {"nn": (((1,), (0,)), ((), ())), "nt": (((1,), (1,)), ((), ())), "tn": (((0,), (0,)), ((), ()))}


def _mm(name, mode, a, b, tiles, grid, a_map, b_map, o_map, out_shape, out_dtype=F32, res=None, a2=None, b2=None,
        ride=None, epilogue=None):
    tm, tn, tk = tiles
    nk = grid[-1]
    dual, has_res = a2 is not None, res is not None
    n_main = 2 + 2 * dual
    dn = _DIMS[mode]

    def body(ins, outs, scratch):
        o_ref = outs[0]

        def product():
            part = lax.dot_general(ins[0][...].astype(BF16), ins[1][...].astype(BF16), dn, preferred_element_type=F32)
            if dual:
                part += lax.dot_general(ins[2][...].astype(BF16), ins[3][...].astype(BF16), dn,
                                        preferred_element_type=F32)
            return part

        def finish(out):
            if has_res:
                out = ins[n_main][...] + out
            if epilogue is not None:
                out = epilogue[0](out, *[r[...] for r in ins[n_main + has_res:]])
            o_ref[...] = out.astype(o_ref.dtype)

        if nk == 1:
            finish(product())
            return
        acc_ref = scratch[0]
        k = pl.program_id(2)

        @pl.when(k == 0)
        def _():
            acc_ref[...] = jnp.zeros_like(acc_ref)

        acc_ref[...] += product()

        @pl.when(k == nk - 1)
        def _():
            finish(acc_ref[...])

    a_spec = pl.BlockSpec((tk, tm) if mode == "tn" else (tm, tk), a_map)
    b_spec = pl.BlockSpec((tn, tk) if mode == "nt" else (tk, tn), b_map)
    o_spec = pl.BlockSpec((tm, tn), o_map)
    operands, in_specs = [a, b], [a_spec, b_spec]
    if dual:
        operands += [a2, b2]
        in_specs += [a_spec, b_spec]
    if has_res:
        operands.append(res)
        in_specs.append(o_spec)
    if epilogue is not None:
        operands += list(epilogue[1])
        in_specs += list(epilogue[2])
    outs = _pallas(body, name, grid, operands, in_specs, [o_spec], [jax.ShapeDtypeStruct(out_shape, out_dtype)],
                   scratch=[pltpu.VMEM((tm, tn), F32)] if nk > 1 else [],
                   sem=("parallel", "parallel", "arbitrary"), ride=ride)
    return (outs[0], outs[1:]) if ride else outs[0]


def _mm_plain(name, mode, a, b, out_dtype=F32, res=None):
    if mode == "tn":
        (K, M), N = a.shape, b.shape[1]
    elif mode == "nt":
        (M, K), N = a.shape, b.shape[0]
    else:
        (M, K), N = a.shape, b.shape[1]
    if mode == "tn":
        tm, tn, tk = _tile(M, 2 * MM_TN), _tile(N, 2 * MM_TN), _tile(K, ROW_TILE)
    else:
        tm = _tile(M, ROW_TILE)
        tk = K if K <= 4 * MM_TK else _tile(K, MM_TK)
        tn = N if N * tk * 2 <= (8 << 20) else _tile(N, MM_TN)
    a_map = (lambda i, j, k: (k, i)) if mode == "tn" else (lambda i, j, k: (i, k))
    b_map = (lambda i, j, k: (j, k)) if mode == "nt" else (lambda i, j, k: (k, j))
    return _mm(name, mode, a, b, (tm, tn, tk), (M // tm, N // tn, K // tk), a_map, b_map, lambda i, j, k: (i, j),
               (M, N), out_dtype, res)


def _slots_out(name, a, w_slots, out_dtype, epilogue=None):
    S, K = a.shape
    n = w_slots.shape[1]
    tm = _tile(S, ROW_TILE)
    nmb = S // tm
    return _mm(name, "nn", a, w_slots, (tm, n, K), (nmb, N_DEV, 1),
               lambda i, j, k: (i, 0), lambda i, j, k: (j, 0), lambda i, j, k: (j * nmb + i, 0),
               (N_DEV * S, n), out_dtype, epilogue=epilogue)


def _slots_in_nt(name, a_slots, w_slots, S, a2=None, w2=None, ride=None):
    n = a_slots.shape[1]
    K = w_slots.shape[0] // N_DEV
    tm = _tile(S, ROW_TILE)
    nmb = S // tm
    return _mm(name, "nt", a_slots, w_slots, (tm, K, n), (nmb, 1, N_DEV),
               lambda i, q, j: (j * nmb + i, 0), lambda i, q, j: (j, 0), lambda i, q, j: (i, 0),
               (S, K), F32, a2=a2, b2=w2, ride=ride)


def _slots_dw(name, a, d_slots, out_dtype=F32, ride=None):
    S, K = a.shape
    n = d_slots.shape[1]
    tk = _tile(S, ROW_TILE)
    nkb = S // tk
    return _mm(name, "tn", a, d_slots, (K, n, tk), (N_DEV, 1, nkb),
               lambda j, q, k: (k, 0), lambda j, q, k: (j * nkb + k, 0), lambda j, q, k: (j, 0),
               (N_DEV * K, n), out_dtype, ride=ride)


def _small_slots_out(name, a, w_slots, out_dtype, epilogue=None):
    S, K = a.shape
    n = w_slots.shape[1]
    tm = _tile(S, ROW_TILE)
    extra = epilogue[1] if epilogue else []

    def body(*refs):
        a_ref, w_ref, o_ref = refs[0], refs[1], refs[-1]
        av = a_ref[...].astype(BF16)
        tables = [r[...] for r in refs[2:-1]]
        for j in range(N_DEV):
            out = jnp.dot(av, w_ref[j], preferred_element_type=F32)
            if epilogue:
                out = epilogue[0](out, *tables)
            o_ref[j] = out.astype(o_ref.dtype)

    return _pcall(
        body, name=name, grid=(S // tm,),
        in_specs=[pl.BlockSpec((tm, K), lambda i: (i, 0)), pl.BlockSpec((N_DEV, K, n), lambda i: (0, 0, 0))]
        + (list(epilogue[2]) if epilogue else []),
        out_specs=pl.BlockSpec((N_DEV, tm, n), lambda i: (0, i, 0)),
        out_shape=jax.ShapeDtypeStruct((N_DEV, S, n), out_dtype),
        compiler_params=_params(("parallel",)),
    )(a, w_slots.reshape(N_DEV, K, n), *extra).reshape(N_DEV * S, n)


def _small_slots_in_nt(name, a_slots, w_slots, S):
    n = a_slots.shape[1]
    K = w_slots.shape[0] // N_DEV
    tm = _tile(S, ROW_TILE)

    def body(a_ref, w_ref, o_ref):
        total = lax.dot_general(a_ref[0], w_ref[0], _DIMS["nt"], preferred_element_type=F32)
        for j in range(1, N_DEV):
            total += lax.dot_general(a_ref[j], w_ref[j], _DIMS["nt"], preferred_element_type=F32)
        o_ref[...] = total

    return _pcall(
        body, name=name, grid=(S // tm,),
        in_specs=[pl.BlockSpec((N_DEV, tm, n), lambda i: (0, i, 0)), pl.BlockSpec((N_DEV, K, n), lambda i: (0, 0, 0))],
        out_specs=pl.BlockSpec((tm, K), lambda i: (i, 0)),
        out_shape=jax.ShapeDtypeStruct((S, K), F32),
        compiler_params=_params(("parallel",)),
    )(a_slots.reshape(N_DEV, S, n), w_slots.reshape(N_DEV, K, n))


def _small_slots_dw(name, a, d_slots, out_dtype):
    S, K = a.shape
    n = d_slots.shape[1]
    tk = _tile(S, ROW_TILE)
    nkb = S // tk

    def body(a_ref, d_ref, o_ref, acc_ref):
        k = pl.program_id(0)
        at = a_ref[...].astype(F32).T.astype(BF16)

        def add(first):
            for j in range(N_DEV):
                part = jnp.dot(at, d_ref[j], preferred_element_type=F32)
                if first:
                    acc_ref[j] = part
                else:
                    acc_ref[j] += part

        @pl.when(k == 0)
        def _():
            add(True)

        @pl.when(k > 0)
        def _():
            add(False)

        @pl.when(k == nkb - 1)
        def _():
            o_ref[...] = acc_ref[...].astype(o_ref.dtype)

    return _pcall(
        body, name=name, grid=(nkb,),
        in_specs=[pl.BlockSpec((tk, K), lambda k: (k, 0)), pl.BlockSpec((N_DEV, tk, n), lambda k: (0, k, 0))],
        out_specs=pl.BlockSpec((N_DEV, K, n), lambda k: (0, 0, 0)),
        out_shape=jax.ShapeDtypeStruct((N_DEV, K, n), out_dtype),
        scratch_shapes=[pltpu.VMEM((N_DEV, K, n), F32)],
        compiler_params=_params(("arbitrary",)),
    )(a, d_slots.reshape(N_DEV, S, n)).reshape(N_DEV * K, n)


def _sigmoid(x):
    return 1.0 / (1.0 + jnp.exp(-x))


def _ffn_up(h, wg, wu, name, ride=None):
    S, D = h.shape
    fs = wg.shape[1]
    tm = _tile(S, ROW_TILE)
    sp = FFN_SLOTS

    def body(ins, outs, scratch):
        h_ref, wg_ref, wu_ref = ins
        g_ref, u_ref, a_ref = outs
        hv = h_ref[...]
        for s in range(sp):
            g = jnp.dot(hv, wg_ref[s], preferred_element_type=F32)
            u = jnp.dot(hv, wu_ref[s], preferred_element_type=F32)
            g_ref[s] = g.astype(g_ref.dtype)
            u_ref[s] = u.astype(u_ref.dtype)
            a_ref[s] = (g * _sigmoid(g) * u).astype(a_ref.dtype)

    w_spec = pl.BlockSpec((sp, D, fs), lambda i, j: (j, 0, 0))
    o_spec = pl.BlockSpec((sp, tm, fs), lambda i, j: (j, i, 0))
    sds = jax.ShapeDtypeStruct((N_DEV, S, fs), BF16)
    g, u, a, *rest = _pallas(body, name, (S // tm, N_DEV // sp), [h, wg.reshape(N_DEV, D, fs), wu.reshape(N_DEV, D, fs)],
                             [pl.BlockSpec((tm, D), lambda i, j: (i, 0)), w_spec, w_spec], [o_spec, o_spec, o_spec],
                             [sds, sds, sds], sem=("parallel", "arbitrary"), ride=ride)
    return [t.reshape(N_DEV * S, fs) for t in (g, u, a)] + rest


def _ffn_dact(dy, wd, g, u, name, ride=None):
    S, D = dy.shape
    fs = g.shape[1]
    tm = _tile(S, ROW_TILE)
    sp = FFN_SLOTS

    def body(ins, outs, scratch):
        dy_ref, wd_ref, g_ref, u_ref = ins
        dg_ref, du_ref = outs
        dy = dy_ref[...]
        for s in range(sp):
            da = lax.dot_general(dy, wd_ref[s], _DIMS["nt"], preferred_element_type=F32)
            g, u = g_ref[s].astype(F32), u_ref[s].astype(F32)
            sig = _sigmoid(g)
            dg_ref[s] = (da * u * (sig * (1.0 + g * (1.0 - sig)))).astype(dg_ref.dtype)
            du_ref[s] = (da * (g * sig)).astype(du_ref.dtype)

    o_spec = pl.BlockSpec((sp, tm, fs), lambda i, j: (j, i, 0))
    sds = jax.ShapeDtypeStruct((N_DEV, S, fs), BF16)
    dg, du, *rest = _pallas(body, name, (S // tm, N_DEV // sp),
                            [dy, wd.reshape(N_DEV, fs, D), g.reshape(N_DEV, S, fs), u.reshape(N_DEV, S, fs)],
                            [pl.BlockSpec((tm, D), lambda i, j: (i, 0)), pl.BlockSpec((sp, fs, D), lambda i, j: (j, 0, 0)),
                             o_spec, o_spec], [o_spec, o_spec], [sds, sds], sem=("parallel", "arbitrary"), ride=ride)
    return [dg.reshape(N_DEV * S, fs), du.reshape(N_DEV * S, fs)] + rest


def _ffn_down(a, wd, x, name):
    S, D = x.shape
    fs = a.shape[1]
    tm = _tile(S, ROW_TILE)
    sp = FFN_SLOTS
    nj = N_DEV // sp

    def body(a_ref, w_ref, x_ref, o_ref, acc_ref):
        j = pl.program_id(1)
        part = jnp.dot(a_ref[0], w_ref[0], preferred_element_type=F32)
        for s in range(1, sp):
            part += jnp.dot(a_ref[s], w_ref[s], preferred_element_type=F32)

        @pl.when(j == 0)
        def _():
            acc_ref[...] = x_ref[...] + part

        @pl.when(j > 0)
        def _():
            acc_ref[...] += part

        @pl.when(j == nj - 1)
        def _():
            o_ref[...] = acc_ref[...]

    row = pl.BlockSpec((tm, D), lambda i, j: (i, 0))
    return _pcall(
        body, name=name, grid=(S // tm, nj),
        in_specs=[pl.BlockSpec((sp, tm, fs), lambda i, j: (j, i, 0)), pl.BlockSpec((sp, fs, D), lambda i, j: (j, 0, 0)), row],
        out_specs=row, out_shape=jax.ShapeDtypeStruct((S, D), F32),
        scratch_shapes=[pltpu.VMEM((tm, D), F32)],
        compiler_params=_params(("parallel", "arbitrary")),
    )(a.reshape(N_DEV, S, fs), wd.reshape(N_DEV, fs, D), x)


def _ffn_dh(dg, du, wg, wu, S, name, ride=None):
    fs = dg.shape[1]
    D = wg.shape[0] // N_DEV
    tm = _tile(S, ROW_TILE)
    sp = FFN_SLOTS
    nj = N_DEV // sp

    def body(ins, outs, scratch):
        dg_ref, du_ref, wg_ref, wu_ref = ins
        o_ref, acc_ref = outs[0], scratch[0]
        j = pl.program_id(1)
        part = None
        for s in range(sp):
            for d_ref, w_ref in ((dg_ref, wg_ref), (du_ref, wu_ref)):
                term = lax.dot_general(d_ref[s], w_ref[s], _DIMS["nt"], preferred_element_type=F32)
                part = term if part is None else part + term

        @pl.when(j == 0)
        def _():
            acc_ref[...] = part

        @pl.when(j > 0)
        def _():
            acc_ref[...] += part

        @pl.when(j == nj - 1)
        def _():
            o_ref[...] = acc_ref[...]

    d_spec = pl.BlockSpec((sp, tm, fs), lambda i, j: (j, i, 0))
    w_spec = pl.BlockSpec((sp, D, fs), lambda i, j: (j, 0, 0))
    return _pallas(body, name, (S // tm, nj),
                   [dg.reshape(N_DEV, S, fs), du.reshape(N_DEV, S, fs), wg.reshape(N_DEV, D, fs), wu.reshape(N_DEV, D, fs)],
                   [d_spec, d_spec, w_spec, w_spec], [pl.BlockSpec((tm, D), lambda i, j: (i, 0))],
                   [jax.ShapeDtypeStruct((S, D), F32)], scratch=[pltpu.VMEM((tm, D), F32)],
                   sem=("parallel", "arbitrary"), ride=ride)


def _slots_dw_t(name, at, d_slots, ride=None):
    K, S = at.shape
    n = d_slots.shape[1]
    tk = _tile(S, 2 * ROW_TILE)
    nkb = S // tk
    return _mm(name, "nn", at, d_slots, (K, n, tk), (N_DEV, 1, nkb),
               lambda j, q, k: (0, k), lambda j, q, k: (j * nkb + k, 0), lambda j, q, k: (j, 0),
               (N_DEV * K, n), BF16, ride=ride)


def _rope_heads(q, cos_t, sin_t):
    parts = []
    for c0 in range(0, q.shape[1], HEAD_PAD):
        parts += [q[:, c0:c0 + QK_NOPE_DIM], _rope(q[:, c0 + QK_NOPE_DIM:c0 + HEAD_PAD], cos_t, sin_t)]
    return jnp.concatenate(parts, axis=1)


def _kv_mid(kv, g, cos_t, sin_t):
    S, W = kv.shape
    R = W - 128
    ts = _tile(S, ROW_TILE)

    def body(kv_ref, g_ref, c_ref, s_ref, c_out, k_out):
        cf = kv_ref[:, :R]
        c_out[...] = (cf * _rms_scale(cf) * g_ref[...]).astype(c_out.dtype)
        k_out[...] = _rope(kv_ref[:, R:], c_ref[...], s_ref[...]).astype(k_out.dtype)

    tab = pl.BlockSpec((ts, 128), lambda i: (i, 0))
    return _pcall(
        body, name="kv_mid", grid=(S // ts,),
        in_specs=[pl.BlockSpec((ts, W), lambda i: (i, 0)), pl.BlockSpec((1, R), lambda i: (0, 0)), tab, tab],
        out_specs=(pl.BlockSpec((ts, R), lambda i: (i, 0)), tab),
        out_shape=(jax.ShapeDtypeStruct((S, R), BF16), jax.ShapeDtypeStruct((S, 128), BF16)),
        compiler_params=_params(("parallel",)),
    )(kv, g, cos_t, sin_t)


def _kv_mid_bwd(kv, g, dc, dkpe, cos_t, sin_t):
    S, W = kv.shape
    R = W - 128
    H = dkpe.shape[0]
    ts = _row_tile(S, H * 128)

    def body(kv_ref, g_ref, dc_ref, dk_ref, c_ref, s_ref, dkv_ref, dg_ref):
        dx, dgt = _rms_bwd(kv_ref[:, :R], g_ref[...], dc_ref[...])
        dkv_ref[:, :R] = dx.astype(dkv_ref.dtype)
        dk = dk_ref[0]
        for h in range(1, H):
            dk = dk + dk_ref[h]
        dkv_ref[:, R:] = _rope(dk, c_ref[...], -s_ref[...]).astype(dkv_ref.dtype)

        @pl.when(pl.program_id(0) == 0)
        def _():
            dg_ref[...] = jnp.zeros_like(dg_ref)

        dg_ref[...] += jnp.sum(dgt, axis=0, keepdims=True)

    tab = pl.BlockSpec((ts, 128), lambda i: (i, 0))
    vec = pl.BlockSpec((1, R), lambda i: (0, 0))
    return _pcall(
        body, name="kv_mid_bwd", grid=(S // ts,),
        in_specs=[pl.BlockSpec((ts, W), lambda i: (i, 0)), vec, pl.BlockSpec((ts, R), lambda i: (i, 0)),
                  pl.BlockSpec((H, ts, 128), lambda i: (0, i, 0)), tab, tab],
        out_specs=(pl.BlockSpec((ts, W), lambda i: (i, 0)), vec),
        out_shape=(jax.ShapeDtypeStruct((S, W), BF16), jax.ShapeDtypeStruct((1, R), F32)),
        compiler_params=_params(("arbitrary",)),
    )(kv, g, dc, dkpe, cos_t, sin_t)


def _block_pairs(nb, by_key):
    pairs = ([(qi, ki) for ki in range(nb) for qi in range(ki, nb)] if by_key
             else [(qi, ki) for qi in range(nb) for ki in range(qi + 1)])
    return jnp.asarray(np.array([p[0] for p in pairs], np.int32)), jnp.asarray(np.array([p[1] for p in pairs], np.int32))


def _causal_mask(blk, transposed=False, rows=None, row0=0):
    shape = (blk if rows is None else rows, blk)
    r = lax.broadcasted_iota(jnp.int32, shape, 0) + row0
    c = lax.broadcasted_iota(jnp.int32, shape, 1)
    return (r <= c) if transposed else (c <= r)


def _attn_specs(S, blk, hpd):
    nb = S // blk
    w = hpd * HEAD_PAD
    return dict(
        q=pl.BlockSpec((blk, w), lambda j, t, qt, kt: (j * nb + qt[t], 0)),
        kv=pl.BlockSpec((blk, w), lambda j, t, qt, kt: (j * nb + kt[t], 0)),
        kp=pl.BlockSpec((blk, 128), lambda j, t, qt, kt: (kt[t], 0)),
        do=pl.BlockSpec((blk, hpd * V_DIM), lambda j, t, qt, kt: (qt[t], j)),
        col=pl.BlockSpec((hpd, blk, 1), lambda j, t, qt, kt: (j, qt[t], 0)),
        row=pl.BlockSpec((hpd, 1, blk), lambda j, t, qt, kt: (j, 0, qt[t])),
        tab=pl.BlockSpec((blk, 128), lambda j, t, qt, kt: (qt[t], 0)))


def _head_k(kv_ref, kp, hh):
    return jnp.concatenate([kv_ref[:, hh * HEAD_PAD:hh * HEAD_PAD + QK_NOPE_DIM], kp], axis=1)


def _head_v(kv_ref, hh):
    return kv_ref[:, hh * HEAD_PAD + QK_NOPE_DIM:(hh + 1) * HEAD_PAD]


def _attn_fwd(q, kvu, kpe, H, ride=None):
    S = kpe.shape[0]
    hpd = H // N_DEV
    blk = _tile(S, ATT_BLOCK)
    q_tab, k_tab = _block_pairs(S // blk, by_key=False)
    rc = min(blk, ATT_ROW_CHUNK)

    def body(ins, outs, scratch):
        qt, kt, q_ref, kv_ref, kp_ref = ins
        o_ref, lse_ref = outs
        m_ref, acc_ref = scratch
        t = pl.program_id(1)
        qi, ki = qt[t], kt[t]

        @pl.when(ki == 0)
        def _():
            m_ref[...] = jnp.full_like(m_ref, -jnp.inf)
            acc_ref[...] = jnp.zeros_like(acc_ref)

        def step(masked):
            kp = kp_ref[...]
            ones = jnp.ones((blk, 128), BF16)
            for hh in range(hpd):
                k = _head_k(kv_ref, kp, hh)
                v_ext = jnp.concatenate([_head_v(kv_ref, hh), ones], axis=1)
                for r0 in range(0, blk, rc):
                    rows = slice(r0, r0 + rc)
                    s = lax.dot_general(q_ref[rows, hh * HEAD_PAD:(hh + 1) * HEAD_PAD], k, _DIMS["nt"],
                                        preferred_element_type=F32)
                    if masked:
                        s = jnp.where(_causal_mask(blk, rows=rc, row0=r0), s, -jnp.inf)
                    m_old = m_ref[hh, rows]
                    m_new = jnp.maximum(m_old, jnp.max(s, axis=-1, keepdims=True))
                    p = jnp.exp2((s - m_new) * EXP2_SCALE).astype(BF16)
                    acc_ref[hh, rows] = (jnp.exp2((m_old - m_new) * EXP2_SCALE) * acc_ref[hh, rows]
                                         + jnp.dot(p, v_ext, preferred_element_type=F32))
                    m_ref[hh, rows] = m_new

        @pl.when(ki < qi)
        def _():
            step(False)

        @pl.when(ki == qi)
        def _():
            step(True)
            for hh in range(hpd):
                l = acc_ref[hh, :, V_DIM:]
                o_ref[:, hh * V_DIM:(hh + 1) * V_DIM] = (acc_ref[hh, :, :V_DIM] / l).astype(o_ref.dtype)
                lse_ref[hh] = m_ref[hh] * EXP2_SCALE + jnp.log2(l[:, :1])

    sp = _attn_specs(S, blk, hpd)
    return _pallas(body, "attn_fwd", (N_DEV, q_tab.shape[0]), [q, kvu, kpe], [sp["q"], sp["kv"], sp["kp"]],
                   [sp["do"], sp["col"]],
                   [jax.ShapeDtypeStruct((S, H * V_DIM), BF16), jax.ShapeDtypeStruct((H, S, 1), F32)],
                   scratch=[pltpu.VMEM((hpd, blk, 1), F32), pltpu.VMEM((hpd, blk, 2 * V_DIM), F32)],
                   sem=("parallel", "arbitrary"), ride=ride, prefetch=[q_tab, k_tab])


def _attn_bwd(q, kvu, kpe, do, o, lse_row, cos_t, sin_t, H):
    S = kpe.shape[0]
    hpd = H // N_DEV
    blk = _tile(S, ATT_BLOCK)
    nb = S // blk
    q_tab, k_tab = _block_pairs(nb, by_key=True)

    def body(qt, kt, q_ref, kv_ref, kp_ref, do_ref, o_ref, lse_ref, c_ref, s_ref, dq_ref, dkv_ref, dkp_ref,
             dq_acc, dk_acc, dv_acc):
        t = pl.program_id(1)
        qi, ki = qt[t], kt[t]
        q_rows = pl.ds(pl.multiple_of(qi * blk, blk), blk)

        def step(diag):
            kp = kp_ref[...]
            ones = jnp.ones((8, V_DIM), BF16)
            for hh in range(hpd):
                k = _head_k(kv_ref, kp, hh)
                qv = q_ref[:, hh * HEAD_PAD:(hh + 1) * HEAD_PAD]
                cols = slice(hh * V_DIM, (hh + 1) * V_DIM)
                dov = do_ref[:, cols]
                doo = (dov.astype(F32) * o_ref[:, cols].astype(F32)).astype(BF16)
                delta = lax.dot_general(ones, doo, _DIMS["nt"], preferred_element_type=F32)[:1]
                st = lax.dot_general(k, qv, _DIMS["nt"], preferred_element_type=F32)
                pt = jnp.exp2(st * EXP2_SCALE - lse_ref[hh])
                if diag:
                    pt = jnp.where(_causal_mask(blk, transposed=True), pt, 0.0)
                dv_new = jnp.dot(pt.astype(BF16), dov, preferred_element_type=F32)
                dpt = lax.dot_general(_head_v(kv_ref, hh), dov, _DIMS["nt"], preferred_element_type=F32)
                dst = (pt * (dpt - delta)).astype(BF16)
                dk_new = jnp.dot(dst, qv, preferred_element_type=F32)
                dq_new = lax.dot_general(dst, k, _DIMS["tn"], preferred_element_type=F32)
                if diag:
                    dv_acc[hh] = dv_new
                    dk_acc[hh] = dk_new
                else:
                    dv_acc[hh] += dv_new
                    dk_acc[hh] += dk_new

                @pl.when(ki == 0)
                def _():
                    dq_acc[hh, q_rows] = dq_new

                @pl.when(ki > 0)
                def _():
                    dq_acc[hh, q_rows] += dq_new

        @pl.when(qi == ki)
        def _():
            step(True)
            for hh in range(hpd):
                c0 = hh * HEAD_PAD
                dq = dq_acc[hh, q_rows] * ATTN_SCALE
                dq_ref[:, c0:c0 + QK_NOPE_DIM] = dq[:, :QK_NOPE_DIM].astype(dq_ref.dtype)
                dq_ref[:, c0 + QK_NOPE_DIM:c0 + HEAD_PAD] = _rope(dq[:, QK_NOPE_DIM:], c_ref[...],
                                                                  -s_ref[...]).astype(dq_ref.dtype)

        @pl.when(qi > ki)
        def _():
            step(False)

        @pl.when(qi == nb - 1)
        def _():
            for hh in range(hpd):
                c0 = hh * HEAD_PAD
                dkv_ref[:, c0:c0 + QK_NOPE_DIM] = (dk_acc[hh, :, :QK_NOPE_DIM] * ATTN_SCALE).astype(dkv_ref.dtype)
                dkv_ref[:, c0 + QK_NOPE_DIM:c0 + HEAD_PAD] = dv_acc[hh].astype(dkv_ref.dtype)
                dkp_ref[hh] = dk_acc[hh, :, QK_NOPE_DIM:] * ATTN_SCALE

    sp = _attn_specs(S, blk, hpd)
    key_tab = pl.BlockSpec((blk, 128), lambda j, t, qt, kt: (kt[t], 0))
    grid_spec = pltpu.PrefetchScalarGridSpec(
        num_scalar_prefetch=2, grid=(N_DEV, q_tab.shape[0]),
        in_specs=[sp["q"], sp["kv"], sp["kp"], sp["do"], sp["do"], sp["row"], key_tab, key_tab],
        out_specs=(sp["kv"], sp["kv"], pl.BlockSpec((hpd, blk, 128), lambda j, t, qt, kt: (j, kt[t], 0))),
        scratch_shapes=[pltpu.VMEM((hpd, S, HEAD_PAD), F32), pltpu.VMEM((hpd, blk, HEAD_PAD), F32),
                        pltpu.VMEM((hpd, blk, V_DIM), F32)])
    return _pcall(
        body, name="attn_bwd", grid_spec=grid_spec,
        out_shape=(jax.ShapeDtypeStruct(q.shape, BF16), jax.ShapeDtypeStruct(kvu.shape, BF16),
                   jax.ShapeDtypeStruct((H, S, 128), F32)),
        compiler_params=_params(("parallel", "arbitrary")),
    )(q_tab, k_tab, q, kvu, kpe, do, o, lse_row, cos_t, sin_t)


def _mesh_pos():
    return lax.axis_index("x"), lax.axis_index("y"), lax.axis_index("c")


def _flip(pos, k):
    x, y, c = pos
    return (1 - x if k & 4 else x, 1 - y if k & 2 else y, 1 - c if k & 1 else c)


def _rank(pos):
    return 4 * pos[0] + 2 * pos[1] + pos[2]


def _copies(n, src_of, dst_refs, local_src_of, send_sems, recv_sems, local_sems):
    me = _mesh_pos()
    local = [pltpu.make_async_copy(local_src_of(a), dst_refs[a].at[_rank(me)], local_sems.at[a]) for a in range(n)]
    sends, arrivals = [], []
    for k in range(1, N_DEV):
        peer = _flip(me, k)
        for a in range(n):
            idx = a * (N_DEV - 1) + k - 1
            for into, group in ((me, sends), (peer, arrivals)):
                group.append(pltpu.make_async_remote_copy(
                    src_ref=src_of(a, peer), dst_ref=dst_refs[a].at[_rank(into)],
                    send_sem=send_sems.at[idx], recv_sem=recv_sems.at[idx],
                    device_id=peer, device_id_type=pl.DeviceIdType.MESH))
    return local, sends, arrivals


def _exchange_start(*args):
    local, sends, _ = _copies(*args)
    for cp in local + sends:
        cp.start()


def _exchange_wait(*args):
    local, sends, arrivals = _copies(*args)
    for cp in arrivals:
        cp.wait_recv()
    for cp in sends:
        cp.wait_send()
    for cp in local:
        cp.wait()


def _exchange(*args):
    _exchange_start(*args)
    _exchange_wait(*args)


def _sems(n):
    return [pltpu.SemaphoreType.DMA((n * (N_DEV - 1),)), pltpu.SemaphoreType.DMA((n * (N_DEV - 1),)),
            pltpu.SemaphoreType.DMA((n,))]


_ANY = pl.BlockSpec(memory_space=pl.ANY)


class _Ride:
    def __init__(self, kind, arrays):
        self.kind, self.arrays, self.n = kind, list(arrays), len(arrays)

    def out_shape(self):
        lead = (N_DEV,) if self.kind == "gather" else ()
        return [jax.ShapeDtypeStruct(lead + a.shape, a.dtype) for a in self.arrays]

    def _args(self, x_refs, out_refs, sems):
        if self.kind == "gather":
            return (self.n, lambda a, peer: x_refs[a], out_refs, lambda a: x_refs[a]) + tuple(sems)
        me = _mesh_pos()
        return (self.n, lambda a, peer: x_refs[a].at[_rank(peer)], out_refs, lambda a: x_refs[a].at[_rank(me)]) + tuple(sems)

    def start(self, x_refs, out_refs, sems):
        _exchange_start(*self._args(x_refs, out_refs, sems))

    def wait(self, x_refs, out_refs, sems):
        _exchange_wait(*self._args(x_refs, out_refs, sems))


def _pallas(body, name, grid, operands, in_specs, out_specs, out_shape, scratch=(), sem=None, ride=None, prefetch=()):
    n_pf, n_in, n_out, n_scr = len(prefetch), len(in_specs), len(out_specs), len(scratch)
    nr = ride.n if ride else 0

    def wrapped(*refs):
        refs = list(refs)
        pf, refs = refs[:n_pf], refs[n_pf:]
        ins, r_in = refs[:n_in], refs[n_in:n_in + nr]
        outs, r_out = refs[n_in + nr:n_in + nr + n_out], refs[n_in + nr + n_out:n_in + 2 * nr + n_out]
        rest = refs[n_in + 2 * nr + n_out:]
        scr, sems = rest[:n_scr], rest[n_scr:]
        if ride:
            first, last = None, None
            for d, size in enumerate(grid):
                i = pl.program_id(d)
                first = (i == 0) if first is None else jnp.logical_and(first, i == 0)
                last = (i == size - 1) if last is None else jnp.logical_and(last, i == size - 1)

            @pl.when(first)
            def _():
                ride.start(r_in, r_out, sems)

        body(pf + ins, outs, scr)
        if ride:
            @pl.when(last)
            def _():
                ride.wait(r_in, r_out, sems)

    grid_spec = pltpu.PrefetchScalarGridSpec(
        num_scalar_prefetch=n_pf, grid=grid,
        in_specs=list(in_specs) + [_ANY] * nr, out_specs=tuple(list(out_specs) + [_ANY] * nr),
        scratch_shapes=list(scratch) + (_sems(nr) if ride else []))
    sem = tuple(sem) if sem is not None and not ride else ("arbitrary",) * len(grid)
    return list(_pcall(
        wrapped, name=name, grid_spec=grid_spec,
        out_shape=tuple(list(out_shape) + (ride.out_shape() if ride else [])),
        compiler_params=_params(sem),
    )(*(list(prefetch) + list(operands) + (ride.arrays if ride else []))))


def _all_gather(shards, name):
    n = len(shards)

    def body(*refs):
        x_refs, out_refs = refs[:n], refs[n:2 * n]
        _exchange(n, lambda a, peer: x_refs[a], out_refs, lambda a: x_refs[a], *refs[2 * n:])

    return _pcall(
        body, name=name, in_specs=[_ANY] * n, out_specs=tuple([_ANY] * n),
        out_shape=tuple(jax.ShapeDtypeStruct((N_DEV,) + s.shape, s.dtype) for s in shards),
        scratch_shapes=_sems(n),
    )(*shards)


def _all_to_all(parts, name):
    n = len(parts)

    def body(*refs):
        x_refs, out_refs = refs[:n], refs[n:2 * n]
        me = _mesh_pos()
        _exchange(n, lambda a, peer: x_refs[a].at[_rank(peer)], out_refs, lambda a: x_refs[a].at[_rank(me)],
                  *refs[2 * n:])

    return _pcall(
        body, name=name, in_specs=[_ANY] * n, out_specs=tuple([_ANY] * n),
        out_shape=tuple(jax.ShapeDtypeStruct(p.shape, p.dtype) for p in parts),
        scratch_shapes=_sems(n),
    )(*parts)


def _all_reduce_small(v):
    R, C = v.shape

    def body(x_ref, out_ref, buf_ref, send_sems, recv_sems, local_sems):
        _exchange(1, lambda a, peer: x_ref, [buf_ref], lambda a: x_ref, send_sems, recv_sems, local_sems)
        total = buf_ref[0]
        for j in range(1, N_DEV):
            total = total + buf_ref[j]
        out_ref[...] = total

    vm = pl.BlockSpec(memory_space=pltpu.VMEM)
    return _pcall(
        body, name="all_reduce_small", in_specs=[vm], out_specs=vm,
        out_shape=jax.ShapeDtypeStruct((R, C), F32),
        scratch_shapes=[pltpu.VMEM((N_DEV, R, C), F32)] + _sems(1),
    )(v)


def _sum_slots(parts, name):
    shape = parts.shape[1:]
    C = shape[-1]
    R = parts.size // (N_DEV * C)
    tr = R
    while N_DEV * tr * C * 4 > (4 << 20) and tr % 32 == 0:
        tr //= 2

    def body(p_ref, o_ref):
        total = p_ref[0].astype(F32)
        for j in range(1, N_DEV):
            total = total + p_ref[j].astype(F32)
        o_ref[...] = total

    return _pcall(
        body, name=name, grid=(R // tr,),
        in_specs=[pl.BlockSpec((N_DEV, tr, C), lambda i: (0, i, 0))],
        out_specs=pl.BlockSpec((tr, C), lambda i: (i, 0)),
        out_shape=jax.ShapeDtypeStruct((R, C), F32),
        compiler_params=_params(("parallel",)),
    )(parts.reshape(N_DEV, R, C)).reshape(shape)


def _adamw(w, g, m, v, name):
    shape = w.shape
    C = shape[-1]
    R = w.size // C
    tr = R
    while tr * C * 4 > (1 << 20) and tr % 16 == 0:
        tr //= 2

    def body(w_ref, g_ref, m_ref, v_ref, d_ref, nm_ref, nv_ref):
        gv = g_ref[...]
        nm = ADAM_B1 * m_ref[...] + (1.0 - ADAM_B1) * gv
        nv = ADAM_B2 * v_ref[...] + (1.0 - ADAM_B2) * (gv * gv)
        m_hat = nm / (1.0 - ADAM_B1 ** ADAM_STEP)
        v_hat = nv / (1.0 - ADAM_B2 ** ADAM_STEP)
        d_ref[...] = -ADAM_LR * (m_hat / (jnp.sqrt(v_hat) + ADAM_EPS) + ADAM_WD * w_ref[...])
        nm_ref[...] = nm
        nv_ref[...] = nv

    blk = pl.BlockSpec((tr, C), lambda i: (i, 0))
    sds = jax.ShapeDtypeStruct((R, C), F32)
    outs = _pcall(
        body, name=name, grid=(R // tr,),
        in_specs=[blk] * 4, out_specs=(blk,) * 3, out_shape=(sds,) * 3,
        compiler_params=_params(("parallel",)),
    )(*(t.reshape(R, C) for t in (w, g, m, v)))
    return tuple(o.reshape(shape) for o in outs)


_REPLICATED = ("kv_in_norm", "kv_latent_norm", "attn_norm", "q_latent_norm", "ffn_norm", "final_norm")
_WEIGHTS = ("pool_norm", "pool_w", "pool_scale", "kv_in_norm", "w_kv_a", "kv_latent_norm", "w_kv_b", "attn_norm",
            "w_q_a", "q_latent_norm", "w_q_b", "w_o", "ffn_norm", "w_gate", "w_up", "w_down", "final_norm")


def _ffn_fwd(x, norm_g, wg, wu, wd, tag, gather=None):
    S, D = x.shape
    fs = wg.shape[1]
    h, ht = _rmsnorm(x, norm_g, "ffn_norm" + tag, transposed_too=True)
    g, u, a, *gathered = _ffn_up(h, wg, wu, "ffn_up" + tag, ride=gather)
    if gather is not None:
        wd = gathered[0].reshape(N_DEV * fs, D)
    y = _ffn_down(a, wd, x, "ffn_down" + tag)
    return y, (ht, g, u, a), wd, gathered


def _ffn_bwd(x, norm_g, wg, wu, wd, saved, dy, dyb, tag, with_bf16, scatter=None):
    S, D = x.shape
    ht, g, u, a = saved
    fs = g.shape[1]
    tk = _tile(S, 2 * ROW_TILE)
    nkb = S // tk
    dg, du, *got_rider = _ffn_dact(dyb, wd, g, u, "ffn_dact" + tag, ride=scatter)
    dwd = _mm("ffn_dwd" + tag, "tn", a, dyb, (fs, D, tk), (N_DEV, 1, nkb),
              lambda j, q, k: (j * nkb + k, 0), lambda j, q, k: (k, 0), lambda j, q, k: (j, 0), (N_DEV * fs, D), BF16)
    dwg, (got_dwd,) = _slots_dw_t("ffn_dwg" + tag, ht, dg, ride=_Ride("scatter", [dwd.reshape(N_DEV, fs, D)]))
    dwu, (got_dwg,) = _slots_dw_t("ffn_dwu" + tag, ht, du, ride=_Ride("scatter", [dwg.reshape(N_DEV, D, fs)]))
    dh, got_dwu = _ffn_dh(dg, du, wg, wu, S, "ffn_dh" + tag, ride=_Ride("scatter", [dwu.reshape(N_DEV, D, fs)]))
    outs = _rmsnorm_bwd(x, norm_g, dh, dy, "ffn_norm_bwd" + tag, with_bf16=with_bf16)
    return outs, got_rider, (got_dwg, got_dwu, got_dwd)


def kernel(x, positions, pool_norm, pool_w, pool_scale, kv_in_norm, w_kv_a, kv_latent_norm, w_kv_b, attn_norm, w_q_a, q_latent_norm, w_q_b, w_o, ffn_norm, w_gate, w_up, w_down, final_norm, loss_target, m_pool_norm, m_pool_w, m_pool_scale, m_kv_in_norm, m_w_kv_a, m_kv_latent_norm, m_w_kv_b, m_attn_norm, m_w_q_a, m_q_latent_norm, m_w_q_b, m_w_o, m_ffn_norm, m_w_gate, m_w_up, m_w_down, m_final_norm, v_pool_norm, v_pool_w, v_pool_scale, v_kv_in_norm, v_w_kv_a, v_kv_latent_norm, v_w_kv_b, v_attn_norm, v_w_q_a, v_q_latent_norm, v_w_q_b, v_w_o, v_ffn_norm, v_w_gate, v_w_up, v_w_down, v_final_norm):
    env = dict(locals())
    weights = {n: env[n] for n in _WEIGHTS}
    m_in = {n: env["m_" + n] for n in _WEIGHTS}
    v_in = {n: env["v_" + n] for n in _WEIGHTS}

    S, D = x.shape[1], x.shape[2]
    xs = x.reshape(S, D)
    target = loss_target.reshape(S, D)
    G, gws, gw = pool_w.shape[1:]
    Ds = w_kv_a.shape[0]
    R_kv, kvw = w_kv_b.shape
    hpd = kvw // (QK_NOPE_DIM + V_DIM)
    H = hpd * N_DEV
    R_q = w_q_a.shape[2]
    fs = w_gate.shape[2]
    bf = lambda t: t.astype(BF16)

    gains = jnp.concatenate([pool_norm, pool_scale], axis=0)
    g_pool_w, g_gains, g_wg0, g_wu0 = _all_gather([bf(pool_w[0]), gains, bf(w_gate[0]), bf(w_up[0])], "gather_first")
    wqb_pad = jnp.pad(w_q_b.reshape(R_q, hpd, QK_DIM), ((0, 0), (0, 0), (0, HEAD_PAD - QK_DIM))).reshape(R_q, hpd * HEAD_PAD)
    gather_mla = _Ride("gather", [bf(w_down[0]), bf(jnp.pad(w_kv_a, ((0, 0), (0, 128 - QK_ROPE_DIM)))), bf(w_kv_b),
                                  bf(w_q_a[0]), bf(wqb_pad), bf(w_o[0])])
    gather_ffn1 = _Ride("gather", [bf(w_gate[1]), bf(w_up[1]), bf(w_down[1])])

    wp = jnp.swapaxes(g_pool_w, 0, 1).reshape(G, gw, gw)
    gains_full = jnp.swapaxes(g_gains, 0, 1).reshape(2, D)
    g_pool_norm, g_pool_scale = gains_full[0:1], gains_full[1:2]
    wg0, wu0 = g_wg0.reshape(N_DEV * D, fs), g_wu0.reshape(N_DEV * D, fs)

    g_kv_in = kv_in_norm.reshape(1, D)
    g_kv_lat = kv_latent_norm.reshape(1, R_kv)
    g_attn = attn_norm.reshape(1, D)
    g_q_lat = q_latent_norm.reshape(1, R_q)
    g_final = final_norm.reshape(1, D)

    half = QK_ROPE_DIM // 2
    inv_freq = ROPE_BASE ** (-jnp.arange(half, dtype=F32) / half)
    ang = positions.reshape(S).astype(F32)[:, None] * inv_freq
    cos, sin = jnp.cos(ang), jnp.sin(ang)
    zeros = jnp.zeros((S, 128 - QK_ROPE_DIM), F32)
    cos_t = jnp.concatenate([cos, cos, zeros], axis=1)
    sin_t = jnp.concatenate([-sin, sin, zeros], axis=1)

    diff = _pool_pre(xs, g_pool_norm)
    x1 = _pool_mix(diff, wp, g_pool_scale, xs)
    x2, ffn0, wd0, (_, g_wkva, g_wkvb, g_wqa, g_wqb, g_wo) = _ffn_fwd(x1, ffn_norm[0:1], wg0, wu0, None, "0", gather_mla)
    wkva = g_wkva.reshape(D, R_kv + 128)
    wkvb = g_wkvb.reshape(N_DEV * R_kv, kvw)
    wqa = g_wqa.reshape(D, R_q)
    wqb = g_wqb.reshape(N_DEV * R_q, hpd * HEAD_PAD)
    wo = g_wo.reshape(H * V_DIM, D)

    hk = _rmsnorm(x2, g_kv_in, "kv_in_norm")
    kv = _mm_plain("kv_a", "nn", hk, wkva)
    ckv, kpe = _kv_mid(kv, g_kv_lat, cos_t, sin_t)
    kvu = _small_slots_out("kv_b", ckv, wkvb, BF16)

    ha = _rmsnorm(x2, g_attn, "attn_norm")
    ql = _mm_plain("q_a", "nn", ha, wqa)
    qn = _rmsnorm(ql, g_q_lat, "q_latent_norm")
    tab = pl.BlockSpec((_tile(S, ROW_TILE), 128), lambda i: (i, 0))
    qr = _small_slots_out("q_b", qn, wqb, BF16, epilogue=(_rope_heads, [cos_t, sin_t], [tab, tab]))
    o, lse, g_wg1, g_wu1, g_wd1 = _attn_fwd(qr, kvu, kpe, H, ride=gather_ffn1)
    wg1, wu1, wd1 = g_wg1.reshape(N_DEV * D, fs), g_wu1.reshape(N_DEV * D, fs), g_wd1.reshape(N_DEV * fs, D)
    x3 = _mm_plain("attn_out", "nn", o, wo, res=x2)
    x4, ffn1, _, _ = _ffn_fwd(x3, ffn_norm[1:2], wg1, wu1, wd1, "1")

    loss_part, dx4, dx4b, d_final = _loss_head(x4, g_final, target)

    (dx3, dx3b, d_ffn1), _, got_ffn1 = _ffn_bwd(x3, ffn_norm[1:2], wg1, wu1, wd1, ffn1, dx4, dx4b, "1", True)

    do = _mm_plain("attn_out_dx", "nt", dx3b, wo, out_dtype=BF16)
    dwo = _mm_plain("attn_out_dw", "tn", o, dx3b, out_dtype=BF16)
    dq, dkvu, dkpe = _attn_bwd(qr, kvu, kpe, do, o, lse.reshape(H, 1, S), cos_t, sin_t, H)

    dqn = _small_slots_in_nt("q_b_dx", dq, wqb, S)
    dwqb = _small_slots_dw("q_b_dw", qn, dq, BF16)
    dql, d_q_lat = _rmsnorm_bwd(ql, g_q_lat, dqn, None, "q_latent_norm_bwd", out_dtype=BF16)
    dha = _mm_plain("q_a_dx", "nt", dql, wqa)
    dwqa = _mm_plain("q_a_dw", "tn", ha, dql, out_dtype=BF16)
    dx2, d_attn = _rmsnorm_bwd(x2, g_attn, dha, dx3, "attn_norm_bwd")

    dckv = _small_slots_in_nt("kv_b_dx", dkvu, wkvb, S)
    dwkvb = _small_slots_dw("kv_b_dw", ckv, dkvu, BF16)
    dkv, d_kv_lat = _kv_mid_bwd(kv, g_kv_lat, dckv, dkpe, cos_t, sin_t)
    dhk = _mm_plain("kv_a_dx", "nt", dkv, wkva)
    dwkva = _mm_plain("kv_a_dw", "tn", hk, dkv, out_dtype=BF16)
    dx2, dx2b, d_kv_in = _rmsnorm_bwd(x2, g_kv_in, dhk, dx2, "kv_in_norm_bwd", with_bf16=True)

    scatter_mla = _Ride("scatter", [dwkva.reshape(N_DEV, Ds, R_kv + 128), dwkvb.reshape(N_DEV, R_kv, kvw),
                                    dwqa.reshape(N_DEV, Ds, R_q), dwqb.reshape(N_DEV, R_q, hpd * HEAD_PAD),
                                    dwo.reshape(N_DEV, H * V_DIM // N_DEV, D)])
    (dx1, d_ffn0), got_mla, got_ffn0 = _ffn_bwd(x1, ffn_norm[0:1], wg0, wu0, wd0, ffn0, dx2, dx2b, "0", False,
                                                  scatter=scatter_mla)

    ddiff, dwp, d_pool_scale = _pool_mix_bwd(diff, wp, g_pool_scale, dx1)
    grad_x, d_pool_norm = _pool_pre_bwd(xs, g_pool_norm, ddiff, dx1)

    d_gains = jnp.swapaxes(jnp.concatenate([d_pool_norm, d_pool_scale], axis=0).reshape(2, N_DEV, D // N_DEV), 0, 1)
    got_pool_w, got_gains = _all_to_all([jnp.swapaxes(dwp.reshape(G, N_DEV, gws, gw), 0, 1), d_gains], "exchange_pool")

    received = dict(zip(("w_kv_a", "w_kv_b", "w_q_a", "w_q_b", "w_o"), got_mla))
    received.update(zip(("w_gate0", "w_up0", "w_down0"), got_ffn0))
    received.update(zip(("w_gate1", "w_up1", "w_down1"), got_ffn1))
    received.update(pool_w=got_pool_w, gains=got_gains)
    sums = {n: _sum_slots(r, "sum_" + n) for n, r in received.items()}
    grads = {
        "pool_w": sums["pool_w"][None],
        "w_kv_a": sums["w_kv_a"][:, :R_kv + QK_ROPE_DIM],
        "w_kv_b": sums["w_kv_b"],
        "w_q_a": sums["w_q_a"][None],
        "w_q_b": sums["w_q_b"].reshape(R_q, hpd, HEAD_PAD)[:, :, :QK_DIM].reshape(1, R_q, hpd * QK_DIM),
        "w_o": sums["w_o"][None],
        "w_gate": jnp.stack([sums["w_gate0"], sums["w_gate1"]]),
        "w_up": jnp.stack([sums["w_up0"], sums["w_up1"]]),
        "w_down": jnp.stack([sums["w_down0"], sums["w_down1"]]),
        "pool_norm": sums["gains"][0:1],
        "pool_scale": sums["gains"][1:2],
    }

    small = {"kv_in_norm": d_kv_in, "kv_latent_norm": d_kv_lat, "attn_norm": d_attn, "q_latent_norm": d_q_lat,
             "ffn_norm": jnp.concatenate([d_ffn0, d_ffn1], axis=0), "final_norm": d_final}
    small_packed = jnp.concatenate([small[n].reshape(-1) for n in _REPLICATED] + [loss_part.reshape(-1)])
    pad = (-small_packed.shape[0]) % (8 * 128)
    reduced = _all_reduce_small(jnp.pad(small_packed, (0, pad)).reshape(-1, 128)).reshape(-1)
    off = 0
    for n in _REPLICATED:
        grads[n] = reduced[off:off + weights[n].size].reshape(weights[n].shape)
        off += weights[n].size
    loss = reduced[off]

    delta_w, new_m, new_v = {}, {}, {}
    for n in _WEIGHTS:
        delta_w[n], new_m[n], new_v[n] = _adamw(weights[n], grads[n], m_in[n], v_in[n], "adamw_" + n)

    return (loss, grad_x.reshape(x.shape), *[grads[n] for n in _WEIGHTS], *[delta_w[n] for n in _WEIGHTS],
            *[new_m[n] for n in _WEIGHTS], *[new_v[n] for n in _WEIGHTS])
```

```python
import math

import jax
import jax.numpy as jnp
import numpy as np
from jax import lax
from jax.experimental import pallas as pl
from jax.experimental.pallas import tpu as pltpu

F32 = jnp.float32
BF16 = jnp.bfloat16

N_DEV = 8
POOL_WINDOWS = (2, 4, 8, 16)
POOL_HALO = 16
QK_NOPE_DIM = 128
QK_ROPE_DIM = 64
QK_DIM = QK_NOPE_DIM + QK_ROPE_DIM
HEAD_PAD = 256
V_DIM = 128
ROPE_BASE = 10000.0
ATTN_SCALE = 1.0 / math.sqrt(QK_DIM)
EXP2_SCALE = ATTN_SCALE * math.log2(math.e)
NORM_EPS = 1e-6
ADAM_LR = 0.001
ADAM_B1 = 0.9
ADAM_B2 = 0.999
ADAM_EPS = 1e-08
ADAM_WD = 0.01
ADAM_STEP = 10

ROW_TILE = 512
ATT_BLOCK = 1024
ATT_ROW_CHUNK = 256
FFN_SLOTS = 2
MM_TN, MM_TK = 512, 512
VMEM_LIMIT = 48 * 1024 * 1024

_pcall = pl.pallas_call


def _params(sem):
    return pltpu.CompilerParams(dimension_semantics=sem, vmem_limit_bytes=VMEM_LIMIT)


def _tile(dim, default):
    if dim % default == 0:
        return default
    assert dim <= 2 * default, (dim, default)
    return dim


def _row_tile(rows, width):
    ts = _tile(rows, ROW_TILE)
    while ts * width * 4 > (2 << 20) and ts % 16 == 0:
        ts //= 2
    return ts


def _rms_scale(x):
    return lax.rsqrt(jnp.mean(x * x, axis=-1, keepdims=True) + NORM_EPS)


def _rms_bwd(x, g, dy):
    r = _rms_scale(x)
    xn = x * r
    u = dy * g
    dx = r * (u - xn * jnp.mean(u * xn, axis=-1, keepdims=True))
    return dx, dy * xn


def _swap_halves(x):
    lane = lax.broadcasted_iota(jnp.int32, x.shape, 1)
    return jnp.where(lane < QK_ROPE_DIM // 2, pltpu.roll(x, 128 - QK_ROPE_DIM // 2, 1), pltpu.roll(x, QK_ROPE_DIM // 2, 1))


def _rope(x, cos_t, sin_t):
    return x * cos_t + _swap_halves(x) * sin_t


def _rmsnorm(x, g, name, transposed_too=False):
    S, D = x.shape
    ts = _row_tile(S, D)

    def body(x_ref, g_ref, o_ref, *t_ref):
        xf = x_ref[...]
        y = xf * _rms_scale(xf) * g_ref[...]
        o_ref[...] = y.astype(o_ref.dtype)
        if transposed_too:
            t_ref[0][...] = y.T.astype(o_ref.dtype)

    row = pl.BlockSpec((ts, D), lambda i: (i, 0))
    outs = _pcall(
        body, name=name, grid=(S // ts,),
        in_specs=[row, pl.BlockSpec((1, D), lambda i: (0, 0))],
        out_specs=(row, pl.BlockSpec((D, ts), lambda i: (0, i))) if transposed_too else row,
        out_shape=((jax.ShapeDtypeStruct((S, D), BF16), jax.ShapeDtypeStruct((D, S), BF16)) if transposed_too
                   else jax.ShapeDtypeStruct((S, D), BF16)),
        compiler_params=_params(("parallel",)),
    )(x, g)
    return outs


def _rmsnorm_bwd(x, g, dy, res, name, out_dtype=F32, with_bf16=False):
    S, D = x.shape
    ts = _row_tile(S, D)
    has_res = res is not None

    def body(*refs):
        refs = list(refs)
        x_ref, g_ref, dy_ref = refs[:3]
        res_ref = refs[3] if has_res else None
        outs = refs[3 + has_res:]
        dx_ref, dg_ref = outs[0], outs[-1]
        dx, dgt = _rms_bwd(x_ref[...], g_ref[...], dy_ref[...].astype(F32))
        if has_res:
            dx = res_ref[...] + dx
        dx_ref[...] = dx.astype(dx_ref.dtype)
        if with_bf16:
            outs[1][...] = dx.astype(BF16)

        @pl.when(pl.program_id(0) == 0)
        def _():
            dg_ref[...] = jnp.zeros_like(dg_ref)

        dg_ref[...] += jnp.sum(dgt, axis=0, keepdims=True)

    row = pl.BlockSpec((ts, D), lambda i: (i, 0))
    vec = pl.BlockSpec((1, D), lambda i: (0, 0))
    out_specs = [row] + ([row] if with_bf16 else []) + [vec]
    out_shape = ([jax.ShapeDtypeStruct((S, D), out_dtype)] + ([jax.ShapeDtypeStruct((S, D), BF16)] if with_bf16 else [])
                 + [jax.ShapeDtypeStruct((1, D), F32)])
    return _pcall(
        body, name=name, grid=(S // ts,),
        in_specs=[row, vec, row] + ([row] if has_res else []),
        out_specs=tuple(out_specs), out_shape=tuple(out_shape),
        compiler_params=_params(("arbitrary",)),
    )(*([x, g, dy] + ([res] if has_res else [])))


def _pool_pre(x, g):
    S, D = x.shape
    ts = _row_tile(S, D)
    gw = D // len(POOL_WINDOWS)
    hb = ts // POOL_HALO

    def body(x_ref, halo_ref, g_ref, o_ref):
        i = pl.program_id(0)
        xx = jnp.concatenate([halo_ref[...], x_ref[...]], axis=0)
        h = xx * _rms_scale(xx) * g_ref[...]
        t = i * ts - POOL_HALO + lax.broadcasted_iota(jnp.int32, (ts + POOL_HALO, 1), 0)
        h = jnp.where(t >= 0, h, 0.0)
        tf = t[POOL_HALO:].astype(F32)
        for gi, w in enumerate(POOL_WINDOWS):
            hg = h[:, gi * gw:(gi + 1) * gw]
            acc, span = hg, 1
            while span < w:
                acc = acc + pltpu.roll(acc, span, 0)
                span *= 2
            count = jnp.minimum(tf + 1.0, float(w))
            diff = acc[POOL_HALO:] / count - hg[POOL_HALO:]
            o_ref[:, gi * gw:(gi + 1) * gw] = diff.astype(o_ref.dtype)

    return _pcall(
        body, name="pool_pre", grid=(S // ts,),
        in_specs=[pl.BlockSpec((ts, D), lambda i: (i, 0)),
                  pl.BlockSpec((POOL_HALO, D), lambda i: (jnp.maximum(i * hb - 1, 0), 0)),
                  pl.BlockSpec((1, D), lambda i: (0, 0))],
        out_specs=pl.BlockSpec((ts, D), lambda i: (i, 0)),
        out_shape=jax.ShapeDtypeStruct((S, D), BF16),
        compiler_params=_params(("parallel",)),
    )(x, x, g)


def _pool_pre_bwd(x, g, dd, res):
    S, D = x.shape
    ts = _row_tile(S, D)
    gw = D // len(POOL_WINDOWS)
    hb = ts // POOL_HALO
    last_halo = S // POOL_HALO - 1

    def body(x_ref, g_ref, dd_ref, halo_ref, res_ref, dx_ref, dg_ref):
        i = pl.program_id(0)
        ee = jnp.concatenate([dd_ref[...], halo_ref[...]], axis=0)
        t = i * ts + lax.broadcasted_iota(jnp.int32, (ts + POOL_HALO, 1), 0)
        ee = jnp.where(t < S, ee, 0.0)
        tf = t.astype(F32)
        n = ts + POOL_HALO
        for gi, w in enumerate(POOL_WINDOWS):
            eg = ee[:, gi * gw:(gi + 1) * gw]
            acc, span = eg / jnp.minimum(tf + 1.0, float(w)), 1
            while span < w:
                acc = acc + pltpu.roll(acc, n - span, 0)
                span *= 2
            dx_ref[:, gi * gw:(gi + 1) * gw] = acc[:ts] - eg[:ts]
        dx, dgt = _rms_bwd(x_ref[...], g_ref[...], dx_ref[...])
        dx_ref[...] = res_ref[...] + dx

        @pl.when(i == 0)
        def _():
            dg_ref[...] = jnp.zeros_like(dg_ref)

        dg_ref[...] += jnp.sum(dgt, axis=0, keepdims=True)

    row = pl.BlockSpec((ts, D), lambda i: (i, 0))
    vec = pl.BlockSpec((1, D), lambda i: (0, 0))
    return _pcall(
        body, name="pool_pre_bwd", grid=(S // ts,),
        in_specs=[row, vec, row,
                  pl.BlockSpec((POOL_HALO, D), lambda i: (jnp.minimum((i + 1) * hb, last_halo), 0)), row],
        out_specs=(row, vec),
        out_shape=(jax.ShapeDtypeStruct((S, D), F32), jax.ShapeDtypeStruct((1, D), F32)),
        compiler_params=_params(("arbitrary",)),
    )(x, g, dd, dd, res)


def _pool_mix(diff, w, scale, x):
    S, D = x.shape
    G, gw, _ = w.shape
    ts = _tile(S, ROW_TILE)

    def body(d_ref, w_ref, s_ref, x_ref, o_ref):
        mo = jnp.dot(d_ref[...], w_ref[0], preferred_element_type=F32)
        o_ref[...] = x_ref[...] + mo * s_ref[...]

    blk = pl.BlockSpec((ts, gw), lambda i, g: (i, g))
    return _pcall(
        body, name="pool_mix", grid=(S // ts, G),
        in_specs=[blk, pl.BlockSpec((1, gw, gw), lambda i, g: (g, 0, 0)), pl.BlockSpec((1, gw), lambda i, g: (0, g)), blk],
        out_specs=blk,
        out_shape=jax.ShapeDtypeStruct((S, D), F32),
        compiler_params=_params(("parallel", "parallel")),
    )(diff, w, scale, x)


def _pool_mix_bwd(diff, w, scale, dy):
    S, D = dy.shape
    G, gw, _ = w.shape
    ts = _tile(S, ROW_TILE)

    def body(d_ref, w_ref, s_ref, dy_ref, dd_ref, dw_ref, ds_ref):
        i = pl.program_id(1)
        d = d_ref[...]
        dy = dy_ref[...]
        mo = jnp.dot(d, w_ref[0], preferred_element_type=F32)
        dmo = (dy * s_ref[...]).astype(BF16)
        dd_ref[...] = lax.dot_general(dmo, w_ref[0], (((1,), (1,)), ((), ())), preferred_element_type=F32)

        @pl.when(i == 0)
        def _():
            dw_ref[...] = jnp.zeros_like(dw_ref)
            ds_ref[...] = jnp.zeros_like(ds_ref)

        dw_ref[0] += lax.dot_general(d, dmo, (((0,), (0,)), ((), ())), preferred_element_type=F32)
        ds_ref[...] += jnp.sum(dy * mo, axis=0, keepdims=True)

    blk = pl.BlockSpec((ts, gw), lambda g, i: (i, g))
    wspec = pl.BlockSpec((1, gw, gw), lambda g, i: (g, 0, 0))
    vec = pl.BlockSpec((1, gw), lambda g, i: (0, g))
    return _pcall(
        body, name="pool_mix_bwd", grid=(G, S // ts),
        in_specs=[blk, wspec, vec, blk],
        out_specs=(blk, wspec, vec),
        out_shape=(jax.ShapeDtypeStruct((S, D), F32), jax.ShapeDtypeStruct((G, gw, gw), F32),
                   jax.ShapeDtypeStruct((1, D), F32)),
        compiler_params=_params(("parallel", "arbitrary")),
    )(diff, w, scale, dy)


def _loss_head(x, g, target):
    S, D = x.shape
    ts = _row_tile(S, D)

    def body(x_ref, g_ref, t_ref, loss_ref, dx_ref, dxb_ref, dg_ref):
        xf = x_ref[...]
        gv = g_ref[...]
        err = xf * _rms_scale(xf) * gv - t_ref[...]
        dx, dgt = _rms_bwd(xf, gv, err * (1.0 / D))
        dx_ref[...] = dx
        dxb_ref[...] = dx.astype(BF16)

        @pl.when(pl.program_id(0) == 0)
        def _():
            loss_ref[...] = jnp.zeros_like(loss_ref)
            dg_ref[...] = jnp.zeros_like(dg_ref)

        part = 0.5 * jnp.sum(jnp.sum(err * err, axis=-1, keepdims=True) * (1.0 / D), axis=0, keepdims=True)
        loss_ref[...] += jnp.broadcast_to(part, loss_ref.shape)
        dg_ref[...] += jnp.sum(dgt, axis=0, keepdims=True)

    row = pl.BlockSpec((ts, D), lambda i: (i, 0))
    vec = pl.BlockSpec((1, D), lambda i: (0, 0))
    return _pcall(
        body, name="loss_head", grid=(S // ts,),
        in_specs=[row, vec, row],
        out_specs=(pl.BlockSpec((1, 128), lambda i: (0, 0)), row, row, vec),
        out_shape=(jax.ShapeDtypeStruct((1, 128), F32), jax.ShapeDtypeStruct((S, D), F32),
                   jax.ShapeDtypeStruct((S, D), BF16), jax.ShapeDtypeStruct((1, D), F32)),
        compiler_params=_params(("arbitrary",)),
    )(x, g, target)


_DIMS = {"nn": (((1,), (0,)), ((), ())), "nt": (((1,), (1,)), ((), ())), "tn": (((0,), (0,)), ((), ()))}


def _mm(name, mode, a, b, tiles, grid, a_map, b_map, o_map, out_shape, out_dtype=F32, res=None, a2=None, b2=None,
        ride=None, epilogue=None):
    tm, tn, tk = tiles
    nk = grid[-1]
    dual, has_res = a2 is not None, res is not None
    n_main = 2 + 2 * dual
    dn = _DIMS[mode]

    def body(ins, outs, scratch):
        o_ref = outs[0]

        def product():
            part = lax.dot_general(ins[0][...].astype(BF16), ins[1][...].astype(BF16), dn, preferred_element_type=F32)
            if dual:
                part += lax.dot_general(ins[2][...].astype(BF16), ins[3][...].astype(BF16), dn,
                                        preferred_element_type=F32)
            return part

        def finish(out):
            if has_res:
                out = ins[n_main][...] + out
            if epilogue is not None:
                out = epilogue[0](out, *[r[...] for r in ins[n_main + has_res:]])
            o_ref[...] = out.astype(o_ref.dtype)

        if nk == 1:
            finish(product())
            return
        acc_ref = scratch[0]
        k = pl.program_id(2)

        @pl.when(k == 0)
        def _():
            acc_ref[...] = jnp.zeros_like(acc_ref)

        acc_ref[...] += product()

        @pl.when(k == nk - 1)
        def _():
            finish(acc_ref[...])

    a_spec = pl.BlockSpec((tk, tm) if mode == "tn" else (tm, tk), a_map)
    b_spec = pl.BlockSpec((tn, tk) if mode == "nt" else (tk, tn), b_map)
    o_spec = pl.BlockSpec((tm, tn), o_map)
    operands, in_specs = [a, b], [a_spec, b_spec]
    if dual:
        operands += [a2, b2]
        in_specs += [a_spec, b_spec]
    if has_res:
        operands.append(res)
        in_specs.append(o_spec)
    if epilogue is not None:
        operands += list(epilogue[1])
        in_specs += list(epilogue[2])
    outs = _pallas(body, name, grid, operands, in_specs, [o_spec], [jax.ShapeDtypeStruct(out_shape, out_dtype)],
                   scratch=[pltpu.VMEM((tm, tn), F32)] if nk > 1 else [],
                   sem=("parallel", "parallel", "arbitrary"), ride=ride)
    return (outs[0], outs[1:]) if ride else outs[0]


def _mm_plain(name, mode, a, b, out_dtype=F32, res=None):
    if mode == "tn":
        (K, M), N = a.shape, b.shape[1]
    elif mode == "nt":
        (M, K), N = a.shape, b.shape[0]
    else:
        (M, K), N = a.shape, b.shape[1]
    if mode == "tn":
        tm, tn, tk = _tile(M, 2 * MM_TN), _tile(N, 2 * MM_TN), _tile(K, ROW_TILE)
    else:
        tm = _tile(M, ROW_TILE)
        tk = K if K <= 4 * MM_TK else _tile(K, MM_TK)
        tn = N if N * tk * 2 <= (8 << 20) else _tile(N, MM_TN)
    a_map = (lambda i, j, k: (k, i)) if mode == "tn" else (lambda i, j, k: (i, k))
    b_map = (lambda i, j, k: (j, k)) if mode == "nt" else (lambda i, j, k: (k, j))
    return _mm(name, mode, a, b, (tm, tn, tk), (M // tm, N // tn, K // tk), a_map, b_map, lambda i, j, k: (i, j),
               (M, N), out_dtype, res)


def _slots_out(name, a, w_slots, out_dtype, epilogue=None):
    S, K = a.shape
    n = w_slots.shape[1]
    tm = _tile(S, ROW_TILE)
    nmb = S // tm
    return _mm(name, "nn", a, w_slots, (tm, n, K), (nmb, N_DEV, 1),
               lambda i, j, k: (i, 0), lambda i, j, k: (j, 0), lambda i, j, k: (j * nmb + i, 0),
               (N_DEV * S, n), out_dtype, epilogue=epilogue)


def _slots_in_nt(name, a_slots, w_slots, S, a2=None, w2=None, ride=None):
    n = a_slots.shape[1]
    K = w_slots.shape[0] // N_DEV
    tm = _tile(S, ROW_TILE)
    nmb = S // tm
    return _mm(name, "nt", a_slots, w_slots, (tm, K, n), (nmb, 1, N_DEV),
               lambda i, q, j: (j * nmb + i, 0), lambda i, q, j: (j, 0), lambda i, q, j: (i, 0),
               (S, K), F32, a2=a2, b2=w2, ride=ride)


def _slots_dw(name, a, d_slots, out_dtype=F32, ride=None):
    S, K = a.shape
    n = d_slots.shape[1]
    tk = _tile(S, ROW_TILE)
    nkb = S // tk
    return _mm(name, "tn", a, d_slots, (K, n, tk), (N_DEV, 1, nkb),
               lambda j, q, k: (k, 0), lambda j, q, k: (j * nkb + k, 0), lambda j, q, k: (j, 0),
               (N_DEV * K, n), out_dtype, ride=ride)


def _small_slots_out(name, a, w_slots, out_dtype, epilogue=None):
    S, K = a.shape
    n = w_slots.shape[1]
    tm = _tile(S, ROW_TILE)
    extra = epilogue[1] if epilogue else []

    def body(*refs):
        a_ref, w_ref, o_ref = refs[0], refs[1], refs[-1]
        av = a_ref[...].astype(BF16)
        tables = [r[...] for r in refs[2:-1]]
        for j in range(N_DEV):
            out = jnp.dot(av, w_ref[j], preferred_element_type=F32)
            if epilogue:
                out = epilogue[0](out, *tables)
            o_ref[j] = out.astype(o_ref.dtype)

    return _pcall(
        body, name=name, grid=(S // tm,),
        in_specs=[pl.BlockSpec((tm, K), lambda i: (i, 0)), pl.BlockSpec((N_DEV, K, n), lambda i: (0, 0, 0))]
        + (list(epilogue[2]) if epilogue else []),
        out_specs=pl.BlockSpec((N_DEV, tm, n), lambda i: (0, i, 0)),
        out_shape=jax.ShapeDtypeStruct((N_DEV, S, n), out_dtype),
        compiler_params=_params(("parallel",)),
    )(a, w_slots.reshape(N_DEV, K, n), *extra).reshape(N_DEV * S, n)


def _small_slots_in_nt(name, a_slots, w_slots, S):
    n = a_slots.shape[1]
    K = w_slots.shape[0] // N_DEV
    tm = _tile(S, ROW_TILE)

    def body(a_ref, w_ref, o_ref):
        total = lax.dot_general(a_ref[0], w_ref[0], _DIMS["nt"], preferred_element_type=F32)
        for j in range(1, N_DEV):
            total += lax.dot_general(a_ref[j], w_ref[j], _DIMS["nt"], preferred_element_type=F32)
        o_ref[...] = total

    return _pcall(
        body, name=name, grid=(S // tm,),
        in_specs=[pl.BlockSpec((N_DEV, tm, n), lambda i: (0, i, 0)), pl.BlockSpec((N_DEV, K, n), lambda i: (0, 0, 0))],
        out_specs=pl.BlockSpec((tm, K), lambda i: (i, 0)),
        out_shape=jax.ShapeDtypeStruct((S, K), F32),
        compiler_params=_params(("parallel",)),
    )(a_slots.reshape(N_DEV, S, n), w_slots.reshape(N_DEV, K, n))


def _small_slots_dw(name, a, d_slots, out_dtype):
    S, K = a.shape
    n = d_slots.shape[1]
    tk = _tile(S, ROW_TILE)
    nkb = S // tk

    def body(a_ref, d_ref, o_ref, acc_ref):
        k = pl.program_id(0)
        at = a_ref[...].astype(F32).T.astype(BF16)

        def add(first):
            for j in range(N_DEV):
                part = jnp.dot(at, d_ref[j], preferred_element_type=F32)
                if first:
                    acc_ref[j] = part
                else:
                    acc_ref[j] += part

        @pl.when(k == 0)
        def _():
            add(True)

        @pl.when(k > 0)
        def _():
            add(False)

        @pl.when(k == nkb - 1)
        def _():
            o_ref[...] = acc_ref[...].astype(o_ref.dtype)

    return _pcall(
        body, name=name, grid=(nkb,),
        in_specs=[pl.BlockSpec((tk, K), lambda k: (k, 0)), pl.BlockSpec((N_DEV, tk, n), lambda k: (0, k, 0))],
        out_specs=pl.BlockSpec((N_DEV, K, n), lambda k: (0, 0, 0)),
        out_shape=jax.ShapeDtypeStruct((N_DEV, K, n), out_dtype),
        scratch_shapes=[pltpu.VMEM((N_DEV, K, n), F32)],
        compiler_params=_params(("arbitrary",)),
    )(a, d_slots.reshape(N_DEV, S, n)).reshape(N_DEV * K, n)


def _sigmoid(x):
    return 1.0 / (1.0 + jnp.exp(-x))


def _ffn_up(h, wg, wu, name, ride=None):
    S, D = h.shape
    fs = wg.shape[1]
    tm = _tile(S, ROW_TILE)
    sp = FFN_SLOTS

    def body(ins, outs, scratch):
        h_ref, wg_ref, wu_ref = ins
        g_ref, u_ref, a_ref = outs
        hv = h_ref[...]
        for s in range(sp):
            g = jnp.dot(hv, wg_ref[s], preferred_element_type=F32)
            u = jnp.dot(hv, wu_ref[s], preferred_element_type=F32)
            g_ref[s] = g.astype(g_ref.dtype)
            u_ref[s] = u.astype(u_ref.dtype)
            a_ref[s] = (g * _sigmoid(g) * u).astype(a_ref.dtype)

    w_spec = pl.BlockSpec((sp, D, fs), lambda i, j: (j, 0, 0))
    o_spec = pl.BlockSpec((sp, tm, fs), lambda i, j: (j, i, 0))
    sds = jax.ShapeDtypeStruct((N_DEV, S, fs), BF16)
    g, u, a, *rest = _pallas(body, name, (S // tm, N_DEV // sp), [h, wg.reshape(N_DEV, D, fs), wu.reshape(N_DEV, D, fs)],
                             [pl.BlockSpec((tm, D), lambda i, j: (i, 0)), w_spec, w_spec], [o_spec, o_spec, o_spec],
                             [sds, sds, sds], sem=("parallel", "arbitrary"), ride=ride)
    return [t.reshape(N_DEV * S, fs) for t in (g, u, a)] + rest


def _ffn_dact(dy, wd, g, u, name, ride=None):
    S, D = dy.shape
    fs = g.shape[1]
    tm = _tile(S, ROW_TILE)
    sp = FFN_SLOTS

    def body(ins, outs, scratch):
        dy_ref, wd_ref, g_ref, u_ref = ins
        dg_ref, du_ref = outs
        dy = dy_ref[...]
        for s in range(sp):
            da = lax.dot_general(dy, wd_ref[s], _DIMS["nt"], preferred_element_type=F32)
            g, u = g_ref[s].astype(F32), u_ref[s].astype(F32)
            sig = _sigmoid(g)
            dg_ref[s] = (da * u * (sig * (1.0 + g * (1.0 - sig)))).astype(dg_ref.dtype)
            du_ref[s] = (da * (g * sig)).astype(du_ref.dtype)

    o_spec = pl.BlockSpec((sp, tm, fs), lambda i, j: (j, i, 0))
    sds = jax.ShapeDtypeStruct((N_DEV, S, fs), BF16)
    dg, du, *rest = _pallas(body, name, (S // tm, N_DEV // sp),
                            [dy, wd.reshape(N_DEV, fs, D), g.reshape(N_DEV, S, fs), u.reshape(N_DEV, S, fs)],
                            [pl.BlockSpec((tm, D), lambda i, j: (i, 0)), pl.BlockSpec((sp, fs, D), lambda i, j: (j, 0, 0)),
                             o_spec, o_spec], [o_spec, o_spec], [sds, sds], sem=("parallel", "arbitrary"), ride=ride)
    return [dg.reshape(N_DEV * S, fs), du.reshape(N_DEV * S, fs)] + rest


def _ffn_down(a, wd, x, name):
    S, D = x.shape
    fs = a.shape[1]
    tm = _tile(S, ROW_TILE)
    sp = FFN_SLOTS
    nj = N_DEV // sp

    def body(a_ref, w_ref, x_ref, o_ref, acc_ref):
        j = pl.program_id(1)
        part = jnp.dot(a_ref[0], w_ref[0], preferred_element_type=F32)
        for s in range(1, sp):
            part += jnp.dot(a_ref[s], w_ref[s], preferred_element_type=F32)

        @pl.when(j == 0)
        def _():
            acc_ref[...] = x_ref[...] + part

        @pl.when(j > 0)
        def _():
            acc_ref[...] += part

        @pl.when(j == nj - 1)
        def _():
            o_ref[...] = acc_ref[...]

    row = pl.BlockSpec((tm, D), lambda i, j: (i, 0))
    return _pcall(
        body, name=name, grid=(S // tm, nj),
        in_specs=[pl.BlockSpec((sp, tm, fs), lambda i, j: (j, i, 0)), pl.BlockSpec((sp, fs, D), lambda i, j: (j, 0, 0)), row],
        out_specs=row, out_shape=jax.ShapeDtypeStruct((S, D), F32),
        scratch_shapes=[pltpu.VMEM((tm, D), F32)],
        compiler_params=_params(("parallel", "arbitrary")),
    )(a.reshape(N_DEV, S, fs), wd.reshape(N_DEV, fs, D), x)


def _ffn_dh(dg, du, wg, wu, S, name, ride=None):
    fs = dg.shape[1]
    D = wg.shape[0] // N_DEV
    tm = _tile(S, ROW_TILE)
    sp = FFN_SLOTS
    nj = N_DEV // sp

    def body(ins, outs, scratch):
        dg_ref, du_ref, wg_ref, wu_ref = ins
        o_ref, acc_ref = outs[0], scratch[0]
        j = pl.program_id(1)
        part = None
        for s in range(sp):
            for d_ref, w_ref in ((dg_ref, wg_ref), (du_ref, wu_ref)):
                term = lax.dot_general(d_ref[s], w_ref[s], _DIMS["nt"], preferred_element_type=F32)
                part = term if part is None else part + term

        @pl.when(j == 0)
        def _():
            acc_ref[...] = part

        @pl.when(j > 0)
        def _():
            acc_ref[...] += part

        @pl.when(j == nj - 1)
        def _():
            o_ref[...] = acc_ref[...]

    d_spec = pl.BlockSpec((sp, tm, fs), lambda i, j: (j, i, 0))
    w_spec = pl.BlockSpec((sp, D, fs), lambda i, j: (j, 0, 0))
    return _pallas(body, name, (S // tm, nj),
                   [dg.reshape(N_DEV, S, fs), du.reshape(N_DEV, S, fs), wg.reshape(N_DEV, D, fs), wu.reshape(N_DEV, D, fs)],
                   [d_spec, d_spec, w_spec, w_spec], [pl.BlockSpec((tm, D), lambda i, j: (i, 0))],
                   [jax.ShapeDtypeStruct((S, D), F32)], scratch=[pltpu.VMEM((tm, D), F32)],
                   sem=("parallel", "arbitrary"), ride=ride)


def _slots_dw_t(name, at, d_slots, ride=None):
    K, S = at.shape
    n = d_slots.shape[1]
    tk = _tile(S, 2 * ROW_TILE)
    nkb = S // tk
    return _mm(name, "nn", at, d_slots, (K, n, tk), (N_DEV, 1, nkb),
               lambda j, q, k: (0, k), lambda j, q, k: (j * nkb + k, 0), lambda j, q, k: (j, 0),
               (N_DEV * K, n), BF16, ride=ride)


def _rope_heads(q, cos_t, sin_t):
    parts = []
    for c0 in range(0, q.shape[1], HEAD_PAD):
        parts += [q[:, c0:c0 + QK_NOPE_DIM], _rope(q[:, c0 + QK_NOPE_DIM:c0 + HEAD_PAD], cos_t, sin_t)]
    return jnp.concatenate(parts, axis=1)


def _kv_mid(kv, g, cos_t, sin_t):
    S, W = kv.shape
    R = W - 128
    ts = _tile(S, ROW_TILE)

    def body(kv_ref, g_ref, c_ref, s_ref, c_out, k_out):
        cf = kv_ref[:, :R]
        c_out[...] = (cf * _rms_scale(cf) * g_ref[...]).astype(c_out.dtype)
        k_out[...] = _rope(kv_ref[:, R:], c_ref[...], s_ref[...]).astype(k_out.dtype)

    tab = pl.BlockSpec((ts, 128), lambda i: (i, 0))
    return _pcall(
        body, name="kv_mid", grid=(S // ts,),
        in_specs=[pl.BlockSpec((ts, W), lambda i: (i, 0)), pl.BlockSpec((1, R), lambda i: (0, 0)), tab, tab],
        out_specs=(pl.BlockSpec((ts, R), lambda i: (i, 0)), tab),
        out_shape=(jax.ShapeDtypeStruct((S, R), BF16), jax.ShapeDtypeStruct((S, 128), BF16)),
        compiler_params=_params(("parallel",)),
    )(kv, g, cos_t, sin_t)


def _kv_mid_bwd(kv, g, dc, dkpe, cos_t, sin_t):
    S, W = kv.shape
    R = W - 128
    H = dkpe.shape[0]
    ts = _row_tile(S, H * 128)

    def body(kv_ref, g_ref, dc_ref, dk_ref, c_ref, s_ref, dkv_ref, dg_ref):
        dx, dgt = _rms_bwd(kv_ref[:, :R], g_ref[...], dc_ref[...])
        dkv_ref[:, :R] = dx.astype(dkv_ref.dtype)
        dk = dk_ref[0]
        for h in range(1, H):
            dk = dk + dk_ref[h]
        dkv_ref[:, R:] = _rope(dk, c_ref[...], -s_ref[...]).astype(dkv_ref.dtype)

        @pl.when(pl.program_id(0) == 0)
        def _():
            dg_ref[...] = jnp.zeros_like(dg_ref)

        dg_ref[...] += jnp.sum(dgt, axis=0, keepdims=True)

    tab = pl.BlockSpec((ts, 128), lambda i: (i, 0))
    vec = pl.BlockSpec((1, R), lambda i: (0, 0))
    return _pcall(
        body, name="kv_mid_bwd", grid=(S // ts,),
        in_specs=[pl.BlockSpec((ts, W), lambda i: (i, 0)), vec, pl.BlockSpec((ts, R), lambda i: (i, 0)),
                  pl.BlockSpec((H, ts, 128), lambda i: (0, i, 0)), tab, tab],
        out_specs=(pl.BlockSpec((ts, W), lambda i: (i, 0)), vec),
        out_shape=(jax.ShapeDtypeStruct((S, W), BF16), jax.ShapeDtypeStruct((1, R), F32)),
        compiler_params=_params(("arbitrary",)),
    )(kv, g, dc, dkpe, cos_t, sin_t)


def _block_pairs(nb, by_key):
    pairs = ([(qi, ki) for ki in range(nb) for qi in range(ki, nb)] if by_key
             else [(qi, ki) for qi in range(nb) for ki in range(qi + 1)])
    return jnp.asarray(np.array([p[0] for p in pairs], np.int32)), jnp.asarray(np.array([p[1] for p in pairs], np.int32))


def _causal_mask(blk, transposed=False, rows=None, row0=0):
    shape = (blk if rows is None else rows, blk)
    r = lax.broadcasted_iota(jnp.int32, shape, 0) + row0
    c = lax.broadcasted_iota(jnp.int32, shape, 1)
    return (r <= c) if transposed else (c <= r)


def _attn_specs(S, blk, hpd):
    nb = S // blk
    w = hpd * HEAD_PAD
    return dict(
        q=pl.BlockSpec((blk, w), lambda j, t, qt, kt: (j * nb + qt[t], 0)),
        kv=pl.BlockSpec((blk, w), lambda j, t, qt, kt: (j * nb + kt[t], 0)),
        kp=pl.BlockSpec((blk, 128), lambda j, t, qt, kt: (kt[t], 0)),
        do=pl.BlockSpec((blk, hpd * V_DIM), lambda j, t, qt, kt: (qt[t], j)),
        col=pl.BlockSpec((hpd, blk, 1), lambda j, t, qt, kt: (j, qt[t], 0)),
        row=pl.BlockSpec((hpd, 1, blk), lambda j, t, qt, kt: (j, 0, qt[t])),
        tab=pl.BlockSpec((blk, 128), lambda j, t, qt, kt: (qt[t], 0)))


def _head_k(kv_ref, kp, hh):
    return jnp.concatenate([kv_ref[:, hh * HEAD_PAD:hh * HEAD_PAD + QK_NOPE_DIM], kp], axis=1)


def _head_v(kv_ref, hh):
    return kv_ref[:, hh * HEAD_PAD + QK_NOPE_DIM:(hh + 1) * HEAD_PAD]


def _attn_fwd(q, kvu, kpe, H, ride=None):
    S = kpe.shape[0]
    hpd = H // N_DEV
    blk = _tile(S, ATT_BLOCK)
    q_tab, k_tab = _block_pairs(S // blk, by_key=False)
    rc = min(blk, ATT_ROW_CHUNK)

    def body(ins, outs, scratch):
        qt, kt, q_ref, kv_ref, kp_ref = ins
        o_ref, lse_ref = outs
        m_ref, acc_ref = scratch
        t = pl.program_id(1)
        qi, ki = qt[t], kt[t]

        @pl.when(ki == 0)
        def _():
            m_ref[...] = jnp.full_like(m_ref, -jnp.inf)
            acc_ref[...] = jnp.zeros_like(acc_ref)

        def step(masked):
            kp = kp_ref[...]
            ones = jnp.ones((blk, 128), BF16)
            for hh in range(hpd):
                k = _head_k(kv_ref, kp, hh)
                v_ext = jnp.concatenate([_head_v(kv_ref, hh), ones], axis=1)
                for r0 in range(0, blk, rc):
                    rows = slice(r0, r0 + rc)
                    s = lax.dot_general(q_ref[rows, hh * HEAD_PAD:(hh + 1) * HEAD_PAD], k, _DIMS["nt"],
                                        preferred_element_type=F32)
                    if masked:
                        s = jnp.where(_causal_mask(blk, rows=rc, row0=r0), s, -jnp.inf)
                    m_old = m_ref[hh, rows]
                    m_new = jnp.maximum(m_old, jnp.max(s, axis=-1, keepdims=True))
                    p = jnp.exp2((s - m_new) * EXP2_SCALE).astype(BF16)
                    acc_ref[hh, rows] = (jnp.exp2((m_old - m_new) * EXP2_SCALE) * acc_ref[hh, rows]
                                         + jnp.dot(p, v_ext, preferred_element_type=F32))
                    m_ref[hh, rows] = m_new

        @pl.when(ki < qi)
        def _():
            step(False)

        @pl.when(ki == qi)
        def _():
            step(True)
            for hh in range(hpd):
                l = acc_ref[hh, :, V_DIM:]
                o_ref[:, hh * V_DIM:(hh + 1) * V_DIM] = (acc_ref[hh, :, :V_DIM] / l).astype(o_ref.dtype)
                lse_ref[hh] = m_ref[hh] * EXP2_SCALE + jnp.log2(l[:, :1])

    sp = _attn_specs(S, blk, hpd)
    return _pallas(body, "attn_fwd", (N_DEV, q_tab.shape[0]), [q, kvu, kpe], [sp["q"], sp["kv"], sp["kp"]],
                   [sp["do"], sp["col"]],
                   [jax.ShapeDtypeStruct((S, H * V_DIM), BF16), jax.ShapeDtypeStruct((H, S, 1), F32)],
                   scratch=[pltpu.VMEM((hpd, blk, 1), F32), pltpu.VMEM((hpd, blk, 2 * V_DIM), F32)],
                   sem=("parallel", "arbitrary"), ride=ride, prefetch=[q_tab, k_tab])


def _attn_bwd(q, kvu, kpe, do, o, lse_row, cos_t, sin_t, H):
    S = kpe.shape[0]
    hpd = H // N_DEV
    blk = _tile(S, ATT_BLOCK)
    nb = S // blk
    q_tab, k_tab = _block_pairs(nb, by_key=True)

    def body(qt, kt, q_ref, kv_ref, kp_ref, do_ref, o_ref, lse_ref, c_ref, s_ref, dq_ref, dkv_ref, dkp_ref,
             dq_acc, dk_acc, dv_acc):
        t = pl.program_id(1)
        qi, ki = qt[t], kt[t]
        q_rows = pl.ds(pl.multiple_of(qi * blk, blk), blk)

        def step(diag):
            kp = kp_ref[...]
            ones = jnp.ones((8, V_DIM), BF16)
            for hh in range(hpd):
                k = _head_k(kv_ref, kp, hh)
                qv = q_ref[:, hh * HEAD_PAD:(hh + 1) * HEAD_PAD]
                cols = slice(hh * V_DIM, (hh + 1) * V_DIM)
                dov = do_ref[:, cols]
                doo = (dov.astype(F32) * o_ref[:, cols].astype(F32)).astype(BF16)
                delta = lax.dot_general(ones, doo, _DIMS["nt"], preferred_element_type=F32)[:1]
                st = lax.dot_general(k, qv, _DIMS["nt"], preferred_element_type=F32)
                pt = jnp.exp2(st * EXP2_SCALE - lse_ref[hh])
                if diag:
                    pt = jnp.where(_causal_mask(blk, transposed=True), pt, 0.0)
                dv_new = jnp.dot(pt.astype(BF16), dov, preferred_element_type=F32)
                dpt = lax.dot_general(_head_v(kv_ref, hh), dov, _DIMS["nt"], preferred_element_type=F32)
                dst = (pt * (dpt - delta)).astype(BF16)
                dk_new = jnp.dot(dst, qv, preferred_element_type=F32)
                dq_new = lax.dot_general(dst, k, _DIMS["tn"], preferred_element_type=F32)
                if diag:
                    dv_acc[hh] = dv_new
                    dk_acc[hh] = dk_new
                else:
                    dv_acc[hh] += dv_new
                    dk_acc[hh] += dk_new

                @pl.when(ki == 0)
                def _():
                    dq_acc[hh, q_rows] = dq_new

                @pl.when(ki > 0)
                def _():
                    dq_acc[hh, q_rows] += dq_new

        @pl.when(qi == ki)
        def _():
            step(True)
            for hh in range(hpd):
                c0 = hh * HEAD_PAD
                dq = dq_acc[hh, q_rows] * ATTN_SCALE
                dq_ref[:, c0:c0 + QK_NOPE_DIM] = dq[:, :QK_NOPE_DIM].astype(dq_ref.dtype)
                dq_ref[:, c0 + QK_NOPE_DIM:c0 + HEAD_PAD] = _rope(dq[:, QK_NOPE_DIM:], c_ref[...],
                                                                  -s_ref[...]).astype(dq_ref.dtype)

        @pl.when(qi > ki)
        def _():
            step(False)

        @pl.when(qi == nb - 1)
        def _():
            for hh in range(hpd):
                c0 = hh * HEAD_PAD
                dkv_ref[:, c0:c0 + QK_NOPE_DIM] = (dk_acc[hh, :, :QK_NOPE_DIM] * ATTN_SCALE).astype(dkv_ref.dtype)
                dkv_ref[:, c0 + QK_NOPE_DIM:c0 + HEAD_PAD] = dv_acc[hh].astype(dkv_ref.dtype)
                dkp_ref[hh] = dk_acc[hh, :, QK_NOPE_DIM:] * ATTN_SCALE

    sp = _attn_specs(S, blk, hpd)
    key_tab = pl.BlockSpec((blk, 128), lambda j, t, qt, kt: (kt[t], 0))
    grid_spec = pltpu.PrefetchScalarGridSpec(
        num_scalar_prefetch=2, grid=(N_DEV, q_tab.shape[0]),
        in_specs=[sp["q"], sp["kv"], sp["kp"], sp["do"], sp["do"], sp["row"], key_tab, key_tab],
        out_specs=(sp["kv"], sp["kv"], pl.BlockSpec((hpd, blk, 128), lambda j, t, qt, kt: (j, kt[t], 0))),
        scratch_shapes=[pltpu.VMEM((hpd, S, HEAD_PAD), F32), pltpu.VMEM((hpd, blk, HEAD_PAD), F32),
                        pltpu.VMEM((hpd, blk, V_DIM), F32)])
    return _pcall(
        body, name="attn_bwd", grid_spec=grid_spec,
        out_shape=(jax.ShapeDtypeStruct(q.shape, BF16), jax.ShapeDtypeStruct(kvu.shape, BF16),
                   jax.ShapeDtypeStruct((H, S, 128), F32)),
        compiler_params=_params(("parallel", "arbitrary")),
    )(q_tab, k_tab, q, kvu, kpe, do, o, lse_row, cos_t, sin_t)


def _mesh_pos():
    return lax.axis_index("x"), lax.axis_index("y"), lax.axis_index("c")


def _flip(pos, k):
    x, y, c = pos
    return (1 - x if k & 4 else x, 1 - y if k & 2 else y, 1 - c if k & 1 else c)


def _rank(pos):
    return 4 * pos[0] + 2 * pos[1] + pos[2]


def _copies(n, src_of, dst_refs, local_src_of, send_sems, recv_sems, local_sems):
    me = _mesh_pos()
    local = [pltpu.make_async_copy(local_src_of(a), dst_refs[a].at[_rank(me)], local_sems.at[a]) for a in range(n)]
    sends, arrivals = [], []
    for k in range(1, N_DEV):
        peer = _flip(me, k)
        for a in range(n):
            idx = a * (N_DEV - 1) + k - 1
            for into, group in ((me, sends), (peer, arrivals)):
                group.append(pltpu.make_async_remote_copy(
                    src_ref=src_of(a, peer), dst_ref=dst_refs[a].at[_rank(into)],
                    send_sem=send_sems.at[idx], recv_sem=recv_sems.at[idx],
                    device_id=peer, device_id_type=pl.DeviceIdType.MESH))
    return local, sends, arrivals


def _exchange_start(*args):
    local, sends, _ = _copies(*args)
    for cp in local + sends:
        cp.start()


def _exchange_wait(*args):
    local, sends, arrivals = _copies(*args)
    for cp in arrivals:
        cp.wait_recv()
    for cp in sends:
        cp.wait_send()
    for cp in local:
        cp.wait()


def _exchange(*args):
    _exchange_start(*args)
    _exchange_wait(*args)


def _sems(n):
    return [pltpu.SemaphoreType.DMA((n * (N_DEV - 1),)), pltpu.SemaphoreType.DMA((n * (N_DEV - 1),)),
            pltpu.SemaphoreType.DMA((n,))]


_ANY = pl.BlockSpec(memory_space=pl.ANY)


class _Ride:
    def __init__(self, kind, arrays):
        self.kind, self.arrays, self.n = kind, list(arrays), len(arrays)

    def out_shape(self):
        lead = (N_DEV,) if self.kind == "gather" else ()
        return [jax.ShapeDtypeStruct(lead + a.shape, a.dtype) for a in self.arrays]

    def _args(self, x_refs, out_refs, sems):
        if self.kind == "gather":
            return (self.n, lambda a, peer: x_refs[a], out_refs, lambda a: x_refs[a]) + tuple(sems)
        me = _mesh_pos()
        return (self.n, lambda a, peer: x_refs[a].at[_rank(peer)], out_refs, lambda a: x_refs[a].at[_rank(me)]) + tuple(sems)

    def start(self, x_refs, out_refs, sems):
        _exchange_start(*self._args(x_refs, out_refs, sems))

    def wait(self, x_refs, out_refs, sems):
        _exchange_wait(*self._args(x_refs, out_refs, sems))


def _pallas(body, name, grid, operands, in_specs, out_specs, out_shape, scratch=(), sem=None, ride=None, prefetch=()):
    n_pf, n_in, n_out, n_scr = len(prefetch), len(in_specs), len(out_specs), len(scratch)
    nr = ride.n if ride else 0

    def wrapped(*refs):
        refs = list(refs)
        pf, refs = refs[:n_pf], refs[n_pf:]
        ins, r_in = refs[:n_in], refs[n_in:n_in + nr]
        outs, r_out = refs[n_in + nr:n_in + nr + n_out], refs[n_in + nr + n_out:n_in + 2 * nr + n_out]
        rest = refs[n_in + 2 * nr + n_out:]
        scr, sems = rest[:n_scr], rest[n_scr:]
        if ride:
            first, last = None, None
            for d, size in enumerate(grid):
                i = pl.program_id(d)
                first = (i == 0) if first is None else jnp.logical_and(first, i == 0)
                last = (i == size - 1) if last is None else jnp.logical_and(last, i == size - 1)

            @pl.when(first)
            def _():
                ride.start(r_in, r_out, sems)

        body(pf + ins, outs, scr)
        if ride:
            @pl.when(last)
            def _():
                ride.wait(r_in, r_out, sems)

    grid_spec = pltpu.PrefetchScalarGridSpec(
        num_scalar_prefetch=n_pf, grid=grid,
        in_specs=list(in_specs) + [_ANY] * nr, out_specs=tuple(list(out_specs) + [_ANY] * nr),
        scratch_shapes=list(scratch) + (_sems(nr) if ride else []))
    sem = tuple(sem) if sem is not None and not ride else ("arbitrary",) * len(grid)
    return list(_pcall(
        wrapped, name=name, grid_spec=grid_spec,
        out_shape=tuple(list(out_shape) + (ride.out_shape() if ride else [])),
        compiler_params=_params(sem),
    )(*(list(prefetch) + list(operands) + (ride.arrays if ride else []))))


def _all_gather(shards, name):
    n = len(shards)

    def body(*refs):
        x_refs, out_refs = refs[:n], refs[n:2 * n]
        send_sems, recv_sems, local_sems = refs[2 * n:]
        x, y, c = _mesh_pos()
        me, sibling = (x, y, c), (x, y, 1 - c)
        chips = [(1 - x, y), (x, 1 - y), (1 - x, 1 - y)]

        def copy(a, k, block, to, src=None):
            rows = out_refs[a].at[_rank(block)]
            return pltpu.make_async_remote_copy(
                src_ref=rows if src is None else src, dst_ref=rows,
                send_sem=send_sems.at[a * (N_DEV - 1) + k], recv_sem=recv_sems.at[a * (N_DEV - 1) + k],
                device_id=to, device_id_type=pl.DeviceIdType.MESH)

        mine = [pltpu.make_async_copy(x_refs[a], out_refs[a].at[_rank(me)], local_sems.at[a]) for a in range(n)]
        first = []
        for a in range(n):
            first.append(copy(a, 0, me, sibling, src=x_refs[a]))
            first += [copy(a, 1 + j, me, (*chip, c), src=x_refs[a]) for j, chip in enumerate(chips)]
        for cp in mine + first:
            cp.start()
        passed = []
        for j, chip in enumerate(chips):
            for a in range(n):
                copy(a, 1 + j, (*chip, c), me).wait_recv()
                passed.append(copy(a, 4 + j, (*chip, c), sibling))
                passed[-1].start()
        for a in range(n):
            copy(a, 0, sibling, me).wait_recv()
            for j, chip in enumerate(chips):
                copy(a, 4 + j, (*chip, 1 - c), me).wait_recv()
        for cp in first + passed:
            cp.wait_send()
        for cp in mine:
            cp.wait()

    return _pcall(
        body, name=name, in_specs=[_ANY] * n, out_specs=tuple([_ANY] * n),
        out_shape=tuple(jax.ShapeDtypeStruct((N_DEV,) + s.shape, s.dtype) for s in shards),
        scratch_shapes=_sems(n),
    )(*shards)


def _all_to_all(parts, name):
    n = len(parts)

    def body(*refs):
        x_refs, out_refs = refs[:n], refs[n:2 * n]
        me = _mesh_pos()
        _exchange(n, lambda a, peer: x_refs[a].at[_rank(peer)], out_refs, lambda a: x_refs[a].at[_rank(me)],
                  *refs[2 * n:])

    return _pcall(
        body, name=name, in_specs=[_ANY] * n, out_specs=tuple([_ANY] * n),
        out_shape=tuple(jax.ShapeDtypeStruct(p.shape, p.dtype) for p in parts),
        scratch_shapes=_sems(n),
    )(*parts)


def _all_reduce_small(v):
    R, C = v.shape

    def body(x_ref, out_ref, buf_ref, send_sems, recv_sems, local_sems):
        _exchange(1, lambda a, peer: x_ref, [buf_ref], lambda a: x_ref, send_sems, recv_sems, local_sems)
        total = buf_ref[0]
        for j in range(1, N_DEV):
            total = total + buf_ref[j]
        out_ref[...] = total

    vm = pl.BlockSpec(memory_space=pltpu.VMEM)
    return _pcall(
        body, name="all_reduce_small", in_specs=[vm], out_specs=vm,
        out_shape=jax.ShapeDtypeStruct((R, C), F32),
        scratch_shapes=[pltpu.VMEM((N_DEV, R, C), F32)] + _sems(1),
    )(v)


def _sum_slots(layers, name):
    L = len(layers)
    shape = layers[0].shape[1:]
    C = shape[-1]
    R = layers[0].size // (N_DEV * C)
    tr = R
    while N_DEV * tr * C * 4 > (4 << 20) and tr % 32 == 0:
        tr //= 2

    def body(*refs):
        o_ref = refs[L]
        for li in range(L):
            @pl.when(pl.program_id(0) == li)
            def _():
                total = refs[li][0].astype(F32)
                for j in range(1, N_DEV):
                    total = total + refs[li][j].astype(F32)
                o_ref[0] = total

    in_specs = [pl.BlockSpec((N_DEV, tr, C), lambda l, i, li=li: (0, jnp.where(l == li, i, 0), 0)) for li in range(L)]
    return _pcall(
        body, name=name, grid=(L, R // tr),
        in_specs=in_specs,
        out_specs=pl.BlockSpec((1, tr, C), lambda l, i: (l, i, 0)),
        out_shape=jax.ShapeDtypeStruct((L, R, C), F32),
        compiler_params=_params(("parallel", "parallel")),
    )(*[p.reshape(N_DEV, R, C) for p in layers]).reshape((L,) + shape)


def _adamw(w, g, m, v, name):
    shape = w.shape
    C = shape[-1]
    R = w.size // C
    tr = R
    while tr * C * 4 > (1 << 20) and tr % 16 == 0:
        tr //= 2

    def body(w_ref, g_ref, m_ref, v_ref, d_ref, nm_ref, nv_ref):
        gv = g_ref[...]
        nm = ADAM_B1 * m_ref[...] + (1.0 - ADAM_B1) * gv
        nv = ADAM_B2 * v_ref[...] + (1.0 - ADAM_B2) * (gv * gv)
        m_hat = nm / (1.0 - ADAM_B1 ** ADAM_STEP)
        v_hat = nv / (1.0 - ADAM_B2 ** ADAM_STEP)
        d_ref[...] = -ADAM_LR * (m_hat / (jnp.sqrt(v_hat) + ADAM_EPS) + ADAM_WD * w_ref[...])
        nm_ref[...] = nm
        nv_ref[...] = nv

    blk = pl.BlockSpec((tr, C), lambda i: (i, 0))
    sds = jax.ShapeDtypeStruct((R, C), F32)
    outs = _pcall(
        body, name=name, grid=(R // tr,),
        in_specs=[blk] * 4, out_specs=(blk,) * 3, out_shape=(sds,) * 3,
        compiler_params=_params(("parallel",)),
    )(*(t.reshape(R, C) for t in (w, g, m, v)))
    return tuple(o.reshape(shape) for o in outs)


_REPLICATED = ("kv_in_norm", "kv_latent_norm", "attn_norm", "q_latent_norm", "ffn_norm", "final_norm")
_WEIGHTS = ("pool_norm", "pool_w", "pool_scale", "kv_in_norm", "w_kv_a", "kv_latent_norm", "w_kv_b", "attn_norm",
            "w_q_a", "q_latent_norm", "w_q_b", "w_o", "ffn_norm", "w_gate", "w_up", "w_down", "final_norm")


def _ffn_fwd(x, norm_g, wg, wu, wd, tag, gather=None):
    S, D = x.shape
    fs = wg.shape[1]
    h, ht = _rmsnorm(x, norm_g, "ffn_norm" + tag, transposed_too=True)
    g, u, a, *gathered = _ffn_up(h, wg, wu, "ffn_up" + tag, ride=gather)
    if gather is not None:
        wd = gathered[0].reshape(N_DEV * fs, D)
    y = _ffn_down(a, wd, x, "ffn_down" + tag)
    return y, (ht, g, u, a), wd, gathered


def _ffn_bwd(x, norm_g, wg, wu, wd, saved, dy, dyb, tag, with_bf16, scatter=None):
    S, D = x.shape
    ht, g, u, a = saved
    fs = g.shape[1]
    tk = _tile(S, 2 * ROW_TILE)
    nkb = S // tk
    dg, du, *got_rider = _ffn_dact(dyb, wd, g, u, "ffn_dact" + tag, ride=scatter)
    dwd = _mm("ffn_dwd" + tag, "tn", a, dyb, (fs, D, tk), (N_DEV, 1, nkb),
              lambda j, q, k: (j * nkb + k, 0), lambda j, q, k: (k, 0), lambda j, q, k: (j, 0), (N_DEV * fs, D), BF16)
    dwg, (got_dwd,) = _slots_dw_t("ffn_dwg" + tag, ht, dg, ride=_Ride("scatter", [dwd.reshape(N_DEV, fs, D)]))
    dwu, (got_dwg,) = _slots_dw_t("ffn_dwu" + tag, ht, du, ride=_Ride("scatter", [dwg.reshape(N_DEV, D, fs)]))
    dh, got_dwu = _ffn_dh(dg, du, wg, wu, S, "ffn_dh" + tag, ride=_Ride("scatter", [dwu.reshape(N_DEV, D, fs)]))
    outs = _rmsnorm_bwd(x, norm_g, dh, dy, "ffn_norm_bwd" + tag, with_bf16=with_bf16)
    return outs, got_rider, (got_dwg, got_dwu, got_dwd)


def kernel(x, positions, pool_norm, pool_w, pool_scale, kv_in_norm, w_kv_a, kv_latent_norm, w_kv_b, attn_norm, w_q_a, q_latent_norm, w_q_b, w_o, ffn_norm, w_gate, w_up, w_down, final_norm, loss_target, m_pool_norm, m_pool_w, m_pool_scale, m_kv_in_norm, m_w_kv_a, m_kv_latent_norm, m_w_kv_b, m_attn_norm, m_w_q_a, m_q_latent_norm, m_w_q_b, m_w_o, m_ffn_norm, m_w_gate, m_w_up, m_w_down, m_final_norm, v_pool_norm, v_pool_w, v_pool_scale, v_kv_in_norm, v_w_kv_a, v_kv_latent_norm, v_w_kv_b, v_attn_norm, v_w_q_a, v_q_latent_norm, v_w_q_b, v_w_o, v_ffn_norm, v_w_gate, v_w_up, v_w_down, v_final_norm):
    env = dict(locals())
    weights = {n: env[n] for n in _WEIGHTS}
    m_in = {n: env["m_" + n] for n in _WEIGHTS}
    v_in = {n: env["v_" + n] for n in _WEIGHTS}

    S, D = x.shape[1], x.shape[2]
    xs = x.reshape(S, D)
    target = loss_target.reshape(S, D)
    G, gws, gw = pool_w.shape[1:]
    Ds = w_kv_a.shape[0]
    R_kv, kvw = w_kv_b.shape
    hpd = kvw // (QK_NOPE_DIM + V_DIM)
    H = hpd * N_DEV
    R_q = w_q_a.shape[2]
    fs = w_gate.shape[2]
    bf = lambda t: t.astype(BF16)

    gains = jnp.concatenate([pool_norm, pool_scale], axis=0)
    g_pool_w, g_gains, g_wg0, g_wu0 = _all_gather([bf(pool_w[0]), gains, bf(w_gate[0]), bf(w_up[0])], "gather_first")
    wqb_pad = jnp.pad(w_q_b.reshape(R_q, hpd, QK_DIM), ((0, 0), (0, 0), (0, HEAD_PAD - QK_DIM))).reshape(R_q, hpd * HEAD_PAD)
    gather_mla = _Ride("gather", [bf(w_down[0]), bf(jnp.pad(w_kv_a, ((0, 0), (0, 128 - QK_ROPE_DIM)))), bf(w_kv_b),
                                  bf(w_q_a[0]), bf(wqb_pad), bf(w_o[0])])
    gather_ffn1 = _Ride("gather", [bf(w_gate[1]), bf(w_up[1]), bf(w_down[1])])

    wp = jnp.swapaxes(g_pool_w, 0, 1).reshape(G, gw, gw)
    gains_full = jnp.swapaxes(g_gains, 0, 1).reshape(2, D)
    g_pool_norm, g_pool_scale = gains_full[0:1], gains_full[1:2]
    wg0, wu0 = g_wg0.reshape(N_DEV * D, fs), g_wu0.reshape(N_DEV * D, fs)

    g_kv_in = kv_in_norm.reshape(1, D)
    g_kv_lat = kv_latent_norm.reshape(1, R_kv)
    g_attn = attn_norm.reshape(1, D)
    g_q_lat = q_latent_norm.reshape(1, R_q)
    g_final = final_norm.reshape(1, D)

    half = QK_ROPE_DIM // 2
    inv_freq = ROPE_BASE ** (-jnp.arange(half, dtype=F32) / half)
    ang = positions.reshape(S).astype(F32)[:, None] * inv_freq
    cos, sin = jnp.cos(ang), jnp.sin(ang)
    zeros = jnp.zeros((S, 128 - QK_ROPE_DIM), F32)
    cos_t = jnp.concatenate([cos, cos, zeros], axis=1)
    sin_t = jnp.concatenate([-sin, sin, zeros], axis=1)

    diff = _pool_pre(xs, g_pool_norm)
    x1 = _pool_mix(diff, wp, g_pool_scale, xs)
    x2, ffn0, wd0, (_, g_wkva, g_wkvb, g_wqa, g_wqb, g_wo) = _ffn_fwd(x1, ffn_norm[0:1], wg0, wu0, None, "0", gather_mla)
    wkva = g_wkva.reshape(D, R_kv + 128)
    wkvb = g_wkvb.reshape(N_DEV * R_kv, kvw)
    wqa = g_wqa.reshape(D, R_q)
    wqb = g_wqb.reshape(N_DEV * R_q, hpd * HEAD_PAD)
    wo = g_wo.reshape(H * V_DIM, D)

    hk = _rmsnorm(x2, g_kv_in, "kv_in_norm")
    kv = _mm_plain("kv_a", "nn", hk, wkva)
    ckv, kpe = _kv_mid(kv, g_kv_lat, cos_t, sin_t)
    kvu = _small_slots_out("kv_b", ckv, wkvb, BF16)

    ha = _rmsnorm(x2, g_attn, "attn_norm")
    ql = _mm_plain("q_a", "nn", ha, wqa)
    qn = _rmsnorm(ql, g_q_lat, "q_latent_norm")
    tab = pl.BlockSpec((_tile(S, ROW_TILE), 128), lambda i: (i, 0))
    qr = _small_slots_out("q_b", qn, wqb, BF16, epilogue=(_rope_heads, [cos_t, sin_t], [tab, tab]))
    o, lse, g_wg1, g_wu1, g_wd1 = _attn_fwd(qr, kvu, kpe, H, ride=gather_ffn1)
    wg1, wu1, wd1 = g_wg1.reshape(N_DEV * D, fs), g_wu1.reshape(N_DEV * D, fs), g_wd1.reshape(N_DEV * fs, D)
    x3 = _mm_plain("attn_out", "nn", o, wo, res=x2)
    x4, ffn1, _, _ = _ffn_fwd(x3, ffn_norm[1:2], wg1, wu1, wd1, "1")

    loss_part, dx4, dx4b, d_final = _loss_head(x4, g_final, target)

    (dx3, dx3b, d_ffn1), _, got_ffn1 = _ffn_bwd(x3, ffn_norm[1:2], wg1, wu1, wd1, ffn1, dx4, dx4b, "1", True)

    do = _mm_plain("attn_out_dx", "nt", dx3b, wo, out_dtype=BF16)
    dwo = _mm_plain("attn_out_dw", "tn", o, dx3b, out_dtype=BF16)
    dq, dkvu, dkpe = _attn_bwd(qr, kvu, kpe, do, o, lse.reshape(H, 1, S), cos_t, sin_t, H)

    dqn = _small_slots_in_nt("q_b_dx", dq, wqb, S)
    dwqb = _small_slots_dw("q_b_dw", qn, dq, BF16)
    dql, d_q_lat = _rmsnorm_bwd(ql, g_q_lat, dqn, None, "q_latent_norm_bwd", out_dtype=BF16)
    dha = _mm_plain("q_a_dx", "nt", dql, wqa)
    dwqa = _mm_plain("q_a_dw", "tn", ha, dql, out_dtype=BF16)
    dx2, d_attn = _rmsnorm_bwd(x2, g_attn, dha, dx3, "attn_norm_bwd")

    dckv = _small_slots_in_nt("kv_b_dx", dkvu, wkvb, S)
    dwkvb = _small_slots_dw("kv_b_dw", ckv, dkvu, BF16)
    dkv, d_kv_lat = _kv_mid_bwd(kv, g_kv_lat, dckv, dkpe, cos_t, sin_t)
    dhk = _mm_plain("kv_a_dx", "nt", dkv, wkva)
    dwkva = _mm_plain("kv_a_dw", "tn", hk, dkv, out_dtype=BF16)
    dx2, dx2b, d_kv_in = _rmsnorm_bwd(x2, g_kv_in, dhk, dx2, "kv_in_norm_bwd", with_bf16=True)

    scatter_mla = _Ride("scatter", [dwkva.reshape(N_DEV, Ds, R_kv + 128), dwkvb.reshape(N_DEV, R_kv, kvw),
                                    dwqa.reshape(N_DEV, Ds, R_q), dwqb.reshape(N_DEV, R_q, hpd * HEAD_PAD),
                                    dwo.reshape(N_DEV, H * V_DIM // N_DEV, D)])
    (dx1, d_ffn0), got_mla, got_ffn0 = _ffn_bwd(x1, ffn_norm[0:1], wg0, wu0, wd0, ffn0, dx2, dx2b, "0", False,
                                                  scatter=scatter_mla)

    ddiff, dwp, d_pool_scale = _pool_mix_bwd(diff, wp, g_pool_scale, dx1)
    grad_x, d_pool_norm = _pool_pre_bwd(xs, g_pool_norm, ddiff, dx1)

    d_gains = jnp.swapaxes(jnp.concatenate([d_pool_norm, d_pool_scale], axis=0).reshape(2, N_DEV, D // N_DEV), 0, 1)
    got_pool_w, got_gains = _all_to_all([jnp.swapaxes(dwp.reshape(G, N_DEV, gws, gw), 0, 1), d_gains], "exchange_pool")

    received = {n: [r] for n, r in zip(("w_kv_a", "w_kv_b", "w_q_a", "w_q_b", "w_o"), got_mla)}
    received.update({n: [r0, r1] for n, r0, r1 in zip(("w_gate", "w_up", "w_down"), got_ffn0, got_ffn1)})
    received.update(pool_w=[got_pool_w], gains=[got_gains])
    sums = {n: _sum_slots(r, "sum_" + n) for n, r in received.items()}
    grads = {
        "pool_w": sums["pool_w"],
        "w_kv_a": sums["w_kv_a"][0, :, :R_kv + QK_ROPE_DIM],
        "w_kv_b": sums["w_kv_b"][0],
        "w_q_a": sums["w_q_a"],
        "w_q_b": sums["w_q_b"].reshape(R_q, hpd, HEAD_PAD)[:, :, :QK_DIM].reshape(1, R_q, hpd * QK_DIM),
        "w_o": sums["w_o"],
        "w_gate": sums["w_gate"],
        "w_up": sums["w_up"],
        "w_down": sums["w_down"],
        "pool_norm": sums["gains"][0, 0:1],
        "pool_scale": sums["gains"][0, 1:2],
    }

    small = {"kv_in_norm": d_kv_in, "kv_latent_norm": d_kv_lat, "attn_norm": d_attn, "q_latent_norm": d_q_lat,
             "ffn_norm": jnp.concatenate([d_ffn0, d_ffn1], axis=0), "final_norm": d_final}
    small_packed = jnp.concatenate([small[n].reshape(-1) for n in _REPLICATED] + [loss_part.reshape(-1)])
    pad = (-small_packed.shape[0]) % (8 * 128)
    reduced = _all_reduce_small(jnp.pad(small_packed, (0, pad)).reshape(-1, 128)).reshape(-1)
    off = 0
    for n in _REPLICATED:
        grads[n] = reduced[off:off + weights[n].size].reshape(weights[n].shape)
        off += weights[n].size
    loss = reduced[off]

    delta_w, new_m, new_v = {}, {}, {}
    for n in _WEIGHTS:
        delta_w[n], new_m[n], new_v[n] = _adamw(weights[n], grads[n], m_in[n], v_in[n], "adamw_" + n)

    return (loss, grad_x.reshape(x.shape), *[grads[n] for n in _WEIGHTS], *[delta_w[n] for n in _WEIGHTS],
            *[new_m[n] for n in _WEIGHTS], *[new_v[n] for n in _WEIGHTS])
```

```python
import math

import jax
import jax.numpy as jnp
import numpy as np
from jax import lax
from jax.experimental import pallas as pl
from jax.experimental.pallas import tpu as pltpu

F32 = jnp.float32
BF16 = jnp.bfloat16

N_DEV = 8
POOL_WINDOWS = (2, 4, 8, 16)
POOL_HALO = 16
QK_NOPE_DIM = 128
QK_ROPE_DIM = 64
QK_DIM = QK_NOPE_DIM + QK_ROPE_DIM
HEAD_PAD = 256
V_DIM = 128
ROPE_BASE = 10000.0
ATTN_SCALE = 1.0 / math.sqrt(QK_DIM)
EXP2_SCALE = ATTN_SCALE * math.log2(math.e)
NORM_EPS = 1e-6
ADAM_LR = 0.001
ADAM_B1 = 0.9
ADAM_B2 = 0.999
ADAM_EPS = 1e-08
ADAM_WD = 0.01
ADAM_STEP = 10

ROW_TILE = 512
ATT_BLOCK = 1024
ATT_ROW_CHUNK = 256
FFN_SLOTS = 2
MM_TN, MM_TK = 512, 512
VMEM_LIMIT = 48 * 1024 * 1024

_pcall = pl.pallas_call


def _params(sem):
    return pltpu.CompilerParams(dimension_semantics=sem, vmem_limit_bytes=VMEM_LIMIT)


def _tile(dim, default):
    if dim % default == 0:
        return default
    assert dim <= 2 * default, (dim, default)
    return dim


def _row_tile(rows, width):
    ts = _tile(rows, ROW_TILE)
    while ts * width * 4 > (2 << 20) and ts % 16 == 0:
        ts //= 2
    return ts


def _rms_scale(x):
    return lax.rsqrt(jnp.mean(x * x, axis=-1, keepdims=True) + NORM_EPS)


def _rms_bwd(x, g, dy):
    r = _rms_scale(x)
    xn = x * r
    u = dy * g
    dx = r * (u - xn * jnp.mean(u * xn, axis=-1, keepdims=True))
    return dx, dy * xn


def _swap_halves(x):
    lane = lax.broadcasted_iota(jnp.int32, x.shape, 1)
    return jnp.where(lane < QK_ROPE_DIM // 2, pltpu.roll(x, 128 - QK_ROPE_DIM // 2, 1), pltpu.roll(x, QK_ROPE_DIM // 2, 1))


def _rope(x, cos_t, sin_t):
    return x * cos_t + _swap_halves(x) * sin_t


def _rmsnorm(x, g, name, transposed_too=False):
    S, D = x.shape
    ts = _row_tile(S, D)

    def body(x_ref, g_ref, o_ref, *t_ref):
        xf = x_ref[...]
        y = xf * _rms_scale(xf) * g_ref[...]
        o_ref[...] = y.astype(o_ref.dtype)
        if transposed_too:
            t_ref[0][...] = y.T.astype(o_ref.dtype)

    row = pl.BlockSpec((ts, D), lambda i: (i, 0))
    outs = _pcall(
        body, name=name, grid=(S // ts,),
        in_specs=[row, pl.BlockSpec((1, D), lambda i: (0, 0))],
        out_specs=(row, pl.BlockSpec((D, ts), lambda i: (0, i))) if transposed_too else row,
        out_shape=((jax.ShapeDtypeStruct((S, D), BF16), jax.ShapeDtypeStruct((D, S), BF16)) if transposed_too
                   else jax.ShapeDtypeStruct((S, D), BF16)),
        compiler_params=_params(("parallel",)),
    )(x, g)
    return outs


def _rmsnorm_bwd(x, g, dy, res, name, out_dtype=F32, with_bf16=False):
    S, D = x.shape
    ts = _row_tile(S, D)
    has_res = res is not None

    def body(*refs):
        refs = list(refs)
        x_ref, g_ref, dy_ref = refs[:3]
        res_ref = refs[3] if has_res else None
        outs = refs[3 + has_res:]
        dx_ref, dg_ref = outs[0], outs[-1]
        dx, dgt = _rms_bwd(x_ref[...], g_ref[...], dy_ref[...].astype(F32))
        if has_res:
            dx = res_ref[...] + dx
        dx_ref[...] = dx.astype(dx_ref.dtype)
        if with_bf16:
            outs[1][...] = dx.astype(BF16)

        @pl.when(pl.program_id(0) == 0)
        def _():
            dg_ref[...] = jnp.zeros_like(dg_ref)

        dg_ref[...] += jnp.sum(dgt, axis=0, keepdims=True)

    row = pl.BlockSpec((ts, D), lambda i: (i, 0))
    vec = pl.BlockSpec((1, D), lambda i: (0, 0))
    out_specs = [row] + ([row] if with_bf16 else []) + [vec]
    out_shape = ([jax.ShapeDtypeStruct((S, D), out_dtype)] + ([jax.ShapeDtypeStruct((S, D), BF16)] if with_bf16 else [])
                 + [jax.ShapeDtypeStruct((1, D), F32)])
    return _pcall(
        body, name=name, grid=(S // ts,),
        in_specs=[row, vec, row] + ([row] if has_res else []),
        out_specs=tuple(out_specs), out_shape=tuple(out_shape),
        compiler_params=_params(("arbitrary",)),
    )(*([x, g, dy] + ([res] if has_res else [])))


def _rmsnorm_pair(x, g1, g2, name):
    S, D = x.shape
    ts = _row_tile(S, D)

    def body(x_ref, g1_ref, g2_ref, o1_ref, o2_ref):
        xf = x_ref[...]
        xn = xf * _rms_scale(xf)
        o1_ref[...] = (xn * g1_ref[...]).astype(o1_ref.dtype)
        o2_ref[...] = (xn * g2_ref[...]).astype(o2_ref.dtype)

    row = pl.BlockSpec((ts, D), lambda i: (i, 0))
    vec = pl.BlockSpec((1, D), lambda i: (0, 0))
    sds = jax.ShapeDtypeStruct((S, D), BF16)
    return _pcall(
        body, name=name, grid=(S // ts,),
        in_specs=[row, vec, vec], out_specs=(row, row), out_shape=(sds, sds),
        compiler_params=_params(("parallel",)),
    )(x, g1, g2)


def _rmsnorm_pair_bwd(x, g1, dy1, g2, dy2, res, name):
    S, D = x.shape
    ts = _row_tile(S, D)

    def body(x_ref, g1_ref, dy1_ref, g2_ref, dy2_ref, res_ref, dx_ref, dxb_ref, dg1_ref, dg2_ref):
        xf = x_ref[...]
        dx1, dgt1 = _rms_bwd(xf, g1_ref[...], dy1_ref[...])
        dx2, dgt2 = _rms_bwd(xf, g2_ref[...], dy2_ref[...])
        dx = (res_ref[...] + dx1) + dx2
        dx_ref[...] = dx
        dxb_ref[...] = dx.astype(BF16)

        @pl.when(pl.program_id(0) == 0)
        def _():
            dg1_ref[...] = jnp.zeros_like(dg1_ref)
            dg2_ref[...] = jnp.zeros_like(dg2_ref)

        dg1_ref[...] += jnp.sum(dgt1, axis=0, keepdims=True)
        dg2_ref[...] += jnp.sum(dgt2, axis=0, keepdims=True)

    row = pl.BlockSpec((ts, D), lambda i: (i, 0))
    vec = pl.BlockSpec((1, D), lambda i: (0, 0))
    return _pcall(
        body, name=name, grid=(S // ts,),
        in_specs=[row, vec, row, vec, row, row], out_specs=(row, row, vec, vec),
        out_shape=(jax.ShapeDtypeStruct((S, D), F32), jax.ShapeDtypeStruct((S, D), BF16),
                   jax.ShapeDtypeStruct((1, D), F32), jax.ShapeDtypeStruct((1, D), F32)),
        compiler_params=_params(("arbitrary",)),
    )(x, g1, dy1, g2, dy2, res)


def _pool_pre(x, g):
    S, D = x.shape
    ts = _row_tile(S, D)
    gw = D // len(POOL_WINDOWS)
    hb = ts // POOL_HALO

    def body(x_ref, halo_ref, g_ref, o_ref):
        i = pl.program_id(0)
        xx = jnp.concatenate([halo_ref[...], x_ref[...]], axis=0)
        h = xx * _rms_scale(xx) * g_ref[...]
        t = i * ts - POOL_HALO + lax.broadcasted_iota(jnp.int32, (ts + POOL_HALO, 1), 0)
        h = jnp.where(t >= 0, h, 0.0)
        tf = t[POOL_HALO:].astype(F32)
        for gi, w in enumerate(POOL_WINDOWS):
            hg = h[:, gi * gw:(gi + 1) * gw]
            acc, span = hg, 1
            while span < w:
                acc = acc + pltpu.roll(acc, span, 0)
                span *= 2
            count = jnp.minimum(tf + 1.0, float(w))
            diff = acc[POOL_HALO:] / count - hg[POOL_HALO:]
            o_ref[:, gi * gw:(gi + 1) * gw] = diff.astype(o_ref.dtype)

    return _pcall(
        body, name="pool_pre", grid=(S // ts,),
        in_specs=[pl.BlockSpec((ts, D), lambda i: (i, 0)),
                  pl.BlockSpec((POOL_HALO, D), lambda i: (jnp.maximum(i * hb - 1, 0), 0)),
                  pl.BlockSpec((1, D), lambda i: (0, 0))],
        out_specs=pl.BlockSpec((ts, D), lambda i: (i, 0)),
        out_shape=jax.ShapeDtypeStruct((S, D), BF16),
        compiler_params=_params(("parallel",)),
    )(x, x, g)


def _pool_pre_bwd(x, g, dd, res):
    S, D = x.shape
    ts = _row_tile(S, D)
    gw = D // len(POOL_WINDOWS)
    hb = ts // POOL_HALO
    last_halo = S // POOL_HALO - 1

    def body(x_ref, g_ref, dd_ref, halo_ref, res_ref, dx_ref, dg_ref):
        i = pl.program_id(0)
        ee = jnp.concatenate([dd_ref[...], halo_ref[...]], axis=0)
        t = i * ts + lax.broadcasted_iota(jnp.int32, (ts + POOL_HALO, 1), 0)
        ee = jnp.where(t < S, ee, 0.0)
        tf = t.astype(F32)
        n = ts + POOL_HALO
        for gi, w in enumerate(POOL_WINDOWS):
            eg = ee[:, gi * gw:(gi + 1) * gw]
            acc, span = eg / jnp.minimum(tf + 1.0, float(w)), 1
            while span < w:
                acc = acc + pltpu.roll(acc, n - span, 0)
                span *= 2
            dx_ref[:, gi * gw:(gi + 1) * gw] = acc[:ts] - eg[:ts]
        dx, dgt = _rms_bwd(x_ref[...], g_ref[...], dx_ref[...])
        dx_ref[...] = res_ref[...] + dx

        @pl.when(i == 0)
        def _():
            dg_ref[...] = jnp.zeros_like(dg_ref)

        dg_ref[...] += jnp.sum(dgt, axis=0, keepdims=True)

    row = pl.BlockSpec((ts, D), lambda i: (i, 0))
    vec = pl.BlockSpec((1, D), lambda i: (0, 0))
    return _pcall(
        body, name="pool_pre_bwd", grid=(S // ts,),
        in_specs=[row, vec, row,
                  pl.BlockSpec((POOL_HALO, D), lambda i: (jnp.minimum((i + 1) * hb, last_halo), 0)), row],
        out_specs=(row, vec),
        out_shape=(jax.ShapeDtypeStruct((S, D), F32), jax.ShapeDtypeStruct((1, D), F32)),
        compiler_params=_params(("arbitrary",)),
    )(x, g, dd, dd, res)


def _pool_mix(diff, w, scale, x):
    S, D = x.shape
    G, gw, _ = w.shape
    ts = _tile(S, ROW_TILE)

    def body(d_ref, w_ref, s_ref, x_ref, o_ref):
        mo = jnp.dot(d_ref[...], w_ref[0], preferred_element_type=F32)
        o_ref[...] = x_ref[...] + mo * s_ref[...]

    blk = pl.BlockSpec((ts, gw), lambda i, g: (i, g))
    return _pcall(
        body, name="pool_mix", grid=(S // ts, G),
        in_specs=[blk, pl.BlockSpec((1, gw, gw), lambda i, g: (g, 0, 0)), pl.BlockSpec((1, gw), lambda i, g: (0, g)), blk],
        out_specs=blk,
        out_shape=jax.ShapeDtypeStruct((S, D), F32),
        compiler_params=_params(("parallel", "parallel")),
    )(diff, w, scale, x)


def _pool_mix_bwd(diff, w, scale, dy):
    S, D = dy.shape
    G, gw, _ = w.shape
    ts = _tile(S, ROW_TILE)

    def body(d_ref, w_ref, s_ref, dy_ref, dd_ref, dw_ref, ds_ref):
        i = pl.program_id(1)
        d = d_ref[...]
        dy = dy_ref[...]
        mo = jnp.dot(d, w_ref[0], preferred_element_type=F32)
        dmo = (dy * s_ref[...]).astype(BF16)
        dd_ref[...] = lax.dot_general(dmo, w_ref[0], (((1,), (1,)), ((), ())), preferred_element_type=F32)

        @pl.when(i == 0)
        def _():
            dw_ref[...] = jnp.zeros_like(dw_ref)
            ds_ref[...] = jnp.zeros_like(ds_ref)

        dw_ref[0] += lax.dot_general(d, dmo, (((0,), (0,)), ((), ())), preferred_element_type=F32)
        ds_ref[...] += jnp.sum(dy * mo, axis=0, keepdims=True)

    blk = pl.BlockSpec((ts, gw), lambda g, i: (i, g))
    wspec = pl.BlockSpec((1, gw, gw), lambda g, i: (g, 0, 0))
    vec = pl.BlockSpec((1, gw), lambda g, i: (0, g))
    return _pcall(
        body, name="pool_mix_bwd", grid=(G, S // ts),
        in_specs=[blk, wspec, vec, blk],
        out_specs=(blk, wspec, vec),
        out_shape=(jax.ShapeDtypeStruct((S, D), F32), jax.ShapeDtypeStruct((G, gw, gw), F32),
                   jax.ShapeDtypeStruct((1, D), F32)),
        compiler_params=_params(("parallel", "arbitrary")),
    )(diff, w, scale, dy)


def _loss_head(x, g, target):
    S, D = x.shape
    ts = _row_tile(S, D)

    def body(x_ref, g_ref, t_ref, loss_ref, dx_ref, dxb_ref, dg_ref):
        xf = x_ref[...]
        gv = g_ref[...]
        err = xf * _rms_scale(xf) * gv - t_ref[...]
        dx, dgt = _rms_bwd(xf, gv, err * (1.0 / D))
        dx_ref[...] = dx
        dxb_ref[...] = dx.astype(BF16)

        @pl.when(pl.program_id(0) == 0)
        def _():
            loss_ref[...] = jnp.zeros_like(loss_ref)
            dg_ref[...] = jnp.zeros_like(dg_ref)

        part = 0.5 * jnp.sum(jnp.sum(err * err, axis=-1, keepdims=True) * (1.0 / D), axis=0, keepdims=True)
        loss_ref[...] += jnp.broadcast_to(part, loss_ref.shape)
        dg_ref[...] += jnp.sum(dgt, axis=0, keepdims=True)

    row = pl.BlockSpec((ts, D), lambda i: (i, 0))
    vec = pl.BlockSpec((1, D), lambda i: (0, 0))
    return _pcall(
        body, name="loss_head", grid=(S // ts,),
        in_specs=[row, vec, row],
        out_specs=(pl.BlockSpec((1, 128), lambda i: (0, 0)), row, row, vec),
        out_shape=(jax.ShapeDtypeStruct((1, 128), F32), jax.ShapeDtypeStruct((S, D), F32),
                   jax.ShapeDtypeStruct((S, D), BF16), jax.ShapeDtypeStruct((1, D), F32)),
        compiler_params=_params(("arbitrary",)),
    )(x, g, target)


_DIMS = {"nn": (((1,), (0,)), ((), ())), "nt": (((1,), (1,)), ((), ())), "tn": (((0,), (0,)), ((), ()))}


def _mm(name, mode, a, b, tiles, grid, a_map, b_map, o_map, out_shape, out_dtype=F32, res=None, ride=None):
    tm, tn, tk = tiles
    nk = grid[-1]
    has_res = res is not None
    dn = _DIMS[mode]

    def body(ins, outs, scratch):
        o_ref = outs[0]

        def product():
            return lax.dot_general(ins[0][...].astype(BF16), ins[1][...].astype(BF16), dn, preferred_element_type=F32)

        def finish(out):
            if has_res:
                out = ins[2][...] + out
            o_ref[...] = out.astype(o_ref.dtype)

        if nk == 1:
            finish(product())
            return
        acc_ref = scratch[0]
        k = pl.program_id(2)

        @pl.when(k == 0)
        def _():
            acc_ref[...] = jnp.zeros_like(acc_ref)

        acc_ref[...] += product()

        @pl.when(k == nk - 1)
        def _():
            finish(acc_ref[...])

    a_spec = pl.BlockSpec((tk, tm) if mode == "tn" else (tm, tk), a_map)
    b_spec = pl.BlockSpec((tn, tk) if mode == "nt" else (tk, tn), b_map)
    o_spec = pl.BlockSpec((tm, tn), o_map)
    operands, in_specs = [a, b], [a_spec, b_spec]
    if has_res:
        operands.append(res)
        in_specs.append(o_spec)
    outs = _pallas(body, name, grid, operands, in_specs, [o_spec], [jax.ShapeDtypeStruct(out_shape, out_dtype)],
                   scratch=[pltpu.VMEM((tm, tn), F32)] if nk > 1 else [],
                   sem=("parallel", "parallel", "arbitrary"), ride=ride)
    return (outs[0], outs[1:]) if ride else outs[0]


def _mm_plain(name, mode, a, b, out_dtype=F32, res=None):
    if mode == "tn":
        (K, M), N = a.shape, b.shape[1]
    elif mode == "nt":
        (M, K), N = a.shape, b.shape[0]
    else:
        (M, K), N = a.shape, b.shape[1]
    if mode == "tn":
        tm, tn, tk = _tile(M, 2 * MM_TN), _tile(N, 2 * MM_TN), _tile(K, ROW_TILE)
    else:
        tm = _tile(M, ROW_TILE)
        tk = K if K <= 4 * MM_TK else _tile(K, MM_TK)
        tn = N if N * tk * 2 <= (8 << 20) else _tile(N, MM_TN)
    a_map = (lambda i, j, k: (k, i)) if mode == "tn" else (lambda i, j, k: (i, k))
    b_map = (lambda i, j, k: (j, k)) if mode == "nt" else (lambda i, j, k: (k, j))
    return _mm(name, mode, a, b, (tm, tn, tk), (M // tm, N // tn, K // tk), a_map, b_map, lambda i, j, k: (i, j),
               (M, N), out_dtype, res)


def _small_slots_out(name, a, w_slots, out_dtype, epilogue=None):
    S, K = a.shape
    n = w_slots.shape[1]
    tm = _tile(S, ROW_TILE)
    extra = epilogue[1] if epilogue else []

    def body(*refs):
        a_ref, w_ref, o_ref = refs[0], refs[1], refs[-1]
        av = a_ref[...].astype(BF16)
        tables = [r[...] for r in refs[2:-1]]
        for j in range(N_DEV):
            out = jnp.dot(av, w_ref[j], preferred_element_type=F32)
            if epilogue:
                out = epilogue[0](out, *tables)
            o_ref[j] = out.astype(o_ref.dtype)

    return _pcall(
        body, name=name, grid=(S // tm,),
        in_specs=[pl.BlockSpec((tm, K), lambda i: (i, 0)), pl.BlockSpec((N_DEV, K, n), lambda i: (0, 0, 0))]
        + (list(epilogue[2]) if epilogue else []),
        out_specs=pl.BlockSpec((N_DEV, tm, n), lambda i: (0, i, 0)),
        out_shape=jax.ShapeDtypeStruct((N_DEV, S, n), out_dtype),
        compiler_params=_params(("parallel",)),
    )(a, w_slots.reshape(N_DEV, K, n), *extra).reshape(N_DEV * S, n)


def _small_slots_in_nt(name, a_slots, w_slots, S):
    n = a_slots.shape[1]
    K = w_slots.shape[0] // N_DEV
    tm = _tile(S, ROW_TILE)

    def body(a_ref, w_ref, o_ref):
        total = lax.dot_general(a_ref[0], w_ref[0], _DIMS["nt"], preferred_element_type=F32)
        for j in range(1, N_DEV):
            total += lax.dot_general(a_ref[j], w_ref[j], _DIMS["nt"], preferred_element_type=F32)
        o_ref[...] = total

    return _pcall(
        body, name=name, grid=(S // tm,),
        in_specs=[pl.BlockSpec((N_DEV, tm, n), lambda i: (0, i, 0)), pl.BlockSpec((N_DEV, K, n), lambda i: (0, 0, 0))],
        out_specs=pl.BlockSpec((tm, K), lambda i: (i, 0)),
        out_shape=jax.ShapeDtypeStruct((S, K), F32),
        compiler_params=_params(("parallel",)),
    )(a_slots.reshape(N_DEV, S, n), w_slots.reshape(N_DEV, K, n))


def _small_slots_dw(name, a, d_slots, out_dtype):
    S, K = a.shape
    n = d_slots.shape[1]
    tk = _tile(S, ROW_TILE)
    nkb = S // tk

    def body(a_ref, d_ref, o_ref, acc_ref):
        k = pl.program_id(0)
        at = a_ref[...].astype(F32).T.astype(BF16)

        def add(first):
            for j in range(N_DEV):
                part = jnp.dot(at, d_ref[j], preferred_element_type=F32)
                if first:
                    acc_ref[j] = part
                else:
                    acc_ref[j] += part

        @pl.when(k == 0)
        def _():
            add(True)

        @pl.when(k > 0)
        def _():
            add(False)

        @pl.when(k == nkb - 1)
        def _():
            o_ref[...] = acc_ref[...].astype(o_ref.dtype)

    return _pcall(
        body, name=name, grid=(nkb,),
        in_specs=[pl.BlockSpec((tk, K), lambda k: (k, 0)), pl.BlockSpec((N_DEV, tk, n), lambda k: (0, k, 0))],
        out_specs=pl.BlockSpec((N_DEV, K, n), lambda k: (0, 0, 0)),
        out_shape=jax.ShapeDtypeStruct((N_DEV, K, n), out_dtype),
        scratch_shapes=[pltpu.VMEM((N_DEV, K, n), F32)],
        compiler_params=_params(("arbitrary",)),
    )(a, d_slots.reshape(N_DEV, S, n)).reshape(N_DEV * K, n)


def _sigmoid(x):
    return 1.0 / (1.0 + jnp.exp(-x))


def _ffn_up(h, wg, wu, name, ride=None):
    S, D = h.shape
    fs = wg.shape[1]
    tm = _tile(S, ROW_TILE)
    sp = FFN_SLOTS

    def body(ins, outs, scratch):
        h_ref, wg_ref, wu_ref = ins
        g_ref, u_ref, a_ref = outs
        hv = h_ref[...]
        for s in range(sp):
            g = jnp.dot(hv, wg_ref[s], preferred_element_type=F32)
            u = jnp.dot(hv, wu_ref[s], preferred_element_type=F32)
            g_ref[s] = g.astype(g_ref.dtype)
            u_ref[s] = u.astype(u_ref.dtype)
            a_ref[s] = (g * _sigmoid(g) * u).astype(a_ref.dtype)

    w_spec = pl.BlockSpec((sp, D, fs), lambda i, j: (j, 0, 0))
    o_spec = pl.BlockSpec((sp, tm, fs), lambda i, j: (j, i, 0))
    sds = jax.ShapeDtypeStruct((N_DEV, S, fs), BF16)
    g, u, a, *rest = _pallas(body, name, (S // tm, N_DEV // sp), [h, wg.reshape(N_DEV, D, fs), wu.reshape(N_DEV, D, fs)],
                             [pl.BlockSpec((tm, D), lambda i, j: (i, 0)), w_spec, w_spec], [o_spec, o_spec, o_spec],
                             [sds, sds, sds], sem=("parallel", "arbitrary"), ride=ride)
    return [t.reshape(N_DEV * S, fs) for t in (g, u, a)] + rest


def _ffn_dact(dy, wd, g, u, name, ride=None):
    S, D = dy.shape
    fs = g.shape[1]
    tm = _tile(S, ROW_TILE)
    sp = FFN_SLOTS

    def body(ins, outs, scratch):
        dy_ref, wd_ref, g_ref, u_ref = ins
        dg_ref, du_ref = outs
        dy = dy_ref[...]
        for s in range(sp):
            da = lax.dot_general(dy, wd_ref[s], _DIMS["nt"], preferred_element_type=F32)
            g, u = g_ref[s].astype(F32), u_ref[s].astype(F32)
            sig = _sigmoid(g)
            dg_ref[s] = (da * u * (sig * (1.0 + g * (1.0 - sig)))).astype(dg_ref.dtype)
            du_ref[s] = (da * (g * sig)).astype(du_ref.dtype)

    o_spec = pl.BlockSpec((sp, tm, fs), lambda j, i: (j, i, 0))
    sds = jax.ShapeDtypeStruct((N_DEV, S, fs), BF16)
    dg, du, *rest = _pallas(body, name, (N_DEV // sp, S // tm),
                            [dy, wd.reshape(N_DEV, fs, D), g.reshape(N_DEV, S, fs), u.reshape(N_DEV, S, fs)],
                            [pl.BlockSpec((tm, D), lambda j, i: (i, 0)), pl.BlockSpec((sp, fs, D), lambda j, i: (j, 0, 0)),
                             o_spec, o_spec], [o_spec, o_spec], [sds, sds], sem=("parallel", "arbitrary"), ride=ride)
    return [dg.reshape(N_DEV * S, fs), du.reshape(N_DEV * S, fs)] + rest


def _ffn_down(a, wd, x, name):
    S, D = x.shape
    fs = a.shape[1]
    tm = _tile(S, ROW_TILE)
    sp = FFN_SLOTS
    nj = N_DEV // sp

    def body(a_ref, w_ref, x_ref, o_ref, acc_ref):
        j = pl.program_id(1)
        part = jnp.dot(a_ref[0], w_ref[0], preferred_element_type=F32)
        for s in range(1, sp):
            part += jnp.dot(a_ref[s], w_ref[s], preferred_element_type=F32)

        @pl.when(j == 0)
        def _():
            acc_ref[...] = x_ref[...] + part

        @pl.when(j > 0)
        def _():
            acc_ref[...] += part

        @pl.when(j == nj - 1)
        def _():
            o_ref[...] = acc_ref[...]

    row = pl.BlockSpec((tm, D), lambda i, j: (i, 0))
    return _pcall(
        body, name=name, grid=(S // tm, nj),
        in_specs=[pl.BlockSpec((sp, tm, fs), lambda i, j: (j, i, 0)), pl.BlockSpec((sp, fs, D), lambda i, j: (j, 0, 0)), row],
        out_specs=row, out_shape=jax.ShapeDtypeStruct((S, D), F32),
        scratch_shapes=[pltpu.VMEM((tm, D), F32)],
        compiler_params=_params(("parallel", "arbitrary")),
    )(a.reshape(N_DEV, S, fs), wd.reshape(N_DEV, fs, D), x)


def _ffn_dh(dg, du, wg, wu, S, name, ride=None):
    fs = dg.shape[1]
    D = wg.shape[0] // N_DEV
    tm = _tile(S, ROW_TILE)
    sp = FFN_SLOTS
    nj = N_DEV // sp

    def body(ins, outs, scratch):
        dg_ref, du_ref, wg_ref, wu_ref = ins
        o_ref, acc_ref = outs[0], scratch[0]
        j = pl.program_id(1)
        part = None
        for s in range(sp):
            for d_ref, w_ref in ((dg_ref, wg_ref), (du_ref, wu_ref)):
                term = lax.dot_general(d_ref[s], w_ref[s], _DIMS["nt"], preferred_element_type=F32)
                part = term if part is None else part + term

        @pl.when(j == 0)
        def _():
            acc_ref[...] = part

        @pl.when(j > 0)
        def _():
            acc_ref[...] += part

        @pl.when(j == nj - 1)
        def _():
            o_ref[...] = acc_ref[...]

    d_spec = pl.BlockSpec((sp, tm, fs), lambda i, j: (j, i, 0))
    w_spec = pl.BlockSpec((sp, D, fs), lambda i, j: (j, 0, 0))
    return _pallas(body, name, (S // tm, nj),
                   [dg.reshape(N_DEV, S, fs), du.reshape(N_DEV, S, fs), wg.reshape(N_DEV, D, fs), wu.reshape(N_DEV, D, fs)],
                   [d_spec, d_spec, w_spec, w_spec], [pl.BlockSpec((tm, D), lambda i, j: (i, 0))],
                   [jax.ShapeDtypeStruct((S, D), F32)], scratch=[pltpu.VMEM((tm, D), F32)],
                   sem=("parallel", "arbitrary"), ride=ride)


def _slots_dw_t(name, at, d_slots, ride=None):
    K, S = at.shape
    n = d_slots.shape[1]
    tk = _tile(S, 2 * ROW_TILE)
    nkb = S // tk
    return _mm(name, "nn", at, d_slots, (K, n, tk), (N_DEV, 1, nkb),
               lambda j, q, k: (0, k), lambda j, q, k: (j * nkb + k, 0), lambda j, q, k: (j, 0),
               (N_DEV * K, n), BF16, ride=ride)


def _rope_heads(q, cos_t, sin_t):
    parts = []
    for c0 in range(0, q.shape[1], HEAD_PAD):
        parts += [q[:, c0:c0 + QK_NOPE_DIM], _rope(q[:, c0 + QK_NOPE_DIM:c0 + HEAD_PAD], cos_t, sin_t)]
    return jnp.concatenate(parts, axis=1)


def _kv_mid(kv, g, cos_t, sin_t):
    S, W = kv.shape
    R = W - 128
    ts = _tile(S, ROW_TILE)

    def body(kv_ref, g_ref, c_ref, s_ref, c_out, k_out):
        cf = kv_ref[:, :R]
        c_out[...] = (cf * _rms_scale(cf) * g_ref[...]).astype(c_out.dtype)
        k_out[...] = _rope(kv_ref[:, R:], c_ref[...], s_ref[...]).astype(k_out.dtype)

    tab = pl.BlockSpec((ts, 128), lambda i: (i, 0))
    return _pcall(
        body, name="kv_mid", grid=(S // ts,),
        in_specs=[pl.BlockSpec((ts, W), lambda i: (i, 0)), pl.BlockSpec((1, R), lambda i: (0, 0)), tab, tab],
        out_specs=(pl.BlockSpec((ts, R), lambda i: (i, 0)), tab),
        out_shape=(jax.ShapeDtypeStruct((S, R), BF16), jax.ShapeDtypeStruct((S, 128), BF16)),
        compiler_params=_params(("parallel",)),
    )(kv, g, cos_t, sin_t)


def _kv_mid_bwd(kv, g, dc, dkpe, cos_t, sin_t):
    S, W = kv.shape
    R = W - 128
    H = dkpe.shape[0]
    ts = _row_tile(S, H * 128)

    def body(kv_ref, g_ref, dc_ref, dk_ref, c_ref, s_ref, dkv_ref, dg_ref):
        dx, dgt = _rms_bwd(kv_ref[:, :R], g_ref[...], dc_ref[...])
        dkv_ref[:, :R] = dx.astype(dkv_ref.dtype)
        dk = dk_ref[0]
        for h in range(1, H):
            dk = dk + dk_ref[h]
        dkv_ref[:, R:] = _rope(dk, c_ref[...], -s_ref[...]).astype(dkv_ref.dtype)

        @pl.when(pl.program_id(0) == 0)
        def _():
            dg_ref[...] = jnp.zeros_like(dg_ref)

        dg_ref[...] += jnp.sum(dgt, axis=0, keepdims=True)

    tab = pl.BlockSpec((ts, 128), lambda i: (i, 0))
    vec = pl.BlockSpec((1, R), lambda i: (0, 0))
    return _pcall(
        body, name="kv_mid_bwd", grid=(S // ts,),
        in_specs=[pl.BlockSpec((ts, W), lambda i: (i, 0)), vec, pl.BlockSpec((ts, R), lambda i: (i, 0)),
                  pl.BlockSpec((H, ts, 128), lambda i: (0, i, 0)), tab, tab],
        out_specs=(pl.BlockSpec((ts, W), lambda i: (i, 0)), vec),
        out_shape=(jax.ShapeDtypeStruct((S, W), BF16), jax.ShapeDtypeStruct((1, R), F32)),
        compiler_params=_params(("arbitrary",)),
    )(kv, g, dc, dkpe, cos_t, sin_t)


def _block_pairs(nb, by_key):
    pairs = ([(qi, ki) for ki in range(nb) for qi in range(ki, nb)] if by_key
             else [(qi, ki) for qi in range(nb) for ki in range(qi + 1)])
    return jnp.asarray(np.array([p[0] for p in pairs], np.int32)), jnp.asarray(np.array([p[1] for p in pairs], np.int32))


def _causal_mask(blk, transposed=False, rows=None, row0=0):
    shape = (blk if rows is None else rows, blk)
    r = lax.broadcasted_iota(jnp.int32, shape, 0) + row0
    c = lax.broadcasted_iota(jnp.int32, shape, 1)
    return (r <= c) if transposed else (c <= r)


def _attn_specs(S, blk, hpd):
    nb = S // blk
    w = hpd * HEAD_PAD
    return dict(
        q=pl.BlockSpec((blk, w), lambda j, t, qt, kt: (j * nb + qt[t], 0)),
        kv=pl.BlockSpec((blk, w), lambda j, t, qt, kt: (j * nb + kt[t], 0)),
        kp=pl.BlockSpec((blk, 128), lambda j, t, qt, kt: (kt[t], 0)),
        do=pl.BlockSpec((blk, hpd * V_DIM), lambda j, t, qt, kt: (qt[t], j)),
        col=pl.BlockSpec((hpd, blk, 1), lambda j, t, qt, kt: (j, qt[t], 0)),
        row=pl.BlockSpec((hpd, 1, blk), lambda j, t, qt, kt: (j, 0, qt[t])))


def _head_k(kv_ref, kp, hh):
    return jnp.concatenate([kv_ref[:, hh * HEAD_PAD:hh * HEAD_PAD + QK_NOPE_DIM], kp], axis=1)


def _head_v(kv_ref, hh):
    return kv_ref[:, hh * HEAD_PAD + QK_NOPE_DIM:(hh + 1) * HEAD_PAD]


def _attn_fwd(q, kvu, kpe, H, ride=None):
    S = kpe.shape[0]
    hpd = H // N_DEV
    blk = _tile(S, ATT_BLOCK)
    q_tab, k_tab = _block_pairs(S // blk, by_key=False)
    rc = min(blk, ATT_ROW_CHUNK)

    def body(ins, outs, scratch):
        qt, kt, q_ref, kv_ref, kp_ref = ins
        o_ref, lse_ref = outs
        m_ref, acc_ref = scratch
        t = pl.program_id(1)
        qi, ki = qt[t], kt[t]

        @pl.when(ki == 0)
        def _():
            m_ref[...] = jnp.full_like(m_ref, -jnp.inf)
            acc_ref[...] = jnp.zeros_like(acc_ref)

        def step(masked):
            kp = kp_ref[...]
            ones = jnp.ones((blk, 128), BF16)
            for hh in range(hpd):
                k = _head_k(kv_ref, kp, hh)
                v_ext = jnp.concatenate([_head_v(kv_ref, hh), ones], axis=1)
                for r0 in range(0, blk, rc):
                    rows = slice(r0, r0 + rc)
                    s = lax.dot_general(q_ref[rows, hh * HEAD_PAD:(hh + 1) * HEAD_PAD], k, _DIMS["nt"],
                                        preferred_element_type=F32)
                    if masked:
                        s = jnp.where(_causal_mask(blk, rows=rc, row0=r0), s, -jnp.inf)
                    m_old = m_ref[hh, rows]
                    m_new = jnp.maximum(m_old, jnp.max(s, axis=-1, keepdims=True))
                    p = jnp.exp2((s - m_new) * EXP2_SCALE).astype(BF16)
                    acc_ref[hh, rows] = (jnp.exp2((m_old - m_new) * EXP2_SCALE) * acc_ref[hh, rows]
                                         + jnp.dot(p, v_ext, preferred_element_type=F32))
                    m_ref[hh, rows] = m_new

        @pl.when(ki < qi)
        def _():
            step(False)

        @pl.when(ki == qi)
        def _():
            step(True)
            for hh in range(hpd):
                l = acc_ref[hh, :, V_DIM:]
                o_ref[:, hh * V_DIM:(hh + 1) * V_DIM] = (acc_ref[hh, :, :V_DIM] / l).astype(o_ref.dtype)
                lse_ref[hh] = m_ref[hh] * EXP2_SCALE + jnp.log2(l[:, :1])

    sp = _attn_specs(S, blk, hpd)
    return _pallas(body, "attn_fwd", (N_DEV, q_tab.shape[0]), [q, kvu, kpe], [sp["q"], sp["kv"], sp["kp"]],
                   [sp["do"], sp["col"]],
                   [jax.ShapeDtypeStruct((S, H * V_DIM), BF16), jax.ShapeDtypeStruct((H, S, 1), F32)],
                   scratch=[pltpu.VMEM((hpd, blk, 1), F32), pltpu.VMEM((hpd, blk, 2 * V_DIM), F32)],
                   sem=("parallel", "arbitrary"), ride=ride, prefetch=[q_tab, k_tab])


def _attn_bwd(q, kvu, kpe, do, o, lse_row, cos_t, sin_t, H):
    S = kpe.shape[0]
    hpd = H // N_DEV
    blk = _tile(S, ATT_BLOCK)
    nb = S // blk
    q_tab, k_tab = _block_pairs(nb, by_key=True)

    def body(qt, kt, q_ref, kv_ref, kp_ref, do_ref, o_ref, lse_ref, c_ref, s_ref, dq_ref, dkv_ref, dkp_ref,
             dq_acc, dk_acc, dv_acc):
        t = pl.program_id(1)
        qi, ki = qt[t], kt[t]
        q_rows = pl.ds(pl.multiple_of(qi * blk, blk), blk)

        def step(diag):
            kp = kp_ref[...]
            ones = jnp.ones((8, V_DIM), BF16)
            for hh in range(hpd):
                k = _head_k(kv_ref, kp, hh)
                qv = q_ref[:, hh * HEAD_PAD:(hh + 1) * HEAD_PAD]
                cols = slice(hh * V_DIM, (hh + 1) * V_DIM)
                dov = do_ref[:, cols]
                doo = (dov.astype(F32) * o_ref[:, cols].astype(F32)).astype(BF16)
                delta = lax.dot_general(ones, doo, _DIMS["nt"], preferred_element_type=F32)[:1]
                st = lax.dot_general(k, qv, _DIMS["nt"], preferred_element_type=F32)
                pt = jnp.exp2(st * EXP2_SCALE - lse_ref[hh])
                if diag:
                    pt = jnp.where(_causal_mask(blk, transposed=True), pt, 0.0)
                dv_new = jnp.dot(pt.astype(BF16), dov, preferred_element_type=F32)
                dpt = lax.dot_general(_head_v(kv_ref, hh), dov, _DIMS["nt"], preferred_element_type=F32)
                dst = (pt * (dpt - delta)).astype(BF16)
                dk_new = jnp.dot(dst, qv, preferred_element_type=F32)
                dq_new = lax.dot_general(dst, k, _DIMS["tn"], preferred_element_type=F32)
                if diag:
                    dv_acc[hh] = dv_new
                    dk_acc[hh] = dk_new
                else:
                    dv_acc[hh] += dv_new
                    dk_acc[hh] += dk_new

                @pl.when(ki == 0)
                def _():
                    dq_acc[hh, q_rows] = dq_new

                @pl.when(ki > 0)
                def _():
                    dq_acc[hh, q_rows] += dq_new

        @pl.when(qi == ki)
        def _():
            step(True)
            for hh in range(hpd):
                c0 = hh * HEAD_PAD
                dq = dq_acc[hh, q_rows] * ATTN_SCALE
                dq_ref[:, c0:c0 + QK_NOPE_DIM] = dq[:, :QK_NOPE_DIM].astype(dq_ref.dtype)
                dq_ref[:, c0 + QK_NOPE_DIM:c0 + HEAD_PAD] = _rope(dq[:, QK_NOPE_DIM:], c_ref[...],
                                                                  -s_ref[...]).astype(dq_ref.dtype)

        @pl.when(qi > ki)
        def _():
            step(False)

        @pl.when(qi == nb - 1)
        def _():
            for hh in range(hpd):
                c0 = hh * HEAD_PAD
                dkv_ref[:, c0:c0 + QK_NOPE_DIM] = (dk_acc[hh, :, :QK_NOPE_DIM] * ATTN_SCALE).astype(dkv_ref.dtype)
                dkv_ref[:, c0 + QK_NOPE_DIM:c0 + HEAD_PAD] = dv_acc[hh].astype(dkv_ref.dtype)
                dkp_ref[hh] = dk_acc[hh, :, QK_NOPE_DIM:] * ATTN_SCALE

    sp = _attn_specs(S, blk, hpd)
    key_tab = pl.BlockSpec((blk, 128), lambda j, t, qt, kt: (kt[t], 0))
    grid_spec = pltpu.PrefetchScalarGridSpec(
        num_scalar_prefetch=2, grid=(N_DEV, q_tab.shape[0]),
        in_specs=[sp["q"], sp["kv"], sp["kp"], sp["do"], sp["do"], sp["row"], key_tab, key_tab],
        out_specs=(sp["kv"], sp["kv"], pl.BlockSpec((hpd, blk, 128), lambda j, t, qt, kt: (j, kt[t], 0))),
        scratch_shapes=[pltpu.VMEM((hpd, S, HEAD_PAD), F32), pltpu.VMEM((hpd, blk, HEAD_PAD), F32),
                        pltpu.VMEM((hpd, blk, V_DIM), F32)])
    return _pcall(
        body, name="attn_bwd", grid_spec=grid_spec,
        out_shape=(jax.ShapeDtypeStruct(q.shape, BF16), jax.ShapeDtypeStruct(kvu.shape, BF16),
                   jax.ShapeDtypeStruct((H, S, 128), F32)),
        compiler_params=_params(("parallel", "arbitrary")),
    )(q_tab, k_tab, q, kvu, kpe, do, o, lse_row, cos_t, sin_t)


def _mesh_pos():
    return lax.axis_index("x"), lax.axis_index("y"), lax.axis_index("c")


def _flip(pos, k):
    x, y, c = pos
    return (1 - x if k & 4 else x, 1 - y if k & 2 else y, 1 - c if k & 1 else c)


def _rank(pos):
    return 4 * pos[0] + 2 * pos[1] + pos[2]


def _copies(n, src_of, dst_refs, local_src_of, send_sems, recv_sems, local_sems):
    me = _mesh_pos()
    local = [pltpu.make_async_copy(local_src_of(a), dst_refs[a].at[_rank(me)], local_sems.at[a]) for a in range(n)]
    sends, arrivals = [], []
    for k in range(1, N_DEV):
        peer = _flip(me, k)
        for a in range(n):
            idx = a * (N_DEV - 1) + k - 1
            for into, group in ((me, sends), (peer, arrivals)):
                group.append(pltpu.make_async_remote_copy(
                    src_ref=src_of(a, peer), dst_ref=dst_refs[a].at[_rank(into)],
                    send_sem=send_sems.at[idx], recv_sem=recv_sems.at[idx],
                    device_id=peer, device_id_type=pl.DeviceIdType.MESH))
    return local, sends, arrivals


def _exchange_start(*args):
    local, sends, _ = _copies(*args)
    for cp in local + sends:
        cp.start()


def _exchange_wait(*args):
    local, sends, arrivals = _copies(*args)
    for cp in arrivals:
        cp.wait_recv()
    for cp in sends:
        cp.wait_send()
    for cp in local:
        cp.wait()


def _exchange(*args):
    _exchange_start(*args)
    _exchange_wait(*args)


def _sems(n):
    return [pltpu.SemaphoreType.DMA((n * (N_DEV - 1),)), pltpu.SemaphoreType.DMA((n * (N_DEV - 1),)),
            pltpu.SemaphoreType.DMA((n,))]


_ANY = pl.BlockSpec(memory_space=pl.ANY)


class _Ride:
    def __init__(self, kind, arrays):
        self.kind, self.arrays, self.n = kind, list(arrays), len(arrays)

    def out_shape(self):
        lead = (N_DEV,) if self.kind == "gather" else ()
        return [jax.ShapeDtypeStruct(lead + a.shape, a.dtype) for a in self.arrays]

    def _args(self, x_refs, out_refs, sems):
        if self.kind == "gather":
            return (self.n, lambda a, peer: x_refs[a], out_refs, lambda a: x_refs[a]) + tuple(sems)
        me = _mesh_pos()
        return (self.n, lambda a, peer: x_refs[a].at[_rank(peer)], out_refs, lambda a: x_refs[a].at[_rank(me)]) + tuple(sems)

    def start(self, x_refs, out_refs, sems):
        _exchange_start(*self._args(x_refs, out_refs, sems))

    def wait(self, x_refs, out_refs, sems):
        _exchange_wait(*self._args(x_refs, out_refs, sems))


def _pallas(body, name, grid, operands, in_specs, out_specs, out_shape, scratch=(), sem=None, ride=None, prefetch=()):
    n_pf, n_in, n_out, n_scr = len(prefetch), len(in_specs), len(out_specs), len(scratch)
    nr = ride.n if ride else 0

    def wrapped(*refs):
        refs = list(refs)
        pf, refs = refs[:n_pf], refs[n_pf:]
        ins, r_in = refs[:n_in], refs[n_in:n_in + nr]
        outs, r_out = refs[n_in + nr:n_in + nr + n_out], refs[n_in + nr + n_out:n_in + 2 * nr + n_out]
        rest = refs[n_in + 2 * nr + n_out:]
        scr, sems = rest[:n_scr], rest[n_scr:]
        if ride:
            first, last = None, None
            for d, size in enumerate(grid):
                i = pl.program_id(d)
                first = (i == 0) if first is None else jnp.logical_and(first, i == 0)
                last = (i == size - 1) if last is None else jnp.logical_and(last, i == size - 1)

            @pl.when(first)
            def _():
                ride.start(r_in, r_out, sems)

        body(pf + ins, outs, scr)
        if ride:
            @pl.when(last)
            def _():
                ride.wait(r_in, r_out, sems)

    grid_spec = pltpu.PrefetchScalarGridSpec(
        num_scalar_prefetch=n_pf, grid=grid,
        in_specs=list(in_specs) + [_ANY] * nr, out_specs=tuple(list(out_specs) + [_ANY] * nr),
        scratch_shapes=list(scratch) + (_sems(nr) if ride else []))
    sem = tuple(sem) if sem is not None and not ride else ("arbitrary",) * len(grid)
    return list(_pcall(
        wrapped, name=name, grid_spec=grid_spec,
        out_shape=tuple(list(out_shape) + (ride.out_shape() if ride else [])),
        compiler_params=_params(sem),
    )(*(list(prefetch) + list(operands) + (ride.arrays if ride else []))))


def _all_gather(shards, name):
    n = len(shards)

    def body(*refs):
        x_refs, out_refs = refs[:n], refs[n:2 * n]
        send_sems, recv_sems, local_sems = refs[2 * n:]
        x, y, c = _mesh_pos()
        me, sibling = (x, y, c), (x, y, 1 - c)
        chips = [(1 - x, y), (x, 1 - y), (1 - x, 1 - y)]

        def copy(a, k, block, to, src=None):
            rows = out_refs[a].at[_rank(block)]
            return pltpu.make_async_remote_copy(
                src_ref=rows if src is None else src, dst_ref=rows,
                send_sem=send_sems.at[a * (N_DEV - 1) + k], recv_sem=recv_sems.at[a * (N_DEV - 1) + k],
                device_id=to, device_id_type=pl.DeviceIdType.MESH)

        mine = [pltpu.make_async_copy(x_refs[a], out_refs[a].at[_rank(me)], local_sems.at[a]) for a in range(n)]
        first = []
        for a in range(n):
            first.append(copy(a, 0, me, sibling, src=x_refs[a]))
            first += [copy(a, 1 + j, me, (*chip, c), src=x_refs[a]) for j, chip in enumerate(chips)]
        for cp in mine + first:
            cp.start()
        passed = []
        for j, chip in enumerate(chips):
            for a in range(n):
                copy(a, 1 + j, (*chip, c), me).wait_recv()
                passed.append(copy(a, 4 + j, (*chip, c), sibling))
                passed[-1].start()
        for a in range(n):
            copy(a, 0, sibling, me).wait_recv()
            for j, chip in enumerate(chips):
                copy(a, 4 + j, (*chip, 1 - c), me).wait_recv()
        for cp in first + passed:
            cp.wait_send()
        for cp in mine:
            cp.wait()

    return _pcall(
        body, name=name, in_specs=[_ANY] * n, out_specs=tuple([_ANY] * n),
        out_shape=tuple(jax.ShapeDtypeStruct((N_DEV,) + s.shape, s.dtype) for s in shards),
        scratch_shapes=_sems(n),
    )(*shards)


def _all_to_all(parts, name):
    n = len(parts)

    def body(*refs):
        x_refs, out_refs = refs[:n], refs[n:2 * n]
        me = _mesh_pos()
        _exchange(n, lambda a, peer: x_refs[a].at[_rank(peer)], out_refs, lambda a: x_refs[a].at[_rank(me)],
                  *refs[2 * n:])

    return _pcall(
        body, name=name, in_specs=[_ANY] * n, out_specs=tuple([_ANY] * n),
        out_shape=tuple(jax.ShapeDtypeStruct(p.shape, p.dtype) for p in parts),
        scratch_shapes=_sems(n),
    )(*parts)


def _all_reduce_small(v):
    R, C = v.shape

    def body(x_ref, out_ref, buf_ref, send_sems, recv_sems, local_sems):
        _exchange(1, lambda a, peer: x_ref, [buf_ref], lambda a: x_ref, send_sems, recv_sems, local_sems)
        total = buf_ref[0]
        for j in range(1, N_DEV):
            total = total + buf_ref[j]
        out_ref[...] = total

    vm = pl.BlockSpec(memory_space=pltpu.VMEM)
    return _pcall(
        body, name="all_reduce_small", in_specs=[vm], out_specs=vm,
        out_shape=jax.ShapeDtypeStruct((R, C), F32),
        scratch_shapes=[pltpu.VMEM((N_DEV, R, C), F32)] + _sems(1),
    )(v)


def _sum_slots(layers, name):
    L = len(layers)
    shape = layers[0].shape[1:]
    C = shape[-1]
    R = layers[0].size // (N_DEV * C)
    tr = R
    while N_DEV * tr * C * 4 > (4 << 20) and tr % 32 == 0:
        tr //= 2

    def body(*refs):
        o_ref = refs[L]
        for li in range(L):
            @pl.when(pl.program_id(0) == li)
            def _():
                total = refs[li][0].astype(F32)
                for j in range(1, N_DEV):
                    total = total + refs[li][j].astype(F32)
                o_ref[0] = total

    in_specs = [pl.BlockSpec((N_DEV, tr, C), lambda l, i, li=li: (0, jnp.where(l == li, i, 0), 0)) for li in range(L)]
    return _pcall(
        body, name=name, grid=(L, R // tr),
        in_specs=in_specs,
        out_specs=pl.BlockSpec((1, tr, C), lambda l, i: (l, i, 0)),
        out_shape=jax.ShapeDtypeStruct((L, R, C), F32),
        compiler_params=_params(("parallel", "parallel")),
    )(*[p.reshape(N_DEV, R, C) for p in layers]).reshape((L,) + shape)


def _adamw(w, g, m, v, name):
    shape = w.shape
    C = shape[-1]
    R = w.size // C
    tr = R
    while tr * C * 4 > (1 << 20) and tr % 16 == 0:
        tr //= 2

    def body(w_ref, g_ref, m_ref, v_ref, d_ref, nm_ref, nv_ref):
        gv = g_ref[...]
        nm = ADAM_B1 * m_ref[...] + (1.0 - ADAM_B1) * gv
        nv = ADAM_B2 * v_ref[...] + (1.0 - ADAM_B2) * (gv * gv)
        m_hat = nm / (1.0 - ADAM_B1 ** ADAM_STEP)
        v_hat = nv / (1.0 - ADAM_B2 ** ADAM_STEP)
        d_ref[...] = -ADAM_LR * (m_hat / (jnp.sqrt(v_hat) + ADAM_EPS) + ADAM_WD * w_ref[...])
        nm_ref[...] = nm
        nv_ref[...] = nv

    blk = pl.BlockSpec((tr, C), lambda i: (i, 0))
    sds = jax.ShapeDtypeStruct((R, C), F32)
    outs = _pcall(
        body, name=name, grid=(R // tr,),
        in_specs=[blk] * 4, out_specs=(blk,) * 3, out_shape=(sds,) * 3,
        compiler_params=_params(("parallel",)),
    )(*(t.reshape(R, C) for t in (w, g, m, v)))
    return tuple(o.reshape(shape) for o in outs)


_REPLICATED = ("kv_in_norm", "kv_latent_norm", "attn_norm", "q_latent_norm", "ffn_norm", "final_norm")
_WEIGHTS = ("pool_norm", "pool_w", "pool_scale", "kv_in_norm", "w_kv_a", "kv_latent_norm", "w_kv_b", "attn_norm",
            "w_q_a", "q_latent_norm", "w_q_b", "w_o", "ffn_norm", "w_gate", "w_up", "w_down", "final_norm")


def _ffn_fwd(x, norm_g, wg, wu, wd, tag, gather=None):
    S, D = x.shape
    fs = wg.shape[1]
    h, ht = _rmsnorm(x, norm_g, "ffn_norm" + tag, transposed_too=True)
    g, u, a, *gathered = _ffn_up(h, wg, wu, "ffn_up" + tag, ride=gather)
    if gather is not None:
        wd = gathered[0].reshape(N_DEV * fs, D)
    y = _ffn_down(a, wd, x, "ffn_down" + tag)
    return y, (ht, g, u, a), wd, gathered


def _ffn_bwd(x, norm_g, wg, wu, wd, saved, dy, dyb, tag, with_bf16, scatter=None):
    S, D = x.shape
    ht, g, u, a = saved
    fs = g.shape[1]
    tk = _tile(S, 2 * ROW_TILE)
    nkb = S // tk
    dg, du, *got_rider = _ffn_dact(dyb, wd, g, u, "ffn_dact" + tag, ride=scatter)
    dwd = _mm("ffn_dwd" + tag, "tn", a, dyb, (fs, D, tk), (N_DEV, 1, nkb),
              lambda j, q, k: (j * nkb + k, 0), lambda j, q, k: (k, 0), lambda j, q, k: (j, 0), (N_DEV * fs, D), BF16)
    dwg, (got_dwd,) = _slots_dw_t("ffn_dwg" + tag, ht, dg, ride=_Ride("scatter", [dwd.reshape(N_DEV, fs, D)]))
    dwu, (got_dwg,) = _slots_dw_t("ffn_dwu" + tag, ht, du, ride=_Ride("scatter", [dwg.reshape(N_DEV, D, fs)]))
    dh, got_dwu = _ffn_dh(dg, du, wg, wu, S, "ffn_dh" + tag, ride=_Ride("scatter", [dwu.reshape(N_DEV, D, fs)]))
    outs = _rmsnorm_bwd(x, norm_g, dh, dy, "ffn_norm_bwd" + tag, with_bf16=with_bf16)
    return outs, got_rider, (got_dwg, got_dwu, got_dwd)


def kernel(x, positions, pool_norm, pool_w, pool_scale, kv_in_norm, w_kv_a, kv_latent_norm, w_kv_b, attn_norm, w_q_a, q_latent_norm, w_q_b, w_o, ffn_norm, w_gate, w_up, w_down, final_norm, loss_target, m_pool_norm, m_pool_w, m_pool_scale, m_kv_in_norm, m_w_kv_a, m_kv_latent_norm, m_w_kv_b, m_attn_norm, m_w_q_a, m_q_latent_norm, m_w_q_b, m_w_o, m_ffn_norm, m_w_gate, m_w_up, m_w_down, m_final_norm, v_pool_norm, v_pool_w, v_pool_scale, v_kv_in_norm, v_w_kv_a, v_kv_latent_norm, v_w_kv_b, v_attn_norm, v_w_q_a, v_q_latent_norm, v_w_q_b, v_w_o, v_ffn_norm, v_w_gate, v_w_up, v_w_down, v_final_norm):
    env = dict(locals())
    weights = {n: env[n] for n in _WEIGHTS}
    m_in = {n: env["m_" + n] for n in _WEIGHTS}
    v_in = {n: env["v_" + n] for n in _WEIGHTS}

    S, D = x.shape[1], x.shape[2]
    xs = x.reshape(S, D)
    target = loss_target.reshape(S, D)
    G, gws, gw = pool_w.shape[1:]
    Ds = w_kv_a.shape[0]
    R_kv, kvw = w_kv_b.shape
    hpd = kvw // (QK_NOPE_DIM + V_DIM)
    H = hpd * N_DEV
    R_q = w_q_a.shape[2]
    fs = w_gate.shape[2]
    bf = lambda t: t.astype(BF16)

    gains = jnp.concatenate([pool_norm, pool_scale], axis=0)
    g_pool_w, g_gains, g_wg0, g_wu0 = _all_gather([bf(pool_w[0]), gains, bf(w_gate[0]), bf(w_up[0])], "gather_first")
    wqb_pad = jnp.pad(w_q_b.reshape(R_q, hpd, QK_DIM), ((0, 0), (0, 0), (0, HEAD_PAD - QK_DIM))).reshape(R_q, hpd * HEAD_PAD)
    gather_mla = _Ride("gather", [bf(w_down[0]), bf(jnp.pad(w_kv_a, ((0, 0), (0, 128 - QK_ROPE_DIM)))), bf(w_kv_b),
                                  bf(w_q_a[0]), bf(wqb_pad), bf(w_o[0])])
    gather_ffn1 = _Ride("gather", [bf(w_gate[1]), bf(w_up[1]), bf(w_down[1])])

    wp = jnp.swapaxes(g_pool_w, 0, 1).reshape(G, gw, gw)
    gains_full = jnp.swapaxes(g_gains, 0, 1).reshape(2, D)
    g_pool_norm, g_pool_scale = gains_full[0:1], gains_full[1:2]
    wg0, wu0 = g_wg0.reshape(N_DEV * D, fs), g_wu0.reshape(N_DEV * D, fs)

    g_kv_in = kv_in_norm.reshape(1, D)
    g_kv_lat = kv_latent_norm.reshape(1, R_kv)
    g_attn = attn_norm.reshape(1, D)
    g_q_lat = q_latent_norm.reshape(1, R_q)
    g_final = final_norm.reshape(1, D)

    half = QK_ROPE_DIM // 2
    inv_freq = ROPE_BASE ** (-jnp.arange(half, dtype=F32) / half)
    ang = positions.reshape(S).astype(F32)[:, None] * inv_freq
    cos, sin = jnp.cos(ang), jnp.sin(ang)
    zeros = jnp.zeros((S, 128 - QK_ROPE_DIM), F32)
    cos_t = jnp.concatenate([cos, cos, zeros], axis=1)
    sin_t = jnp.concatenate([-sin, sin, zeros], axis=1)

    diff = _pool_pre(xs, g_pool_norm)
    x1 = _pool_mix(diff, wp, g_pool_scale, xs)
    x2, ffn0, wd0, (_, g_wkva, g_wkvb, g_wqa, g_wqb, g_wo) = _ffn_fwd(x1, ffn_norm[0:1], wg0, wu0, None, "0", gather_mla)
    wkva = g_wkva.reshape(D, R_kv + 128)
    wkvb = g_wkvb.reshape(N_DEV * R_kv, kvw)
    wqa = g_wqa.reshape(D, R_q)
    wqb = g_wqb.reshape(N_DEV * R_q, hpd * HEAD_PAD)
    wo = g_wo.reshape(H * V_DIM, D)

    hk, ha = _rmsnorm_pair(x2, g_kv_in, g_attn, "kv_attn_norm")
    kv = _mm_plain("kv_a", "nn", hk, wkva)
    ckv, kpe = _kv_mid(kv, g_kv_lat, cos_t, sin_t)
    kvu = _small_slots_out("kv_b", ckv, wkvb, BF16)

    ql = _mm_plain("q_a", "nn", ha, wqa)
    qn = _rmsnorm(ql, g_q_lat, "q_latent_norm")
    tab = pl.BlockSpec((_tile(S, ROW_TILE), 128), lambda i: (i, 0))
    qr = _small_slots_out("q_b", qn, wqb, BF16, epilogue=(_rope_heads, [cos_t, sin_t], [tab, tab]))
    o, lse, g_wg1, g_wu1, g_wd1 = _attn_fwd(qr, kvu, kpe, H, ride=gather_ffn1)
    wg1, wu1, wd1 = g_wg1.reshape(N_DEV * D, fs), g_wu1.reshape(N_DEV * D, fs), g_wd1.reshape(N_DEV * fs, D)
    x3 = _mm_plain("attn_out", "nn", o, wo, res=x2)
    x4, ffn1, _, _ = _ffn_fwd(x3, ffn_norm[1:2], wg1, wu1, wd1, "1")

    loss_part, dx4, dx4b, d_final = _loss_head(x4, g_final, target)

    (dx3, dx3b, d_ffn1), _, got_ffn1 = _ffn_bwd(x3, ffn_norm[1:2], wg1, wu1, wd1, ffn1, dx4, dx4b, "1", True)

    do = _mm_plain("attn_out_dx", "nt", dx3b, wo, out_dtype=BF16)
    dwo = _mm_plain("attn_out_dw", "tn", o, dx3b, out_dtype=BF16)
    dq, dkvu, dkpe = _attn_bwd(qr, kvu, kpe, do, o, lse.reshape(H, 1, S), cos_t, sin_t, H)

    dqn = _small_slots_in_nt("q_b_dx", dq, wqb, S)
    dwqb = _small_slots_dw("q_b_dw", qn, dq, BF16)
    dql, d_q_lat = _rmsnorm_bwd(ql, g_q_lat, dqn, None, "q_latent_norm_bwd", out_dtype=BF16)
    dha = _mm_plain("q_a_dx", "nt", dql, wqa)
    dwqa = _mm_plain("q_a_dw", "tn", ha, dql, out_dtype=BF16)

    dckv = _small_slots_in_nt("kv_b_dx", dkvu, wkvb, S)
    dwkvb = _small_slots_dw("kv_b_dw", ckv, dkvu, BF16)
    dkv, d_kv_lat = _kv_mid_bwd(kv, g_kv_lat, dckv, dkpe, cos_t, sin_t)
    dhk = _mm_plain("kv_a_dx", "nt", dkv, wkva)
    dwkva = _mm_plain("kv_a_dw", "tn", hk, dkv, out_dtype=BF16)
    dx2, dx2b, d_attn, d_kv_in = _rmsnorm_pair_bwd(x2, g_attn, dha, g_kv_in, dhk, dx3, "kv_attn_norm_bwd")

    scatter_mla = _Ride("scatter", [dwkva.reshape(N_DEV, Ds, R_kv + 128), dwkvb.reshape(N_DEV, R_kv, kvw),
                                    dwqa.reshape(N_DEV, Ds, R_q), dwqb.reshape(N_DEV, R_q, hpd * HEAD_PAD),
                                    dwo.reshape(N_DEV, H * V_DIM // N_DEV, D)])
    (dx1, d_ffn0), got_mla, got_ffn0 = _ffn_bwd(x1, ffn_norm[0:1], wg0, wu0, wd0, ffn0, dx2, dx2b, "0", False,
                                                  scatter=scatter_mla)

    ddiff, dwp, d_pool_scale = _pool_mix_bwd(diff, wp, g_pool_scale, dx1)
    grad_x, d_pool_norm = _pool_pre_bwd(xs, g_pool_norm, ddiff, dx1)

    d_gains = jnp.swapaxes(jnp.concatenate([d_pool_norm, d_pool_scale], axis=0).reshape(2, N_DEV, D // N_DEV), 0, 1)
    got_pool_w, got_gains = _all_to_all([jnp.swapaxes(dwp.reshape(G, N_DEV, gws, gw), 0, 1), d_gains], "exchange_pool")

    received = {n: [r] for n, r in zip(("w_kv_a", "w_kv_b", "w_q_a", "w_q_b", "w_o"), got_mla)}
    received.update({n: [r0, r1] for n, r0, r1 in zip(("w_gate", "w_up", "w_down"), got_ffn0, got_ffn1)})
    received.update(pool_w=[got_pool_w], gains=[got_gains])
    sums = {n: _sum_slots(r, "sum_" + n) for n, r in received.items()}
    grads = {
        "pool_w": sums["pool_w"],
        "w_kv_a": sums["w_kv_a"][0, :, :R_kv + QK_ROPE_DIM],
        "w_kv_b": sums["w_kv_b"][0],
        "w_q_a": sums["w_q_a"],
        "w_q_b": sums["w_q_b"].reshape(R_q, hpd, HEAD_PAD)[:, :, :QK_DIM].reshape(1, R_q, hpd * QK_DIM),
        "w_o": sums["w_o"],
        "w_gate": sums["w_gate"],
        "w_up": sums["w_up"],
        "w_down": sums["w_down"],
        "pool_norm": sums["gains"][0, 0:1],
        "pool_scale": sums["gains"][0, 1:2],
    }

    small = {"kv_in_norm": d_kv_in, "kv_latent_norm": d_kv_lat, "attn_norm": d_attn, "q_latent_norm": d_q_lat,
             "ffn_norm": jnp.concatenate([d_ffn0, d_ffn1], axis=0), "final_norm": d_final}
    small_packed = jnp.concatenate([small[n].reshape(-1) for n in _REPLICATED] + [loss_part.reshape(-1)])
    pad = (-small_packed.shape[0]) % (8 * 128)
    reduced = _all_reduce_small(jnp.pad(small_packed, (0, pad)).reshape(-1, 128)).reshape(-1)
    off = 0
    for n in _REPLICATED:
        grads[n] = reduced[off:off + weights[n].size].reshape(weights[n].shape)
        off += weights[n].size
    loss = reduced[off]

    delta_w, new_m, new_v = {}, {}, {}
    for n in _WEIGHTS:
        delta_w[n], new_m[n], new_v[n] = _adamw(weights[n], grads[n], m_in[n], v_in[n], "adamw_" + n)

    return (loss, grad_x.reshape(x.shape), *[grads[n] for n in _WEIGHTS], *[delta_w[n] for n in _WEIGHTS],
            *[new_m[n] for n in _WEIGHTS], *[new_v[n] for n in _WEIGHTS])
```

```python
import math

import jax
import jax.numpy as jnp
import numpy as np
from jax import lax
from jax.experimental import pallas as pl
from jax.experimental.pallas import tpu as pltpu

F32 = jnp.float32
BF16 = jnp.bfloat16

N_DEV = 8
POOL_WINDOWS = (2, 4, 8, 16)
POOL_HALO = 16
QK_NOPE_DIM = 128
QK_ROPE_DIM = 64
QK_DIM = QK_NOPE_DIM + QK_ROPE_DIM
HEAD_PAD = 256
V_DIM = 128
ROPE_BASE = 10000.0
ATTN_SCALE = 1.0 / math.sqrt(QK_DIM)
EXP2_SCALE = ATTN_SCALE * math.log2(math.e)
NORM_EPS = 1e-6
ADAM_LR = 0.001
ADAM_B1 = 0.9
ADAM_B2 = 0.999
ADAM_EPS = 1e-08
ADAM_WD = 0.01
ADAM_STEP = 10

ROW_TILE = 512
ATT_BLOCK = 1024
ATT_ROW_CHUNK = 256
FFN_SLOTS = 2
MM_TN, MM_TK = 512, 512
VMEM_LIMIT = 48 * 1024 * 1024

_pcall = pl.pallas_call


def _params(sem):
    return pltpu.CompilerParams(dimension_semantics=sem, vmem_limit_bytes=VMEM_LIMIT)


def _tile(dim, default):
    if dim % default == 0:
        return default
    assert dim <= 2 * default, (dim, default)
    return dim


def _row_tile(rows, width):
    ts = _tile(rows, ROW_TILE)
    while ts * width * 4 > (2 << 20) and ts % 16 == 0:
        ts //= 2
    return ts


def _rms_scale(x):
    return lax.rsqrt(jnp.mean(x * x, axis=-1, keepdims=True) + NORM_EPS)


def _rms_bwd(x, g, dy):
    r = _rms_scale(x)
    xn = x * r
    u = dy * g
    dx = r * (u - xn * jnp.mean(u * xn, axis=-1, keepdims=True))
    return dx, dy * xn


def _swap_halves(x):
    lane = lax.broadcasted_iota(jnp.int32, x.shape, 1)
    return jnp.where(lane < QK_ROPE_DIM // 2, pltpu.roll(x, 128 - QK_ROPE_DIM // 2, 1), pltpu.roll(x, QK_ROPE_DIM // 2, 1))


def _rope(x, cos_t, sin_t):
    return x * cos_t + _swap_halves(x) * sin_t


def _rmsnorm(x, g, name, transposed_too=False):
    S, D = x.shape
    ts = _row_tile(S, D)

    def body(x_ref, g_ref, o_ref, *t_ref):
        xf = x_ref[...]
        y = xf * _rms_scale(xf) * g_ref[...]
        o_ref[...] = y.astype(o_ref.dtype)
        if transposed_too:
            t_ref[0][...] = y.T.astype(o_ref.dtype)

    row = pl.BlockSpec((ts, D), lambda i: (i, 0))
    outs = _pcall(
        body, name=name, grid=(S // ts,),
        in_specs=[row, pl.BlockSpec((1, D), lambda i: (0, 0))],
        out_specs=(row, pl.BlockSpec((D, ts), lambda i: (0, i))) if transposed_too else row,
        out_shape=((jax.ShapeDtypeStruct((S, D), BF16), jax.ShapeDtypeStruct((D, S), BF16)) if transposed_too
                   else jax.ShapeDtypeStruct((S, D), BF16)),
        compiler_params=_params(("parallel",)),
    )(x, g)
    return outs


def _rmsnorm_bwd(x, g, dy, res, name, out_dtype=F32, with_bf16=False):
    S, D = x.shape
    ts = _row_tile(S, D)
    has_res = res is not None

    def body(*refs):
        refs = list(refs)
        x_ref, g_ref, dy_ref = refs[:3]
        res_ref = refs[3] if has_res else None
        outs = refs[3 + has_res:]
        dx_ref, dg_ref = outs[0], outs[-1]
        dx, dgt = _rms_bwd(x_ref[...], g_ref[...], dy_ref[...].astype(F32))
        if has_res:
            dx = res_ref[...] + dx
        dx_ref[...] = dx.astype(dx_ref.dtype)
        if with_bf16:
            outs[1][...] = dx.astype(BF16)

        @pl.when(pl.program_id(0) == 0)
        def _():
            dg_ref[...] = jnp.zeros_like(dg_ref)

        dg_ref[...] += jnp.sum(dgt, axis=0, keepdims=True)

    row = pl.BlockSpec((ts, D), lambda i: (i, 0))
    vec = pl.BlockSpec((1, D), lambda i: (0, 0))
    out_specs = [row] + ([row] if with_bf16 else []) + [vec]
    out_shape = ([jax.ShapeDtypeStruct((S, D), out_dtype)] + ([jax.ShapeDtypeStruct((S, D), BF16)] if with_bf16 else [])
                 + [jax.ShapeDtypeStruct((1, D), F32)])
    return _pcall(
        body, name=name, grid=(S // ts,),
        in_specs=[row, vec, row] + ([row] if has_res else []),
        out_specs=tuple(out_specs), out_shape=tuple(out_shape),
        compiler_params=_params(("arbitrary",)),
    )(*([x, g, dy] + ([res] if has_res else [])))


def _rmsnorm_pair(x, g1, g2, name):
    S, D = x.shape
    ts = _row_tile(S, D)

    def body(x_ref, g1_ref, g2_ref, o1_ref, o2_ref):
        xf = x_ref[...]
        xn = xf * _rms_scale(xf)
        o1_ref[...] = (xn * g1_ref[...]).astype(o1_ref.dtype)
        o2_ref[...] = (xn * g2_ref[...]).astype(o2_ref.dtype)

    row = pl.BlockSpec((ts, D), lambda i: (i, 0))
    vec = pl.BlockSpec((1, D), lambda i: (0, 0))
    sds = jax.ShapeDtypeStruct((S, D), BF16)
    return _pcall(
        body, name=name, grid=(S // ts,),
        in_specs=[row, vec, vec], out_specs=(row, row), out_shape=(sds, sds),
        compiler_params=_params(("parallel",)),
    )(x, g1, g2)


def _rmsnorm_pair_bwd(x, g1, dy1, g2, dy2, res, name):
    S, D = x.shape
    ts = _row_tile(S, D)

    def body(x_ref, g1_ref, dy1_ref, g2_ref, dy2_ref, res_ref, dx_ref, dxb_ref, dg1_ref, dg2_ref):
        xf = x_ref[...]
        dx1, dgt1 = _rms_bwd(xf, g1_ref[...], dy1_ref[...])
        dx2, dgt2 = _rms_bwd(xf, g2_ref[...], dy2_ref[...])
        dx = (res_ref[...] + dx1) + dx2
        dx_ref[...] = dx
        dxb_ref[...] = dx.astype(BF16)

        @pl.when(pl.program_id(0) == 0)
        def _():
            dg1_ref[...] = jnp.zeros_like(dg1_ref)
            dg2_ref[...] = jnp.zeros_like(dg2_ref)

        dg1_ref[...] += jnp.sum(dgt1, axis=0, keepdims=True)
        dg2_ref[...] += jnp.sum(dgt2, axis=0, keepdims=True)

    row = pl.BlockSpec((ts, D), lambda i: (i, 0))
    vec = pl.BlockSpec((1, D), lambda i: (0, 0))
    return _pcall(
        body, name=name, grid=(S // ts,),
        in_specs=[row, vec, row, vec, row, row], out_specs=(row, row, vec, vec),
        out_shape=(jax.ShapeDtypeStruct((S, D), F32), jax.ShapeDtypeStruct((S, D), BF16),
                   jax.ShapeDtypeStruct((1, D), F32), jax.ShapeDtypeStruct((1, D), F32)),
        compiler_params=_params(("arbitrary",)),
    )(x, g1, dy1, g2, dy2, res)


def _pool_pre(x, g, ride=None):
    S, D = x.shape
    ts = _row_tile(S, D)
    gw = D // len(POOL_WINDOWS)
    hb = ts // POOL_HALO

    def body(ins, outs, scratch):
        (x_ref, halo_ref, g_ref), (o_ref,) = ins, outs
        i = pl.program_id(0)
        xx = jnp.concatenate([halo_ref[...], x_ref[...]], axis=0)
        h = xx * _rms_scale(xx) * g_ref[...]
        t = i * ts - POOL_HALO + lax.broadcasted_iota(jnp.int32, (ts + POOL_HALO, 1), 0)
        h = jnp.where(t >= 0, h, 0.0)
        tf = t[POOL_HALO:].astype(F32)
        for gi, w in enumerate(POOL_WINDOWS):
            hg = h[:, gi * gw:(gi + 1) * gw]
            acc, span = hg, 1
            while span < w:
                acc = acc + pltpu.roll(acc, span, 0)
                span *= 2
            count = jnp.minimum(tf + 1.0, float(w))
            diff = acc[POOL_HALO:] / count - hg[POOL_HALO:]
            o_ref[:, gi * gw:(gi + 1) * gw] = diff.astype(o_ref.dtype)

    return _pallas(body, "pool_pre", (S // ts,), [x, x, g],
                   [pl.BlockSpec((ts, D), lambda i: (i, 0)),
                    pl.BlockSpec((POOL_HALO, D), lambda i: (jnp.maximum(i * hb - 1, 0), 0)),
                    pl.BlockSpec((1, D), lambda i: (0, 0))],
                   [pl.BlockSpec((ts, D), lambda i: (i, 0))], [jax.ShapeDtypeStruct((S, D), BF16)],
                   sem=("parallel",), ride=ride)


def _pool_pre_bwd(x, g, dd, res):
    S, D = x.shape
    ts = _row_tile(S, D)
    gw = D // len(POOL_WINDOWS)
    hb = ts // POOL_HALO
    last_halo = S // POOL_HALO - 1

    def body(x_ref, g_ref, dd_ref, halo_ref, res_ref, dx_ref, dg_ref):
        i = pl.program_id(0)
        ee = jnp.concatenate([dd_ref[...], halo_ref[...]], axis=0)
        t = i * ts + lax.broadcasted_iota(jnp.int32, (ts + POOL_HALO, 1), 0)
        ee = jnp.where(t < S, ee, 0.0)
        tf = t.astype(F32)
        n = ts + POOL_HALO
        for gi, w in enumerate(POOL_WINDOWS):
            eg = ee[:, gi * gw:(gi + 1) * gw]
            acc, span = eg / jnp.minimum(tf + 1.0, float(w)), 1
            while span < w:
                acc = acc + pltpu.roll(acc, n - span, 0)
                span *= 2
            dx_ref[:, gi * gw:(gi + 1) * gw] = acc[:ts] - eg[:ts]
        dx, dgt = _rms_bwd(x_ref[...], g_ref[...], dx_ref[...])
        dx_ref[...] = res_ref[...] + dx

        @pl.when(i == 0)
        def _():
            dg_ref[...] = jnp.zeros_like(dg_ref)

        dg_ref[...] += jnp.sum(dgt, axis=0, keepdims=True)

    row = pl.BlockSpec((ts, D), lambda i: (i, 0))
    vec = pl.BlockSpec((1, D), lambda i: (0, 0))
    return _pcall(
        body, name="pool_pre_bwd", grid=(S // ts,),
        in_specs=[row, vec, row,
                  pl.BlockSpec((POOL_HALO, D), lambda i: (jnp.minimum((i + 1) * hb, last_halo), 0)), row],
        out_specs=(row, vec),
        out_shape=(jax.ShapeDtypeStruct((S, D), F32), jax.ShapeDtypeStruct((1, D), F32)),
        compiler_params=_params(("arbitrary",)),
    )(x, g, dd, dd, res)


def _pool_mix(diff, w, scale, x, ride=None):
    S, D = x.shape
    G, gw, _ = w.shape
    ts = _tile(S, ROW_TILE)

    def body(ins, outs, scratch):
        (d_ref, w_ref, s_ref, x_ref), (o_ref,) = ins, outs
        mo = jnp.dot(d_ref[...], w_ref[0], preferred_element_type=F32)
        o_ref[...] = x_ref[...] + mo * s_ref[...]

    blk = pl.BlockSpec((ts, gw), lambda i, g: (i, g))
    return _pallas(body, "pool_mix", (S // ts, G), [diff, w, scale, x],
                   [blk, pl.BlockSpec((1, gw, gw), lambda i, g: (g, 0, 0)), pl.BlockSpec((1, gw), lambda i, g: (0, g)), blk],
                   [blk], [jax.ShapeDtypeStruct((S, D), F32)], sem=("parallel", "parallel"), ride=ride)


def _pool_mix_bwd(diff, w, scale, dy):
    S, D = dy.shape
    G, gw, _ = w.shape
    ts = _tile(S, ROW_TILE)

    def body(d_ref, w_ref, s_ref, dy_ref, dd_ref, dw_ref, ds_ref):
        i = pl.program_id(1)
        d = d_ref[...]
        dy = dy_ref[...]
        mo = jnp.dot(d, w_ref[0], preferred_element_type=F32)
        dmo = (dy * s_ref[...]).astype(BF16)
        dd_ref[...] = lax.dot_general(dmo, w_ref[0], (((1,), (1,)), ((), ())), preferred_element_type=F32)

        @pl.when(i == 0)
        def _():
            dw_ref[...] = jnp.zeros_like(dw_ref)
            ds_ref[...] = jnp.zeros_like(ds_ref)

        dw_ref[0] += lax.dot_general(d, dmo, (((0,), (0,)), ((), ())), preferred_element_type=F32)
        ds_ref[...] += jnp.sum(dy * mo, axis=0, keepdims=True)

    blk = pl.BlockSpec((ts, gw), lambda g, i: (i, g))
    wspec = pl.BlockSpec((1, gw, gw), lambda g, i: (g, 0, 0))
    vec = pl.BlockSpec((1, gw), lambda g, i: (0, g))
    return _pcall(
        body, name="pool_mix_bwd", grid=(G, S // ts),
        in_specs=[blk, wspec, vec, blk],
        out_specs=(blk, wspec, vec),
        out_shape=(jax.ShapeDtypeStruct((S, D), F32), jax.ShapeDtypeStruct((G, gw, gw), F32),
                   jax.ShapeDtypeStruct((1, D), F32)),
        compiler_params=_params(("parallel", "arbitrary")),
    )(diff, w, scale, dy)


def _loss_head(x, g, target):
    S, D = x.shape
    ts = _row_tile(S, D)

    def body(x_ref, g_ref, t_ref, loss_ref, dx_ref, dxb_ref, dg_ref):
        xf = x_ref[...]
        gv = g_ref[...]
        err = xf * _rms_scale(xf) * gv - t_ref[...]
        dx, dgt = _rms_bwd(xf, gv, err * (1.0 / D))
        dx_ref[...] = dx
        dxb_ref[...] = dx.astype(BF16)

        @pl.when(pl.program_id(0) == 0)
        def _():
            loss_ref[...] = jnp.zeros_like(loss_ref)
            dg_ref[...] = jnp.zeros_like(dg_ref)

        part = 0.5 * jnp.sum(jnp.sum(err * err, axis=-1, keepdims=True) * (1.0 / D), axis=0, keepdims=True)
        loss_ref[...] += jnp.broadcast_to(part, loss_ref.shape)
        dg_ref[...] += jnp.sum(dgt, axis=0, keepdims=True)

    row = pl.BlockSpec((ts, D), lambda i: (i, 0))
    vec = pl.BlockSpec((1, D), lambda i: (0, 0))
    return _pcall(
        body, name="loss_head", grid=(S // ts,),
        in_specs=[row, vec, row],
        out_specs=(pl.BlockSpec((1, 128), lambda i: (0, 0)), row, row, vec),
        out_shape=(jax.ShapeDtypeStruct((1, 128), F32), jax.ShapeDtypeStruct((S, D), F32),
                   jax.ShapeDtypeStruct((S, D), BF16), jax.ShapeDtypeStruct((1, D), F32)),
        compiler_params=_params(("arbitrary",)),
    )(x, g, target)


_DIMS = {"nn": (((1,), (0,)), ((), ())), "nt": (((1,), (1,)), ((), ())), "tn": (((0,), (0,)), ((), ()))}


def _mm(name, mode, a, b, tiles, grid, a_map, b_map, o_map, out_shape, out_dtype=F32, res=None, ride=None):
    tm, tn, tk = tiles
    nk = grid[-1]
    has_res = res is not None
    dn = _DIMS[mode]

    def body(ins, outs, scratch):
        o_ref = outs[0]

        def product():
            return lax.dot_general(ins[0][...].astype(BF16), ins[1][...].astype(BF16), dn, preferred_element_type=F32)

        def finish(out):
            if has_res:
                out = ins[2][...] + out
            o_ref[...] = out.astype(o_ref.dtype)

        if nk == 1:
            finish(product())
            return
        acc_ref = scratch[0]
        k = pl.program_id(2)

        @pl.when(k == 0)
        def _():
            acc_ref[...] = jnp.zeros_like(acc_ref)

        acc_ref[...] += product()

        @pl.when(k == nk - 1)
        def _():
            finish(acc_ref[...])

    a_spec = pl.BlockSpec((tk, tm) if mode == "tn" else (tm, tk), a_map)
    b_spec = pl.BlockSpec((tn, tk) if mode == "nt" else (tk, tn), b_map)
    o_spec = pl.BlockSpec((tm, tn), o_map)
    operands, in_specs = [a, b], [a_spec, b_spec]
    if has_res:
        operands.append(res)
        in_specs.append(o_spec)
    outs = _pallas(body, name, grid, operands, in_specs, [o_spec], [jax.ShapeDtypeStruct(out_shape, out_dtype)],
                   scratch=[pltpu.VMEM((tm, tn), F32)] if nk > 1 else [],
                   sem=("parallel", "parallel", "arbitrary"), ride=ride)
    return (outs[0], outs[1:]) if ride else outs[0]


def _mm_plain(name, mode, a, b, out_dtype=F32, res=None):
    if mode == "tn":
        (K, M), N = a.shape, b.shape[1]
    elif mode == "nt":
        (M, K), N = a.shape, b.shape[0]
    else:
        (M, K), N = a.shape, b.shape[1]
    if mode == "tn":
        tm, tn, tk = _tile(M, 2 * MM_TN), _tile(N, 2 * MM_TN), _tile(K, ROW_TILE)
    else:
        tm = _tile(M, ROW_TILE)
        tk = K if K <= 4 * MM_TK else _tile(K, MM_TK)
        tn = N if N * tk * 2 <= (8 << 20) else _tile(N, MM_TN)
    a_map = (lambda i, j, k: (k, i)) if mode == "tn" else (lambda i, j, k: (i, k))
    b_map = (lambda i, j, k: (j, k)) if mode == "nt" else (lambda i, j, k: (k, j))
    return _mm(name, mode, a, b, (tm, tn, tk), (M // tm, N // tn, K // tk), a_map, b_map, lambda i, j, k: (i, j),
               (M, N), out_dtype, res)


def _small_slots_out(name, a, w_slots, out_dtype, epilogue=None):
    S, K = a.shape
    n = w_slots.shape[1]
    tm = _tile(S, ROW_TILE)
    extra = epilogue[1] if epilogue else []

    def body(*refs):
        a_ref, w_ref, o_ref = refs[0], refs[1], refs[-1]
        av = a_ref[...].astype(BF16)
        tables = [r[...] for r in refs[2:-1]]
        for j in range(N_DEV):
            out = jnp.dot(av, w_ref[j], preferred_element_type=F32)
            if epilogue:
                out = epilogue[0](out, *tables)
            o_ref[j] = out.astype(o_ref.dtype)

    return _pcall(
        body, name=name, grid=(S // tm,),
        in_specs=[pl.BlockSpec((tm, K), lambda i: (i, 0)), pl.BlockSpec((N_DEV, K, n), lambda i: (0, 0, 0))]
        + (list(epilogue[2]) if epilogue else []),
        out_specs=pl.BlockSpec((N_DEV, tm, n), lambda i: (0, i, 0)),
        out_shape=jax.ShapeDtypeStruct((N_DEV, S, n), out_dtype),
        compiler_params=_params(("parallel",)),
    )(a, w_slots.reshape(N_DEV, K, n), *extra).reshape(N_DEV * S, n)


def _small_slots_in_nt(name, a_slots, w_slots, S):
    n = a_slots.shape[1]
    K = w_slots.shape[0] // N_DEV
    tm = _tile(S, ROW_TILE)

    def body(a_ref, w_ref, o_ref):
        total = lax.dot_general(a_ref[0], w_ref[0], _DIMS["nt"], preferred_element_type=F32)
        for j in range(1, N_DEV):
            total += lax.dot_general(a_ref[j], w_ref[j], _DIMS["nt"], preferred_element_type=F32)
        o_ref[...] = total

    return _pcall(
        body, name=name, grid=(S // tm,),
        in_specs=[pl.BlockSpec((N_DEV, tm, n), lambda i: (0, i, 0)), pl.BlockSpec((N_DEV, K, n), lambda i: (0, 0, 0))],
        out_specs=pl.BlockSpec((tm, K), lambda i: (i, 0)),
        out_shape=jax.ShapeDtypeStruct((S, K), F32),
        compiler_params=_params(("parallel",)),
    )(a_slots.reshape(N_DEV, S, n), w_slots.reshape(N_DEV, K, n))


def _small_slots_dw(name, a, d_slots, out_dtype):
    S, K = a.shape
    n = d_slots.shape[1]
    tk = _tile(S, ROW_TILE)
    nkb = S // tk

    def body(a_ref, d_ref, o_ref, acc_ref):
        k = pl.program_id(0)
        at = a_ref[...].astype(F32).T.astype(BF16)

        def add(first):
            for j in range(N_DEV):
                part = jnp.dot(at, d_ref[j], preferred_element_type=F32)
                if first:
                    acc_ref[j] = part
                else:
                    acc_ref[j] += part

        @pl.when(k == 0)
        def _():
            add(True)

        @pl.when(k > 0)
        def _():
            add(False)

        @pl.when(k == nkb - 1)
        def _():
            o_ref[...] = acc_ref[...].astype(o_ref.dtype)

    return _pcall(
        body, name=name, grid=(nkb,),
        in_specs=[pl.BlockSpec((tk, K), lambda k: (k, 0)), pl.BlockSpec((N_DEV, tk, n), lambda k: (0, k, 0))],
        out_specs=pl.BlockSpec((N_DEV, K, n), lambda k: (0, 0, 0)),
        out_shape=jax.ShapeDtypeStruct((N_DEV, K, n), out_dtype),
        scratch_shapes=[pltpu.VMEM((N_DEV, K, n), F32)],
        compiler_params=_params(("arbitrary",)),
    )(a, d_slots.reshape(N_DEV, S, n)).reshape(N_DEV * K, n)


def _sigmoid(x):
    return 1.0 / (1.0 + jnp.exp(-x))


def _ffn_up(h, wg, wu, name, ride=None):
    S, D = h.shape
    fs = wg.shape[1]
    tm = _tile(S, ROW_TILE)
    sp = FFN_SLOTS

    def body(ins, outs, scratch):
        h_ref, wg_ref, wu_ref = ins
        g_ref, u_ref, a_ref = outs
        hv = h_ref[...]
        for s in range(sp):
            g = jnp.dot(hv, wg_ref[s], preferred_element_type=F32)
            u = jnp.dot(hv, wu_ref[s], preferred_element_type=F32)
            g_ref[s] = g.astype(g_ref.dtype)
            u_ref[s] = u.astype(u_ref.dtype)
            a_ref[s] = (g * _sigmoid(g) * u).astype(a_ref.dtype)

    w_spec = pl.BlockSpec((sp, D, fs), lambda i, j: (j, 0, 0))
    o_spec = pl.BlockSpec((sp, tm, fs), lambda i, j: (j, i, 0))
    sds = jax.ShapeDtypeStruct((N_DEV, S, fs), BF16)
    g, u, a, *rest = _pallas(body, name, (S // tm, N_DEV // sp), [h, wg.reshape(N_DEV, D, fs), wu.reshape(N_DEV, D, fs)],
                             [pl.BlockSpec((tm, D), lambda i, j: (i, 0)), w_spec, w_spec], [o_spec, o_spec, o_spec],
                             [sds, sds, sds], sem=("parallel", "arbitrary"), ride=ride)
    return [t.reshape(N_DEV * S, fs) for t in (g, u, a)] + rest


def _ffn_dact(dy, wd, g, u, name, ride=None):
    S, D = dy.shape
    fs = g.shape[1]
    tm = _tile(S, ROW_TILE)
    sp = FFN_SLOTS

    def body(ins, outs, scratch):
        dy_ref, wd_ref, g_ref, u_ref = ins
        dg_ref, du_ref = outs
        dy = dy_ref[...]
        for s in range(sp):
            da = lax.dot_general(dy, wd_ref[s], _DIMS["nt"], preferred_element_type=F32)
            g, u = g_ref[s].astype(F32), u_ref[s].astype(F32)
            sig = _sigmoid(g)
            dg_ref[s] = (da * u * (sig * (1.0 + g * (1.0 - sig)))).astype(dg_ref.dtype)
            du_ref[s] = (da * (g * sig)).astype(du_ref.dtype)

    o_spec = pl.BlockSpec((sp, tm, fs), lambda j, i: (j, i, 0))
    sds = jax.ShapeDtypeStruct((N_DEV, S, fs), BF16)
    dg, du, *rest = _pallas(body, name, (N_DEV // sp, S // tm),
                            [dy, wd.reshape(N_DEV, fs, D), g.reshape(N_DEV, S, fs), u.reshape(N_DEV, S, fs)],
                            [pl.BlockSpec((tm, D), lambda j, i: (i, 0)), pl.BlockSpec((sp, fs, D), lambda j, i: (j, 0, 0)),
                             o_spec, o_spec], [o_spec, o_spec], [sds, sds], sem=("parallel", "arbitrary"), ride=ride)
    return [dg.reshape(N_DEV * S, fs), du.reshape(N_DEV * S, fs)] + rest


def _ffn_down(a, wd, x, name):
    S, D = x.shape
    fs = a.shape[1]
    tm = _tile(S, ROW_TILE)
    sp = FFN_SLOTS
    nj = N_DEV // sp

    def body(a_ref, w_ref, x_ref, o_ref, acc_ref):
        j = pl.program_id(1)
        part = jnp.dot(a_ref[0], w_ref[0], preferred_element_type=F32)
        for s in range(1, sp):
            part += jnp.dot(a_ref[s], w_ref[s], preferred_element_type=F32)

        @pl.when(j == 0)
        def _():
            acc_ref[...] = x_ref[...] + part

        @pl.when(j > 0)
        def _():
            acc_ref[...] += part

        @pl.when(j == nj - 1)
        def _():
            o_ref[...] = acc_ref[...]

    row = pl.BlockSpec((tm, D), lambda i, j: (i, 0))
    return _pcall(
        body, name=name, grid=(S // tm, nj),
        in_specs=[pl.BlockSpec((sp, tm, fs), lambda i, j: (j, i, 0)), pl.BlockSpec((sp, fs, D), lambda i, j: (j, 0, 0)), row],
        out_specs=row, out_shape=jax.ShapeDtypeStruct((S, D), F32),
        scratch_shapes=[pltpu.VMEM((tm, D), F32)],
        compiler_params=_params(("parallel", "arbitrary")),
    )(a.reshape(N_DEV, S, fs), wd.reshape(N_DEV, fs, D), x)


def _ffn_dh(dg, du, wg, wu, S, name, ride=None):
    fs = dg.shape[1]
    D = wg.shape[0] // N_DEV
    tm = _tile(S, ROW_TILE)
    sp = FFN_SLOTS
    nj = N_DEV // sp

    def body(ins, outs, scratch):
        dg_ref, du_ref, wg_ref, wu_ref = ins
        o_ref, acc_ref = outs[0], scratch[0]
        j = pl.program_id(1)
        part = None
        for s in range(sp):
            for d_ref, w_ref in ((dg_ref, wg_ref), (du_ref, wu_ref)):
                term = lax.dot_general(d_ref[s], w_ref[s], _DIMS["nt"], preferred_element_type=F32)
                part = term if part is None else part + term

        @pl.when(j == 0)
        def _():
            acc_ref[...] = part

        @pl.when(j > 0)
        def _():
            acc_ref[...] += part

        @pl.when(j == nj - 1)
        def _():
            o_ref[...] = acc_ref[...]

    d_spec = pl.BlockSpec((sp, tm, fs), lambda i, j: (j, i, 0))
    w_spec = pl.BlockSpec((sp, D, fs), lambda i, j: (j, 0, 0))
    return _pallas(body, name, (S // tm, nj),
                   [dg.reshape(N_DEV, S, fs), du.reshape(N_DEV, S, fs), wg.reshape(N_DEV, D, fs), wu.reshape(N_DEV, D, fs)],
                   [d_spec, d_spec, w_spec, w_spec], [pl.BlockSpec((tm, D), lambda i, j: (i, 0))],
                   [jax.ShapeDtypeStruct((S, D), F32)], scratch=[pltpu.VMEM((tm, D), F32)],
                   sem=("parallel", "arbitrary"), ride=ride)


def _slots_dw_t(name, at, d_slots, ride=None):
    K, S = at.shape
    n = d_slots.shape[1]
    tk = _tile(S, 2 * ROW_TILE)
    nkb = S // tk
    return _mm(name, "nn", at, d_slots, (K, n, tk), (N_DEV, 1, nkb),
               lambda j, q, k: (0, k), lambda j, q, k: (j * nkb + k, 0), lambda j, q, k: (j, 0),
               (N_DEV * K, n), BF16, ride=ride)


def _rope_heads(q, cos_t, sin_t):
    parts = []
    for c0 in range(0, q.shape[1], HEAD_PAD):
        parts += [q[:, c0:c0 + QK_NOPE_DIM], _rope(q[:, c0 + QK_NOPE_DIM:c0 + HEAD_PAD], cos_t, sin_t)]
    return jnp.concatenate(parts, axis=1)


def _kv_mid(kv, g, cos_t, sin_t):
    S, W = kv.shape
    R = W - 128
    ts = _tile(S, ROW_TILE)

    def body(kv_ref, g_ref, c_ref, s_ref, c_out, k_out):
        cf = kv_ref[:, :R]
        c_out[...] = (cf * _rms_scale(cf) * g_ref[...]).astype(c_out.dtype)
        k_out[...] = _rope(kv_ref[:, R:], c_ref[...], s_ref[...]).astype(k_out.dtype)

    tab = pl.BlockSpec((ts, 128), lambda i: (i, 0))
    return _pcall(
        body, name="kv_mid", grid=(S // ts,),
        in_specs=[pl.BlockSpec((ts, W), lambda i: (i, 0)), pl.BlockSpec((1, R), lambda i: (0, 0)), tab, tab],
        out_specs=(pl.BlockSpec((ts, R), lambda i: (i, 0)), tab),
        out_shape=(jax.ShapeDtypeStruct((S, R), BF16), jax.ShapeDtypeStruct((S, 128), BF16)),
        compiler_params=_params(("parallel",)),
    )(kv, g, cos_t, sin_t)


def _kv_mid_bwd(kv, g, dc, dkpe, cos_t, sin_t):
    S, W = kv.shape
    R = W - 128
    H = dkpe.shape[0]
    ts = _row_tile(S, H * 128)

    def body(kv_ref, g_ref, dc_ref, dk_ref, c_ref, s_ref, dkv_ref, dg_ref):
        dx, dgt = _rms_bwd(kv_ref[:, :R], g_ref[...], dc_ref[...])
        dkv_ref[:, :R] = dx.astype(dkv_ref.dtype)
        dk = dk_ref[0]
        for h in range(1, H):
            dk = dk + dk_ref[h]
        dkv_ref[:, R:] = _rope(dk, c_ref[...], -s_ref[...]).astype(dkv_ref.dtype)

        @pl.when(pl.program_id(0) == 0)
        def _():
            dg_ref[...] = jnp.zeros_like(dg_ref)

        dg_ref[...] += jnp.sum(dgt, axis=0, keepdims=True)

    tab = pl.BlockSpec((ts, 128), lambda i: (i, 0))
    vec = pl.BlockSpec((1, R), lambda i: (0, 0))
    return _pcall(
        body, name="kv_mid_bwd", grid=(S // ts,),
        in_specs=[pl.BlockSpec((ts, W), lambda i: (i, 0)), vec, pl.BlockSpec((ts, R), lambda i: (i, 0)),
                  pl.BlockSpec((H, ts, 128), lambda i: (0, i, 0)), tab, tab],
        out_specs=(pl.BlockSpec((ts, W), lambda i: (i, 0)), vec),
        out_shape=(jax.ShapeDtypeStruct((S, W), BF16), jax.ShapeDtypeStruct((1, R), F32)),
        compiler_params=_params(("arbitrary",)),
    )(kv, g, dc, dkpe, cos_t, sin_t)


def _block_pairs(nb, by_key):
    pairs = ([(qi, ki) for ki in range(nb) for qi in range(ki, nb)] if by_key
             else [(qi, ki) for qi in range(nb) for ki in range(qi + 1)])
    return jnp.asarray(np.array([p[0] for p in pairs], np.int32)), jnp.asarray(np.array([p[1] for p in pairs], np.int32))


def _causal_mask(blk, transposed=False, rows=None, row0=0):
    shape = (blk if rows is None else rows, blk)
    r = lax.broadcasted_iota(jnp.int32, shape, 0) + row0
    c = lax.broadcasted_iota(jnp.int32, shape, 1)
    return (r <= c) if transposed else (c <= r)


def _attn_specs(S, blk, hpd):
    nb = S // blk
    w = hpd * HEAD_PAD
    return dict(
        q=pl.BlockSpec((blk, w), lambda j, t, qt, kt: (j * nb + qt[t], 0)),
        kv=pl.BlockSpec((blk, w), lambda j, t, qt, kt: (j * nb + kt[t], 0)),
        kp=pl.BlockSpec((blk, 128), lambda j, t, qt, kt: (kt[t], 0)),
        do=pl.BlockSpec((blk, hpd * V_DIM), lambda j, t, qt, kt: (qt[t], j)),
        col=pl.BlockSpec((hpd, blk, 1), lambda j, t, qt, kt: (j, qt[t], 0)),
        row=pl.BlockSpec((hpd, 1, blk), lambda j, t, qt, kt: (j, 0, qt[t])))


def _head_k(kv_ref, kp, hh):
    return jnp.concatenate([kv_ref[:, hh * HEAD_PAD:hh * HEAD_PAD + QK_NOPE_DIM], kp], axis=1)


def _head_v(kv_ref, hh):
    return kv_ref[:, hh * HEAD_PAD + QK_NOPE_DIM:(hh + 1) * HEAD_PAD]


def _attn_fwd(q, kvu, kpe, H, ride=None):
    S = kpe.shape[0]
    hpd = H // N_DEV
    blk = _tile(S, ATT_BLOCK)
    q_tab, k_tab = _block_pairs(S // blk, by_key=False)
    rc = min(blk, ATT_ROW_CHUNK)

    def body(ins, outs, scratch):
        qt, kt, q_ref, kv_ref, kp_ref = ins
        o_ref, lse_ref = outs
        m_ref, acc_ref = scratch
        t = pl.program_id(1)
        qi, ki = qt[t], kt[t]

        @pl.when(ki == 0)
        def _():
            m_ref[...] = jnp.full_like(m_ref, -jnp.inf)
            acc_ref[...] = jnp.zeros_like(acc_ref)

        def step(masked):
            kp = kp_ref[...]
            ones = jnp.ones((blk, 128), BF16)
            for hh in range(hpd):
                k = _head_k(kv_ref, kp, hh)
                v_ext = jnp.concatenate([_head_v(kv_ref, hh), ones], axis=1)
                for r0 in range(0, blk, rc):
                    rows = slice(r0, r0 + rc)
                    s = lax.dot_general(q_ref[rows, hh * HEAD_PAD:(hh + 1) * HEAD_PAD], k, _DIMS["nt"],
                                        preferred_element_type=F32)
                    if masked:
                        s = jnp.where(_causal_mask(blk, rows=rc, row0=r0), s, -jnp.inf)
                    m_old = m_ref[hh, rows]
                    m_new = jnp.maximum(m_old, jnp.max(s, axis=-1, keepdims=True))
                    p = jnp.exp2((s - m_new) * EXP2_SCALE).astype(BF16)
                    acc_ref[hh, rows] = (jnp.exp2((m_old - m_new) * EXP2_SCALE) * acc_ref[hh, rows]
                                         + jnp.dot(p, v_ext, preferred_element_type=F32))
                    m_ref[hh, rows] = m_new

        @pl.when(ki < qi)
        def _():
            step(False)

        @pl.when(ki == qi)
        def _():
            step(True)
            for hh in range(hpd):
                l = acc_ref[hh, :, V_DIM:]
                o_ref[:, hh * V_DIM:(hh + 1) * V_DIM] = (acc_ref[hh, :, :V_DIM] / l).astype(o_ref.dtype)
                lse_ref[hh] = m_ref[hh] * EXP2_SCALE + jnp.log2(l[:, :1])

    sp = _attn_specs(S, blk, hpd)
    return _pallas(body, "attn_fwd", (N_DEV, q_tab.shape[0]), [q, kvu, kpe], [sp["q"], sp["kv"], sp["kp"]],
                   [sp["do"], sp["col"]],
                   [jax.ShapeDtypeStruct((S, H * V_DIM), BF16), jax.ShapeDtypeStruct((H, S, 1), F32)],
                   scratch=[pltpu.VMEM((hpd, blk, 1), F32), pltpu.VMEM((hpd, blk, 2 * V_DIM), F32)],
                   sem=("parallel", "arbitrary"), ride=ride, prefetch=[q_tab, k_tab])


def _attn_bwd(q, kvu, kpe, do, o, lse_row, cos_t, sin_t, H):
    S = kpe.shape[0]
    hpd = H // N_DEV
    blk = _tile(S, ATT_BLOCK)
    nb = S // blk
    q_tab, k_tab = _block_pairs(nb, by_key=True)

    def body(qt, kt, q_ref, kv_ref, kp_ref, do_ref, o_ref, lse_ref, c_ref, s_ref, dq_ref, dkv_ref, dkp_ref,
             dq_acc, dk_acc, dv_acc):
        t = pl.program_id(1)
        qi, ki = qt[t], kt[t]
        q_rows = pl.ds(pl.multiple_of(qi * blk, blk), blk)

        def step(diag):
            kp = kp_ref[...]
            ones = jnp.ones((8, V_DIM), BF16)
            for hh in range(hpd):
                k = _head_k(kv_ref, kp, hh)
                qv = q_ref[:, hh * HEAD_PAD:(hh + 1) * HEAD_PAD]
                cols = slice(hh * V_DIM, (hh + 1) * V_DIM)
                dov = do_ref[:, cols]
                doo = (dov.astype(F32) * o_ref[:, cols].astype(F32)).astype(BF16)
                delta = lax.dot_general(ones, doo, _DIMS["nt"], preferred_element_type=F32)[:1]
                st = lax.dot_general(k, qv, _DIMS["nt"], preferred_element_type=F32)
                pt = jnp.exp2(st * EXP2_SCALE - lse_ref[hh])
                if diag:
                    pt = jnp.where(_causal_mask(blk, transposed=True), pt, 0.0)
                dv_new = jnp.dot(pt.astype(BF16), dov, preferred_element_type=F32)
                dpt = lax.dot_general(_head_v(kv_ref, hh), dov, _DIMS["nt"], preferred_element_type=F32)
                dst = (pt * (dpt - delta)).astype(BF16)
                dk_new = jnp.dot(dst, qv, preferred_element_type=F32)
                dq_new = lax.dot_general(dst, k, _DIMS["tn"], preferred_element_type=F32)
                if diag:
                    dv_acc[hh] = dv_new
                    dk_acc[hh] = dk_new
                else:
                    dv_acc[hh] += dv_new
                    dk_acc[hh] += dk_new

                @pl.when(ki == 0)
                def _():
                    dq_acc[hh, q_rows] = dq_new

                @pl.when(ki > 0)
                def _():
                    dq_acc[hh, q_rows] += dq_new

        @pl.when(qi == ki)
        def _():
            step(True)
            for hh in range(hpd):
                c0 = hh * HEAD_PAD
                dq = dq_acc[hh, q_rows] * ATTN_SCALE
                dq_ref[:, c0:c0 + QK_NOPE_DIM] = dq[:, :QK_NOPE_DIM].astype(dq_ref.dtype)
                dq_ref[:, c0 + QK_NOPE_DIM:c0 + HEAD_PAD] = _rope(dq[:, QK_NOPE_DIM:], c_ref[...],
                                                                  -s_ref[...]).astype(dq_ref.dtype)

        @pl.when(qi > ki)
        def _():
            step(False)

        @pl.when(qi == nb - 1)
        def _():
            for hh in range(hpd):
                c0 = hh * HEAD_PAD
                dkv_ref[:, c0:c0 + QK_NOPE_DIM] = (dk_acc[hh, :, :QK_NOPE_DIM] * ATTN_SCALE).astype(dkv_ref.dtype)
                dkv_ref[:, c0 + QK_NOPE_DIM:c0 + HEAD_PAD] = dv_acc[hh].astype(dkv_ref.dtype)
                dkp_ref[hh] = dk_acc[hh, :, QK_NOPE_DIM:] * ATTN_SCALE

    sp = _attn_specs(S, blk, hpd)
    key_tab = pl.BlockSpec((blk, 128), lambda j, t, qt, kt: (kt[t], 0))
    grid_spec = pltpu.PrefetchScalarGridSpec(
        num_scalar_prefetch=2, grid=(N_DEV, q_tab.shape[0]),
        in_specs=[sp["q"], sp["kv"], sp["kp"], sp["do"], sp["do"], sp["row"], key_tab, key_tab],
        out_specs=(sp["kv"], sp["kv"], pl.BlockSpec((hpd, blk, 128), lambda j, t, qt, kt: (j, kt[t], 0))),
        scratch_shapes=[pltpu.VMEM((hpd, S, HEAD_PAD), F32), pltpu.VMEM((hpd, blk, HEAD_PAD), F32),
                        pltpu.VMEM((hpd, blk, V_DIM), F32)])
    return _pcall(
        body, name="attn_bwd", grid_spec=grid_spec,
        out_shape=(jax.ShapeDtypeStruct(q.shape, BF16), jax.ShapeDtypeStruct(kvu.shape, BF16),
                   jax.ShapeDtypeStruct((H, S, 128), F32)),
        compiler_params=_params(("parallel", "arbitrary")),
    )(q_tab, k_tab, q, kvu, kpe, do, o, lse_row, cos_t, sin_t)


def _mesh_pos():
    return lax.axis_index("x"), lax.axis_index("y"), lax.axis_index("c")


def _flip(pos, k):
    x, y, c = pos
    return (1 - x if k & 4 else x, 1 - y if k & 2 else y, 1 - c if k & 1 else c)


def _rank(pos):
    return 4 * pos[0] + 2 * pos[1] + pos[2]


def _copies(n, src_of, dst_refs, local_src_of, send_sems, recv_sems, local_sems):
    me = _mesh_pos()
    local = [pltpu.make_async_copy(local_src_of(a), dst_refs[a].at[_rank(me)], local_sems.at[a]) for a in range(n)]
    sends, arrivals = [], []
    for k in range(1, N_DEV):
        peer = _flip(me, k)
        for a in range(n):
            idx = a * (N_DEV - 1) + k - 1
            for into, group in ((me, sends), (peer, arrivals)):
                group.append(pltpu.make_async_remote_copy(
                    src_ref=src_of(a, peer), dst_ref=dst_refs[a].at[_rank(into)],
                    send_sem=send_sems.at[idx], recv_sem=recv_sems.at[idx],
                    device_id=peer, device_id_type=pl.DeviceIdType.MESH))
    return local, sends, arrivals


def _exchange_start(*args):
    local, sends, _ = _copies(*args)
    for cp in local + sends:
        cp.start()


def _exchange_wait(*args):
    local, sends, arrivals = _copies(*args)
    for cp in arrivals:
        cp.wait_recv()
    for cp in sends:
        cp.wait_send()
    for cp in local:
        cp.wait()


def _exchange(*args):
    _exchange_start(*args)
    _exchange_wait(*args)


def _sems(n):
    return [pltpu.SemaphoreType.DMA((n * (N_DEV - 1),)), pltpu.SemaphoreType.DMA((n * (N_DEV - 1),)),
            pltpu.SemaphoreType.DMA((n,))]


_ANY = pl.BlockSpec(memory_space=pl.ANY)


class _Ride:
    def __init__(self, kind, arrays):
        self.kind, self.arrays, self.n = kind, list(arrays), len(arrays)

    def out_shape(self):
        lead = () if self.kind == "scatter" else (N_DEV,)
        return [jax.ShapeDtypeStruct(lead + a.shape, a.dtype) for a in self.arrays]

    def _args(self, x_refs, out_refs, sems):
        if self.kind == "gather":
            return (self.n, lambda a, peer: x_refs[a], out_refs, lambda a: x_refs[a]) + tuple(sems)
        me = _mesh_pos()
        return (self.n, lambda a, peer: x_refs[a].at[_rank(peer)], out_refs, lambda a: x_refs[a].at[_rank(me)]) + tuple(sems)

    def start(self, x_refs, out_refs, sems):
        if self.kind == "gather2":
            _two_level_gather(x_refs, out_refs, *sems)[0]()
        else:
            _exchange_start(*self._args(x_refs, out_refs, sems))

    def wait(self, x_refs, out_refs, sems):
        if self.kind == "gather2":
            _two_level_gather(x_refs, out_refs, *sems)[1]()
        else:
            _exchange_wait(*self._args(x_refs, out_refs, sems))


def _pallas(body, name, grid, operands, in_specs, out_specs, out_shape, scratch=(), sem=None, ride=None, prefetch=()):
    n_pf, n_in, n_out, n_scr = len(prefetch), len(in_specs), len(out_specs), len(scratch)
    nr = ride.n if ride else 0

    def wrapped(*refs):
        refs = list(refs)
        pf, refs = refs[:n_pf], refs[n_pf:]
        ins, r_in = refs[:n_in], refs[n_in:n_in + nr]
        outs, r_out = refs[n_in + nr:n_in + nr + n_out], refs[n_in + nr + n_out:n_in + 2 * nr + n_out]
        rest = refs[n_in + 2 * nr + n_out:]
        scr, sems = rest[:n_scr], rest[n_scr:]
        if ride:
            first, last = None, None
            for d, size in enumerate(grid):
                i = pl.program_id(d)
                first = (i == 0) if first is None else jnp.logical_and(first, i == 0)
                last = (i == size - 1) if last is None else jnp.logical_and(last, i == size - 1)

            @pl.when(first)
            def _():
                ride.start(r_in, r_out, sems)

        body(pf + ins, outs, scr)
        if ride:
            @pl.when(last)
            def _():
                ride.wait(r_in, r_out, sems)

    grid_spec = pltpu.PrefetchScalarGridSpec(
        num_scalar_prefetch=n_pf, grid=grid,
        in_specs=list(in_specs) + [_ANY] * nr, out_specs=tuple(list(out_specs) + [_ANY] * nr),
        scratch_shapes=list(scratch) + (_sems(nr) if ride else []))
    sem = tuple(sem) if sem is not None and not ride else ("arbitrary",) * len(grid)
    return list(_pcall(
        wrapped, name=name, grid_spec=grid_spec,
        out_shape=tuple(list(out_shape) + (ride.out_shape() if ride else [])),
        compiler_params=_params(sem),
    )(*(list(prefetch) + list(operands) + (ride.arrays if ride else []))))


def _two_level_gather(x_refs, out_refs, send_sems, recv_sems, local_sems):
    n = len(x_refs)
    x, y, c = _mesh_pos()
    me, sibling = (x, y, c), (x, y, 1 - c)
    chips = [(1 - x, y), (x, 1 - y), (1 - x, 1 - y)]

    def copy(a, k, block, to, src=None):
        rows = out_refs[a].at[_rank(block)]
        return pltpu.make_async_remote_copy(
            src_ref=rows if src is None else src, dst_ref=rows,
            send_sem=send_sems.at[a * (N_DEV - 1) + k], recv_sem=recv_sems.at[a * (N_DEV - 1) + k],
            device_id=to, device_id_type=pl.DeviceIdType.MESH)

    def own_copies():
        mine = [pltpu.make_async_copy(x_refs[a], out_refs[a].at[_rank(me)], local_sems.at[a]) for a in range(n)]
        first = []
        for a in range(n):
            first.append(copy(a, 0, me, sibling, src=x_refs[a]))
            first += [copy(a, 1 + j, me, (*chip, c), src=x_refs[a]) for j, chip in enumerate(chips)]
        return mine, first

    def start():
        mine, first = own_copies()
        for cp in mine + first:
            cp.start()

    def finish():
        mine, first = own_copies()
        passed = []
        for j, chip in enumerate(chips):
            for a in range(n):
                copy(a, 1 + j, (*chip, c), me).wait_recv()
                passed.append(copy(a, 4 + j, (*chip, c), sibling))
                passed[-1].start()
        for a in range(n):
            copy(a, 0, sibling, me).wait_recv()
            for j, chip in enumerate(chips):
                copy(a, 4 + j, (*chip, 1 - c), me).wait_recv()
        for cp in first + passed:
            cp.wait_send()
        for cp in mine:
            cp.wait()

    return start, finish


def _all_gather(shards, name):
    n = len(shards)

    def body(*refs):
        start, finish = _two_level_gather(refs[:n], refs[n:2 * n], *refs[2 * n:])
        start()
        finish()

    return _pcall(
        body, name=name, in_specs=[_ANY] * n, out_specs=tuple([_ANY] * n),
        out_shape=tuple(jax.ShapeDtypeStruct((N_DEV,) + s.shape, s.dtype) for s in shards),
        scratch_shapes=_sems(n),
    )(*shards)


def _all_to_all(parts, name):
    n = len(parts)

    def body(*refs):
        x_refs, out_refs = refs[:n], refs[n:2 * n]
        me = _mesh_pos()
        _exchange(n, lambda a, peer: x_refs[a].at[_rank(peer)], out_refs, lambda a: x_refs[a].at[_rank(me)],
                  *refs[2 * n:])

    return _pcall(
        body, name=name, in_specs=[_ANY] * n, out_specs=tuple([_ANY] * n),
        out_shape=tuple(jax.ShapeDtypeStruct(p.shape, p.dtype) for p in parts),
        scratch_shapes=_sems(n),
    )(*parts)


def _all_reduce_small(v):
    R, C = v.shape

    def body(x_ref, out_ref, buf_ref, send_sems, recv_sems, local_sems):
        _exchange(1, lambda a, peer: x_ref, [buf_ref], lambda a: x_ref, send_sems, recv_sems, local_sems)
        total = buf_ref[0]
        for j in range(1, N_DEV):
            total = total + buf_ref[j]
        out_ref[...] = total

    vm = pl.BlockSpec(memory_space=pltpu.VMEM)
    return _pcall(
        body, name="all_reduce_small", in_specs=[vm], out_specs=vm,
        out_shape=jax.ShapeDtypeStruct((R, C), F32),
        scratch_shapes=[pltpu.VMEM((N_DEV, R, C), F32)] + _sems(1),
    )(v)


def _sum_slots(layers, name):
    L = len(layers)
    shape = layers[0].shape[1:]
    C = shape[-1]
    R = layers[0].size // (N_DEV * C)
    tr = R
    while N_DEV * tr * C * 4 > (4 << 20) and tr % 32 == 0:
        tr //= 2

    def body(*refs):
        o_ref = refs[L]
        for li in range(L):
            @pl.when(pl.program_id(0) == li)
            def _():
                total = refs[li][0].astype(F32)
                for j in range(1, N_DEV):
                    total = total + refs[li][j].astype(F32)
                o_ref[0] = total

    in_specs = [pl.BlockSpec((N_DEV, tr, C), lambda l, i, li=li: (0, jnp.where(l == li, i, 0), 0)) for li in range(L)]
    return _pcall(
        body, name=name, grid=(L, R // tr),
        in_specs=in_specs,
        out_specs=pl.BlockSpec((1, tr, C), lambda l, i: (l, i, 0)),
        out_shape=jax.ShapeDtypeStruct((L, R, C), F32),
        compiler_params=_params(("parallel", "parallel")),
    )(*[p.reshape(N_DEV, R, C) for p in layers]).reshape((L,) + shape)


def _adamw(w, g, m, v, name):
    shape = w.shape
    C = shape[-1]
    R = w.size // C
    tr = R
    while tr * C * 4 > (1 << 20) and tr % 16 == 0:
        tr //= 2

    def body(w_ref, g_ref, m_ref, v_ref, d_ref, nm_ref, nv_ref):
        gv = g_ref[...]
        nm = ADAM_B1 * m_ref[...] + (1.0 - ADAM_B1) * gv
        nv = ADAM_B2 * v_ref[...] + (1.0 - ADAM_B2) * (gv * gv)
        m_hat = nm / (1.0 - ADAM_B1 ** ADAM_STEP)
        v_hat = nv / (1.0 - ADAM_B2 ** ADAM_STEP)
        d_ref[...] = -ADAM_LR * (m_hat / (jnp.sqrt(v_hat) + ADAM_EPS) + ADAM_WD * w_ref[...])
        nm_ref[...] = nm
        nv_ref[...] = nv

    blk = pl.BlockSpec((tr, C), lambda i: (i, 0))
    sds = jax.ShapeDtypeStruct((R, C), F32)
    outs = _pcall(
        body, name=name, grid=(R // tr,),
        in_specs=[blk] * 4, out_specs=(blk,) * 3, out_shape=(sds,) * 3,
        compiler_params=_params(("parallel",)),
    )(*(t.reshape(R, C) for t in (w, g, m, v)))
    return tuple(o.reshape(shape) for o in outs)


_REPLICATED = ("kv_in_norm", "kv_latent_norm", "attn_norm", "q_latent_norm", "ffn_norm", "final_norm")
_WEIGHTS = ("pool_norm", "pool_w", "pool_scale", "kv_in_norm", "w_kv_a", "kv_latent_norm", "w_kv_b", "attn_norm",
            "w_q_a", "q_latent_norm", "w_q_b", "w_o", "ffn_norm", "w_gate", "w_up", "w_down", "final_norm")


def _ffn_fwd(x, norm_g, wg, wu, wd, tag, gather=None):
    S, D = x.shape
    fs = wg.shape[1]
    h, ht = _rmsnorm(x, norm_g, "ffn_norm" + tag, transposed_too=True)
    g, u, a, *gathered = _ffn_up(h, wg, wu, "ffn_up" + tag, ride=gather)
    if gather is not None:
        wd = gathered[0].reshape(N_DEV * fs, D)
    y = _ffn_down(a, wd, x, "ffn_down" + tag)
    return y, (ht, g, u, a), wd, gathered


def _ffn_bwd(x, norm_g, wg, wu, wd, saved, dy, dyb, tag, with_bf16, scatter=None):
    S, D = x.shape
    ht, g, u, a = saved
    fs = g.shape[1]
    tk = _tile(S, 2 * ROW_TILE)
    nkb = S // tk
    dg, du, *got_rider = _ffn_dact(dyb, wd, g, u, "ffn_dact" + tag, ride=scatter)
    dwd = _mm("ffn_dwd" + tag, "tn", a, dyb, (fs, D, tk), (N_DEV, 1, nkb),
              lambda j, q, k: (j * nkb + k, 0), lambda j, q, k: (k, 0), lambda j, q, k: (j, 0), (N_DEV * fs, D), BF16)
    dwg, (got_dwd,) = _slots_dw_t("ffn_dwg" + tag, ht, dg, ride=_Ride("scatter", [dwd.reshape(N_DEV, fs, D)]))
    dwu, (got_dwg,) = _slots_dw_t("ffn_dwu" + tag, ht, du, ride=_Ride("scatter", [dwg.reshape(N_DEV, D, fs)]))
    dh, got_dwu = _ffn_dh(dg, du, wg, wu, S, "ffn_dh" + tag, ride=_Ride("scatter", [dwu.reshape(N_DEV, D, fs)]))
    outs = _rmsnorm_bwd(x, norm_g, dh, dy, "ffn_norm_bwd" + tag, with_bf16=with_bf16)
    return outs, got_rider, (got_dwg, got_dwu, got_dwd)


def kernel(x, positions, pool_norm, pool_w, pool_scale, kv_in_norm, w_kv_a, kv_latent_norm, w_kv_b, attn_norm, w_q_a, q_latent_norm, w_q_b, w_o, ffn_norm, w_gate, w_up, w_down, final_norm, loss_target, m_pool_norm, m_pool_w, m_pool_scale, m_kv_in_norm, m_w_kv_a, m_kv_latent_norm, m_w_kv_b, m_attn_norm, m_w_q_a, m_q_latent_norm, m_w_q_b, m_w_o, m_ffn_norm, m_w_gate, m_w_up, m_w_down, m_final_norm, v_pool_norm, v_pool_w, v_pool_scale, v_kv_in_norm, v_w_kv_a, v_kv_latent_norm, v_w_kv_b, v_attn_norm, v_w_q_a, v_q_latent_norm, v_w_q_b, v_w_o, v_ffn_norm, v_w_gate, v_w_up, v_w_down, v_final_norm):
    env = dict(locals())
    weights = {n: env[n] for n in _WEIGHTS}
    m_in = {n: env["m_" + n] for n in _WEIGHTS}
    v_in = {n: env["v_" + n] for n in _WEIGHTS}

    S, D = x.shape[1], x.shape[2]
    xs = x.reshape(S, D)
    target = loss_target.reshape(S, D)
    G, gws, gw = pool_w.shape[1:]
    Ds = w_kv_a.shape[0]
    R_kv, kvw = w_kv_b.shape
    hpd = kvw // (QK_NOPE_DIM + V_DIM)
    H = hpd * N_DEV
    R_q = w_q_a.shape[2]
    fs = w_gate.shape[2]
    bf = lambda t: t.astype(BF16)

    gains = jnp.concatenate([pool_norm, pool_scale], axis=0)
    g_pool_w, g_gains = _all_gather([bf(pool_w[0]), gains], "gather_first")
    wqb_pad = jnp.pad(w_q_b.reshape(R_q, hpd, QK_DIM), ((0, 0), (0, 0), (0, HEAD_PAD - QK_DIM))).reshape(R_q, hpd * HEAD_PAD)
    gather_mla = _Ride("gather", [bf(w_down[0]), bf(jnp.pad(w_kv_a, ((0, 0), (0, 128 - QK_ROPE_DIM)))), bf(w_kv_b),
                                  bf(w_q_a[0]), bf(wqb_pad), bf(w_o[0])])
    gather_ffn1 = _Ride("gather", [bf(w_gate[1]), bf(w_up[1]), bf(w_down[1])])

    wp = jnp.swapaxes(g_pool_w, 0, 1).reshape(G, gw, gw)
    gains_full = jnp.swapaxes(g_gains, 0, 1).reshape(2, D)
    g_pool_norm, g_pool_scale = gains_full[0:1], gains_full[1:2]

    g_kv_in = kv_in_norm.reshape(1, D)
    g_kv_lat = kv_latent_norm.reshape(1, R_kv)
    g_attn = attn_norm.reshape(1, D)
    g_q_lat = q_latent_norm.reshape(1, R_q)
    g_final = final_norm.reshape(1, D)

    half = QK_ROPE_DIM // 2
    inv_freq = ROPE_BASE ** (-jnp.arange(half, dtype=F32) / half)
    ang = positions.reshape(S).astype(F32)[:, None] * inv_freq
    cos, sin = jnp.cos(ang), jnp.sin(ang)
    zeros = jnp.zeros((S, 128 - QK_ROPE_DIM), F32)
    cos_t = jnp.concatenate([cos, cos, zeros], axis=1)
    sin_t = jnp.concatenate([-sin, sin, zeros], axis=1)

    diff, g_wg0 = _pool_pre(xs, g_pool_norm, ride=_Ride("gather2", [bf(w_gate[0])]))
    x1, g_wu0 = _pool_mix(diff, wp, g_pool_scale, xs, ride=_Ride("gather2", [bf(w_up[0])]))
    wg0, wu0 = g_wg0.reshape(N_DEV * D, fs), g_wu0.reshape(N_DEV * D, fs)
    x2, ffn0, wd0, (_, g_wkva, g_wkvb, g_wqa, g_wqb, g_wo) = _ffn_fwd(x1, ffn_norm[0:1], wg0, wu0, None, "0", gather_mla)
    wkva = g_wkva.reshape(D, R_kv + 128)
    wkvb = g_wkvb.reshape(N_DEV * R_kv, kvw)
    wqa = g_wqa.reshape(D, R_q)
    wqb = g_wqb.reshape(N_DEV * R_q, hpd * HEAD_PAD)
    wo = g_wo.reshape(H * V_DIM, D)

    hk, ha = _rmsnorm_pair(x2, g_kv_in, g_attn, "kv_attn_norm")
    kv = _mm_plain("kv_a", "nn", hk, wkva)
    ckv, kpe = _kv_mid(kv, g_kv_lat, cos_t, sin_t)
    kvu = _small_slots_out("kv_b", ckv, wkvb, BF16)

    ql = _mm_plain("q_a", "nn", ha, wqa)
    qn = _rmsnorm(ql, g_q_lat, "q_latent_norm")
    tab = pl.BlockSpec((_tile(S, ROW_TILE), 128), lambda i: (i, 0))
    qr = _small_slots_out("q_b", qn, wqb, BF16, epilogue=(_rope_heads, [cos_t, sin_t], [tab, tab]))
    o, lse, g_wg1, g_wu1, g_wd1 = _attn_fwd(qr, kvu, kpe, H, ride=gather_ffn1)
    wg1, wu1, wd1 = g_wg1.reshape(N_DEV * D, fs), g_wu1.reshape(N_DEV * D, fs), g_wd1.reshape(N_DEV * fs, D)
    x3 = _mm_plain("attn_out", "nn", o, wo, res=x2)
    x4, ffn1, _, _ = _ffn_fwd(x3, ffn_norm[1:2], wg1, wu1, wd1, "1")

    loss_part, dx4, dx4b, d_final = _loss_head(x4, g_final, target)

    (dx3, dx3b, d_ffn1), _, got_ffn1 = _ffn_bwd(x3, ffn_norm[1:2], wg1, wu1, wd1, ffn1, dx4, dx4b, "1", True)

    do = _mm_plain("attn_out_dx", "nt", dx3b, wo, out_dtype=BF16)
    dwo = _mm_plain("attn_out_dw", "tn", o, dx3b, out_dtype=BF16)
    dq, dkvu, dkpe = _attn_bwd(qr, kvu, kpe, do, o, lse.reshape(H, 1, S), cos_t, sin_t, H)

    dqn = _small_slots_in_nt("q_b_dx", dq, wqb, S)
    dwqb = _small_slots_dw("q_b_dw", qn, dq, BF16)
    dql, d_q_lat = _rmsnorm_bwd(ql, g_q_lat, dqn, None, "q_latent_norm_bwd", out_dtype=BF16)
    dha = _mm_plain("q_a_dx", "nt", dql, wqa)
    dwqa = _mm_plain("q_a_dw", "tn", ha, dql, out_dtype=BF16)

    dckv = _small_slots_in_nt("kv_b_dx", dkvu, wkvb, S)
    dwkvb = _small_slots_dw("kv_b_dw", ckv, dkvu, BF16)
    dkv, d_kv_lat = _kv_mid_bwd(kv, g_kv_lat, dckv, dkpe, cos_t, sin_t)
    dhk = _mm_plain("kv_a_dx", "nt", dkv, wkva)
    dwkva = _mm_plain("kv_a_dw", "tn", hk, dkv, out_dtype=BF16)
    dx2, dx2b, d_attn, d_kv_in = _rmsnorm_pair_bwd(x2, g_attn, dha, g_kv_in, dhk, dx3, "kv_attn_norm_bwd")

    scatter_mla = _Ride("scatter", [dwkva.reshape(N_DEV, Ds, R_kv + 128), dwkvb.reshape(N_DEV, R_kv, kvw),
                                    dwqa.reshape(N_DEV, Ds, R_q), dwqb.reshape(N_DEV, R_q, hpd * HEAD_PAD),
                                    dwo.reshape(N_DEV, H * V_DIM // N_DEV, D)])
    (dx1, d_ffn0), got_mla, got_ffn0 = _ffn_bwd(x1, ffn_norm[0:1], wg0, wu0, wd0, ffn0, dx2, dx2b, "0", False,
                                                  scatter=scatter_mla)

    ddiff, dwp, d_pool_scale = _pool_mix_bwd(diff, wp, g_pool_scale, dx1)
    grad_x, d_pool_norm = _pool_pre_bwd(xs, g_pool_norm, ddiff, dx1)

    d_gains = jnp.swapaxes(jnp.concatenate([d_pool_norm, d_pool_scale], axis=0).reshape(2, N_DEV, D // N_DEV), 0, 1)
    got_pool_w, got_gains = _all_to_all([jnp.swapaxes(dwp.reshape(G, N_DEV, gws, gw), 0, 1), d_gains], "exchange_pool")

    received = {n: [r] for n, r in zip(("w_kv_a", "w_kv_b", "w_q_a", "w_q_b", "w_o"), got_mla)}
    received.update({n: [r0, r1] for n, r0, r1 in zip(("w_gate", "w_up", "w_down"), got_ffn0, got_ffn1)})
    received.update(pool_w=[got_pool_w], gains=[got_gains])
    sums = {n: _sum_slots(r, "sum_" + n) for n, r in received.items()}
    grads = {
        "pool_w": sums["pool_w"],
        "w_kv_a": sums["w_kv_a"][0, :, :R_kv + QK_ROPE_DIM],
        "w_kv_b": sums["w_kv_b"][0],
        "w_q_a": sums["w_q_a"],
        "w_q_b": sums["w_q_b"].reshape(R_q, hpd, HEAD_PAD)[:, :, :QK_DIM].reshape(1, R_q, hpd * QK_DIM),
        "w_o": sums["w_o"],
        "w_gate": sums["w_gate"],
        "w_up": sums["w_up"],
        "w_down": sums["w_down"],
        "pool_norm": sums["gains"][0, 0:1],
        "pool_scale": sums["gains"][0, 1:2],
    }

    small = {"kv_in_norm": d_kv_in, "kv_latent_norm": d_kv_lat, "attn_norm": d_attn, "q_latent_norm": d_q_lat,
             "ffn_norm": jnp.concatenate([d_ffn0, d_ffn1], axis=0), "final_norm": d_final}
    small_packed = jnp.concatenate([small[n].reshape(-1) for n in _REPLICATED] + [loss_part.reshape(-1)])
    pad = (-small_packed.shape[0]) % (8 * 128)
    reduced = _all_reduce_small(jnp.pad(small_packed, (0, pad)).reshape(-1, 128)).reshape(-1)
    off = 0
    for n in _REPLICATED:
        grads[n] = reduced[off:off + weights[n].size].reshape(weights[n].shape)
        off += weights[n].size
    loss = reduced[off]

    delta_w, new_m, new_v = {}, {}, {}
    for n in _WEIGHTS:
        delta_w[n], new_m[n], new_v[n] = _adamw(weights[n], grads[n], m_in[n], v_in[n], "adamw_" + n)

    return (loss, grad_x.reshape(x.shape), *[grads[n] for n in _WEIGHTS], *[delta_w[n] for n in _WEIGHTS],
            *[new_m[n] for n in _WEIGHTS], *[new_v[n] for n in _WEIGHTS])
```

```python
import math

import jax
import jax.numpy as jnp
import numpy as np
from jax import lax
from jax.experimental import pallas as pl
from jax.experimental.pallas import tpu as pltpu

F32 = jnp.float32
BF16 = jnp.bfloat16

N_DEV = 8
POOL_WINDOWS = (2, 4, 8, 16)
POOL_HALO = 16
QK_NOPE_DIM = 128
QK_ROPE_DIM = 64
QK_DIM = QK_NOPE_DIM + QK_ROPE_DIM
HEAD_PAD = 256
V_DIM = 128
ROPE_BASE = 10000.0
ATTN_SCALE = 1.0 / math.sqrt(QK_DIM)
EXP2_SCALE = ATTN_SCALE * math.log2(math.e)
NORM_EPS = 1e-6
ADAM_LR = 0.001
ADAM_B1 = 0.9
ADAM_B2 = 0.999
ADAM_EPS = 1e-08
ADAM_WD = 0.01
ADAM_STEP = 10

ROW_TILE = 512
ATT_BLOCK = 1024
ATT_ROW_CHUNK = 256
FFN_SLOTS = 2
MM_TN, MM_TK = 512, 512
VMEM_LIMIT = 48 * 1024 * 1024

_pcall = pl.pallas_call


def _params(sem):
    return pltpu.CompilerParams(dimension_semantics=sem, vmem_limit_bytes=VMEM_LIMIT)


def _tile(dim, default):
    if dim % default == 0:
        return default
    assert dim <= 2 * default, (dim, default)
    return dim


def _row_tile(rows, width):
    ts = _tile(rows, ROW_TILE)
    while ts * width * 4 > (2 << 20) and ts % 16 == 0:
        ts //= 2
    return ts


def _rms_scale(x):
    return lax.rsqrt(jnp.mean(x * x, axis=-1, keepdims=True) + NORM_EPS)


def _rms_bwd(x, g, dy):
    r = _rms_scale(x)
    xn = x * r
    u = dy * g
    dx = r * (u - xn * jnp.mean(u * xn, axis=-1, keepdims=True))
    return dx, dy * xn


def _swap_halves(x):
    lane = lax.broadcasted_iota(jnp.int32, x.shape, 1)
    return jnp.where(lane < QK_ROPE_DIM // 2, pltpu.roll(x, 128 - QK_ROPE_DIM // 2, 1), pltpu.roll(x, QK_ROPE_DIM // 2, 1))


def _rope(x, cos_t, sin_t):
    return x * cos_t + _swap_halves(x) * sin_t


def _rmsnorm(x, g, name, transposed_too=False):
    S, D = x.shape
    ts = _row_tile(S, D)

    def body(x_ref, g_ref, o_ref, *t_ref):
        xf = x_ref[...]
        y = xf * _rms_scale(xf) * g_ref[...]
        o_ref[...] = y.astype(o_ref.dtype)
        if transposed_too:
            t_ref[0][...] = y.T.astype(o_ref.dtype)

    row = pl.BlockSpec((ts, D), lambda i: (i, 0))
    outs = _pcall(
        body, name=name, grid=(S // ts,),
        in_specs=[row, pl.BlockSpec((1, D), lambda i: (0, 0))],
        out_specs=(row, pl.BlockSpec((D, ts), lambda i: (0, i))) if transposed_too else row,
        out_shape=((jax.ShapeDtypeStruct((S, D), BF16), jax.ShapeDtypeStruct((D, S), BF16)) if transposed_too
                   else jax.ShapeDtypeStruct((S, D), BF16)),
        compiler_params=_params(("parallel",)),
    )(x, g)
    return outs


def _rmsnorm_bwd(x, g, dy, res, name, out_dtype=F32, with_bf16=False):
    S, D = x.shape
    ts = _row_tile(S, D)
    has_res = res is not None

    def body(*refs):
        refs = list(refs)
        x_ref, g_ref, dy_ref = refs[:3]
        res_ref = refs[3] if has_res else None
        outs = refs[3 + has_res:]
        dx_ref, dg_ref = outs[0], outs[-1]
        dx, dgt = _rms_bwd(x_ref[...], g_ref[...], dy_ref[...].astype(F32))
        if has_res:
            dx = res_ref[...] + dx
        dx_ref[...] = dx.astype(dx_ref.dtype)
        if with_bf16:
            outs[1][...] = dx.astype(BF16)

        @pl.when(pl.program_id(0) == 0)
        def _():
            dg_ref[...] = jnp.zeros_like(dg_ref)

        dg_ref[...] += jnp.sum(dgt, axis=0, keepdims=True)

    row = pl.BlockSpec((ts, D), lambda i: (i, 0))
    vec = pl.BlockSpec((1, D), lambda i: (0, 0))
    out_specs = [row] + ([row] if with_bf16 else []) + [vec]
    out_shape = ([jax.ShapeDtypeStruct((S, D), out_dtype)] + ([jax.ShapeDtypeStruct((S, D), BF16)] if with_bf16 else [])
                 + [jax.ShapeDtypeStruct((1, D), F32)])
    return _pcall(
        body, name=name, grid=(S // ts,),
        in_specs=[row, vec, row] + ([row] if has_res else []),
        out_specs=tuple(out_specs), out_shape=tuple(out_shape),
        compiler_params=_params(("arbitrary",)),
    )(*([x, g, dy] + ([res] if has_res else [])))


def _rmsnorm_pair(x, g1, g2, name):
    S, D = x.shape
    ts = _row_tile(S, D)

    def body(x_ref, g1_ref, g2_ref, o1_ref, o2_ref):
        xf = x_ref[...]
        xn = xf * _rms_scale(xf)
        o1_ref[...] = (xn * g1_ref[...]).astype(o1_ref.dtype)
        o2_ref[...] = (xn * g2_ref[...]).astype(o2_ref.dtype)

    row = pl.BlockSpec((ts, D), lambda i: (i, 0))
    vec = pl.BlockSpec((1, D), lambda i: (0, 0))
    sds = jax.ShapeDtypeStruct((S, D), BF16)
    return _pcall(
        body, name=name, grid=(S // ts,),
        in_specs=[row, vec, vec], out_specs=(row, row), out_shape=(sds, sds),
        compiler_params=_params(("parallel",)),
    )(x, g1, g2)


def _rmsnorm_pair_bwd(x, g1, dy1, g2, dy2, res, name):
    S, D = x.shape
    ts = _row_tile(S, D)

    def body(x_ref, g1_ref, dy1_ref, g2_ref, dy2_ref, res_ref, dx_ref, dxb_ref, dg1_ref, dg2_ref):
        xf = x_ref[...]
        dx1, dgt1 = _rms_bwd(xf, g1_ref[...], dy1_ref[...])
        dx2, dgt2 = _rms_bwd(xf, g2_ref[...], dy2_ref[...])
        dx = (res_ref[...] + dx1) + dx2
        dx_ref[...] = dx
        dxb_ref[...] = dx.astype(BF16)

        @pl.when(pl.program_id(0) == 0)
        def _():
            dg1_ref[...] = jnp.zeros_like(dg1_ref)
            dg2_ref[...] = jnp.zeros_like(dg2_ref)

        dg1_ref[...] += jnp.sum(dgt1, axis=0, keepdims=True)
        dg2_ref[...] += jnp.sum(dgt2, axis=0, keepdims=True)

    row = pl.BlockSpec((ts, D), lambda i: (i, 0))
    vec = pl.BlockSpec((1, D), lambda i: (0, 0))
    return _pcall(
        body, name=name, grid=(S // ts,),
        in_specs=[row, vec, row, vec, row, row], out_specs=(row, row, vec, vec),
        out_shape=(jax.ShapeDtypeStruct((S, D), F32), jax.ShapeDtypeStruct((S, D), BF16),
                   jax.ShapeDtypeStruct((1, D), F32), jax.ShapeDtypeStruct((1, D), F32)),
        compiler_params=_params(("arbitrary",)),
    )(x, g1, dy1, g2, dy2, res)


def _pool_pre(x, g, ride=None):
    S, D = x.shape
    ts = _row_tile(S, D)
    gw = D // len(POOL_WINDOWS)
    hb = ts // POOL_HALO

    def body(ins, outs, scratch):
        (x_ref, halo_ref, g_ref), (o_ref,) = ins, outs
        i = pl.program_id(0)
        xx = jnp.concatenate([halo_ref[...], x_ref[...]], axis=0)
        h = xx * _rms_scale(xx) * g_ref[...]
        t = i * ts - POOL_HALO + lax.broadcasted_iota(jnp.int32, (ts + POOL_HALO, 1), 0)
        h = jnp.where(t >= 0, h, 0.0)
        tf = t[POOL_HALO:].astype(F32)
        for gi, w in enumerate(POOL_WINDOWS):
            hg = h[:, gi * gw:(gi + 1) * gw]
            acc, span = hg, 1
            while span < w:
                acc = acc + pltpu.roll(acc, span, 0)
                span *= 2
            count = jnp.minimum(tf + 1.0, float(w))
            diff = acc[POOL_HALO:] / count - hg[POOL_HALO:]
            o_ref[:, gi * gw:(gi + 1) * gw] = diff.astype(o_ref.dtype)

    return _pallas(body, "pool_pre", (S // ts,), [x, x, g],
                   [pl.BlockSpec((ts, D), lambda i: (i, 0)),
                    pl.BlockSpec((POOL_HALO, D), lambda i: (jnp.maximum(i * hb - 1, 0), 0)),
                    pl.BlockSpec((1, D), lambda i: (0, 0))],
                   [pl.BlockSpec((ts, D), lambda i: (i, 0))], [jax.ShapeDtypeStruct((S, D), BF16)],
                   sem=("parallel",), ride=ride)


def _pool_pre_bwd(x, g, dd, res):
    S, D = x.shape
    ts = _row_tile(S, D)
    gw = D // len(POOL_WINDOWS)
    hb = ts // POOL_HALO
    last_halo = S // POOL_HALO - 1

    def body(x_ref, g_ref, dd_ref, halo_ref, res_ref, dx_ref, dg_ref):
        i = pl.program_id(0)
        ee = jnp.concatenate([dd_ref[...], halo_ref[...]], axis=0)
        t = i * ts + lax.broadcasted_iota(jnp.int32, (ts + POOL_HALO, 1), 0)
        ee = jnp.where(t < S, ee, 0.0)
        tf = t.astype(F32)
        n = ts + POOL_HALO
        for gi, w in enumerate(POOL_WINDOWS):
            eg = ee[:, gi * gw:(gi + 1) * gw]
            acc, span = eg / jnp.minimum(tf + 1.0, float(w)), 1
            while span < w:
                acc = acc + pltpu.roll(acc, n - span, 0)
                span *= 2
            dx_ref[:, gi * gw:(gi + 1) * gw] = acc[:ts] - eg[:ts]
        dx, dgt = _rms_bwd(x_ref[...], g_ref[...], dx_ref[...])
        dx_ref[...] = res_ref[...] + dx

        @pl.when(i == 0)
        def _():
            dg_ref[...] = jnp.zeros_like(dg_ref)

        dg_ref[...] += jnp.sum(dgt, axis=0, keepdims=True)

    row = pl.BlockSpec((ts, D), lambda i: (i, 0))
    vec = pl.BlockSpec((1, D), lambda i: (0, 0))
    return _pcall(
        body, name="pool_pre_bwd", grid=(S // ts,),
        in_specs=[row, vec, row,
                  pl.BlockSpec((POOL_HALO, D), lambda i: (jnp.minimum((i + 1) * hb, last_halo), 0)), row],
        out_specs=(row, vec),
        out_shape=(jax.ShapeDtypeStruct((S, D), F32), jax.ShapeDtypeStruct((1, D), F32)),
        compiler_params=_params(("arbitrary",)),
    )(x, g, dd, dd, res)


def _pool_mix(diff, w, scale, x, ride=None):
    S, D = x.shape
    G, gw, _ = w.shape
    ts = _tile(S, ROW_TILE)

    def body(ins, outs, scratch):
        (d_ref, w_ref, s_ref, x_ref), (o_ref,) = ins, outs
        mo = jnp.dot(d_ref[...], w_ref[0], preferred_element_type=F32)
        o_ref[...] = x_ref[...] + mo * s_ref[...]

    blk = pl.BlockSpec((ts, gw), lambda i, g: (i, g))
    return _pallas(body, "pool_mix", (S // ts, G), [diff, w, scale, x],
                   [blk, pl.BlockSpec((1, gw, gw), lambda i, g: (g, 0, 0)), pl.BlockSpec((1, gw), lambda i, g: (0, g)), blk],
                   [blk], [jax.ShapeDtypeStruct((S, D), F32)], sem=("parallel", "parallel"), ride=ride)


def _pool_mix_bwd(diff, w, scale, dy):
    S, D = dy.shape
    G, gw, _ = w.shape
    ts = _tile(S, ROW_TILE)

    def body(d_ref, w_ref, s_ref, dy_ref, dd_ref, dw_ref, ds_ref):
        i = pl.program_id(1)
        d = d_ref[...]
        dy = dy_ref[...]
        mo = jnp.dot(d, w_ref[0], preferred_element_type=F32)
        dmo = (dy * s_ref[...]).astype(BF16)
        dd_ref[...] = lax.dot_general(dmo, w_ref[0], (((1,), (1,)), ((), ())), preferred_element_type=F32)

        @pl.when(i == 0)
        def _():
            dw_ref[...] = jnp.zeros_like(dw_ref)
            ds_ref[...] = jnp.zeros_like(ds_ref)

        dw_ref[0] += lax.dot_general(d, dmo, (((0,), (0,)), ((), ())), preferred_element_type=F32)
        ds_ref[...] += jnp.sum(dy * mo, axis=0, keepdims=True)

    blk = pl.BlockSpec((ts, gw), lambda g, i: (i, g))
    wspec = pl.BlockSpec((1, gw, gw), lambda g, i: (g, 0, 0))
    vec = pl.BlockSpec((1, gw), lambda g, i: (0, g))
    return _pcall(
        body, name="pool_mix_bwd", grid=(G, S // ts),
        in_specs=[blk, wspec, vec, blk],
        out_specs=(blk, wspec, vec),
        out_shape=(jax.ShapeDtypeStruct((S, D), F32), jax.ShapeDtypeStruct((G, gw, gw), F32),
                   jax.ShapeDtypeStruct((1, D), F32)),
        compiler_params=_params(("parallel", "arbitrary")),
    )(diff, w, scale, dy)


def _loss_head(x, g, target):
    S, D = x.shape
    ts = _row_tile(S, D)

    def body(x_ref, g_ref, t_ref, loss_ref, dx_ref, dxb_ref, dg_ref):
        xf = x_ref[...]
        gv = g_ref[...]
        err = xf * _rms_scale(xf) * gv - t_ref[...]
        dx, dgt = _rms_bwd(xf, gv, err * (1.0 / D))
        dx_ref[...] = dx
        dxb_ref[...] = dx.astype(BF16)

        @pl.when(pl.program_id(0) == 0)
        def _():
            loss_ref[...] = jnp.zeros_like(loss_ref)
            dg_ref[...] = jnp.zeros_like(dg_ref)

        part = 0.5 * jnp.sum(jnp.sum(err * err, axis=-1, keepdims=True) * (1.0 / D), axis=0, keepdims=True)
        loss_ref[...] += jnp.broadcast_to(part, loss_ref.shape)
        dg_ref[...] += jnp.sum(dgt, axis=0, keepdims=True)

    row = pl.BlockSpec((ts, D), lambda i: (i, 0))
    vec = pl.BlockSpec((1, D), lambda i: (0, 0))
    return _pcall(
        body, name="loss_head", grid=(S // ts,),
        in_specs=[row, vec, row],
        out_specs=(pl.BlockSpec((1, 128), lambda i: (0, 0)), row, row, vec),
        out_shape=(jax.ShapeDtypeStruct((1, 128), F32), jax.ShapeDtypeStruct((S, D), F32),
                   jax.ShapeDtypeStruct((S, D), BF16), jax.ShapeDtypeStruct((1, D), F32)),
        compiler_params=_params(("arbitrary",)),
    )(x, g, target)


_DIMS = {"nn": (((1,), (0,)), ((), ())), "nt": (((1,), (1,)), ((), ())), "tn": (((0,), (0,)), ((), ()))}


def _mm(name, mode, a, b, tiles, grid, a_map, b_map, o_map, out_shape, out_dtype=F32, res=None, ride=None):
    tm, tn, tk = tiles
    nk = grid[-1]
    has_res = res is not None
    dn = _DIMS[mode]

    def body(ins, outs, scratch):
        o_ref = outs[0]

        def product():
            return lax.dot_general(ins[0][...].astype(BF16), ins[1][...].astype(BF16), dn, preferred_element_type=F32)

        def finish(out):
            if has_res:
                out = ins[2][...] + out
            o_ref[...] = out.astype(o_ref.dtype)

        if nk == 1:
            finish(product())
            return
        acc_ref = scratch[0]
        k = pl.program_id(2)

        @pl.when(k == 0)
        def _():
            acc_ref[...] = jnp.zeros_like(acc_ref)

        acc_ref[...] += product()

        @pl.when(k == nk - 1)
        def _():
            finish(acc_ref[...])

    a_spec = pl.BlockSpec((tk, tm) if mode == "tn" else (tm, tk), a_map)
    b_spec = pl.BlockSpec((tn, tk) if mode == "nt" else (tk, tn), b_map)
    o_spec = pl.BlockSpec((tm, tn), o_map)
    operands, in_specs = [a, b], [a_spec, b_spec]
    if has_res:
        operands.append(res)
        in_specs.append(o_spec)
    outs = _pallas(body, name, grid, operands, in_specs, [o_spec], [jax.ShapeDtypeStruct(out_shape, out_dtype)],
                   scratch=[pltpu.VMEM((tm, tn), F32)] if nk > 1 else [],
                   sem=("parallel", "parallel", "arbitrary"), ride=ride)
    return (outs[0], outs[1:]) if ride else outs[0]


def _mm_plain(name, mode, a, b, out_dtype=F32, res=None):
    if mode == "tn":
        (K, M), N = a.shape, b.shape[1]
    elif mode == "nt":
        (M, K), N = a.shape, b.shape[0]
    else:
        (M, K), N = a.shape, b.shape[1]
    if mode == "tn":
        tm, tn, tk = _tile(M, 2 * MM_TN), _tile(N, 2 * MM_TN), _tile(K, ROW_TILE)
    else:
        tm = _tile(M, ROW_TILE)
        tk = K if K <= 4 * MM_TK else _tile(K, MM_TK)
        tn = N if N * tk * 2 <= (8 << 20) else _tile(N, MM_TN)
    a_map = (lambda i, j, k: (k, i)) if mode == "tn" else (lambda i, j, k: (i, k))
    b_map = (lambda i, j, k: (j, k)) if mode == "nt" else (lambda i, j, k: (k, j))
    return _mm(name, mode, a, b, (tm, tn, tk), (M // tm, N // tn, K // tk), a_map, b_map, lambda i, j, k: (i, j),
               (M, N), out_dtype, res)


def _small_slots_out(name, a, w_slots, out_dtype, epilogue=None):
    S, K = a.shape
    n = w_slots.shape[1]
    tm = _tile(S, ROW_TILE)
    extra = epilogue[1] if epilogue else []

    def body(*refs):
        a_ref, w_ref, o_ref = refs[0], refs[1], refs[-1]
        av = a_ref[...].astype(BF16)
        tables = [r[...] for r in refs[2:-1]]
        for j in range(N_DEV):
            out = jnp.dot(av, w_ref[j], preferred_element_type=F32)
            if epilogue:
                out = epilogue[0](out, *tables)
            o_ref[j] = out.astype(o_ref.dtype)

    return _pcall(
        body, name=name, grid=(S // tm,),
        in_specs=[pl.BlockSpec((tm, K), lambda i: (i, 0)), pl.BlockSpec((N_DEV, K, n), lambda i: (0, 0, 0))]
        + (list(epilogue[2]) if epilogue else []),
        out_specs=pl.BlockSpec((N_DEV, tm, n), lambda i: (0, i, 0)),
        out_shape=jax.ShapeDtypeStruct((N_DEV, S, n), out_dtype),
        compiler_params=_params(("parallel",)),
    )(a, w_slots.reshape(N_DEV, K, n), *extra).reshape(N_DEV * S, n)


def _small_slots_in_nt(name, a_slots, w_slots, S):
    n = a_slots.shape[1]
    K = w_slots.shape[0] // N_DEV
    tm = _tile(S, ROW_TILE)

    def body(a_ref, w_ref, o_ref):
        total = lax.dot_general(a_ref[0], w_ref[0], _DIMS["nt"], preferred_element_type=F32)
        for j in range(1, N_DEV):
            total += lax.dot_general(a_ref[j], w_ref[j], _DIMS["nt"], preferred_element_type=F32)
        o_ref[...] = total

    return _pcall(
        body, name=name, grid=(S // tm,),
        in_specs=[pl.BlockSpec((N_DEV, tm, n), lambda i: (0, i, 0)), pl.BlockSpec((N_DEV, K, n), lambda i: (0, 0, 0))],
        out_specs=pl.BlockSpec((tm, K), lambda i: (i, 0)),
        out_shape=jax.ShapeDtypeStruct((S, K), F32),
        compiler_params=_params(("parallel",)),
    )(a_slots.reshape(N_DEV, S, n), w_slots.reshape(N_DEV, K, n))


def _small_slots_dw(name, a, d_slots, out_dtype):
    S, K = a.shape
    n = d_slots.shape[1]
    tk = _tile(S, ROW_TILE)
    nkb = S // tk

    def body(a_ref, d_ref, o_ref, acc_ref):
        k = pl.program_id(0)
        at = a_ref[...].astype(F32).T.astype(BF16)

        def add(first):
            for j in range(N_DEV):
                part = jnp.dot(at, d_ref[j], preferred_element_type=F32)
                if first:
                    acc_ref[j] = part
                else:
                    acc_ref[j] += part

        @pl.when(k == 0)
        def _():
            add(True)

        @pl.when(k > 0)
        def _():
            add(False)

        @pl.when(k == nkb - 1)
        def _():
            o_ref[...] = acc_ref[...].astype(o_ref.dtype)

    return _pcall(
        body, name=name, grid=(nkb,),
        in_specs=[pl.BlockSpec((tk, K), lambda k: (k, 0)), pl.BlockSpec((N_DEV, tk, n), lambda k: (0, k, 0))],
        out_specs=pl.BlockSpec((N_DEV, K, n), lambda k: (0, 0, 0)),
        out_shape=jax.ShapeDtypeStruct((N_DEV, K, n), out_dtype),
        scratch_shapes=[pltpu.VMEM((N_DEV, K, n), F32)],
        compiler_params=_params(("arbitrary",)),
    )(a, d_slots.reshape(N_DEV, S, n)).reshape(N_DEV * K, n)


def _sigmoid(x):
    return 1.0 / (1.0 + jnp.exp(-x))


def _ffn_up(h, wg, wu, name, ride=None):
    S, D = h.shape
    fs = wg.shape[1]
    tm = _tile(S, ROW_TILE)
    sp = FFN_SLOTS

    def body(ins, outs, scratch):
        h_ref, wg_ref, wu_ref = ins
        g_ref, u_ref, a_ref = outs
        hv = h_ref[...]
        for s in range(sp):
            g = jnp.dot(hv, wg_ref[s], preferred_element_type=F32)
            u = jnp.dot(hv, wu_ref[s], preferred_element_type=F32)
            g_ref[s] = g.astype(g_ref.dtype)
            u_ref[s] = u.astype(u_ref.dtype)
            a_ref[s] = (g * _sigmoid(g) * u).astype(a_ref.dtype)

    w_spec = pl.BlockSpec((sp, D, fs), lambda i, j: (j, 0, 0))
    o_spec = pl.BlockSpec((sp, tm, fs), lambda i, j: (j, i, 0))
    sds = jax.ShapeDtypeStruct((N_DEV, S, fs), BF16)
    g, u, a, *rest = _pallas(body, name, (S // tm, N_DEV // sp), [h, wg.reshape(N_DEV, D, fs), wu.reshape(N_DEV, D, fs)],
                             [pl.BlockSpec((tm, D), lambda i, j: (i, 0)), w_spec, w_spec], [o_spec, o_spec, o_spec],
                             [sds, sds, sds], sem=("parallel", "arbitrary"), ride=ride)
    return [t.reshape(N_DEV * S, fs) for t in (g, u, a)] + rest


def _ffn_dact(dy, wd, g, u, name, ride=None):
    S, D = dy.shape
    fs = g.shape[1]
    tm = _tile(S, ROW_TILE)
    sp = FFN_SLOTS

    def body(ins, outs, scratch):
        dy_ref, wd_ref, g_ref, u_ref = ins
        dg_ref, du_ref = outs
        dy = dy_ref[...]
        for s in range(sp):
            da = lax.dot_general(dy, wd_ref[s], _DIMS["nt"], preferred_element_type=F32)
            g, u = g_ref[s].astype(F32), u_ref[s].astype(F32)
            sig = _sigmoid(g)
            dg_ref[s] = (da * u * (sig * (1.0 + g * (1.0 - sig)))).astype(dg_ref.dtype)
            du_ref[s] = (da * (g * sig)).astype(du_ref.dtype)

    o_spec = pl.BlockSpec((sp, tm, fs), lambda j, i: (j, i, 0))
    sds = jax.ShapeDtypeStruct((N_DEV, S, fs), BF16)
    dg, du, *rest = _pallas(body, name, (N_DEV // sp, S // tm),
                            [dy, wd.reshape(N_DEV, fs, D), g.reshape(N_DEV, S, fs), u.reshape(N_DEV, S, fs)],
                            [pl.BlockSpec((tm, D), lambda j, i: (i, 0)), pl.BlockSpec((sp, fs, D), lambda j, i: (j, 0, 0)),
                             o_spec, o_spec], [o_spec, o_spec], [sds, sds], sem=("parallel", "arbitrary"), ride=ride)
    return [dg.reshape(N_DEV * S, fs), du.reshape(N_DEV * S, fs)] + rest


def _ffn_down(a, wd, x, name):
    S, D = x.shape
    fs = a.shape[1]
    tm = _tile(S, ROW_TILE)
    sp = FFN_SLOTS
    nj = N_DEV // sp

    def body(a_ref, w_ref, x_ref, o_ref, acc_ref):
        j = pl.program_id(1)
        part = jnp.dot(a_ref[0], w_ref[0], preferred_element_type=F32)
        for s in range(1, sp):
            part += jnp.dot(a_ref[s], w_ref[s], preferred_element_type=F32)

        @pl.when(j == 0)
        def _():
            acc_ref[...] = x_ref[...] + part

        @pl.when(j > 0)
        def _():
            acc_ref[...] += part

        @pl.when(j == nj - 1)
        def _():
            o_ref[...] = acc_ref[...]

    row = pl.BlockSpec((tm, D), lambda i, j: (i, 0))
    return _pcall(
        body, name=name, grid=(S // tm, nj),
        in_specs=[pl.BlockSpec((sp, tm, fs), lambda i, j: (j, i, 0)), pl.BlockSpec((sp, fs, D), lambda i, j: (j, 0, 0)), row],
        out_specs=row, out_shape=jax.ShapeDtypeStruct((S, D), F32),
        scratch_shapes=[pltpu.VMEM((tm, D), F32)],
        compiler_params=_params(("parallel", "arbitrary")),
    )(a.reshape(N_DEV, S, fs), wd.reshape(N_DEV, fs, D), x)


def _ffn_dh(dg, du, wg, wu, S, name, ride=None):
    fs = dg.shape[1]
    D = wg.shape[0] // N_DEV
    tm = _tile(S, ROW_TILE)
    sp = FFN_SLOTS
    nj = N_DEV // sp

    def body(ins, outs, scratch):
        dg_ref, du_ref, wg_ref, wu_ref = ins
        o_ref, acc_ref = outs[0], scratch[0]
        j = pl.program_id(1)
        part = None
        for s in range(sp):
            for d_ref, w_ref in ((dg_ref, wg_ref), (du_ref, wu_ref)):
                term = lax.dot_general(d_ref[s], w_ref[s], _DIMS["nt"], preferred_element_type=F32)
                part = term if part is None else part + term

        @pl.when(j == 0)
        def _():
            acc_ref[...] = part

        @pl.when(j > 0)
        def _():
            acc_ref[...] += part

        @pl.when(j == nj - 1)
        def _():
            o_ref[...] = acc_ref[...]

    d_spec = pl.BlockSpec((sp, tm, fs), lambda i, j: (j, i, 0))
    w_spec = pl.BlockSpec((sp, D, fs), lambda i, j: (j, 0, 0))
    return _pallas(body, name, (S // tm, nj),
                   [dg.reshape(N_DEV, S, fs), du.reshape(N_DEV, S, fs), wg.reshape(N_DEV, D, fs), wu.reshape(N_DEV, D, fs)],
                   [d_spec, d_spec, w_spec, w_spec], [pl.BlockSpec((tm, D), lambda i, j: (i, 0))],
                   [jax.ShapeDtypeStruct((S, D), F32)], scratch=[pltpu.VMEM((tm, D), F32)],
                   sem=("parallel", "arbitrary"), ride=ride)


def _slots_dw_t(name, at, d_slots, ride=None):
    K, S = at.shape
    n = d_slots.shape[1]
    tk = _tile(S, 2 * ROW_TILE)
    nkb = S // tk
    return _mm(name, "nn", at, d_slots, (K, n, tk), (N_DEV, 1, nkb),
               lambda j, q, k: (0, k), lambda j, q, k: (j * nkb + k, 0), lambda j, q, k: (j, 0),
               (N_DEV * K, n), BF16, ride=ride)


def _rope_heads(q, cos_t, sin_t):
    parts = []
    for c0 in range(0, q.shape[1], HEAD_PAD):
        parts += [q[:, c0:c0 + QK_NOPE_DIM], _rope(q[:, c0 + QK_NOPE_DIM:c0 + HEAD_PAD], cos_t, sin_t)]
    return jnp.concatenate(parts, axis=1)


def _kv_mid(kv, g, cos_t, sin_t):
    S, W = kv.shape
    R = W - 128
    ts = _tile(S, ROW_TILE)

    def body(kv_ref, g_ref, c_ref, s_ref, c_out, k_out):
        cf = kv_ref[:, :R]
        c_out[...] = (cf * _rms_scale(cf) * g_ref[...]).astype(c_out.dtype)
        k_out[...] = _rope(kv_ref[:, R:], c_ref[...], s_ref[...]).astype(k_out.dtype)

    tab = pl.BlockSpec((ts, 128), lambda i: (i, 0))
    return _pcall(
        body, name="kv_mid", grid=(S // ts,),
        in_specs=[pl.BlockSpec((ts, W), lambda i: (i, 0)), pl.BlockSpec((1, R), lambda i: (0, 0)), tab, tab],
        out_specs=(pl.BlockSpec((ts, R), lambda i: (i, 0)), tab),
        out_shape=(jax.ShapeDtypeStruct((S, R), BF16), jax.ShapeDtypeStruct((S, 128), BF16)),
        compiler_params=_params(("parallel",)),
    )(kv, g, cos_t, sin_t)


def _kv_mid_bwd(kv, g, dc, dkpe, cos_t, sin_t):
    S, W = kv.shape
    R = W - 128
    H = dkpe.shape[0]
    ts = _row_tile(S, H * 128)

    def body(kv_ref, g_ref, dc_ref, dk_ref, c_ref, s_ref, dkv_ref, dg_ref):
        dx, dgt = _rms_bwd(kv_ref[:, :R], g_ref[...], dc_ref[...])
        dkv_ref[:, :R] = dx.astype(dkv_ref.dtype)
        dk = dk_ref[0]
        for h in range(1, H):
            dk = dk + dk_ref[h]
        dkv_ref[:, R:] = _rope(dk, c_ref[...], -s_ref[...]).astype(dkv_ref.dtype)

        @pl.when(pl.program_id(0) == 0)
        def _():
            dg_ref[...] = jnp.zeros_like(dg_ref)

        dg_ref[...] += jnp.sum(dgt, axis=0, keepdims=True)

    tab = pl.BlockSpec((ts, 128), lambda i: (i, 0))
    vec = pl.BlockSpec((1, R), lambda i: (0, 0))
    return _pcall(
        body, name="kv_mid_bwd", grid=(S // ts,),
        in_specs=[pl.BlockSpec((ts, W), lambda i: (i, 0)), vec, pl.BlockSpec((ts, R), lambda i: (i, 0)),
                  pl.BlockSpec((H, ts, 128), lambda i: (0, i, 0)), tab, tab],
        out_specs=(pl.BlockSpec((ts, W), lambda i: (i, 0)), vec),
        out_shape=(jax.ShapeDtypeStruct((S, W), BF16), jax.ShapeDtypeStruct((1, R), F32)),
        compiler_params=_params(("arbitrary",)),
    )(kv, g, dc, dkpe, cos_t, sin_t)


def _block_pairs(nb, by_key):
    pairs = ([(qi, ki) for ki in range(nb) for qi in range(ki, nb)] if by_key
             else [(qi, ki) for qi in range(nb) for ki in range(qi + 1)])
    return jnp.asarray(np.array([p[0] for p in pairs], np.int32)), jnp.asarray(np.array([p[1] for p in pairs], np.int32))


def _causal_mask(blk, transposed=False, rows=None, row0=0):
    shape = (blk if rows is None else rows, blk)
    r = lax.broadcasted_iota(jnp.int32, shape, 0) + row0
    c = lax.broadcasted_iota(jnp.int32, shape, 1)
    return (r <= c) if transposed else (c <= r)


def _attn_specs(S, blk, hpd):
    nb = S // blk
    w = hpd * HEAD_PAD
    return dict(
        q=pl.BlockSpec((blk, w), lambda j, t, qt, kt: (j * nb + qt[t], 0)),
        kv=pl.BlockSpec((blk, w), lambda j, t, qt, kt: (j * nb + kt[t], 0)),
        kp=pl.BlockSpec((blk, 128), lambda j, t, qt, kt: (kt[t], 0)),
        do=pl.BlockSpec((blk, hpd * V_DIM), lambda j, t, qt, kt: (qt[t], j)),
        col=pl.BlockSpec((hpd, blk, 1), lambda j, t, qt, kt: (j, qt[t], 0)),
        row=pl.BlockSpec((hpd, 1, blk), lambda j, t, qt, kt: (j, 0, qt[t])))


def _head_k(kv_ref, kp, hh):
    return jnp.concatenate([kv_ref[:, hh * HEAD_PAD:hh * HEAD_PAD + QK_NOPE_DIM], kp], axis=1)


def _head_v(kv_ref, hh):
    return kv_ref[:, hh * HEAD_PAD + QK_NOPE_DIM:(hh + 1) * HEAD_PAD]


def _attn_fwd(q, kvu, kpe, H, ride=None):
    S = kpe.shape[0]
    hpd = H // N_DEV
    blk = _tile(S, ATT_BLOCK)
    q_tab, k_tab = _block_pairs(S // blk, by_key=False)
    rc = min(blk, ATT_ROW_CHUNK)

    def body(ins, outs, scratch):
        qt, kt, q_ref, kv_ref, kp_ref = ins
        o_ref, lse_ref = outs
        m_ref, acc_ref = scratch
        t = pl.program_id(1)
        qi, ki = qt[t], kt[t]

        @pl.when(ki == 0)
        def _():
            m_ref[...] = jnp.full_like(m_ref, -jnp.inf)
            acc_ref[...] = jnp.zeros_like(acc_ref)

        def step(masked):
            kp = kp_ref[...]
            ones = jnp.ones((blk, 128), BF16)
            for hh in range(hpd):
                k = _head_k(kv_ref, kp, hh)
                v_ext = jnp.concatenate([_head_v(kv_ref, hh), ones], axis=1)
                for r0 in range(0, blk, rc):
                    rows = slice(r0, r0 + rc)
                    s = lax.dot_general(q_ref[rows, hh * HEAD_PAD:(hh + 1) * HEAD_PAD], k, _DIMS["nt"],
                                        preferred_element_type=F32)
                    if masked:
                        s = jnp.where(_causal_mask(blk, rows=rc, row0=r0), s, -jnp.inf)
                    m_old = m_ref[hh, rows]
                    m_new = jnp.maximum(m_old, jnp.max(s, axis=-1, keepdims=True))
                    p = jnp.exp2((s - m_new) * EXP2_SCALE).astype(BF16)
                    acc_ref[hh, rows] = (jnp.exp2((m_old - m_new) * EXP2_SCALE) * acc_ref[hh, rows]
                                         + jnp.dot(p, v_ext, preferred_element_type=F32))
                    m_ref[hh, rows] = m_new

        @pl.when(ki < qi)
        def _():
            step(False)

        @pl.when(ki == qi)
        def _():
            step(True)
            for hh in range(hpd):
                l = acc_ref[hh, :, V_DIM:]
                o_ref[:, hh * V_DIM:(hh + 1) * V_DIM] = (acc_ref[hh, :, :V_DIM] / l).astype(o_ref.dtype)
                lse_ref[hh] = m_ref[hh] * EXP2_SCALE + jnp.log2(l[:, :1])

    sp = _attn_specs(S, blk, hpd)
    return _pallas(body, "attn_fwd", (N_DEV, q_tab.shape[0]), [q, kvu, kpe], [sp["q"], sp["kv"], sp["kp"]],
                   [sp["do"], sp["col"]],
                   [jax.ShapeDtypeStruct((S, H * V_DIM), BF16), jax.ShapeDtypeStruct((H, S, 1), F32)],
                   scratch=[pltpu.VMEM((hpd, blk, 1), F32), pltpu.VMEM((hpd, blk, 2 * V_DIM), F32)],
                   sem=("parallel", "arbitrary"), ride=ride, prefetch=[q_tab, k_tab])


def _attn_bwd(q, kvu, kpe, do, o, lse_row, cos_t, sin_t, H):
    S = kpe.shape[0]
    hpd = H // N_DEV
    blk = _tile(S, ATT_BLOCK)
    nb = S // blk
    q_tab, k_tab = _block_pairs(nb, by_key=True)

    def body(qt, kt, q_ref, kv_ref, kp_ref, do_ref, o_ref, lse_ref, c_ref, s_ref, dq_ref, dkv_ref, dkp_ref,
             dq_acc, dk_acc, dv_acc):
        t = pl.program_id(1)
        qi, ki = qt[t], kt[t]
        q_rows = pl.ds(pl.multiple_of(qi * blk, blk), blk)

        def step(diag):
            kp = kp_ref[...]
            ones = jnp.ones((8, V_DIM), BF16)
            for hh in range(hpd):
                k = _head_k(kv_ref, kp, hh)
                qv = q_ref[:, hh * HEAD_PAD:(hh + 1) * HEAD_PAD]
                cols = slice(hh * V_DIM, (hh + 1) * V_DIM)
                dov = do_ref[:, cols]
                doo = (dov.astype(F32) * o_ref[:, cols].astype(F32)).astype(BF16)
                delta = lax.dot_general(ones, doo, _DIMS["nt"], preferred_element_type=F32)[:1]
                st = lax.dot_general(k, qv, _DIMS["nt"], preferred_element_type=F32)
                pt = jnp.exp2(st * EXP2_SCALE - lse_ref[hh])
                if diag:
                    pt = jnp.where(_causal_mask(blk, transposed=True), pt, 0.0)
                dv_new = jnp.dot(pt.astype(BF16), dov, preferred_element_type=F32)
                dpt = lax.dot_general(_head_v(kv_ref, hh), dov, _DIMS["nt"], preferred_element_type=F32)
                dst = (pt * (dpt - delta)).astype(BF16)
                dk_new = jnp.dot(dst, qv, preferred_element_type=F32)
                dq_new = lax.dot_general(dst, k, _DIMS["tn"], preferred_element_type=F32)
                if diag:
                    dv_acc[hh] = dv_new
                    dk_acc[hh] = dk_new
                else:
                    dv_acc[hh] += dv_new
                    dk_acc[hh] += dk_new

                @pl.when(ki == 0)
                def _():
                    dq_acc[hh, q_rows] = dq_new

                @pl.when(ki > 0)
                def _():
                    dq_acc[hh, q_rows] += dq_new

        @pl.when(qi == ki)
        def _():
            step(True)
            for hh in range(hpd):
                c0 = hh * HEAD_PAD
                dq = dq_acc[hh, q_rows] * ATTN_SCALE
                dq_ref[:, c0:c0 + QK_NOPE_DIM] = dq[:, :QK_NOPE_DIM].astype(dq_ref.dtype)
                dq_ref[:, c0 + QK_NOPE_DIM:c0 + HEAD_PAD] = _rope(dq[:, QK_NOPE_DIM:], c_ref[...],
                                                                  -s_ref[...]).astype(dq_ref.dtype)

        @pl.when(qi > ki)
        def _():
            step(False)

        @pl.when(qi == nb - 1)
        def _():
            for hh in range(hpd):
                c0 = hh * HEAD_PAD
                dkv_ref[:, c0:c0 + QK_NOPE_DIM] = (dk_acc[hh, :, :QK_NOPE_DIM] * ATTN_SCALE).astype(dkv_ref.dtype)
                dkv_ref[:, c0 + QK_NOPE_DIM:c0 + HEAD_PAD] = dv_acc[hh].astype(dkv_ref.dtype)
                dkp_ref[hh] = dk_acc[hh, :, QK_NOPE_DIM:] * ATTN_SCALE

    sp = _attn_specs(S, blk, hpd)
    key_tab = pl.BlockSpec((blk, 128), lambda j, t, qt, kt: (kt[t], 0))
    grid_spec = pltpu.PrefetchScalarGridSpec(
        num_scalar_prefetch=2, grid=(N_DEV, q_tab.shape[0]),
        in_specs=[sp["q"], sp["kv"], sp["kp"], sp["do"], sp["do"], sp["row"], key_tab, key_tab],
        out_specs=(sp["kv"], sp["kv"], pl.BlockSpec((hpd, blk, 128), lambda j, t, qt, kt: (j, kt[t], 0))),
        scratch_shapes=[pltpu.VMEM((hpd, S, HEAD_PAD), F32), pltpu.VMEM((hpd, blk, HEAD_PAD), F32),
                        pltpu.VMEM((hpd, blk, V_DIM), F32)])
    return _pcall(
        body, name="attn_bwd", grid_spec=grid_spec,
        out_shape=(jax.ShapeDtypeStruct(q.shape, BF16), jax.ShapeDtypeStruct(kvu.shape, BF16),
                   jax.ShapeDtypeStruct((H, S, 128), F32)),
        compiler_params=_params(("parallel", "arbitrary")),
    )(q_tab, k_tab, q, kvu, kpe, do, o, lse_row, cos_t, sin_t)


def _mesh_pos():
    return lax.axis_index("x"), lax.axis_index("y"), lax.axis_index("c")


def _flip(pos, k):
    x, y, c = pos
    return (1 - x if k & 4 else x, 1 - y if k & 2 else y, 1 - c if k & 1 else c)


def _rank(pos):
    return 4 * pos[0] + 2 * pos[1] + pos[2]


def _copies(n, src_of, dst_refs, local_src_of, send_sems, recv_sems, local_sems):
    me = _mesh_pos()
    local = [pltpu.make_async_copy(local_src_of(a), dst_refs[a].at[_rank(me)], local_sems.at[a]) for a in range(n)]
    sends, arrivals = [], []
    for k in range(1, N_DEV):
        peer = _flip(me, k)
        for a in range(n):
            idx = a * (N_DEV - 1) + k - 1
            for into, group in ((me, sends), (peer, arrivals)):
                group.append(pltpu.make_async_remote_copy(
                    src_ref=src_of(a, peer), dst_ref=dst_refs[a].at[_rank(into)],
                    send_sem=send_sems.at[idx], recv_sem=recv_sems.at[idx],
                    device_id=peer, device_id_type=pl.DeviceIdType.MESH))
    return local, sends, arrivals


def _exchange_start(*args):
    local, sends, _ = _copies(*args)
    for cp in local + sends:
        cp.start()


def _exchange_wait(*args):
    local, sends, arrivals = _copies(*args)
    for cp in arrivals:
        cp.wait_recv()
    for cp in sends:
        cp.wait_send()
    for cp in local:
        cp.wait()


def _exchange(*args):
    _exchange_start(*args)
    _exchange_wait(*args)


def _sems(n):
    return [pltpu.SemaphoreType.DMA((n * (N_DEV - 1),)), pltpu.SemaphoreType.DMA((n * (N_DEV - 1),)),
            pltpu.SemaphoreType.DMA((n,))]


_ANY = pl.BlockSpec(memory_space=pl.ANY)


class _Ride:
    def __init__(self, kind, arrays):
        self.kind, self.arrays, self.n = kind, list(arrays), len(arrays)

    def out_shape(self):
        lead = () if self.kind == "scatter" else (N_DEV,)
        return [jax.ShapeDtypeStruct(lead + a.shape, a.dtype) for a in self.arrays]

    def _args(self, x_refs, out_refs, sems):
        if self.kind == "gather":
            return (self.n, lambda a, peer: x_refs[a], out_refs, lambda a: x_refs[a]) + tuple(sems)
        me = _mesh_pos()
        return (self.n, lambda a, peer: x_refs[a].at[_rank(peer)], out_refs, lambda a: x_refs[a].at[_rank(me)]) + tuple(sems)

    def start(self, x_refs, out_refs, sems):
        if self.kind == "gather2":
            _two_level_gather(x_refs, out_refs, *sems)[0]()
        else:
            _exchange_start(*self._args(x_refs, out_refs, sems))

    def wait(self, x_refs, out_refs, sems):
        if self.kind == "gather2":
            _two_level_gather(x_refs, out_refs, *sems)[1]()
        else:
            _exchange_wait(*self._args(x_refs, out_refs, sems))


def _pallas(body, name, grid, operands, in_specs, out_specs, out_shape, scratch=(), sem=None, ride=None, prefetch=()):
    n_pf, n_in, n_out, n_scr = len(prefetch), len(in_specs), len(out_specs), len(scratch)
    nr = ride.n if ride else 0

    def wrapped(*refs):
        refs = list(refs)
        pf, refs = refs[:n_pf], refs[n_pf:]
        ins, r_in = refs[:n_in], refs[n_in:n_in + nr]
        outs, r_out = refs[n_in + nr:n_in + nr + n_out], refs[n_in + nr + n_out:n_in + 2 * nr + n_out]
        rest = refs[n_in + 2 * nr + n_out:]
        scr, sems = rest[:n_scr], rest[n_scr:]
        if ride:
            first, last = None, None
            for d, size in enumerate(grid):
                i = pl.program_id(d)
                first = (i == 0) if first is None else jnp.logical_and(first, i == 0)
                last = (i == size - 1) if last is None else jnp.logical_and(last, i == size - 1)

            @pl.when(first)
            def _():
                ride.start(r_in, r_out, sems)

        body(pf + ins, outs, scr)
        if ride:
            @pl.when(last)
            def _():
                ride.wait(r_in, r_out, sems)

    grid_spec = pltpu.PrefetchScalarGridSpec(
        num_scalar_prefetch=n_pf, grid=grid,
        in_specs=list(in_specs) + [_ANY] * nr, out_specs=tuple(list(out_specs) + [_ANY] * nr),
        scratch_shapes=list(scratch) + (_sems(nr) if ride else []))
    sem = tuple(sem) if sem is not None and not ride else ("arbitrary",) * len(grid)
    return list(_pcall(
        wrapped, name=name, grid_spec=grid_spec,
        out_shape=tuple(list(out_shape) + (ride.out_shape() if ride else [])),
        compiler_params=_params(sem),
    )(*(list(prefetch) + list(operands) + (ride.arrays if ride else []))))


def _two_level_gather(x_refs, out_refs, send_sems, recv_sems, local_sems):
    n = len(x_refs)
    x, y, c = _mesh_pos()
    me, sibling = (x, y, c), (x, y, 1 - c)
    chips = [(1 - x, y), (x, 1 - y), (1 - x, 1 - y)]

    def copy(a, k, block, to, src=None):
        rows = out_refs[a].at[_rank(block)]
        return pltpu.make_async_remote_copy(
            src_ref=rows if src is None else src, dst_ref=rows,
            send_sem=send_sems.at[a * (N_DEV - 1) + k], recv_sem=recv_sems.at[a * (N_DEV - 1) + k],
            device_id=to, device_id_type=pl.DeviceIdType.MESH)

    def own_copies():
        mine = [pltpu.make_async_copy(x_refs[a], out_refs[a].at[_rank(me)], local_sems.at[a]) for a in range(n)]
        first = []
        for a in range(n):
            first.append(copy(a, 0, me, sibling, src=x_refs[a]))
            first += [copy(a, 1 + j, me, (*chip, c), src=x_refs[a]) for j, chip in enumerate(chips)]
        return mine, first

    def start():
        mine, first = own_copies()
        for cp in mine + first:
            cp.start()

    def finish():
        mine, first = own_copies()
        passed = []
        for j, chip in enumerate(chips):
            for a in range(n):
                copy(a, 1 + j, (*chip, c), me).wait_recv()
                passed.append(copy(a, 4 + j, (*chip, c), sibling))
                passed[-1].start()
        for a in range(n):
            copy(a, 0, sibling, me).wait_recv()
            for j, chip in enumerate(chips):
                copy(a, 4 + j, (*chip, 1 - c), me).wait_recv()
        for cp in first + passed:
            cp.wait_send()
        for cp in mine:
            cp.wait()

    return start, finish


def _all_gather(shards, name):
    n = len(shards)

    def body(*refs):
        start, finish = _two_level_gather(refs[:n], refs[n:2 * n], *refs[2 * n:])
        start()
        finish()

    return _pcall(
        body, name=name, in_specs=[_ANY] * n, out_specs=tuple([_ANY] * n),
        out_shape=tuple(jax.ShapeDtypeStruct((N_DEV,) + s.shape, s.dtype) for s in shards),
        scratch_shapes=_sems(n),
    )(*shards)


def _all_reduce_small(v):
    R, C = v.shape

    def body(x_ref, out_ref, buf_ref, send_sems, recv_sems, local_sems):
        _exchange(1, lambda a, peer: x_ref, [buf_ref], lambda a: x_ref, send_sems, recv_sems, local_sems)
        total = buf_ref[0]
        for j in range(1, N_DEV):
            total = total + buf_ref[j]
        out_ref[...] = total

    vm = pl.BlockSpec(memory_space=pltpu.VMEM)
    return _pcall(
        body, name="all_reduce_small", in_specs=[vm], out_specs=vm,
        out_shape=jax.ShapeDtypeStruct((R, C), F32),
        scratch_shapes=[pltpu.VMEM((N_DEV, R, C), F32)] + _sems(1),
    )(v)


def _sum_slots(layers, name):
    L = len(layers)
    shape = layers[0].shape[1:]
    C = shape[-1]
    R = layers[0].size // (N_DEV * C)
    tr = R
    while N_DEV * tr * C * 4 > (4 << 20) and tr % 32 == 0:
        tr //= 2

    def body(*refs):
        o_ref = refs[L]
        for li in range(L):
            @pl.when(pl.program_id(0) == li)
            def _():
                total = refs[li][0].astype(F32)
                for j in range(1, N_DEV):
                    total = total + refs[li][j].astype(F32)
                o_ref[0] = total

    in_specs = [pl.BlockSpec((N_DEV, tr, C), lambda l, i, li=li: (0, jnp.where(l == li, i, 0), 0)) for li in range(L)]
    return _pcall(
        body, name=name, grid=(L, R // tr),
        in_specs=in_specs,
        out_specs=pl.BlockSpec((1, tr, C), lambda l, i: (l, i, 0)),
        out_shape=jax.ShapeDtypeStruct((L, R, C), F32),
        compiler_params=_params(("parallel", "parallel")),
    )(*[p.reshape(N_DEV, R, C) for p in layers]).reshape((L,) + shape)


def _adamw_update(w, g, m, v):
    nm = ADAM_B1 * m + (1.0 - ADAM_B1) * g
    nv = ADAM_B2 * v + (1.0 - ADAM_B2) * (g * g)
    m_hat = nm / (1.0 - ADAM_B1 ** ADAM_STEP)
    v_hat = nv / (1.0 - ADAM_B2 ** ADAM_STEP)
    return -ADAM_LR * (m_hat / (jnp.sqrt(v_hat) + ADAM_EPS) + ADAM_WD * w), nm, nv


def _adamw_from_slots(w, layers, m, v, name, ride=None):
    L = len(layers)
    shape = w.shape
    C = shape[-1]
    R = w.size // (L * C)
    fits = [t for t in range(16, R + 1, 16) if R % t == 0 and N_DEV * t * C * 2 <= (1 << 20)]
    tr = max(fits) if fits else R

    def body(ins, outs, scratch):
        w_ref, m_ref, v_ref = ins[L:]
        g_ref, d_ref, nm_ref, nv_ref = outs
        for li in range(L):
            @pl.when(pl.program_id(0) == li)
            def _():
                g = ins[li][0].astype(F32)
                for j in range(1, N_DEV):
                    g = g + ins[li][j].astype(F32)
                g_ref[0] = g
                d_ref[0], nm_ref[0], nv_ref[0] = _adamw_update(w_ref[0], g, m_ref[0], v_ref[0])

    slot_specs = [pl.BlockSpec((N_DEV, tr, C), lambda l, i, li=li: (0, jnp.where(l == li, i, 0), 0)) for li in range(L)]
    blk = pl.BlockSpec((1, tr, C), lambda l, i: (l, i, 0))
    sds = jax.ShapeDtypeStruct((L, R, C), F32)
    outs = _pallas(body, name, (L, R // tr),
                   [p.reshape(N_DEV, R, C) for p in layers] + [t.reshape(L, R, C) for t in (w, m, v)],
                   slot_specs + [blk] * 3, [blk] * 4, [sds] * 4, sem=("parallel", "parallel"), ride=ride)
    return [o.reshape(shape) for o in outs[:4]] + outs[4:]


def _adamw(w, g, m, v, name):
    shape = w.shape
    C = shape[-1]
    R = w.size // C
    tr = R
    while tr * C * 4 > (1 << 20) and tr % 16 == 0:
        tr //= 2

    def body(w_ref, g_ref, m_ref, v_ref, d_ref, nm_ref, nv_ref):
        d_ref[...], nm_ref[...], nv_ref[...] = _adamw_update(w_ref[...], g_ref[...], m_ref[...], v_ref[...])

    blk = pl.BlockSpec((tr, C), lambda i: (i, 0))
    sds = jax.ShapeDtypeStruct((R, C), F32)
    outs = _pcall(
        body, name=name, grid=(R // tr,),
        in_specs=[blk] * 4, out_specs=(blk,) * 3, out_shape=(sds,) * 3,
        compiler_params=_params(("parallel",)),
    )(*(t.reshape(R, C) for t in (w, g, m, v)))
    return tuple(o.reshape(shape) for o in outs)


_REPLICATED = ("kv_in_norm", "kv_latent_norm", "attn_norm", "q_latent_norm", "ffn_norm", "final_norm")
_WEIGHTS = ("pool_norm", "pool_w", "pool_scale", "kv_in_norm", "w_kv_a", "kv_latent_norm", "w_kv_b", "attn_norm",
            "w_q_a", "q_latent_norm", "w_q_b", "w_o", "ffn_norm", "w_gate", "w_up", "w_down", "final_norm")


def _ffn_fwd(x, norm_g, wg, wu, wd, tag, gather=None):
    S, D = x.shape
    fs = wg.shape[1]
    h, ht = _rmsnorm(x, norm_g, "ffn_norm" + tag, transposed_too=True)
    g, u, a, *gathered = _ffn_up(h, wg, wu, "ffn_up" + tag, ride=gather)
    if gather is not None:
        wd = gathered[0].reshape(N_DEV * fs, D)
    y = _ffn_down(a, wd, x, "ffn_down" + tag)
    return y, (ht, g, u, a), wd, gathered


def _ffn_bwd(x, norm_g, wg, wu, wd, saved, dy, dyb, tag, with_bf16, scatter=None):
    S, D = x.shape
    ht, g, u, a = saved
    fs = g.shape[1]
    tk = _tile(S, 2 * ROW_TILE)
    nkb = S // tk
    dg, du, *got_rider = _ffn_dact(dyb, wd, g, u, "ffn_dact" + tag, ride=scatter)
    dwd = _mm("ffn_dwd" + tag, "tn", a, dyb, (fs, D, tk), (N_DEV, 1, nkb),
              lambda j, q, k: (j * nkb + k, 0), lambda j, q, k: (k, 0), lambda j, q, k: (j, 0), (N_DEV * fs, D), BF16)
    dwg, (got_dwd,) = _slots_dw_t("ffn_dwg" + tag, ht, dg, ride=_Ride("scatter", [dwd.reshape(N_DEV, fs, D)]))
    dwu, (got_dwg,) = _slots_dw_t("ffn_dwu" + tag, ht, du, ride=_Ride("scatter", [dwg.reshape(N_DEV, D, fs)]))
    dh, got_dwu = _ffn_dh(dg, du, wg, wu, S, "ffn_dh" + tag, ride=_Ride("scatter", [dwu.reshape(N_DEV, D, fs)]))
    outs = _rmsnorm_bwd(x, norm_g, dh, dy, "ffn_norm_bwd" + tag, with_bf16=with_bf16)
    return outs, got_rider, (got_dwg, got_dwu, got_dwd)


def kernel(x, positions, pool_norm, pool_w, pool_scale, kv_in_norm, w_kv_a, kv_latent_norm, w_kv_b, attn_norm, w_q_a, q_latent_norm, w_q_b, w_o, ffn_norm, w_gate, w_up, w_down, final_norm, loss_target, m_pool_norm, m_pool_w, m_pool_scale, m_kv_in_norm, m_w_kv_a, m_kv_latent_norm, m_w_kv_b, m_attn_norm, m_w_q_a, m_q_latent_norm, m_w_q_b, m_w_o, m_ffn_norm, m_w_gate, m_w_up, m_w_down, m_final_norm, v_pool_norm, v_pool_w, v_pool_scale, v_kv_in_norm, v_w_kv_a, v_kv_latent_norm, v_w_kv_b, v_attn_norm, v_w_q_a, v_q_latent_norm, v_w_q_b, v_w_o, v_ffn_norm, v_w_gate, v_w_up, v_w_down, v_final_norm):
    env = dict(locals())
    weights = {n: env[n] for n in _WEIGHTS}
    m_in = {n: env["m_" + n] for n in _WEIGHTS}
    v_in = {n: env["v_" + n] for n in _WEIGHTS}

    S, D = x.shape[1], x.shape[2]
    xs = x.reshape(S, D)
    target = loss_target.reshape(S, D)
    G, gws, gw = pool_w.shape[1:]
    Ds = w_kv_a.shape[0]
    R_kv, kvw = w_kv_b.shape
    hpd = kvw // (QK_NOPE_DIM + V_DIM)
    H = hpd * N_DEV
    R_q = w_q_a.shape[2]
    fs = w_gate.shape[2]
    bf = lambda t: t.astype(BF16)

    gains = jnp.concatenate([pool_norm, pool_scale], axis=0)
    g_pool_w, g_gains = _all_gather([bf(pool_w[0]), gains], "gather_first")
    wqb_pad = jnp.pad(w_q_b.reshape(R_q, hpd, QK_DIM), ((0, 0), (0, 0), (0, HEAD_PAD - QK_DIM))).reshape(R_q, hpd * HEAD_PAD)
    gather_mla = _Ride("gather", [bf(w_down[0]), bf(jnp.pad(w_kv_a, ((0, 0), (0, 128 - QK_ROPE_DIM)))), bf(w_kv_b),
                                  bf(w_q_a[0]), bf(wqb_pad), bf(w_o[0])])
    gather_ffn1 = _Ride("gather", [bf(w_gate[1]), bf(w_up[1]), bf(w_down[1])])

    wp = jnp.swapaxes(g_pool_w, 0, 1).reshape(G, gw, gw)
    gains_full = jnp.swapaxes(g_gains, 0, 1).reshape(2, D)
    g_pool_norm, g_pool_scale = gains_full[0:1], gains_full[1:2]

    g_kv_in = kv_in_norm.reshape(1, D)
    g_kv_lat = kv_latent_norm.reshape(1, R_kv)
    g_attn = attn_norm.reshape(1, D)
    g_q_lat = q_latent_norm.reshape(1, R_q)
    g_final = final_norm.reshape(1, D)

    half = QK_ROPE_DIM // 2
    inv_freq = ROPE_BASE ** (-jnp.arange(half, dtype=F32) / half)
    ang = positions.reshape(S).astype(F32)[:, None] * inv_freq
    cos, sin = jnp.cos(ang), jnp.sin(ang)
    zeros = jnp.zeros((S, 128 - QK_ROPE_DIM), F32)
    cos_t = jnp.concatenate([cos, cos, zeros], axis=1)
    sin_t = jnp.concatenate([-sin, sin, zeros], axis=1)

    diff, g_wg0 = _pool_pre(xs, g_pool_norm, ride=_Ride("gather2", [bf(w_gate[0])]))
    x1, g_wu0 = _pool_mix(diff, wp, g_pool_scale, xs, ride=_Ride("gather2", [bf(w_up[0])]))
    wg0, wu0 = g_wg0.reshape(N_DEV * D, fs), g_wu0.reshape(N_DEV * D, fs)
    x2, ffn0, wd0, (_, g_wkva, g_wkvb, g_wqa, g_wqb, g_wo) = _ffn_fwd(x1, ffn_norm[0:1], wg0, wu0, None, "0", gather_mla)
    wkva = g_wkva.reshape(D, R_kv + 128)
    wkvb = g_wkvb.reshape(N_DEV * R_kv, kvw)
    wqa = g_wqa.reshape(D, R_q)
    wqb = g_wqb.reshape(N_DEV * R_q, hpd * HEAD_PAD)
    wo = g_wo.reshape(H * V_DIM, D)

    hk, ha = _rmsnorm_pair(x2, g_kv_in, g_attn, "kv_attn_norm")
    kv = _mm_plain("kv_a", "nn", hk, wkva)
    ckv, kpe = _kv_mid(kv, g_kv_lat, cos_t, sin_t)
    kvu = _small_slots_out("kv_b", ckv, wkvb, BF16)

    ql = _mm_plain("q_a", "nn", ha, wqa)
    qn = _rmsnorm(ql, g_q_lat, "q_latent_norm")
    tab = pl.BlockSpec((_tile(S, ROW_TILE), 128), lambda i: (i, 0))
    qr = _small_slots_out("q_b", qn, wqb, BF16, epilogue=(_rope_heads, [cos_t, sin_t], [tab, tab]))
    o, lse, g_wg1, g_wu1, g_wd1 = _attn_fwd(qr, kvu, kpe, H, ride=gather_ffn1)
    wg1, wu1, wd1 = g_wg1.reshape(N_DEV * D, fs), g_wu1.reshape(N_DEV * D, fs), g_wd1.reshape(N_DEV * fs, D)
    x3 = _mm_plain("attn_out", "nn", o, wo, res=x2)
    x4, ffn1, _, _ = _ffn_fwd(x3, ffn_norm[1:2], wg1, wu1, wd1, "1")

    loss_part, dx4, dx4b, d_final = _loss_head(x4, g_final, target)

    (dx3, dx3b, d_ffn1), _, got_ffn1 = _ffn_bwd(x3, ffn_norm[1:2], wg1, wu1, wd1, ffn1, dx4, dx4b, "1", True)

    do = _mm_plain("attn_out_dx", "nt", dx3b, wo, out_dtype=BF16)
    dwo = _mm_plain("attn_out_dw", "tn", o, dx3b, out_dtype=BF16)
    dq, dkvu, dkpe = _attn_bwd(qr, kvu, kpe, do, o, lse.reshape(H, 1, S), cos_t, sin_t, H)

    dqn = _small_slots_in_nt("q_b_dx", dq, wqb, S)
    dwqb = _small_slots_dw("q_b_dw", qn, dq, BF16)
    dql, d_q_lat = _rmsnorm_bwd(ql, g_q_lat, dqn, None, "q_latent_norm_bwd", out_dtype=BF16)
    dha = _mm_plain("q_a_dx", "nt", dql, wqa)
    dwqa = _mm_plain("q_a_dw", "tn", ha, dql, out_dtype=BF16)

    dckv = _small_slots_in_nt("kv_b_dx", dkvu, wkvb, S)
    dwkvb = _small_slots_dw("kv_b_dw", ckv, dkvu, BF16)
    dkv, d_kv_lat = _kv_mid_bwd(kv, g_kv_lat, dckv, dkpe, cos_t, sin_t)
    dhk = _mm_plain("kv_a_dx", "nt", dkv, wkva)
    dwkva = _mm_plain("kv_a_dw", "tn", hk, dkv, out_dtype=BF16)
    dx2, dx2b, d_attn, d_kv_in = _rmsnorm_pair_bwd(x2, g_attn, dha, g_kv_in, dhk, dx3, "kv_attn_norm_bwd")

    scatter_mla = _Ride("scatter", [dwkva.reshape(N_DEV, Ds, R_kv + 128), dwkvb.reshape(N_DEV, R_kv, kvw),
                                    dwqa.reshape(N_DEV, Ds, R_q), dwqb.reshape(N_DEV, R_q, hpd * HEAD_PAD),
                                    dwo.reshape(N_DEV, H * V_DIM // N_DEV, D)])
    (dx1, d_ffn0), got_mla, got_ffn0 = _ffn_bwd(x1, ffn_norm[0:1], wg0, wu0, wd0, ffn0, dx2, dx2b, "0", False,
                                                  scatter=scatter_mla)

    ddiff, dwp, d_pool_scale = _pool_mix_bwd(diff, wp, g_pool_scale, dx1)
    grad_x, d_pool_norm = _pool_pre_bwd(xs, g_pool_norm, ddiff, dx1)

    d_gains = jnp.swapaxes(jnp.concatenate([d_pool_norm, d_pool_scale], axis=0).reshape(2, N_DEV, D // N_DEV), 0, 1)
    scatter_pool = _Ride("scatter", [jnp.swapaxes(dwp.reshape(G, N_DEV, gws, gw), 0, 1), d_gains])

    grads, delta_w, new_m, new_v = {}, {}, {}, {}
    ffn_slots = {n: [r0, r1] for n, r0, r1 in zip(("w_gate", "w_up", "w_down"), got_ffn0, got_ffn1)}
    got_pool = []
    for n, ride in zip(ffn_slots, (scatter_pool, None, None)):
        grads[n], delta_w[n], new_m[n], new_v[n], *got = _adamw_from_slots(weights[n], ffn_slots[n], m_in[n], v_in[n],
                                                                            "adamw_" + n, ride=ride)
        got_pool += got
    got_pool_w, got_gains = got_pool
    received = {n: [r] for n, r in zip(("w_kv_a", "w_kv_b", "w_q_a", "w_q_b", "w_o"), got_mla)}
    received.update(pool_w=[got_pool_w], gains=[got_gains])
    sums = {n: _sum_slots(r, "sum_" + n) for n, r in received.items()}
    grads.update({
        "pool_w": sums["pool_w"],
        "w_kv_a": sums["w_kv_a"][0, :, :R_kv + QK_ROPE_DIM],
        "w_kv_b": sums["w_kv_b"][0],
        "w_q_a": sums["w_q_a"],
        "w_q_b": sums["w_q_b"].reshape(R_q, hpd, HEAD_PAD)[:, :, :QK_DIM].reshape(1, R_q, hpd * QK_DIM),
        "w_o": sums["w_o"],
        "pool_norm": sums["gains"][0, 0:1],
        "pool_scale": sums["gains"][0, 1:2],
    })

    small = {"kv_in_norm": d_kv_in, "kv_latent_norm": d_kv_lat, "attn_norm": d_attn, "q_latent_norm": d_q_lat,
             "ffn_norm": jnp.concatenate([d_ffn0, d_ffn1], axis=0), "final_norm": d_final}
    small_packed = jnp.concatenate([small[n].reshape(-1) for n in _REPLICATED] + [loss_part.reshape(-1)])
    pad = (-small_packed.shape[0]) % (8 * 128)
    reduced = _all_reduce_small(jnp.pad(small_packed, (0, pad)).reshape(-1, 128)).reshape(-1)
    off = 0
    for n in _REPLICATED:
        grads[n] = reduced[off:off + weights[n].size].reshape(weights[n].shape)
        off += weights[n].size
    loss = reduced[off]

    for n in _WEIGHTS:
        if n not in ffn_slots:
            delta_w[n], new_m[n], new_v[n] = _adamw(weights[n], grads[n], m_in[n], v_in[n], "adamw_" + n)

    return (loss, grad_x.reshape(x.shape), *[grads[n] for n in _WEIGHTS], *[delta_w[n] for n in _WEIGHTS],
            *[new_m[n] for n in _WEIGHTS], *[new_v[n] for n in _WEIGHTS])
```

```python
import math

import jax
import jax.numpy as jnp
import numpy as np
from jax import lax
from jax.experimental import pallas as pl
from jax.experimental.pallas import tpu as pltpu

F32 = jnp.float32
BF16 = jnp.bfloat16

N_DEV = 8
POOL_WINDOWS = (2, 4, 8, 16)
POOL_HALO = 16
QK_NOPE_DIM = 128
QK_ROPE_DIM = 64
QK_DIM = QK_NOPE_DIM + QK_ROPE_DIM
HEAD_PAD = 256
V_DIM = 128
ROPE_BASE = 10000.0
ATTN_SCALE = 1.0 / math.sqrt(QK_DIM)
EXP2_SCALE = ATTN_SCALE * math.log2(math.e)
NORM_EPS = 1e-6
ADAM_LR = 0.001
ADAM_B1 = 0.9
ADAM_B2 = 0.999
ADAM_EPS = 1e-08
ADAM_WD = 0.01
ADAM_STEP = 10

ROW_TILE = 512
ATT_BLOCK = 1024
ATT_LOOKAHEAD = 1
ATT_ROW_CHUNK = 256
FFN_SLOTS = 2
MM_TN, MM_TK = 512, 512
VMEM_LIMIT = 48 * 1024 * 1024

_pcall = pl.pallas_call


def _params(sem):
    return pltpu.CompilerParams(dimension_semantics=sem, vmem_limit_bytes=VMEM_LIMIT)


def _tile(dim, default):
    if dim % default == 0:
        return default
    assert dim <= 2 * default, (dim, default)
    return dim


def _row_tile(rows, width):
    ts = _tile(rows, ROW_TILE)
    while ts * width * 4 > (2 << 20) and ts % 16 == 0:
        ts //= 2
    return ts


def _rms_scale(x):
    return lax.rsqrt(jnp.mean(x * x, axis=-1, keepdims=True) + NORM_EPS)


def _rms_bwd(x, g, dy):
    r = _rms_scale(x)
    xn = x * r
    u = dy * g
    dx = r * (u - xn * jnp.mean(u * xn, axis=-1, keepdims=True))
    return dx, dy * xn


def _swap_halves(x):
    lane = lax.broadcasted_iota(jnp.int32, x.shape, 1)
    return jnp.where(lane < QK_ROPE_DIM // 2, pltpu.roll(x, 128 - QK_ROPE_DIM // 2, 1), pltpu.roll(x, QK_ROPE_DIM // 2, 1))


def _rope(x, cos_t, sin_t):
    return x * cos_t + _swap_halves(x) * sin_t


def _rmsnorm(x, g, name, transposed_too=False):
    S, D = x.shape
    ts = _row_tile(S, D)

    def body(x_ref, g_ref, o_ref, *t_ref):
        xf = x_ref[...]
        y = xf * _rms_scale(xf) * g_ref[...]
        o_ref[...] = y.astype(o_ref.dtype)
        if transposed_too:
            t_ref[0][...] = y.T.astype(o_ref.dtype)

    row = pl.BlockSpec((ts, D), lambda i: (i, 0))
    outs = _pcall(
        body, name=name, grid=(S // ts,),
        in_specs=[row, pl.BlockSpec((1, D), lambda i: (0, 0))],
        out_specs=(row, pl.BlockSpec((D, ts), lambda i: (0, i))) if transposed_too else row,
        out_shape=((jax.ShapeDtypeStruct((S, D), BF16), jax.ShapeDtypeStruct((D, S), BF16)) if transposed_too
                   else jax.ShapeDtypeStruct((S, D), BF16)),
        compiler_params=_params(("parallel",)),
    )(x, g)
    return outs


def _rmsnorm_bwd(x, g, dy, res, name, out_dtype=F32, with_bf16=False):
    S, D = x.shape
    ts = _row_tile(S, D)
    has_res = res is not None

    def body(*refs):
        refs = list(refs)
        x_ref, g_ref, dy_ref = refs[:3]
        res_ref = refs[3] if has_res else None
        outs = refs[3 + has_res:]
        dx_ref, dg_ref = outs[0], outs[-1]
        dx, dgt = _rms_bwd(x_ref[...], g_ref[...], dy_ref[...].astype(F32))
        if has_res:
            dx = res_ref[...] + dx
        dx_ref[...] = dx.astype(dx_ref.dtype)
        if with_bf16:
            outs[1][...] = dx.astype(BF16)

        @pl.when(pl.program_id(0) == 0)
        def _():
            dg_ref[...] = jnp.zeros_like(dg_ref)

        dg_ref[...] += jnp.sum(dgt, axis=0, keepdims=True)

    row = pl.BlockSpec((ts, D), lambda i: (i, 0))
    vec = pl.BlockSpec((1, D), lambda i: (0, 0))
    out_specs = [row] + ([row] if with_bf16 else []) + [vec]
    out_shape = ([jax.ShapeDtypeStruct((S, D), out_dtype)] + ([jax.ShapeDtypeStruct((S, D), BF16)] if with_bf16 else [])
                 + [jax.ShapeDtypeStruct((1, D), F32)])
    return _pcall(
        body, name=name, grid=(S // ts,),
        in_specs=[row, vec, row] + ([row] if has_res else []),
        out_specs=tuple(out_specs), out_shape=tuple(out_shape),
        compiler_params=_params(("arbitrary",)),
    )(*([x, g, dy] + ([res] if has_res else [])))


def _rmsnorm_pair(x, g1, g2, name):
    S, D = x.shape
    ts = _row_tile(S, D)

    def body(x_ref, g1_ref, g2_ref, o1_ref, o2_ref):
        xf = x_ref[...]
        xn = xf * _rms_scale(xf)
        o1_ref[...] = (xn * g1_ref[...]).astype(o1_ref.dtype)
        o2_ref[...] = (xn * g2_ref[...]).astype(o2_ref.dtype)

    row = pl.BlockSpec((ts, D), lambda i: (i, 0))
    vec = pl.BlockSpec((1, D), lambda i: (0, 0))
    sds = jax.ShapeDtypeStruct((S, D), BF16)
    return _pcall(
        body, name=name, grid=(S // ts,),
        in_specs=[row, vec, vec], out_specs=(row, row), out_shape=(sds, sds),
        compiler_params=_params(("parallel",)),
    )(x, g1, g2)


def _rmsnorm_pair_bwd(x, g1, dy1, g2, dy2, res, name):
    S, D = x.shape
    ts = _row_tile(S, D)

    def body(x_ref, g1_ref, dy1_ref, g2_ref, dy2_ref, res_ref, dx_ref, dxb_ref, dg1_ref, dg2_ref):
        xf = x_ref[...]
        dx1, dgt1 = _rms_bwd(xf, g1_ref[...], dy1_ref[...])
        dx2, dgt2 = _rms_bwd(xf, g2_ref[...], dy2_ref[...])
        dx = (res_ref[...] + dx1) + dx2
        dx_ref[...] = dx
        dxb_ref[...] = dx.astype(BF16)

        @pl.when(pl.program_id(0) == 0)
        def _():
            dg1_ref[...] = jnp.zeros_like(dg1_ref)
            dg2_ref[...] = jnp.zeros_like(dg2_ref)

        dg1_ref[...] += jnp.sum(dgt1, axis=0, keepdims=True)
        dg2_ref[...] += jnp.sum(dgt2, axis=0, keepdims=True)

    row = pl.BlockSpec((ts, D), lambda i: (i, 0))
    vec = pl.BlockSpec((1, D), lambda i: (0, 0))
    return _pcall(
        body, name=name, grid=(S // ts,),
        in_specs=[row, vec, row, vec, row, row], out_specs=(row, row, vec, vec),
        out_shape=(jax.ShapeDtypeStruct((S, D), F32), jax.ShapeDtypeStruct((S, D), BF16),
                   jax.ShapeDtypeStruct((1, D), F32), jax.ShapeDtypeStruct((1, D), F32)),
        compiler_params=_params(("arbitrary",)),
    )(x, g1, dy1, g2, dy2, res)


def _pool_pre(x, g, ride=None):
    S, D = x.shape
    ts = _row_tile(S, D)
    gw = D // len(POOL_WINDOWS)
    hb = ts // POOL_HALO

    def body(ins, outs, scratch):
        (x_ref, halo_ref, g_ref), (o_ref,) = ins, outs
        i = pl.program_id(0)
        xx = jnp.concatenate([halo_ref[...], x_ref[...]], axis=0)
        h = xx * _rms_scale(xx) * g_ref[...]
        t = i * ts - POOL_HALO + lax.broadcasted_iota(jnp.int32, (ts + POOL_HALO, 1), 0)
        h = jnp.where(t >= 0, h, 0.0)
        tf = t[POOL_HALO:].astype(F32)
        for gi, w in enumerate(POOL_WINDOWS):
            hg = h[:, gi * gw:(gi + 1) * gw]
            acc, span = hg, 1
            while span < w:
                acc = acc + pltpu.roll(acc, span, 0)
                span *= 2
            count = jnp.minimum(tf + 1.0, float(w))
            diff = acc[POOL_HALO:] / count - hg[POOL_HALO:]
            o_ref[:, gi * gw:(gi + 1) * gw] = diff.astype(o_ref.dtype)

    return _pallas(body, "pool_pre", (S // ts,), [x, x, g],
                   [pl.BlockSpec((ts, D), lambda i: (i, 0)),
                    pl.BlockSpec((POOL_HALO, D), lambda i: (jnp.maximum(i * hb - 1, 0), 0)),
                    pl.BlockSpec((1, D), lambda i: (0, 0))],
                   [pl.BlockSpec((ts, D), lambda i: (i, 0))], [jax.ShapeDtypeStruct((S, D), BF16)],
                   sem=("parallel",), ride=ride)


def _pool_pre_bwd(x, g, dd, res):
    S, D = x.shape
    ts = _row_tile(S, D)
    gw = D // len(POOL_WINDOWS)
    hb = ts // POOL_HALO
    last_halo = S // POOL_HALO - 1

    def body(x_ref, g_ref, dd_ref, halo_ref, res_ref, dx_ref, dg_ref):
        i = pl.program_id(0)
        ee = jnp.concatenate([dd_ref[...], halo_ref[...]], axis=0)
        t = i * ts + lax.broadcasted_iota(jnp.int32, (ts + POOL_HALO, 1), 0)
        ee = jnp.where(t < S, ee, 0.0)
        tf = t.astype(F32)
        n = ts + POOL_HALO
        for gi, w in enumerate(POOL_WINDOWS):
            eg = ee[:, gi * gw:(gi + 1) * gw]
            acc, span = eg / jnp.minimum(tf + 1.0, float(w)), 1
            while span < w:
                acc = acc + pltpu.roll(acc, n - span, 0)
                span *= 2
            dx_ref[:, gi * gw:(gi + 1) * gw] = acc[:ts] - eg[:ts]
        dx, dgt = _rms_bwd(x_ref[...], g_ref[...], dx_ref[...])
        dx_ref[...] = res_ref[...] + dx

        @pl.when(i == 0)
        def _():
            dg_ref[...] = jnp.zeros_like(dg_ref)

        dg_ref[...] += jnp.sum(dgt, axis=0, keepdims=True)

    row = pl.BlockSpec((ts, D), lambda i: (i, 0))
    vec = pl.BlockSpec((1, D), lambda i: (0, 0))
    return _pcall(
        body, name="pool_pre_bwd", grid=(S // ts,),
        in_specs=[row, vec, row,
                  pl.BlockSpec((POOL_HALO, D), lambda i: (jnp.minimum((i + 1) * hb, last_halo), 0)), row],
        out_specs=(row, vec),
        out_shape=(jax.ShapeDtypeStruct((S, D), F32), jax.ShapeDtypeStruct((1, D), F32)),
        compiler_params=_params(("arbitrary",)),
    )(x, g, dd, dd, res)


def _pool_mix(diff, w, scale, x, ride=None):
    S, D = x.shape
    G, gw, _ = w.shape
    ts = _tile(S, ROW_TILE)

    def body(ins, outs, scratch):
        (d_ref, w_ref, s_ref, x_ref), (o_ref,) = ins, outs
        mo = jnp.dot(d_ref[...], w_ref[0], preferred_element_type=F32)
        o_ref[...] = x_ref[...] + mo * s_ref[...]

    blk = pl.BlockSpec((ts, gw), lambda i, g: (i, g))
    return _pallas(body, "pool_mix", (S // ts, G), [diff, w, scale, x],
                   [blk, pl.BlockSpec((1, gw, gw), lambda i, g: (g, 0, 0)), pl.BlockSpec((1, gw), lambda i, g: (0, g)), blk],
                   [blk], [jax.ShapeDtypeStruct((S, D), F32)], sem=("parallel", "parallel"), ride=ride)


def _pool_mix_bwd(diff, w, scale, dy):
    S, D = dy.shape
    G, gw, _ = w.shape
    ts = _tile(S, ROW_TILE)

    def body(d_ref, w_ref, s_ref, dy_ref, dd_ref, dw_ref, ds_ref):
        i = pl.program_id(1)
        d = d_ref[...]
        dy = dy_ref[...]
        mo = jnp.dot(d, w_ref[0], preferred_element_type=F32)
        dmo = (dy * s_ref[...]).astype(BF16)
        dd_ref[...] = lax.dot_general(dmo, w_ref[0], (((1,), (1,)), ((), ())), preferred_element_type=F32)

        @pl.when(i == 0)
        def _():
            dw_ref[...] = jnp.zeros_like(dw_ref)
            ds_ref[...] = jnp.zeros_like(ds_ref)

        dw_ref[0] += lax.dot_general(d, dmo, (((0,), (0,)), ((), ())), preferred_element_type=F32)
        ds_ref[...] += jnp.sum(dy * mo, axis=0, keepdims=True)

    blk = pl.BlockSpec((ts, gw), lambda g, i: (i, g))
    wspec = pl.BlockSpec((1, gw, gw), lambda g, i: (g, 0, 0))
    vec = pl.BlockSpec((1, gw), lambda g, i: (0, g))
    return _pcall(
        body, name="pool_mix_bwd", grid=(G, S // ts),
        in_specs=[blk, wspec, vec, blk],
        out_specs=(blk, wspec, vec),
        out_shape=(jax.ShapeDtypeStruct((S, D), F32), jax.ShapeDtypeStruct((G, gw, gw), F32),
                   jax.ShapeDtypeStruct((1, D), F32)),
        compiler_params=_params(("parallel", "arbitrary")),
    )(diff, w, scale, dy)


def _loss_head(x, g, target):
    S, D = x.shape
    ts = _row_tile(S, D)

    def body(x_ref, g_ref, t_ref, loss_ref, dx_ref, dxb_ref, dg_ref):
        xf = x_ref[...]
        gv = g_ref[...]
        err = xf * _rms_scale(xf) * gv - t_ref[...]
        dx, dgt = _rms_bwd(xf, gv, err * (1.0 / D))
        dx_ref[...] = dx
        dxb_ref[...] = dx.astype(BF16)

        @pl.when(pl.program_id(0) == 0)
        def _():
            loss_ref[...] = jnp.zeros_like(loss_ref)
            dg_ref[...] = jnp.zeros_like(dg_ref)

        part = 0.5 * jnp.sum(jnp.sum(err * err, axis=-1, keepdims=True) * (1.0 / D), axis=0, keepdims=True)
        loss_ref[...] += jnp.broadcast_to(part, loss_ref.shape)
        dg_ref[...] += jnp.sum(dgt, axis=0, keepdims=True)

    row = pl.BlockSpec((ts, D), lambda i: (i, 0))
    vec = pl.BlockSpec((1, D), lambda i: (0, 0))
    return _pcall(
        body, name="loss_head", grid=(S // ts,),
        in_specs=[row, vec, row],
        out_specs=(pl.BlockSpec((1, 128), lambda i: (0, 0)), row, row, vec),
        out_shape=(jax.ShapeDtypeStruct((1, 128), F32), jax.ShapeDtypeStruct((S, D), F32),
                   jax.ShapeDtypeStruct((S, D), BF16), jax.ShapeDtypeStruct((1, D), F32)),
        compiler_params=_params(("arbitrary",)),
    )(x, g, target)


_DIMS = {"nn": (((1,), (0,)), ((), ())), "nt": (((1,), (1,)), ((), ())), "tn": (((0,), (0,)), ((), ()))}


def _mm(name, mode, a, b, tiles, grid, a_map, b_map, o_map, out_shape, out_dtype=F32, res=None, ride=None):
    tm, tn, tk = tiles
    nk = grid[-1]
    has_res = res is not None
    dn = _DIMS[mode]

    def body(ins, outs, scratch):
        o_ref = outs[0]

        def product():
            return lax.dot_general(ins[0][...].astype(BF16), ins[1][...].astype(BF16), dn, preferred_element_type=F32)

        def finish(out):
            if has_res:
                out = ins[2][...] + out
            o_ref[...] = out.astype(o_ref.dtype)

        if nk == 1:
            finish(product())
            return
        acc_ref = scratch[0]
        k = pl.program_id(2)

        @pl.when(k == 0)
        def _():
            acc_ref[...] = jnp.zeros_like(acc_ref)

        acc_ref[...] += product()

        @pl.when(k == nk - 1)
        def _():
            finish(acc_ref[...])

    a_spec = pl.BlockSpec((tk, tm) if mode == "tn" else (tm, tk), a_map)
    b_spec = pl.BlockSpec((tn, tk) if mode == "nt" else (tk, tn), b_map)
    o_spec = pl.BlockSpec((tm, tn), o_map)
    operands, in_specs = [a, b], [a_spec, b_spec]
    if has_res:
        operands.append(res)
        in_specs.append(o_spec)
    outs = _pallas(body, name, grid, operands, in_specs, [o_spec], [jax.ShapeDtypeStruct(out_shape, out_dtype)],
                   scratch=[pltpu.VMEM((tm, tn), F32)] if nk > 1 else [],
                   sem=("parallel", "parallel", "arbitrary"), ride=ride)
    return (outs[0], outs[1:]) if ride else outs[0]


def _mm_plain(name, mode, a, b, out_dtype=F32, res=None):
    if mode == "tn":
        (K, M), N = a.shape, b.shape[1]
    elif mode == "nt":
        (M, K), N = a.shape, b.shape[0]
    else:
        (M, K), N = a.shape, b.shape[1]
    if mode == "tn":
        tm, tn, tk = _tile(M, 2 * MM_TN), _tile(N, 2 * MM_TN), _tile(K, ROW_TILE)
    else:
        tm = _tile(M, ROW_TILE)
        tk = K if K <= 4 * MM_TK else _tile(K, MM_TK)
        tn = N if N * tk * 2 <= (8 << 20) else _tile(N, MM_TN)
    a_map = (lambda i, j, k: (k, i)) if mode == "tn" else (lambda i, j, k: (i, k))
    b_map = (lambda i, j, k: (j, k)) if mode == "nt" else (lambda i, j, k: (k, j))
    return _mm(name, mode, a, b, (tm, tn, tk), (M // tm, N // tn, K // tk), a_map, b_map, lambda i, j, k: (i, j),
               (M, N), out_dtype, res)


def _small_slots_out(name, a, w_slots, out_dtype, epilogue=None):
    S, K = a.shape
    n = w_slots.shape[1]
    tm = _tile(S, ROW_TILE)
    extra = epilogue[1] if epilogue else []

    def body(*refs):
        a_ref, w_ref, o_ref = refs[0], refs[1], refs[-1]
        av = a_ref[...].astype(BF16)
        tables = [r[...] for r in refs[2:-1]]
        for j in range(N_DEV):
            out = jnp.dot(av, w_ref[j], preferred_element_type=F32)
            if epilogue:
                out = epilogue[0](out, *tables)
            o_ref[j] = out.astype(o_ref.dtype)

    return _pcall(
        body, name=name, grid=(S // tm,),
        in_specs=[pl.BlockSpec((tm, K), lambda i: (i, 0)), pl.BlockSpec((N_DEV, K, n), lambda i: (0, 0, 0))]
        + (list(epilogue[2]) if epilogue else []),
        out_specs=pl.BlockSpec((N_DEV, tm, n), lambda i: (0, i, 0)),
        out_shape=jax.ShapeDtypeStruct((N_DEV, S, n), out_dtype),
        compiler_params=_params(("parallel",)),
    )(a, w_slots.reshape(N_DEV, K, n), *extra).reshape(N_DEV * S, n)


def _small_slots_in_nt(name, a_slots, w_slots, S):
    n = a_slots.shape[1]
    K = w_slots.shape[0] // N_DEV
    tm = _tile(S, ROW_TILE)

    def body(a_ref, w_ref, o_ref):
        total = lax.dot_general(a_ref[0], w_ref[0], _DIMS["nt"], preferred_element_type=F32)
        for j in range(1, N_DEV):
            total += lax.dot_general(a_ref[j], w_ref[j], _DIMS["nt"], preferred_element_type=F32)
        o_ref[...] = total

    return _pcall(
        body, name=name, grid=(S // tm,),
        in_specs=[pl.BlockSpec((N_DEV, tm, n), lambda i: (0, i, 0)), pl.BlockSpec((N_DEV, K, n), lambda i: (0, 0, 0))],
        out_specs=pl.BlockSpec((tm, K), lambda i: (i, 0)),
        out_shape=jax.ShapeDtypeStruct((S, K), F32),
        compiler_params=_params(("parallel",)),
    )(a_slots.reshape(N_DEV, S, n), w_slots.reshape(N_DEV, K, n))


def _small_slots_dw(name, a, d_slots, out_dtype):
    S, K = a.shape
    n = d_slots.shape[1]
    tk = _tile(S, ROW_TILE)
    nkb = S // tk

    def body(a_ref, d_ref, o_ref, acc_ref):
        k = pl.program_id(0)
        at = a_ref[...].astype(F32).T.astype(BF16)

        def add(first):
            for j in range(N_DEV):
                part = jnp.dot(at, d_ref[j], preferred_element_type=F32)
                if first:
                    acc_ref[j] = part
                else:
                    acc_ref[j] += part

        @pl.when(k == 0)
        def _():
            add(True)

        @pl.when(k > 0)
        def _():
            add(False)

        @pl.when(k == nkb - 1)
        def _():
            o_ref[...] = acc_ref[...].astype(o_ref.dtype)

    return _pcall(
        body, name=name, grid=(nkb,),
        in_specs=[pl.BlockSpec((tk, K), lambda k: (k, 0)), pl.BlockSpec((N_DEV, tk, n), lambda k: (0, k, 0))],
        out_specs=pl.BlockSpec((N_DEV, K, n), lambda k: (0, 0, 0)),
        out_shape=jax.ShapeDtypeStruct((N_DEV, K, n), out_dtype),
        scratch_shapes=[pltpu.VMEM((N_DEV, K, n), F32)],
        compiler_params=_params(("arbitrary",)),
    )(a, d_slots.reshape(N_DEV, S, n)).reshape(N_DEV * K, n)


def _sigmoid(x):
    return 1.0 / (1.0 + jnp.exp(-x))


def _ffn_up(h, wg, wu, name, ride=None):
    S, D = h.shape
    fs = wg.shape[1]
    tm = _tile(S, ROW_TILE)
    sp = FFN_SLOTS

    def body(ins, outs, scratch):
        h_ref, wg_ref, wu_ref = ins
        g_ref, u_ref, a_ref = outs
        hv = h_ref[...]
        for s in range(sp):
            g = jnp.dot(hv, wg_ref[s], preferred_element_type=F32)
            u = jnp.dot(hv, wu_ref[s], preferred_element_type=F32)
            g_ref[s] = g.astype(g_ref.dtype)
            u_ref[s] = u.astype(u_ref.dtype)
            a_ref[s] = (g * _sigmoid(g) * u).astype(a_ref.dtype)

    w_spec = pl.BlockSpec((sp, D, fs), lambda i, j: (j, 0, 0))
    o_spec = pl.BlockSpec((sp, tm, fs), lambda i, j: (j, i, 0))
    sds = jax.ShapeDtypeStruct((N_DEV, S, fs), BF16)
    g, u, a, *rest = _pallas(body, name, (S // tm, N_DEV // sp), [h, wg.reshape(N_DEV, D, fs), wu.reshape(N_DEV, D, fs)],
                             [pl.BlockSpec((tm, D), lambda i, j: (i, 0)), w_spec, w_spec], [o_spec, o_spec, o_spec],
                             [sds, sds, sds], sem=("parallel", "arbitrary"), ride=ride)
    return [t.reshape(N_DEV * S, fs) for t in (g, u, a)] + rest


def _ffn_dact(dy, wd, g, u, name, ride=None):
    S, D = dy.shape
    fs = g.shape[1]
    tm = _tile(S, ROW_TILE)
    sp = FFN_SLOTS

    def body(ins, outs, scratch):
        dy_ref, wd_ref, g_ref, u_ref = ins
        dg_ref, du_ref = outs
        dy = dy_ref[...]
        for s in range(sp):
            da = lax.dot_general(dy, wd_ref[s], _DIMS["nt"], preferred_element_type=F32)
            g, u = g_ref[s].astype(F32), u_ref[s].astype(F32)
            sig = _sigmoid(g)
            dg_ref[s] = (da * u * (sig * (1.0 + g * (1.0 - sig)))).astype(dg_ref.dtype)
            du_ref[s] = (da * (g * sig)).astype(du_ref.dtype)

    o_spec = pl.BlockSpec((sp, tm, fs), lambda j, i: (j, i, 0))
    sds = jax.ShapeDtypeStruct((N_DEV, S, fs), BF16)
    dg, du, *rest = _pallas(body, name, (N_DEV // sp, S // tm),
                            [dy, wd.reshape(N_DEV, fs, D), g.reshape(N_DEV, S, fs), u.reshape(N_DEV, S, fs)],
                            [pl.BlockSpec((tm, D), lambda j, i: (i, 0)), pl.BlockSpec((sp, fs, D), lambda j, i: (j, 0, 0)),
                             o_spec, o_spec], [o_spec, o_spec], [sds, sds], sem=("parallel", "arbitrary"), ride=ride)
    return [dg.reshape(N_DEV * S, fs), du.reshape(N_DEV * S, fs)] + rest


def _ffn_down(a, wd, x, name):
    S, D = x.shape
    fs = a.shape[1]
    tm = _tile(S, ROW_TILE)
    sp = FFN_SLOTS
    nj = N_DEV // sp

    def body(a_ref, w_ref, x_ref, o_ref, acc_ref):
        j = pl.program_id(1)
        part = jnp.dot(a_ref[0], w_ref[0], preferred_element_type=F32)
        for s in range(1, sp):
            part += jnp.dot(a_ref[s], w_ref[s], preferred_element_type=F32)

        @pl.when(j == 0)
        def _():
            acc_ref[...] = x_ref[...] + part

        @pl.when(j > 0)
        def _():
            acc_ref[...] += part

        @pl.when(j == nj - 1)
        def _():
            o_ref[...] = acc_ref[...]

    row = pl.BlockSpec((tm, D), lambda i, j: (i, 0))
    return _pcall(
        body, name=name, grid=(S // tm, nj),
        in_specs=[pl.BlockSpec((sp, tm, fs), lambda i, j: (j, i, 0)), pl.BlockSpec((sp, fs, D), lambda i, j: (j, 0, 0)), row],
        out_specs=row, out_shape=jax.ShapeDtypeStruct((S, D), F32),
        scratch_shapes=[pltpu.VMEM((tm, D), F32)],
        compiler_params=_params(("parallel", "arbitrary")),
    )(a.reshape(N_DEV, S, fs), wd.reshape(N_DEV, fs, D), x)


def _ffn_dh(dg, du, wg, wu, S, name, ride=None):
    fs = dg.shape[1]
    D = wg.shape[0] // N_DEV
    tm = _tile(S, ROW_TILE)
    sp = FFN_SLOTS
    nj = N_DEV // sp

    def body(ins, outs, scratch):
        dg_ref, du_ref, wg_ref, wu_ref = ins
        o_ref, acc_ref = outs[0], scratch[0]
        j = pl.program_id(1)
        part = None
        for s in range(sp):
            for d_ref, w_ref in ((dg_ref, wg_ref), (du_ref, wu_ref)):
                term = lax.dot_general(d_ref[s], w_ref[s], _DIMS["nt"], preferred_element_type=F32)
                part = term if part is None else part + term

        @pl.when(j == 0)
        def _():
            acc_ref[...] = part

        @pl.when(j > 0)
        def _():
            acc_ref[...] += part

        @pl.when(j == nj - 1)
        def _():
            o_ref[...] = acc_ref[...]

    d_spec = pl.BlockSpec((sp, tm, fs), lambda i, j: (j, i, 0))
    w_spec = pl.BlockSpec((sp, D, fs), lambda i, j: (j, 0, 0))
    return _pallas(body, name, (S // tm, nj),
                   [dg.reshape(N_DEV, S, fs), du.reshape(N_DEV, S, fs), wg.reshape(N_DEV, D, fs), wu.reshape(N_DEV, D, fs)],
                   [d_spec, d_spec, w_spec, w_spec], [pl.BlockSpec((tm, D), lambda i, j: (i, 0))],
                   [jax.ShapeDtypeStruct((S, D), F32)], scratch=[pltpu.VMEM((tm, D), F32)],
                   sem=("parallel", "arbitrary"), ride=ride)


def _slots_dw_t(name, at, d_slots, ride=None):
    K, S = at.shape
    n = d_slots.shape[1]
    tk = _tile(S, 2 * ROW_TILE)
    nkb = S // tk
    return _mm(name, "nn", at, d_slots, (K, n, tk), (N_DEV, 1, nkb),
               lambda j, q, k: (0, k), lambda j, q, k: (j * nkb + k, 0), lambda j, q, k: (j, 0),
               (N_DEV * K, n), BF16, ride=ride)


def _rope_heads(q, cos_t, sin_t):
    parts = []
    for c0 in range(0, q.shape[1], HEAD_PAD):
        parts += [q[:, c0:c0 + QK_NOPE_DIM], _rope(q[:, c0 + QK_NOPE_DIM:c0 + HEAD_PAD], cos_t, sin_t)]
    return jnp.concatenate(parts, axis=1)


def _kv_mid(kv, g, cos_t, sin_t):
    S, W = kv.shape
    R = W - 128
    ts = _tile(S, ROW_TILE)

    def body(kv_ref, g_ref, c_ref, s_ref, c_out, k_out):
        cf = kv_ref[:, :R]
        c_out[...] = (cf * _rms_scale(cf) * g_ref[...]).astype(c_out.dtype)
        k_out[...] = _rope(kv_ref[:, R:], c_ref[...], s_ref[...]).astype(k_out.dtype)

    tab = pl.BlockSpec((ts, 128), lambda i: (i, 0))
    return _pcall(
        body, name="kv_mid", grid=(S // ts,),
        in_specs=[pl.BlockSpec((ts, W), lambda i: (i, 0)), pl.BlockSpec((1, R), lambda i: (0, 0)), tab, tab],
        out_specs=(pl.BlockSpec((ts, R), lambda i: (i, 0)), tab),
        out_shape=(jax.ShapeDtypeStruct((S, R), BF16), jax.ShapeDtypeStruct((S, 128), BF16)),
        compiler_params=_params(("parallel",)),
    )(kv, g, cos_t, sin_t)


def _kv_mid_bwd(kv, g, dc, dkpe, cos_t, sin_t):
    S, W = kv.shape
    R = W - 128
    H = dkpe.shape[0]
    ts = _row_tile(S, H * 128)

    def body(kv_ref, g_ref, dc_ref, dk_ref, c_ref, s_ref, dkv_ref, dg_ref):
        dx, dgt = _rms_bwd(kv_ref[:, :R], g_ref[...], dc_ref[...])
        dkv_ref[:, :R] = dx.astype(dkv_ref.dtype)
        dk = dk_ref[0]
        for h in range(1, H):
            dk = dk + dk_ref[h]
        dkv_ref[:, R:] = _rope(dk, c_ref[...], -s_ref[...]).astype(dkv_ref.dtype)

        @pl.when(pl.program_id(0) == 0)
        def _():
            dg_ref[...] = jnp.zeros_like(dg_ref)

        dg_ref[...] += jnp.sum(dgt, axis=0, keepdims=True)

    tab = pl.BlockSpec((ts, 128), lambda i: (i, 0))
    vec = pl.BlockSpec((1, R), lambda i: (0, 0))
    return _pcall(
        body, name="kv_mid_bwd", grid=(S // ts,),
        in_specs=[pl.BlockSpec((ts, W), lambda i: (i, 0)), vec, pl.BlockSpec((ts, R), lambda i: (i, 0)),
                  pl.BlockSpec((H, ts, 128), lambda i: (0, i, 0)), tab, tab],
        out_specs=(pl.BlockSpec((ts, W), lambda i: (i, 0)), vec),
        out_shape=(jax.ShapeDtypeStruct((S, W), BF16), jax.ShapeDtypeStruct((1, R), F32)),
        compiler_params=_params(("arbitrary",)),
    )(kv, g, dc, dkpe, cos_t, sin_t)


def _block_pairs(nb, by_key):
    pairs = ([(qi, ki) for ki in range(nb) for qi in range(ki, nb)] if by_key
             else [(qi, ki) for qi in range(nb) for ki in range(qi + 1)])
    return jnp.asarray(np.array([p[0] for p in pairs], np.int32)), jnp.asarray(np.array([p[1] for p in pairs], np.int32))


def _causal_mask(blk, transposed=False, rows=None, row0=0):
    shape = (blk if rows is None else rows, blk)
    r = lax.broadcasted_iota(jnp.int32, shape, 0) + row0
    c = lax.broadcasted_iota(jnp.int32, shape, 1)
    return (r <= c) if transposed else (c <= r)


def _attn_specs(S, blk, hpd):
    nb = S // blk
    w = hpd * HEAD_PAD
    return dict(
        q=pl.BlockSpec((blk, w), lambda j, t, qt, kt: (j * nb + qt[t], 0)),
        kv=pl.BlockSpec((blk, w), lambda j, t, qt, kt: (j * nb + kt[t], 0)),
        kp=pl.BlockSpec((blk, 128), lambda j, t, qt, kt: (kt[t], 0)),
        do=pl.BlockSpec((blk, hpd * V_DIM), lambda j, t, qt, kt: (qt[t], j)),
        col=pl.BlockSpec((hpd, blk, 1), lambda j, t, qt, kt: (j, qt[t], 0)),
        row=pl.BlockSpec((hpd, 1, blk), lambda j, t, qt, kt: (j, 0, qt[t])))


def _head_k(kv_ref, kp, hh):
    return jnp.concatenate([kv_ref[:, hh * HEAD_PAD:hh * HEAD_PAD + QK_NOPE_DIM], kp], axis=1)


def _head_v(kv_ref, hh):
    return kv_ref[:, hh * HEAD_PAD + QK_NOPE_DIM:(hh + 1) * HEAD_PAD]


def _attn_fwd(q, kvu, kpe, H, ride=None):
    S = kpe.shape[0]
    hpd = H // N_DEV
    blk = _tile(S, ATT_BLOCK)
    q_tab, k_tab = _block_pairs(S // blk, by_key=False)
    rc = min(blk, ATT_ROW_CHUNK)

    def body(ins, outs, scratch):
        qt, kt, q_ref, kv_ref, kp_ref = ins
        o_ref, lse_ref = outs
        m_ref, acc_ref = scratch
        t = pl.program_id(1)
        qi, ki = qt[t], kt[t]

        @pl.when(ki == 0)
        def _():
            m_ref[...] = jnp.full_like(m_ref, -jnp.inf)
            acc_ref[...] = jnp.zeros_like(acc_ref)

        def step(masked):
            kp = kp_ref[...]
            ones = jnp.ones((blk, 128), BF16)
            ks = [_head_k(kv_ref, kp, hh) for hh in range(hpd)]
            v_exts = [jnp.concatenate([_head_v(kv_ref, hh), ones], axis=1) for hh in range(hpd)]
            chains = [(hh, r0) for r0 in range(0, blk, rc) for hh in range(hpd)]

            def scores(hh, r0):
                return lax.dot_general(q_ref[r0:r0 + rc, hh * HEAD_PAD:(hh + 1) * HEAD_PAD], ks[hh], _DIMS["nt"],
                                       preferred_element_type=F32)

            ahead = [scores(*ch) for ch in chains[:ATT_LOOKAHEAD]]
            for n, (hh, r0) in enumerate(chains):
                rows = slice(r0, r0 + rc)
                s = ahead.pop(0)
                if n + ATT_LOOKAHEAD < len(chains):
                    ahead.append(scores(*chains[n + ATT_LOOKAHEAD]))
                if masked:
                    s = jnp.where(_causal_mask(blk, rows=rc, row0=r0), s, -jnp.inf)
                m_old = m_ref[hh, rows]
                m_new = jnp.maximum(m_old, jnp.max(s, axis=-1, keepdims=True))
                p = jnp.exp2((s - m_new) * EXP2_SCALE).astype(BF16)
                acc_ref[hh, rows] = (jnp.exp2((m_old - m_new) * EXP2_SCALE) * acc_ref[hh, rows]
                                     + jnp.dot(p, v_exts[hh], preferred_element_type=F32))
                m_ref[hh, rows] = m_new

        @pl.when(ki < qi)
        def _():
            step(False)

        @pl.when(ki == qi)
        def _():
            step(True)
            for hh in range(hpd):
                l = acc_ref[hh, :, V_DIM:]
                o_ref[:, hh * V_DIM:(hh + 1) * V_DIM] = (acc_ref[hh, :, :V_DIM] / l).astype(o_ref.dtype)
                lse_ref[hh] = m_ref[hh] * EXP2_SCALE + jnp.log2(l[:, :1])

    sp = _attn_specs(S, blk, hpd)
    return _pallas(body, "attn_fwd", (N_DEV, q_tab.shape[0]), [q, kvu, kpe], [sp["q"], sp["kv"], sp["kp"]],
                   [sp["do"], sp["col"]],
                   [jax.ShapeDtypeStruct((S, H * V_DIM), BF16), jax.ShapeDtypeStruct((H, S, 1), F32)],
                   scratch=[pltpu.VMEM((hpd, blk, 1), F32), pltpu.VMEM((hpd, blk, 2 * V_DIM), F32)],
                   sem=("parallel", "arbitrary"), ride=ride, prefetch=[q_tab, k_tab])


def _attn_bwd(q, kvu, kpe, do, o, lse_row, cos_t, sin_t, H):
    S = kpe.shape[0]
    hpd = H // N_DEV
    blk = _tile(S, ATT_BLOCK)
    nb = S // blk
    q_tab, k_tab = _block_pairs(nb, by_key=True)

    def body(qt, kt, q_ref, kv_ref, kp_ref, do_ref, o_ref, lse_ref, c_ref, s_ref, dq_ref, dkv_ref, dkp_ref,
             dq_acc, dk_acc, dv_acc):
        t = pl.program_id(1)
        qi, ki = qt[t], kt[t]
        q_rows = pl.ds(pl.multiple_of(qi * blk, blk), blk)

        def step(diag):
            kp = kp_ref[...]
            ones = jnp.ones((8, V_DIM), BF16)
            for hh in range(hpd):
                k = _head_k(kv_ref, kp, hh)
                qv = q_ref[:, hh * HEAD_PAD:(hh + 1) * HEAD_PAD]
                cols = slice(hh * V_DIM, (hh + 1) * V_DIM)
                dov = do_ref[:, cols]
                doo = (dov.astype(F32) * o_ref[:, cols].astype(F32)).astype(BF16)
                delta = lax.dot_general(ones, doo, _DIMS["nt"], preferred_element_type=F32)[:1]
                st = lax.dot_general(k, qv, _DIMS["nt"], preferred_element_type=F32)
                pt = jnp.exp2(st * EXP2_SCALE - lse_ref[hh])
                if diag:
                    pt = jnp.where(_causal_mask(blk, transposed=True), pt, 0.0)
                dv_new = jnp.dot(pt.astype(BF16), dov, preferred_element_type=F32)
                dpt = lax.dot_general(_head_v(kv_ref, hh), dov, _DIMS["nt"], preferred_element_type=F32)
                dst = (pt * (dpt - delta)).astype(BF16)
                dk_new = jnp.dot(dst, qv, preferred_element_type=F32)
                dq_new = lax.dot_general(dst, k, _DIMS["tn"], preferred_element_type=F32)
                if diag:
                    dv_acc[hh] = dv_new
                    dk_acc[hh] = dk_new
                else:
                    dv_acc[hh] += dv_new
                    dk_acc[hh] += dk_new
                dq_acc[hh, q_rows] += dq_new

        @pl.when(t == 0)
        def _():
            dq_acc[...] = jnp.zeros_like(dq_acc)

        @pl.when(qi == ki)
        def _():
            step(True)
            for hh in range(hpd):
                c0 = hh * HEAD_PAD
                dq = dq_acc[hh, q_rows] * ATTN_SCALE
                dq_ref[:, c0:c0 + QK_NOPE_DIM] = dq[:, :QK_NOPE_DIM].astype(dq_ref.dtype)
                dq_ref[:, c0 + QK_NOPE_DIM:c0 + HEAD_PAD] = _rope(dq[:, QK_NOPE_DIM:], c_ref[...],
                                                                  -s_ref[...]).astype(dq_ref.dtype)

        @pl.when(qi > ki)
        def _():
            step(False)

        @pl.when(qi == nb - 1)
        def _():
            for hh in range(hpd):
                c0 = hh * HEAD_PAD
                dkv_ref[:, c0:c0 + QK_NOPE_DIM] = (dk_acc[hh, :, :QK_NOPE_DIM] * ATTN_SCALE).astype(dkv_ref.dtype)
                dkv_ref[:, c0 + QK_NOPE_DIM:c0 + HEAD_PAD] = dv_acc[hh].astype(dkv_ref.dtype)
                dkp_ref[hh] = dk_acc[hh, :, QK_NOPE_DIM:] * ATTN_SCALE

    sp = _attn_specs(S, blk, hpd)
    key_tab = pl.BlockSpec((blk, 128), lambda j, t, qt, kt: (kt[t], 0))
    grid_spec = pltpu.PrefetchScalarGridSpec(
        num_scalar_prefetch=2, grid=(N_DEV, q_tab.shape[0]),
        in_specs=[sp["q"], sp["kv"], sp["kp"], sp["do"], sp["do"], sp["row"], key_tab, key_tab],
        out_specs=(sp["kv"], sp["kv"], pl.BlockSpec((hpd, blk, 128), lambda j, t, qt, kt: (j, kt[t], 0))),
        scratch_shapes=[pltpu.VMEM((hpd, S, HEAD_PAD), F32), pltpu.VMEM((hpd, blk, HEAD_PAD), F32),
                        pltpu.VMEM((hpd, blk, V_DIM), F32)])
    return _pcall(
        body, name="attn_bwd", grid_spec=grid_spec,
        out_shape=(jax.ShapeDtypeStruct(q.shape, BF16), jax.ShapeDtypeStruct(kvu.shape, BF16),
                   jax.ShapeDtypeStruct((H, S, 128), F32)),
        compiler_params=_params(("parallel", "arbitrary")),
    )(q_tab, k_tab, q, kvu, kpe, do, o, lse_row, cos_t, sin_t)


def _mesh_pos():
    return lax.axis_index("x"), lax.axis_index("y"), lax.axis_index("c")


def _flip(pos, k):
    x, y, c = pos
    return (1 - x if k & 4 else x, 1 - y if k & 2 else y, 1 - c if k & 1 else c)


def _rank(pos):
    return 4 * pos[0] + 2 * pos[1] + pos[2]


def _copies(n, src_of, dst_refs, local_src_of, send_sems, recv_sems, local_sems):
    me = _mesh_pos()
    local = [pltpu.make_async_copy(local_src_of(a), dst_refs[a].at[_rank(me)], local_sems.at[a]) for a in range(n)]
    sends, arrivals = [], []
    for k in range(1, N_DEV):
        peer = _flip(me, k)
        for a in range(n):
            idx = a * (N_DEV - 1) + k - 1
            for into, group in ((me, sends), (peer, arrivals)):
                group.append(pltpu.make_async_remote_copy(
                    src_ref=src_of(a, peer), dst_ref=dst_refs[a].at[_rank(into)],
                    send_sem=send_sems.at[idx], recv_sem=recv_sems.at[idx],
                    device_id=peer, device_id_type=pl.DeviceIdType.MESH))
    return local, sends, arrivals


def _exchange_start(*args):
    local, sends, _ = _copies(*args)
    for cp in local + sends:
        cp.start()


def _exchange_wait(*args):
    local, sends, arrivals = _copies(*args)
    for cp in arrivals:
        cp.wait_recv()
    for cp in sends:
        cp.wait_send()
    for cp in local:
        cp.wait()


def _exchange(*args):
    _exchange_start(*args)
    _exchange_wait(*args)


def _sems(n):
    return [pltpu.SemaphoreType.DMA((n * (N_DEV - 1),)), pltpu.SemaphoreType.DMA((n * (N_DEV - 1),)),
            pltpu.SemaphoreType.DMA((n,))]


_ANY = pl.BlockSpec(memory_space=pl.ANY)


class _Ride:
    def __init__(self, kind, arrays):
        self.kind, self.arrays, self.n = kind, list(arrays), len(arrays)

    def out_shape(self):
        lead = () if self.kind == "scatter" else (N_DEV,)
        return [jax.ShapeDtypeStruct(lead + a.shape, a.dtype) for a in self.arrays]

    def _args(self, x_refs, out_refs, sems):
        if self.kind == "gather":
            return (self.n, lambda a, peer: x_refs[a], out_refs, lambda a: x_refs[a]) + tuple(sems)
        me = _mesh_pos()
        return (self.n, lambda a, peer: x_refs[a].at[_rank(peer)], out_refs, lambda a: x_refs[a].at[_rank(me)]) + tuple(sems)

    def start(self, x_refs, out_refs, sems):
        if self.kind == "gather2":
            _two_level_gather(x_refs, out_refs, *sems)[0]()
        else:
            _exchange_start(*self._args(x_refs, out_refs, sems))

    def wait(self, x_refs, out_refs, sems):
        if self.kind == "gather2":
            _two_level_gather(x_refs, out_refs, *sems)[1]()
        else:
            _exchange_wait(*self._args(x_refs, out_refs, sems))


def _pallas(body, name, grid, operands, in_specs, out_specs, out_shape, scratch=(), sem=None, ride=None, prefetch=()):
    n_pf, n_in, n_out, n_scr = len(prefetch), len(in_specs), len(out_specs), len(scratch)
    nr = ride.n if ride else 0

    def wrapped(*refs):
        refs = list(refs)
        pf, refs = refs[:n_pf], refs[n_pf:]
        ins, r_in = refs[:n_in], refs[n_in:n_in + nr]
        outs, r_out = refs[n_in + nr:n_in + nr + n_out], refs[n_in + nr + n_out:n_in + 2 * nr + n_out]
        rest = refs[n_in + 2 * nr + n_out:]
        scr, sems = rest[:n_scr], rest[n_scr:]
        if ride:
            first, last = None, None
            for d, size in enumerate(grid):
                i = pl.program_id(d)
                first = (i == 0) if first is None else jnp.logical_and(first, i == 0)
                last = (i == size - 1) if last is None else jnp.logical_and(last, i == size - 1)

            @pl.when(first)
            def _():
                ride.start(r_in, r_out, sems)

        body(pf + ins, outs, scr)
        if ride:
            @pl.when(last)
            def _():
                ride.wait(r_in, r_out, sems)

    grid_spec = pltpu.PrefetchScalarGridSpec(
        num_scalar_prefetch=n_pf, grid=grid,
        in_specs=list(in_specs) + [_ANY] * nr, out_specs=tuple(list(out_specs) + [_ANY] * nr),
        scratch_shapes=list(scratch) + (_sems(nr) if ride else []))
    sem = tuple(sem) if sem is not None and not ride else ("arbitrary",) * len(grid)
    return list(_pcall(
        wrapped, name=name, grid_spec=grid_spec,
        out_shape=tuple(list(out_shape) + (ride.out_shape() if ride else [])),
        compiler_params=_params(sem),
    )(*(list(prefetch) + list(operands) + (ride.arrays if ride else []))))


def _two_level_gather(x_refs, out_refs, send_sems, recv_sems, local_sems):
    n = len(x_refs)
    x, y, c = _mesh_pos()
    me, sibling = (x, y, c), (x, y, 1 - c)
    chips = [(1 - x, y), (x, 1 - y), (1 - x, 1 - y)]

    def copy(a, k, block, to, src=None):
        rows = out_refs[a].at[_rank(block)]
        return pltpu.make_async_remote_copy(
            src_ref=rows if src is None else src, dst_ref=rows,
            send_sem=send_sems.at[a * (N_DEV - 1) + k], recv_sem=recv_sems.at[a * (N_DEV - 1) + k],
            device_id=to, device_id_type=pl.DeviceIdType.MESH)

    def own_copies():
        mine = [pltpu.make_async_copy(x_refs[a], out_refs[a].at[_rank(me)], local_sems.at[a]) for a in range(n)]
        first = []
        for a in range(n):
            first.append(copy(a, 0, me, sibling, src=x_refs[a]))
            first += [copy(a, 1 + j, me, (*chip, c), src=x_refs[a]) for j, chip in enumerate(chips)]
        return mine, first

    def start():
        mine, first = own_copies()
        for cp in mine + first:
            cp.start()

    def finish():
        mine, first = own_copies()
        passed = []
        for j, chip in enumerate(chips):
            for a in range(n):
                copy(a, 1 + j, (*chip, c), me).wait_recv()
                passed.append(copy(a, 4 + j, (*chip, c), sibling))
                passed[-1].start()
        for a in range(n):
            copy(a, 0, sibling, me).wait_recv()
            for j, chip in enumerate(chips):
                copy(a, 4 + j, (*chip, 1 - c), me).wait_recv()
        for cp in first + passed:
            cp.wait_send()
        for cp in mine:
            cp.wait()

    return start, finish


def _all_gather(shards, name):
    n = len(shards)

    def body(*refs):
        start, finish = _two_level_gather(refs[:n], refs[n:2 * n], *refs[2 * n:])
        start()
        finish()

    return _pcall(
        body, name=name, in_specs=[_ANY] * n, out_specs=tuple([_ANY] * n),
        out_shape=tuple(jax.ShapeDtypeStruct((N_DEV,) + s.shape, s.dtype) for s in shards),
        scratch_shapes=_sems(n),
    )(*shards)


def _all_reduce_small(v):
    R, C = v.shape

    def body(x_ref, out_ref, buf_ref, send_sems, recv_sems, local_sems):
        _exchange(1, lambda a, peer: x_ref, [buf_ref], lambda a: x_ref, send_sems, recv_sems, local_sems)
        total = buf_ref[0]
        for j in range(1, N_DEV):
            total = total + buf_ref[j]
        out_ref[...] = total

    vm = pl.BlockSpec(memory_space=pltpu.VMEM)
    return _pcall(
        body, name="all_reduce_small", in_specs=[vm], out_specs=vm,
        out_shape=jax.ShapeDtypeStruct((R, C), F32),
        scratch_shapes=[pltpu.VMEM((N_DEV, R, C), F32)] + _sems(1),
    )(v)


def _sum_slots(layers, name):
    L = len(layers)
    shape = layers[0].shape[1:]
    C = shape[-1]
    R = layers[0].size // (N_DEV * C)
    tr = R
    while N_DEV * tr * C * 4 > (4 << 20) and tr % 32 == 0:
        tr //= 2

    def body(*refs):
        o_ref = refs[L]
        for li in range(L):
            @pl.when(pl.program_id(0) == li)
            def _():
                total = refs[li][0].astype(F32)
                for j in range(1, N_DEV):
                    total = total + refs[li][j].astype(F32)
                o_ref[0] = total

    in_specs = [pl.BlockSpec((N_DEV, tr, C), lambda l, i, li=li: (0, jnp.where(l == li, i, 0), 0)) for li in range(L)]
    return _pcall(
        body, name=name, grid=(L, R // tr),
        in_specs=in_specs,
        out_specs=pl.BlockSpec((1, tr, C), lambda l, i: (l, i, 0)),
        out_shape=jax.ShapeDtypeStruct((L, R, C), F32),
        compiler_params=_params(("parallel", "parallel")),
    )(*[p.reshape(N_DEV, R, C) for p in layers]).reshape((L,) + shape)


def _adamw_update(w, g, m, v):
    nm = ADAM_B1 * m + (1.0 - ADAM_B1) * g
    nv = ADAM_B2 * v + (1.0 - ADAM_B2) * (g * g)
    m_hat = nm / (1.0 - ADAM_B1 ** ADAM_STEP)
    v_hat = nv / (1.0 - ADAM_B2 ** ADAM_STEP)
    return -ADAM_LR * (m_hat / (jnp.sqrt(v_hat) + ADAM_EPS) + ADAM_WD * w), nm, nv


def _adamw_from_slots(w, layers, m, v, name, ride=None):
    L = len(layers)
    shape = w.shape
    C = shape[-1]
    R = w.size // (L * C)
    fits = [t for t in range(16, R + 1, 16) if R % t == 0 and N_DEV * t * C * 2 <= (1 << 20)]
    tr = max(fits) if fits else R

    def body(ins, outs, scratch):
        w_ref, m_ref, v_ref = ins[L:]
        g_ref, d_ref, nm_ref, nv_ref = outs
        for li in range(L):
            @pl.when(pl.program_id(0) == li)
            def _():
                g = ins[li][0].astype(F32)
                for j in range(1, N_DEV):
                    g = g + ins[li][j].astype(F32)
                g_ref[0] = g
                d_ref[0], nm_ref[0], nv_ref[0] = _adamw_update(w_ref[0], g, m_ref[0], v_ref[0])

    slot_specs = [pl.BlockSpec((N_DEV, tr, C), lambda l, i, li=li: (0, jnp.where(l == li, i, 0), 0)) for li in range(L)]
    blk = pl.BlockSpec((1, tr, C), lambda l, i: (l, i, 0))
    sds = jax.ShapeDtypeStruct((L, R, C), F32)
    outs = _pallas(body, name, (L, R // tr),
                   [p.reshape(N_DEV, R, C) for p in layers] + [t.reshape(L, R, C) for t in (w, m, v)],
                   slot_specs + [blk] * 3, [blk] * 4, [sds] * 4, sem=("parallel", "parallel"), ride=ride)
    return [o.reshape(shape) for o in outs[:4]] + outs[4:]


def _adamw(w, g, m, v, name):
    shape = w.shape
    C = shape[-1]
    R = w.size // C
    tr = R
    while tr * C * 4 > (1 << 20) and tr % 16 == 0:
        tr //= 2

    def body(w_ref, g_ref, m_ref, v_ref, d_ref, nm_ref, nv_ref):
        d_ref[...], nm_ref[...], nv_ref[...] = _adamw_update(w_ref[...], g_ref[...], m_ref[...], v_ref[...])

    blk = pl.BlockSpec((tr, C), lambda i: (i, 0))
    sds = jax.ShapeDtypeStruct((R, C), F32)
    outs = _pcall(
        body, name=name, grid=(R // tr,),
        in_specs=[blk] * 4, out_specs=(blk,) * 3, out_shape=(sds,) * 3,
        compiler_params=_params(("parallel",)),
    )(*(t.reshape(R, C) for t in (w, g, m, v)))
    return tuple(o.reshape(shape) for o in outs)


_REPLICATED = ("kv_in_norm", "kv_latent_norm", "attn_norm", "q_latent_norm", "ffn_norm", "final_norm")
_WEIGHTS = ("pool_norm", "pool_w", "pool_scale", "kv_in_norm", "w_kv_a", "kv_latent_norm", "w_kv_b", "attn_norm",
            "w_q_a", "q_latent_norm", "w_q_b", "w_o", "ffn_norm", "w_gate", "w_up", "w_down", "final_norm")


def _ffn_fwd(x, norm_g, wg, wu, wd, tag, gather=None):
    S, D = x.shape
    fs = wg.shape[1]
    h, ht = _rmsnorm(x, norm_g, "ffn_norm" + tag, transposed_too=True)
    g, u, a, *gathered = _ffn_up(h, wg, wu, "ffn_up" + tag, ride=gather)
    if gather is not None:
        wd = gathered[0].reshape(N_DEV * fs, D)
    y = _ffn_down(a, wd, x, "ffn_down" + tag)
    return y, (ht, g, u, a), wd, gathered


def _ffn_bwd(x, norm_g, wg, wu, wd, saved, dy, dyb, tag, with_bf16, scatter=None):
    S, D = x.shape
    ht, g, u, a = saved
    fs = g.shape[1]
    tk = _tile(S, 2 * ROW_TILE)
    nkb = S // tk
    dg, du, *got_rider = _ffn_dact(dyb, wd, g, u, "ffn_dact" + tag, ride=scatter)
    dwd = _mm("ffn_dwd" + tag, "tn", a, dyb, (fs, D, tk), (N_DEV, 1, nkb),
              lambda j, q, k: (j * nkb + k, 0), lambda j, q, k: (k, 0), lambda j, q, k: (j, 0), (N_DEV * fs, D), BF16)
    dwg, (got_dwd,) = _slots_dw_t("ffn_dwg" + tag, ht, dg, ride=_Ride("scatter", [dwd.reshape(N_DEV, fs, D)]))
    dwu, (got_dwg,) = _slots_dw_t("ffn_dwu" + tag, ht, du, ride=_Ride("scatter", [dwg.reshape(N_DEV, D, fs)]))
    dh, got_dwu = _ffn_dh(dg, du, wg, wu, S, "ffn_dh" + tag, ride=_Ride("scatter", [dwu.reshape(N_DEV, D, fs)]))
    outs = _rmsnorm_bwd(x, norm_g, dh, dy, "ffn_norm_bwd" + tag, with_bf16=with_bf16)
    return outs, got_rider, (got_dwg, got_dwu, got_dwd)


def kernel(x, positions, pool_norm, pool_w, pool_scale, kv_in_norm, w_kv_a, kv_latent_norm, w_kv_b, attn_norm, w_q_a, q_latent_norm, w_q_b, w_o, ffn_norm, w_gate, w_up, w_down, final_norm, loss_target, m_pool_norm, m_pool_w, m_pool_scale, m_kv_in_norm, m_w_kv_a, m_kv_latent_norm, m_w_kv_b, m_attn_norm, m_w_q_a, m_q_latent_norm, m_w_q_b, m_w_o, m_ffn_norm, m_w_gate, m_w_up, m_w_down, m_final_norm, v_pool_norm, v_pool_w, v_pool_scale, v_kv_in_norm, v_w_kv_a, v_kv_latent_norm, v_w_kv_b, v_attn_norm, v_w_q_a, v_q_latent_norm, v_w_q_b, v_w_o, v_ffn_norm, v_w_gate, v_w_up, v_w_down, v_final_norm):
    env = dict(locals())
    weights = {n: env[n] for n in _WEIGHTS}
    m_in = {n: env["m_" + n] for n in _WEIGHTS}
    v_in = {n: env["v_" + n] for n in _WEIGHTS}

    S, D = x.shape[1], x.shape[2]
    xs = x.reshape(S, D)
    target = loss_target.reshape(S, D)
    G, gws, gw = pool_w.shape[1:]
    Ds = w_kv_a.shape[0]
    R_kv, kvw = w_kv_b.shape
    hpd = kvw // (QK_NOPE_DIM + V_DIM)
    H = hpd * N_DEV
    R_q = w_q_a.shape[2]
    fs = w_gate.shape[2]
    bf = lambda t: t.astype(BF16)

    gains = jnp.concatenate([pool_norm, pool_scale], axis=0)
    g_pool_w, g_gains = _all_gather([bf(pool_w[0]), gains], "gather_first")
    wqb_pad = jnp.pad(w_q_b.reshape(R_q, hpd, QK_DIM), ((0, 0), (0, 0), (0, HEAD_PAD - QK_DIM))).reshape(R_q, hpd * HEAD_PAD)
    gather_mla = _Ride("gather", [bf(w_down[0]), bf(jnp.pad(w_kv_a, ((0, 0), (0, 128 - QK_ROPE_DIM)))), bf(w_kv_b),
                                  bf(w_q_a[0]), bf(wqb_pad), bf(w_o[0])])
    gather_ffn1 = _Ride("gather", [bf(w_gate[1]), bf(w_up[1]), bf(w_down[1])])

    wp = jnp.swapaxes(g_pool_w, 0, 1).reshape(G, gw, gw)
    gains_full = jnp.swapaxes(g_gains, 0, 1).reshape(2, D)
    g_pool_norm, g_pool_scale = gains_full[0:1], gains_full[1:2]

    g_kv_in = kv_in_norm.reshape(1, D)
    g_kv_lat = kv_latent_norm.reshape(1, R_kv)
    g_attn = attn_norm.reshape(1, D)
    g_q_lat = q_latent_norm.reshape(1, R_q)
    g_final = final_norm.reshape(1, D)

    half = QK_ROPE_DIM // 2
    inv_freq = ROPE_BASE ** (-jnp.arange(half, dtype=F32) / half)
    ang = positions.reshape(S).astype(F32)[:, None] * inv_freq
    cos, sin = jnp.cos(ang), jnp.sin(ang)
    zeros = jnp.zeros((S, 128 - QK_ROPE_DIM), F32)
    cos_t = jnp.concatenate([cos, cos, zeros], axis=1)
    sin_t = jnp.concatenate([-sin, sin, zeros], axis=1)

    diff, g_wg0 = _pool_pre(xs, g_pool_norm, ride=_Ride("gather2", [bf(w_gate[0])]))
    x1, g_wu0 = _pool_mix(diff, wp, g_pool_scale, xs, ride=_Ride("gather2", [bf(w_up[0])]))
    wg0, wu0 = g_wg0.reshape(N_DEV * D, fs), g_wu0.reshape(N_DEV * D, fs)
    x2, ffn0, wd0, (_, g_wkva, g_wkvb, g_wqa, g_wqb, g_wo) = _ffn_fwd(x1, ffn_norm[0:1], wg0, wu0, None, "0", gather_mla)
    wkva = g_wkva.reshape(D, R_kv + 128)
    wkvb = g_wkvb.reshape(N_DEV * R_kv, kvw)
    wqa = g_wqa.reshape(D, R_q)
    wqb = g_wqb.reshape(N_DEV * R_q, hpd * HEAD_PAD)
    wo = g_wo.reshape(H * V_DIM, D)

    hk, ha = _rmsnorm_pair(x2, g_kv_in, g_attn, "kv_attn_norm")
    kv = _mm_plain("kv_a", "nn", hk, wkva)
    ckv, kpe = _kv_mid(kv, g_kv_lat, cos_t, sin_t)
    kvu = _small_slots_out("kv_b", ckv, wkvb, BF16)

    ql = _mm_plain("q_a", "nn", ha, wqa)
    qn = _rmsnorm(ql, g_q_lat, "q_latent_norm")
    tab = pl.BlockSpec((_tile(S, ROW_TILE), 128), lambda i: (i, 0))
    qr = _small_slots_out("q_b", qn, wqb, BF16, epilogue=(_rope_heads, [cos_t, sin_t], [tab, tab]))
    o, lse, g_wg1, g_wu1, g_wd1 = _attn_fwd(qr, kvu, kpe, H, ride=gather_ffn1)
    wg1, wu1, wd1 = g_wg1.reshape(N_DEV * D, fs), g_wu1.reshape(N_DEV * D, fs), g_wd1.reshape(N_DEV * fs, D)
    x3 = _mm_plain("attn_out", "nn", o, wo, res=x2)
    x4, ffn1, _, _ = _ffn_fwd(x3, ffn_norm[1:2], wg1, wu1, wd1, "1")

    loss_part, dx4, dx4b, d_final = _loss_head(x4, g_final, target)

    (dx3, dx3b, d_ffn1), _, got_ffn1 = _ffn_bwd(x3, ffn_norm[1:2], wg1, wu1, wd1, ffn1, dx4, dx4b, "1", True)

    do = _mm_plain("attn_out_dx", "nt", dx3b, wo, out_dtype=BF16)
    dwo = _mm_plain("attn_out_dw", "tn", o, dx3b, out_dtype=BF16)
    dq, dkvu, dkpe = _attn_bwd(qr, kvu, kpe, do, o, lse.reshape(H, 1, S), cos_t, sin_t, H)

    dqn = _small_slots_in_nt("q_b_dx", dq, wqb, S)
    dwqb = _small_slots_dw("q_b_dw", qn, dq, BF16)
    dql, d_q_lat = _rmsnorm_bwd(ql, g_q_lat, dqn, None, "q_latent_norm_bwd", out_dtype=BF16)
    dha = _mm_plain("q_a_dx", "nt", dql, wqa)
    dwqa = _mm_plain("q_a_dw", "tn", ha, dql, out_dtype=BF16)

    dckv = _small_slots_in_nt("kv_b_dx", dkvu, wkvb, S)
    dwkvb = _small_slots_dw("kv_b_dw", ckv, dkvu, BF16)
    dkv, d_kv_lat = _kv_mid_bwd(kv, g_kv_lat, dckv, dkpe, cos_t, sin_t)
    dhk = _mm_plain("kv_a_dx", "nt", dkv, wkva)
    dwkva = _mm_plain("kv_a_dw", "tn", hk, dkv, out_dtype=BF16)
    dx2, dx2b, d_attn, d_kv_in = _rmsnorm_pair_bwd(x2, g_attn, dha, g_kv_in, dhk, dx3, "kv_attn_norm_bwd")

    scatter_mla = _Ride("scatter", [dwkva.reshape(N_DEV, Ds, R_kv + 128), dwkvb.reshape(N_DEV, R_kv, kvw),
                                    dwqa.reshape(N_DEV, Ds, R_q), dwqb.reshape(N_DEV, R_q, hpd * HEAD_PAD),
                                    dwo.reshape(N_DEV, H * V_DIM // N_DEV, D)])
    (dx1, d_ffn0), got_mla, got_ffn0 = _ffn_bwd(x1, ffn_norm[0:1], wg0, wu0, wd0, ffn0, dx2, dx2b, "0", False,
                                                  scatter=scatter_mla)

    ddiff, dwp, d_pool_scale = _pool_mix_bwd(diff, wp, g_pool_scale, dx1)
    grad_x, d_pool_norm = _pool_pre_bwd(xs, g_pool_norm, ddiff, dx1)

    d_gains = jnp.swapaxes(jnp.concatenate([d_pool_norm, d_pool_scale], axis=0).reshape(2, N_DEV, D // N_DEV), 0, 1)
    scatter_pool = _Ride("scatter", [jnp.swapaxes(dwp.reshape(G, N_DEV, gws, gw), 0, 1), d_gains])

    grads, delta_w, new_m, new_v = {}, {}, {}, {}
    ffn_slots = {n: [r0, r1] for n, r0, r1 in zip(("w_gate", "w_up", "w_down"), got_ffn0, got_ffn1)}
    got_pool = []
    for n, ride in zip(ffn_slots, (scatter_pool, None, None)):
        grads[n], delta_w[n], new_m[n], new_v[n], *got = _adamw_from_slots(weights[n], ffn_slots[n], m_in[n], v_in[n],
                                                                            "adamw_" + n, ride=ride)
        got_pool += got
    got_pool_w, got_gains = got_pool
    received = {n: [r] for n, r in zip(("w_kv_a", "w_kv_b", "w_q_a", "w_q_b", "w_o"), got_mla)}
    received.update(pool_w=[got_pool_w], gains=[got_gains])
    sums = {n: _sum_slots(r, "sum_" + n) for n, r in received.items()}
    grads.update({
        "pool_w": sums["pool_w"],
        "w_kv_a": sums["w_kv_a"][0, :, :R_kv + QK_ROPE_DIM],
        "w_kv_b": sums["w_kv_b"][0],
        "w_q_a": sums["w_q_a"],
        "w_q_b": sums["w_q_b"].reshape(R_q, hpd, HEAD_PAD)[:, :, :QK_DIM].reshape(1, R_q, hpd * QK_DIM),
        "w_o": sums["w_o"],
        "pool_norm": sums["gains"][0, 0:1],
        "pool_scale": sums["gains"][0, 1:2],
    })

    small = {"kv_in_norm": d_kv_in, "kv_latent_norm": d_kv_lat, "attn_norm": d_attn, "q_latent_norm": d_q_lat,
             "ffn_norm": jnp.concatenate([d_ffn0, d_ffn1], axis=0), "final_norm": d_final}
    small_packed = jnp.concatenate([small[n].reshape(-1) for n in _REPLICATED] + [loss_part.reshape(-1)])
    pad = (-small_packed.shape[0]) % (8 * 128)
    reduced = _all_reduce_small(jnp.pad(small_packed, (0, pad)).reshape(-1, 128)).reshape(-1)
    off = 0
    for n in _REPLICATED:
        grads[n] = reduced[off:off + weights[n].size].reshape(weights[n].shape)
        off += weights[n].size
    loss = reduced[off]

    for n in _WEIGHTS:
        if n not in ffn_slots:
            delta_w[n], new_m[n], new_v[n] = _adamw(weights[n], grads[n], m_in[n], v_in[n], "adamw_" + n)

    return (loss, grad_x.reshape(x.shape), *[grads[n] for n in _WEIGHTS], *[delta_w[n] for n in _WEIGHTS],
            *[new_m[n] for n in _WEIGHTS], *[new_v[n] for n in _WEIGHTS])
```

```python
import math

import jax
import jax.numpy as jnp
import numpy as np
from jax import lax
from jax.experimental import pallas as pl
from jax.experimental.pallas import tpu as pltpu

F32 = jnp.float32
BF16 = jnp.bfloat16

N_DEV = 8
POOL_WINDOWS = (2, 4, 8, 16)
POOL_HALO = 16
QK_NOPE_DIM = 128
QK_ROPE_DIM = 64
QK_DIM = QK_NOPE_DIM + QK_ROPE_DIM
HEAD_PAD = 256
V_DIM = 128
ROPE_BASE = 10000.0
ATTN_SCALE = 1.0 / math.sqrt(QK_DIM)
EXP2_SCALE = ATTN_SCALE * math.log2(math.e)
NORM_EPS = 1e-6
ADAM_LR = 0.001
ADAM_B1 = 0.9
ADAM_B2 = 0.999
ADAM_EPS = 1e-08
ADAM_WD = 0.01
ADAM_STEP = 10

ROW_TILE = 512
ATT_BLOCK = 1024
ATT_LOOKAHEAD = 1
ATT_ROW_CHUNK = 256
FFN_SLOTS = 2
MM_TN, MM_TK = 512, 512
VMEM_LIMIT = 48 * 1024 * 1024

_pcall = pl.pallas_call


def _params(sem):
    return pltpu.CompilerParams(dimension_semantics=sem, vmem_limit_bytes=VMEM_LIMIT)


def _tile(dim, default):
    if dim % default == 0:
        return default
    assert dim <= 2 * default, (dim, default)
    return dim


def _row_tile(rows, width):
    ts = _tile(rows, ROW_TILE)
    while ts * width * 4 > (2 << 20) and ts % 16 == 0:
        ts //= 2
    return ts


def _rms_scale(x):
    return lax.rsqrt(jnp.mean(x * x, axis=-1, keepdims=True) + NORM_EPS)


def _rms_bwd(x, g, dy):
    r = _rms_scale(x)
    xn = x * r
    u = dy * g
    dx = r * (u - xn * jnp.mean(u * xn, axis=-1, keepdims=True))
    return dx, dy * xn


def _swap_halves(x):
    lane = lax.broadcasted_iota(jnp.int32, x.shape, 1)
    return jnp.where(lane < QK_ROPE_DIM // 2, pltpu.roll(x, 128 - QK_ROPE_DIM // 2, 1), pltpu.roll(x, QK_ROPE_DIM // 2, 1))


def _rope(x, cos_t, sin_t):
    return x * cos_t + _swap_halves(x) * sin_t


def _rmsnorm(x, g, name, transposed_too=False):
    S, D = x.shape
    ts = _row_tile(S, D)

    def body(x_ref, g_ref, o_ref, *t_ref):
        xf = x_ref[...]
        y = xf * _rms_scale(xf) * g_ref[...]
        o_ref[...] = y.astype(o_ref.dtype)
        if transposed_too:
            t_ref[0][...] = y.T.astype(o_ref.dtype)

    row = pl.BlockSpec((ts, D), lambda i: (i, 0))
    outs = _pcall(
        body, name=name, grid=(S // ts,),
        in_specs=[row, pl.BlockSpec((1, D), lambda i: (0, 0))],
        out_specs=(row, pl.BlockSpec((D, ts), lambda i: (0, i))) if transposed_too else row,
        out_shape=((jax.ShapeDtypeStruct((S, D), BF16), jax.ShapeDtypeStruct((D, S), BF16)) if transposed_too
                   else jax.ShapeDtypeStruct((S, D), BF16)),
        compiler_params=_params(("parallel",)),
    )(x, g)
    return outs


def _rmsnorm_bwd(x, g, dy, res, name, out_dtype=F32, with_bf16=False):
    S, D = x.shape
    ts = _row_tile(S, D)
    has_res = res is not None

    def body(*refs):
        refs = list(refs)
        x_ref, g_ref, dy_ref = refs[:3]
        res_ref = refs[3] if has_res else None
        outs = refs[3 + has_res:]
        dx_ref, dg_ref = outs[0], outs[-1]
        dx, dgt = _rms_bwd(x_ref[...], g_ref[...], dy_ref[...].astype(F32))
        if has_res:
            dx = res_ref[...] + dx
        dx_ref[...] = dx.astype(dx_ref.dtype)
        if with_bf16:
            outs[1][...] = dx.astype(BF16)

        @pl.when(pl.program_id(0) == 0)
        def _():
            dg_ref[...] = jnp.zeros_like(dg_ref)

        dg_ref[...] += jnp.sum(dgt, axis=0, keepdims=True)

    row = pl.BlockSpec((ts, D), lambda i: (i, 0))
    vec = pl.BlockSpec((1, D), lambda i: (0, 0))
    out_specs = [row] + ([row] if with_bf16 else []) + [vec]
    out_shape = ([jax.ShapeDtypeStruct((S, D), out_dtype)] + ([jax.ShapeDtypeStruct((S, D), BF16)] if with_bf16 else [])
                 + [jax.ShapeDtypeStruct((1, D), F32)])
    return _pcall(
        body, name=name, grid=(S // ts,),
        in_specs=[row, vec, row] + ([row] if has_res else []),
        out_specs=tuple(out_specs), out_shape=tuple(out_shape),
        compiler_params=_params(("arbitrary",)),
    )(*([x, g, dy] + ([res] if has_res else [])))


def _rmsnorm_pair(x, g1, g2, name):
    S, D = x.shape
    ts = _row_tile(S, D)

    def body(x_ref, g1_ref, g2_ref, o1_ref, o2_ref):
        xf = x_ref[...]
        xn = xf * _rms_scale(xf)
        o1_ref[...] = (xn * g1_ref[...]).astype(o1_ref.dtype)
        o2_ref[...] = (xn * g2_ref[...]).astype(o2_ref.dtype)

    row = pl.BlockSpec((ts, D), lambda i: (i, 0))
    vec = pl.BlockSpec((1, D), lambda i: (0, 0))
    sds = jax.ShapeDtypeStruct((S, D), BF16)
    return _pcall(
        body, name=name, grid=(S // ts,),
        in_specs=[row, vec, vec], out_specs=(row, row), out_shape=(sds, sds),
        compiler_params=_params(("parallel",)),
    )(x, g1, g2)


def _rmsnorm_pair_bwd(x, g1, dy1, g2, dy2, res, name):
    S, D = x.shape
    ts = _row_tile(S, D)

    def body(x_ref, g1_ref, dy1_ref, g2_ref, dy2_ref, res_ref, dx_ref, dxb_ref, dg1_ref, dg2_ref):
        xf = x_ref[...]
        dx1, dgt1 = _rms_bwd(xf, g1_ref[...], dy1_ref[...])
        dx2, dgt2 = _rms_bwd(xf, g2_ref[...], dy2_ref[...])
        dx = (res_ref[...] + dx1) + dx2
        dx_ref[...] = dx
        dxb_ref[...] = dx.astype(BF16)

        @pl.when(pl.program_id(0) == 0)
        def _():
            dg1_ref[...] = jnp.zeros_like(dg1_ref)
            dg2_ref[...] = jnp.zeros_like(dg2_ref)

        dg1_ref[...] += jnp.sum(dgt1, axis=0, keepdims=True)
        dg2_ref[...] += jnp.sum(dgt2, axis=0, keepdims=True)

    row = pl.BlockSpec((ts, D), lambda i: (i, 0))
    vec = pl.BlockSpec((1, D), lambda i: (0, 0))
    return _pcall(
        body, name=name, grid=(S // ts,),
        in_specs=[row, vec, row, vec, row, row], out_specs=(row, row, vec, vec),
        out_shape=(jax.ShapeDtypeStruct((S, D), F32), jax.ShapeDtypeStruct((S, D), BF16),
                   jax.ShapeDtypeStruct((1, D), F32), jax.ShapeDtypeStruct((1, D), F32)),
        compiler_params=_params(("arbitrary",)),
    )(x, g1, dy1, g2, dy2, res)


def _pool_pre(x, g, ride=None):
    S, D = x.shape
    ts = _row_tile(S, D)
    gw = D // len(POOL_WINDOWS)
    hb = ts // POOL_HALO

    def body(ins, outs, scratch):
        (x_ref, halo_ref, g_ref), (o_ref,) = ins, outs
        i = pl.program_id(0)
        xx = jnp.concatenate([halo_ref[...], x_ref[...]], axis=0)
        h = xx * _rms_scale(xx) * g_ref[...]
        t = i * ts - POOL_HALO + lax.broadcasted_iota(jnp.int32, (ts + POOL_HALO, 1), 0)
        h = jnp.where(t >= 0, h, 0.0)
        tf = t[POOL_HALO:].astype(F32)
        for gi, w in enumerate(POOL_WINDOWS):
            hg = h[:, gi * gw:(gi + 1) * gw]
            acc, span = hg, 1
            while span < w:
                acc = acc + pltpu.roll(acc, span, 0)
                span *= 2
            count = jnp.minimum(tf + 1.0, float(w))
            diff = acc[POOL_HALO:] / count - hg[POOL_HALO:]
            o_ref[:, gi * gw:(gi + 1) * gw] = diff.astype(o_ref.dtype)

    return _pallas(body, "pool_pre", (S // ts,), [x, x, g],
                   [pl.BlockSpec((ts, D), lambda i: (i, 0)),
                    pl.BlockSpec((POOL_HALO, D), lambda i: (jnp.maximum(i * hb - 1, 0), 0)),
                    pl.BlockSpec((1, D), lambda i: (0, 0))],
                   [pl.BlockSpec((ts, D), lambda i: (i, 0))], [jax.ShapeDtypeStruct((S, D), BF16)],
                   sem=("parallel",), ride=ride)


def _pool_pre_bwd(x, g, dd, res):
    S, D = x.shape
    ts = _row_tile(S, D)
    gw = D // len(POOL_WINDOWS)
    hb = ts // POOL_HALO
    last_halo = S // POOL_HALO - 1

    def body(x_ref, g_ref, dd_ref, halo_ref, res_ref, dx_ref, dg_ref):
        i = pl.program_id(0)
        ee = jnp.concatenate([dd_ref[...], halo_ref[...]], axis=0)
        t = i * ts + lax.broadcasted_iota(jnp.int32, (ts + POOL_HALO, 1), 0)
        ee = jnp.where(t < S, ee, 0.0)
        tf = t.astype(F32)
        n = ts + POOL_HALO
        for gi, w in enumerate(POOL_WINDOWS):
            eg = ee[:, gi * gw:(gi + 1) * gw]
            acc, span = eg / jnp.minimum(tf + 1.0, float(w)), 1
            while span < w:
                acc = acc + pltpu.roll(acc, n - span, 0)
                span *= 2
            dx_ref[:, gi * gw:(gi + 1) * gw] = acc[:ts] - eg[:ts]
        dx, dgt = _rms_bwd(x_ref[...], g_ref[...], dx_ref[...])
        dx_ref[...] = res_ref[...] + dx

        @pl.when(i == 0)
        def _():
            dg_ref[...] = jnp.zeros_like(dg_ref)

        dg_ref[...] += jnp.sum(dgt, axis=0, keepdims=True)

    row = pl.BlockSpec((ts, D), lambda i: (i, 0))
    vec = pl.BlockSpec((1, D), lambda i: (0, 0))
    return _pcall(
        body, name="pool_pre_bwd", grid=(S // ts,),
        in_specs=[row, vec, row,
                  pl.BlockSpec((POOL_HALO, D), lambda i: (jnp.minimum((i + 1) * hb, last_halo), 0)), row],
        out_specs=(row, vec),
        out_shape=(jax.ShapeDtypeStruct((S, D), F32), jax.ShapeDtypeStruct((1, D), F32)),
        compiler_params=_params(("arbitrary",)),
    )(x, g, dd, dd, res)


def _pool_mix(diff, w, scale, x, ride=None):
    S, D = x.shape
    G, gw, _ = w.shape
    ts = _tile(S, ROW_TILE)

    def body(ins, outs, scratch):
        (d_ref, w_ref, s_ref, x_ref), (o_ref,) = ins, outs
        mo = jnp.dot(d_ref[...], w_ref[0], preferred_element_type=F32)
        o_ref[...] = x_ref[...] + mo * s_ref[...]

    blk = pl.BlockSpec((ts, gw), lambda i, g: (i, g))
    return _pallas(body, "pool_mix", (S // ts, G), [diff, w, scale, x],
                   [blk, pl.BlockSpec((1, gw, gw), lambda i, g: (g, 0, 0)), pl.BlockSpec((1, gw), lambda i, g: (0, g)), blk],
                   [blk], [jax.ShapeDtypeStruct((S, D), F32)], sem=("parallel", "parallel"), ride=ride)


def _pool_mix_bwd(diff, w, scale, dy):
    S, D = dy.shape
    G, gw, _ = w.shape
    ts = _tile(S, ROW_TILE)

    def body(d_ref, w_ref, s_ref, dy_ref, dd_ref, dw_ref, ds_ref):
        i = pl.program_id(1)
        d = d_ref[...]
        dy = dy_ref[...]
        mo = jnp.dot(d, w_ref[0], preferred_element_type=F32)
        dmo = (dy * s_ref[...]).astype(BF16)
        dd_ref[...] = lax.dot_general(dmo, w_ref[0], (((1,), (1,)), ((), ())), preferred_element_type=F32)

        @pl.when(i == 0)
        def _():
            dw_ref[...] = jnp.zeros_like(dw_ref)
            ds_ref[...] = jnp.zeros_like(ds_ref)

        dw_ref[0] += lax.dot_general(d, dmo, (((0,), (0,)), ((), ())), preferred_element_type=F32)
        ds_ref[...] += jnp.sum(dy * mo, axis=0, keepdims=True)

    blk = pl.BlockSpec((ts, gw), lambda g, i: (i, g))
    wspec = pl.BlockSpec((1, gw, gw), lambda g, i: (g, 0, 0))
    vec = pl.BlockSpec((1, gw), lambda g, i: (0, g))
    return _pcall(
        body, name="pool_mix_bwd", grid=(G, S // ts),
        in_specs=[blk, wspec, vec, blk],
        out_specs=(blk, wspec, vec),
        out_shape=(jax.ShapeDtypeStruct((S, D), F32), jax.ShapeDtypeStruct((G, gw, gw), F32),
                   jax.ShapeDtypeStruct((1, D), F32)),
        compiler_params=_params(("parallel", "arbitrary")),
    )(diff, w, scale, dy)


def _loss_head(x, g, target):
    S, D = x.shape
    ts = _row_tile(S, D)

    def body(x_ref, g_ref, t_ref, loss_ref, dx_ref, dxb_ref, dg_ref):
        xf = x_ref[...]
        gv = g_ref[...]
        err = xf * _rms_scale(xf) * gv - t_ref[...]
        dx, dgt = _rms_bwd(xf, gv, err * (1.0 / D))
        dx_ref[...] = dx
        dxb_ref[...] = dx.astype(BF16)

        @pl.when(pl.program_id(0) == 0)
        def _():
            loss_ref[...] = jnp.zeros_like(loss_ref)
            dg_ref[...] = jnp.zeros_like(dg_ref)

        part = 0.5 * jnp.sum(jnp.sum(err * err, axis=-1, keepdims=True) * (1.0 / D), axis=0, keepdims=True)
        loss_ref[...] += jnp.broadcast_to(part, loss_ref.shape)
        dg_ref[...] += jnp.sum(dgt, axis=0, keepdims=True)

    row = pl.BlockSpec((ts, D), lambda i: (i, 0))
    vec = pl.BlockSpec((1, D), lambda i: (0, 0))
    return _pcall(
        body, name="loss_head", grid=(S // ts,),
        in_specs=[row, vec, row],
        out_specs=(pl.BlockSpec((1, 128), lambda i: (0, 0)), row, row, vec),
        out_shape=(jax.ShapeDtypeStruct((1, 128), F32), jax.ShapeDtypeStruct((S, D), F32),
                   jax.ShapeDtypeStruct((S, D), BF16), jax.ShapeDtypeStruct((1, D), F32)),
        compiler_params=_params(("arbitrary",)),
    )(x, g, target)


_DIMS = {"nn": (((1,), (0,)), ((), ())), "nt": (((1,), (1,)), ((), ())), "tn": (((0,), (0,)), ((), ()))}


def _mm(name, mode, a, b, tiles, grid, a_map, b_map, o_map, out_shape, out_dtype=F32, res=None, ride=None):
    tm, tn, tk = tiles
    nk = grid[-1]
    has_res = res is not None
    dn = _DIMS[mode]

    def body(ins, outs, scratch):
        o_ref = outs[0]

        def product():
            return lax.dot_general(ins[0][...].astype(BF16), ins[1][...].astype(BF16), dn, preferred_element_type=F32)

        def finish(out):
            if has_res:
                out = ins[2][...] + out
            o_ref[...] = out.astype(o_ref.dtype)

        if nk == 1:
            finish(product())
            return
        acc_ref = scratch[0]
        k = pl.program_id(2)

        @pl.when(k == 0)
        def _():
            acc_ref[...] = jnp.zeros_like(acc_ref)

        acc_ref[...] += product()

        @pl.when(k == nk - 1)
        def _():
            finish(acc_ref[...])

    a_spec = pl.BlockSpec((tk, tm) if mode == "tn" else (tm, tk), a_map)
    b_spec = pl.BlockSpec((tn, tk) if mode == "nt" else (tk, tn), b_map)
    o_spec = pl.BlockSpec((tm, tn), o_map)
    operands, in_specs = [a, b], [a_spec, b_spec]
    if has_res:
        operands.append(res)
        in_specs.append(o_spec)
    outs = _pallas(body, name, grid, operands, in_specs, [o_spec], [jax.ShapeDtypeStruct(out_shape, out_dtype)],
                   scratch=[pltpu.VMEM((tm, tn), F32)] if nk > 1 else [],
                   sem=("parallel", "parallel", "arbitrary"), ride=ride)
    return (outs[0], outs[1:]) if ride else outs[0]


def _mm_plain(name, mode, a, b, out_dtype=F32, res=None):
    if mode == "tn":
        (K, M), N = a.shape, b.shape[1]
    elif mode == "nt":
        (M, K), N = a.shape, b.shape[0]
    else:
        (M, K), N = a.shape, b.shape[1]
    if mode == "tn":
        tm, tn, tk = _tile(M, 2 * MM_TN), _tile(N, 2 * MM_TN), _tile(K, ROW_TILE)
    else:
        tm = _tile(M, ROW_TILE)
        tk = K if K <= 4 * MM_TK else _tile(K, MM_TK)
        tn = N if N * tk * 2 <= (8 << 20) else _tile(N, MM_TN)
    a_map = (lambda i, j, k: (k, i)) if mode == "tn" else (lambda i, j, k: (i, k))
    b_map = (lambda i, j, k: (j, k)) if mode == "nt" else (lambda i, j, k: (k, j))
    return _mm(name, mode, a, b, (tm, tn, tk), (M // tm, N // tn, K // tk), a_map, b_map, lambda i, j, k: (i, j),
               (M, N), out_dtype, res)


def _small_slots_out(name, a, w_slots, out_dtype, epilogue=None):
    S, K = a.shape
    n = w_slots.shape[1]
    tm = _tile(S, ROW_TILE)
    extra = epilogue[1] if epilogue else []

    def body(*refs):
        a_ref, w_ref, o_ref = refs[0], refs[1], refs[-1]
        av = a_ref[...].astype(BF16)
        tables = [r[...] for r in refs[2:-1]]
        for j in range(N_DEV):
            out = jnp.dot(av, w_ref[j], preferred_element_type=F32)
            if epilogue:
                out = epilogue[0](out, *tables)
            o_ref[j] = out.astype(o_ref.dtype)

    return _pcall(
        body, name=name, grid=(S // tm,),
        in_specs=[pl.BlockSpec((tm, K), lambda i: (i, 0)), pl.BlockSpec((N_DEV, K, n), lambda i: (0, 0, 0))]
        + (list(epilogue[2]) if epilogue else []),
        out_specs=pl.BlockSpec((N_DEV, tm, n), lambda i: (0, i, 0)),
        out_shape=jax.ShapeDtypeStruct((N_DEV, S, n), out_dtype),
        compiler_params=_params(("parallel",)),
    )(a, w_slots.reshape(N_DEV, K, n), *extra).reshape(N_DEV * S, n)


def _small_slots_in_nt(name, a_slots, w_slots, S):
    n = a_slots.shape[1]
    K = w_slots.shape[0] // N_DEV
    tm = _tile(S, ROW_TILE)

    def body(a_ref, w_ref, o_ref):
        total = lax.dot_general(a_ref[0], w_ref[0], _DIMS["nt"], preferred_element_type=F32)
        for j in range(1, N_DEV):
            total += lax.dot_general(a_ref[j], w_ref[j], _DIMS["nt"], preferred_element_type=F32)
        o_ref[...] = total

    return _pcall(
        body, name=name, grid=(S // tm,),
        in_specs=[pl.BlockSpec((N_DEV, tm, n), lambda i: (0, i, 0)), pl.BlockSpec((N_DEV, K, n), lambda i: (0, 0, 0))],
        out_specs=pl.BlockSpec((tm, K), lambda i: (i, 0)),
        out_shape=jax.ShapeDtypeStruct((S, K), F32),
        compiler_params=_params(("parallel",)),
    )(a_slots.reshape(N_DEV, S, n), w_slots.reshape(N_DEV, K, n))


def _small_slots_dw(name, a, d_slots, out_dtype):
    S, K = a.shape
    n = d_slots.shape[1]
    tk = _tile(S, ROW_TILE)
    nkb = S // tk

    def body(a_ref, d_ref, o_ref, acc_ref):
        k = pl.program_id(0)

        @pl.when(k == 0)
        def _():
            acc_ref[...] = jnp.zeros_like(acc_ref)

        at = a_ref[...].astype(F32).T.astype(BF16)
        for j in range(N_DEV):
            acc_ref[j] += jnp.dot(at, d_ref[j], preferred_element_type=F32)

        @pl.when(k == nkb - 1)
        def _():
            o_ref[...] = acc_ref[...].astype(o_ref.dtype)

    return _pcall(
        body, name=name, grid=(nkb,),
        in_specs=[pl.BlockSpec((tk, K), lambda k: (k, 0)), pl.BlockSpec((N_DEV, tk, n), lambda k: (0, k, 0))],
        out_specs=pl.BlockSpec((N_DEV, K, n), lambda k: (0, 0, 0)),
        out_shape=jax.ShapeDtypeStruct((N_DEV, K, n), out_dtype),
        scratch_shapes=[pltpu.VMEM((N_DEV, K, n), F32)],
        compiler_params=_params(("arbitrary",)),
    )(a, d_slots.reshape(N_DEV, S, n)).reshape(N_DEV * K, n)


def _sigmoid(x):
    return 1.0 / (1.0 + jnp.exp(-x))


def _ffn_up(h, wg, wu, name, ride=None):
    S, D = h.shape
    fs = wg.shape[1]
    tm = _tile(S, ROW_TILE)
    sp = FFN_SLOTS

    def body(ins, outs, scratch):
        h_ref, wg_ref, wu_ref = ins
        g_ref, u_ref, a_ref = outs
        hv = h_ref[...]
        for s in range(sp):
            g = jnp.dot(hv, wg_ref[s], preferred_element_type=F32)
            u = jnp.dot(hv, wu_ref[s], preferred_element_type=F32)
            g_ref[s] = g.astype(g_ref.dtype)
            u_ref[s] = u.astype(u_ref.dtype)
            a_ref[s] = (g * _sigmoid(g) * u).astype(a_ref.dtype)

    w_spec = pl.BlockSpec((sp, D, fs), lambda i, j: (j, 0, 0))
    o_spec = pl.BlockSpec((sp, tm, fs), lambda i, j: (j, i, 0))
    sds = jax.ShapeDtypeStruct((N_DEV, S, fs), BF16)
    g, u, a, *rest = _pallas(body, name, (S // tm, N_DEV // sp), [h, wg.reshape(N_DEV, D, fs), wu.reshape(N_DEV, D, fs)],
                             [pl.BlockSpec((tm, D), lambda i, j: (i, 0)), w_spec, w_spec], [o_spec, o_spec, o_spec],
                             [sds, sds, sds], sem=("parallel", "arbitrary"), ride=ride)
    return [t.reshape(N_DEV * S, fs) for t in (g, u, a)] + rest


def _ffn_dact(dy, wd, g, u, name, ride=None):
    S, D = dy.shape
    fs = g.shape[1]
    tm = _tile(S, ROW_TILE)
    sp = FFN_SLOTS

    def body(ins, outs, scratch):
        dy_ref, wd_ref, g_ref, u_ref = ins
        dg_ref, du_ref = outs
        dy = dy_ref[...]
        for s in range(sp):
            da = lax.dot_general(dy, wd_ref[s], _DIMS["nt"], preferred_element_type=F32)
            g, u = g_ref[s].astype(F32), u_ref[s].astype(F32)
            sig = _sigmoid(g)
            dg_ref[s] = (da * u * (sig * (1.0 + g * (1.0 - sig)))).astype(dg_ref.dtype)
            du_ref[s] = (da * (g * sig)).astype(du_ref.dtype)

    o_spec = pl.BlockSpec((sp, tm, fs), lambda j, i: (j, i, 0))
    sds = jax.ShapeDtypeStruct((N_DEV, S, fs), BF16)
    dg, du, *rest = _pallas(body, name, (N_DEV // sp, S // tm),
                            [dy, wd.reshape(N_DEV, fs, D), g.reshape(N_DEV, S, fs), u.reshape(N_DEV, S, fs)],
                            [pl.BlockSpec((tm, D), lambda j, i: (i, 0)), pl.BlockSpec((sp, fs, D), lambda j, i: (j, 0, 0)),
                             o_spec, o_spec], [o_spec, o_spec], [sds, sds], sem=("parallel", "arbitrary"), ride=ride)
    return [dg.reshape(N_DEV * S, fs), du.reshape(N_DEV * S, fs)] + rest


def _ffn_down(a, wd, x, name):
    S, D = x.shape
    fs = a.shape[1]
    tm = _tile(S, ROW_TILE)
    sp = FFN_SLOTS
    nj = N_DEV // sp

    def body(a_ref, w_ref, x_ref, o_ref, acc_ref):
        j = pl.program_id(1)

        @pl.when(j == 0)
        def _():
            acc_ref[...] = x_ref[...]

        part = jnp.dot(a_ref[0], w_ref[0], preferred_element_type=F32)
        for s in range(1, sp):
            part += jnp.dot(a_ref[s], w_ref[s], preferred_element_type=F32)
        acc_ref[...] += part

        @pl.when(j == nj - 1)
        def _():
            o_ref[...] = acc_ref[...]

    row = pl.BlockSpec((tm, D), lambda i, j: (i, 0))
    return _pcall(
        body, name=name, grid=(S // tm, nj),
        in_specs=[pl.BlockSpec((sp, tm, fs), lambda i, j: (j, i, 0)), pl.BlockSpec((sp, fs, D), lambda i, j: (j, 0, 0)), row],
        out_specs=row, out_shape=jax.ShapeDtypeStruct((S, D), F32),
        scratch_shapes=[pltpu.VMEM((tm, D), F32)],
        compiler_params=_params(("parallel", "arbitrary")),
    )(a.reshape(N_DEV, S, fs), wd.reshape(N_DEV, fs, D), x)


def _ffn_dh(dg, du, wg, wu, S, name, ride=None):
    fs = dg.shape[1]
    D = wg.shape[0] // N_DEV
    tm = _tile(S, ROW_TILE)
    sp = FFN_SLOTS
    nj = N_DEV // sp

    def body(ins, outs, scratch):
        dg_ref, du_ref, wg_ref, wu_ref = ins
        o_ref, acc_ref = outs[0], scratch[0]
        j = pl.program_id(1)

        @pl.when(j == 0)
        def _():
            acc_ref[...] = jnp.zeros_like(acc_ref)

        part = None
        for s in range(sp):
            for d_ref, w_ref in ((dg_ref, wg_ref), (du_ref, wu_ref)):
                term = lax.dot_general(d_ref[s], w_ref[s], _DIMS["nt"], preferred_element_type=F32)
                part = term if part is None else part + term
        acc_ref[...] += part

        @pl.when(j == nj - 1)
        def _():
            o_ref[...] = acc_ref[...]

    d_spec = pl.BlockSpec((sp, tm, fs), lambda i, j: (j, i, 0))
    w_spec = pl.BlockSpec((sp, D, fs), lambda i, j: (j, 0, 0))
    return _pallas(body, name, (S // tm, nj),
                   [dg.reshape(N_DEV, S, fs), du.reshape(N_DEV, S, fs), wg.reshape(N_DEV, D, fs), wu.reshape(N_DEV, D, fs)],
                   [d_spec, d_spec, w_spec, w_spec], [pl.BlockSpec((tm, D), lambda i, j: (i, 0))],
                   [jax.ShapeDtypeStruct((S, D), F32)], scratch=[pltpu.VMEM((tm, D), F32)],
                   sem=("parallel", "arbitrary"), ride=ride)


def _slots_dw_t(name, at, d_slots, ride=None):
    K, S = at.shape
    n = d_slots.shape[1]
    tk = _tile(S, 2 * ROW_TILE)
    nkb = S // tk
    return _mm(name, "nn", at, d_slots, (K, n, tk), (N_DEV, 1, nkb),
               lambda j, q, k: (0, k), lambda j, q, k: (j * nkb + k, 0), lambda j, q, k: (j, 0),
               (N_DEV * K, n), BF16, ride=ride)


def _rope_heads(q, cos_t, sin_t):
    parts = []
    for c0 in range(0, q.shape[1], HEAD_PAD):
        parts += [q[:, c0:c0 + QK_NOPE_DIM], _rope(q[:, c0 + QK_NOPE_DIM:c0 + HEAD_PAD], cos_t, sin_t)]
    return jnp.concatenate(parts, axis=1)


def _kv_mid(kv, g, cos_t, sin_t):
    S, W = kv.shape
    R = W - 128
    ts = _tile(S, ROW_TILE)

    def body(kv_ref, g_ref, c_ref, s_ref, c_out, k_out):
        cf = kv_ref[:, :R]
        c_out[...] = (cf * _rms_scale(cf) * g_ref[...]).astype(c_out.dtype)
        k_out[...] = _rope(kv_ref[:, R:], c_ref[...], s_ref[...]).astype(k_out.dtype)

    tab = pl.BlockSpec((ts, 128), lambda i: (i, 0))
    return _pcall(
        body, name="kv_mid", grid=(S // ts,),
        in_specs=[pl.BlockSpec((ts, W), lambda i: (i, 0)), pl.BlockSpec((1, R), lambda i: (0, 0)), tab, tab],
        out_specs=(pl.BlockSpec((ts, R), lambda i: (i, 0)), tab),
        out_shape=(jax.ShapeDtypeStruct((S, R), BF16), jax.ShapeDtypeStruct((S, 128), BF16)),
        compiler_params=_params(("parallel",)),
    )(kv, g, cos_t, sin_t)


def _kv_mid_bwd(kv, g, dc, dkpe, cos_t, sin_t):
    S, W = kv.shape
    R = W - 128
    H = dkpe.shape[0]
    ts = _row_tile(S, H * 128)

    def body(kv_ref, g_ref, dc_ref, dk_ref, c_ref, s_ref, dkv_ref, dg_ref):
        dx, dgt = _rms_bwd(kv_ref[:, :R], g_ref[...], dc_ref[...])
        dkv_ref[:, :R] = dx.astype(dkv_ref.dtype)
        dk = dk_ref[0]
        for h in range(1, H):
            dk = dk + dk_ref[h]
        dkv_ref[:, R:] = _rope(dk, c_ref[...], -s_ref[...]).astype(dkv_ref.dtype)

        @pl.when(pl.program_id(0) == 0)
        def _():
            dg_ref[...] = jnp.zeros_like(dg_ref)

        dg_ref[...] += jnp.sum(dgt, axis=0, keepdims=True)

    tab = pl.BlockSpec((ts, 128), lambda i: (i, 0))
    vec = pl.BlockSpec((1, R), lambda i: (0, 0))
    return _pcall(
        body, name="kv_mid_bwd", grid=(S // ts,),
        in_specs=[pl.BlockSpec((ts, W), lambda i: (i, 0)), vec, pl.BlockSpec((ts, R), lambda i: (i, 0)),
                  pl.BlockSpec((H, ts, 128), lambda i: (0, i, 0)), tab, tab],
        out_specs=(pl.BlockSpec((ts, W), lambda i: (i, 0)), vec),
        out_shape=(jax.ShapeDtypeStruct((S, W), BF16), jax.ShapeDtypeStruct((1, R), F32)),
        compiler_params=_params(("arbitrary",)),
    )(kv, g, dc, dkpe, cos_t, sin_t)


def _block_pairs(nb, by_key):
    pairs = ([(qi, ki) for ki in range(nb) for qi in range(ki, nb)] if by_key
             else [(qi, ki) for qi in range(nb) for ki in range(qi + 1)])
    return jnp.asarray(np.array([p[0] for p in pairs], np.int32)), jnp.asarray(np.array([p[1] for p in pairs], np.int32))


def _causal_mask(blk, transposed=False, rows=None, row0=0):
    shape = (blk if rows is None else rows, blk)
    r = lax.broadcasted_iota(jnp.int32, shape, 0) + row0
    c = lax.broadcasted_iota(jnp.int32, shape, 1)
    return (r <= c) if transposed else (c <= r)


def _attn_specs(S, blk, hpd):
    nb = S // blk
    w = hpd * HEAD_PAD
    return dict(
        q=pl.BlockSpec((blk, w), lambda j, t, qt, kt: (j * nb + qt[t], 0)),
        kv=pl.BlockSpec((blk, w), lambda j, t, qt, kt: (j * nb + kt[t], 0)),
        kp=pl.BlockSpec((blk, 128), lambda j, t, qt, kt: (kt[t], 0)),
        do=pl.BlockSpec((blk, hpd * V_DIM), lambda j, t, qt, kt: (qt[t], j)),
        col=pl.BlockSpec((hpd, blk, 1), lambda j, t, qt, kt: (j, qt[t], 0)),
        row=pl.BlockSpec((hpd, 1, blk), lambda j, t, qt, kt: (j, 0, qt[t])))


def _head_k(kv_ref, kp, hh):
    return jnp.concatenate([kv_ref[:, hh * HEAD_PAD:hh * HEAD_PAD + QK_NOPE_DIM], kp], axis=1)


def _head_v(kv_ref, hh):
    return kv_ref[:, hh * HEAD_PAD + QK_NOPE_DIM:(hh + 1) * HEAD_PAD]


def _attn_fwd(q, kvu, kpe, H, ride=None):
    S = kpe.shape[0]
    hpd = H // N_DEV
    blk = _tile(S, ATT_BLOCK)
    q_tab, k_tab = _block_pairs(S // blk, by_key=False)
    rc = min(blk, ATT_ROW_CHUNK)

    def body(ins, outs, scratch):
        qt, kt, q_ref, kv_ref, kp_ref = ins
        o_ref, lse_ref = outs
        m_ref, acc_ref = scratch
        t = pl.program_id(1)
        qi, ki = qt[t], kt[t]

        @pl.when(ki == 0)
        def _():
            m_ref[...] = jnp.full_like(m_ref, -jnp.inf)
            acc_ref[...] = jnp.zeros_like(acc_ref)

        def step(masked):
            kp = kp_ref[...]
            ones = jnp.ones((blk, 128), BF16)
            ks = [_head_k(kv_ref, kp, hh) for hh in range(hpd)]
            v_exts = [jnp.concatenate([_head_v(kv_ref, hh), ones], axis=1) for hh in range(hpd)]
            chains = [(hh, r0) for r0 in range(0, blk, rc) for hh in range(hpd)]

            def scores(hh, r0):
                return lax.dot_general(q_ref[r0:r0 + rc, hh * HEAD_PAD:(hh + 1) * HEAD_PAD], ks[hh], _DIMS["nt"],
                                       preferred_element_type=F32)

            ahead = [scores(*ch) for ch in chains[:ATT_LOOKAHEAD]]
            for n, (hh, r0) in enumerate(chains):
                rows = slice(r0, r0 + rc)
                s = ahead.pop(0)
                if n + ATT_LOOKAHEAD < len(chains):
                    ahead.append(scores(*chains[n + ATT_LOOKAHEAD]))
                if masked:
                    s = jnp.where(_causal_mask(blk, rows=rc, row0=r0), s, -jnp.inf)
                m_old = m_ref[hh, rows]
                m_new = jnp.maximum(m_old, jnp.max(s, axis=-1, keepdims=True))
                p = jnp.exp2((s - m_new) * EXP2_SCALE).astype(BF16)
                acc_ref[hh, rows] = (jnp.exp2((m_old - m_new) * EXP2_SCALE) * acc_ref[hh, rows]
                                     + jnp.dot(p, v_exts[hh], preferred_element_type=F32))
                m_ref[hh, rows] = m_new

        @pl.when(ki < qi)
        def _():
            step(False)

        @pl.when(ki == qi)
        def _():
            step(True)
            for hh in range(hpd):
                l = acc_ref[hh, :, V_DIM:]
                o_ref[:, hh * V_DIM:(hh + 1) * V_DIM] = (acc_ref[hh, :, :V_DIM] / l).astype(o_ref.dtype)
                lse_ref[hh] = m_ref[hh] * EXP2_SCALE + jnp.log2(l[:, :1])

    sp = _attn_specs(S, blk, hpd)
    return _pallas(body, "attn_fwd", (N_DEV, q_tab.shape[0]), [q, kvu, kpe], [sp["q"], sp["kv"], sp["kp"]],
                   [sp["do"], sp["col"]],
                   [jax.ShapeDtypeStruct((S, H * V_DIM), BF16), jax.ShapeDtypeStruct((H, S, 1), F32)],
                   scratch=[pltpu.VMEM((hpd, blk, 1), F32), pltpu.VMEM((hpd, blk, 2 * V_DIM), F32)],
                   sem=("parallel", "arbitrary"), ride=ride, prefetch=[q_tab, k_tab])


def _attn_bwd(q, kvu, kpe, do, o, lse_row, cos_t, sin_t, H):
    S = kpe.shape[0]
    hpd = H // N_DEV
    blk = _tile(S, ATT_BLOCK)
    nb = S // blk
    q_tab, k_tab = _block_pairs(nb, by_key=True)

    def body(qt, kt, q_ref, kv_ref, kp_ref, do_ref, o_ref, lse_ref, c_ref, s_ref, dq_ref, dkv_ref, dkp_ref,
             dq_acc, dk_acc, dv_acc):
        t = pl.program_id(1)
        qi, ki = qt[t], kt[t]
        q_rows = pl.ds(pl.multiple_of(qi * blk, blk), blk)

        def step(diag):
            kp = kp_ref[...]
            ones = jnp.ones((8, V_DIM), BF16)
            for hh in range(hpd):
                k = _head_k(kv_ref, kp, hh)
                qv = q_ref[:, hh * HEAD_PAD:(hh + 1) * HEAD_PAD]
                cols = slice(hh * V_DIM, (hh + 1) * V_DIM)
                dov = do_ref[:, cols]
                doo = (dov.astype(F32) * o_ref[:, cols].astype(F32)).astype(BF16)
                delta = lax.dot_general(ones, doo, _DIMS["nt"], preferred_element_type=F32)[:1]
                st = lax.dot_general(k, qv, _DIMS["nt"], preferred_element_type=F32)
                pt = jnp.exp2(st * EXP2_SCALE - lse_ref[hh])
                if diag:
                    pt = jnp.where(_causal_mask(blk, transposed=True), pt, 0.0)
                dv_new = jnp.dot(pt.astype(BF16), dov, preferred_element_type=F32)
                dpt = lax.dot_general(_head_v(kv_ref, hh), dov, _DIMS["nt"], preferred_element_type=F32)
                dst = (pt * (dpt - delta)).astype(BF16)
                dk_new = jnp.dot(dst, qv, preferred_element_type=F32)
                dq_new = lax.dot_general(dst, k, _DIMS["tn"], preferred_element_type=F32)
                if diag:
                    dv_acc[hh] = dv_new
                    dk_acc[hh] = dk_new
                else:
                    dv_acc[hh] += dv_new
                    dk_acc[hh] += dk_new
                dq_acc[hh, q_rows] += dq_new

        @pl.when(t == 0)
        def _():
            dq_acc[...] = jnp.zeros_like(dq_acc)

        @pl.when(qi == ki)
        def _():
            step(True)
            for hh in range(hpd):
                c0 = hh * HEAD_PAD
                dq = dq_acc[hh, q_rows] * ATTN_SCALE
                dq_ref[:, c0:c0 + QK_NOPE_DIM] = dq[:, :QK_NOPE_DIM].astype(dq_ref.dtype)
                dq_ref[:, c0 + QK_NOPE_DIM:c0 + HEAD_PAD] = _rope(dq[:, QK_NOPE_DIM:], c_ref[...],
                                                                  -s_ref[...]).astype(dq_ref.dtype)

        @pl.when(qi > ki)
        def _():
            step(False)

        @pl.when(qi == nb - 1)
        def _():
            for hh in range(hpd):
                c0 = hh * HEAD_PAD
                dkv_ref[:, c0:c0 + QK_NOPE_DIM] = (dk_acc[hh, :, :QK_NOPE_DIM] * ATTN_SCALE).astype(dkv_ref.dtype)
                dkv_ref[:, c0 + QK_NOPE_DIM:c0 + HEAD_PAD] = dv_acc[hh].astype(dkv_ref.dtype)
                dkp_ref[hh] = dk_acc[hh, :, QK_NOPE_DIM:] * ATTN_SCALE

    sp = _attn_specs(S, blk, hpd)
    key_tab = pl.BlockSpec((blk, 128), lambda j, t, qt, kt: (kt[t], 0))
    grid_spec = pltpu.PrefetchScalarGridSpec(
        num_scalar_prefetch=2, grid=(N_DEV, q_tab.shape[0]),
        in_specs=[sp["q"], sp["kv"], sp["kp"], sp["do"], sp["do"], sp["row"], key_tab, key_tab],
        out_specs=(sp["kv"], sp["kv"], pl.BlockSpec((hpd, blk, 128), lambda j, t, qt, kt: (j, kt[t], 0))),
        scratch_shapes=[pltpu.VMEM((hpd, S, HEAD_PAD), F32), pltpu.VMEM((hpd, blk, HEAD_PAD), F32),
                        pltpu.VMEM((hpd, blk, V_DIM), F32)])
    return _pcall(
        body, name="attn_bwd", grid_spec=grid_spec,
        out_shape=(jax.ShapeDtypeStruct(q.shape, BF16), jax.ShapeDtypeStruct(kvu.shape, BF16),
                   jax.ShapeDtypeStruct((H, S, 128), F32)),
        compiler_params=_params(("parallel", "arbitrary")),
    )(q_tab, k_tab, q, kvu, kpe, do, o, lse_row, cos_t, sin_t)


def _mesh_pos():
    return lax.axis_index("x"), lax.axis_index("y"), lax.axis_index("c")


def _flip(pos, k):
    x, y, c = pos
    return (1 - x if k & 4 else x, 1 - y if k & 2 else y, 1 - c if k & 1 else c)


def _rank(pos):
    return 4 * pos[0] + 2 * pos[1] + pos[2]


def _copies(n, src_of, dst_refs, local_src_of, send_sems, recv_sems, local_sems):
    me = _mesh_pos()
    local = [pltpu.make_async_copy(local_src_of(a), dst_refs[a].at[_rank(me)], local_sems.at[a]) for a in range(n)]
    sends, arrivals = [], []
    for k in range(1, N_DEV):
        peer = _flip(me, k)
        for a in range(n):
            idx = a * (N_DEV - 1) + k - 1
            for into, group in ((me, sends), (peer, arrivals)):
                group.append(pltpu.make_async_remote_copy(
                    src_ref=src_of(a, peer), dst_ref=dst_refs[a].at[_rank(into)],
                    send_sem=send_sems.at[idx], recv_sem=recv_sems.at[idx],
                    device_id=peer, device_id_type=pl.DeviceIdType.MESH))
    return local, sends, arrivals


def _exchange_start(*args):
    local, sends, _ = _copies(*args)
    for cp in local + sends:
        cp.start()


def _exchange_wait(*args):
    local, sends, arrivals = _copies(*args)
    for cp in arrivals:
        cp.wait_recv()
    for cp in sends:
        cp.wait_send()
    for cp in local:
        cp.wait()


def _exchange(*args):
    _exchange_start(*args)
    _exchange_wait(*args)


def _sems(n):
    return [pltpu.SemaphoreType.DMA((n * (N_DEV - 1),)), pltpu.SemaphoreType.DMA((n * (N_DEV - 1),)),
            pltpu.SemaphoreType.DMA((n,))]


_ANY = pl.BlockSpec(memory_space=pl.ANY)


class _Ride:
    def __init__(self, kind, arrays):
        self.kind, self.arrays, self.n = kind, list(arrays), len(arrays)

    def out_shape(self):
        lead = () if self.kind == "scatter" else (N_DEV,)
        return [jax.ShapeDtypeStruct(lead + a.shape, a.dtype) for a in self.arrays]

    def _args(self, x_refs, out_refs, sems):
        if self.kind == "gather":
            return (self.n, lambda a, peer: x_refs[a], out_refs, lambda a: x_refs[a]) + tuple(sems)
        me = _mesh_pos()
        return (self.n, lambda a, peer: x_refs[a].at[_rank(peer)], out_refs, lambda a: x_refs[a].at[_rank(me)]) + tuple(sems)

    def start(self, x_refs, out_refs, sems):
        if self.kind == "gather2":
            _two_level_gather(x_refs, out_refs, *sems)[0]()
        else:
            _exchange_start(*self._args(x_refs, out_refs, sems))

    def wait(self, x_refs, out_refs, sems):
        if self.kind == "gather2":
            _two_level_gather(x_refs, out_refs, *sems)[1]()
        else:
            _exchange_wait(*self._args(x_refs, out_refs, sems))


def _pallas(body, name, grid, operands, in_specs, out_specs, out_shape, scratch=(), sem=None, ride=None, prefetch=()):
    n_pf, n_in, n_out, n_scr = len(prefetch), len(in_specs), len(out_specs), len(scratch)
    nr = ride.n if ride else 0

    def wrapped(*refs):
        refs = list(refs)
        pf, refs = refs[:n_pf], refs[n_pf:]
        ins, r_in = refs[:n_in], refs[n_in:n_in + nr]
        outs, r_out = refs[n_in + nr:n_in + nr + n_out], refs[n_in + nr + n_out:n_in + 2 * nr + n_out]
        rest = refs[n_in + 2 * nr + n_out:]
        scr, sems = rest[:n_scr], rest[n_scr:]
        if ride:
            first, last = None, None
            for d, size in enumerate(grid):
                i = pl.program_id(d)
                first = (i == 0) if first is None else jnp.logical_and(first, i == 0)
                last = (i == size - 1) if last is None else jnp.logical_and(last, i == size - 1)

            @pl.when(first)
            def _():
                ride.start(r_in, r_out, sems)

        body(pf + ins, outs, scr)
        if ride:
            @pl.when(last)
            def _():
                ride.wait(r_in, r_out, sems)

    grid_spec = pltpu.PrefetchScalarGridSpec(
        num_scalar_prefetch=n_pf, grid=grid,
        in_specs=list(in_specs) + [_ANY] * nr, out_specs=tuple(list(out_specs) + [_ANY] * nr),
        scratch_shapes=list(scratch) + (_sems(nr) if ride else []))
    sem = tuple(sem) if sem is not None and not ride else ("arbitrary",) * len(grid)
    return list(_pcall(
        wrapped, name=name, grid_spec=grid_spec,
        out_shape=tuple(list(out_shape) + (ride.out_shape() if ride else [])),
        compiler_params=_params(sem),
    )(*(list(prefetch) + list(operands) + (ride.arrays if ride else []))))


def _two_level_gather(x_refs, out_refs, send_sems, recv_sems, local_sems):
    n = len(x_refs)
    x, y, c = _mesh_pos()
    me, sibling = (x, y, c), (x, y, 1 - c)
    chips = [(1 - x, y), (x, 1 - y), (1 - x, 1 - y)]

    def copy(a, k, block, to, src=None):
        rows = out_refs[a].at[_rank(block)]
        return pltpu.make_async_remote_copy(
            src_ref=rows if src is None else src, dst_ref=rows,
            send_sem=send_sems.at[a * (N_DEV - 1) + k], recv_sem=recv_sems.at[a * (N_DEV - 1) + k],
            device_id=to, device_id_type=pl.DeviceIdType.MESH)

    def own_copies():
        mine = [pltpu.make_async_copy(x_refs[a], out_refs[a].at[_rank(me)], local_sems.at[a]) for a in range(n)]
        first = []
        for a in range(n):
            first.append(copy(a, 0, me, sibling, src=x_refs[a]))
            first += [copy(a, 1 + j, me, (*chip, c), src=x_refs[a]) for j, chip in enumerate(chips)]
        return mine, first

    def start():
        mine, first = own_copies()
        for cp in mine + first:
            cp.start()

    def finish():
        mine, first = own_copies()
        passed = []
        for j, chip in enumerate(chips):
            for a in range(n):
                copy(a, 1 + j, (*chip, c), me).wait_recv()
                passed.append(copy(a, 4 + j, (*chip, c), sibling))
                passed[-1].start()
        for a in range(n):
            copy(a, 0, sibling, me).wait_recv()
            for j, chip in enumerate(chips):
                copy(a, 4 + j, (*chip, 1 - c), me).wait_recv()
        for cp in first + passed:
            cp.wait_send()
        for cp in mine:
            cp.wait()

    return start, finish


def _all_gather(shards, name):
    n = len(shards)

    def body(*refs):
        start, finish = _two_level_gather(refs[:n], refs[n:2 * n], *refs[2 * n:])
        start()
        finish()

    return _pcall(
        body, name=name, in_specs=[_ANY] * n, out_specs=tuple([_ANY] * n),
        out_shape=tuple(jax.ShapeDtypeStruct((N_DEV,) + s.shape, s.dtype) for s in shards),
        scratch_shapes=_sems(n),
    )(*shards)


def _all_reduce_small(v):
    R, C = v.shape

    def body(x_ref, out_ref, buf_ref, send_sems, recv_sems, local_sems):
        _exchange(1, lambda a, peer: x_ref, [buf_ref], lambda a: x_ref, send_sems, recv_sems, local_sems)
        total = buf_ref[0]
        for j in range(1, N_DEV):
            total = total + buf_ref[j]
        out_ref[...] = total

    vm = pl.BlockSpec(memory_space=pltpu.VMEM)
    return _pcall(
        body, name="all_reduce_small", in_specs=[vm], out_specs=vm,
        out_shape=jax.ShapeDtypeStruct((R, C), F32),
        scratch_shapes=[pltpu.VMEM((N_DEV, R, C), F32)] + _sems(1),
    )(v)


def _sum_slots(layers, name):
    L = len(layers)
    shape = layers[0].shape[1:]
    C = shape[-1]
    R = layers[0].size // (N_DEV * C)
    tr = R
    while N_DEV * tr * C * 4 > (4 << 20) and tr % 32 == 0:
        tr //= 2

    def body(*refs):
        o_ref = refs[L]
        for li in range(L):
            @pl.when(pl.program_id(0) == li)
            def _():
                total = refs[li][0].astype(F32)
                for j in range(1, N_DEV):
                    total = total + refs[li][j].astype(F32)
                o_ref[0] = total

    in_specs = [pl.BlockSpec((N_DEV, tr, C), lambda l, i, li=li: (0, jnp.where(l == li, i, 0), 0)) for li in range(L)]
    return _pcall(
        body, name=name, grid=(L, R // tr),
        in_specs=in_specs,
        out_specs=pl.BlockSpec((1, tr, C), lambda l, i: (l, i, 0)),
        out_shape=jax.ShapeDtypeStruct((L, R, C), F32),
        compiler_params=_params(("parallel", "parallel")),
    )(*[p.reshape(N_DEV, R, C) for p in layers]).reshape((L,) + shape)


def _adamw_update(w, g, m, v):
    nm = ADAM_B1 * m + (1.0 - ADAM_B1) * g
    nv = ADAM_B2 * v + (1.0 - ADAM_B2) * (g * g)
    m_hat = nm / (1.0 - ADAM_B1 ** ADAM_STEP)
    v_hat = nv / (1.0 - ADAM_B2 ** ADAM_STEP)
    return -ADAM_LR * (m_hat / (jnp.sqrt(v_hat) + ADAM_EPS) + ADAM_WD * w), nm, nv


def _adamw_from_slots(w, layers, m, v, name, ride=None):
    L = len(layers)
    shape = w.shape
    C = shape[-1]
    R = w.size // (L * C)
    fits = [t for t in range(16, R + 1, 16) if R % t == 0 and N_DEV * t * C * 2 <= (1 << 20)]
    tr = max(fits) if fits else R

    def body(ins, outs, scratch):
        w_ref, m_ref, v_ref = ins[L:]
        g_ref, d_ref, nm_ref, nv_ref = outs
        for li in range(L):
            @pl.when(pl.program_id(0) == li)
            def _():
                g = ins[li][0].astype(F32)
                for j in range(1, N_DEV):
                    g = g + ins[li][j].astype(F32)
                g_ref[0] = g
                d_ref[0], nm_ref[0], nv_ref[0] = _adamw_update(w_ref[0], g, m_ref[0], v_ref[0])

    slot_specs = [pl.BlockSpec((N_DEV, tr, C), lambda l, i, li=li: (0, jnp.where(l == li, i, 0), 0)) for li in range(L)]
    blk = pl.BlockSpec((1, tr, C), lambda l, i: (l, i, 0))
    sds = jax.ShapeDtypeStruct((L, R, C), F32)
    outs = _pallas(body, name, (L, R // tr),
                   [p.reshape(N_DEV, R, C) for p in layers] + [t.reshape(L, R, C) for t in (w, m, v)],
                   slot_specs + [blk] * 3, [blk] * 4, [sds] * 4, sem=("parallel", "parallel"), ride=ride)
    return [o.reshape(shape) for o in outs[:4]] + outs[4:]


def _adamw(w, g, m, v, name):
    shape = w.shape
    C = shape[-1]
    R = w.size // C
    tr = R
    while tr * C * 4 > (1 << 20) and tr % 16 == 0:
        tr //= 2

    def body(w_ref, g_ref, m_ref, v_ref, d_ref, nm_ref, nv_ref):
        d_ref[...], nm_ref[...], nv_ref[...] = _adamw_update(w_ref[...], g_ref[...], m_ref[...], v_ref[...])

    blk = pl.BlockSpec((tr, C), lambda i: (i, 0))
    sds = jax.ShapeDtypeStruct((R, C), F32)
    outs = _pcall(
        body, name=name, grid=(R // tr,),
        in_specs=[blk] * 4, out_specs=(blk,) * 3, out_shape=(sds,) * 3,
        compiler_params=_params(("parallel",)),
    )(*(t.reshape(R, C) for t in (w, g, m, v)))
    return tuple(o.reshape(shape) for o in outs)


_REPLICATED = ("kv_in_norm", "kv_latent_norm", "attn_norm", "q_latent_norm", "ffn_norm", "final_norm")
_WEIGHTS = ("pool_norm", "pool_w", "pool_scale", "kv_in_norm", "w_kv_a", "kv_latent_norm", "w_kv_b", "attn_norm",
            "w_q_a", "q_latent_norm", "w_q_b", "w_o", "ffn_norm", "w_gate", "w_up", "w_down", "final_norm")


def _ffn_fwd(x, norm_g, wg, wu, wd, tag, gather=None):
    S, D = x.shape
    fs = wg.shape[1]
    h, ht = _rmsnorm(x, norm_g, "ffn_norm" + tag, transposed_too=True)
    g, u, a, *gathered = _ffn_up(h, wg, wu, "ffn_up" + tag, ride=gather)
    if gather is not None:
        wd = gathered[0].reshape(N_DEV * fs, D)
    y = _ffn_down(a, wd, x, "ffn_down" + tag)
    return y, (ht, g, u, a), wd, gathered


def _ffn_bwd(x, norm_g, wg, wu, wd, saved, dy, dyb, tag, with_bf16, scatter=None):
    S, D = x.shape
    ht, g, u, a = saved
    fs = g.shape[1]
    tk = _tile(S, 2 * ROW_TILE)
    nkb = S // tk
    dg, du, *got_rider = _ffn_dact(dyb, wd, g, u, "ffn_dact" + tag, ride=scatter)
    dwd = _mm("ffn_dwd" + tag, "tn", a, dyb, (fs, D, tk), (N_DEV, 1, nkb),
              lambda j, q, k: (j * nkb + k, 0), lambda j, q, k: (k, 0), lambda j, q, k: (j, 0), (N_DEV * fs, D), BF16)
    dwg, (got_dwd,) = _slots_dw_t("ffn_dwg" + tag, ht, dg, ride=_Ride("scatter", [dwd.reshape(N_DEV, fs, D)]))
    dwu, (got_dwg,) = _slots_dw_t("ffn_dwu" + tag, ht, du, ride=_Ride("scatter", [dwg.reshape(N_DEV, D, fs)]))
    dh, got_dwu = _ffn_dh(dg, du, wg, wu, S, "ffn_dh" + tag, ride=_Ride("scatter", [dwu.reshape(N_DEV, D, fs)]))
    outs = _rmsnorm_bwd(x, norm_g, dh, dy, "ffn_norm_bwd" + tag, with_bf16=with_bf16)
    return outs, got_rider, (got_dwg, got_dwu, got_dwd)


def kernel(x, positions, pool_norm, pool_w, pool_scale, kv_in_norm, w_kv_a, kv_latent_norm, w_kv_b, attn_norm, w_q_a, q_latent_norm, w_q_b, w_o, ffn_norm, w_gate, w_up, w_down, final_norm, loss_target, m_pool_norm, m_pool_w, m_pool_scale, m_kv_in_norm, m_w_kv_a, m_kv_latent_norm, m_w_kv_b, m_attn_norm, m_w_q_a, m_q_latent_norm, m_w_q_b, m_w_o, m_ffn_norm, m_w_gate, m_w_up, m_w_down, m_final_norm, v_pool_norm, v_pool_w, v_pool_scale, v_kv_in_norm, v_w_kv_a, v_kv_latent_norm, v_w_kv_b, v_attn_norm, v_w_q_a, v_q_latent_norm, v_w_q_b, v_w_o, v_ffn_norm, v_w_gate, v_w_up, v_w_down, v_final_norm):
    env = dict(locals())
    weights = {n: env[n] for n in _WEIGHTS}
    m_in = {n: env["m_" + n] for n in _WEIGHTS}
    v_in = {n: env["v_" + n] for n in _WEIGHTS}

    S, D = x.shape[1], x.shape[2]
    xs = x.reshape(S, D)
    target = loss_target.reshape(S, D)
    G, gws, gw = pool_w.shape[1:]
    Ds = w_kv_a.shape[0]
    R_kv, kvw = w_kv_b.shape
    hpd = kvw // (QK_NOPE_DIM + V_DIM)
    H = hpd * N_DEV
    R_q = w_q_a.shape[2]
    fs = w_gate.shape[2]
    bf = lambda t: t.astype(BF16)

    gains = jnp.concatenate([pool_norm, pool_scale], axis=0)
    g_pool_w, g_gains = _all_gather([bf(pool_w[0]), gains], "gather_first")
    wqb_pad = jnp.pad(w_q_b.reshape(R_q, hpd, QK_DIM), ((0, 0), (0, 0), (0, HEAD_PAD - QK_DIM))).reshape(R_q, hpd * HEAD_PAD)
    gather_mla = _Ride("gather", [bf(w_down[0]), bf(jnp.pad(w_kv_a, ((0, 0), (0, 128 - QK_ROPE_DIM)))), bf(w_kv_b),
                                  bf(w_q_a[0]), bf(wqb_pad), bf(w_o[0])])
    gather_ffn1 = _Ride("gather", [bf(w_gate[1]), bf(w_up[1]), bf(w_down[1])])

    wp = jnp.swapaxes(g_pool_w, 0, 1).reshape(G, gw, gw)
    gains_full = jnp.swapaxes(g_gains, 0, 1).reshape(2, D)
    g_pool_norm, g_pool_scale = gains_full[0:1], gains_full[1:2]

    g_kv_in = kv_in_norm.reshape(1, D)
    g_kv_lat = kv_latent_norm.reshape(1, R_kv)
    g_attn = attn_norm.reshape(1, D)
    g_q_lat = q_latent_norm.reshape(1, R_q)
    g_final = final_norm.reshape(1, D)

    half = QK_ROPE_DIM // 2
    inv_freq = ROPE_BASE ** (-jnp.arange(half, dtype=F32) / half)
    ang = positions.reshape(S).astype(F32)[:, None] * inv_freq
    cos, sin = jnp.cos(ang), jnp.sin(ang)
    zeros = jnp.zeros((S, 128 - QK_ROPE_DIM), F32)
    cos_t = jnp.concatenate([cos, cos, zeros], axis=1)
    sin_t = jnp.concatenate([-sin, sin, zeros], axis=1)

    diff, g_wg0 = _pool_pre(xs, g_pool_norm, ride=_Ride("gather2", [bf(w_gate[0])]))
    x1, g_wu0 = _pool_mix(diff, wp, g_pool_scale, xs, ride=_Ride("gather2", [bf(w_up[0])]))
    wg0, wu0 = g_wg0.reshape(N_DEV * D, fs), g_wu0.reshape(N_DEV * D, fs)
    x2, ffn0, wd0, (_, g_wkva, g_wkvb, g_wqa, g_wqb, g_wo) = _ffn_fwd(x1, ffn_norm[0:1], wg0, wu0, None, "0", gather_mla)
    wkva = g_wkva.reshape(D, R_kv + 128)
    wkvb = g_wkvb.reshape(N_DEV * R_kv, kvw)
    wqa = g_wqa.reshape(D, R_q)
    wqb = g_wqb.reshape(N_DEV * R_q, hpd * HEAD_PAD)
    wo = g_wo.reshape(H * V_DIM, D)

    hk, ha = _rmsnorm_pair(x2, g_kv_in, g_attn, "kv_attn_norm")
    kv = _mm_plain("kv_a", "nn", hk, wkva)
    ckv, kpe = _kv_mid(kv, g_kv_lat, cos_t, sin_t)
    kvu = _small_slots_out("kv_b", ckv, wkvb, BF16)

    ql = _mm_plain("q_a", "nn", ha, wqa)
    qn = _rmsnorm(ql, g_q_lat, "q_latent_norm")
    tab = pl.BlockSpec((_tile(S, ROW_TILE), 128), lambda i: (i, 0))
    qr = _small_slots_out("q_b", qn, wqb, BF16, epilogue=(_rope_heads, [cos_t, sin_t], [tab, tab]))
    o, lse, g_wg1, g_wu1, g_wd1 = _attn_fwd(qr, kvu, kpe, H, ride=gather_ffn1)
    wg1, wu1, wd1 = g_wg1.reshape(N_DEV * D, fs), g_wu1.reshape(N_DEV * D, fs), g_wd1.reshape(N_DEV * fs, D)
    x3 = _mm_plain("attn_out", "nn", o, wo, res=x2)
    x4, ffn1, _, _ = _ffn_fwd(x3, ffn_norm[1:2], wg1, wu1, wd1, "1")

    loss_part, dx4, dx4b, d_final = _loss_head(x4, g_final, target)

    (dx3, dx3b, d_ffn1), _, got_ffn1 = _ffn_bwd(x3, ffn_norm[1:2], wg1, wu1, wd1, ffn1, dx4, dx4b, "1", True)

    do = _mm_plain("attn_out_dx", "nt", dx3b, wo, out_dtype=BF16)
    dwo = _mm_plain("attn_out_dw", "tn", o, dx3b, out_dtype=BF16)
    dq, dkvu, dkpe = _attn_bwd(qr, kvu, kpe, do, o, lse.reshape(H, 1, S), cos_t, sin_t, H)

    dqn = _small_slots_in_nt("q_b_dx", dq, wqb, S)
    dwqb = _small_slots_dw("q_b_dw", qn, dq, BF16)
    dql, d_q_lat = _rmsnorm_bwd(ql, g_q_lat, dqn, None, "q_latent_norm_bwd", out_dtype=BF16)
    dha = _mm_plain("q_a_dx", "nt", dql, wqa)
    dwqa = _mm_plain("q_a_dw", "tn", ha, dql, out_dtype=BF16)

    dckv = _small_slots_in_nt("kv_b_dx", dkvu, wkvb, S)
    dwkvb = _small_slots_dw("kv_b_dw", ckv, dkvu, BF16)
    dkv, d_kv_lat = _kv_mid_bwd(kv, g_kv_lat, dckv, dkpe, cos_t, sin_t)
    dhk = _mm_plain("kv_a_dx", "nt", dkv, wkva)
    dwkva = _mm_plain("kv_a_dw", "tn", hk, dkv, out_dtype=BF16)
    dx2, dx2b, d_attn, d_kv_in = _rmsnorm_pair_bwd(x2, g_attn, dha, g_kv_in, dhk, dx3, "kv_attn_norm_bwd")

    scatter_mla = _Ride("scatter", [dwkva.reshape(N_DEV, Ds, R_kv + 128), dwkvb.reshape(N_DEV, R_kv, kvw),
                                    dwqa.reshape(N_DEV, Ds, R_q), dwqb.reshape(N_DEV, R_q, hpd * HEAD_PAD),
                                    dwo.reshape(N_DEV, H * V_DIM // N_DEV, D)])
    (dx1, d_ffn0), got_mla, got_ffn0 = _ffn_bwd(x1, ffn_norm[0:1], wg0, wu0, wd0, ffn0, dx2, dx2b, "0", False,
                                                  scatter=scatter_mla)

    ddiff, dwp, d_pool_scale = _pool_mix_bwd(diff, wp, g_pool_scale, dx1)
    grad_x, d_pool_norm = _pool_pre_bwd(xs, g_pool_norm, ddiff, dx1)

    d_gains = jnp.swapaxes(jnp.concatenate([d_pool_norm, d_pool_scale], axis=0).reshape(2, N_DEV, D // N_DEV), 0, 1)
    scatter_pool = _Ride("scatter", [jnp.swapaxes(dwp.reshape(G, N_DEV, gws, gw), 0, 1), d_gains])

    grads, delta_w, new_m, new_v = {}, {}, {}, {}
    ffn_slots = {n: [r0, r1] for n, r0, r1 in zip(("w_gate", "w_up", "w_down"), got_ffn0, got_ffn1)}
    got_pool = []
    for n, ride in zip(ffn_slots, (scatter_pool, None, None)):
        grads[n], delta_w[n], new_m[n], new_v[n], *got = _adamw_from_slots(weights[n], ffn_slots[n], m_in[n], v_in[n],
                                                                            "adamw_" + n, ride=ride)
        got_pool += got
    got_pool_w, got_gains = got_pool
    received = {n: [r] for n, r in zip(("w_kv_a", "w_kv_b", "w_q_a", "w_q_b", "w_o"), got_mla)}
    received.update(pool_w=[got_pool_w], gains=[got_gains])
    sums = {n: _sum_slots(r, "sum_" + n) for n, r in received.items()}
    grads.update({
        "pool_w": sums["pool_w"],
        "w_kv_a": sums["w_kv_a"][0, :, :R_kv + QK_ROPE_DIM],
        "w_kv_b": sums["w_kv_b"][0],
        "w_q_a": sums["w_q_a"],
        "w_q_b": sums["w_q_b"].reshape(R_q, hpd, HEAD_PAD)[:, :, :QK_DIM].reshape(1, R_q, hpd * QK_DIM),
        "w_o": sums["w_o"],
        "pool_norm": sums["gains"][0, 0:1],
        "pool_scale": sums["gains"][0, 1:2],
    })

    small = {"kv_in_norm": d_kv_in, "kv_latent_norm": d_kv_lat, "attn_norm": d_attn, "q_latent_norm": d_q_lat,
             "ffn_norm": jnp.concatenate([d_ffn0, d_ffn1], axis=0), "final_norm": d_final}
    small_packed = jnp.concatenate([small[n].reshape(-1) for n in _REPLICATED] + [loss_part.reshape(-1)])
    pad = (-small_packed.shape[0]) % (8 * 128)
    reduced = _all_reduce_small(jnp.pad(small_packed, (0, pad)).reshape(-1, 128)).reshape(-1)
    off = 0
    for n in _REPLICATED:
        grads[n] = reduced[off:off + weights[n].size].reshape(weights[n].shape)
        off += weights[n].size
    loss = reduced[off]

    for n in _WEIGHTS:
        if n not in ffn_slots:
            delta_w[n], new_m[n], new_v[n] = _adamw(weights[n], grads[n], m_in[n], v_in[n], "adamw_" + n)

    return (loss, grad_x.reshape(x.shape), *[grads[n] for n in _WEIGHTS], *[delta_w[n] for n in _WEIGHTS],
            *[new_m[n] for n in _WEIGHTS], *[new_v[n] for n in _WEIGHTS])
```

```python
import math

import jax
import jax.numpy as jnp
import numpy as np
from jax import lax
from jax.experimental import pallas as pl
from jax.experimental.pallas import tpu as pltpu

F32 = jnp.float32
BF16 = jnp.bfloat16

N_DEV = 8
POOL_WINDOWS = (2, 4, 8, 16)
POOL_HALO = 16
QK_NOPE_DIM = 128
QK_ROPE_DIM = 64
QK_DIM = QK_NOPE_DIM + QK_ROPE_DIM
HEAD_PAD = 256
V_DIM = 128
ROPE_BASE = 10000.0
ATTN_SCALE = 1.0 / math.sqrt(QK_DIM)
EXP2_SCALE = ATTN_SCALE * math.log2(math.e)
NORM_EPS = 1e-6
ADAM_LR = 0.001
ADAM_B1 = 0.9
ADAM_B2 = 0.999
ADAM_EPS = 1e-08
ADAM_WD = 0.01
ADAM_STEP = 10

ROW_TILE = 512
ATT_BLOCK = 1024
ATT_LOOKAHEAD = 1
ATT_ROW_CHUNK = 256
FFN_SLOTS = 2
MM_TN, MM_TK = 512, 512
VMEM_LIMIT = 48 * 1024 * 1024

_pcall = pl.pallas_call


def _params(sem):
    return pltpu.CompilerParams(dimension_semantics=sem, vmem_limit_bytes=VMEM_LIMIT)


def _tile(dim, default):
    if dim % default == 0:
        return default
    assert dim <= 2 * default, (dim, default)
    return dim


def _row_tile(rows, width):
    ts = _tile(rows, ROW_TILE)
    while ts * width * 4 > (2 << 20) and ts % 16 == 0:
        ts //= 2
    return ts


def _rms_scale(x):
    return lax.rsqrt(jnp.mean(x * x, axis=-1, keepdims=True) + NORM_EPS)


def _rms_bwd(x, g, dy):
    r = _rms_scale(x)
    xn = x * r
    u = dy * g
    dx = r * (u - xn * jnp.mean(u * xn, axis=-1, keepdims=True))
    return dx, dy * xn


def _swap_halves(x):
    lane = lax.broadcasted_iota(jnp.int32, x.shape, 1)
    return jnp.where(lane < QK_ROPE_DIM // 2, pltpu.roll(x, 128 - QK_ROPE_DIM // 2, 1), pltpu.roll(x, QK_ROPE_DIM // 2, 1))


def _rope(x, cos_t, sin_t):
    return x * cos_t + _swap_halves(x) * sin_t


def _rmsnorm(x, g, name, transposed_too=False):
    S, D = x.shape
    ts = _row_tile(S, D)

    def body(x_ref, g_ref, o_ref, *t_ref):
        xf = x_ref[...]
        y = xf * _rms_scale(xf) * g_ref[...]
        o_ref[...] = y.astype(o_ref.dtype)
        if transposed_too:
            t_ref[0][...] = y.T.astype(o_ref.dtype)

    row = pl.BlockSpec((ts, D), lambda i: (i, 0))
    outs = _pcall(
        body, name=name, grid=(S // ts,),
        in_specs=[row, pl.BlockSpec((1, D), lambda i: (0, 0))],
        out_specs=(row, pl.BlockSpec((D, ts), lambda i: (0, i))) if transposed_too else row,
        out_shape=((jax.ShapeDtypeStruct((S, D), BF16), jax.ShapeDtypeStruct((D, S), BF16)) if transposed_too
                   else jax.ShapeDtypeStruct((S, D), BF16)),
        compiler_params=_params(("parallel",)),
    )(x, g)
    return outs


def _rmsnorm_bwd(x, g, dy, res, name, out_dtype=F32, with_bf16=False):
    S, D = x.shape
    ts = _row_tile(S, D)
    has_res = res is not None

    def body(*refs):
        refs = list(refs)
        x_ref, g_ref, dy_ref = refs[:3]
        res_ref = refs[3] if has_res else None
        outs = refs[3 + has_res:]
        dx_ref, dg_ref = outs[0], outs[-1]
        dx, dgt = _rms_bwd(x_ref[...], g_ref[...], dy_ref[...].astype(F32))
        if has_res:
            dx = res_ref[...] + dx
        dx_ref[...] = dx.astype(dx_ref.dtype)
        if with_bf16:
            outs[1][...] = dx.astype(BF16)

        @pl.when(pl.program_id(0) == 0)
        def _():
            dg_ref[...] = jnp.zeros_like(dg_ref)

        dg_ref[...] += jnp.sum(dgt, axis=0, keepdims=True)

    row = pl.BlockSpec((ts, D), lambda i: (i, 0))
    vec = pl.BlockSpec((1, D), lambda i: (0, 0))
    out_specs = [row] + ([row] if with_bf16 else []) + [vec]
    out_shape = ([jax.ShapeDtypeStruct((S, D), out_dtype)] + ([jax.ShapeDtypeStruct((S, D), BF16)] if with_bf16 else [])
                 + [jax.ShapeDtypeStruct((1, D), F32)])
    return _pcall(
        body, name=name, grid=(S // ts,),
        in_specs=[row, vec, row] + ([row] if has_res else []),
        out_specs=tuple(out_specs), out_shape=tuple(out_shape),
        compiler_params=_params(("arbitrary",)),
    )(*([x, g, dy] + ([res] if has_res else [])))


def _rmsnorm_pair(x, g1, g2, name):
    S, D = x.shape
    ts = _row_tile(S, D)

    def body(x_ref, g1_ref, g2_ref, o1_ref, o2_ref):
        xf = x_ref[...]
        xn = xf * _rms_scale(xf)
        o1_ref[...] = (xn * g1_ref[...]).astype(o1_ref.dtype)
        o2_ref[...] = (xn * g2_ref[...]).astype(o2_ref.dtype)

    row = pl.BlockSpec((ts, D), lambda i: (i, 0))
    vec = pl.BlockSpec((1, D), lambda i: (0, 0))
    sds = jax.ShapeDtypeStruct((S, D), BF16)
    return _pcall(
        body, name=name, grid=(S // ts,),
        in_specs=[row, vec, vec], out_specs=(row, row), out_shape=(sds, sds),
        compiler_params=_params(("parallel",)),
    )(x, g1, g2)


def _rmsnorm_pair_bwd(x, g1, dy1, g2, dy2, res, name):
    S, D = x.shape
    ts = _row_tile(S, D)

    def body(x_ref, g1_ref, dy1_ref, g2_ref, dy2_ref, res_ref, dx_ref, dxb_ref, dg1_ref, dg2_ref):
        xf = x_ref[...]
        dx1, dgt1 = _rms_bwd(xf, g1_ref[...], dy1_ref[...])
        dx2, dgt2 = _rms_bwd(xf, g2_ref[...], dy2_ref[...])
        dx = (res_ref[...] + dx1) + dx2
        dx_ref[...] = dx
        dxb_ref[...] = dx.astype(BF16)

        @pl.when(pl.program_id(0) == 0)
        def _():
            dg1_ref[...] = jnp.zeros_like(dg1_ref)
            dg2_ref[...] = jnp.zeros_like(dg2_ref)

        dg1_ref[...] += jnp.sum(dgt1, axis=0, keepdims=True)
        dg2_ref[...] += jnp.sum(dgt2, axis=0, keepdims=True)

    row = pl.BlockSpec((ts, D), lambda i: (i, 0))
    vec = pl.BlockSpec((1, D), lambda i: (0, 0))
    return _pcall(
        body, name=name, grid=(S // ts,),
        in_specs=[row, vec, row, vec, row, row], out_specs=(row, row, vec, vec),
        out_shape=(jax.ShapeDtypeStruct((S, D), F32), jax.ShapeDtypeStruct((S, D), BF16),
                   jax.ShapeDtypeStruct((1, D), F32), jax.ShapeDtypeStruct((1, D), F32)),
        compiler_params=_params(("arbitrary",)),
    )(x, g1, dy1, g2, dy2, res)


def _pool_pre(x, g, ride=None):
    S, D = x.shape
    ts = _row_tile(S, D)
    gw = D // len(POOL_WINDOWS)
    hb = ts // POOL_HALO

    def body(ins, outs, scratch):
        (x_ref, halo_ref, g_ref), (o_ref,) = ins, outs
        i = pl.program_id(0)
        xx = jnp.concatenate([halo_ref[...], x_ref[...]], axis=0)
        h = xx * _rms_scale(xx) * g_ref[...]
        t = i * ts - POOL_HALO + lax.broadcasted_iota(jnp.int32, (ts + POOL_HALO, 1), 0)
        h = jnp.where(t >= 0, h, 0.0)
        tf = t[POOL_HALO:].astype(F32)
        for gi, w in enumerate(POOL_WINDOWS):
            hg = h[:, gi * gw:(gi + 1) * gw]
            acc, span = hg, 1
            while span < w:
                acc = acc + pltpu.roll(acc, span, 0)
                span *= 2
            count = jnp.minimum(tf + 1.0, float(w))
            diff = acc[POOL_HALO:] / count - hg[POOL_HALO:]
            o_ref[:, gi * gw:(gi + 1) * gw] = diff.astype(o_ref.dtype)

    return _pallas(body, "pool_pre", (S // ts,), [x, x, g],
                   [pl.BlockSpec((ts, D), lambda i: (i, 0)),
                    pl.BlockSpec((POOL_HALO, D), lambda i: (jnp.maximum(i * hb - 1, 0), 0)),
                    pl.BlockSpec((1, D), lambda i: (0, 0))],
                   [pl.BlockSpec((ts, D), lambda i: (i, 0))], [jax.ShapeDtypeStruct((S, D), BF16)],
                   sem=("parallel",), ride=ride)


def _pool_pre_bwd(x, g, dd, res):
    S, D = x.shape
    ts = _row_tile(S, D)
    gw = D // len(POOL_WINDOWS)
    hb = ts // POOL_HALO
    last_halo = S // POOL_HALO - 1

    def body(x_ref, g_ref, dd_ref, halo_ref, res_ref, dx_ref, dg_ref):
        i = pl.program_id(0)
        ee = jnp.concatenate([dd_ref[...], halo_ref[...]], axis=0)
        t = i * ts + lax.broadcasted_iota(jnp.int32, (ts + POOL_HALO, 1), 0)
        ee = jnp.where(t < S, ee, 0.0)
        tf = t.astype(F32)
        n = ts + POOL_HALO
        for gi, w in enumerate(POOL_WINDOWS):
            eg = ee[:, gi * gw:(gi + 1) * gw]
            acc, span = eg / jnp.minimum(tf + 1.0, float(w)), 1
            while span < w:
                acc = acc + pltpu.roll(acc, n - span, 0)
                span *= 2
            dx_ref[:, gi * gw:(gi + 1) * gw] = acc[:ts] - eg[:ts]
        dx, dgt = _rms_bwd(x_ref[...], g_ref[...], dx_ref[...])
        dx_ref[...] = res_ref[...] + dx

        @pl.when(i == 0)
        def _():
            dg_ref[...] = jnp.zeros_like(dg_ref)

        dg_ref[...] += jnp.sum(dgt, axis=0, keepdims=True)

    row = pl.BlockSpec((ts, D), lambda i: (i, 0))
    vec = pl.BlockSpec((1, D), lambda i: (0, 0))
    return _pcall(
        body, name="pool_pre_bwd", grid=(S // ts,),
        in_specs=[row, vec, row,
                  pl.BlockSpec((POOL_HALO, D), lambda i: (jnp.minimum((i + 1) * hb, last_halo), 0)), row],
        out_specs=(row, vec),
        out_shape=(jax.ShapeDtypeStruct((S, D), F32), jax.ShapeDtypeStruct((1, D), F32)),
        compiler_params=_params(("arbitrary",)),
    )(x, g, dd, dd, res)


def _pool_mix(diff, w, scale, x, ride=None):
    S, D = x.shape
    G, gw, _ = w.shape
    ts = _tile(S, ROW_TILE)

    def body(ins, outs, scratch):
        (d_ref, w_ref, s_ref, x_ref), (o_ref,) = ins, outs
        mo = jnp.dot(d_ref[...], w_ref[0], preferred_element_type=F32)
        o_ref[...] = x_ref[...] + mo * s_ref[...]

    blk = pl.BlockSpec((ts, gw), lambda i, g: (i, g))
    return _pallas(body, "pool_mix", (S // ts, G), [diff, w, scale, x],
                   [blk, pl.BlockSpec((1, gw, gw), lambda i, g: (g, 0, 0)), pl.BlockSpec((1, gw), lambda i, g: (0, g)), blk],
                   [blk], [jax.ShapeDtypeStruct((S, D), F32)], sem=("parallel", "parallel"), ride=ride)


def _pool_mix_bwd(diff, w, scale, dy):
    S, D = dy.shape
    G, gw, _ = w.shape
    ts = _tile(S, ROW_TILE)

    def body(d_ref, w_ref, s_ref, dy_ref, dd_ref, dw_ref, ds_ref):
        i = pl.program_id(1)
        d = d_ref[...]
        dy = dy_ref[...]
        mo = jnp.dot(d, w_ref[0], preferred_element_type=F32)
        dmo = (dy * s_ref[...]).astype(BF16)
        dd_ref[...] = lax.dot_general(dmo, w_ref[0], (((1,), (1,)), ((), ())), preferred_element_type=F32)

        @pl.when(i == 0)
        def _():
            dw_ref[...] = jnp.zeros_like(dw_ref)
            ds_ref[...] = jnp.zeros_like(ds_ref)

        dw_ref[0] += lax.dot_general(d, dmo, (((0,), (0,)), ((), ())), preferred_element_type=F32)
        ds_ref[...] += jnp.sum(dy * mo, axis=0, keepdims=True)

    blk = pl.BlockSpec((ts, gw), lambda g, i: (i, g))
    wspec = pl.BlockSpec((1, gw, gw), lambda g, i: (g, 0, 0))
    vec = pl.BlockSpec((1, gw), lambda g, i: (0, g))
    return _pcall(
        body, name="pool_mix_bwd", grid=(G, S // ts),
        in_specs=[blk, wspec, vec, blk],
        out_specs=(blk, wspec, vec),
        out_shape=(jax.ShapeDtypeStruct((S, D), F32), jax.ShapeDtypeStruct((G, gw, gw), F32),
                   jax.ShapeDtypeStruct((1, D), F32)),
        compiler_params=_params(("parallel", "arbitrary")),
    )(diff, w, scale, dy)


def _loss_head(x, g, target):
    S, D = x.shape
    ts = _row_tile(S, D)

    def body(x_ref, g_ref, t_ref, loss_ref, dx_ref, dxb_ref, dg_ref):
        xf = x_ref[...]
        gv = g_ref[...]
        err = xf * _rms_scale(xf) * gv - t_ref[...]
        dx, dgt = _rms_bwd(xf, gv, err * (1.0 / D))
        dx_ref[...] = dx
        dxb_ref[...] = dx.astype(BF16)

        @pl.when(pl.program_id(0) == 0)
        def _():
            loss_ref[...] = jnp.zeros_like(loss_ref)
            dg_ref[...] = jnp.zeros_like(dg_ref)

        part = 0.5 * jnp.sum(jnp.sum(err * err, axis=-1, keepdims=True) * (1.0 / D), axis=0, keepdims=True)
        loss_ref[...] += jnp.broadcast_to(part, loss_ref.shape)
        dg_ref[...] += jnp.sum(dgt, axis=0, keepdims=True)

    row = pl.BlockSpec((ts, D), lambda i: (i, 0))
    vec = pl.BlockSpec((1, D), lambda i: (0, 0))
    return _pcall(
        body, name="loss_head", grid=(S // ts,),
        in_specs=[row, vec, row],
        out_specs=(pl.BlockSpec((1, 128), lambda i: (0, 0)), row, row, vec),
        out_shape=(jax.ShapeDtypeStruct((1, 128), F32), jax.ShapeDtypeStruct((S, D), F32),
                   jax.ShapeDtypeStruct((S, D), BF16), jax.ShapeDtypeStruct((1, D), F32)),
        compiler_params=_params(("arbitrary",)),
    )(x, g, target)


_DIMS = {"nn": (((1,), (0,)), ((), ())), "nt": (((1,), (1,)), ((), ())), "tn": (((0,), (0,)), ((), ()))}


def _mm(name, mode, a, b, tiles, grid, a_map, b_map, o_map, out_shape, out_dtype=F32, res=None, ride=None):
    tm, tn, tk = tiles
    nk = grid[-1]
    has_res = res is not None
    dn = _DIMS[mode]

    def body(ins, outs, scratch):
        o_ref = outs[0]

        def product():
            return lax.dot_general(ins[0][...].astype(BF16), ins[1][...].astype(BF16), dn, preferred_element_type=F32)

        def finish(out):
            if has_res:
                out = ins[2][...] + out
            o_ref[...] = out.astype(o_ref.dtype)

        if nk == 1:
            finish(product())
            return
        acc_ref = scratch[0]
        k = pl.program_id(2)

        @pl.when(k == 0)
        def _():
            acc_ref[...] = jnp.zeros_like(acc_ref)

        acc_ref[...] += product()

        @pl.when(k == nk - 1)
        def _():
            finish(acc_ref[...])

    a_spec = pl.BlockSpec((tk, tm) if mode == "tn" else (tm, tk), a_map)
    b_spec = pl.BlockSpec((tn, tk) if mode == "nt" else (tk, tn), b_map)
    o_spec = pl.BlockSpec((tm, tn), o_map)
    operands, in_specs = [a, b], [a_spec, b_spec]
    if has_res:
        operands.append(res)
        in_specs.append(o_spec)
    outs = _pallas(body, name, grid, operands, in_specs, [o_spec], [jax.ShapeDtypeStruct(out_shape, out_dtype)],
                   scratch=[pltpu.VMEM((tm, tn), F32)] if nk > 1 else [],
                   sem=("parallel", "parallel", "arbitrary"), ride=ride)
    return (outs[0], outs[1:]) if ride else outs[0]


def _mm_plain(name, mode, a, b, out_dtype=F32, res=None):
    if mode == "tn":
        (K, M), N = a.shape, b.shape[1]
    elif mode == "nt":
        (M, K), N = a.shape, b.shape[0]
    else:
        (M, K), N = a.shape, b.shape[1]
    if mode == "tn":
        tm, tn, tk = _tile(M, 2 * MM_TN), _tile(N, 2 * MM_TN), _tile(K, ROW_TILE)
    else:
        tm = _tile(M, ROW_TILE)
        tk = K if K <= 4 * MM_TK else _tile(K, MM_TK)
        tn = N if N * tk * 2 <= (8 << 20) else _tile(N, MM_TN)
    a_map = (lambda i, j, k: (k, i)) if mode == "tn" else (lambda i, j, k: (i, k))
    b_map = (lambda i, j, k: (j, k)) if mode == "nt" else (lambda i, j, k: (k, j))
    return _mm(name, mode, a, b, (tm, tn, tk), (M // tm, N // tn, K // tk), a_map, b_map, lambda i, j, k: (i, j),
               (M, N), out_dtype, res)


def _small_slots_out(name, a, w_slots, out_dtype, epilogue=None):
    S, K = a.shape
    n = w_slots.shape[1]
    tm = _tile(S, ROW_TILE)
    extra = epilogue[1] if epilogue else []

    def body(*refs):
        a_ref, w_ref, o_ref = refs[0], refs[1], refs[-1]
        av = a_ref[...].astype(BF16)
        tables = [r[...] for r in refs[2:-1]]
        for j in range(N_DEV):
            out = jnp.dot(av, w_ref[j], preferred_element_type=F32)
            if epilogue:
                out = epilogue[0](out, *tables)
            o_ref[j] = out.astype(o_ref.dtype)

    return _pcall(
        body, name=name, grid=(S // tm,),
        in_specs=[pl.BlockSpec((tm, K), lambda i: (i, 0)), pl.BlockSpec((N_DEV, K, n), lambda i: (0, 0, 0))]
        + (list(epilogue[2]) if epilogue else []),
        out_specs=pl.BlockSpec((N_DEV, tm, n), lambda i: (0, i, 0)),
        out_shape=jax.ShapeDtypeStruct((N_DEV, S, n), out_dtype),
        compiler_params=_params(("parallel",)),
    )(a, w_slots.reshape(N_DEV, K, n), *extra).reshape(N_DEV * S, n)


def _small_slots_in_nt(name, a_slots, w_slots, S):
    n = a_slots.shape[1]
    K = w_slots.shape[0] // N_DEV
    tm = _tile(S, ROW_TILE)

    def body(a_ref, w_ref, o_ref):
        total = lax.dot_general(a_ref[0], w_ref[0], _DIMS["nt"], preferred_element_type=F32)
        for j in range(1, N_DEV):
            total += lax.dot_general(a_ref[j], w_ref[j], _DIMS["nt"], preferred_element_type=F32)
        o_ref[...] = total

    return _pcall(
        body, name=name, grid=(S // tm,),
        in_specs=[pl.BlockSpec((N_DEV, tm, n), lambda i: (0, i, 0)), pl.BlockSpec((N_DEV, K, n), lambda i: (0, 0, 0))],
        out_specs=pl.BlockSpec((tm, K), lambda i: (i, 0)),
        out_shape=jax.ShapeDtypeStruct((S, K), F32),
        compiler_params=_params(("parallel",)),
    )(a_slots.reshape(N_DEV, S, n), w_slots.reshape(N_DEV, K, n))


def _small_slots_dw(name, a, d_slots, out_dtype):
    S, K = a.shape
    n = d_slots.shape[1]
    tk = _tile(S, ROW_TILE)
    nkb = S // tk

    def body(a_ref, d_ref, o_ref, acc_ref):
        k = pl.program_id(0)

        @pl.when(k == 0)
        def _():
            acc_ref[...] = jnp.zeros_like(acc_ref)

        at = a_ref[...].astype(F32).T.astype(BF16)
        for j in range(N_DEV):
            acc_ref[j] += jnp.dot(at, d_ref[j], preferred_element_type=F32)

        @pl.when(k == nkb - 1)
        def _():
            o_ref[...] = acc_ref[...].astype(o_ref.dtype)

    return _pcall(
        body, name=name, grid=(nkb,),
        in_specs=[pl.BlockSpec((tk, K), lambda k: (k, 0)), pl.BlockSpec((N_DEV, tk, n), lambda k: (0, k, 0))],
        out_specs=pl.BlockSpec((N_DEV, K, n), lambda k: (0, 0, 0)),
        out_shape=jax.ShapeDtypeStruct((N_DEV, K, n), out_dtype),
        scratch_shapes=[pltpu.VMEM((N_DEV, K, n), F32)],
        compiler_params=_params(("arbitrary",)),
    )(a, d_slots.reshape(N_DEV, S, n)).reshape(N_DEV * K, n)


def _sigmoid(x):
    return 1.0 / (1.0 + jnp.exp(-x))


def _ffn_up(h, wg, wu, name, ride=None):
    S, D = h.shape
    fs = wg.shape[1]
    tm = _tile(S, ROW_TILE)
    sp = FFN_SLOTS

    def body(ins, outs, scratch):
        h_ref, wg_ref, wu_ref = ins
        g_ref, u_ref, a_ref = outs
        hv = h_ref[...]
        for s in range(sp):
            g = jnp.dot(hv, wg_ref[s], preferred_element_type=F32)
            u = jnp.dot(hv, wu_ref[s], preferred_element_type=F32)
            g_ref[s] = g.astype(g_ref.dtype)
            u_ref[s] = u.astype(u_ref.dtype)
            a_ref[s] = (g * _sigmoid(g) * u).astype(a_ref.dtype)

    w_spec = pl.BlockSpec((sp, D, fs), lambda i, j: (j, 0, 0))
    o_spec = pl.BlockSpec((sp, tm, fs), lambda i, j: (j, i, 0))
    sds = jax.ShapeDtypeStruct((N_DEV, S, fs), BF16)
    g, u, a, *rest = _pallas(body, name, (S // tm, N_DEV // sp), [h, wg.reshape(N_DEV, D, fs), wu.reshape(N_DEV, D, fs)],
                             [pl.BlockSpec((tm, D), lambda i, j: (i, 0)), w_spec, w_spec], [o_spec, o_spec, o_spec],
                             [sds, sds, sds], sem=("parallel", "arbitrary"), ride=ride)
    return [t.reshape(N_DEV * S, fs) for t in (g, u, a)] + rest


def _ffn_dact(dy, wd, g, u, name, ride=None):
    S, D = dy.shape
    fs = g.shape[1]
    tm = _tile(S, ROW_TILE)
    sp = FFN_SLOTS

    def body(ins, outs, scratch):
        dy_ref, wd_ref, g_ref, u_ref = ins
        dg_ref, du_ref = outs
        dy = dy_ref[...]
        for s in range(sp):
            da = lax.dot_general(dy, wd_ref[s], _DIMS["nt"], preferred_element_type=F32)
            g, u = g_ref[s].astype(F32), u_ref[s].astype(F32)
            sig = _sigmoid(g)
            dg_ref[s] = (da * u * (sig * (1.0 + g * (1.0 - sig)))).astype(dg_ref.dtype)
            du_ref[s] = (da * (g * sig)).astype(du_ref.dtype)

    o_spec = pl.BlockSpec((sp, tm, fs), lambda j, i: (j, i, 0))
    sds = jax.ShapeDtypeStruct((N_DEV, S, fs), BF16)
    dg, du, *rest = _pallas(body, name, (N_DEV // sp, S // tm),
                            [dy, wd.reshape(N_DEV, fs, D), g.reshape(N_DEV, S, fs), u.reshape(N_DEV, S, fs)],
                            [pl.BlockSpec((tm, D), lambda j, i: (i, 0)), pl.BlockSpec((sp, fs, D), lambda j, i: (j, 0, 0)),
                             o_spec, o_spec], [o_spec, o_spec], [sds, sds], sem=("parallel", "arbitrary"), ride=ride)
    return [dg.reshape(N_DEV * S, fs), du.reshape(N_DEV * S, fs)] + rest


def _ffn_down(a, wd, x, name):
    S, D = x.shape
    fs = a.shape[1]
    tm = _tile(S, ROW_TILE)
    sp = FFN_SLOTS
    nj = N_DEV // sp

    def body(a_ref, w_ref, x_ref, o_ref, acc_ref):
        j = pl.program_id(1)

        @pl.when(j == 0)
        def _():
            acc_ref[...] = x_ref[...]

        part = jnp.dot(a_ref[0], w_ref[0], preferred_element_type=F32)
        for s in range(1, sp):
            part += jnp.dot(a_ref[s], w_ref[s], preferred_element_type=F32)
        acc_ref[...] += part

        @pl.when(j == nj - 1)
        def _():
            o_ref[...] = acc_ref[...]

    row = pl.BlockSpec((tm, D), lambda i, j: (i, 0))
    return _pcall(
        body, name=name, grid=(S // tm, nj),
        in_specs=[pl.BlockSpec((sp, tm, fs), lambda i, j: (j, i, 0)), pl.BlockSpec((sp, fs, D), lambda i, j: (j, 0, 0)), row],
        out_specs=row, out_shape=jax.ShapeDtypeStruct((S, D), F32),
        scratch_shapes=[pltpu.VMEM((tm, D), F32)],
        compiler_params=_params(("parallel", "arbitrary")),
    )(a.reshape(N_DEV, S, fs), wd.reshape(N_DEV, fs, D), x)


def _ffn_dh(dg, du, wg, wu, S, name, ride=None):
    fs = dg.shape[1]
    D = wg.shape[0] // N_DEV
    tm = _tile(S, ROW_TILE)
    sp = FFN_SLOTS
    nj = N_DEV // sp

    def body(ins, outs, scratch):
        dg_ref, du_ref, wg_ref, wu_ref = ins
        o_ref, acc_ref = outs[0], scratch[0]
        j = pl.program_id(1)

        @pl.when(j == 0)
        def _():
            acc_ref[...] = jnp.zeros_like(acc_ref)

        part = None
        for s in range(sp):
            for d_ref, w_ref in ((dg_ref, wg_ref), (du_ref, wu_ref)):
                term = lax.dot_general(d_ref[s], w_ref[s], _DIMS["nt"], preferred_element_type=F32)
                part = term if part is None else part + term
        acc_ref[...] += part

        @pl.when(j == nj - 1)
        def _():
            o_ref[...] = acc_ref[...]

    d_spec = pl.BlockSpec((sp, tm, fs), lambda i, j: (j, i, 0))
    w_spec = pl.BlockSpec((sp, D, fs), lambda i, j: (j, 0, 0))
    return _pallas(body, name, (S // tm, nj),
                   [dg.reshape(N_DEV, S, fs), du.reshape(N_DEV, S, fs), wg.reshape(N_DEV, D, fs), wu.reshape(N_DEV, D, fs)],
                   [d_spec, d_spec, w_spec, w_spec], [pl.BlockSpec((tm, D), lambda i, j: (i, 0))],
                   [jax.ShapeDtypeStruct((S, D), F32)], scratch=[pltpu.VMEM((tm, D), F32)],
                   sem=("parallel", "arbitrary"), ride=ride)


def _slots_dw_t(name, at, d_slots, ride=None):
    K, S = at.shape
    n = d_slots.shape[1]
    tk = _tile(S, 4 * ROW_TILE)
    nkb = S // tk
    return _mm(name, "nn", at, d_slots, (K, n, tk), (N_DEV, 1, nkb),
               lambda j, q, k: (0, k), lambda j, q, k: (j * nkb + k, 0), lambda j, q, k: (j, 0),
               (N_DEV * K, n), BF16, ride=ride)


def _rope_heads(q, cos_t, sin_t):
    parts = []
    for c0 in range(0, q.shape[1], HEAD_PAD):
        parts += [q[:, c0:c0 + QK_NOPE_DIM], _rope(q[:, c0 + QK_NOPE_DIM:c0 + HEAD_PAD], cos_t, sin_t)]
    return jnp.concatenate(parts, axis=1)


def _kv_mid(kv, g, cos_t, sin_t):
    S, W = kv.shape
    R = W - 128
    ts = _tile(S, ROW_TILE)

    def body(kv_ref, g_ref, c_ref, s_ref, c_out, k_out):
        cf = kv_ref[:, :R]
        c_out[...] = (cf * _rms_scale(cf) * g_ref[...]).astype(c_out.dtype)
        k_out[...] = _rope(kv_ref[:, R:], c_ref[...], s_ref[...]).astype(k_out.dtype)

    tab = pl.BlockSpec((ts, 128), lambda i: (i, 0))
    return _pcall(
        body, name="kv_mid", grid=(S // ts,),
        in_specs=[pl.BlockSpec((ts, W), lambda i: (i, 0)), pl.BlockSpec((1, R), lambda i: (0, 0)), tab, tab],
        out_specs=(pl.BlockSpec((ts, R), lambda i: (i, 0)), tab),
        out_shape=(jax.ShapeDtypeStruct((S, R), BF16), jax.ShapeDtypeStruct((S, 128), BF16)),
        compiler_params=_params(("parallel",)),
    )(kv, g, cos_t, sin_t)


def _kv_mid_bwd(kv, g, dc, dkpe, cos_t, sin_t):
    S, W = kv.shape
    R = W - 128
    H = dkpe.shape[0]
    ts = _row_tile(S, H * 128)

    def body(kv_ref, g_ref, dc_ref, dk_ref, c_ref, s_ref, dkv_ref, dg_ref):
        dx, dgt = _rms_bwd(kv_ref[:, :R], g_ref[...], dc_ref[...])
        dkv_ref[:, :R] = dx.astype(dkv_ref.dtype)
        dk = dk_ref[0]
        for h in range(1, H):
            dk = dk + dk_ref[h]
        dkv_ref[:, R:] = _rope(dk, c_ref[...], -s_ref[...]).astype(dkv_ref.dtype)

        @pl.when(pl.program_id(0) == 0)
        def _():
            dg_ref[...] = jnp.zeros_like(dg_ref)

        dg_ref[...] += jnp.sum(dgt, axis=0, keepdims=True)

    tab = pl.BlockSpec((ts, 128), lambda i: (i, 0))
    vec = pl.BlockSpec((1, R), lambda i: (0, 0))
    return _pcall(
        body, name="kv_mid_bwd", grid=(S // ts,),
        in_specs=[pl.BlockSpec((ts, W), lambda i: (i, 0)), vec, pl.BlockSpec((ts, R), lambda i: (i, 0)),
                  pl.BlockSpec((H, ts, 128), lambda i: (0, i, 0)), tab, tab],
        out_specs=(pl.BlockSpec((ts, W), lambda i: (i, 0)), vec),
        out_shape=(jax.ShapeDtypeStruct((S, W), BF16), jax.ShapeDtypeStruct((1, R), F32)),
        compiler_params=_params(("arbitrary",)),
    )(kv, g, dc, dkpe, cos_t, sin_t)


def _block_pairs(nb, by_key):
    pairs = ([(qi, ki) for ki in range(nb) for qi in range(ki, nb)] if by_key
             else [(qi, ki) for qi in range(nb) for ki in range(qi + 1)])
    return jnp.asarray(np.array([p[0] for p in pairs], np.int32)), jnp.asarray(np.array([p[1] for p in pairs], np.int32))


def _causal_mask(blk, transposed=False, rows=None, row0=0):
    shape = (blk if rows is None else rows, blk)
    r = lax.broadcasted_iota(jnp.int32, shape, 0) + row0
    c = lax.broadcasted_iota(jnp.int32, shape, 1)
    return (r <= c) if transposed else (c <= r)


def _attn_specs(S, blk, hpd):
    nb = S // blk
    w = hpd * HEAD_PAD
    return dict(
        q=pl.BlockSpec((blk, w), lambda j, t, qt, kt: (j * nb + qt[t], 0)),
        kv=pl.BlockSpec((blk, w), lambda j, t, qt, kt: (j * nb + kt[t], 0)),
        kp=pl.BlockSpec((blk, 128), lambda j, t, qt, kt: (kt[t], 0)),
        do=pl.BlockSpec((blk, hpd * V_DIM), lambda j, t, qt, kt: (qt[t], j)),
        col=pl.BlockSpec((hpd, blk, 1), lambda j, t, qt, kt: (j, qt[t], 0)),
        row=pl.BlockSpec((hpd, 1, blk), lambda j, t, qt, kt: (j, 0, qt[t])))


def _head_k(kv_ref, kp, hh):
    return jnp.concatenate([kv_ref[:, hh * HEAD_PAD:hh * HEAD_PAD + QK_NOPE_DIM], kp], axis=1)


def _head_v(kv_ref, hh):
    return kv_ref[:, hh * HEAD_PAD + QK_NOPE_DIM:(hh + 1) * HEAD_PAD]


def _attn_fwd(q, kvu, kpe, H, ride=None):
    S = kpe.shape[0]
    hpd = H // N_DEV
    blk = _tile(S, ATT_BLOCK)
    q_tab, k_tab = _block_pairs(S // blk, by_key=False)
    rc = min(blk, ATT_ROW_CHUNK)

    def body(ins, outs, scratch):
        qt, kt, q_ref, kv_ref, kp_ref = ins
        o_ref, lse_ref = outs
        m_ref, acc_ref = scratch
        t = pl.program_id(1)
        qi, ki = qt[t], kt[t]

        @pl.when(ki == 0)
        def _():
            m_ref[...] = jnp.full_like(m_ref, -jnp.inf)
            acc_ref[...] = jnp.zeros_like(acc_ref)

        def step(masked):
            kp = kp_ref[...]
            ones = jnp.ones((blk, 128), BF16)
            ks = [_head_k(kv_ref, kp, hh) for hh in range(hpd)]
            v_exts = [jnp.concatenate([_head_v(kv_ref, hh), ones], axis=1) for hh in range(hpd)]
            chains = [(hh, r0) for r0 in range(0, blk, rc) for hh in range(hpd)]

            def scores(hh, r0):
                return lax.dot_general(q_ref[r0:r0 + rc, hh * HEAD_PAD:(hh + 1) * HEAD_PAD], ks[hh], _DIMS["nt"],
                                       preferred_element_type=F32)

            ahead = [scores(*ch) for ch in chains[:ATT_LOOKAHEAD]]
            for n, (hh, r0) in enumerate(chains):
                rows = slice(r0, r0 + rc)
                s = ahead.pop(0)
                if n + ATT_LOOKAHEAD < len(chains):
                    ahead.append(scores(*chains[n + ATT_LOOKAHEAD]))
                if masked:
                    s = jnp.where(_causal_mask(blk, rows=rc, row0=r0), s, -jnp.inf)
                m_old = m_ref[hh, rows]
                m_new = jnp.maximum(m_old, jnp.max(s, axis=-1, keepdims=True))
                p = jnp.exp2((s - m_new) * EXP2_SCALE).astype(BF16)
                acc_ref[hh, rows] = (jnp.exp2((m_old - m_new) * EXP2_SCALE) * acc_ref[hh, rows]
                                     + jnp.dot(p, v_exts[hh], preferred_element_type=F32))
                m_ref[hh, rows] = m_new

        @pl.when(ki < qi)
        def _():
            step(False)

        @pl.when(ki == qi)
        def _():
            step(True)
            for hh in range(hpd):
                l = acc_ref[hh, :, V_DIM:]
                o_ref[:, hh * V_DIM:(hh + 1) * V_DIM] = (acc_ref[hh, :, :V_DIM] / l).astype(o_ref.dtype)
                lse_ref[hh] = m_ref[hh] * EXP2_SCALE + jnp.log2(l[:, :1])

    sp = _attn_specs(S, blk, hpd)
    return _pallas(body, "attn_fwd", (N_DEV, q_tab.shape[0]), [q, kvu, kpe], [sp["q"], sp["kv"], sp["kp"]],
                   [sp["do"], sp["col"]],
                   [jax.ShapeDtypeStruct((S, H * V_DIM), BF16), jax.ShapeDtypeStruct((H, S, 1), F32)],
                   scratch=[pltpu.VMEM((hpd, blk, 1), F32), pltpu.VMEM((hpd, blk, 2 * V_DIM), F32)],
                   sem=("parallel", "arbitrary"), ride=ride, prefetch=[q_tab, k_tab])


def _attn_bwd(q, kvu, kpe, do, o, lse_row, cos_t, sin_t, H):
    S = kpe.shape[0]
    hpd = H // N_DEV
    blk = _tile(S, ATT_BLOCK)
    nb = S // blk
    q_tab, k_tab = _block_pairs(nb, by_key=True)

    def body(qt, kt, q_ref, kv_ref, kp_ref, do_ref, o_ref, lse_ref, c_ref, s_ref, dq_ref, dkv_ref, dkp_ref,
             dq_acc, dk_acc, dv_acc):
        t = pl.program_id(1)
        qi, ki = qt[t], kt[t]
        q_rows = pl.ds(pl.multiple_of(qi * blk, blk), blk)

        def step(diag):
            kp = kp_ref[...]
            ones = jnp.ones((8, V_DIM), BF16)
            for hh in range(hpd):
                k = _head_k(kv_ref, kp, hh)
                qv = q_ref[:, hh * HEAD_PAD:(hh + 1) * HEAD_PAD]
                cols = slice(hh * V_DIM, (hh + 1) * V_DIM)
                dov = do_ref[:, cols]
                doo = (dov.astype(F32) * o_ref[:, cols].astype(F32)).astype(BF16)
                delta = lax.dot_general(ones, doo, _DIMS["nt"], preferred_element_type=F32)[:1]
                st = lax.dot_general(k, qv, _DIMS["nt"], preferred_element_type=F32)
                pt = jnp.exp2(st * EXP2_SCALE - lse_ref[hh])
                if diag:
                    pt = jnp.where(_causal_mask(blk, transposed=True), pt, 0.0)
                dv_new = jnp.dot(pt.astype(BF16), dov, preferred_element_type=F32)
                dpt = lax.dot_general(_head_v(kv_ref, hh), dov, _DIMS["nt"], preferred_element_type=F32)
                dst = (pt * (dpt - delta)).astype(BF16)
                dk_new = jnp.dot(dst, qv, preferred_element_type=F32)
                dq_new = lax.dot_general(dst, k, _DIMS["tn"], preferred_element_type=F32)
                if diag:
                    dv_acc[hh] = dv_new
                    dk_acc[hh] = dk_new
                else:
                    dv_acc[hh] += dv_new
                    dk_acc[hh] += dk_new
                dq_acc[hh, q_rows] += dq_new

        @pl.when(t == 0)
        def _():
            dq_acc[...] = jnp.zeros_like(dq_acc)

        @pl.when(qi == ki)
        def _():
            step(True)
            for hh in range(hpd):
                c0 = hh * HEAD_PAD
                dq = dq_acc[hh, q_rows] * ATTN_SCALE
                dq_ref[:, c0:c0 + QK_NOPE_DIM] = dq[:, :QK_NOPE_DIM].astype(dq_ref.dtype)
                dq_ref[:, c0 + QK_NOPE_DIM:c0 + HEAD_PAD] = _rope(dq[:, QK_NOPE_DIM:], c_ref[...],
                                                                  -s_ref[...]).astype(dq_ref.dtype)

        @pl.when(qi > ki)
        def _():
            step(False)

        @pl.when(qi == nb - 1)
        def _():
            for hh in range(hpd):
                c0 = hh * HEAD_PAD
                dkv_ref[:, c0:c0 + QK_NOPE_DIM] = (dk_acc[hh, :, :QK_NOPE_DIM] * ATTN_SCALE).astype(dkv_ref.dtype)
                dkv_ref[:, c0 + QK_NOPE_DIM:c0 + HEAD_PAD] = dv_acc[hh].astype(dkv_ref.dtype)
                dkp_ref[hh] = dk_acc[hh, :, QK_NOPE_DIM:] * ATTN_SCALE

    sp = _attn_specs(S, blk, hpd)
    key_tab = pl.BlockSpec((blk, 128), lambda j, t, qt, kt: (kt[t], 0))
    grid_spec = pltpu.PrefetchScalarGridSpec(
        num_scalar_prefetch=2, grid=(N_DEV, q_tab.shape[0]),
        in_specs=[sp["q"], sp["kv"], sp["kp"], sp["do"], sp["do"], sp["row"], key_tab, key_tab],
        out_specs=(sp["kv"], sp["kv"], pl.BlockSpec((hpd, blk, 128), lambda j, t, qt, kt: (j, kt[t], 0))),
        scratch_shapes=[pltpu.VMEM((hpd, S, HEAD_PAD), F32), pltpu.VMEM((hpd, blk, HEAD_PAD), F32),
                        pltpu.VMEM((hpd, blk, V_DIM), F32)])
    return _pcall(
        body, name="attn_bwd", grid_spec=grid_spec,
        out_shape=(jax.ShapeDtypeStruct(q.shape, BF16), jax.ShapeDtypeStruct(kvu.shape, BF16),
                   jax.ShapeDtypeStruct((H, S, 128), F32)),
        compiler_params=_params(("parallel", "arbitrary")),
    )(q_tab, k_tab, q, kvu, kpe, do, o, lse_row, cos_t, sin_t)


def _mesh_pos():
    return lax.axis_index("x"), lax.axis_index("y"), lax.axis_index("c")


def _flip(pos, k):
    x, y, c = pos
    return (1 - x if k & 4 else x, 1 - y if k & 2 else y, 1 - c if k & 1 else c)


def _rank(pos):
    return 4 * pos[0] + 2 * pos[1] + pos[2]


def _copies(n, src_of, dst_refs, local_src_of, send_sems, recv_sems, local_sems):
    me = _mesh_pos()
    local = [pltpu.make_async_copy(local_src_of(a), dst_refs[a].at[_rank(me)], local_sems.at[a]) for a in range(n)]
    sends, arrivals = [], []
    for k in range(1, N_DEV):
        peer = _flip(me, k)
        for a in range(n):
            idx = a * (N_DEV - 1) + k - 1
            for into, group in ((me, sends), (peer, arrivals)):
                group.append(pltpu.make_async_remote_copy(
                    src_ref=src_of(a, peer), dst_ref=dst_refs[a].at[_rank(into)],
                    send_sem=send_sems.at[idx], recv_sem=recv_sems.at[idx],
                    device_id=peer, device_id_type=pl.DeviceIdType.MESH))
    return local, sends, arrivals


def _exchange_start(*args):
    local, sends, _ = _copies(*args)
    for cp in local + sends:
        cp.start()


def _exchange_wait(*args):
    local, sends, arrivals = _copies(*args)
    for cp in arrivals:
        cp.wait_recv()
    for cp in sends:
        cp.wait_send()
    for cp in local:
        cp.wait()


def _exchange(*args):
    _exchange_start(*args)
    _exchange_wait(*args)


def _sems(n):
    return [pltpu.SemaphoreType.DMA((n * (N_DEV - 1),)), pltpu.SemaphoreType.DMA((n * (N_DEV - 1),)),
            pltpu.SemaphoreType.DMA((n,))]


_ANY = pl.BlockSpec(memory_space=pl.ANY)


class _Ride:
    def __init__(self, kind, arrays):
        self.kind, self.arrays, self.n = kind, list(arrays), len(arrays)

    def out_shape(self):
        lead = () if self.kind == "scatter" else (N_DEV,)
        return [jax.ShapeDtypeStruct(lead + a.shape, a.dtype) for a in self.arrays]

    def _args(self, x_refs, out_refs, sems):
        if self.kind == "gather":
            return (self.n, lambda a, peer: x_refs[a], out_refs, lambda a: x_refs[a]) + tuple(sems)
        me = _mesh_pos()
        return (self.n, lambda a, peer: x_refs[a].at[_rank(peer)], out_refs, lambda a: x_refs[a].at[_rank(me)]) + tuple(sems)

    def start(self, x_refs, out_refs, sems):
        if self.kind == "gather2":
            _two_level_gather(x_refs, out_refs, *sems)[0]()
        else:
            _exchange_start(*self._args(x_refs, out_refs, sems))

    def wait(self, x_refs, out_refs, sems):
        if self.kind == "gather2":
            _two_level_gather(x_refs, out_refs, *sems)[1]()
        else:
            _exchange_wait(*self._args(x_refs, out_refs, sems))


def _pallas(body, name, grid, operands, in_specs, out_specs, out_shape, scratch=(), sem=None, ride=None, prefetch=()):
    n_pf, n_in, n_out, n_scr = len(prefetch), len(in_specs), len(out_specs), len(scratch)
    nr = ride.n if ride else 0

    def wrapped(*refs):
        refs = list(refs)
        pf, refs = refs[:n_pf], refs[n_pf:]
        ins, r_in = refs[:n_in], refs[n_in:n_in + nr]
        outs, r_out = refs[n_in + nr:n_in + nr + n_out], refs[n_in + nr + n_out:n_in + 2 * nr + n_out]
        rest = refs[n_in + 2 * nr + n_out:]
        scr, sems = rest[:n_scr], rest[n_scr:]
        if ride:
            first, last = None, None
            for d, size in enumerate(grid):
                i = pl.program_id(d)
                first = (i == 0) if first is None else jnp.logical_and(first, i == 0)
                last = (i == size - 1) if last is None else jnp.logical_and(last, i == size - 1)

            @pl.when(first)
            def _():
                ride.start(r_in, r_out, sems)

        body(pf + ins, outs, scr)
        if ride:
            @pl.when(last)
            def _():
                ride.wait(r_in, r_out, sems)

    grid_spec = pltpu.PrefetchScalarGridSpec(
        num_scalar_prefetch=n_pf, grid=grid,
        in_specs=list(in_specs) + [_ANY] * nr, out_specs=tuple(list(out_specs) + [_ANY] * nr),
        scratch_shapes=list(scratch) + (_sems(nr) if ride else []))
    sem = tuple(sem) if sem is not None and not ride else ("arbitrary",) * len(grid)
    return list(_pcall(
        wrapped, name=name, grid_spec=grid_spec,
        out_shape=tuple(list(out_shape) + (ride.out_shape() if ride else [])),
        compiler_params=_params(sem),
    )(*(list(prefetch) + list(operands) + (ride.arrays if ride else []))))


def _two_level_gather(x_refs, out_refs, send_sems, recv_sems, local_sems):
    n = len(x_refs)
    x, y, c = _mesh_pos()
    me, sibling = (x, y, c), (x, y, 1 - c)
    chips = [(1 - x, y), (x, 1 - y), (1 - x, 1 - y)]

    def copy(a, k, block, to, src=None):
        rows = out_refs[a].at[_rank(block)]
        return pltpu.make_async_remote_copy(
            src_ref=rows if src is None else src, dst_ref=rows,
            send_sem=send_sems.at[a * (N_DEV - 1) + k], recv_sem=recv_sems.at[a * (N_DEV - 1) + k],
            device_id=to, device_id_type=pl.DeviceIdType.MESH)

    def own_copies():
        mine = [pltpu.make_async_copy(x_refs[a], out_refs[a].at[_rank(me)], local_sems.at[a]) for a in range(n)]
        first = []
        for a in range(n):
            first.append(copy(a, 0, me, sibling, src=x_refs[a]))
            first += [copy(a, 1 + j, me, (*chip, c), src=x_refs[a]) for j, chip in enumerate(chips)]
        return mine, first

    def start():
        mine, first = own_copies()
        for cp in mine + first:
            cp.start()

    def finish():
        mine, first = own_copies()
        passed = []
        for j, chip in enumerate(chips):
            for a in range(n):
                copy(a, 1 + j, (*chip, c), me).wait_recv()
                passed.append(copy(a, 4 + j, (*chip, c), sibling))
                passed[-1].start()
        for a in range(n):
            copy(a, 0, sibling, me).wait_recv()
            for j, chip in enumerate(chips):
                copy(a, 4 + j, (*chip, 1 - c), me).wait_recv()
        for cp in first + passed:
            cp.wait_send()
        for cp in mine:
            cp.wait()

    return start, finish


def _all_gather(shards, name):
    n = len(shards)

    def body(*refs):
        start, finish = _two_level_gather(refs[:n], refs[n:2 * n], *refs[2 * n:])
        start()
        finish()

    return _pcall(
        body, name=name, in_specs=[_ANY] * n, out_specs=tuple([_ANY] * n),
        out_shape=tuple(jax.ShapeDtypeStruct((N_DEV,) + s.shape, s.dtype) for s in shards),
        scratch_shapes=_sems(n),
    )(*shards)


def _all_reduce_small(v):
    R, C = v.shape

    def body(x_ref, out_ref, buf_ref, send_sems, recv_sems, local_sems):
        _exchange(1, lambda a, peer: x_ref, [buf_ref], lambda a: x_ref, send_sems, recv_sems, local_sems)
        total = buf_ref[0]
        for j in range(1, N_DEV):
            total = total + buf_ref[j]
        out_ref[...] = total

    vm = pl.BlockSpec(memory_space=pltpu.VMEM)
    return _pcall(
        body, name="all_reduce_small", in_specs=[vm], out_specs=vm,
        out_shape=jax.ShapeDtypeStruct((R, C), F32),
        scratch_shapes=[pltpu.VMEM((N_DEV, R, C), F32)] + _sems(1),
    )(v)


def _sum_slots(layers, name):
    L = len(layers)
    shape = layers[0].shape[1:]
    C = shape[-1]
    R = layers[0].size // (N_DEV * C)
    tr = R
    while N_DEV * tr * C * 4 > (4 << 20) and tr % 32 == 0:
        tr //= 2

    def body(*refs):
        o_ref = refs[L]
        for li in range(L):
            @pl.when(pl.program_id(0) == li)
            def _():
                total = refs[li][0].astype(F32)
                for j in range(1, N_DEV):
                    total = total + refs[li][j].astype(F32)
                o_ref[0] = total

    in_specs = [pl.BlockSpec((N_DEV, tr, C), lambda l, i, li=li: (0, jnp.where(l == li, i, 0), 0)) for li in range(L)]
    return _pcall(
        body, name=name, grid=(L, R // tr),
        in_specs=in_specs,
        out_specs=pl.BlockSpec((1, tr, C), lambda l, i: (l, i, 0)),
        out_shape=jax.ShapeDtypeStruct((L, R, C), F32),
        compiler_params=_params(("parallel", "parallel")),
    )(*[p.reshape(N_DEV, R, C) for p in layers]).reshape((L,) + shape)


def _adamw_update(w, g, m, v):
    nm = ADAM_B1 * m + (1.0 - ADAM_B1) * g
    nv = ADAM_B2 * v + (1.0 - ADAM_B2) * (g * g)
    m_hat = nm / (1.0 - ADAM_B1 ** ADAM_STEP)
    v_hat = nv / (1.0 - ADAM_B2 ** ADAM_STEP)
    return -ADAM_LR * (m_hat / (jnp.sqrt(v_hat) + ADAM_EPS) + ADAM_WD * w), nm, nv


def _adamw_from_slots(w, layers, m, v, name, ride=None):
    L = len(layers)
    shape = w.shape
    C = shape[-1]
    R = w.size // (L * C)
    fits = [t for t in range(16, R + 1, 16) if R % t == 0 and N_DEV * t * C * 2 <= (1 << 20)]
    tr = max(fits) if fits else R

    def body(ins, outs, scratch):
        w_ref, m_ref, v_ref = ins[L:]
        g_ref, d_ref, nm_ref, nv_ref = outs
        for li in range(L):
            @pl.when(pl.program_id(0) == li)
            def _():
                g = ins[li][0].astype(F32)
                for j in range(1, N_DEV):
                    g = g + ins[li][j].astype(F32)
                g_ref[0] = g
                d_ref[0], nm_ref[0], nv_ref[0] = _adamw_update(w_ref[0], g, m_ref[0], v_ref[0])

    slot_specs = [pl.BlockSpec((N_DEV, tr, C), lambda l, i, li=li: (0, jnp.where(l == li, i, 0), 0)) for li in range(L)]
    blk = pl.BlockSpec((1, tr, C), lambda l, i: (l, i, 0))
    sds = jax.ShapeDtypeStruct((L, R, C), F32)
    outs = _pallas(body, name, (L, R // tr),
                   [p.reshape(N_DEV, R, C) for p in layers] + [t.reshape(L, R, C) for t in (w, m, v)],
                   slot_specs + [blk] * 3, [blk] * 4, [sds] * 4, sem=("parallel", "parallel"), ride=ride)
    return [o.reshape(shape) for o in outs[:4]] + outs[4:]


def _adamw(w, g, m, v, name):
    shape = w.shape
    C = shape[-1]
    R = w.size // C
    tr = R
    while tr * C * 4 > (1 << 20) and tr % 16 == 0:
        tr //= 2

    def body(w_ref, g_ref, m_ref, v_ref, d_ref, nm_ref, nv_ref):
        d_ref[...], nm_ref[...], nv_ref[...] = _adamw_update(w_ref[...], g_ref[...], m_ref[...], v_ref[...])

    blk = pl.BlockSpec((tr, C), lambda i: (i, 0))
    sds = jax.ShapeDtypeStruct((R, C), F32)
    outs = _pcall(
        body, name=name, grid=(R // tr,),
        in_specs=[blk] * 4, out_specs=(blk,) * 3, out_shape=(sds,) * 3,
        compiler_params=_params(("parallel",)),
    )(*(t.reshape(R, C) for t in (w, g, m, v)))
    return tuple(o.reshape(shape) for o in outs)


_REPLICATED = ("kv_in_norm", "kv_latent_norm", "attn_norm", "q_latent_norm", "ffn_norm", "final_norm")
_WEIGHTS = ("pool_norm", "pool_w", "pool_scale", "kv_in_norm", "w_kv_a", "kv_latent_norm", "w_kv_b", "attn_norm",
            "w_q_a", "q_latent_norm", "w_q_b", "w_o", "ffn_norm", "w_gate", "w_up", "w_down", "final_norm")


def _ffn_fwd(x, norm_g, wg, wu, wd, tag, gather=None):
    S, D = x.shape
    fs = wg.shape[1]
    h, ht = _rmsnorm(x, norm_g, "ffn_norm" + tag, transposed_too=True)
    g, u, a, *gathered = _ffn_up(h, wg, wu, "ffn_up" + tag, ride=gather)
    if gather is not None:
        wd = gathered[0].reshape(N_DEV * fs, D)
    y = _ffn_down(a, wd, x, "ffn_down" + tag)
    return y, (ht, g, u, a), wd, gathered


def _ffn_bwd(x, norm_g, wg, wu, wd, saved, dy, dyb, tag, with_bf16, scatter=None):
    S, D = x.shape
    ht, g, u, a = saved
    fs = g.shape[1]
    tk = _tile(S, 4 * ROW_TILE)
    nkb = S // tk
    dg, du, *got_rider = _ffn_dact(dyb, wd, g, u, "ffn_dact" + tag, ride=scatter)
    dwd = _mm("ffn_dwd" + tag, "tn", a, dyb, (fs, D, tk), (N_DEV, 1, nkb),
              lambda j, q, k: (j * nkb + k, 0), lambda j, q, k: (k, 0), lambda j, q, k: (j, 0), (N_DEV * fs, D), BF16)
    dwg, (got_dwd,) = _slots_dw_t("ffn_dwg" + tag, ht, dg, ride=_Ride("scatter", [dwd.reshape(N_DEV, fs, D)]))
    dwu, (got_dwg,) = _slots_dw_t("ffn_dwu" + tag, ht, du, ride=_Ride("scatter", [dwg.reshape(N_DEV, D, fs)]))
    dh, got_dwu = _ffn_dh(dg, du, wg, wu, S, "ffn_dh" + tag, ride=_Ride("scatter", [dwu.reshape(N_DEV, D, fs)]))
    outs = _rmsnorm_bwd(x, norm_g, dh, dy, "ffn_norm_bwd" + tag, with_bf16=with_bf16)
    return outs, got_rider, (got_dwg, got_dwu, got_dwd)


def kernel(x, positions, pool_norm, pool_w, pool_scale, kv_in_norm, w_kv_a, kv_latent_norm, w_kv_b, attn_norm, w_q_a, q_latent_norm, w_q_b, w_o, ffn_norm, w_gate, w_up, w_down, final_norm, loss_target, m_pool_norm, m_pool_w, m_pool_scale, m_kv_in_norm, m_w_kv_a, m_kv_latent_norm, m_w_kv_b, m_attn_norm, m_w_q_a, m_q_latent_norm, m_w_q_b, m_w_o, m_ffn_norm, m_w_gate, m_w_up, m_w_down, m_final_norm, v_pool_norm, v_pool_w, v_pool_scale, v_kv_in_norm, v_w_kv_a, v_kv_latent_norm, v_w_kv_b, v_attn_norm, v_w_q_a, v_q_latent_norm, v_w_q_b, v_w_o, v_ffn_norm, v_w_gate, v_w_up, v_w_down, v_final_norm):
    env = dict(locals())
    weights = {n: env[n] for n in _WEIGHTS}
    m_in = {n: env["m_" + n] for n in _WEIGHTS}
    v_in = {n: env["v_" + n] for n in _WEIGHTS}

    S, D = x.shape[1], x.shape[2]
    xs = x.reshape(S, D)
    target = loss_target.reshape(S, D)
    G, gws, gw = pool_w.shape[1:]
    Ds = w_kv_a.shape[0]
    R_kv, kvw = w_kv_b.shape
    hpd = kvw // (QK_NOPE_DIM + V_DIM)
    H = hpd * N_DEV
    R_q = w_q_a.shape[2]
    fs = w_gate.shape[2]
    bf = lambda t: t.astype(BF16)

    gains = jnp.concatenate([pool_norm, pool_scale], axis=0)
    g_pool_w, g_gains = _all_gather([bf(pool_w[0]), gains], "gather_first")
    wqb_pad = jnp.pad(w_q_b.reshape(R_q, hpd, QK_DIM), ((0, 0), (0, 0), (0, HEAD_PAD - QK_DIM))).reshape(R_q, hpd * HEAD_PAD)
    gather_mla = _Ride("gather", [bf(w_down[0]), bf(jnp.pad(w_kv_a, ((0, 0), (0, 128 - QK_ROPE_DIM)))), bf(w_kv_b),
                                  bf(w_q_a[0]), bf(wqb_pad), bf(w_o[0])])
    gather_ffn1 = _Ride("gather", [bf(w_gate[1]), bf(w_up[1]), bf(w_down[1])])

    wp = jnp.swapaxes(g_pool_w, 0, 1).reshape(G, gw, gw)
    gains_full = jnp.swapaxes(g_gains, 0, 1).reshape(2, D)
    g_pool_norm, g_pool_scale = gains_full[0:1], gains_full[1:2]

    g_kv_in = kv_in_norm.reshape(1, D)
    g_kv_lat = kv_latent_norm.reshape(1, R_kv)
    g_attn = attn_norm.reshape(1, D)
    g_q_lat = q_latent_norm.reshape(1, R_q)
    g_final = final_norm.reshape(1, D)

    half = QK_ROPE_DIM // 2
    inv_freq = ROPE_BASE ** (-jnp.arange(half, dtype=F32) / half)
    ang = positions.reshape(S).astype(F32)[:, None] * inv_freq
    cos, sin = jnp.cos(ang), jnp.sin(ang)
    zeros = jnp.zeros((S, 128 - QK_ROPE_DIM), F32)
    cos_t = jnp.concatenate([cos, cos, zeros], axis=1)
    sin_t = jnp.concatenate([-sin, sin, zeros], axis=1)

    diff, g_wg0 = _pool_pre(xs, g_pool_norm, ride=_Ride("gather2", [bf(w_gate[0])]))
    x1, g_wu0 = _pool_mix(diff, wp, g_pool_scale, xs, ride=_Ride("gather2", [bf(w_up[0])]))
    wg0, wu0 = g_wg0.reshape(N_DEV * D, fs), g_wu0.reshape(N_DEV * D, fs)
    x2, ffn0, wd0, (_, g_wkva, g_wkvb, g_wqa, g_wqb, g_wo) = _ffn_fwd(x1, ffn_norm[0:1], wg0, wu0, None, "0", gather_mla)
    wkva = g_wkva.reshape(D, R_kv + 128)
    wkvb = g_wkvb.reshape(N_DEV * R_kv, kvw)
    wqa = g_wqa.reshape(D, R_q)
    wqb = g_wqb.reshape(N_DEV * R_q, hpd * HEAD_PAD)
    wo = g_wo.reshape(H * V_DIM, D)

    hk, ha = _rmsnorm_pair(x2, g_kv_in, g_attn, "kv_attn_norm")
    kv = _mm_plain("kv_a", "nn", hk, wkva)
    ckv, kpe = _kv_mid(kv, g_kv_lat, cos_t, sin_t)
    kvu = _small_slots_out("kv_b", ckv, wkvb, BF16)

    ql = _mm_plain("q_a", "nn", ha, wqa)
    qn = _rmsnorm(ql, g_q_lat, "q_latent_norm")
    tab = pl.BlockSpec((_tile(S, ROW_TILE), 128), lambda i: (i, 0))
    qr = _small_slots_out("q_b", qn, wqb, BF16, epilogue=(_rope_heads, [cos_t, sin_t], [tab, tab]))
    o, lse, g_wg1, g_wu1, g_wd1 = _attn_fwd(qr, kvu, kpe, H, ride=gather_ffn1)
    wg1, wu1, wd1 = g_wg1.reshape(N_DEV * D, fs), g_wu1.reshape(N_DEV * D, fs), g_wd1.reshape(N_DEV * fs, D)
    x3 = _mm_plain("attn_out", "nn", o, wo, res=x2)
    x4, ffn1, _, _ = _ffn_fwd(x3, ffn_norm[1:2], wg1, wu1, wd1, "1")

    loss_part, dx4, dx4b, d_final = _loss_head(x4, g_final, target)

    (dx3, dx3b, d_ffn1), _, got_ffn1 = _ffn_bwd(x3, ffn_norm[1:2], wg1, wu1, wd1, ffn1, dx4, dx4b, "1", True)

    do = _mm_plain("attn_out_dx", "nt", dx3b, wo, out_dtype=BF16)
    dwo = _mm_plain("attn_out_dw", "tn", o, dx3b, out_dtype=BF16)
    dq, dkvu, dkpe = _attn_bwd(qr, kvu, kpe, do, o, lse.reshape(H, 1, S), cos_t, sin_t, H)

    dqn = _small_slots_in_nt("q_b_dx", dq, wqb, S)
    dwqb = _small_slots_dw("q_b_dw", qn, dq, BF16)
    dql, d_q_lat = _rmsnorm_bwd(ql, g_q_lat, dqn, None, "q_latent_norm_bwd", out_dtype=BF16)
    dha = _mm_plain("q_a_dx", "nt", dql, wqa)
    dwqa = _mm_plain("q_a_dw", "tn", ha, dql, out_dtype=BF16)

    dckv = _small_slots_in_nt("kv_b_dx", dkvu, wkvb, S)
    dwkvb = _small_slots_dw("kv_b_dw", ckv, dkvu, BF16)
    dkv, d_kv_lat = _kv_mid_bwd(kv, g_kv_lat, dckv, dkpe, cos_t, sin_t)
    dhk = _mm_plain("kv_a_dx", "nt", dkv, wkva)
    dwkva = _mm_plain("kv_a_dw", "tn", hk, dkv, out_dtype=BF16)
    dx2, dx2b, d_attn, d_kv_in = _rmsnorm_pair_bwd(x2, g_attn, dha, g_kv_in, dhk, dx3, "kv_attn_norm_bwd")

    scatter_mla = _Ride("scatter", [dwkva.reshape(N_DEV, Ds, R_kv + 128), dwkvb.reshape(N_DEV, R_kv, kvw),
                                    dwqa.reshape(N_DEV, Ds, R_q), dwqb.reshape(N_DEV, R_q, hpd * HEAD_PAD),
                                    dwo.reshape(N_DEV, H * V_DIM // N_DEV, D)])
    (dx1, d_ffn0), got_mla, got_ffn0 = _ffn_bwd(x1, ffn_norm[0:1], wg0, wu0, wd0, ffn0, dx2, dx2b, "0", False,
                                                  scatter=scatter_mla)

    ddiff, dwp, d_pool_scale = _pool_mix_bwd(diff, wp, g_pool_scale, dx1)
    grad_x, d_pool_norm = _pool_pre_bwd(xs, g_pool_norm, ddiff, dx1)

    d_gains = jnp.swapaxes(jnp.concatenate([d_pool_norm, d_pool_scale], axis=0).reshape(2, N_DEV, D // N_DEV), 0, 1)
    scatter_pool = _Ride("scatter", [jnp.swapaxes(dwp.reshape(G, N_DEV, gws, gw), 0, 1), d_gains])

    grads, delta_w, new_m, new_v = {}, {}, {}, {}
    ffn_slots = {n: [r0, r1] for n, r0, r1 in zip(("w_gate", "w_up", "w_down"), got_ffn0, got_ffn1)}
    got_pool = []
    for n, ride in zip(ffn_slots, (scatter_pool, None, None)):
        grads[n], delta_w[n], new_m[n], new_v[n], *got = _adamw_from_slots(weights[n], ffn_slots[n], m_in[n], v_in[n],
                                                                            "adamw_" + n, ride=ride)
        got_pool += got
    got_pool_w, got_gains = got_pool
    received = {n: [r] for n, r in zip(("w_kv_a", "w_kv_b", "w_q_a", "w_q_b", "w_o"), got_mla)}
    received.update(pool_w=[got_pool_w], gains=[got_gains])
    sums = {n: _sum_slots(r, "sum_" + n) for n, r in received.items()}
    grads.update({
        "pool_w": sums["pool_w"],
        "w_kv_a": sums["w_kv_a"][0, :, :R_kv + QK_ROPE_DIM],
        "w_kv_b": sums["w_kv_b"][0],
        "w_q_a": sums["w_q_a"],
        "w_q_b": sums["w_q_b"].reshape(R_q, hpd, HEAD_PAD)[:, :, :QK_DIM].reshape(1, R_q, hpd * QK_DIM),
        "w_o": sums["w_o"],
        "pool_norm": sums["gains"][0, 0:1],
        "pool_scale": sums["gains"][0, 1:2],
    })

    small = {"kv_in_norm": d_kv_in, "kv_latent_norm": d_kv_lat, "attn_norm": d_attn, "q_latent_norm": d_q_lat,
             "ffn_norm": jnp.concatenate([d_ffn0, d_ffn1], axis=0), "final_norm": d_final}
    small_packed = jnp.concatenate([small[n].reshape(-1) for n in _REPLICATED] + [loss_part.reshape(-1)])
    pad = (-small_packed.shape[0]) % (8 * 128)
    reduced = _all_reduce_small(jnp.pad(small_packed, (0, pad)).reshape(-1, 128)).reshape(-1)
    off = 0
    for n in _REPLICATED:
        grads[n] = reduced[off:off + weights[n].size].reshape(weights[n].shape)
        off += weights[n].size
    loss = reduced[off]

    for n in _WEIGHTS:
        if n not in ffn_slots:
            delta_w[n], new_m[n], new_v[n] = _adamw(weights[n], grads[n], m_in[n], v_in[n], "adamw_" + n)

    return (loss, grad_x.reshape(x.shape), *[grads[n] for n in _WEIGHTS], *[delta_w[n] for n in _WEIGHTS],
            *[new_m[n] for n in _WEIGHTS], *[new_v[n] for n in _WEIGHTS])
```

```python
import math

import jax
import jax.numpy as jnp
import numpy as np
from jax import lax
from jax.experimental import pallas as pl
from jax.experimental.pallas import tpu as pltpu

F32 = jnp.float32
BF16 = jnp.bfloat16

N_DEV = 8
POOL_WINDOWS = (2, 4, 8, 16)
POOL_HALO = 16
QK_NOPE_DIM = 128
QK_ROPE_DIM = 64
QK_DIM = QK_NOPE_DIM + QK_ROPE_DIM
HEAD_PAD = 256
V_DIM = 128
ROPE_BASE = 10000.0
ATTN_SCALE = 1.0 / math.sqrt(QK_DIM)
EXP2_SCALE = ATTN_SCALE * math.log2(math.e)
NORM_EPS = 1e-6
ADAM_LR = 0.001
ADAM_B1 = 0.9
ADAM_B2 = 0.999
ADAM_EPS = 1e-08
ADAM_WD = 0.01
ADAM_STEP = 10

ROW_TILE = 512
ATT_BLOCK = 1024
ATT_LOOKAHEAD = 1
ATT_ROW_CHUNK = 256
FFN_SLOTS = 2
MM_TN, MM_TK = 512, 512
VMEM_LIMIT = 48 * 1024 * 1024

_pcall = pl.pallas_call


def _params(sem):
    return pltpu.CompilerParams(dimension_semantics=sem, vmem_limit_bytes=VMEM_LIMIT)


def _tile(dim, default):
    if dim % default == 0:
        return default
    assert dim <= 2 * default, (dim, default)
    return dim


def _row_tile(rows, width):
    ts = _tile(rows, ROW_TILE)
    while ts * width * 4 > (2 << 20) and ts % 16 == 0:
        ts //= 2
    return ts


def _rms_scale(x):
    return lax.rsqrt(jnp.mean(x * x, axis=-1, keepdims=True) + NORM_EPS)


def _rms_bwd(x, g, dy):
    r = _rms_scale(x)
    xn = x * r
    u = dy * g
    dx = r * (u - xn * jnp.mean(u * xn, axis=-1, keepdims=True))
    return dx, dy * xn


def _swap_halves(x):
    lane = lax.broadcasted_iota(jnp.int32, x.shape, 1)
    return jnp.where(lane < QK_ROPE_DIM // 2, pltpu.roll(x, 128 - QK_ROPE_DIM // 2, 1), pltpu.roll(x, QK_ROPE_DIM // 2, 1))


def _rope(x, cos_t, sin_t):
    return x * cos_t + _swap_halves(x) * sin_t


def _rmsnorm(x, g, name, transposed_too=False):
    S, D = x.shape
    ts = _row_tile(S, D)

    def body(x_ref, g_ref, o_ref, *t_ref):
        xf = x_ref[...]
        y = xf * _rms_scale(xf) * g_ref[...]
        o_ref[...] = y.astype(o_ref.dtype)
        if transposed_too:
            t_ref[0][...] = y.T.astype(o_ref.dtype)

    row = pl.BlockSpec((ts, D), lambda i: (i, 0))
    outs = _pcall(
        body, name=name, grid=(S // ts,),
        in_specs=[row, pl.BlockSpec((1, D), lambda i: (0, 0))],
        out_specs=(row, pl.BlockSpec((D, ts), lambda i: (0, i))) if transposed_too else row,
        out_shape=((jax.ShapeDtypeStruct((S, D), BF16), jax.ShapeDtypeStruct((D, S), BF16)) if transposed_too
                   else jax.ShapeDtypeStruct((S, D), BF16)),
        compiler_params=_params(("parallel",)),
    )(x, g)
    return outs


def _rmsnorm_bwd(x, g, dy, res, name, out_dtype=F32, with_bf16=False):
    S, D = x.shape
    ts = _row_tile(S, D)
    has_res = res is not None

    def body(*refs):
        refs = list(refs)
        x_ref, g_ref, dy_ref = refs[:3]
        res_ref = refs[3] if has_res else None
        outs = refs[3 + has_res:]
        dx_ref, dg_ref = outs[0], outs[-1]
        dx, dgt = _rms_bwd(x_ref[...], g_ref[...], dy_ref[...].astype(F32))
        if has_res:
            dx = res_ref[...] + dx
        dx_ref[...] = dx.astype(dx_ref.dtype)
        if with_bf16:
            outs[1][...] = dx.astype(BF16)

        @pl.when(pl.program_id(0) == 0)
        def _():
            dg_ref[...] = jnp.zeros_like(dg_ref)

        dg_ref[...] += jnp.sum(dgt, axis=0, keepdims=True)

    row = pl.BlockSpec((ts, D), lambda i: (i, 0))
    vec = pl.BlockSpec((1, D), lambda i: (0, 0))
    out_specs = [row] + ([row] if with_bf16 else []) + [vec]
    out_shape = ([jax.ShapeDtypeStruct((S, D), out_dtype)] + ([jax.ShapeDtypeStruct((S, D), BF16)] if with_bf16 else [])
                 + [jax.ShapeDtypeStruct((1, D), F32)])
    return _pcall(
        body, name=name, grid=(S // ts,),
        in_specs=[row, vec, row] + ([row] if has_res else []),
        out_specs=tuple(out_specs), out_shape=tuple(out_shape),
        compiler_params=_params(("arbitrary",)),
    )(*([x, g, dy] + ([res] if has_res else [])))


def _rmsnorm_pair(x, g1, g2, name):
    S, D = x.shape
    ts = _row_tile(S, D)

    def body(x_ref, g1_ref, g2_ref, o1_ref, o2_ref):
        xf = x_ref[...]
        xn = xf * _rms_scale(xf)
        o1_ref[...] = (xn * g1_ref[...]).astype(o1_ref.dtype)
        o2_ref[...] = (xn * g2_ref[...]).astype(o2_ref.dtype)

    row = pl.BlockSpec((ts, D), lambda i: (i, 0))
    vec = pl.BlockSpec((1, D), lambda i: (0, 0))
    sds = jax.ShapeDtypeStruct((S, D), BF16)
    return _pcall(
        body, name=name, grid=(S // ts,),
        in_specs=[row, vec, vec], out_specs=(row, row), out_shape=(sds, sds),
        compiler_params=_params(("parallel",)),
    )(x, g1, g2)


def _rmsnorm_pair_bwd(x, g1, dy1, g2, dy2, res, name):
    S, D = x.shape
    ts = _row_tile(S, D)

    def body(x_ref, g1_ref, dy1_ref, g2_ref, dy2_ref, res_ref, dx_ref, dxb_ref, dg1_ref, dg2_ref):
        xf = x_ref[...]
        dx1, dgt1 = _rms_bwd(xf, g1_ref[...], dy1_ref[...])
        dx2, dgt2 = _rms_bwd(xf, g2_ref[...], dy2_ref[...])
        dx = (res_ref[...] + dx1) + dx2
        dx_ref[...] = dx
        dxb_ref[...] = dx.astype(BF16)

        @pl.when(pl.program_id(0) == 0)
        def _():
            dg1_ref[...] = jnp.zeros_like(dg1_ref)
            dg2_ref[...] = jnp.zeros_like(dg2_ref)

        dg1_ref[...] += jnp.sum(dgt1, axis=0, keepdims=True)
        dg2_ref[...] += jnp.sum(dgt2, axis=0, keepdims=True)

    row = pl.BlockSpec((ts, D), lambda i: (i, 0))
    vec = pl.BlockSpec((1, D), lambda i: (0, 0))
    return _pcall(
        body, name=name, grid=(S // ts,),
        in_specs=[row, vec, row, vec, row, row], out_specs=(row, row, vec, vec),
        out_shape=(jax.ShapeDtypeStruct((S, D), F32), jax.ShapeDtypeStruct((S, D), BF16),
                   jax.ShapeDtypeStruct((1, D), F32), jax.ShapeDtypeStruct((1, D), F32)),
        compiler_params=_params(("arbitrary",)),
    )(x, g1, dy1, g2, dy2, res)


def _pool_pre(x, g, ride=None):
    S, D = x.shape
    ts = _row_tile(S, D)
    gw = D // len(POOL_WINDOWS)
    hb = ts // POOL_HALO

    def body(ins, outs, scratch):
        (x_ref, halo_ref, g_ref), (o_ref,) = ins, outs
        i = pl.program_id(0)
        xx = jnp.concatenate([halo_ref[...], x_ref[...]], axis=0)
        h = xx * _rms_scale(xx) * g_ref[...]
        t = i * ts - POOL_HALO + lax.broadcasted_iota(jnp.int32, (ts + POOL_HALO, 1), 0)
        h = jnp.where(t >= 0, h, 0.0)
        tf = t[POOL_HALO:].astype(F32)
        for gi, w in enumerate(POOL_WINDOWS):
            hg = h[:, gi * gw:(gi + 1) * gw]
            acc, span = hg, 1
            while span < w:
                acc = acc + pltpu.roll(acc, span, 0)
                span *= 2
            count = jnp.minimum(tf + 1.0, float(w))
            diff = acc[POOL_HALO:] / count - hg[POOL_HALO:]
            o_ref[:, gi * gw:(gi + 1) * gw] = diff.astype(o_ref.dtype)

    return _pallas(body, "pool_pre", (S // ts,), [x, x, g],
                   [pl.BlockSpec((ts, D), lambda i: (i, 0)),
                    pl.BlockSpec((POOL_HALO, D), lambda i: (jnp.maximum(i * hb - 1, 0), 0)),
                    pl.BlockSpec((1, D), lambda i: (0, 0))],
                   [pl.BlockSpec((ts, D), lambda i: (i, 0))], [jax.ShapeDtypeStruct((S, D), BF16)],
                   sem=("parallel",), ride=ride)


def _pool_pre_bwd(x, g, dd, res):
    S, D = x.shape
    ts = _row_tile(S, D)
    gw = D // len(POOL_WINDOWS)
    hb = ts // POOL_HALO
    last_halo = S // POOL_HALO - 1

    def body(x_ref, g_ref, dd_ref, halo_ref, res_ref, dx_ref, dg_ref):
        i = pl.program_id(0)
        ee = jnp.concatenate([dd_ref[...], halo_ref[...]], axis=0)
        t = i * ts + lax.broadcasted_iota(jnp.int32, (ts + POOL_HALO, 1), 0)
        ee = jnp.where(t < S, ee, 0.0)
        tf = t.astype(F32)
        n = ts + POOL_HALO
        for gi, w in enumerate(POOL_WINDOWS):
            eg = ee[:, gi * gw:(gi + 1) * gw]
            acc, span = eg / jnp.minimum(tf + 1.0, float(w)), 1
            while span < w:
                acc = acc + pltpu.roll(acc, n - span, 0)
                span *= 2
            dx_ref[:, gi * gw:(gi + 1) * gw] = acc[:ts] - eg[:ts]
        dx, dgt = _rms_bwd(x_ref[...], g_ref[...], dx_ref[...])
        dx_ref[...] = res_ref[...] + dx

        @pl.when(i == 0)
        def _():
            dg_ref[...] = jnp.zeros_like(dg_ref)

        dg_ref[...] += jnp.sum(dgt, axis=0, keepdims=True)

    row = pl.BlockSpec((ts, D), lambda i: (i, 0))
    vec = pl.BlockSpec((1, D), lambda i: (0, 0))
    return _pcall(
        body, name="pool_pre_bwd", grid=(S // ts,),
        in_specs=[row, vec, row,
                  pl.BlockSpec((POOL_HALO, D), lambda i: (jnp.minimum((i + 1) * hb, last_halo), 0)), row],
        out_specs=(row, vec),
        out_shape=(jax.ShapeDtypeStruct((S, D), F32), jax.ShapeDtypeStruct((1, D), F32)),
        compiler_params=_params(("arbitrary",)),
    )(x, g, dd, dd, res)


def _pool_mix(diff, w, scale, x, ride=None):
    S, D = x.shape
    G, gw, _ = w.shape
    ts = _tile(S, ROW_TILE)

    def body(ins, outs, scratch):
        (d_ref, w_ref, s_ref, x_ref), (o_ref,) = ins, outs
        mo = jnp.dot(d_ref[...], w_ref[0], preferred_element_type=F32)
        o_ref[...] = x_ref[...] + mo * s_ref[...]

    blk = pl.BlockSpec((ts, gw), lambda i, g: (i, g))
    return _pallas(body, "pool_mix", (S // ts, G), [diff, w, scale, x],
                   [blk, pl.BlockSpec((1, gw, gw), lambda i, g: (g, 0, 0)), pl.BlockSpec((1, gw), lambda i, g: (0, g)), blk],
                   [blk], [jax.ShapeDtypeStruct((S, D), F32)], sem=("parallel", "parallel"), ride=ride)


def _pool_mix_bwd(diff, w, scale, dy):
    S, D = dy.shape
    G, gw, _ = w.shape
    ts = _tile(S, ROW_TILE)

    def body(d_ref, w_ref, s_ref, dy_ref, dd_ref, dw_ref, ds_ref):
        i = pl.program_id(1)
        d = d_ref[...]
        dy = dy_ref[...]
        mo = jnp.dot(d, w_ref[0], preferred_element_type=F32)
        dmo = (dy * s_ref[...]).astype(BF16)
        dd_ref[...] = lax.dot_general(dmo, w_ref[0], (((1,), (1,)), ((), ())), preferred_element_type=F32)

        @pl.when(i == 0)
        def _():
            dw_ref[...] = jnp.zeros_like(dw_ref)
            ds_ref[...] = jnp.zeros_like(ds_ref)

        dw_ref[0] += lax.dot_general(d, dmo, (((0,), (0,)), ((), ())), preferred_element_type=F32)
        ds_ref[...] += jnp.sum(dy * mo, axis=0, keepdims=True)

    blk = pl.BlockSpec((ts, gw), lambda g, i: (i, g))
    wspec = pl.BlockSpec((1, gw, gw), lambda g, i: (g, 0, 0))
    vec = pl.BlockSpec((1, gw), lambda g, i: (0, g))
    return _pcall(
        body, name="pool_mix_bwd", grid=(G, S // ts),
        in_specs=[blk, wspec, vec, blk],
        out_specs=(blk, wspec, vec),
        out_shape=(jax.ShapeDtypeStruct((S, D), F32), jax.ShapeDtypeStruct((G, gw, gw), F32),
                   jax.ShapeDtypeStruct((1, D), F32)),
        compiler_params=_params(("parallel", "arbitrary")),
    )(diff, w, scale, dy)


def _loss_head(x, g, target):
    S, D = x.shape
    ts = _row_tile(S, D)

    def body(x_ref, g_ref, t_ref, loss_ref, dx_ref, dxb_ref, dg_ref):
        xf = x_ref[...]
        gv = g_ref[...]
        err = xf * _rms_scale(xf) * gv - t_ref[...]
        dx, dgt = _rms_bwd(xf, gv, err * (1.0 / D))
        dx_ref[...] = dx
        dxb_ref[...] = dx.astype(BF16)

        @pl.when(pl.program_id(0) == 0)
        def _():
            loss_ref[...] = jnp.zeros_like(loss_ref)
            dg_ref[...] = jnp.zeros_like(dg_ref)

        part = 0.5 * jnp.sum(jnp.sum(err * err, axis=-1, keepdims=True) * (1.0 / D), axis=0, keepdims=True)
        loss_ref[...] += jnp.broadcast_to(part, loss_ref.shape)
        dg_ref[...] += jnp.sum(dgt, axis=0, keepdims=True)

    row = pl.BlockSpec((ts, D), lambda i: (i, 0))
    vec = pl.BlockSpec((1, D), lambda i: (0, 0))
    return _pcall(
        body, name="loss_head", grid=(S // ts,),
        in_specs=[row, vec, row],
        out_specs=(pl.BlockSpec((1, 128), lambda i: (0, 0)), row, row, vec),
        out_shape=(jax.ShapeDtypeStruct((1, 128), F32), jax.ShapeDtypeStruct((S, D), F32),
                   jax.ShapeDtypeStruct((S, D), BF16), jax.ShapeDtypeStruct((1, D), F32)),
        compiler_params=_params(("arbitrary",)),
    )(x, g, target)


_DIMS = {"nn": (((1,), (0,)), ((), ())), "nt": (((1,), (1,)), ((), ())), "tn": (((0,), (0,)), ((), ()))}


def _mm(name, mode, a, b, tiles, grid, a_map, b_map, o_map, out_shape, out_dtype=F32, res=None, ride=None):
    tm, tn, tk = tiles
    nk = grid[-1]
    has_res = res is not None
    dn = _DIMS[mode]

    def body(ins, outs, scratch):
        o_ref = outs[0]

        def product():
            return lax.dot_general(ins[0][...].astype(BF16), ins[1][...].astype(BF16), dn, preferred_element_type=F32)

        def finish(out):
            if has_res:
                out = ins[2][...] + out
            o_ref[...] = out.astype(o_ref.dtype)

        if nk == 1:
            finish(product())
            return
        acc_ref = scratch[0]
        k = pl.program_id(2)

        @pl.when(k == 0)
        def _():
            acc_ref[...] = jnp.zeros_like(acc_ref)

        acc_ref[...] += product()

        @pl.when(k == nk - 1)
        def _():
            finish(acc_ref[...])

    a_spec = pl.BlockSpec((tk, tm) if mode == "tn" else (tm, tk), a_map)
    b_spec = pl.BlockSpec((tn, tk) if mode == "nt" else (tk, tn), b_map)
    o_spec = pl.BlockSpec((tm, tn), o_map)
    operands, in_specs = [a, b], [a_spec, b_spec]
    if has_res:
        operands.append(res)
        in_specs.append(o_spec)
    outs = _pallas(body, name, grid, operands, in_specs, [o_spec], [jax.ShapeDtypeStruct(out_shape, out_dtype)],
                   scratch=[pltpu.VMEM((tm, tn), F32)] if nk > 1 else [],
                   sem=("parallel", "parallel", "arbitrary"), ride=ride)
    return (outs[0], outs[1:]) if ride else outs[0]


def _mm_plain(name, mode, a, b, out_dtype=F32, res=None):
    if mode == "tn":
        (K, M), N = a.shape, b.shape[1]
    elif mode == "nt":
        (M, K), N = a.shape, b.shape[0]
    else:
        (M, K), N = a.shape, b.shape[1]
    if mode == "tn":
        tm, tn, tk = _tile(M, 2 * MM_TN), _tile(N, 2 * MM_TN), _tile(K, 4 * ROW_TILE)
    else:
        tm = _tile(M, ROW_TILE)
        tk = K if K <= 4 * MM_TK else _tile(K, MM_TK)
        tn = N if N * tk * 2 <= (8 << 20) else _tile(N, MM_TN)
    a_map = (lambda i, j, k: (k, i)) if mode == "tn" else (lambda i, j, k: (i, k))
    b_map = (lambda i, j, k: (j, k)) if mode == "nt" else (lambda i, j, k: (k, j))
    return _mm(name, mode, a, b, (tm, tn, tk), (M // tm, N // tn, K // tk), a_map, b_map, lambda i, j, k: (i, j),
               (M, N), out_dtype, res)


def _small_slots_out(name, a, w_slots, out_dtype, epilogue=None):
    S, K = a.shape
    n = w_slots.shape[1]
    tm = _tile(S, ROW_TILE)
    extra = epilogue[1] if epilogue else []

    def body(*refs):
        a_ref, w_ref, o_ref = refs[0], refs[1], refs[-1]
        av = a_ref[...].astype(BF16)
        tables = [r[...] for r in refs[2:-1]]
        for j in range(N_DEV):
            out = jnp.dot(av, w_ref[j], preferred_element_type=F32)
            if epilogue:
                out = epilogue[0](out, *tables)
            o_ref[j] = out.astype(o_ref.dtype)

    return _pcall(
        body, name=name, grid=(S // tm,),
        in_specs=[pl.BlockSpec((tm, K), lambda i: (i, 0)), pl.BlockSpec((N_DEV, K, n), lambda i: (0, 0, 0))]
        + (list(epilogue[2]) if epilogue else []),
        out_specs=pl.BlockSpec((N_DEV, tm, n), lambda i: (0, i, 0)),
        out_shape=jax.ShapeDtypeStruct((N_DEV, S, n), out_dtype),
        compiler_params=_params(("parallel",)),
    )(a, w_slots.reshape(N_DEV, K, n), *extra).reshape(N_DEV * S, n)


def _small_slots_in_nt(name, a_slots, w_slots, S):
    n = a_slots.shape[1]
    K = w_slots.shape[0] // N_DEV
    tm = _tile(S, ROW_TILE)

    def body(a_ref, w_ref, o_ref):
        total = lax.dot_general(a_ref[0], w_ref[0], _DIMS["nt"], preferred_element_type=F32)
        for j in range(1, N_DEV):
            total += lax.dot_general(a_ref[j], w_ref[j], _DIMS["nt"], preferred_element_type=F32)
        o_ref[...] = total

    return _pcall(
        body, name=name, grid=(S // tm,),
        in_specs=[pl.BlockSpec((N_DEV, tm, n), lambda i: (0, i, 0)), pl.BlockSpec((N_DEV, K, n), lambda i: (0, 0, 0))],
        out_specs=pl.BlockSpec((tm, K), lambda i: (i, 0)),
        out_shape=jax.ShapeDtypeStruct((S, K), F32),
        compiler_params=_params(("parallel",)),
    )(a_slots.reshape(N_DEV, S, n), w_slots.reshape(N_DEV, K, n))


def _small_slots_dw(name, a, d_slots, out_dtype):
    S, K = a.shape
    n = d_slots.shape[1]
    tk = _tile(S, ROW_TILE)
    nkb = S // tk

    def body(a_ref, d_ref, o_ref, acc_ref):
        k = pl.program_id(0)

        @pl.when(k == 0)
        def _():
            acc_ref[...] = jnp.zeros_like(acc_ref)

        at = a_ref[...].astype(F32).T.astype(BF16)
        for j in range(N_DEV):
            acc_ref[j] += jnp.dot(at, d_ref[j], preferred_element_type=F32)

        @pl.when(k == nkb - 1)
        def _():
            o_ref[...] = acc_ref[...].astype(o_ref.dtype)

    return _pcall(
        body, name=name, grid=(nkb,),
        in_specs=[pl.BlockSpec((tk, K), lambda k: (k, 0)), pl.BlockSpec((N_DEV, tk, n), lambda k: (0, k, 0))],
        out_specs=pl.BlockSpec((N_DEV, K, n), lambda k: (0, 0, 0)),
        out_shape=jax.ShapeDtypeStruct((N_DEV, K, n), out_dtype),
        scratch_shapes=[pltpu.VMEM((N_DEV, K, n), F32)],
        compiler_params=_params(("arbitrary",)),
    )(a, d_slots.reshape(N_DEV, S, n)).reshape(N_DEV * K, n)


def _sigmoid(x):
    return 1.0 / (1.0 + jnp.exp(-x))


def _ffn_up(h, wg, wu, name, ride=None):
    S, D = h.shape
    fs = wg.shape[1]
    tm = _tile(S, ROW_TILE)
    sp = FFN_SLOTS

    def body(ins, outs, scratch):
        h_ref, wg_ref, wu_ref = ins
        g_ref, u_ref, a_ref = outs
        hv = h_ref[...]
        for s in range(sp):
            g = jnp.dot(hv, wg_ref[s], preferred_element_type=F32)
            u = jnp.dot(hv, wu_ref[s], preferred_element_type=F32)
            g_ref[s] = g.astype(g_ref.dtype)
            u_ref[s] = u.astype(u_ref.dtype)
            a_ref[s] = (g * _sigmoid(g) * u).astype(a_ref.dtype)

    w_spec = pl.BlockSpec((sp, D, fs), lambda i, j: (j, 0, 0))
    o_spec = pl.BlockSpec((sp, tm, fs), lambda i, j: (j, i, 0))
    sds = jax.ShapeDtypeStruct((N_DEV, S, fs), BF16)
    g, u, a, *rest = _pallas(body, name, (S // tm, N_DEV // sp), [h, wg.reshape(N_DEV, D, fs), wu.reshape(N_DEV, D, fs)],
                             [pl.BlockSpec((tm, D), lambda i, j: (i, 0)), w_spec, w_spec], [o_spec, o_spec, o_spec],
                             [sds, sds, sds], sem=("parallel", "arbitrary"), ride=ride)
    return [t.reshape(N_DEV * S, fs) for t in (g, u, a)] + rest


def _ffn_dact(dy, wd, g, u, name, ride=None):
    S, D = dy.shape
    fs = g.shape[1]
    tm = _tile(S, ROW_TILE)
    sp = FFN_SLOTS

    def body(ins, outs, scratch):
        dy_ref, wd_ref, g_ref, u_ref = ins
        dg_ref, du_ref = outs
        dy = dy_ref[...]
        for s in range(sp):
            da = lax.dot_general(dy, wd_ref[s], _DIMS["nt"], preferred_element_type=F32)
            g, u = g_ref[s].astype(F32), u_ref[s].astype(F32)
            sig = _sigmoid(g)
            dg_ref[s] = (da * u * (sig * (1.0 + g * (1.0 - sig)))).astype(dg_ref.dtype)
            du_ref[s] = (da * (g * sig)).astype(du_ref.dtype)

    o_spec = pl.BlockSpec((sp, tm, fs), lambda j, i: (j, i, 0))
    sds = jax.ShapeDtypeStruct((N_DEV, S, fs), BF16)
    dg, du, *rest = _pallas(body, name, (N_DEV // sp, S // tm),
                            [dy, wd.reshape(N_DEV, fs, D), g.reshape(N_DEV, S, fs), u.reshape(N_DEV, S, fs)],
                            [pl.BlockSpec((tm, D), lambda j, i: (i, 0)), pl.BlockSpec((sp, fs, D), lambda j, i: (j, 0, 0)),
                             o_spec, o_spec], [o_spec, o_spec], [sds, sds], sem=("parallel", "arbitrary"), ride=ride)
    return [dg.reshape(N_DEV * S, fs), du.reshape(N_DEV * S, fs)] + rest


def _ffn_down(a, wd, x, name):
    S, D = x.shape
    fs = a.shape[1]
    tm = _tile(S, ROW_TILE)
    sp = FFN_SLOTS
    nj = N_DEV // sp

    def body(a_ref, w_ref, x_ref, o_ref, acc_ref):
        j = pl.program_id(1)

        @pl.when(j == 0)
        def _():
            acc_ref[...] = x_ref[...]

        part = jnp.dot(a_ref[0], w_ref[0], preferred_element_type=F32)
        for s in range(1, sp):
            part += jnp.dot(a_ref[s], w_ref[s], preferred_element_type=F32)
        acc_ref[...] += part

        @pl.when(j == nj - 1)
        def _():
            o_ref[...] = acc_ref[...]

    row = pl.BlockSpec((tm, D), lambda i, j: (i, 0))
    return _pcall(
        body, name=name, grid=(S // tm, nj),
        in_specs=[pl.BlockSpec((sp, tm, fs), lambda i, j: (j, i, 0)), pl.BlockSpec((sp, fs, D), lambda i, j: (j, 0, 0)), row],
        out_specs=row, out_shape=jax.ShapeDtypeStruct((S, D), F32),
        scratch_shapes=[pltpu.VMEM((tm, D), F32)],
        compiler_params=_params(("parallel", "arbitrary")),
    )(a.reshape(N_DEV, S, fs), wd.reshape(N_DEV, fs, D), x)


def _ffn_dh(dg, du, wg, wu, S, name, ride=None):
    fs = dg.shape[1]
    D = wg.shape[0] // N_DEV
    tm = _tile(S, ROW_TILE)
    sp = FFN_SLOTS
    nj = N_DEV // sp

    def body(ins, outs, scratch):
        dg_ref, du_ref, wg_ref, wu_ref = ins
        o_ref, acc_ref = outs[0], scratch[0]
        j = pl.program_id(1)

        @pl.when(j == 0)
        def _():
            acc_ref[...] = jnp.zeros_like(acc_ref)

        part = None
        for s in range(sp):
            for d_ref, w_ref in ((dg_ref, wg_ref), (du_ref, wu_ref)):
                term = lax.dot_general(d_ref[s], w_ref[s], _DIMS["nt"], preferred_element_type=F32)
                part = term if part is None else part + term
        acc_ref[...] += part

        @pl.when(j == nj - 1)
        def _():
            o_ref[...] = acc_ref[...]

    d_spec = pl.BlockSpec((sp, tm, fs), lambda i, j: (j, i, 0))
    w_spec = pl.BlockSpec((sp, D, fs), lambda i, j: (j, 0, 0))
    return _pallas(body, name, (S // tm, nj),
                   [dg.reshape(N_DEV, S, fs), du.reshape(N_DEV, S, fs), wg.reshape(N_DEV, D, fs), wu.reshape(N_DEV, D, fs)],
                   [d_spec, d_spec, w_spec, w_spec], [pl.BlockSpec((tm, D), lambda i, j: (i, 0))],
                   [jax.ShapeDtypeStruct((S, D), F32)], scratch=[pltpu.VMEM((tm, D), F32)],
                   sem=("parallel", "arbitrary"), ride=ride)


def _slots_dw_t(name, at, d_slots, ride=None):
    K, S = at.shape
    n = d_slots.shape[1]
    tk = _tile(S, 4 * ROW_TILE)
    nkb = S // tk
    return _mm(name, "nn", at, d_slots, (K, n, tk), (N_DEV, 1, nkb),
               lambda j, q, k: (0, k), lambda j, q, k: (j * nkb + k, 0), lambda j, q, k: (j, 0),
               (N_DEV * K, n), BF16, ride=ride)


def _rope_heads(q, cos_t, sin_t):
    parts = []
    for c0 in range(0, q.shape[1], HEAD_PAD):
        parts += [q[:, c0:c0 + QK_NOPE_DIM], _rope(q[:, c0 + QK_NOPE_DIM:c0 + HEAD_PAD], cos_t, sin_t)]
    return jnp.concatenate(parts, axis=1)


def _kv_mid(kv, g, cos_t, sin_t):
    S, W = kv.shape
    R = W - 128
    ts = _tile(S, ROW_TILE)

    def body(kv_ref, g_ref, c_ref, s_ref, c_out, k_out):
        cf = kv_ref[:, :R]
        c_out[...] = (cf * _rms_scale(cf) * g_ref[...]).astype(c_out.dtype)
        k_out[...] = _rope(kv_ref[:, R:], c_ref[...], s_ref[...]).astype(k_out.dtype)

    tab = pl.BlockSpec((ts, 128), lambda i: (i, 0))
    return _pcall(
        body, name="kv_mid", grid=(S // ts,),
        in_specs=[pl.BlockSpec((ts, W), lambda i: (i, 0)), pl.BlockSpec((1, R), lambda i: (0, 0)), tab, tab],
        out_specs=(pl.BlockSpec((ts, R), lambda i: (i, 0)), tab),
        out_shape=(jax.ShapeDtypeStruct((S, R), BF16), jax.ShapeDtypeStruct((S, 128), BF16)),
        compiler_params=_params(("parallel",)),
    )(kv, g, cos_t, sin_t)


def _kv_mid_bwd(kv, g, dc, dkpe, cos_t, sin_t):
    S, W = kv.shape
    R = W - 128
    H = dkpe.shape[0]
    ts = _row_tile(S, H * 128)

    def body(kv_ref, g_ref, dc_ref, dk_ref, c_ref, s_ref, dkv_ref, dg_ref):
        dx, dgt = _rms_bwd(kv_ref[:, :R], g_ref[...], dc_ref[...])
        dkv_ref[:, :R] = dx.astype(dkv_ref.dtype)
        dk = dk_ref[0]
        for h in range(1, H):
            dk = dk + dk_ref[h]
        dkv_ref[:, R:] = _rope(dk, c_ref[...], -s_ref[...]).astype(dkv_ref.dtype)

        @pl.when(pl.program_id(0) == 0)
        def _():
            dg_ref[...] = jnp.zeros_like(dg_ref)

        dg_ref[...] += jnp.sum(dgt, axis=0, keepdims=True)

    tab = pl.BlockSpec((ts, 128), lambda i: (i, 0))
    vec = pl.BlockSpec((1, R), lambda i: (0, 0))
    return _pcall(
        body, name="kv_mid_bwd", grid=(S // ts,),
        in_specs=[pl.BlockSpec((ts, W), lambda i: (i, 0)), vec, pl.BlockSpec((ts, R), lambda i: (i, 0)),
                  pl.BlockSpec((H, ts, 128), lambda i: (0, i, 0)), tab, tab],
        out_specs=(pl.BlockSpec((ts, W), lambda i: (i, 0)), vec),
        out_shape=(jax.ShapeDtypeStruct((S, W), BF16), jax.ShapeDtypeStruct((1, R), F32)),
        compiler_params=_params(("arbitrary",)),
    )(kv, g, dc, dkpe, cos_t, sin_t)


def _block_pairs(nb, by_key):
    pairs = ([(qi, ki) for ki in range(nb) for qi in range(ki, nb)] if by_key
             else [(qi, ki) for qi in range(nb) for ki in range(qi + 1)])
    return jnp.asarray(np.array([p[0] for p in pairs], np.int32)), jnp.asarray(np.array([p[1] for p in pairs], np.int32))


def _causal_mask(blk, transposed=False, rows=None, row0=0):
    shape = (blk if rows is None else rows, blk)
    r = lax.broadcasted_iota(jnp.int32, shape, 0) + row0
    c = lax.broadcasted_iota(jnp.int32, shape, 1)
    return (r <= c) if transposed else (c <= r)


def _attn_specs(S, blk, hpd):
    nb = S // blk
    w = hpd * HEAD_PAD
    return dict(
        q=pl.BlockSpec((blk, w), lambda j, t, qt, kt: (j * nb + qt[t], 0)),
        kv=pl.BlockSpec((blk, w), lambda j, t, qt, kt: (j * nb + kt[t], 0)),
        kp=pl.BlockSpec((blk, 128), lambda j, t, qt, kt: (kt[t], 0)),
        do=pl.BlockSpec((blk, hpd * V_DIM), lambda j, t, qt, kt: (qt[t], j)),
        col=pl.BlockSpec((hpd, blk, 1), lambda j, t, qt, kt: (j, qt[t], 0)),
        row=pl.BlockSpec((hpd, 1, blk), lambda j, t, qt, kt: (j, 0, qt[t])))


def _head_k(kv_ref, kp, hh):
    return jnp.concatenate([kv_ref[:, hh * HEAD_PAD:hh * HEAD_PAD + QK_NOPE_DIM], kp], axis=1)


def _head_v(kv_ref, hh):
    return kv_ref[:, hh * HEAD_PAD + QK_NOPE_DIM:(hh + 1) * HEAD_PAD]


def _attn_fwd(q, kvu, kpe, H, ride=None):
    S = kpe.shape[0]
    hpd = H // N_DEV
    blk = _tile(S, ATT_BLOCK)
    q_tab, k_tab = _block_pairs(S // blk, by_key=False)
    rc = min(blk, ATT_ROW_CHUNK)

    def body(ins, outs, scratch):
        qt, kt, q_ref, kv_ref, kp_ref = ins
        o_ref, lse_ref = outs
        m_ref, acc_ref = scratch
        t = pl.program_id(1)
        qi, ki = qt[t], kt[t]

        @pl.when(ki == 0)
        def _():
            m_ref[...] = jnp.full_like(m_ref, -jnp.inf)
            acc_ref[...] = jnp.zeros_like(acc_ref)

        def step(masked):
            kp = kp_ref[...]
            ones = jnp.ones((blk, 128), BF16)
            ks = [_head_k(kv_ref, kp, hh) for hh in range(hpd)]
            v_exts = [jnp.concatenate([_head_v(kv_ref, hh), ones], axis=1) for hh in range(hpd)]
            chains = [(hh, r0) for r0 in range(0, blk, rc) for hh in range(hpd)]

            def scores(hh, r0):
                return lax.dot_general(q_ref[r0:r0 + rc, hh * HEAD_PAD:(hh + 1) * HEAD_PAD], ks[hh], _DIMS["nt"],
                                       preferred_element_type=F32)

            ahead = [scores(*ch) for ch in chains[:ATT_LOOKAHEAD]]
            for n, (hh, r0) in enumerate(chains):
                rows = slice(r0, r0 + rc)
                s = ahead.pop(0)
                if n + ATT_LOOKAHEAD < len(chains):
                    ahead.append(scores(*chains[n + ATT_LOOKAHEAD]))
                if masked:
                    s = jnp.where(_causal_mask(blk, rows=rc, row0=r0), s, -jnp.inf)
                m_old = m_ref[hh, rows]
                m_new = jnp.maximum(m_old, jnp.max(s, axis=-1, keepdims=True))
                p = jnp.exp2((s - m_new) * EXP2_SCALE).astype(BF16)
                acc_ref[hh, rows] = (jnp.exp2((m_old - m_new) * EXP2_SCALE) * acc_ref[hh, rows]
                                     + jnp.dot(p, v_exts[hh], preferred_element_type=F32))
                m_ref[hh, rows] = m_new

        @pl.when(ki < qi)
        def _():
            step(False)

        @pl.when(ki == qi)
        def _():
            step(True)
            for hh in range(hpd):
                l = acc_ref[hh, :, V_DIM:]
                o_ref[:, hh * V_DIM:(hh + 1) * V_DIM] = (acc_ref[hh, :, :V_DIM] / l).astype(o_ref.dtype)
                lse_ref[hh] = m_ref[hh] * EXP2_SCALE + jnp.log2(l[:, :1])

    sp = _attn_specs(S, blk, hpd)
    return _pallas(body, "attn_fwd", (N_DEV, q_tab.shape[0]), [q, kvu, kpe], [sp["q"], sp["kv"], sp["kp"]],
                   [sp["do"], sp["col"]],
                   [jax.ShapeDtypeStruct((S, H * V_DIM), BF16), jax.ShapeDtypeStruct((H, S, 1), F32)],
                   scratch=[pltpu.VMEM((hpd, blk, 1), F32), pltpu.VMEM((hpd, blk, 2 * V_DIM), F32)],
                   sem=("parallel", "arbitrary"), ride=ride, prefetch=[q_tab, k_tab])


def _attn_bwd(q, kvu, kpe, do, o, lse_row, cos_t, sin_t, H):
    S = kpe.shape[0]
    hpd = H // N_DEV
    blk = _tile(S, ATT_BLOCK)
    nb = S // blk
    q_tab, k_tab = _block_pairs(nb, by_key=True)

    def body(qt, kt, q_ref, kv_ref, kp_ref, do_ref, o_ref, lse_ref, c_ref, s_ref, dq_ref, dkv_ref, dkp_ref,
             dq_acc, dk_acc, dv_acc):
        t = pl.program_id(1)
        qi, ki = qt[t], kt[t]
        q_rows = pl.ds(pl.multiple_of(qi * blk, blk), blk)

        def step(diag):
            kp = kp_ref[...]
            ones = jnp.ones((8, V_DIM), BF16)
            for hh in range(hpd):
                k = _head_k(kv_ref, kp, hh)
                qv = q_ref[:, hh * HEAD_PAD:(hh + 1) * HEAD_PAD]
                cols = slice(hh * V_DIM, (hh + 1) * V_DIM)
                dov = do_ref[:, cols]
                doo = (dov.astype(F32) * o_ref[:, cols].astype(F32)).astype(BF16)
                delta = lax.dot_general(ones, doo, _DIMS["nt"], preferred_element_type=F32)[:1]
                st = lax.dot_general(k, qv, _DIMS["nt"], preferred_element_type=F32)
                pt = jnp.exp2(st * EXP2_SCALE - lse_ref[hh])
                if diag:
                    pt = jnp.where(_causal_mask(blk, transposed=True), pt, 0.0)
                dv_new = jnp.dot(pt.astype(BF16), dov, preferred_element_type=F32)
                dpt = lax.dot_general(_head_v(kv_ref, hh), dov, _DIMS["nt"], preferred_element_type=F32)
                dst = (pt * (dpt - delta)).astype(BF16)
                dk_new = jnp.dot(dst, qv, preferred_element_type=F32)
                dq_new = lax.dot_general(dst, k, _DIMS["tn"], preferred_element_type=F32)
                if diag:
                    dv_acc[hh] = dv_new
                    dk_acc[hh] = dk_new
                else:
                    dv_acc[hh] += dv_new
                    dk_acc[hh] += dk_new
                dq_acc[hh, q_rows] += dq_new

        @pl.when(t == 0)
        def _():
            dq_acc[...] = jnp.zeros_like(dq_acc)

        @pl.when(qi == ki)
        def _():
            step(True)
            for hh in range(hpd):
                c0 = hh * HEAD_PAD
                dq = dq_acc[hh, q_rows] * ATTN_SCALE
                dq_ref[:, c0:c0 + QK_NOPE_DIM] = dq[:, :QK_NOPE_DIM].astype(dq_ref.dtype)
                dq_ref[:, c0 + QK_NOPE_DIM:c0 + HEAD_PAD] = _rope(dq[:, QK_NOPE_DIM:], c_ref[...],
                                                                  -s_ref[...]).astype(dq_ref.dtype)

        @pl.when(qi > ki)
        def _():
            step(False)

        @pl.when(qi == nb - 1)
        def _():
            for hh in range(hpd):
                c0 = hh * HEAD_PAD
                dkv_ref[:, c0:c0 + QK_NOPE_DIM] = (dk_acc[hh, :, :QK_NOPE_DIM] * ATTN_SCALE).astype(dkv_ref.dtype)
                dkv_ref[:, c0 + QK_NOPE_DIM:c0 + HEAD_PAD] = dv_acc[hh].astype(dkv_ref.dtype)
                dkp_ref[hh] = dk_acc[hh, :, QK_NOPE_DIM:] * ATTN_SCALE

    sp = _attn_specs(S, blk, hpd)
    key_tab = pl.BlockSpec((blk, 128), lambda j, t, qt, kt: (kt[t], 0))
    grid_spec = pltpu.PrefetchScalarGridSpec(
        num_scalar_prefetch=2, grid=(N_DEV, q_tab.shape[0]),
        in_specs=[sp["q"], sp["kv"], sp["kp"], sp["do"], sp["do"], sp["row"], key_tab, key_tab],
        out_specs=(sp["kv"], sp["kv"], pl.BlockSpec((hpd, blk, 128), lambda j, t, qt, kt: (j, kt[t], 0))),
        scratch_shapes=[pltpu.VMEM((hpd, S, HEAD_PAD), F32), pltpu.VMEM((hpd, blk, HEAD_PAD), F32),
                        pltpu.VMEM((hpd, blk, V_DIM), F32)])
    return _pcall(
        body, name="attn_bwd", grid_spec=grid_spec,
        out_shape=(jax.ShapeDtypeStruct(q.shape, BF16), jax.ShapeDtypeStruct(kvu.shape, BF16),
                   jax.ShapeDtypeStruct((H, S, 128), F32)),
        compiler_params=_params(("parallel", "arbitrary")),
    )(q_tab, k_tab, q, kvu, kpe, do, o, lse_row, cos_t, sin_t)


def _mesh_pos():
    return lax.axis_index("x"), lax.axis_index("y"), lax.axis_index("c")


def _flip(pos, k):
    x, y, c = pos
    return (1 - x if k & 4 else x, 1 - y if k & 2 else y, 1 - c if k & 1 else c)


def _rank(pos):
    return 4 * pos[0] + 2 * pos[1] + pos[2]


def _copies(n, src_of, dst_refs, local_src_of, send_sems, recv_sems, local_sems):
    me = _mesh_pos()
    local = [pltpu.make_async_copy(local_src_of(a), dst_refs[a].at[_rank(me)], local_sems.at[a]) for a in range(n)]
    sends, arrivals = [], []
    for k in range(1, N_DEV):
        peer = _flip(me, k)
        for a in range(n):
            idx = a * (N_DEV - 1) + k - 1
            for into, group in ((me, sends), (peer, arrivals)):
                group.append(pltpu.make_async_remote_copy(
                    src_ref=src_of(a, peer), dst_ref=dst_refs[a].at[_rank(into)],
                    send_sem=send_sems.at[idx], recv_sem=recv_sems.at[idx],
                    device_id=peer, device_id_type=pl.DeviceIdType.MESH))
    return local, sends, arrivals


def _exchange_start(*args):
    local, sends, _ = _copies(*args)
    for cp in local + sends:
        cp.start()


def _exchange_wait(*args):
    local, sends, arrivals = _copies(*args)
    for cp in arrivals:
        cp.wait_recv()
    for cp in sends:
        cp.wait_send()
    for cp in local:
        cp.wait()


def _exchange(*args):
    _exchange_start(*args)
    _exchange_wait(*args)


def _sems(n):
    return [pltpu.SemaphoreType.DMA((n * (N_DEV - 1),)), pltpu.SemaphoreType.DMA((n * (N_DEV - 1),)),
            pltpu.SemaphoreType.DMA((n,))]


_ANY = pl.BlockSpec(memory_space=pl.ANY)


class _Ride:
    def __init__(self, kind, arrays):
        self.kind, self.arrays, self.n = kind, list(arrays), len(arrays)

    def out_shape(self):
        lead = () if self.kind == "scatter" else (N_DEV,)
        return [jax.ShapeDtypeStruct(lead + a.shape, a.dtype) for a in self.arrays]

    def _args(self, x_refs, out_refs, sems):
        if self.kind == "gather":
            return (self.n, lambda a, peer: x_refs[a], out_refs, lambda a: x_refs[a]) + tuple(sems)
        me = _mesh_pos()
        return (self.n, lambda a, peer: x_refs[a].at[_rank(peer)], out_refs, lambda a: x_refs[a].at[_rank(me)]) + tuple(sems)

    def start(self, x_refs, out_refs, sems):
        if self.kind == "gather2":
            _two_level_gather(x_refs, out_refs, *sems)[0]()
        else:
            _exchange_start(*self._args(x_refs, out_refs, sems))

    def wait(self, x_refs, out_refs, sems):
        if self.kind == "gather2":
            _two_level_gather(x_refs, out_refs, *sems)[1]()
        else:
            _exchange_wait(*self._args(x_refs, out_refs, sems))


def _pallas(body, name, grid, operands, in_specs, out_specs, out_shape, scratch=(), sem=None, ride=None, prefetch=()):
    n_pf, n_in, n_out, n_scr = len(prefetch), len(in_specs), len(out_specs), len(scratch)
    nr = ride.n if ride else 0

    def wrapped(*refs):
        refs = list(refs)
        pf, refs = refs[:n_pf], refs[n_pf:]
        ins, r_in = refs[:n_in], refs[n_in:n_in + nr]
        outs, r_out = refs[n_in + nr:n_in + nr + n_out], refs[n_in + nr + n_out:n_in + 2 * nr + n_out]
        rest = refs[n_in + 2 * nr + n_out:]
        scr, sems = rest[:n_scr], rest[n_scr:]
        if ride:
            first, last = None, None
            for d, size in enumerate(grid):
                i = pl.program_id(d)
                first = (i == 0) if first is None else jnp.logical_and(first, i == 0)
                last = (i == size - 1) if last is None else jnp.logical_and(last, i == size - 1)

            @pl.when(first)
            def _():
                ride.start(r_in, r_out, sems)

        body(pf + ins, outs, scr)
        if ride:
            @pl.when(last)
            def _():
                ride.wait(r_in, r_out, sems)

    grid_spec = pltpu.PrefetchScalarGridSpec(
        num_scalar_prefetch=n_pf, grid=grid,
        in_specs=list(in_specs) + [_ANY] * nr, out_specs=tuple(list(out_specs) + [_ANY] * nr),
        scratch_shapes=list(scratch) + (_sems(nr) if ride else []))
    sem = tuple(sem) if sem is not None and not ride else ("arbitrary",) * len(grid)
    return list(_pcall(
        wrapped, name=name, grid_spec=grid_spec,
        out_shape=tuple(list(out_shape) + (ride.out_shape() if ride else [])),
        compiler_params=_params(sem),
    )(*(list(prefetch) + list(operands) + (ride.arrays if ride else []))))


def _two_level_gather(x_refs, out_refs, send_sems, recv_sems, local_sems):
    n = len(x_refs)
    x, y, c = _mesh_pos()
    me, sibling = (x, y, c), (x, y, 1 - c)
    chips = [(1 - x, y), (x, 1 - y), (1 - x, 1 - y)]

    def copy(a, k, block, to, src=None):
        rows = out_refs[a].at[_rank(block)]
        return pltpu.make_async_remote_copy(
            src_ref=rows if src is None else src, dst_ref=rows,
            send_sem=send_sems.at[a * (N_DEV - 1) + k], recv_sem=recv_sems.at[a * (N_DEV - 1) + k],
            device_id=to, device_id_type=pl.DeviceIdType.MESH)

    def own_copies():
        mine = [pltpu.make_async_copy(x_refs[a], out_refs[a].at[_rank(me)], local_sems.at[a]) for a in range(n)]
        first = []
        for a in range(n):
            first.append(copy(a, 0, me, sibling, src=x_refs[a]))
            first += [copy(a, 1 + j, me, (*chip, c), src=x_refs[a]) for j, chip in enumerate(chips)]
        return mine, first

    def start():
        mine, first = own_copies()
        for cp in mine + first:
            cp.start()

    def finish():
        mine, first = own_copies()
        passed = []
        for j, chip in enumerate(chips):
            for a in range(n):
                copy(a, 1 + j, (*chip, c), me).wait_recv()
                passed.append(copy(a, 4 + j, (*chip, c), sibling))
                passed[-1].start()
        for a in range(n):
            copy(a, 0, sibling, me).wait_recv()
            for j, chip in enumerate(chips):
                copy(a, 4 + j, (*chip, 1 - c), me).wait_recv()
        for cp in first + passed:
            cp.wait_send()
        for cp in mine:
            cp.wait()

    return start, finish


def _all_gather(shards, name):
    n = len(shards)

    def body(*refs):
        start, finish = _two_level_gather(refs[:n], refs[n:2 * n], *refs[2 * n:])
        start()
        finish()

    return _pcall(
        body, name=name, in_specs=[_ANY] * n, out_specs=tuple([_ANY] * n),
        out_shape=tuple(jax.ShapeDtypeStruct((N_DEV,) + s.shape, s.dtype) for s in shards),
        scratch_shapes=_sems(n),
    )(*shards)


def _all_reduce_small(v):
    R, C = v.shape

    def body(x_ref, out_ref, buf_ref, send_sems, recv_sems, local_sems):
        _exchange(1, lambda a, peer: x_ref, [buf_ref], lambda a: x_ref, send_sems, recv_sems, local_sems)
        total = buf_ref[0]
        for j in range(1, N_DEV):
            total = total + buf_ref[j]
        out_ref[...] = total

    vm = pl.BlockSpec(memory_space=pltpu.VMEM)
    return _pcall(
        body, name="all_reduce_small", in_specs=[vm], out_specs=vm,
        out_shape=jax.ShapeDtypeStruct((R, C), F32),
        scratch_shapes=[pltpu.VMEM((N_DEV, R, C), F32)] + _sems(1),
    )(v)


def _sum_slots(layers, name):
    L = len(layers)
    shape = layers[0].shape[1:]
    C = shape[-1]
    R = layers[0].size // (N_DEV * C)
    tr = R
    while N_DEV * tr * C * 4 > (4 << 20) and tr % 32 == 0:
        tr //= 2

    def body(*refs):
        o_ref = refs[L]
        for li in range(L):
            @pl.when(pl.program_id(0) == li)
            def _():
                total = refs[li][0].astype(F32)
                for j in range(1, N_DEV):
                    total = total + refs[li][j].astype(F32)
                o_ref[0] = total

    in_specs = [pl.BlockSpec((N_DEV, tr, C), lambda l, i, li=li: (0, jnp.where(l == li, i, 0), 0)) for li in range(L)]
    return _pcall(
        body, name=name, grid=(L, R // tr),
        in_specs=in_specs,
        out_specs=pl.BlockSpec((1, tr, C), lambda l, i: (l, i, 0)),
        out_shape=jax.ShapeDtypeStruct((L, R, C), F32),
        compiler_params=_params(("parallel", "parallel")),
    )(*[p.reshape(N_DEV, R, C) for p in layers]).reshape((L,) + shape)


def _adamw_update(w, g, m, v):
    nm = ADAM_B1 * m + (1.0 - ADAM_B1) * g
    nv = ADAM_B2 * v + (1.0 - ADAM_B2) * (g * g)
    m_hat = nm / (1.0 - ADAM_B1 ** ADAM_STEP)
    v_hat = nv / (1.0 - ADAM_B2 ** ADAM_STEP)
    return -ADAM_LR * (m_hat / (jnp.sqrt(v_hat) + ADAM_EPS) + ADAM_WD * w), nm, nv


def _adamw_from_slots(w, layers, m, v, name, ride=None):
    L = len(layers)
    shape = w.shape
    C = shape[-1]
    R = w.size // (L * C)
    fits = [t for t in range(16, R + 1, 16) if R % t == 0 and N_DEV * t * C * 2 <= (1 << 20)]
    tr = max(fits) if fits else R

    def body(ins, outs, scratch):
        w_ref, m_ref, v_ref = ins[L:]
        g_ref, d_ref, nm_ref, nv_ref = outs
        for li in range(L):
            @pl.when(pl.program_id(0) == li)
            def _():
                g = ins[li][0].astype(F32)
                for j in range(1, N_DEV):
                    g = g + ins[li][j].astype(F32)
                g_ref[0] = g
                d_ref[0], nm_ref[0], nv_ref[0] = _adamw_update(w_ref[0], g, m_ref[0], v_ref[0])

    slot_specs = [pl.BlockSpec((N_DEV, tr, C), lambda l, i, li=li: (0, jnp.where(l == li, i, 0), 0)) for li in range(L)]
    blk = pl.BlockSpec((1, tr, C), lambda l, i: (l, i, 0))
    sds = jax.ShapeDtypeStruct((L, R, C), F32)
    outs = _pallas(body, name, (L, R // tr),
                   [p.reshape(N_DEV, R, C) for p in layers] + [t.reshape(L, R, C) for t in (w, m, v)],
                   slot_specs + [blk] * 3, [blk] * 4, [sds] * 4, sem=("parallel", "parallel"), ride=ride)
    return [o.reshape(shape) for o in outs[:4]] + outs[4:]


def _adamw(w, g, m, v, name):
    shape = w.shape
    C = shape[-1]
    R = w.size // C
    tr = R
    while tr * C * 4 > (1 << 20) and tr % 16 == 0:
        tr //= 2

    def body(w_ref, g_ref, m_ref, v_ref, d_ref, nm_ref, nv_ref):
        d_ref[...], nm_ref[...], nv_ref[...] = _adamw_update(w_ref[...], g_ref[...], m_ref[...], v_ref[...])

    blk = pl.BlockSpec((tr, C), lambda i: (i, 0))
    sds = jax.ShapeDtypeStruct((R, C), F32)
    outs = _pcall(
        body, name=name, grid=(R // tr,),
        in_specs=[blk] * 4, out_specs=(blk,) * 3, out_shape=(sds,) * 3,
        compiler_params=_params(("parallel",)),
    )(*(t.reshape(R, C) for t in (w, g, m, v)))
    return tuple(o.reshape(shape) for o in outs)


_REPLICATED = ("kv_in_norm", "kv_latent_norm", "attn_norm", "q_latent_norm", "ffn_norm", "final_norm")
_WEIGHTS = ("pool_norm", "pool_w", "pool_scale", "kv_in_norm", "w_kv_a", "kv_latent_norm", "w_kv_b", "attn_norm",
            "w_q_a", "q_latent_norm", "w_q_b", "w_o", "ffn_norm", "w_gate", "w_up", "w_down", "final_norm")


def _ffn_fwd(x, norm_g, wg, wu, wd, tag, gather=None):
    S, D = x.shape
    fs = wg.shape[1]
    h, ht = _rmsnorm(x, norm_g, "ffn_norm" + tag, transposed_too=True)
    g, u, a, *gathered = _ffn_up(h, wg, wu, "ffn_up" + tag, ride=gather)
    if gather is not None:
        wd = gathered[0].reshape(N_DEV * fs, D)
    y = _ffn_down(a, wd, x, "ffn_down" + tag)
    return y, (ht, g, u, a), wd, gathered


def _ffn_bwd(x, norm_g, wg, wu, wd, saved, dy, dyb, tag, with_bf16, scatter=None):
    S, D = x.shape
    ht, g, u, a = saved
    fs = g.shape[1]
    tk = _tile(S, 4 * ROW_TILE)
    nkb = S // tk
    dg, du, *got_rider = _ffn_dact(dyb, wd, g, u, "ffn_dact" + tag, ride=scatter)
    dwd = _mm("ffn_dwd" + tag, "tn", a, dyb, (fs, D, tk), (N_DEV, 1, nkb),
              lambda j, q, k: (j * nkb + k, 0), lambda j, q, k: (k, 0), lambda j, q, k: (j, 0), (N_DEV * fs, D), BF16)
    dwg, (got_dwd,) = _slots_dw_t("ffn_dwg" + tag, ht, dg, ride=_Ride("scatter", [dwd.reshape(N_DEV, fs, D)]))
    dwu, (got_dwg,) = _slots_dw_t("ffn_dwu" + tag, ht, du, ride=_Ride("scatter", [dwg.reshape(N_DEV, D, fs)]))
    dh, got_dwu = _ffn_dh(dg, du, wg, wu, S, "ffn_dh" + tag, ride=_Ride("scatter", [dwu.reshape(N_DEV, D, fs)]))
    outs = _rmsnorm_bwd(x, norm_g, dh, dy, "ffn_norm_bwd" + tag, with_bf16=with_bf16)
    return outs, got_rider, (got_dwg, got_dwu, got_dwd)


def kernel(x, positions, pool_norm, pool_w, pool_scale, kv_in_norm, w_kv_a, kv_latent_norm, w_kv_b, attn_norm, w_q_a, q_latent_norm, w_q_b, w_o, ffn_norm, w_gate, w_up, w_down, final_norm, loss_target, m_pool_norm, m_pool_w, m_pool_scale, m_kv_in_norm, m_w_kv_a, m_kv_latent_norm, m_w_kv_b, m_attn_norm, m_w_q_a, m_q_latent_norm, m_w_q_b, m_w_o, m_ffn_norm, m_w_gate, m_w_up, m_w_down, m_final_norm, v_pool_norm, v_pool_w, v_pool_scale, v_kv_in_norm, v_w_kv_a, v_kv_latent_norm, v_w_kv_b, v_attn_norm, v_w_q_a, v_q_latent_norm, v_w_q_b, v_w_o, v_ffn_norm, v_w_gate, v_w_up, v_w_down, v_final_norm):
    env = dict(locals())
    weights = {n: env[n] for n in _WEIGHTS}
    m_in = {n: env["m_" + n] for n in _WEIGHTS}
    v_in = {n: env["v_" + n] for n in _WEIGHTS}

    S, D = x.shape[1], x.shape[2]
    xs = x.reshape(S, D)
    target = loss_target.reshape(S, D)
    G, gws, gw = pool_w.shape[1:]
    Ds = w_kv_a.shape[0]
    R_kv, kvw = w_kv_b.shape
    hpd = kvw // (QK_NOPE_DIM + V_DIM)
    H = hpd * N_DEV
    R_q = w_q_a.shape[2]
    fs = w_gate.shape[2]
    bf = lambda t: t.astype(BF16)

    gains = jnp.concatenate([pool_norm, pool_scale], axis=0)
    g_pool_w, g_gains = _all_gather([bf(pool_w[0]), gains], "gather_first")
    wqb_pad = jnp.pad(w_q_b.reshape(R_q, hpd, QK_DIM), ((0, 0), (0, 0), (0, HEAD_PAD - QK_DIM))).reshape(R_q, hpd * HEAD_PAD)
    gather_mla = _Ride("gather", [bf(w_down[0]), bf(jnp.pad(w_kv_a, ((0, 0), (0, 128 - QK_ROPE_DIM)))), bf(w_kv_b),
                                  bf(w_q_a[0]), bf(wqb_pad), bf(w_o[0])])
    gather_ffn1 = _Ride("gather", [bf(w_gate[1]), bf(w_up[1]), bf(w_down[1])])

    wp = jnp.swapaxes(g_pool_w, 0, 1).reshape(G, gw, gw)
    gains_full = jnp.swapaxes(g_gains, 0, 1).reshape(2, D)
    g_pool_norm, g_pool_scale = gains_full[0:1], gains_full[1:2]

    g_kv_in = kv_in_norm.reshape(1, D)
    g_kv_lat = kv_latent_norm.reshape(1, R_kv)
    g_attn = attn_norm.reshape(1, D)
    g_q_lat = q_latent_norm.reshape(1, R_q)
    g_final = final_norm.reshape(1, D)

    half = QK_ROPE_DIM // 2
    inv_freq = ROPE_BASE ** (-jnp.arange(half, dtype=F32) / half)
    ang = positions.reshape(S).astype(F32)[:, None] * inv_freq
    cos, sin = jnp.cos(ang), jnp.sin(ang)
    zeros = jnp.zeros((S, 128 - QK_ROPE_DIM), F32)
    cos_t = jnp.concatenate([cos, cos, zeros], axis=1)
    sin_t = jnp.concatenate([-sin, sin, zeros], axis=1)

    diff, g_wg0 = _pool_pre(xs, g_pool_norm, ride=_Ride("gather2", [bf(w_gate[0])]))
    x1, g_wu0 = _pool_mix(diff, wp, g_pool_scale, xs, ride=_Ride("gather2", [bf(w_up[0])]))
    wg0, wu0 = g_wg0.reshape(N_DEV * D, fs), g_wu0.reshape(N_DEV * D, fs)
    x2, ffn0, wd0, (_, g_wkva, g_wkvb, g_wqa, g_wqb, g_wo) = _ffn_fwd(x1, ffn_norm[0:1], wg0, wu0, None, "0", gather_mla)
    wkva = g_wkva.reshape(D, R_kv + 128)
    wkvb = g_wkvb.reshape(N_DEV * R_kv, kvw)
    wqa = g_wqa.reshape(D, R_q)
    wqb = g_wqb.reshape(N_DEV * R_q, hpd * HEAD_PAD)
    wo = g_wo.reshape(H * V_DIM, D)

    hk, ha = _rmsnorm_pair(x2, g_kv_in, g_attn, "kv_attn_norm")
    kv = _mm_plain("kv_a", "nn", hk, wkva)
    ckv, kpe = _kv_mid(kv, g_kv_lat, cos_t, sin_t)
    kvu = _small_slots_out("kv_b", ckv, wkvb, BF16)

    ql = _mm_plain("q_a", "nn", ha, wqa)
    qn = _rmsnorm(ql, g_q_lat, "q_latent_norm")
    tab = pl.BlockSpec((_tile(S, ROW_TILE), 128), lambda i: (i, 0))
    qr = _small_slots_out("q_b", qn, wqb, BF16, epilogue=(_rope_heads, [cos_t, sin_t], [tab, tab]))
    o, lse, g_wg1, g_wu1, g_wd1 = _attn_fwd(qr, kvu, kpe, H, ride=gather_ffn1)
    wg1, wu1, wd1 = g_wg1.reshape(N_DEV * D, fs), g_wu1.reshape(N_DEV * D, fs), g_wd1.reshape(N_DEV * fs, D)
    x3 = _mm_plain("attn_out", "nn", o, wo, res=x2)
    x4, ffn1, _, _ = _ffn_fwd(x3, ffn_norm[1:2], wg1, wu1, wd1, "1")

    loss_part, dx4, dx4b, d_final = _loss_head(x4, g_final, target)

    (dx3, dx3b, d_ffn1), _, got_ffn1 = _ffn_bwd(x3, ffn_norm[1:2], wg1, wu1, wd1, ffn1, dx4, dx4b, "1", True)

    do = _mm_plain("attn_out_dx", "nt", dx3b, wo, out_dtype=BF16)
    dwo = _mm_plain("attn_out_dw", "tn", o, dx3b, out_dtype=BF16)
    dq, dkvu, dkpe = _attn_bwd(qr, kvu, kpe, do, o, lse.reshape(H, 1, S), cos_t, sin_t, H)

    dqn = _small_slots_in_nt("q_b_dx", dq, wqb, S)
    dwqb = _small_slots_dw("q_b_dw", qn, dq, BF16)
    dql, d_q_lat = _rmsnorm_bwd(ql, g_q_lat, dqn, None, "q_latent_norm_bwd", out_dtype=BF16)
    dha = _mm_plain("q_a_dx", "nt", dql, wqa)
    dwqa = _mm_plain("q_a_dw", "tn", ha, dql, out_dtype=BF16)

    dckv = _small_slots_in_nt("kv_b_dx", dkvu, wkvb, S)
    dwkvb = _small_slots_dw("kv_b_dw", ckv, dkvu, BF16)
    dkv, d_kv_lat = _kv_mid_bwd(kv, g_kv_lat, dckv, dkpe, cos_t, sin_t)
    dhk = _mm_plain("kv_a_dx", "nt", dkv, wkva)
    dwkva = _mm_plain("kv_a_dw", "tn", hk, dkv, out_dtype=BF16)
    dx2, dx2b, d_attn, d_kv_in = _rmsnorm_pair_bwd(x2, g_attn, dha, g_kv_in, dhk, dx3, "kv_attn_norm_bwd")

    scatter_mla = _Ride("scatter", [dwkva.reshape(N_DEV, Ds, R_kv + 128), dwkvb.reshape(N_DEV, R_kv, kvw),
                                    dwqa.reshape(N_DEV, Ds, R_q), dwqb.reshape(N_DEV, R_q, hpd * HEAD_PAD),
                                    dwo.reshape(N_DEV, H * V_DIM // N_DEV, D)])
    (dx1, d_ffn0), got_mla, got_ffn0 = _ffn_bwd(x1, ffn_norm[0:1], wg0, wu0, wd0, ffn0, dx2, dx2b, "0", False,
                                                  scatter=scatter_mla)

    ddiff, dwp, d_pool_scale = _pool_mix_bwd(diff, wp, g_pool_scale, dx1)
    grad_x, d_pool_norm = _pool_pre_bwd(xs, g_pool_norm, ddiff, dx1)

    d_gains = jnp.swapaxes(jnp.concatenate([d_pool_norm, d_pool_scale], axis=0).reshape(2, N_DEV, D // N_DEV), 0, 1)
    scatter_pool = _Ride("scatter", [jnp.swapaxes(dwp.reshape(G, N_DEV, gws, gw), 0, 1), d_gains])

    grads, delta_w, new_m, new_v = {}, {}, {}, {}
    ffn_slots = {n: [r0, r1] for n, r0, r1 in zip(("w_gate", "w_up", "w_down"), got_ffn0, got_ffn1)}
    got_pool = []
    for n, ride in zip(ffn_slots, (scatter_pool, None, None)):
        grads[n], delta_w[n], new_m[n], new_v[n], *got = _adamw_from_slots(weights[n], ffn_slots[n], m_in[n], v_in[n],
                                                                            "adamw_" + n, ride=ride)
        got_pool += got
    got_pool_w, got_gains = got_pool
    received = {n: [r] for n, r in zip(("w_kv_a", "w_kv_b", "w_q_a", "w_q_b", "w_o"), got_mla)}
    received.update(pool_w=[got_pool_w], gains=[got_gains])
    sums = {n: _sum_slots(r, "sum_" + n) for n, r in received.items()}
    grads.update({
        "pool_w": sums["pool_w"],
        "w_kv_a": sums["w_kv_a"][0, :, :R_kv + QK_ROPE_DIM],
        "w_kv_b": sums["w_kv_b"][0],
        "w_q_a": sums["w_q_a"],
        "w_q_b": sums["w_q_b"].reshape(R_q, hpd, HEAD_PAD)[:, :, :QK_DIM].reshape(1, R_q, hpd * QK_DIM),
        "w_o": sums["w_o"],
        "pool_norm": sums["gains"][0, 0:1],
        "pool_scale": sums["gains"][0, 1:2],
    })

    small = {"kv_in_norm": d_kv_in, "kv_latent_norm": d_kv_lat, "attn_norm": d_attn, "q_latent_norm": d_q_lat,
             "ffn_norm": jnp.concatenate([d_ffn0, d_ffn1], axis=0), "final_norm": d_final}
    small_packed = jnp.concatenate([small[n].reshape(-1) for n in _REPLICATED] + [loss_part.reshape(-1)])
    pad = (-small_packed.shape[0]) % (8 * 128)
    reduced = _all_reduce_small(jnp.pad(small_packed, (0, pad)).reshape(-1, 128)).reshape(-1)
    off = 0
    for n in _REPLICATED:
        grads[n] = reduced[off:off + weights[n].size].reshape(weights[n].shape)
        off += weights[n].size
    loss = reduced[off]

    for n in _WEIGHTS:
        if n not in ffn_slots:
            delta_w[n], new_m[n], new_v[n] = _adamw(weights[n], grads[n], m_in[n], v_in[n], "adamw_" + n)

    return (loss, grad_x.reshape(x.shape), *[grads[n] for n in _WEIGHTS], *[delta_w[n] for n in _WEIGHTS],
            *[new_m[n] for n in _WEIGHTS], *[new_v[n] for n in _WEIGHTS])
```
